```python
import math
import jax, jax.numpy as jnp
from jax import lax
import numpy as np

D_MODEL = 1024
BATCH = 8
SEQ = 4096
DEPTH = 4

PLE_DIM = 256
N_A_LAYERS = DEPTH // 2
N_B_LAYERS = DEPTH - N_A_LAYERS
GDN_HEADS = 8
GDN_HEAD_DIM = 128
GDN_WIDTH = GDN_HEADS * GDN_HEAD_DIM
CONV_WIDTH = 4
GDN_CHUNK = 64
SB_HEADS = 8
SB_HEAD_DIM = 128
SB_WIDTH = SB_HEADS * SB_HEAD_DIM
SB_BLOCK = 128
FFN_HIDDEN = -(-8 * D_MODEL // (3 * 256)) * 256
EPS = 1e-6

kernel_name = "yoco_gdn_stickbreaking_hybrid"


def rms_norm(x, g):
    xf = x.astype(jnp.float32)
    y = xf * lax.rsqrt(jnp.mean(xf * xf, axis=-1, keepdims=True) + EPS)
    return (y * g.astype(jnp.float32)).astype(x.dtype)


def l2_norm(x):
    return x * lax.rsqrt(jnp.sum(x * x, axis=-1, keepdims=True) + EPS)


def causal_conv(x, w):
    k_w, c = w.shape
    return lax.conv_general_dilated(x, w[:, None, :], window_strides=(1,), padding=[(k_w - 1, 0)],
                                    dimension_numbers=('NWC', 'WIO', 'NWC'), feature_group_count=c)


def to_chunks(t):
    b, s, h = t.shape[:3]
    t = t.reshape((b, s // GDN_CHUNK, GDN_CHUNK, h) + t.shape[3:])
    return jnp.swapaxes(t, 2, 3)


def gated_delta_rule(q, k, v, g, beta):
    b_, s_, h_, dk = q.shape
    dv = v.shape[-1]
    c = GDN_CHUNK
    q, k, v, g, beta = to_chunks(q), to_chunks(k), to_chunks(v), to_chunks(g), to_chunks(beta)
    G = jnp.cumsum(g, axis=-1)
    causal = jnp.tril(jnp.ones((c, c), dtype=bool))
    strict = jnp.tril(jnp.ones((c, c), dtype=bool), -1)
    decay_mat = jnp.exp(jnp.where(causal, G[..., :, None] - G[..., None, :], -jnp.inf))
    kk = jnp.einsum('bnhrd,bnhsd->bnhrs', k, k)
    a_low = jnp.where(strict, beta[..., None] * kk * decay_mat, 0.0)
    eye = jnp.eye(c, dtype=q.dtype)
    rhs = jnp.concatenate([v * beta[..., None], k * (beta * jnp.exp(G))[..., None]], axis=-1)
    sol = lax.linalg.triangular_solve(a_low + eye, rhs, left_side=True, lower=True, unit_diagonal=True)
    u, w = sol[..., :dv], sol[..., dv:]
    qk = jnp.einsum('bnhrd,bnhsd->bnhrs', q, k) * decay_mat
    q_dec = q * jnp.exp(G)[..., None]
    k_dec = k * jnp.exp(G[..., -1:] - G)[..., None]
    g_last = jnp.exp(G[..., -1])

    def step(state, xs):
        q_c, k_c, qk_c, u_c, w_c, gl_c = xs
        v_new = u_c - jnp.einsum('bhcd,bhde->bhce', w_c, state)
        o = jnp.einsum('bhcd,bhde->bhce', q_c, state) + jnp.einsum('bhrs,bhse->bhre', qk_c, v_new)
        state = state * gl_c[..., None, None] + jnp.einsum('bhcd,bhce->bhde', k_c, v_new)
        return state, o

    xs = tuple(jnp.moveaxis(t, 1, 0) for t in (q_dec, k_dec, qk, u, w, g_last))
    s0 = jnp.zeros((b_, h_, dk, dv), q.dtype)
    _, o = lax.scan(step, s0, xs)
    return o.transpose(1, 0, 3, 2, 4).reshape(b_, s_, h_, dv)


def gdn_mixer(hn, w_in, w_conv, a_log, dt_bias, norm_g, w_out):
    b_, s_, _ = hn.shape
    proj = hn @ w_in
    qkv = jax.nn.silu(causal_conv(proj[..., :3 * GDN_WIDTH], w_conv))
    gate = proj[..., 3 * GDN_WIDTH:4 * GDN_WIDTH].reshape(b_, s_, GDN_HEADS, GDN_HEAD_DIM)
    a_in = proj[..., 4 * GDN_WIDTH:4 * GDN_WIDTH + GDN_HEADS].astype(jnp.float32)
    b_in = proj[..., 4 * GDN_WIDTH + GDN_HEADS:].astype(jnp.float32)
    qkv = qkv.astype(jnp.float32).reshape(b_, s_, 3, GDN_HEADS, GDN_HEAD_DIM)
    q = l2_norm(qkv[:, :, 0]) * (GDN_HEAD_DIM ** -0.5)
    k = l2_norm(qkv[:, :, 1])
    v = qkv[:, :, 2]
    beta = jax.nn.sigmoid(b_in)
    g = -jnp.exp(a_log.astype(jnp.float32)) * jax.nn.softplus(a_in + dt_bias.astype(jnp.float32))
    o = gated_delta_rule(q, k, v, g, beta).astype(hn.dtype)
    o = rms_norm(o, norm_g) * jax.nn.silu(gate)
    return o.reshape(b_, s_, GDN_WIDTH) @ w_out


def shared_kv(h, kv_norm, w_kv, k_norm):
    b_, s_, _ = h.shape
    kv = (rms_norm(h, kv_norm) @ w_kv).reshape(b_, s_, 2, SB_HEADS, SB_HEAD_DIM)
    k = rms_norm(kv[:, :, 0], k_norm).transpose(0, 2, 1, 3)
    v = kv[:, :, 1].transpose(0, 2, 1, 3)
    return k, v


def stick_breaking(q, k, v):
    s_ = q.shape[2]
    outs = []
    for blk in range(s_ // SB_BLOCK):
        t0, t1 = blk * SB_BLOCK, (blk + 1) * SB_BLOCK
        z = jnp.einsum('bhtd,bhsd->bhts', q[:, :, t0:t1], k[:, :, :t1]).astype(jnp.float32)
        t_idx = t0 + jnp.arange(SB_BLOCK)[:, None]
        s_idx = jnp.arange(t1)[None, :]
        mask = s_idx < t_idx
        log_not = jnp.where(mask, jax.nn.log_sigmoid(-z), 0.0)
        suffix = lax.cumsum(log_not, axis=3, reverse=True) - log_not
        log_w = jnp.where(mask, jax.nn.log_sigmoid(z) + suffix, -jnp.inf)
        wgt = jnp.exp(log_w).astype(v.dtype)
        outs.append(jnp.einsum('bhts,bhsd->bhtd', wgt, v[:, :, :t1]))
    return jnp.concatenate(outs, axis=2)


def sb_mixer(hn, k_sh, v_sh, w_q, q_norm, w_out):
    b_, s_, _ = hn.shape
    q = (hn @ w_q).reshape(b_, s_, SB_HEADS, SB_HEAD_DIM)
    q = (rms_norm(q, q_norm) * (SB_HEAD_DIM ** -0.5)).transpose(0, 2, 1, 3)
    o = stick_breaking(q, k_sh, v_sh)
    return o.transpose(0, 2, 1, 3).reshape(b_, s_, SB_WIDTH) @ w_out


def swiglu(hn, w_in, w_out):
    gu = hn @ w_in
    return (jax.nn.silu(gu[..., :FFN_HIDDEN]) * gu[..., FFN_HIDDEN:]) @ w_out


def _fwd_setup_inputs(seed: int = 0) -> dict:
    key = jax.random.key(seed)
    ks = jax.random.split(key, 24)
    f32 = jnp.float32

    def nrm(k, shape, fan_in):
        return jax.random.normal(k, shape, f32) * (fan_in ** -0.5)

    def gain(k, shape):
        return 1.0 + 0.02 * jax.random.normal(k, shape, f32)

    dt = jnp.exp(jax.random.uniform(ks[8], (N_A_LAYERS, GDN_HEADS), f32, math.log(1e-3), math.log(1e-1)))
    return {
        "x": jax.random.normal(ks[0], (BATCH, SEQ, D_MODEL), f32),
        "p": jax.random.normal(ks[1], (DEPTH, BATCH, SEQ, PLE_DIM), f32),
        "ln_mix": gain(ks[2], (DEPTH, D_MODEL)),
        "ln_ffn": gain(ks[3], (DEPTH, D_MODEL)),
        "ln_ple": gain(ks[4], (DEPTH, D_MODEL)),
        "gdn_w_in": nrm(ks[5], (N_A_LAYERS, D_MODEL, 4 * GDN_WIDTH + 2 * GDN_HEADS), D_MODEL),
        "gdn_conv": nrm(ks[6], (N_A_LAYERS, CONV_WIDTH, 3 * GDN_WIDTH), CONV_WIDTH),
        "gdn_a_log": jnp.log(jax.random.uniform(ks[7], (N_A_LAYERS, GDN_HEADS), f32, 1.0, 16.0)),
        "gdn_dt_bias": dt + jnp.log(-jnp.expm1(-dt)),
        "gdn_norm": gain(ks[9], (N_A_LAYERS, GDN_HEAD_DIM)),
        "gdn_w_out": nrm(ks[10], (N_A_LAYERS, GDN_WIDTH, D_MODEL), GDN_WIDTH),
        "kv_norm": gain(ks[11], (D_MODEL,)),
        "w_kv": nrm(ks[12], (D_MODEL, 2 * SB_WIDTH), D_MODEL),
        "k_norm": gain(ks[13], (SB_HEAD_DIM,)),
        "sb_w_q": nrm(ks[14], (N_B_LAYERS, D_MODEL, SB_WIDTH), D_MODEL),
        "sb_q_norm": gain(ks[15], (N_B_LAYERS, SB_HEAD_DIM)),
        "sb_w_out": nrm(ks[16], (N_B_LAYERS, SB_WIDTH, D_MODEL), SB_WIDTH),
        "ffn_w_in": nrm(ks[17], (DEPTH, D_MODEL, 2 * FFN_HIDDEN), D_MODEL),
        "ffn_w_out": nrm(ks[18], (DEPTH, FFN_HIDDEN, D_MODEL), FFN_HIDDEN),
        "ple_w_proj": nrm(ks[19], (DEPTH, PLE_DIM, D_MODEL), PLE_DIM),
        "ple_w_gate": nrm(ks[20], (DEPTH, D_MODEL, D_MODEL), D_MODEL),
    }


def _fwd_reference(x, p, ln_mix, ln_ffn, ln_ple, gdn_w_in, gdn_conv, gdn_a_log, gdn_dt_bias, gdn_norm,
              gdn_w_out, kv_norm, w_kv, k_norm, sb_w_q, sb_q_norm, sb_w_out, ffn_w_in, ffn_w_out,
              ple_w_proj, ple_w_gate):
    h = x
    k_sh, v_sh = None, None
    for i in range(DEPTH):
        hn = rms_norm(h, ln_mix[i])
        if i < N_A_LAYERS:
            h = h + gdn_mixer(hn, gdn_w_in[i], gdn_conv[i], gdn_a_log[i], gdn_dt_bias[i],
                              gdn_norm[i], gdn_w_out[i])
        else:
            j = i - N_A_LAYERS
            h = h + sb_mixer(hn, k_sh, v_sh, sb_w_q[j], sb_q_norm[j], sb_w_out[j])
        h = h + swiglu(rms_norm(h, ln_ffn[i]), ffn_w_in[i], ffn_w_out[i])
        h = h + (p[i] @ ple_w_proj[i]) * jax.nn.sigmoid(rms_norm(h, ln_ple[i]) @ ple_w_gate[i])
        if i == N_A_LAYERS - 1:
            k_sh, v_sh = shared_kv(h, kv_norm, w_kv, k_norm)
    return h


import jax as _jax
import jax.numpy as _jnp

TWIN_FORMAT = 'train_step'
FWD_PARAMS = ['x', 'p', 'ln_mix', 'ln_ffn', 'ln_ple', 'gdn_w_in', 'gdn_conv', 'gdn_a_log', 'gdn_dt_bias', 'gdn_norm', 'gdn_w_out', 'kv_norm', 'w_kv', 'k_norm', 'sb_w_q', 'sb_q_norm', 'sb_w_out', 'ffn_w_in', 'ffn_w_out', 'ple_w_proj', 'ple_w_gate']
TWIN_WEIGHTS = ['ln_mix', 'ln_ffn', 'ln_ple', 'gdn_w_in', 'gdn_conv', 'gdn_a_log', 'gdn_dt_bias', 'gdn_norm', 'gdn_w_out', 'kv_norm', 'w_kv', 'k_norm', 'sb_w_q', 'sb_q_norm', 'sb_w_out', 'ffn_w_in', 'ffn_w_out', 'ple_w_proj', 'ple_w_gate']
TWIN_DIFF_INPUT = 'x'
TWIN_INPUTS = ['x', 'p', 'ln_mix', 'ln_ffn', 'ln_ple', 'gdn_w_in', 'gdn_conv', 'gdn_a_log', 'gdn_dt_bias', 'gdn_norm', 'gdn_w_out', 'kv_norm', 'w_kv', 'k_norm', 'sb_w_q', 'sb_q_norm', 'sb_w_out', 'ffn_w_in', 'ffn_w_out', 'ple_w_proj', 'ple_w_gate', 'loss_target', 'm_ln_mix', 'm_ln_ffn', 'm_ln_ple', 'm_gdn_w_in', 'm_gdn_conv', 'm_gdn_a_log', 'm_gdn_dt_bias', 'm_gdn_norm', 'm_gdn_w_out', 'm_kv_norm', 'm_w_kv', 'm_k_norm', 'm_sb_w_q', 'm_sb_q_norm', 'm_sb_w_out', 'm_ffn_w_in', 'm_ffn_w_out', 'm_ple_w_proj', 'm_ple_w_gate', 'v_ln_mix', 'v_ln_ffn', 'v_ln_ple', 'v_gdn_w_in', 'v_gdn_conv', 'v_gdn_a_log', 'v_gdn_dt_bias', 'v_gdn_norm', 'v_gdn_w_out', 'v_kv_norm', 'v_w_kv', 'v_k_norm', 'v_sb_w_q', 'v_sb_q_norm', 'v_sb_w_out', 'v_ffn_w_in', 'v_ffn_w_out', 'v_ple_w_proj', 'v_ple_w_gate']
TWIN_OUTPUTS = ['loss', 'grad_x', 'grad_ln_mix', 'grad_ln_ffn', 'grad_ln_ple', 'grad_gdn_w_in', 'grad_gdn_conv', 'grad_gdn_a_log', 'grad_gdn_dt_bias', 'grad_gdn_norm', 'grad_gdn_w_out', 'grad_kv_norm', 'grad_w_kv', 'grad_k_norm', 'grad_sb_w_q', 'grad_sb_q_norm', 'grad_sb_w_out', 'grad_ffn_w_in', 'grad_ffn_w_out', 'grad_ple_w_proj', 'grad_ple_w_gate', 'delta_ln_mix', 'delta_ln_ffn', 'delta_ln_ple', 'delta_gdn_w_in', 'delta_gdn_conv', 'delta_gdn_a_log', 'delta_gdn_dt_bias', 'delta_gdn_norm', 'delta_gdn_w_out', 'delta_kv_norm', 'delta_w_kv', 'delta_k_norm', 'delta_sb_w_q', 'delta_sb_q_norm', 'delta_sb_w_out', 'delta_ffn_w_in', 'delta_ffn_w_out', 'delta_ple_w_proj', 'delta_ple_w_gate', 'new_m_ln_mix', 'new_m_ln_ffn', 'new_m_ln_ple', 'new_m_gdn_w_in', 'new_m_gdn_conv', 'new_m_gdn_a_log', 'new_m_gdn_dt_bias', 'new_m_gdn_norm', 'new_m_gdn_w_out', 'new_m_kv_norm', 'new_m_w_kv', 'new_m_k_norm', 'new_m_sb_w_q', 'new_m_sb_q_norm', 'new_m_sb_w_out', 'new_m_ffn_w_in', 'new_m_ffn_w_out', 'new_m_ple_w_proj', 'new_m_ple_w_gate', 'new_v_ln_mix', 'new_v_ln_ffn', 'new_v_ln_ple', 'new_v_gdn_w_in', 'new_v_gdn_conv', 'new_v_gdn_a_log', 'new_v_gdn_dt_bias', 'new_v_gdn_norm', 'new_v_gdn_w_out', 'new_v_kv_norm', 'new_v_w_kv', 'new_v_k_norm', 'new_v_sb_w_q', 'new_v_sb_q_norm', 'new_v_sb_w_out', 'new_v_ffn_w_in', 'new_v_ffn_w_out', 'new_v_ple_w_proj', 'new_v_ple_w_gate']
TWIN_LEAF_KINDS = {'loss': 'loss', 'grad_x': 'grad_x', 'grad_ln_mix': 'grad_w', 'grad_ln_ffn': 'grad_w', 'grad_ln_ple': 'grad_w', 'grad_gdn_w_in': 'grad_w', 'grad_gdn_conv': 'grad_w', 'grad_gdn_a_log': 'grad_w', 'grad_gdn_dt_bias': 'grad_w', 'grad_gdn_norm': 'grad_w', 'grad_gdn_w_out': 'grad_w', 'grad_kv_norm': 'grad_w', 'grad_w_kv': 'grad_w', 'grad_k_norm': 'grad_w', 'grad_sb_w_q': 'grad_w', 'grad_sb_q_norm': 'grad_w', 'grad_sb_w_out': 'grad_w', 'grad_ffn_w_in': 'grad_w', 'grad_ffn_w_out': 'grad_w', 'grad_ple_w_proj': 'grad_w', 'grad_ple_w_gate': 'grad_w', 'delta_ln_mix': 'delta_w', 'delta_ln_ffn': 'delta_w', 'delta_ln_ple': 'delta_w', 'delta_gdn_w_in': 'delta_w', 'delta_gdn_conv': 'delta_w', 'delta_gdn_a_log': 'delta_w', 'delta_gdn_dt_bias': 'delta_w', 'delta_gdn_norm': 'delta_w', 'delta_gdn_w_out': 'delta_w', 'delta_kv_norm': 'delta_w', 'delta_w_kv': 'delta_w', 'delta_k_norm': 'delta_w', 'delta_sb_w_q': 'delta_w', 'delta_sb_q_norm': 'delta_w', 'delta_sb_w_out': 'delta_w', 'delta_ffn_w_in': 'delta_w', 'delta_ffn_w_out': 'delta_w', 'delta_ple_w_proj': 'delta_w', 'delta_ple_w_gate': 'delta_w', 'new_m_ln_mix': 'new_m', 'new_m_ln_ffn': 'new_m', 'new_m_ln_ple': 'new_m', 'new_m_gdn_w_in': 'new_m', 'new_m_gdn_conv': 'new_m', 'new_m_gdn_a_log': 'new_m', 'new_m_gdn_dt_bias': 'new_m', 'new_m_gdn_norm': 'new_m', 'new_m_gdn_w_out': 'new_m', 'new_m_kv_norm': 'new_m', 'new_m_w_kv': 'new_m', 'new_m_k_norm': 'new_m', 'new_m_sb_w_q': 'new_m', 'new_m_sb_q_norm': 'new_m', 'new_m_sb_w_out': 'new_m', 'new_m_ffn_w_in': 'new_m', 'new_m_ffn_w_out': 'new_m', 'new_m_ple_w_proj': 'new_m', 'new_m_ple_w_gate': 'new_m', 'new_v_ln_mix': 'new_v', 'new_v_ln_ffn': 'new_v', 'new_v_ln_ple': 'new_v', 'new_v_gdn_w_in': 'new_v', 'new_v_gdn_conv': 'new_v', 'new_v_gdn_a_log': 'new_v', 'new_v_gdn_dt_bias': 'new_v', 'new_v_gdn_norm': 'new_v', 'new_v_gdn_w_out': 'new_v', 'new_v_kv_norm': 'new_v', 'new_v_w_kv': 'new_v', 'new_v_k_norm': 'new_v', 'new_v_sb_w_q': 'new_v', 'new_v_sb_q_norm': 'new_v', 'new_v_sb_w_out': 'new_v', 'new_v_ffn_w_in': 'new_v', 'new_v_ffn_w_out': 'new_v', 'new_v_ple_w_proj': 'new_v', 'new_v_ple_w_gate': 'new_v'}


def _forward(args):
    return _fwd_reference(*[args[k] for k in FWD_PARAMS])


def _output_shape():
    def fwd():
        inp = _fwd_setup_inputs(0)
        return _fwd_reference(*[inp[k] for k in FWD_PARAMS])
    out = _jax.eval_shape(fwd)
    return out.shape, out.dtype

N_MICROBATCH = 1
ADAM_LR = 0.001
ADAM_B1 = 0.9
ADAM_B2 = 0.999
ADAM_EPS = 1e-08
ADAM_WD = 0.01
ADAM_STEP = 10
PER_EXAMPLE_BATCH_AXIS = {'x': 0, 'p': 1, 'loss_target': 0}
SHARED_INPUTS = []
_WEIGHT_DTYPES = {'ln_mix': _jnp.float32, 'ln_ffn': _jnp.float32, 'ln_ple': _jnp.float32, 'gdn_w_in': _jnp.float32, 'gdn_conv': _jnp.float32, 'gdn_a_log': _jnp.float32, 'gdn_dt_bias': _jnp.float32, 'gdn_norm': _jnp.float32, 'gdn_w_out': _jnp.float32, 'kv_norm': _jnp.float32, 'w_kv': _jnp.float32, 'k_norm': _jnp.float32, 'sb_w_q': _jnp.float32, 'sb_q_norm': _jnp.float32, 'sb_w_out': _jnp.float32, 'ffn_w_in': _jnp.float32, 'ffn_w_out': _jnp.float32, 'ple_w_proj': _jnp.float32, 'ple_w_gate': _jnp.float32}
MOMENT_SCALE = {'ln_mix': 9.889609e+00, 'ln_ffn': 2.459412e+01, 'ln_ple': 1.020503e+00, 'gdn_w_in': 7.538218e-01, 'gdn_conv': 1.084741e+00, 'gdn_a_log': 2.319308e+01, 'gdn_dt_bias': 2.188602e+01, 'gdn_norm': 9.790059e+01, 'gdn_w_out': 2.376885e+00, 'kv_norm': 2.741779e+01, 'w_kv': 1.385754e+00, 'k_norm': 3.109009e+01, 'sb_w_q': 4.050444e-01, 'sb_q_norm': 1.553541e+01, 'sb_w_out': 1.210223e+00, 'ffn_w_in': 4.205023e-01, 'ffn_w_out': 6.434750e-01, 'ple_w_proj': 5.918271e-01, 'ple_w_gate': 3.394811e-01}


def _to_microbatches(a, axis):
    t = _jnp.moveaxis(a, axis, 0)
    t = t.reshape((N_MICROBATCH, t.shape[0] // N_MICROBATCH) + t.shape[1:])
    return _jnp.moveaxis(t, 1, axis + 1)


def setup_inputs(seed: int = 0) -> dict:
    inp = _fwd_setup_inputs(seed)
    key = _jax.random.fold_in(_jax.random.key(seed), 7919)
    shape, _ = _output_shape()
    out = dict(inp)
    out["loss_target"] = _jax.random.normal(_jax.random.fold_in(key, 0), shape, _jnp.float32)
    for i, name in enumerate(TWIN_WEIGHTS):
        w = inp[name].astype(_jnp.float32)
        if MOMENT_SCALE is None:
            s = _jnp.sqrt(_jnp.mean(_jnp.square(w)) + 1e-30)
        else:
            s = MOMENT_SCALE[name]
        km, kv = _jax.random.split(_jax.random.fold_in(key, i + 1))
        out[name] = w
        out["m_" + name] = s * _jax.random.normal(km, w.shape, _jnp.float32)
        out["v_" + name] = (s * s) * _jax.random.uniform(kv, w.shape, _jnp.float32, 0.5, 1.5)
    if N_MICROBATCH > 1:
        for name, axis in PER_EXAMPLE_BATCH_AXIS.items():
            out[name] = _to_microbatches(out[name], axis)
    return {'x': out['x'], 'p': out['p'], 'ln_mix': out['ln_mix'], 'ln_ffn': out['ln_ffn'], 'ln_ple': out['ln_ple'], 'gdn_w_in': out['gdn_w_in'], 'gdn_conv': out['gdn_conv'], 'gdn_a_log': out['gdn_a_log'], 'gdn_dt_bias': out['gdn_dt_bias'], 'gdn_norm': out['gdn_norm'], 'gdn_w_out': out['gdn_w_out'], 'kv_norm': out['kv_norm'], 'w_kv': out['w_kv'], 'k_norm': out['k_norm'], 'sb_w_q': out['sb_w_q'], 'sb_q_norm': out['sb_q_norm'], 'sb_w_out': out['sb_w_out'], 'ffn_w_in': out['ffn_w_in'], 'ffn_w_out': out['ffn_w_out'], 'ple_w_proj': out['ple_w_proj'], 'ple_w_gate': out['ple_w_gate'], 'loss_target': out['loss_target'], 'm_ln_mix': out['m_ln_mix'], 'm_ln_ffn': out['m_ln_ffn'], 'm_ln_ple': out['m_ln_ple'], 'm_gdn_w_in': out['m_gdn_w_in'], 'm_gdn_conv': out['m_gdn_conv'], 'm_gdn_a_log': out['m_gdn_a_log'], 'm_gdn_dt_bias': out['m_gdn_dt_bias'], 'm_gdn_norm': out['m_gdn_norm'], 'm_gdn_w_out': out['m_gdn_w_out'], 'm_kv_norm': out['m_kv_norm'], 'm_w_kv': out['m_w_kv'], 'm_k_norm': out['m_k_norm'], 'm_sb_w_q': out['m_sb_w_q'], 'm_sb_q_norm': out['m_sb_q_norm'], 'm_sb_w_out': out['m_sb_w_out'], 'm_ffn_w_in': out['m_ffn_w_in'], 'm_ffn_w_out': out['m_ffn_w_out'], 'm_ple_w_proj': out['m_ple_w_proj'], 'm_ple_w_gate': out['m_ple_w_gate'], 'v_ln_mix': out['v_ln_mix'], 'v_ln_ffn': out['v_ln_ffn'], 'v_ln_ple': out['v_ln_ple'], 'v_gdn_w_in': out['v_gdn_w_in'], 'v_gdn_conv': out['v_gdn_conv'], 'v_gdn_a_log': out['v_gdn_a_log'], 'v_gdn_dt_bias': out['v_gdn_dt_bias'], 'v_gdn_norm': out['v_gdn_norm'], 'v_gdn_w_out': out['v_gdn_w_out'], 'v_kv_norm': out['v_kv_norm'], 'v_w_kv': out['v_w_kv'], 'v_k_norm': out['v_k_norm'], 'v_sb_w_q': out['v_sb_w_q'], 'v_sb_q_norm': out['v_sb_q_norm'], 'v_sb_w_out': out['v_sb_w_out'], 'v_ffn_w_in': out['v_ffn_w_in'], 'v_ffn_w_out': out['v_ffn_w_out'], 'v_ple_w_proj': out['v_ple_w_proj'], 'v_ple_w_gate': out['v_ple_w_gate']}


def _loss(weights, diff, rest, loss_target):
    with _jax.named_scope("forward"):
        args = {**rest, TWIN_DIFF_INPUT: diff, **{k: w.astype(_WEIGHT_DTYPES[k]) for k, w in weights.items()}}
        y = _forward(args)
    with _jax.named_scope("loss_head"):
        err = _jnp.square(y.astype(_jnp.float32) - loss_target)
        return 0.5 * _jnp.sum(_jnp.mean(err, axis=-1)) if err.ndim else 0.5 * err


def _adamw(w, g, m, v):
    m = ADAM_B1 * m + (1.0 - ADAM_B1) * g
    v = ADAM_B2 * v + (1.0 - ADAM_B2) * _jnp.square(g)
    m_hat = m / (1.0 - ADAM_B1 ** ADAM_STEP)
    v_hat = v / (1.0 - ADAM_B2 ** ADAM_STEP)
    delta = -ADAM_LR * (m_hat / (_jnp.sqrt(v_hat) + ADAM_EPS) + ADAM_WD * w)
    return delta, m, v


def reference(x, p, ln_mix, ln_ffn, ln_ple, gdn_w_in, gdn_conv, gdn_a_log, gdn_dt_bias, gdn_norm, gdn_w_out, kv_norm, w_kv, k_norm, sb_w_q, sb_q_norm, sb_w_out, ffn_w_in, ffn_w_out, ple_w_proj, ple_w_gate, loss_target, m_ln_mix, m_ln_ffn, m_ln_ple, m_gdn_w_in, m_gdn_conv, m_gdn_a_log, m_gdn_dt_bias, m_gdn_norm, m_gdn_w_out, m_kv_norm, m_w_kv, m_k_norm, m_sb_w_q, m_sb_q_norm, m_sb_w_out, m_ffn_w_in, m_ffn_w_out, m_ple_w_proj, m_ple_w_gate, v_ln_mix, v_ln_ffn, v_ln_ple, v_gdn_w_in, v_gdn_conv, v_gdn_a_log, v_gdn_dt_bias, v_gdn_norm, v_gdn_w_out, v_kv_norm, v_w_kv, v_k_norm, v_sb_w_q, v_sb_q_norm, v_sb_w_out, v_ffn_w_in, v_ffn_w_out, v_ple_w_proj, v_ple_w_gate):
    given = dict(x=x, p=p, ln_mix=ln_mix, ln_ffn=ln_ffn, ln_ple=ln_ple, gdn_w_in=gdn_w_in, gdn_conv=gdn_conv, gdn_a_log=gdn_a_log, gdn_dt_bias=gdn_dt_bias, gdn_norm=gdn_norm, gdn_w_out=gdn_w_out, kv_norm=kv_norm, w_kv=w_kv, k_norm=k_norm, sb_w_q=sb_w_q, sb_q_norm=sb_q_norm, sb_w_out=sb_w_out, ffn_w_in=ffn_w_in, ffn_w_out=ffn_w_out, ple_w_proj=ple_w_proj, ple_w_gate=ple_w_gate, loss_target=loss_target, m_ln_mix=m_ln_mix, m_ln_ffn=m_ln_ffn, m_ln_ple=m_ln_ple, m_gdn_w_in=m_gdn_w_in, m_gdn_conv=m_gdn_conv, m_gdn_a_log=m_gdn_a_log, m_gdn_dt_bias=m_gdn_dt_bias, m_gdn_norm=m_gdn_norm, m_gdn_w_out=m_gdn_w_out, m_kv_norm=m_kv_norm, m_w_kv=m_w_kv, m_k_norm=m_k_norm, m_sb_w_q=m_sb_w_q, m_sb_q_norm=m_sb_q_norm, m_sb_w_out=m_sb_w_out, m_ffn_w_in=m_ffn_w_in, m_ffn_w_out=m_ffn_w_out, m_ple_w_proj=m_ple_w_proj, m_ple_w_gate=m_ple_w_gate, v_ln_mix=v_ln_mix, v_ln_ffn=v_ln_ffn, v_ln_ple=v_ln_ple, v_gdn_w_in=v_gdn_w_in, v_gdn_conv=v_gdn_conv, v_gdn_a_log=v_gdn_a_log, v_gdn_dt_bias=v_gdn_dt_bias, v_gdn_norm=v_gdn_norm, v_gdn_w_out=v_gdn_w_out, v_kv_norm=v_kv_norm, v_w_kv=v_w_kv, v_k_norm=v_k_norm, v_sb_w_q=v_sb_w_q, v_sb_q_norm=v_sb_q_norm, v_sb_w_out=v_sb_w_out, v_ffn_w_in=v_ffn_w_in, v_ffn_w_out=v_ffn_w_out, v_ple_w_proj=v_ple_w_proj, v_ple_w_gate=v_ple_w_gate)
    weights = {n: given[n] for n in TWIN_WEIGHTS}
    shared = {n: given[n] for n in SHARED_INPUTS}
    per_example = {n: given[n] for n in ['x', 'p']}
    grad_fn = _jax.value_and_grad(_loss, argnums=(0, 1))

    def one_microbatch(ex, loss_target):
        ex = dict(ex)
        diff = ex.pop(TWIN_DIFF_INPUT)
        return grad_fn(weights, diff, {**shared, **ex}, loss_target)

    if N_MICROBATCH == 1:
        loss, (grad_w, grad_x) = one_microbatch(per_example, given["loss_target"])
    else:
        def body(carry, xs):
            loss_sum, grad_sum = carry
            l_k, (gw_k, gx_k) = one_microbatch(xs[0], xs[1])
            with _jax.named_scope("update"):
                return (loss_sum + l_k, _jax.tree.map(_jnp.add, grad_sum, gw_k)), gx_k

        init = (_jnp.zeros((), _jnp.float32), _jax.tree.map(_jnp.zeros_like, weights))
        (loss, grad_w), grad_x = _jax.lax.scan(body, init, (per_example, given["loss_target"]))
    with _jax.named_scope("update"):
        delta_w, new_m, new_v = {}, {}, {}
        for n in TWIN_WEIGHTS:
            delta_w[n], new_m[n], new_v[n] = _adamw(weights[n], grad_w[n], given["m_" + n], given["v_" + n])
    return (loss, grad_x, *[grad_w[n] for n in TWIN_WEIGHTS], *[delta_w[n] for n in TWIN_WEIGHTS],
            *[new_m[n] for n in TWIN_WEIGHTS], *[new_v[n] for n in TWIN_WEIGHTS])
```

```python
import functools
import math

import jax
import jax.numpy as jnp
from jax import lax
from jax.experimental import pallas as pl
from jax.experimental.pallas import tpu as pltpu

F32 = jnp.float32
BF16 = jnp.bfloat16
NDEV = 8
HEAD = 128
CHUNK = 64
SBLK = 128
EPS = 1e-6
LR, B1, B2, ADAM_EPS, WD, STEP = 0.001, 0.9, 0.999, 1e-08, 0.01, 10
NEG = -1e30

NN = (((1,), (0,)), ((), ()))
NT = (((1,), (1,)), ((), ()))
TN = (((0,), (0,)), ((), ()))
MESH = pl.DeviceIdType.MESH


def _dot(a, b, dims=NN):
    return lax.dot_general(a.astype(BF16), b.astype(BF16), dims, preferred_element_type=F32)


def _dot_hilo(a, b01, dims=NN):
    hi = a.astype(BF16)
    lo = (a - hi.astype(F32)).astype(BF16)
    return (lax.dot_general(hi, b01, dims, preferred_element_type=F32)
            + lax.dot_general(lo, b01, dims, preferred_element_type=F32))


def _pick(dim, cands):
    for c in cands:
        if dim % c == 0:
            return c
    return dim


def _params(sem, vmem_mb=48):
    return pltpu.CompilerParams(dimension_semantics=sem, vmem_limit_bytes=vmem_mb * 1024 * 1024)


def _silu(x):
    return x * jax.nn.sigmoid(x)


def _dsilu(x):
    s = jax.nn.sigmoid(x)
    return s * (1.0 + x * (1.0 - s))


def _mm(a, b, mode, name, out_dtype=F32, res=None):
    if mode == "nn":
        (m, k), n = a.shape, b.shape[1]
    elif mode == "nt":
        (m, k), n = a.shape, b.shape[0]
    else:
        (k, m), n = a.shape, b.shape[1]
    tm = _pick(m, (512, 256, 128))
    tn = _pick(n, (512, 256, 128))
    tk = _pick(k, (1024, 512, 256, 128))
    nk = k // tk
    dims = {"nn": NN, "nt": NT, "tn": TN}[mode]
    if mode == "tn":
        a_spec = pl.BlockSpec((tk, tm), lambda i, j, kk: (kk, i))
    else:
        a_spec = pl.BlockSpec((tm, tk), lambda i, j, kk: (i, kk))
    if mode == "nt":
        b_spec = pl.BlockSpec((tn, tk), lambda i, j, kk: (j, kk))
    else:
        b_spec = pl.BlockSpec((tk, tn), lambda i, j, kk: (kk, j))
    mn_spec = pl.BlockSpec((tm, tn), lambda i, j, kk: (i, j))
    has_res = res is not None

    def body(*refs):
        if has_res:
            a_ref, b_ref, r_ref, o_ref, acc = refs
        else:
            a_ref, b_ref, o_ref, acc = refs
        kk = pl.program_id(2)

        @pl.when(kk == 0)
        def _():
            acc[...] = jnp.zeros_like(acc)

        acc[...] += _dot(a_ref[...], b_ref[...], dims)

        @pl.when(kk == nk - 1)
        def _():
            r = acc[...]
            if has_res:
                r = r + r_ref[...].astype(F32)
            o_ref[...] = r.astype(out_dtype)

    ins = [a, b] + ([res] if has_res else [])
    in_specs = [a_spec, b_spec] + ([mn_spec] if has_res else [])
    return pl.pallas_call(
        body, name=name, grid=(m // tm, n // tn, nk), in_specs=in_specs, out_specs=mn_spec,
        out_shape=jax.ShapeDtypeStruct((m, n), out_dtype), scratch_shapes=[pltpu.VMEM((tm, tn), F32)],
        compiler_params=_params(("parallel", "parallel", "arbitrary")))(*ins)


def _rms_fwd(h, g, name):
    s, d = h.shape
    tm = _pick(s, (512, 256, 128))

    def body(h_ref, g_ref, o_ref):
        x = h_ref[...]
        r = lax.rsqrt(jnp.mean(x * x, axis=-1, keepdims=True) + EPS)
        o_ref[...] = (x * r * g_ref[...]).astype(BF16)

    return pl.pallas_call(
        body, name=name, grid=(s // tm,),
        in_specs=[pl.BlockSpec((tm, d), lambda i: (i, 0)), pl.BlockSpec((1, d), lambda i: (0, 0))],
        out_specs=pl.BlockSpec((tm, d), lambda i: (i, 0)),
        out_shape=jax.ShapeDtypeStruct((s, d), BF16), compiler_params=_params(("parallel",)))(h, g.reshape(1, d))


def _rms_bwd(dy, x, g, dres, name):
    s, d = x.shape
    tm = _pick(s, (512, 256, 128))

    def body(dy_ref, x_ref, g_ref, dr_ref, dx_ref, dg_ref):
        @pl.when(pl.program_id(0) == 0)
        def _():
            dg_ref[...] = jnp.zeros_like(dg_ref)

        xv = x_ref[...]
        dyv = dy_ref[...].astype(F32)
        r = lax.rsqrt(jnp.mean(xv * xv, axis=-1, keepdims=True) + EPS)
        gdy = dyv * g_ref[...]
        mean_t = jnp.mean(xv * gdy, axis=-1, keepdims=True)
        dx_ref[...] = dr_ref[...] + r * gdy - xv * (r * r * r) * mean_t
        dg_ref[...] += jnp.sum(dyv * xv * r, axis=0, keepdims=True)

    row = pl.BlockSpec((tm, d), lambda i: (i, 0))
    vec = pl.BlockSpec((1, d), lambda i: (0, 0))
    return pl.pallas_call(
        body, name=name, grid=(s // tm,), in_specs=[row, row, vec, row], out_specs=[row, vec],
        out_shape=[jax.ShapeDtypeStruct((s, d), F32), jax.ShapeDtypeStruct((1, d), F32)],
        compiler_params=_params(("arbitrary",)))(dy, x, g.reshape(1, d), dres)


def _headnorm_fwd(x, g, name, scale=1.0, gate=None, gate_col0=0, out_dtype=BF16, width=None):
    s, d = x.shape[0], (width or x.shape[1])
    nh = d // HEAD
    tm = _pick(s, (256, 128))
    has_gate = gate is not None
    gb = gate_col0 // d

    def body(*refs):
        if has_gate:
            x_ref, g_ref, gt_ref, o_ref = refs
        else:
            x_ref, g_ref, o_ref = refs
        gv = g_ref[...]
        for h in range(nh):
            sl = slice(h * HEAD, (h + 1) * HEAD)
            xv = x_ref[:, sl].astype(F32)
            r = lax.rsqrt(jnp.mean(xv * xv, axis=-1, keepdims=True) + EPS)
            y = xv * r * gv
            if scale != 1.0:
                y = y * scale
            if has_gate:
                y = y * _silu(gt_ref[:, sl])
            o_ref[:, sl] = y.astype(out_dtype)

    row = pl.BlockSpec((tm, d), lambda i: (i, 0))
    ins = [x, g.reshape(1, HEAD)]
    in_specs = [row, pl.BlockSpec((1, HEAD), lambda i: (0, 0))]
    if has_gate:
        ins.append(gate)
        in_specs.append(pl.BlockSpec((tm, d), lambda i: (i, gb)))
    return pl.pallas_call(
        body, name=name, grid=(s // tm,), in_specs=in_specs, out_specs=row,
        out_shape=jax.ShapeDtypeStruct((s, d), out_dtype), compiler_params=_params(("parallel",)))(*ins)


def _headnorm_bwd(dy, x, g, name, scale=1.0, gate=None, gate_col0=0, dx_dtype=F32):
    s, d = dy.shape
    nh = d // HEAD
    tm = _pick(s, (256, 128))
    has_gate = gate is not None
    gb = gate_col0 // d

    def body(*refs):
        if has_gate:
            dy_ref, x_ref, g_ref, gt_ref, dx_ref, dg_ref, dgt_ref = refs
        else:
            dy_ref, x_ref, g_ref, dx_ref, dg_ref = refs

        @pl.when(pl.program_id(0) == 0)
        def _():
            dg_ref[...] = jnp.zeros_like(dg_ref)

        gv = g_ref[...]
        dg_acc = jnp.zeros((1, HEAD), F32)
        for h in range(nh):
            sl = slice(h * HEAD, (h + 1) * HEAD)
            xv = x_ref[:, sl].astype(F32)
            dyv = dy_ref[:, sl].astype(F32)
            r = lax.rsqrt(jnp.mean(xv * xv, axis=-1, keepdims=True) + EPS)
            if has_gate:
                gt = gt_ref[:, sl]
                dgt_ref[:, sl] = (dyv * (xv * r * gv) * _dsilu(gt)).astype(dgt_ref.dtype)
                dn = dyv * _silu(gt)
            else:
                dn = dyv
            if scale != 1.0:
                dn = dn * scale
            gdn = dn * gv
            mean_t = jnp.mean(xv * gdn, axis=-1, keepdims=True)
            dx_ref[:, sl] = (r * gdn - xv * (r * r * r) * mean_t).astype(dx_dtype)
            dg_acc = dg_acc + jnp.sum(dn * xv * r, axis=0, keepdims=True)
        dg_ref[...] += dg_acc

    row = pl.BlockSpec((tm, d), lambda i: (i, 0))
    vec = pl.BlockSpec((1, HEAD), lambda i: (0, 0))
    ins = [dy, x, g.reshape(1, HEAD)]
    in_specs = [row, row, vec]
    out_specs = [row, vec]
    out_shape = [jax.ShapeDtypeStruct((s, d), dx_dtype), jax.ShapeDtypeStruct((1, HEAD), F32)]
    if has_gate:
        ins.append(gate)
        in_specs.append(pl.BlockSpec((tm, d), lambda i: (i, gb)))
        out_specs.append(row)
        out_shape.append(jax.ShapeDtypeStruct((s, d), BF16))
    return pl.pallas_call(
        body, name=name, grid=(s // tm,), in_specs=in_specs, out_specs=out_specs, out_shape=out_shape,
        compiler_params=_params(("arbitrary",)))(*ins)


def _swiglu_fwd(hn, wf_t, name):
    s, d = hn.shape
    f = wf_t.shape[0] // 2
    tm = _pick(s, (512, 256, 128))
    tn = _pick(f, (512, 256, 128))
    nj = f // tn

    def body(a_ref, wg_ref, wu_ref, act_ref, g_ref, u_ref):
        a = a_ref[...]
        g = _dot(a, wg_ref[...], NT)
        u = _dot(a, wu_ref[...], NT)
        act_ref[...] = (_silu(g) * u).astype(BF16)
        g_ref[...] = g.astype(BF16)
        u_ref[...] = u.astype(BF16)

    o_spec = pl.BlockSpec((tm, tn), lambda i, j: (i, j))
    sds = jax.ShapeDtypeStruct((s, f), BF16)
    return pl.pallas_call(
        body, name=name, grid=(s // tm, nj),
        in_specs=[pl.BlockSpec((tm, d), lambda i, j: (i, 0)), pl.BlockSpec((tn, d), lambda i, j: (j, 0)),
                  pl.BlockSpec((tn, d), lambda i, j: (j + nj, 0))],
        out_specs=[o_spec, o_spec, o_spec], out_shape=[sds, sds, sds],
        compiler_params=_params(("parallel", "parallel")))(hn, wf_t, wf_t)


def _swiglu_bwd(dh, w_out, g, u, name):
    s, d = dh.shape
    f = w_out.shape[0]
    tm = _pick(s, (512, 256, 128))
    tn = _pick(f, (512, 256, 128))

    def body(dh_ref, w_ref, g_ref, u_ref, dg_ref, du_ref):
        dact = _dot(dh_ref[...], w_ref[...], NT)
        gv = g_ref[...].astype(F32)
        uv = u_ref[...].astype(F32)
        dg_ref[...] = (dact * uv * _dsilu(gv)).astype(BF16)
        du_ref[...] = (dact * _silu(gv)).astype(BF16)

    o_spec = pl.BlockSpec((tm, tn), lambda i, j: (i, j))
    sds = jax.ShapeDtypeStruct((s, f), BF16)
    return pl.pallas_call(
        body, name=name, grid=(s // tm, f // tn),
        in_specs=[pl.BlockSpec((tm, d), lambda i, j: (i, 0)), pl.BlockSpec((tn, d), lambda i, j: (j, 0)), o_spec, o_spec],
        out_specs=[o_spec, o_spec], out_shape=[sds, sds],
        compiler_params=_params(("parallel", "parallel")))(dh, w_out, g, u)


def _ple_fwd(h, hn, p, w_gate, wp_t, name):
    s, d = h.shape
    pd = p.shape[1]
    tm = _pick(s, (512, 256, 128))
    tn = _pick(d, (512, 256, 128))

    def body(h_ref, hn_ref, p_ref, wg_ref, wp_ref, o_ref, gp_ref, pp_ref):
        gpre = _dot(hn_ref[...], wg_ref[...], NN)
        pp = _dot(p_ref[...], wp_ref[...], NT)
        o_ref[...] = h_ref[...] + pp * jax.nn.sigmoid(gpre)
        gp_ref[...] = gpre.astype(BF16)
        pp_ref[...] = pp.astype(BF16)

    mn = pl.BlockSpec((tm, tn), lambda i, j: (i, j))
    return pl.pallas_call(
        body, name=name, grid=(s // tm, d // tn),
        in_specs=[mn, pl.BlockSpec((tm, d), lambda i, j: (i, 0)), pl.BlockSpec((tm, pd), lambda i, j: (i, 0)),
                  pl.BlockSpec((d, tn), lambda i, j: (0, j)), pl.BlockSpec((tn, pd), lambda i, j: (j, 0))],
        out_specs=[mn, mn, mn],
        out_shape=[jax.ShapeDtypeStruct((s, d), F32), jax.ShapeDtypeStruct((s, d), BF16), jax.ShapeDtypeStruct((s, d), BF16)],
        compiler_params=_params(("parallel", "parallel")))(h, hn, p, w_gate, wp_t)


def _ple_bwd(dh, gpre, pp, name):
    s, d = dh.shape
    tm = _pick(s, (512, 256, 128))

    def body(dh_ref, gp_ref, pp_ref, dgp_ref, dpp_ref):
        dv = dh_ref[...]
        sig = jax.nn.sigmoid(gp_ref[...].astype(F32))
        ppv = pp_ref[...].astype(F32)
        dpp_ref[...] = (dv * sig).astype(BF16)
        dgp_ref[...] = (dv * ppv * sig * (1.0 - sig)).astype(BF16)

    row = pl.BlockSpec((tm, d), lambda i: (i, 0))
    sds = jax.ShapeDtypeStruct((s, d), BF16)
    return pl.pallas_call(
        body, name=name, grid=(s // tm,), in_specs=[row, row, row], out_specs=[row, row], out_shape=[sds, sds],
        compiler_params=_params(("parallel",)))(dh, gpre, pp)


def _loss_fwd_bwd(y, t, name):
    s, d = y.shape
    tm = _pick(s, (512, 256, 128))

    def body(y_ref, t_ref, dy_ref, l_ref):
        @pl.when(pl.program_id(0) == 0)
        def _():
            l_ref[...] = jnp.zeros_like(l_ref)

        e = y_ref[...] - t_ref[...]
        dy_ref[...] = e * (1.0 / d)
        l_ref[...] += jnp.sum(e * e, axis=0, keepdims=True) * (0.5 / d)

    row = pl.BlockSpec((tm, d), lambda i: (i, 0))
    vec = pl.BlockSpec((1, d), lambda i: (0, 0))
    return pl.pallas_call(
        body, name=name, grid=(s // tm,), in_specs=[row, row], out_specs=[row, vec],
        out_shape=[jax.ShapeDtypeStruct((s, d), F32), jax.ShapeDtypeStruct((1, d), F32)],
        compiler_params=_params(("arbitrary",)))(y, t)


PADR = 8


def _conv_fwd(proj, w_conv, d, name):
    s = proj.shape[0]
    nh = d // HEAD
    kw = w_conv.shape[0]
    qscale = HEAD ** -0.5

    def body(x_ref, w_ref, o_ref, xp):
        kind = pl.program_id(0) // nh
        xp[0:PADR, :] = jnp.zeros((PADR, HEAD), F32)
        xp[PADR:, :] = x_ref[...]
        acc = jnp.zeros((s, HEAD), F32)
        for j in range(kw):
            acc = acc + w_ref[j:j + 1, :] * xp[PADR - (kw - 1) + j:PADR - (kw - 1) + j + s, :]
        a = _silu(acc)
        r = lax.rsqrt(jnp.sum(a * a, axis=-1, keepdims=True) + EPS)
        fac = jnp.where(kind == 0, r * qscale, jnp.where(kind == 1, r, jnp.ones_like(r)))
        o_ref[...] = a * fac

    blk = pl.BlockSpec((s, HEAD), lambda c: (0, c))
    return pl.pallas_call(
        body, name=name, grid=(3 * nh,), in_specs=[blk, pl.BlockSpec((kw, HEAD), lambda c: (0, c))], out_specs=blk,
        out_shape=jax.ShapeDtypeStruct((s, 3 * d), F32), scratch_shapes=[pltpu.VMEM((s + PADR, HEAD), F32)],
        compiler_params=_params(("parallel",)))(proj, w_conv)


def _conv_bwd(dqkv, proj, w_conv, d, name):
    s = proj.shape[0]
    nh = d // HEAD
    kw = w_conv.shape[0]
    qscale = HEAD ** -0.5

    def body(dy_ref, x_ref, w_ref, dx_ref, dw_ref, xp, dp):
        kind = pl.program_id(0) // nh
        xp[0:PADR, :] = jnp.zeros((PADR, HEAD), F32)
        xp[PADR:, :] = x_ref[...]
        acc = jnp.zeros((s, HEAD), F32)
        for j in range(kw):
            acc = acc + w_ref[j:j + 1, :] * xp[PADR - (kw - 1) + j:PADR - (kw - 1) + j + s, :]
        a = _silu(acc)
        dy = dy_ref[...]
        r = lax.rsqrt(jnp.sum(a * a, axis=-1, keepdims=True) + EPS)
        sc = jnp.where(kind == 0, qscale, 1.0)
        dyn = dy * sc
        da_norm = r * dyn - a * (r * r * r) * jnp.sum(a * dyn, axis=-1, keepdims=True)
        da = jnp.where(kind == 2, dy, da_norm)
        dacc = da * _dsilu(acc)
        dp[0:s, :] = dacc
        dp[s:, :] = jnp.zeros((PADR, HEAD), F32)
        dx = jnp.zeros((s, HEAD), F32)
        for j in range(kw):
            sh = kw - 1 - j
            dx = dx + w_ref[j:j + 1, :] * dp[sh:sh + s, :]
            dw_ref[j:j + 1, :] = jnp.sum(dacc * xp[PADR - sh:PADR - sh + s, :], axis=0, keepdims=True)
        dx_ref[...] = dx.astype(BF16)

    blk = pl.BlockSpec((s, HEAD), lambda c: (0, c))
    wblk = pl.BlockSpec((kw, HEAD), lambda c: (0, c))
    return pl.pallas_call(
        body, name=name, grid=(3 * nh,), in_specs=[blk, blk, wblk], out_specs=[blk, wblk],
        out_shape=[jax.ShapeDtypeStruct((s, 3 * d), BF16), jax.ShapeDtypeStruct((kw, 3 * d), F32)],
        scratch_shapes=[pltpu.VMEM((s + PADR, HEAD), F32), pltpu.VMEM((s + PADR, HEAD), F32)],
        compiler_params=_params(("parallel",)))(dqkv, proj, w_conv)


def _softplus(x):
    return jnp.maximum(x, 0.0) + jnp.log(1.0 + jnp.exp(-jnp.abs(x)))


def _gates_fwd(pab, a_log, dt_bias, nh, name):
    s = pab.shape[0]
    tm = _pick(s, (512, 256, 128))

    def body(x_ref, al_ref, dt_ref, o_ref):
        x = x_ref[...]
        lane = lax.broadcasted_iota(jnp.int32, x.shape, 1)
        g = -jnp.exp(al_ref[...]) * _softplus(x + dt_ref[...])
        o_ref[...] = jnp.where(lane < nh, g, jnp.where(lane < 2 * nh, jax.nn.sigmoid(x), 0.0))

    row = pl.BlockSpec((tm, HEAD), lambda i: (i, 0))
    vec = pl.BlockSpec((1, HEAD), lambda i: (0, 0))
    return pl.pallas_call(
        body, name=name, grid=(s // tm,), in_specs=[row, vec, vec], out_specs=row,
        out_shape=jax.ShapeDtypeStruct((s, HEAD), F32), compiler_params=_params(("parallel",)))(pab, a_log, dt_bias)


def _gates_bwd(dgb, pab, a_log, dt_bias, nh, name):
    s = pab.shape[0]
    tm = _pick(s, (512, 256, 128))

    def body(d_ref, x_ref, al_ref, dt_ref, dx_ref, dal_ref, ddt_ref):
        @pl.when(pl.program_id(0) == 0)
        def _():
            dal_ref[...] = jnp.zeros_like(dal_ref)
            ddt_ref[...] = jnp.zeros_like(ddt_ref)

        x = x_ref[...]
        dv = d_ref[...]
        lane = lax.broadcasted_iota(jnp.int32, x.shape, 1)
        ea = jnp.exp(al_ref[...])
        xs = x + dt_ref[...]
        g = -ea * _softplus(xs)
        dxs = jnp.where(lane < nh, dv * (-ea) * jax.nn.sigmoid(xs), 0.0)
        sg = jax.nn.sigmoid(x)
        dxb = jnp.where((lane >= nh) & (lane < 2 * nh), dv * sg * (1.0 - sg), 0.0)
        dx_ref[...] = (dxs + dxb).astype(BF16)
        dal_ref[...] += jnp.sum(jnp.where(lane < nh, dv * g, 0.0), axis=0, keepdims=True)
        ddt_ref[...] += jnp.sum(dxs, axis=0, keepdims=True)

    row = pl.BlockSpec((tm, HEAD), lambda i: (i, 0))
    vec = pl.BlockSpec((1, HEAD), lambda i: (0, 0))
    return pl.pallas_call(
        body, name=name, grid=(s // tm,), in_specs=[row, row, vec, vec], out_specs=[row, vec, vec],
        out_shape=[jax.ShapeDtypeStruct((s, HEAD), BF16), jax.ShapeDtypeStruct((1, HEAD), F32),
                   jax.ShapeDtypeStruct((1, HEAD), F32)],
        compiler_params=_params(("arbitrary",)))(dgb, pab, a_log, dt_bias)


def _tri_inv(a_low, eye_f):
    n = -a_low
    p = eye_f + n
    steps = int(math.log2(a_low.shape[0])) - 1
    for _ in range(steps):
        n = _dot(n, n)
        p = p + _dot(p, n)
    return p


def _lane_col(x, lane, idx):
    return jnp.sum(jnp.where(lane == idx, x, 0.0), axis=1, keepdims=True)


def _gdn_chunk(qh, kh, vh, g_col, beta_col, st):
    c = qh.shape[0]
    r_i = lax.broadcasted_iota(jnp.int32, (c, c), 0)
    c_i = lax.broadcasted_iota(jnp.int32, (c, c), 1)
    incl = c_i <= r_i
    strict = c_i < r_i
    eye = c_i == r_i
    g_row = jnp.sum(jnp.where(eye, g_col, 0.0), axis=0, keepdims=True)
    gc_col = jnp.sum(jnp.where(incl, g_row, 0.0), axis=1, keepdims=True)
    gc_row = jnp.sum(jnp.where(eye, gc_col, 0.0), axis=0, keepdims=True)
    g_last = jnp.sum(g_col, axis=0, keepdims=True)
    decay = jnp.exp(jnp.where(incl, gc_col - gc_row, NEG))
    kk = _dot(kh, kh, NT)
    a_low = jnp.where(strict, beta_col * kk * decay, 0.0)
    t_inv = _tri_inv(a_low, eye.astype(F32))
    e_g = jnp.exp(gc_col)
    bk = beta_col * e_g
    rhs = jnp.concatenate([vh * beta_col, kh * bk], axis=1)
    sol = _dot(t_inv, rhs)
    u, w = sol[:, :HEAD], sol[:, HEAD:]
    qk_raw = _dot(qh, kh, NT)
    qk = qk_raw * decay
    q_dec = qh * e_g
    e2 = jnp.exp(g_last - gc_col)
    k_dec = kh * e2
    gl = jnp.exp(g_last)
    ws = _dot(jnp.concatenate([w, q_dec], axis=0), st)
    v_new = u - ws[:c]
    o = ws[c:] + _dot(qk, v_new)
    st_new = st * gl + _dot(k_dec, v_new, TN)
    inter = dict(incl=incl, strict=strict, eye=eye, decay=decay, kk=kk, t_inv=t_inv, e_g=e_g, bk=bk, sol=sol, w=w,
                 qk_raw=qk_raw, qk=qk, q_dec=q_dec, e2=e2, k_dec=k_dec, gl=gl, v_new=v_new, c_i=c_i, r_i=r_i)
    return o, st_new, inter


def _gdn_fwd(qkv, gb, d, name):
    s = qkv.shape[0]
    nh = d // HEAD
    nc = s // CHUNK

    def body(q_ref, k_ref, v_ref, gb_ref, o_ref, st_ref, state):
        @pl.when(pl.program_id(0) == 0)
        def _():
            state[...] = jnp.zeros_like(state)

        gbv = gb_ref[...]
        lane = lax.broadcasted_iota(jnp.int32, gbv.shape, 1)
        st_ref[...] = state[...]
        for h in range(nh):
            sl = slice(h * HEAD, (h + 1) * HEAD)
            g_col = _lane_col(gbv, lane, h)
            beta_col = _lane_col(gbv, lane, nh + h)
            o, st_new, _ = _gdn_chunk(q_ref[:, sl], k_ref[:, sl], v_ref[:, sl], g_col, beta_col, state[sl, :])
            o_ref[:, sl] = o
            state[sl, :] = st_new

    def qspec(part):
        return pl.BlockSpec((CHUNK, d), lambda n: (n, part))

    return pl.pallas_call(
        body, name=name, grid=(nc,),
        in_specs=[qspec(0), qspec(1), qspec(2), pl.BlockSpec((CHUNK, HEAD), lambda n: (n, 0))],
        out_specs=[pl.BlockSpec((CHUNK, d), lambda n: (n, 0)), pl.BlockSpec((None, d, HEAD), lambda n: (n, 0, 0))],
        out_shape=[jax.ShapeDtypeStruct((s, d), F32), jax.ShapeDtypeStruct((nc, d, HEAD), F32)],
        scratch_shapes=[pltpu.VMEM((d, HEAD), F32)],
        compiler_params=_params(("arbitrary",)))(qkv, qkv, qkv, gb)


def _gdn_bwd(qkv, gb, do, states, d, name):
    s = qkv.shape[0]
    nh = d // HEAD
    nc = s // CHUNK
    c = CHUNK

    def body(q_ref, k_ref, v_ref, gb_ref, do_ref, st_ref, dq_ref, dk_ref, dv_ref, dgb_ref, dstate):
        @pl.when(pl.program_id(0) == 0)
        def _():
            dstate[...] = jnp.zeros_like(dstate)

        gbv = gb_ref[...]
        lane = lax.broadcasted_iota(jnp.int32, gbv.shape, 1)
        dgb_acc = jnp.zeros(gbv.shape, F32)
        for h in range(nh):
            sl = slice(h * HEAD, (h + 1) * HEAD)
            qh, kh, vh = q_ref[:, sl], k_ref[:, sl], v_ref[:, sl]
            g_col = _lane_col(gbv, lane, h)
            beta_col = _lane_col(gbv, lane, nh + h)
            st = st_ref[sl, :]
            dst = dstate[sl, :]
            doh = do_ref[:, sl]
            _, _, it = _gdn_chunk(qh, kh, vh, g_col, beta_col, st)
            incl, strict, eye, decay = it["incl"], it["strict"], it["eye"], it["decay"]
            u_w = it["sol"]
            dv_new = _dot(it["qk"], doh, TN) + _dot(it["k_dec"], dst)
            d_qk = _dot(doh, it["v_new"], NT)
            dd = _dot(jnp.concatenate([doh, -dv_new], axis=0), st, NT)
            dq_dec, dw = dd[:c], dd[c:]
            dst_new = _dot(it["q_dec"], doh, TN) + it["gl"] * dst - _dot(it["w"], dv_new, TN)
            dgl = jnp.sum(jnp.sum(dst * st, axis=1, keepdims=True), axis=0, keepdims=True)
            dk_dec = _dot(it["v_new"], dst, NT)
            dsol = jnp.concatenate([dv_new, dw], axis=1)
            drhs = _dot(it["t_inv"], dsol, TN)
            d_a = jnp.where(strict, -_dot(drhs, u_w, NT), 0.0)
            drhs_u, drhs_w = drhs[:, :HEAD], drhs[:, HEAD:]
            dvh = beta_col * drhs_u
            rw_k = jnp.sum(drhs_w * kh, axis=1, keepdims=True)
            dbeta = jnp.sum(drhs_u * vh, axis=1, keepdims=True) + it["e_g"] * rw_k
            dkh = it["bk"] * drhs_w
            dgc_col = it["bk"] * rw_k
            dkk = d_a * beta_col * decay
            dbeta = dbeta + jnp.sum(d_a * it["kk"] * decay, axis=1, keepdims=True)
            ddecay = d_a * beta_col * it["kk"]
            dkh = dkh + _dot(dkk, kh) + _dot(dkk, kh, TN)
            dqk_raw = d_qk * decay
            ddecay = ddecay + d_qk * it["qk_raw"]
            dqh = _dot(dqk_raw, kh)
            dkh = dkh + _dot(dqk_raw, qh, TN)
            ddm = jnp.where(incl, ddecay * decay, 0.0)
            dgc_col = dgc_col + jnp.sum(ddm, axis=1, keepdims=True)
            dgc_row = -jnp.sum(ddm, axis=0, keepdims=True)
            dqh = dqh + dq_dec * it["e_g"]
            dgc_col = dgc_col + jnp.sum(dq_dec * it["q_dec"], axis=1, keepdims=True)
            dkh = dkh + dk_dec * it["e2"]
            tmp = jnp.sum(dk_dec * it["k_dec"], axis=1, keepdims=True)
            dgc_col = dgc_col - tmp
            dg_last = jnp.sum(tmp, axis=0, keepdims=True) + dgl * it["gl"]
            dgc_tot_row = dgc_row + jnp.sum(jnp.where(eye, dgc_col, 0.0), axis=0, keepdims=True)
            dg_col = jnp.sum(jnp.where(it["c_i"] >= it["r_i"], dgc_tot_row, 0.0), axis=1, keepdims=True) + dg_last
            dq_ref[:, sl] = dqh
            dk_ref[:, sl] = dkh
            dv_ref[:, sl] = dvh
            dstate[sl, :] = dst_new
            dgb_acc = jnp.where(lane == h, dg_col, jnp.where(lane == nh + h, dbeta, dgb_acc))
        dgb_ref[...] = dgb_acc

    def rev(part):
        return pl.BlockSpec((CHUNK, d), lambda n: (nc - 1 - n, part))

    gspec = pl.BlockSpec((CHUNK, HEAD), lambda n: (nc - 1 - n, 0))
    sds = jax.ShapeDtypeStruct((s, d), F32)
    dq, dk, dv, dgb = pl.pallas_call(
        body, name=name, grid=(nc,),
        in_specs=[rev(0), rev(1), rev(2), gspec, rev(0), pl.BlockSpec((None, d, HEAD), lambda n: (nc - 1 - n, 0, 0))],
        out_specs=[rev(0), rev(0), rev(0), gspec],
        out_shape=[sds, sds, sds, jax.ShapeDtypeStruct((s, HEAD), F32)],
        scratch_shapes=[pltpu.VMEM((d, HEAD), F32)],
        compiler_params=_params(("arbitrary",)))(qkv, qkv, qkv, gb, do, states)
    return dq, dk, dv, dgb


def _sb_tile(qt, kblk, qb, kb, csum):
    tq = qt.shape[0]
    r_i = lax.broadcasted_iota(jnp.int32, (tq, SBLK), 0)
    c_i = lax.broadcasted_iota(jnp.int32, (tq, SBLK), 1)
    j_i = lax.broadcasted_iota(jnp.int32, (SBLK, SBLK), 0)
    s_i = lax.broadcasted_iota(jnp.int32, (SBLK, SBLK), 1)
    z = _dot(qt, kblk, NT)
    mask = (kb * SBLK + c_i) < (qb * tq + r_i)
    sp = _softplus(z)
    ln = jnp.where(mask, -sp, 0.0)
    suffix = _dot_hilo(ln, (j_i > s_i).astype(BF16))
    logw = (z - sp) + suffix + csum
    wgt = jnp.where(mask, jnp.exp(logw), 0.0)
    return z, mask, ln, wgt


def _sb_fwd(q, k, v, name):
    s, d = q.shape
    nh = d // HEAD
    tq = SBLK

    def body(q_ref, k_ref, v_ref, o_ref, c_ref, acc, cs):
        qb = pl.program_id(1)
        qt = q_ref[...]
        lane = lax.broadcasted_iota(jnp.int32, (tq, HEAD), 1)
        acc[...] = jnp.zeros_like(acc)
        cs[...] = jnp.zeros_like(cs)
        c_ref[...] = jnp.zeros_like(c_ref)

        def step(it, carry):
            kb = qb - it
            off = pl.multiple_of(kb * SBLK, SBLK)
            csum = cs[...]
            _, _, ln, wgt = _sb_tile(qt, k_ref[pl.ds(off, SBLK), :], qb, kb, csum)
            acc[...] += _dot(wgt, v_ref[pl.ds(off, SBLK), :])
            c_ref[...] = jnp.where(lane == kb, csum, c_ref[...])
            cs[...] = csum + jnp.sum(ln, axis=1, keepdims=True)
            return carry

        lax.fori_loop(0, qb + 1, step, 0)
        o_ref[...] = acc[...].astype(BF16)

    qspec = pl.BlockSpec((tq, HEAD), lambda h, i: (i, h))
    kspec = pl.BlockSpec((s, HEAD), lambda h, i: (0, h))
    return pl.pallas_call(
        body, name=name, grid=(nh, s // tq), in_specs=[qspec, kspec, kspec], out_specs=[qspec, qspec],
        out_shape=[jax.ShapeDtypeStruct((s, d), BF16), jax.ShapeDtypeStruct((s, d), F32)],
        scratch_shapes=[pltpu.VMEM((tq, HEAD), F32), pltpu.VMEM((tq, 1), F32)],
        compiler_params=_params(("parallel", "arbitrary")))(q, k, v)


def _sb_bwd(q, k, v, do, ctab, name):
    s, d = q.shape
    nh = d // HEAD
    tq = SBLK

    def body(q_ref, k_ref, v_ref, do_ref, c_ref, dq_ref, dk_ref, dv_ref, ps):
        qb = pl.program_id(1)

        @pl.when(qb == 0)
        def _():
            dk_ref[...] = jnp.zeros_like(dk_ref)
            dv_ref[...] = jnp.zeros_like(dv_ref)

        dq_ref[...] = jnp.zeros_like(dq_ref)
        ps[...] = jnp.zeros_like(ps)
        qt = q_ref[...]
        dot_ = do_ref[...]
        ctv = c_ref[...]
        lane = lax.broadcasted_iota(jnp.int32, (tq, HEAD), 1)
        j_i = lax.broadcasted_iota(jnp.int32, (SBLK, SBLK), 0)
        s_i = lax.broadcasted_iota(jnp.int32, (SBLK, SBLK), 1)
        before = (j_i < s_i).astype(BF16)

        def step(kb, carry):
            off = pl.multiple_of(kb * SBLK, SBLK)
            kblk = k_ref[pl.ds(off, SBLK), :]
            vblk = v_ref[pl.ds(off, SBLK), :]
            csum = _lane_col(ctv, lane, kb)
            z, mask, _, wgt = _sb_tile(qt, kblk, qb, kb, csum)
            dlw = _dot(dot_, vblk, NT) * wgt
            pfx = ps[...]
            pre = pfx + _dot_hilo(dlw, before)
            sig = jax.nn.sigmoid(z)
            dz = jnp.where(mask, dlw * (1.0 - sig) - sig * pre, 0.0)
            dq_ref[...] += _dot(dz, kblk)
            dk_ref[pl.ds(off, SBLK), :] += _dot(dz, qt, TN)
            dv_ref[pl.ds(off, SBLK), :] += _dot(wgt, dot_, TN)
            ps[...] = pfx + jnp.sum(dlw, axis=1, keepdims=True)
            return carry

        lax.fori_loop(0, qb + 1, step, 0)

    qspec = pl.BlockSpec((tq, HEAD), lambda h, i: (i, h))
    kspec = pl.BlockSpec((s, HEAD), lambda h, i: (0, h))
    sds = jax.ShapeDtypeStruct((s, d), F32)
    return pl.pallas_call(
        body, name=name, grid=(nh, s // tq), in_specs=[qspec, kspec, kspec, qspec, qspec],
        out_specs=[qspec, kspec, kspec], out_shape=[sds, sds, sds],
        scratch_shapes=[pltpu.VMEM((tq, 1), F32)],
        compiler_params=_params(("parallel", "arbitrary")))(q, k, v, do, ctab)


def _my_index():
    return 4 * lax.axis_index("x") + 2 * lax.axis_index("y") + lax.axis_index("c")


def _all_gather(x_shard, name):
    m_per, n = x_shard.shape

    def body(x_ref, out_ref, send_sems, recv_sems, local_sem):
        x, y, c = lax.axis_index("x"), lax.axis_index("y"), lax.axis_index("c")
        me, sibling = (x, y, c), (x, y, 1 - c)
        chips = [(1 - x, y), (x, 1 - y), (1 - x, 1 - y)]

        def rows(px, py, pc):
            return out_ref.at[pl.ds((4 * px + 2 * py + pc) * m_per, m_per), :]

        def copy(k, block, to, src=None):
            return pltpu.make_async_remote_copy(
                src_ref=rows(*block) if src is None else src, dst_ref=rows(*block),
                send_sem=send_sems.at[k], recv_sem=recv_sems.at[k], device_id=to, device_id_type=MESH)

        mine = pltpu.make_async_copy(x_ref, rows(*me), local_sem)
        mine.start()
        first = [copy(0, me, sibling, src=x_ref)]
        first += [copy(1 + j, me, (*chip, c), src=x_ref) for j, chip in enumerate(chips)]
        for cp in first:
            cp.start()
        passed = [copy(4 + j, (*chip, c), sibling) for j, chip in enumerate(chips)]
        for j, chip in enumerate(chips):
            copy(1 + j, (*chip, c), me).wait_recv()
            passed[j].start()
        copy(0, sibling, me).wait_recv()
        for j, chip in enumerate(chips):
            copy(4 + j, (*chip, 1 - c), me).wait_recv()
        for cp in first + passed:
            cp.wait_send()
        mine.wait()

    return pl.pallas_call(
        body, name=name, out_shape=jax.ShapeDtypeStruct((NDEV * m_per, n), x_shard.dtype),
        in_specs=[pl.BlockSpec(memory_space=pl.ANY)], out_specs=pl.BlockSpec(memory_space=pl.ANY),
        scratch_shapes=[pltpu.SemaphoreType.DMA((7,)), pltpu.SemaphoreType.DMA((7,)), pltpu.SemaphoreType.DMA],
    )(x_shard)


def _exchange(slabs, name):
    rows8, n = slabs.shape
    m_per = rows8 // NDEV

    def body(x_ref, out_ref, send_sems, recv_sems, local_sem):
        x, y, c = lax.axis_index("x"), lax.axis_index("y"), lax.axis_index("c")
        me = 4 * x + 2 * y + c

        def slab(ref, idx):
            return ref.at[pl.ds(idx * m_per, m_per), :]

        mine = pltpu.make_async_copy(slab(x_ref, me), slab(out_ref, me), local_sem)
        mine.start()
        copies = []
        for k in range(1, NDEV):
            px, py, pc = x ^ ((k >> 2) & 1), y ^ ((k >> 1) & 1), c ^ (k & 1)
            peer = 4 * px + 2 * py + pc
            copies.append(pltpu.make_async_remote_copy(
                src_ref=slab(x_ref, peer), dst_ref=slab(out_ref, me), send_sem=send_sems.at[k - 1],
                recv_sem=recv_sems.at[k - 1], device_id=(px, py, pc), device_id_type=MESH))
        for cp in copies:
            cp.start()
        for cp in copies:
            cp.wait_recv()
        for cp in copies:
            cp.wait_send()
        mine.wait()

    return pl.pallas_call(
        body, name=name, out_shape=jax.ShapeDtypeStruct((rows8, n), slabs.dtype),
        in_specs=[pl.BlockSpec(memory_space=pl.ANY)], out_specs=pl.BlockSpec(memory_space=pl.ANY),
        scratch_shapes=[pltpu.SemaphoreType.DMA((7,)), pltpu.SemaphoreType.DMA((7,)), pltpu.SemaphoreType.DMA],
    )(slabs)


def _sum_slots(x, name):
    _, r, c = x.shape
    tr = _pick(r, (512, 256, 128, 64, 32, 16, 8))

    def body(x_ref, o_ref):
        acc = x_ref[0].astype(F32)
        for i in range(1, NDEV):
            acc = acc + x_ref[i].astype(F32)
        o_ref[...] = acc

    return pl.pallas_call(
        body, name=name, grid=(r // tr,), in_specs=[pl.BlockSpec((NDEV, tr, c), lambda i: (0, i, 0))],
        out_specs=pl.BlockSpec((tr, c), lambda i: (i, 0)), out_shape=jax.ShapeDtypeStruct((r, c), F32),
        compiler_params=_params(("parallel",)))(x)


def _adamw(w, g, m, v, name):
    r, c = w.shape
    tr = _pick(r, (256, 128, 64, 32, 16, 8))
    c1 = 1.0 - B1 ** STEP
    c2 = 1.0 - B2 ** STEP

    def body(w_ref, g_ref, m_ref, v_ref, d_ref, nm_ref, nv_ref):
        gv = g_ref[...]
        nm = B1 * m_ref[...] + (1.0 - B1) * gv
        nv = B2 * v_ref[...] + (1.0 - B2) * (gv * gv)
        d_ref[...] = -LR * ((nm / c1) / (jnp.sqrt(nv / c2) + ADAM_EPS) + WD * w_ref[...])
        nm_ref[...] = nm
        nv_ref[...] = nv

    blk = pl.BlockSpec((tr, c), lambda i: (i, 0))
    sds = jax.ShapeDtypeStruct((r, c), F32)
    return pl.pallas_call(
        body, name=name, grid=(r // tr,), in_specs=[blk] * 4, out_specs=[blk] * 3, out_shape=[sds] * 3,
        compiler_params=_params(("parallel",)))(w, g, m, v)


def _pad_rows(a, mult):
    r = a.shape[0]
    pad = (-r) % mult
    return a if pad == 0 else jnp.pad(a, ((0, pad), (0, 0)))


def _pad_lanes(v, width=HEAD):
    return jnp.pad(v.reshape(1, -1), ((0, 0), (0, width - v.shape[-1])))


def kernel(x, p, ln_mix, ln_ffn, ln_ple, gdn_w_in, gdn_conv, gdn_a_log, gdn_dt_bias, gdn_norm, gdn_w_out, kv_norm, w_kv, k_norm, sb_w_q, sb_q_norm, sb_w_out, ffn_w_in, ffn_w_out, ple_w_proj, ple_w_gate, loss_target, m_ln_mix, m_ln_ffn, m_ln_ple, m_gdn_w_in, m_gdn_conv, m_gdn_a_log, m_gdn_dt_bias, m_gdn_norm, m_gdn_w_out, m_kv_norm, m_w_kv, m_k_norm, m_sb_w_q, m_sb_q_norm, m_sb_w_out, m_ffn_w_in, m_ffn_w_out, m_ple_w_proj, m_ple_w_gate, v_ln_mix, v_ln_ffn, v_ln_ple, v_gdn_w_in, v_gdn_conv, v_gdn_a_log, v_gdn_dt_bias, v_gdn_norm, v_gdn_w_out, v_kv_norm, v_w_kv, v_k_norm, v_sb_w_q, v_sb_q_norm, v_sb_w_out, v_ffn_w_in, v_ffn_w_out, v_ple_w_proj, v_ple_w_gate):
    s, d = x.shape[1], x.shape[2]
    nh = d // HEAD
    depth = ln_mix.shape[0]
    n_a = gdn_w_in.shape[0]
    n_b = sb_w_q.shape[0]
    me = _my_index()
    win_cols = gdn_w_in.shape[2]
    win_rows = 4 * d + 2 * nh

    def col_t(w):
        return jnp.transpose(w).astype(BF16)

    pieces = []
    parts = []

    def add(key, a):
        a = _pad_rows(a, 16)
        pieces.append((key, a.shape[0]))
        parts.append(a)

    for l in range(n_a):
        add(("gdn_w_in", l), col_t(gdn_w_in[l]))
        add(("gdn_w_out", l), gdn_w_out[l].astype(BF16))
    add(("w_kv", 0), col_t(w_kv))
    for j in range(n_b):
        add(("sb_w_q", j), sb_w_q[j].astype(BF16))
        add(("sb_w_out", j), sb_w_out[j].astype(BF16))
    for l in range(depth):
        add(("ffn_w_in", l), col_t(ffn_w_in[l]))
        add(("ffn_w_out", l), ffn_w_out[l].astype(BF16))
        add(("ple_w_proj", l), col_t(ple_w_proj[l]).reshape(-1, d))
        add(("ple_w_gate", l), ple_w_gate[l].astype(BF16))
    pack = jnp.concatenate(parts, axis=0)
    r_pack = pack.shape[0]
    gathered = _all_gather(pack, "comm_gather_weights").reshape(NDEV, r_pack, d)

    full = {}
    off = 0
    for key, r in pieces:
        full[key] = gathered[:, off:off + r, :]
        off += r

    def whole(key, valid=None):
        a = full[key]
        if valid is not None:
            a = a[:, :valid, :]
        return a.reshape(-1, d)

    pd = p.shape[-1]
    w_in_t, w_ab_t, w_gout = [], [], []
    for l in range(n_a):
        wt = whole(("gdn_w_in", l), win_cols)
        w_in_t.append(wt[:4 * d])
        w_ab_t.append(jnp.pad(wt[4 * d:], ((0, HEAD - 2 * nh), (0, 0))))
        w_gout.append(whole(("gdn_w_out", l)))
    wkv_t = whole(("w_kv", 0))
    w_q = [whole(("sb_w_q", j)) for j in range(n_b)]
    w_sout = [whole(("sb_w_out", j)) for j in range(n_b)]
    wf_t = [whole(("ffn_w_in", l)) for l in range(depth)]
    w_fout = [whole(("ffn_w_out", l)) for l in range(depth)]
    wp_t = [full[("ple_w_proj", l)].reshape(d, pd) for l in range(depth)]
    w_pg = [whole(("ple_w_gate", l)) for l in range(depth)]

    conv_rows = n_a * gdn_conv.shape[1]
    conv_sh = _pad_rows(gdn_conv.reshape(conv_rows, -1), 8)
    conv_g = _all_gather(conv_sh, "comm_gather_conv").reshape(NDEV, conv_sh.shape[0], -1)
    conv_full = jnp.transpose(conv_g[:, :conv_rows, :], (1, 0, 2)).reshape(n_a, gdn_conv.shape[1], 3 * d)

    h = x[0]
    sv = []
    kv_sv = None
    k_sh = v_sh = None
    for l in range(depth):
        t = {}
        t["h0"] = h
        hn = _rms_fwd(h, ln_mix[l], f"rms_mix_{l}")
        t["hn"] = hn
        if l < n_a:
            proj = _mm(hn, w_in_t[l], "nt", f"gdn_proj_{l}")
            pab = _mm(hn, w_ab_t[l], "nt", f"gdn_proj_ab_{l}")
            qkv = _conv_fwd(proj, conv_full[l], d, f"gdn_conv_{l}")
            al, dtb = _pad_lanes(gdn_a_log[l]), _pad_lanes(gdn_dt_bias[l])
            gb = _gates_fwd(pab, al, dtb, nh, f"gdn_gates_{l}")
            o_raw, states = _gdn_fwd(qkv, gb, d, f"gdn_rule_{l}")
            o2 = _headnorm_fwd(o_raw, gdn_norm[l], f"gdn_outnorm_{l}", gate=proj, gate_col0=3 * d)
            h = _mm(o2, w_gout[l], "nn", f"gdn_out_{l}", res=h)
            t.update(proj=proj, pab=pab, qkv=qkv, gb=gb, o_raw=o_raw, states=states, o2=o2, al=al, dtb=dtb)
        else:
            j = l - n_a
            qpre = _mm(hn, w_q[j], "nn", f"sb_qproj_{j}")
            qn = _headnorm_fwd(qpre, sb_q_norm[j], f"sb_qnorm_{j}", scale=HEAD ** -0.5)
            o, ctab = _sb_fwd(qn, k_sh, v_sh, f"sb_attn_{j}")
            h = _mm(o, w_sout[j], "nn", f"sb_out_{j}", res=h)
            t.update(qpre=qpre, qn=qn, o=o, ctab=ctab)
        t["h1"] = h
        hn2 = _rms_fwd(h, ln_ffn[l], f"rms_ffn_{l}")
        act, gs, us = _swiglu_fwd(hn2, wf_t[l], f"ffn_in_{l}")
        h = _mm(act, w_fout[l], "nn", f"ffn_out_{l}", res=h)
        t.update(hn2=hn2, act=act, gs=gs, us=us, h2=h)
        hn3 = _rms_fwd(h, ln_ple[l], f"rms_ple_{l}")
        h, gpre, pp = _ple_fwd(h, hn3, p[l, 0], w_pg[l], wp_t[l], f"ple_{l}")
        t.update(hn3=hn3, gpre=gpre, pp=pp)
        sv.append(t)
        if l == n_a - 1:
            kvn = _rms_fwd(h, kv_norm, "rms_kv")
            kv = _mm(kvn, wkv_t, "nt", "kv_proj")
            k_sh = _headnorm_fwd(kv, k_norm, "k_norm", width=d)
            v_sh = kv[:, d:].astype(BF16)
            kv_sv = dict(h=h, kvn=kvn, kv=kv)

    dh, loss_vec = _loss_fwd_bwd(h, loss_target[0], "loss")
    loss = lax.psum(jnp.sum(loss_vec), ("x", "y", "c"))

    gw = {}
    small = {}
    dk_sh = jnp.zeros((s, d), F32)
    dv_sh = jnp.zeros((s, d), F32)
    for l in reversed(range(depth)):
        t = sv[l]
        if l == n_a - 1:
            dkv_k, dkn = _headnorm_bwd(dk_sh, kv_sv["kv"], k_norm, "k_norm_bwd", dx_dtype=BF16)
            dkv = jnp.concatenate([dkv_k, dv_sh.astype(BF16)], axis=1)
            gw[("w_kv", 0)] = _mm(dkv, kv_sv["kvn"], "tn", "kv_dw", out_dtype=BF16)
            dkvn = _mm(dkv, wkv_t, "nn", "kv_dx")
            dh, dg = _rms_bwd(dkvn, kv_sv["h"], kv_norm, dh, "rms_kv_bwd")
            small["kv_norm"] = dg
            small["k_norm"] = dkn
        dgp, dpp = _ple_bwd(dh, t["gpre"], t["pp"], f"ple_bwd_{l}")
        gw[("ple_w_gate", l)] = _mm(t["hn3"], dgp, "tn", f"ple_dwg_{l}", out_dtype=BF16)
        gw[("ple_w_proj", l)] = _mm(dpp, p[l, 0], "tn", f"ple_dwp_{l}", out_dtype=BF16)
        dhn3 = _mm(dgp, w_pg[l], "nt", f"ple_dx_{l}")
        dh, dg = _rms_bwd(dhn3, t["h2"], ln_ple[l], dh, f"rms_ple_bwd_{l}")
        small[("ln_ple", l)] = dg
        dgs, dus = _swiglu_bwd(dh, w_fout[l], t["gs"], t["us"], f"ffn_bwd_act_{l}")
        gw[("ffn_w_out", l)] = _mm(t["act"], dh, "tn", f"ffn_dwo_{l}", out_dtype=BF16)
        f = dgs.shape[1]
        dwg = _mm(dgs, t["hn2"], "tn", f"ffn_dwg_{l}", out_dtype=BF16)
        dwu = _mm(dus, t["hn2"], "tn", f"ffn_dwu_{l}", out_dtype=BF16)
        gw[("ffn_w_in", l)] = jnp.concatenate([dwg, dwu], axis=0)
        dhn2 = _mm(dgs, wf_t[l][:f], "nn", f"ffn_dxg_{l}")
        dhn2 = _mm(dus, wf_t[l][f:], "nn", f"ffn_dxu_{l}", res=dhn2)
        dh, dg = _rms_bwd(dhn2, t["h1"], ln_ffn[l], dh, f"rms_ffn_bwd_{l}")
        small[("ln_ffn", l)] = dg
        if l < n_a:
            do2 = _mm(dh, w_gout[l], "nt", f"gdn_out_dx_{l}")
            gw[("gdn_w_out", l)] = _mm(t["o2"], dh, "tn", f"gdn_out_dw_{l}", out_dtype=BF16)
            do_raw, dgn, dgate = _headnorm_bwd(do2, t["o_raw"], gdn_norm[l], f"gdn_outnorm_bwd_{l}",
                                               gate=t["proj"], gate_col0=3 * d)
            small[("gdn_norm", l)] = dgn
            dq, dk, dv, dgb = _gdn_bwd(t["qkv"], t["gb"], do_raw, t["states"], d, f"gdn_rule_bwd_{l}")
            dpab, dal, ddt = _gates_bwd(dgb, t["pab"], t["al"], t["dtb"], nh, f"gdn_gates_bwd_{l}")
            small[("gdn_a_log", l)] = dal
            small[("gdn_dt_bias", l)] = ddt
            dqkv = jnp.concatenate([dq, dk, dv], axis=1)
            dproj_qkv, dconv = _conv_bwd(dqkv, t["proj"], conv_full[l], d, f"gdn_conv_bwd_{l}")
            small[("gdn_conv", l)] = dconv
            dproj = jnp.concatenate([dproj_qkv, dgate], axis=1)
            dw_main = _mm(dproj, t["hn"], "tn", f"gdn_proj_dw_{l}", out_dtype=BF16)
            dw_ab = _mm(dpab, t["hn"], "tn", f"gdn_proj_ab_dw_{l}", out_dtype=BF16)
            gw[("gdn_w_in", l)] = jnp.concatenate([dw_main, dw_ab[:16]], axis=0)[:win_rows]
            dhn = _mm(dproj, w_in_t[l], "nn", f"gdn_proj_dx_{l}")
            dhn = _mm(dpab, w_ab_t[l], "nn", f"gdn_proj_ab_dx_{l}", res=dhn)
        else:
            j = l - n_a
            do = _mm(dh, w_sout[j], "nt", f"sb_out_dx_{j}", out_dtype=BF16)
            gw[("sb_w_out", j)] = _mm(t["o"], dh, "tn", f"sb_out_dw_{j}", out_dtype=BF16)
            dq, dk, dv = _sb_bwd(t["qn"], k_sh, v_sh, do, t["ctab"], f"sb_attn_bwd_{j}")
            dk_sh = dk_sh + dk
            dv_sh = dv_sh + dv
            dqpre, dqn = _headnorm_bwd(dq, t["qpre"], sb_q_norm[j], f"sb_qnorm_bwd_{j}", scale=HEAD ** -0.5, dx_dtype=BF16)
            small[("sb_q_norm", j)] = dqn
            gw[("sb_w_q", j)] = _mm(t["hn"], dqpre, "tn", f"sb_q_dw_{j}", out_dtype=BF16)
            dhn = _mm(dqpre, w_q[j], "nt", f"sb_q_dx_{j}")
        dh, dg = _rms_bwd(dhn, t["h0"], ln_mix[l], dh, f"rms_mix_bwd_{l}")
        small[("ln_mix", l)] = dg
    grad_x = dh[None]

    gparts = []
    for key, r in pieces:
        g = gw[key]
        if key[0] == "ple_w_proj":
            g = g.reshape(NDEV, -1, d)
        else:
            g = g.reshape(NDEV, -1, g.shape[-1])
        padr = r - g.shape[1]
        if padr:
            g = jnp.pad(g, ((0, 0), (0, padr), (0, 0)))
        gparts.append(g)
    gpack = jnp.concatenate(gparts, axis=1).reshape(NDEV * r_pack, d)
    recv = _exchange(gpack, "comm_scatter_grads").reshape(NDEV, r_pack, d)
    gsum = _sum_slots(recv, "grad_sum")
    gshard = {}
    off = 0
    for key, r in pieces:
        gshard[key] = gsum[off:off + r]
        off += r

    def col_back(key, n_valid):
        return jnp.transpose(gshard[key][:n_valid])

    g_gdn_w_in = jnp.stack([col_back(("gdn_w_in", l), win_cols) for l in range(n_a)])
    g_gdn_w_out = jnp.stack([gshard[("gdn_w_out", l)] for l in range(n_a)])
    g_w_kv = col_back(("w_kv", 0), w_kv.shape[1])
    g_sb_w_q = jnp.stack([gshard[("sb_w_q", j)] for j in range(n_b)])
    g_sb_w_out = jnp.stack([gshard[("sb_w_out", j)] for j in range(n_b)])
    g_ffn_w_in = jnp.stack([col_back(("ffn_w_in", l), ffn_w_in.shape[2]) for l in range(depth)])
    g_ffn_w_out = jnp.stack([gshard[("ffn_w_out", l)] for l in range(depth)])
    g_ple_w_proj = jnp.stack([jnp.transpose(gshard[("ple_w_proj", l)].reshape(-1, pd)) for l in range(depth)])
    g_ple_w_gate = jnp.stack([gshard[("ple_w_gate", l)] for l in range(depth)])

    def vec_rows(v):
        return v.reshape(-1, HEAD)

    small_items = []
    for name_, cnt in (("ln_mix", depth), ("ln_ffn", depth), ("ln_ple", depth)):
        for l in range(cnt):
            small_items.append(((name_, l), vec_rows(small[(name_, l)])))
    for l in range(n_a):
        small_items.append((("gdn_conv", l), small[("gdn_conv", l)].reshape(-1, HEAD)))
        small_items.append((("gdn_a_log", l), small[("gdn_a_log", l)]))
        small_items.append((("gdn_dt_bias", l), small[("gdn_dt_bias", l)]))
        small_items.append((("gdn_norm", l), small[("gdn_norm", l)]))
    small_items.append(("kv_norm", vec_rows(small["kv_norm"])))
    small_items.append(("k_norm", small["k_norm"]))
    for j in range(n_b):
        small_items.append((("sb_q_norm", j), small[("sb_q_norm", j)]))
    spack = jnp.concatenate([_pad_rows(a, 8) for _, a in small_items], axis=0)
    sg = _all_gather(spack, "comm_gather_small").reshape(NDEV, spack.shape[0], HEAD)
    ssum = _sum_slots(sg, "small_sum")
    sm = {}
    off = 0
    for key, a in small_items:
        sm[key] = ssum[off:off + a.shape[0]]
        off += a.shape[0] + (-a.shape[0]) % 8

    g_ln_mix = jnp.stack([sm[("ln_mix", l)].reshape(d) for l in range(depth)])
    g_ln_ffn = jnp.stack([sm[("ln_ffn", l)].reshape(d) for l in range(depth)])
    g_ln_ple = jnp.stack([sm[("ln_ple", l)].reshape(d) for l in range(depth)])
    conv_loc = gdn_conv.shape[2]
    g_conv_full = jnp.stack([sm[("gdn_conv", l)].reshape(gdn_conv.shape[1], 3 * d) for l in range(n_a)])
    g_gdn_conv = lax.dynamic_slice_in_dim(g_conv_full, me * conv_loc, conv_loc, axis=2)
    g_a_log = jnp.stack([sm[("gdn_a_log", l)][0, :nh] for l in range(n_a)])
    g_dt_bias = jnp.stack([sm[("gdn_dt_bias", l)][0, :nh] for l in range(n_a)])
    g_gdn_norm = jnp.stack([sm[("gdn_norm", l)][0] for l in range(n_a)])
    g_kv_norm = sm["kv_norm"].reshape(d)
    g_k_norm = sm["k_norm"][0]
    g_sb_q_norm = jnp.stack([sm[("sb_q_norm", j)][0] for j in range(n_b)])

    grads = [g_ln_mix, g_ln_ffn, g_ln_ple, g_gdn_w_in, g_gdn_conv, g_a_log, g_dt_bias, g_gdn_norm, g_gdn_w_out,
             g_kv_norm, g_w_kv, g_k_norm, g_sb_w_q, g_sb_q_norm, g_sb_w_out, g_ffn_w_in, g_ffn_w_out, g_ple_w_proj,
             g_ple_w_gate]
    weights = [ln_mix, ln_ffn, ln_ple, gdn_w_in, gdn_conv, gdn_a_log, gdn_dt_bias, gdn_norm, gdn_w_out, kv_norm, w_kv,
               k_norm, sb_w_q, sb_q_norm, sb_w_out, ffn_w_in, ffn_w_out, ple_w_proj, ple_w_gate]
    moms = [m_ln_mix, m_ln_ffn, m_ln_ple, m_gdn_w_in, m_gdn_conv, m_gdn_a_log, m_gdn_dt_bias, m_gdn_norm, m_gdn_w_out,
            m_kv_norm, m_w_kv, m_k_norm, m_sb_w_q, m_sb_q_norm, m_sb_w_out, m_ffn_w_in, m_ffn_w_out, m_ple_w_proj,
            m_ple_w_gate]
    vels = [v_ln_mix, v_ln_ffn, v_ln_ple, v_gdn_w_in, v_gdn_conv, v_gdn_a_log, v_gdn_dt_bias, v_gdn_norm, v_gdn_w_out,
            v_kv_norm, v_w_kv, v_k_norm, v_sb_w_q, v_sb_q_norm, v_sb_w_out, v_ffn_w_in, v_ffn_w_out, v_ple_w_proj,
            v_ple_w_gate]

    deltas, new_m, new_v = [], [], []
    small_idx = [i for i, w in enumerate(weights) if w.size < 8 * HEAD * 16]
    for i, (w, g, m, v) in enumerate(zip(weights, grads, moms, vels)):
        if i in small_idx:
            deltas.append(None), new_m.append(None), new_v.append(None)
            continue
        shp = w.shape
        two = lambda a: a.reshape(-1, shp[-1])
        dl, nm, nv = _adamw(two(w), two(g), two(m), two(v), f"adamw_{i}")
        deltas.append(dl.reshape(shp)), new_m.append(nm.reshape(shp)), new_v.append(nv.reshape(shp))

    def flat_pack(arrs):
        flat = jnp.concatenate([a.reshape(-1) for a in arrs])
        pad = (-flat.shape[0]) % (8 * HEAD)
        return jnp.pad(flat, (0, pad)).reshape(-1, HEAD)

    sw = flat_pack([weights[i] for i in small_idx])
    sgr = flat_pack([grads[i] for i in small_idx])
    smo = flat_pack([moms[i] for i in small_idx])
    sve = flat_pack([vels[i] for i in small_idx])
    sdl, snm, snv = _adamw(sw, sgr, smo, sve, "adamw_small")
    off = 0
    for i in small_idx:
        n = weights[i].size
        shp = weights[i].shape
        deltas[i] = sdl.reshape(-1)[off:off + n].reshape(shp)
        new_m[i] = snm.reshape(-1)[off:off + n].reshape(shp)
        new_v[i] = snv.reshape(-1)[off:off + n].reshape(shp)
        off += n

    return (loss, grad_x, *grads, *deltas, *new_m, *new_v)
```

```python
import functools
import math

import jax
import jax.numpy as jnp
from jax import lax
from jax.experimental import pallas as pl
from jax.experimental.pallas import tpu as pltpu

F32 = jnp.float32
BF16 = jnp.bfloat16
NDEV = 8
HEAD = 128
CHUNK = 64
SBLK = 128
EPS = 1e-6
LR, B1, B2, ADAM_EPS, WD, STEP = 0.001, 0.9, 0.999, 1e-08, 0.01, 10
NEG = -1e30
MM_VMEM_BUDGET = 36 * 1024 * 1024

NN = (((1,), (0,)), ((), ()))
NT = (((1,), (1,)), ((), ()))
TN = (((0,), (0,)), ((), ()))
MESH = pl.DeviceIdType.MESH


def _dot(a, b, dims=NN):
    return lax.dot_general(a.astype(BF16), b.astype(BF16), dims, preferred_element_type=F32)


def _dot_hilo(a, b01, dims=NN):
    hi = a.astype(BF16)
    lo = (a - hi.astype(F32)).astype(BF16)
    return (lax.dot_general(hi, b01, dims, preferred_element_type=F32)
            + lax.dot_general(lo, b01, dims, preferred_element_type=F32))


def _pick(dim, cands):
    for c in cands:
        if dim % c == 0:
            return c
    return dim


def _params(sem, vmem_mb=48):
    return pltpu.CompilerParams(dimension_semantics=sem, vmem_limit_bytes=vmem_mb * 1024 * 1024)


def _silu(x):
    return x * jax.nn.sigmoid(x)


def _dsilu(x):
    s = jax.nn.sigmoid(x)
    return s * (1.0 + x * (1.0 - s))


def _mm(a, b, mode, name, out_dtype=F32, res=None):
    if mode == "nn":
        (m, k), n = a.shape, b.shape[1]
    elif mode == "nt":
        (m, k), n = a.shape, b.shape[0]
    else:
        (k, m), n = a.shape, b.shape[1]
    tn = _pick(n, (512, 256, 128))
    tk = k if k <= 4096 else _pick(k, (2048, 1024, 512, 256, 128))
    nk = k // tk
    out_b = jnp.dtype(out_dtype).itemsize + (res.dtype.itemsize if res is not None else 0)
    for tm in (1024, 512, 256, 128, m):
        need = 2 * (tm * tk * a.dtype.itemsize + tk * tn * b.dtype.itemsize + tm * tn * out_b) + 4 * tm * tn
        if m % tm == 0 and need <= MM_VMEM_BUDGET:
            break
    dims = {"nn": NN, "nt": NT, "tn": TN}[mode]
    if mode == "tn":
        a_spec = pl.BlockSpec((tk, tm), lambda i, j, kk: (kk, i))
    else:
        a_spec = pl.BlockSpec((tm, tk), lambda i, j, kk: (i, kk))
    if mode == "nt":
        b_spec = pl.BlockSpec((tn, tk), lambda i, j, kk: (j, kk))
    else:
        b_spec = pl.BlockSpec((tk, tn), lambda i, j, kk: (kk, j))
    mn_spec = pl.BlockSpec((tm, tn), lambda i, j, kk: (i, j))
    has_res = res is not None

    def body(*refs):
        if has_res:
            a_ref, b_ref, r_ref, o_ref, acc = refs
        else:
            a_ref, b_ref, o_ref, acc = refs
        kk = pl.program_id(2)

        @pl.when(kk == 0)
        def _():
            acc[...] = jnp.zeros_like(acc)

        acc[...] += _dot(a_ref[...], b_ref[...], dims)

        @pl.when(kk == nk - 1)
        def _():
            r = acc[...]
            if has_res:
                r = r + r_ref[...].astype(F32)
            o_ref[...] = r.astype(out_dtype)

    ins = [a, b] + ([res] if has_res else [])
    in_specs = [a_spec, b_spec] + ([mn_spec] if has_res else [])
    return pl.pallas_call(
        body, name=name, grid=(m // tm, n // tn, nk), in_specs=in_specs, out_specs=mn_spec,
        out_shape=jax.ShapeDtypeStruct((m, n), out_dtype), scratch_shapes=[pltpu.VMEM((tm, tn), F32)],
        compiler_params=_params(("parallel", "parallel", "arbitrary")))(*ins)


def _rms_fwd(h, g, name):
    s, d = h.shape
    tm = _pick(s, (512, 256, 128))

    def body(h_ref, g_ref, o_ref):
        x = h_ref[...]
        r = lax.rsqrt(jnp.mean(x * x, axis=-1, keepdims=True) + EPS)
        o_ref[...] = (x * r * g_ref[...]).astype(BF16)

    return pl.pallas_call(
        body, name=name, grid=(s // tm,),
        in_specs=[pl.BlockSpec((tm, d), lambda i: (i, 0)), pl.BlockSpec((1, d), lambda i: (0, 0))],
        out_specs=pl.BlockSpec((tm, d), lambda i: (i, 0)),
        out_shape=jax.ShapeDtypeStruct((s, d), BF16), compiler_params=_params(("parallel",)))(h, g.reshape(1, d))


def _rms_bwd(dy, x, g, dres, name):
    s, d = x.shape
    tm = _pick(s, (512, 256, 128))

    def body(dy_ref, x_ref, g_ref, dr_ref, dx_ref, dg_ref):
        @pl.when(pl.program_id(0) == 0)
        def _():
            dg_ref[...] = jnp.zeros_like(dg_ref)

        xv = x_ref[...]
        dyv = dy_ref[...].astype(F32)
        r = lax.rsqrt(jnp.mean(xv * xv, axis=-1, keepdims=True) + EPS)
        gdy = dyv * g_ref[...]
        mean_t = jnp.mean(xv * gdy, axis=-1, keepdims=True)
        dx_ref[...] = dr_ref[...] + r * gdy - xv * (r * r * r) * mean_t
        dg_ref[...] += jnp.sum(dyv * xv * r, axis=0, keepdims=True)

    row = pl.BlockSpec((tm, d), lambda i: (i, 0))
    vec = pl.BlockSpec((1, d), lambda i: (0, 0))
    return pl.pallas_call(
        body, name=name, grid=(s // tm,), in_specs=[row, row, vec, row], out_specs=[row, vec],
        out_shape=[jax.ShapeDtypeStruct((s, d), F32), jax.ShapeDtypeStruct((1, d), F32)],
        compiler_params=_params(("arbitrary",)))(dy, x, g.reshape(1, d), dres)


def _headnorm_fwd(x, g, name, scale=1.0, gate=None, gate_col0=0, out_dtype=BF16, width=None):
    s, d = x.shape[0], (width or x.shape[1])
    nh = d // HEAD
    tm = _pick(s, (256, 128))
    has_gate = gate is not None
    gb = gate_col0 // d

    def body(*refs):
        if has_gate:
            x_ref, g_ref, gt_ref, o_ref = refs
        else:
            x_ref, g_ref, o_ref = refs
        gv = g_ref[...]
        for h in range(nh):
            sl = slice(h * HEAD, (h + 1) * HEAD)
            xv = x_ref[:, sl].astype(F32)
            r = lax.rsqrt(jnp.mean(xv * xv, axis=-1, keepdims=True) + EPS)
            y = xv * r * gv
            if scale != 1.0:
                y = y * scale
            if has_gate:
                y = y * _silu(gt_ref[:, sl])
            o_ref[:, sl] = y.astype(out_dtype)

    row = pl.BlockSpec((tm, d), lambda i: (i, 0))
    ins = [x, g.reshape(1, HEAD)]
    in_specs = [row, pl.BlockSpec((1, HEAD), lambda i: (0, 0))]
    if has_gate:
        ins.append(gate)
        in_specs.append(pl.BlockSpec((tm, d), lambda i: (i, gb)))
    return pl.pallas_call(
        body, name=name, grid=(s // tm,), in_specs=in_specs, out_specs=row,
        out_shape=jax.ShapeDtypeStruct((s, d), out_dtype), compiler_params=_params(("parallel",)))(*ins)


def _headnorm_bwd(dy, x, g, name, scale=1.0, gate=None, gate_col0=0, dx_dtype=F32):
    s, d = dy.shape
    nh = d // HEAD
    tm = _pick(s, (256, 128))
    has_gate = gate is not None
    gb = gate_col0 // d

    def body(*refs):
        if has_gate:
            dy_ref, x_ref, g_ref, gt_ref, dx_ref, dg_ref, dgt_ref = refs
        else:
            dy_ref, x_ref, g_ref, dx_ref, dg_ref = refs

        @pl.when(pl.program_id(0) == 0)
        def _():
            dg_ref[...] = jnp.zeros_like(dg_ref)

        gv = g_ref[...]
        dg_acc = jnp.zeros((1, HEAD), F32)
        for h in range(nh):
            sl = slice(h * HEAD, (h + 1) * HEAD)
            xv = x_ref[:, sl].astype(F32)
            dyv = dy_ref[:, sl].astype(F32)
            r = lax.rsqrt(jnp.mean(xv * xv, axis=-1, keepdims=True) + EPS)
            if has_gate:
                gt = gt_ref[:, sl]
                dgt_ref[:, sl] = (dyv * (xv * r * gv) * _dsilu(gt)).astype(dgt_ref.dtype)
                dn = dyv * _silu(gt)
            else:
                dn = dyv
            if scale != 1.0:
                dn = dn * scale
            gdn = dn * gv
            mean_t = jnp.mean(xv * gdn, axis=-1, keepdims=True)
            dx_ref[:, sl] = (r * gdn - xv * (r * r * r) * mean_t).astype(dx_dtype)
            dg_acc = dg_acc + jnp.sum(dn * xv * r, axis=0, keepdims=True)
        dg_ref[...] += dg_acc

    row = pl.BlockSpec((tm, d), lambda i: (i, 0))
    vec = pl.BlockSpec((1, HEAD), lambda i: (0, 0))
    ins = [dy, x, g.reshape(1, HEAD)]
    in_specs = [row, row, vec]
    out_specs = [row, vec]
    out_shape = [jax.ShapeDtypeStruct((s, d), dx_dtype), jax.ShapeDtypeStruct((1, HEAD), F32)]
    if has_gate:
        ins.append(gate)
        in_specs.append(pl.BlockSpec((tm, d), lambda i: (i, gb)))
        out_specs.append(row)
        out_shape.append(jax.ShapeDtypeStruct((s, d), BF16))
    return pl.pallas_call(
        body, name=name, grid=(s // tm,), in_specs=in_specs, out_specs=out_specs, out_shape=out_shape,
        compiler_params=_params(("arbitrary",)))(*ins)


def _swiglu_fwd(hn, wf_t, name):
    s, d = hn.shape
    f = wf_t.shape[0] // 2
    tm = _pick(s, (512, 256, 128))
    tn = _pick(f, (512, 256, 128))
    nj = f // tn

    def body(a_ref, wg_ref, wu_ref, act_ref, g_ref, u_ref):
        a = a_ref[...]
        g = _dot(a, wg_ref[...], NT)
        u = _dot(a, wu_ref[...], NT)
        act_ref[...] = (_silu(g) * u).astype(BF16)
        g_ref[...] = g.astype(BF16)
        u_ref[...] = u.astype(BF16)

    o_spec = pl.BlockSpec((tm, tn), lambda i, j: (i, j))
    sds = jax.ShapeDtypeStruct((s, f), BF16)
    return pl.pallas_call(
        body, name=name, grid=(s // tm, nj),
        in_specs=[pl.BlockSpec((tm, d), lambda i, j: (i, 0)), pl.BlockSpec((tn, d), lambda i, j: (j, 0)),
                  pl.BlockSpec((tn, d), lambda i, j: (j + nj, 0))],
        out_specs=[o_spec, o_spec, o_spec], out_shape=[sds, sds, sds],
        compiler_params=_params(("parallel", "parallel")))(hn, wf_t, wf_t)


def _swiglu_bwd(dh, w_out, g, u, name):
    s, d = dh.shape
    f = w_out.shape[0]
    tm = _pick(s, (512, 256, 128))
    tn = _pick(f, (512, 256, 128))

    def body(dh_ref, w_ref, g_ref, u_ref, dg_ref, du_ref):
        dact = _dot(dh_ref[...], w_ref[...], NT)
        gv = g_ref[...].astype(F32)
        uv = u_ref[...].astype(F32)
        dg_ref[...] = (dact * uv * _dsilu(gv)).astype(BF16)
        du_ref[...] = (dact * _silu(gv)).astype(BF16)

    o_spec = pl.BlockSpec((tm, tn), lambda i, j: (i, j))
    sds = jax.ShapeDtypeStruct((s, f), BF16)
    return pl.pallas_call(
        body, name=name, grid=(s // tm, f // tn),
        in_specs=[pl.BlockSpec((tm, d), lambda i, j: (i, 0)), pl.BlockSpec((tn, d), lambda i, j: (j, 0)), o_spec, o_spec],
        out_specs=[o_spec, o_spec], out_shape=[sds, sds],
        compiler_params=_params(("parallel", "parallel")))(dh, w_out, g, u)


def _ple_fwd(h, hn, p, w_gate, wp_t, name):
    s, d = h.shape
    pd = p.shape[1]
    tm = _pick(s, (512, 256, 128))
    tn = _pick(d, (512, 256, 128))

    def body(h_ref, hn_ref, p_ref, wg_ref, wp_ref, o_ref, gp_ref, pp_ref):
        gpre = _dot(hn_ref[...], wg_ref[...], NN)
        pp = _dot(p_ref[...], wp_ref[...], NT)
        o_ref[...] = h_ref[...] + pp * jax.nn.sigmoid(gpre)
        gp_ref[...] = gpre.astype(BF16)
        pp_ref[...] = pp.astype(BF16)

    mn = pl.BlockSpec((tm, tn), lambda i, j: (i, j))
    return pl.pallas_call(
        body, name=name, grid=(s // tm, d // tn),
        in_specs=[mn, pl.BlockSpec((tm, d), lambda i, j: (i, 0)), pl.BlockSpec((tm, pd), lambda i, j: (i, 0)),
                  pl.BlockSpec((d, tn), lambda i, j: (0, j)), pl.BlockSpec((tn, pd), lambda i, j: (j, 0))],
        out_specs=[mn, mn, mn],
        out_shape=[jax.ShapeDtypeStruct((s, d), F32), jax.ShapeDtypeStruct((s, d), BF16), jax.ShapeDtypeStruct((s, d), BF16)],
        compiler_params=_params(("parallel", "parallel")))(h, hn, p, w_gate, wp_t)


def _ple_bwd(dh, gpre, pp, name):
    s, d = dh.shape
    tm = _pick(s, (512, 256, 128))

    def body(dh_ref, gp_ref, pp_ref, dgp_ref, dpp_ref):
        dv = dh_ref[...]
        sig = jax.nn.sigmoid(gp_ref[...].astype(F32))
        ppv = pp_ref[...].astype(F32)
        dpp_ref[...] = (dv * sig).astype(BF16)
        dgp_ref[...] = (dv * ppv * sig * (1.0 - sig)).astype(BF16)

    row = pl.BlockSpec((tm, d), lambda i: (i, 0))
    sds = jax.ShapeDtypeStruct((s, d), BF16)
    return pl.pallas_call(
        body, name=name, grid=(s // tm,), in_specs=[row, row, row], out_specs=[row, row], out_shape=[sds, sds],
        compiler_params=_params(("parallel",)))(dh, gpre, pp)


def _loss_fwd_bwd(y, t, name):
    s, d = y.shape
    tm = _pick(s, (512, 256, 128))

    def body(y_ref, t_ref, dy_ref, l_ref):
        @pl.when(pl.program_id(0) == 0)
        def _():
            l_ref[...] = jnp.zeros_like(l_ref)

        e = y_ref[...] - t_ref[...]
        dy_ref[...] = e * (1.0 / d)
        l_ref[...] += jnp.sum(e * e, axis=0, keepdims=True) * (0.5 / d)

    row = pl.BlockSpec((tm, d), lambda i: (i, 0))
    vec = pl.BlockSpec((1, d), lambda i: (0, 0))
    return pl.pallas_call(
        body, name=name, grid=(s // tm,), in_specs=[row, row], out_specs=[row, vec],
        out_shape=[jax.ShapeDtypeStruct((s, d), F32), jax.ShapeDtypeStruct((1, d), F32)],
        compiler_params=_params(("arbitrary",)))(y, t)


PADR = 8


def _conv_fwd(proj, w_conv, d, name):
    s = proj.shape[0]
    nh = d // HEAD
    kw = w_conv.shape[0]
    qscale = HEAD ** -0.5

    def body(x_ref, w_ref, o_ref, xp):
        kind = pl.program_id(0) // nh
        xp[0:PADR, :] = jnp.zeros((PADR, HEAD), F32)
        xp[PADR:, :] = x_ref[...]
        acc = jnp.zeros((s, HEAD), F32)
        for j in range(kw):
            acc = acc + w_ref[j:j + 1, :] * xp[PADR - (kw - 1) + j:PADR - (kw - 1) + j + s, :]
        a = _silu(acc)
        r = lax.rsqrt(jnp.sum(a * a, axis=-1, keepdims=True) + EPS)
        fac = jnp.where(kind == 0, r * qscale, jnp.where(kind == 1, r, jnp.ones_like(r)))
        o_ref[...] = a * fac

    blk = pl.BlockSpec((s, HEAD), lambda c: (0, c))
    return pl.pallas_call(
        body, name=name, grid=(3 * nh,), in_specs=[blk, pl.BlockSpec((kw, HEAD), lambda c: (0, c))], out_specs=blk,
        out_shape=jax.ShapeDtypeStruct((s, 3 * d), F32), scratch_shapes=[pltpu.VMEM((s + PADR, HEAD), F32)],
        compiler_params=_params(("parallel",)))(proj, w_conv)


def _conv_bwd(dqkv, proj, w_conv, d, name):
    s = proj.shape[0]
    nh = d // HEAD
    kw = w_conv.shape[0]
    qscale = HEAD ** -0.5

    def body(dy_ref, x_ref, w_ref, dx_ref, dw_ref, xp, dp):
        kind = pl.program_id(0) // nh
        xp[0:PADR, :] = jnp.zeros((PADR, HEAD), F32)
        xp[PADR:, :] = x_ref[...]
        acc = jnp.zeros((s, HEAD), F32)
        for j in range(kw):
            acc = acc + w_ref[j:j + 1, :] * xp[PADR - (kw - 1) + j:PADR - (kw - 1) + j + s, :]
        a = _silu(acc)
        dy = dy_ref[...]
        r = lax.rsqrt(jnp.sum(a * a, axis=-1, keepdims=True) + EPS)
        sc = jnp.where(kind == 0, qscale, 1.0)
        dyn = dy * sc
        da_norm = r * dyn - a * (r * r * r) * jnp.sum(a * dyn, axis=-1, keepdims=True)
        da = jnp.where(kind == 2, dy, da_norm)
        dacc = da * _dsilu(acc)
        dp[0:s, :] = dacc
        dp[s:, :] = jnp.zeros((PADR, HEAD), F32)
        dx = jnp.zeros((s, HEAD), F32)
        for j in range(kw):
            sh = kw - 1 - j
            dx = dx + w_ref[j:j + 1, :] * dp[sh:sh + s, :]
            dw_ref[j:j + 1, :] = jnp.sum(dacc * xp[PADR - sh:PADR - sh + s, :], axis=0, keepdims=True)
        dx_ref[...] = dx.astype(BF16)

    blk = pl.BlockSpec((s, HEAD), lambda c: (0, c))
    wblk = pl.BlockSpec((kw, HEAD), lambda c: (0, c))
    return pl.pallas_call(
        body, name=name, grid=(3 * nh,), in_specs=[blk, blk, wblk], out_specs=[blk, wblk],
        out_shape=[jax.ShapeDtypeStruct((s, 3 * d), BF16), jax.ShapeDtypeStruct((kw, 3 * d), F32)],
        scratch_shapes=[pltpu.VMEM((s + PADR, HEAD), F32), pltpu.VMEM((s + PADR, HEAD), F32)],
        compiler_params=_params(("parallel",)))(dqkv, proj, w_conv)


def _softplus(x):
    return jnp.maximum(x, 0.0) + jnp.log(1.0 + jnp.exp(-jnp.abs(x)))


def _gates_fwd(pab, a_log, dt_bias, nh, name):
    s = pab.shape[0]
    tm = _pick(s, (512, 256, 128))

    def body(x_ref, al_ref, dt_ref, o_ref):
        x = x_ref[...]
        lane = lax.broadcasted_iota(jnp.int32, x.shape, 1)
        g = -jnp.exp(al_ref[...]) * _softplus(x + dt_ref[...])
        o_ref[...] = jnp.where(lane < nh, g, jnp.where(lane < 2 * nh, jax.nn.sigmoid(x), 0.0))

    row = pl.BlockSpec((tm, HEAD), lambda i: (i, 0))
    vec = pl.BlockSpec((1, HEAD), lambda i: (0, 0))
    return pl.pallas_call(
        body, name=name, grid=(s // tm,), in_specs=[row, vec, vec], out_specs=row,
        out_shape=jax.ShapeDtypeStruct((s, HEAD), F32), compiler_params=_params(("parallel",)))(pab, a_log, dt_bias)


def _gates_bwd(dgb, pab, a_log, dt_bias, nh, name):
    s = pab.shape[0]
    tm = _pick(s, (512, 256, 128))

    def body(d_ref, x_ref, al_ref, dt_ref, dx_ref, dal_ref, ddt_ref):
        @pl.when(pl.program_id(0) == 0)
        def _():
            dal_ref[...] = jnp.zeros_like(dal_ref)
            ddt_ref[...] = jnp.zeros_like(ddt_ref)

        x = x_ref[...]
        dv = d_ref[...]
        lane = lax.broadcasted_iota(jnp.int32, x.shape, 1)
        ea = jnp.exp(al_ref[...])
        xs = x + dt_ref[...]
        g = -ea * _softplus(xs)
        dxs = jnp.where(lane < nh, dv * (-ea) * jax.nn.sigmoid(xs), 0.0)
        sg = jax.nn.sigmoid(x)
        dxb = jnp.where((lane >= nh) & (lane < 2 * nh), dv * sg * (1.0 - sg), 0.0)
        dx_ref[...] = (dxs + dxb).astype(BF16)
        dal_ref[...] += jnp.sum(jnp.where(lane < nh, dv * g, 0.0), axis=0, keepdims=True)
        ddt_ref[...] += jnp.sum(dxs, axis=0, keepdims=True)

    row = pl.BlockSpec((tm, HEAD), lambda i: (i, 0))
    vec = pl.BlockSpec((1, HEAD), lambda i: (0, 0))
    return pl.pallas_call(
        body, name=name, grid=(s // tm,), in_specs=[row, row, vec, vec], out_specs=[row, vec, vec],
        out_shape=[jax.ShapeDtypeStruct((s, HEAD), BF16), jax.ShapeDtypeStruct((1, HEAD), F32),
                   jax.ShapeDtypeStruct((1, HEAD), F32)],
        compiler_params=_params(("arbitrary",)))(dgb, pab, a_log, dt_bias)


def _tri_inv(a_low, eye_f):
    n = -a_low
    p = eye_f + n
    steps = int(math.log2(a_low.shape[0])) - 1
    for _ in range(steps):
        n = _dot(n, n)
        p = p + _dot(p, n)
    return p


def _lane_col(x, lane, idx):
    return jnp.sum(jnp.where(lane == idx, x, 0.0), axis=1, keepdims=True)


def _gdn_chunk(qh, kh, vh, g_col, beta_col, st):
    c = qh.shape[0]
    r_i = lax.broadcasted_iota(jnp.int32, (c, c), 0)
    c_i = lax.broadcasted_iota(jnp.int32, (c, c), 1)
    incl = c_i <= r_i
    strict = c_i < r_i
    eye = c_i == r_i
    g_row = jnp.sum(jnp.where(eye, g_col, 0.0), axis=0, keepdims=True)
    gc_col = jnp.sum(jnp.where(incl, g_row, 0.0), axis=1, keepdims=True)
    gc_row = jnp.sum(jnp.where(eye, gc_col, 0.0), axis=0, keepdims=True)
    g_last = jnp.sum(g_col, axis=0, keepdims=True)
    decay = jnp.exp(jnp.where(incl, gc_col - gc_row, NEG))
    kk = _dot(kh, kh, NT)
    a_low = jnp.where(strict, beta_col * kk * decay, 0.0)
    t_inv = _tri_inv(a_low, eye.astype(F32))
    e_g = jnp.exp(gc_col)
    bk = beta_col * e_g
    rhs = jnp.concatenate([vh * beta_col, kh * bk], axis=1)
    sol = _dot(t_inv, rhs)
    u, w = sol[:, :HEAD], sol[:, HEAD:]
    qk_raw = _dot(qh, kh, NT)
    qk = qk_raw * decay
    q_dec = qh * e_g
    e2 = jnp.exp(g_last - gc_col)
    k_dec = kh * e2
    gl = jnp.exp(g_last)
    ws = _dot(jnp.concatenate([w, q_dec], axis=0), st)
    v_new = u - ws[:c]
    o = ws[c:] + _dot(qk, v_new)
    st_new = st * gl + _dot(k_dec, v_new, TN)
    inter = dict(incl=incl, strict=strict, eye=eye, decay=decay, kk=kk, t_inv=t_inv, e_g=e_g, bk=bk, sol=sol, w=w,
                 qk_raw=qk_raw, qk=qk, q_dec=q_dec, e2=e2, k_dec=k_dec, gl=gl, v_new=v_new, c_i=c_i, r_i=r_i)
    return o, st_new, inter


def _gdn_fwd(qkv, gb, d, name):
    s = qkv.shape[0]
    nh = d // HEAD
    nc = s // CHUNK

    def body(q_ref, k_ref, v_ref, gb_ref, o_ref, st_ref, state):
        @pl.when(pl.program_id(0) == 0)
        def _():
            state[...] = jnp.zeros_like(state)

        gbv = gb_ref[...]
        lane = lax.broadcasted_iota(jnp.int32, gbv.shape, 1)
        st_ref[...] = state[...]
        for h in range(nh):
            sl = slice(h * HEAD, (h + 1) * HEAD)
            g_col = _lane_col(gbv, lane, h)
            beta_col = _lane_col(gbv, lane, nh + h)
            o, st_new, _ = _gdn_chunk(q_ref[:, sl], k_ref[:, sl], v_ref[:, sl], g_col, beta_col, state[sl, :])
            o_ref[:, sl] = o
            state[sl, :] = st_new

    def qspec(part):
        return pl.BlockSpec((CHUNK, d), lambda n: (n, part))

    return pl.pallas_call(
        body, name=name, grid=(nc,),
        in_specs=[qspec(0), qspec(1), qspec(2), pl.BlockSpec((CHUNK, HEAD), lambda n: (n, 0))],
        out_specs=[pl.BlockSpec((CHUNK, d), lambda n: (n, 0)), pl.BlockSpec((None, d, HEAD), lambda n: (n, 0, 0))],
        out_shape=[jax.ShapeDtypeStruct((s, d), F32), jax.ShapeDtypeStruct((nc, d, HEAD), F32)],
        scratch_shapes=[pltpu.VMEM((d, HEAD), F32)],
        compiler_params=_params(("arbitrary",)))(qkv, qkv, qkv, gb)


def _gdn_bwd(qkv, gb, do, states, d, name):
    s = qkv.shape[0]
    nh = d // HEAD
    nc = s // CHUNK
    c = CHUNK

    def body(q_ref, k_ref, v_ref, gb_ref, do_ref, st_ref, dq_ref, dk_ref, dv_ref, dgb_ref, dstate):
        @pl.when(pl.program_id(0) == 0)
        def _():
            dstate[...] = jnp.zeros_like(dstate)

        gbv = gb_ref[...]
        lane = lax.broadcasted_iota(jnp.int32, gbv.shape, 1)
        dgb_acc = jnp.zeros(gbv.shape, F32)
        for h in range(nh):
            sl = slice(h * HEAD, (h + 1) * HEAD)
            qh, kh, vh = q_ref[:, sl], k_ref[:, sl], v_ref[:, sl]
            g_col = _lane_col(gbv, lane, h)
            beta_col = _lane_col(gbv, lane, nh + h)
            st = st_ref[sl, :]
            dst = dstate[sl, :]
            doh = do_ref[:, sl]
            _, _, it = _gdn_chunk(qh, kh, vh, g_col, beta_col, st)
            incl, strict, eye, decay = it["incl"], it["strict"], it["eye"], it["decay"]
            u_w = it["sol"]
            dv_new = _dot(it["qk"], doh, TN) + _dot(it["k_dec"], dst)
            d_qk = _dot(doh, it["v_new"], NT)
            dd = _dot(jnp.concatenate([doh, -dv_new], axis=0), st, NT)
            dq_dec, dw = dd[:c], dd[c:]
            dst_new = _dot(it["q_dec"], doh, TN) + it["gl"] * dst - _dot(it["w"], dv_new, TN)
            dgl = jnp.sum(jnp.sum(dst * st, axis=1, keepdims=True), axis=0, keepdims=True)
            dk_dec = _dot(it["v_new"], dst, NT)
            dsol = jnp.concatenate([dv_new, dw], axis=1)
            drhs = _dot(it["t_inv"], dsol, TN)
            d_a = jnp.where(strict, -_dot(drhs, u_w, NT), 0.0)
            drhs_u, drhs_w = drhs[:, :HEAD], drhs[:, HEAD:]
            dvh = beta_col * drhs_u
            rw_k = jnp.sum(drhs_w * kh, axis=1, keepdims=True)
            dbeta = jnp.sum(drhs_u * vh, axis=1, keepdims=True) + it["e_g"] * rw_k
            dkh = it["bk"] * drhs_w
            dgc_col = it["bk"] * rw_k
            dkk = d_a * beta_col * decay
            dbeta = dbeta + jnp.sum(d_a * it["kk"] * decay, axis=1, keepdims=True)
            ddecay = d_a * beta_col * it["kk"]
            dkh = dkh + _dot(dkk, kh) + _dot(dkk, kh, TN)
            dqk_raw = d_qk * decay
            ddecay = ddecay + d_qk * it["qk_raw"]
            dqh = _dot(dqk_raw, kh)
            dkh = dkh + _dot(dqk_raw, qh, TN)
            ddm = jnp.where(incl, ddecay * decay, 0.0)
            dgc_col = dgc_col + jnp.sum(ddm, axis=1, keepdims=True)
            dgc_row = -jnp.sum(ddm, axis=0, keepdims=True)
            dqh = dqh + dq_dec * it["e_g"]
            dgc_col = dgc_col + jnp.sum(dq_dec * it["q_dec"], axis=1, keepdims=True)
            dkh = dkh + dk_dec * it["e2"]
            tmp = jnp.sum(dk_dec * it["k_dec"], axis=1, keepdims=True)
            dgc_col = dgc_col - tmp
            dg_last = jnp.sum(tmp, axis=0, keepdims=True) + dgl * it["gl"]
            dgc_tot_row = dgc_row + jnp.sum(jnp.where(eye, dgc_col, 0.0), axis=0, keepdims=True)
            dg_col = jnp.sum(jnp.where(it["c_i"] >= it["r_i"], dgc_tot_row, 0.0), axis=1, keepdims=True) + dg_last
            dq_ref[:, sl] = dqh
            dk_ref[:, sl] = dkh
            dv_ref[:, sl] = dvh
            dstate[sl, :] = dst_new
            dgb_acc = jnp.where(lane == h, dg_col, jnp.where(lane == nh + h, dbeta, dgb_acc))
        dgb_ref[...] = dgb_acc

    def rev(part):
        return pl.BlockSpec((CHUNK, d), lambda n: (nc - 1 - n, part))

    gspec = pl.BlockSpec((CHUNK, HEAD), lambda n: (nc - 1 - n, 0))
    sds = jax.ShapeDtypeStruct((s, d), F32)
    dq, dk, dv, dgb = pl.pallas_call(
        body, name=name, grid=(nc,),
        in_specs=[rev(0), rev(1), rev(2), gspec, rev(0), pl.BlockSpec((None, d, HEAD), lambda n: (nc - 1 - n, 0, 0))],
        out_specs=[rev(0), rev(0), rev(0), gspec],
        out_shape=[sds, sds, sds, jax.ShapeDtypeStruct((s, HEAD), F32)],
        scratch_shapes=[pltpu.VMEM((d, HEAD), F32)],
        compiler_params=_params(("arbitrary",)))(qkv, qkv, qkv, gb, do, states)
    return dq, dk, dv, dgb


SB_TQ = 512


def _tri01(rel):
    j_i = lax.broadcasted_iota(jnp.int32, (SBLK, SBLK), 0)
    s_i = lax.broadcasted_iota(jnp.int32, (SBLK, SBLK), 1)
    return rel(j_i, s_i).astype(BF16)


def _sb_scores(qt, kblk, mask, csum, rhs01):
    z = _dot(qt, kblk, NT)
    e = jnp.exp(-jnp.abs(z))
    sp = jnp.maximum(z, 0.0) + jnp.log(1.0 + e)
    ln = -sp if mask is None else jnp.where(mask, -sp, 0.0)
    st = _dot_hilo(ln, rhs01)
    wgt = jnp.exp((z - sp) + st[:, :SBLK] + csum)
    if mask is not None:
        wgt = jnp.where(mask, wgt, 0.0)
    return z, e, wgt, st


def _band_mask(rows, j, row0):
    r_i = lax.broadcasted_iota(jnp.int32, (rows, SBLK), 0)
    c_i = lax.broadcasted_iota(jnp.int32, (rows, SBLK), 1)
    return (j * SBLK + c_i) < (row0 + r_i)


def _sb_fwd(q, k, v, name):
    s, d = q.shape
    nh = d // HEAD
    tq = min(SB_TQ, s)
    nb = tq // SBLK

    def body(q_ref, k_ref, v_ref, o_ref, c_ref, acc, cs):
        qb = pl.program_id(1)
        lane = lax.broadcasted_iota(jnp.int32, (tq, HEAD), 1)
        rhs01 = jnp.concatenate([_tri01(lambda j, t: j > t), jnp.ones((SBLK, SBLK), BF16)], axis=1)
        acc[...] = jnp.zeros_like(acc)
        cs[...] = jnp.zeros_like(cs)
        c_ref[...] = jnp.zeros_like(c_ref)

        def process(rs, kb, mask):
            off = pl.multiple_of(kb * SBLK, SBLK)
            csum = cs[rs, :]
            _, _, wgt, st = _sb_scores(q_ref[rs, :], k_ref[pl.ds(off, SBLK), :], mask, csum, rhs01)
            acc[rs, :] += _dot(wgt, v_ref[pl.ds(off, SBLK), :])
            c_ref[rs, :] = jnp.where(lane[rs, :] == kb, csum, c_ref[rs, :])
            cs[rs, :] = csum + st[:, SBLK:]

        for j in reversed(range(nb)):
            process(slice(j * SBLK, tq), qb * nb + j, _band_mask(tq - j * SBLK, j, j * SBLK))

        def step(it, carry):
            process(slice(0, tq), qb * nb - 1 - it, None)
            return carry

        lax.fori_loop(0, qb * nb, step, 0)
        o_ref[...] = acc[...].astype(BF16)

    qspec = pl.BlockSpec((tq, HEAD), lambda h, i: (i, h))
    kspec = pl.BlockSpec((s, HEAD), lambda h, i: (0, h))
    return pl.pallas_call(
        body, name=name, grid=(nh, s // tq), in_specs=[qspec, kspec, kspec], out_specs=[qspec, qspec],
        out_shape=[jax.ShapeDtypeStruct((s, d), BF16), jax.ShapeDtypeStruct((s, d), F32)],
        scratch_shapes=[pltpu.VMEM((tq, HEAD), F32), pltpu.VMEM((tq, HEAD), F32)],
        compiler_params=_params(("parallel", "arbitrary")))(q, k, v)


def _sb_bwd(q, k, v, do, ctab, name):
    s, d = q.shape
    nh = d // HEAD
    tq = min(SB_TQ, s)
    nb = tq // SBLK

    def body(q_ref, k_ref, v_ref, do_ref, c_ref, dq_ref, dk_ref, dv_ref, ps):
        qb = pl.program_id(1)

        @pl.when(qb == 0)
        def _():
            dk_ref[...] = jnp.zeros_like(dk_ref)
            dv_ref[...] = jnp.zeros_like(dv_ref)

        dq_ref[...] = jnp.zeros_like(dq_ref)
        ps[...] = jnp.zeros_like(ps)
        lane = lax.broadcasted_iota(jnp.int32, (tq, HEAD), 1)
        after = _tri01(lambda j, t: j > t)
        rhs_pre = jnp.concatenate([_tri01(lambda j, t: j < t), jnp.ones((SBLK, SBLK), BF16)], axis=1)

        def process(rs, kb, mask):
            off = pl.multiple_of(kb * SBLK, SBLK)
            kblk = k_ref[pl.ds(off, SBLK), :]
            vblk = v_ref[pl.ds(off, SBLK), :]
            qt = q_ref[rs, :]
            dot_ = do_ref[rs, :]
            csum = _lane_col(c_ref[rs, :], lane[rs, :], kb)
            z, e, wgt, _ = _sb_scores(qt, kblk, mask, csum, after)
            dlw = _dot(dot_, vblk, NT) * wgt
            pt = _dot_hilo(dlw, rhs_pre)
            pfx = ps[rs, :]
            r = 1.0 / (1.0 + e)
            sig = jnp.where(z >= 0.0, r, e * r)
            dz = dlw * (1.0 - sig) - sig * (pfx + pt[:, :SBLK])
            if mask is not None:
                dz = jnp.where(mask, dz, 0.0)
            dq_ref[rs, :] += _dot(dz, kblk)
            dk_ref[pl.ds(off, SBLK), :] += _dot(dz, qt, TN)
            dv_ref[pl.ds(off, SBLK), :] += _dot(wgt, dot_, TN)
            ps[rs, :] = pfx + pt[:, SBLK:]

        def step(kb, carry):
            process(slice(0, tq), kb, None)
            return carry

        lax.fori_loop(0, qb * nb, step, 0)
        for j in range(nb):
            process(slice(j * SBLK, tq), qb * nb + j, _band_mask(tq - j * SBLK, j, j * SBLK))

    qspec = pl.BlockSpec((tq, HEAD), lambda h, i: (i, h))
    kspec = pl.BlockSpec((s, HEAD), lambda h, i: (0, h))
    sds = jax.ShapeDtypeStruct((s, d), F32)
    return pl.pallas_call(
        body, name=name, grid=(nh, s // tq), in_specs=[qspec, kspec, kspec, qspec, qspec],
        out_specs=[qspec, kspec, kspec], out_shape=[sds, sds, sds],
        scratch_shapes=[pltpu.VMEM((tq, HEAD), F32)],
        compiler_params=_params(("parallel", "arbitrary")))(q, k, v, do, ctab)


def _my_index():
    return 4 * lax.axis_index("x") + 2 * lax.axis_index("y") + lax.axis_index("c")


def _all_gather(x_shard, name):
    m_per, n = x_shard.shape

    def body(x_ref, out_ref, send_sems, recv_sems, local_sem):
        x, y, c = lax.axis_index("x"), lax.axis_index("y"), lax.axis_index("c")
        me, sibling = (x, y, c), (x, y, 1 - c)
        chips = [(1 - x, y), (x, 1 - y), (1 - x, 1 - y)]

        def rows(px, py, pc):
            return out_ref.at[pl.ds((4 * px + 2 * py + pc) * m_per, m_per), :]

        def copy(k, block, to, src=None):
            return pltpu.make_async_remote_copy(
                src_ref=rows(*block) if src is None else src, dst_ref=rows(*block),
                send_sem=send_sems.at[k], recv_sem=recv_sems.at[k], device_id=to, device_id_type=MESH)

        mine = pltpu.make_async_copy(x_ref, rows(*me), local_sem)
        mine.start()
        first = [copy(0, me, sibling, src=x_ref)]
        first += [copy(1 + j, me, (*chip, c), src=x_ref) for j, chip in enumerate(chips)]
        for cp in first:
            cp.start()
        passed = [copy(4 + j, (*chip, c), sibling) for j, chip in enumerate(chips)]
        for j, chip in enumerate(chips):
            copy(1 + j, (*chip, c), me).wait_recv()
            passed[j].start()
        copy(0, sibling, me).wait_recv()
        for j, chip in enumerate(chips):
            copy(4 + j, (*chip, 1 - c), me).wait_recv()
        for cp in first + passed:
            cp.wait_send()
        mine.wait()

    return pl.pallas_call(
        body, name=name, out_shape=jax.ShapeDtypeStruct((NDEV * m_per, n), x_shard.dtype),
        in_specs=[pl.BlockSpec(memory_space=pl.ANY)], out_specs=pl.BlockSpec(memory_space=pl.ANY),
        scratch_shapes=[pltpu.SemaphoreType.DMA((7,)), pltpu.SemaphoreType.DMA((7,)), pltpu.SemaphoreType.DMA],
    )(x_shard)


def _exchange(slabs, name):
    rows8, n = slabs.shape
    m_per = rows8 // NDEV

    def body(x_ref, out_ref, send_sems, recv_sems, local_sem):
        x, y, c = lax.axis_index("x"), lax.axis_index("y"), lax.axis_index("c")
        me = 4 * x + 2 * y + c

        def slab(ref, idx):
            return ref.at[pl.ds(idx * m_per, m_per), :]

        mine = pltpu.make_async_copy(slab(x_ref, me), slab(out_ref, me), local_sem)
        mine.start()
        copies = []
        for k in range(1, NDEV):
            px, py, pc = x ^ ((k >> 2) & 1), y ^ ((k >> 1) & 1), c ^ (k & 1)
            peer = 4 * px + 2 * py + pc
            copies.append(pltpu.make_async_remote_copy(
                src_ref=slab(x_ref, peer), dst_ref=slab(out_ref, me), send_sem=send_sems.at[k - 1],
                recv_sem=recv_sems.at[k - 1], device_id=(px, py, pc), device_id_type=MESH))
        for cp in copies:
            cp.start()
        for cp in copies:
            cp.wait_recv()
        for cp in copies:
            cp.wait_send()
        mine.wait()

    return pl.pallas_call(
        body, name=name, out_shape=jax.ShapeDtypeStruct((rows8, n), slabs.dtype),
        in_specs=[pl.BlockSpec(memory_space=pl.ANY)], out_specs=pl.BlockSpec(memory_space=pl.ANY),
        scratch_shapes=[pltpu.SemaphoreType.DMA((7,)), pltpu.SemaphoreType.DMA((7,)), pltpu.SemaphoreType.DMA],
    )(slabs)


def _sum_slots(x, name):
    _, r, c = x.shape
    tr = _pick(r, (512, 256, 128, 64, 32, 16, 8))

    def body(x_ref, o_ref):
        acc = x_ref[0].astype(F32)
        for i in range(1, NDEV):
            acc = acc + x_ref[i].astype(F32)
        o_ref[...] = acc

    return pl.pallas_call(
        body, name=name, grid=(r // tr,), in_specs=[pl.BlockSpec((NDEV, tr, c), lambda i: (0, i, 0))],
        out_specs=pl.BlockSpec((tr, c), lambda i: (i, 0)), out_shape=jax.ShapeDtypeStruct((r, c), F32),
        compiler_params=_params(("parallel",)))(x)


def _adamw(w, g, m, v, name):
    r, c = w.shape
    tr = _pick(r, (256, 128, 64, 32, 16, 8))
    c1 = 1.0 - B1 ** STEP
    c2 = 1.0 - B2 ** STEP

    def body(w_ref, g_ref, m_ref, v_ref, d_ref, nm_ref, nv_ref):
        gv = g_ref[...]
        nm = B1 * m_ref[...] + (1.0 - B1) * gv
        nv = B2 * v_ref[...] + (1.0 - B2) * (gv * gv)
        d_ref[...] = -LR * ((nm / c1) / (jnp.sqrt(nv / c2) + ADAM_EPS) + WD * w_ref[...])
        nm_ref[...] = nm
        nv_ref[...] = nv

    blk = pl.BlockSpec((tr, c), lambda i: (i, 0))
    sds = jax.ShapeDtypeStruct((r, c), F32)
    return pl.pallas_call(
        body, name=name, grid=(r // tr,), in_specs=[blk] * 4, out_specs=[blk] * 3, out_shape=[sds] * 3,
        compiler_params=_params(("parallel",)))(w, g, m, v)


def _pad_rows(a, mult):
    r = a.shape[0]
    pad = (-r) % mult
    return a if pad == 0 else jnp.pad(a, ((0, pad), (0, 0)))


def _pad_lanes(v, width=HEAD):
    return jnp.pad(v.reshape(1, -1), ((0, 0), (0, width - v.shape[-1])))


def kernel(x, p, ln_mix, ln_ffn, ln_ple, gdn_w_in, gdn_conv, gdn_a_log, gdn_dt_bias, gdn_norm, gdn_w_out, kv_norm, w_kv, k_norm, sb_w_q, sb_q_norm, sb_w_out, ffn_w_in, ffn_w_out, ple_w_proj, ple_w_gate, loss_target, m_ln_mix, m_ln_ffn, m_ln_ple, m_gdn_w_in, m_gdn_conv, m_gdn_a_log, m_gdn_dt_bias, m_gdn_norm, m_gdn_w_out, m_kv_norm, m_w_kv, m_k_norm, m_sb_w_q, m_sb_q_norm, m_sb_w_out, m_ffn_w_in, m_ffn_w_out, m_ple_w_proj, m_ple_w_gate, v_ln_mix, v_ln_ffn, v_ln_ple, v_gdn_w_in, v_gdn_conv, v_gdn_a_log, v_gdn_dt_bias, v_gdn_norm, v_gdn_w_out, v_kv_norm, v_w_kv, v_k_norm, v_sb_w_q, v_sb_q_norm, v_sb_w_out, v_ffn_w_in, v_ffn_w_out, v_ple_w_proj, v_ple_w_gate):
    s, d = x.shape[1], x.shape[2]
    nh = d // HEAD
    depth = ln_mix.shape[0]
    n_a = gdn_w_in.shape[0]
    n_b = sb_w_q.shape[0]
    me = _my_index()
    win_cols = gdn_w_in.shape[2]
    win_rows = 4 * d + 2 * nh

    def col_t(w):
        return jnp.transpose(w).astype(BF16)

    pieces = []
    parts = []

    def add(key, a):
        a = _pad_rows(a, 16)
        pieces.append((key, a.shape[0]))
        parts.append(a)

    for l in range(n_a):
        add(("gdn_w_in", l), col_t(gdn_w_in[l]))
        add(("gdn_w_out", l), gdn_w_out[l].astype(BF16))
    add(("w_kv", 0), col_t(w_kv))
    for j in range(n_b):
        add(("sb_w_q", j), sb_w_q[j].astype(BF16))
        add(("sb_w_out", j), sb_w_out[j].astype(BF16))
    for l in range(depth):
        add(("ffn_w_in", l), col_t(ffn_w_in[l]))
        add(("ffn_w_out", l), ffn_w_out[l].astype(BF16))
        add(("ple_w_proj", l), col_t(ple_w_proj[l]).reshape(-1, d))
        add(("ple_w_gate", l), ple_w_gate[l].astype(BF16))
    pack = jnp.concatenate(parts, axis=0)
    r_pack = pack.shape[0]
    gathered = _all_gather(pack, "comm_gather_weights").reshape(NDEV, r_pack, d)

    full = {}
    off = 0
    for key, r in pieces:
        full[key] = gathered[:, off:off + r, :]
        off += r

    def whole(key, valid=None):
        a = full[key]
        if valid is not None:
            a = a[:, :valid, :]
        return a.reshape(-1, d)

    pd = p.shape[-1]
    w_in_t, w_ab_t, w_gout = [], [], []
    for l in range(n_a):
        wt = whole(("gdn_w_in", l), win_cols)
        w_in_t.append(wt[:4 * d])
        w_ab_t.append(jnp.pad(wt[4 * d:], ((0, HEAD - 2 * nh), (0, 0))))
        w_gout.append(whole(("gdn_w_out", l)))
    wkv_t = whole(("w_kv", 0))
    w_q = [whole(("sb_w_q", j)) for j in range(n_b)]
    w_sout = [whole(("sb_w_out", j)) for j in range(n_b)]
    wf_t = [whole(("ffn_w_in", l)) for l in range(depth)]
    w_fout = [whole(("ffn_w_out", l)) for l in range(depth)]
    wp_t = [full[("ple_w_proj", l)].reshape(d, pd) for l in range(depth)]
    w_pg = [whole(("ple_w_gate", l)) for l in range(depth)]

    conv_rows = n_a * gdn_conv.shape[1]
    conv_sh = _pad_rows(gdn_conv.reshape(conv_rows, -1), 8)
    conv_g = _all_gather(conv_sh, "comm_gather_conv").reshape(NDEV, conv_sh.shape[0], -1)
    conv_full = jnp.transpose(conv_g[:, :conv_rows, :], (1, 0, 2)).reshape(n_a, gdn_conv.shape[1], 3 * d)

    h = x[0]
    sv = []
    kv_sv = None
    k_sh = v_sh = None
    for l in range(depth):
        t = {}
        t["h0"] = h
        hn = _rms_fwd(h, ln_mix[l], f"rms_mix_{l}")
        t["hn"] = hn
        if l < n_a:
            proj = _mm(hn, w_in_t[l], "nt", f"gdn_proj_{l}")
            pab = _mm(hn, w_ab_t[l], "nt", f"gdn_proj_ab_{l}")
            qkv = _conv_fwd(proj, conv_full[l], d, f"gdn_conv_{l}")
            al, dtb = _pad_lanes(gdn_a_log[l]), _pad_lanes(gdn_dt_bias[l])
            gb = _gates_fwd(pab, al, dtb, nh, f"gdn_gates_{l}")
            o_raw, states = _gdn_fwd(qkv, gb, d, f"gdn_rule_{l}")
            o2 = _headnorm_fwd(o_raw, gdn_norm[l], f"gdn_outnorm_{l}", gate=proj, gate_col0=3 * d)
            h = _mm(o2, w_gout[l], "nn", f"gdn_out_{l}", res=h)
            t.update(proj=proj, pab=pab, qkv=qkv, gb=gb, o_raw=o_raw, states=states, o2=o2, al=al, dtb=dtb)
        else:
            j = l - n_a
            qpre = _mm(hn, w_q[j], "nn", f"sb_qproj_{j}")
            qn = _headnorm_fwd(qpre, sb_q_norm[j], f"sb_qnorm_{j}", scale=HEAD ** -0.5)
            o, ctab = _sb_fwd(qn, k_sh, v_sh, f"sb_attn_{j}")
            h = _mm(o, w_sout[j], "nn", f"sb_out_{j}", res=h)
            t.update(qpre=qpre, qn=qn, o=o, ctab=ctab)
        t["h1"] = h
        hn2 = _rms_fwd(h, ln_ffn[l], f"rms_ffn_{l}")
        act, gs, us = _swiglu_fwd(hn2, wf_t[l], f"ffn_in_{l}")
        h = _mm(act, w_fout[l], "nn", f"ffn_out_{l}", res=h)
        t.update(hn2=hn2, act=act, gs=gs, us=us, h2=h)
        hn3 = _rms_fwd(h, ln_ple[l], f"rms_ple_{l}")
        h, gpre, pp = _ple_fwd(h, hn3, p[l, 0], w_pg[l], wp_t[l], f"ple_{l}")
        t.update(hn3=hn3, gpre=gpre, pp=pp)
        sv.append(t)
        if l == n_a - 1:
            kvn = _rms_fwd(h, kv_norm, "rms_kv")
            kv = _mm(kvn, wkv_t, "nt", "kv_proj")
            k_sh = _headnorm_fwd(kv, k_norm, "k_norm", width=d)
            v_sh = kv[:, d:].astype(BF16)
            kv_sv = dict(h=h, kvn=kvn, kv=kv)

    dh, loss_vec = _loss_fwd_bwd(h, loss_target[0], "loss")
    loss = lax.psum(jnp.sum(loss_vec), ("x", "y", "c"))

    gw = {}
    small = {}
    dk_sh = jnp.zeros((s, d), F32)
    dv_sh = jnp.zeros((s, d), F32)
    for l in reversed(range(depth)):
        t = sv[l]
        if l == n_a - 1:
            dkv_k, dkn = _headnorm_bwd(dk_sh, kv_sv["kv"], k_norm, "k_norm_bwd", dx_dtype=BF16)
            dkv = jnp.concatenate([dkv_k, dv_sh.astype(BF16)], axis=1)
            gw[("w_kv", 0)] = _mm(dkv, kv_sv["kvn"], "tn", "kv_dw", out_dtype=BF16)
            dkvn = _mm(dkv, wkv_t, "nn", "kv_dx")
            dh, dg = _rms_bwd(dkvn, kv_sv["h"], kv_norm, dh, "rms_kv_bwd")
            small["kv_norm"] = dg
            small["k_norm"] = dkn
        dgp, dpp = _ple_bwd(dh, t["gpre"], t["pp"], f"ple_bwd_{l}")
        gw[("ple_w_gate", l)] = _mm(t["hn3"], dgp, "tn", f"ple_dwg_{l}", out_dtype=BF16)
        gw[("ple_w_proj", l)] = _mm(dpp, p[l, 0], "tn", f"ple_dwp_{l}", out_dtype=BF16)
        dhn3 = _mm(dgp, w_pg[l], "nt", f"ple_dx_{l}")
        dh, dg = _rms_bwd(dhn3, t["h2"], ln_ple[l], dh, f"rms_ple_bwd_{l}")
        small[("ln_ple", l)] = dg
        dgs, dus = _swiglu_bwd(dh, w_fout[l], t["gs"], t["us"], f"ffn_bwd_act_{l}")
        gw[("ffn_w_out", l)] = _mm(t["act"], dh, "tn", f"ffn_dwo_{l}", out_dtype=BF16)
        f = dgs.shape[1]
        dwg = _mm(dgs, t["hn2"], "tn", f"ffn_dwg_{l}", out_dtype=BF16)
        dwu = _mm(dus, t["hn2"], "tn", f"ffn_dwu_{l}", out_dtype=BF16)
        gw[("ffn_w_in", l)] = jnp.concatenate([dwg, dwu], axis=0)
        dhn2 = _mm(dgs, wf_t[l][:f], "nn", f"ffn_dxg_{l}")
        dhn2 = _mm(dus, wf_t[l][f:], "nn", f"ffn_dxu_{l}", res=dhn2)
        dh, dg = _rms_bwd(dhn2, t["h1"], ln_ffn[l], dh, f"rms_ffn_bwd_{l}")
        small[("ln_ffn", l)] = dg
        if l < n_a:
            do2 = _mm(dh, w_gout[l], "nt", f"gdn_out_dx_{l}")
            gw[("gdn_w_out", l)] = _mm(t["o2"], dh, "tn", f"gdn_out_dw_{l}", out_dtype=BF16)
            do_raw, dgn, dgate = _headnorm_bwd(do2, t["o_raw"], gdn_norm[l], f"gdn_outnorm_bwd_{l}",
                                               gate=t["proj"], gate_col0=3 * d)
            small[("gdn_norm", l)] = dgn
            dq, dk, dv, dgb = _gdn_bwd(t["qkv"], t["gb"], do_raw, t["states"], d, f"gdn_rule_bwd_{l}")
            dpab, dal, ddt = _gates_bwd(dgb, t["pab"], t["al"], t["dtb"], nh, f"gdn_gates_bwd_{l}")
            small[("gdn_a_log", l)] = dal
            small[("gdn_dt_bias", l)] = ddt
            dqkv = jnp.concatenate([dq, dk, dv], axis=1)
            dproj_qkv, dconv = _conv_bwd(dqkv, t["proj"], conv_full[l], d, f"gdn_conv_bwd_{l}")
            small[("gdn_conv", l)] = dconv
            dproj = jnp.concatenate([dproj_qkv, dgate], axis=1)
            dw_main = _mm(dproj, t["hn"], "tn", f"gdn_proj_dw_{l}", out_dtype=BF16)
            dw_ab = _mm(dpab, t["hn"], "tn", f"gdn_proj_ab_dw_{l}", out_dtype=BF16)
            gw[("gdn_w_in", l)] = jnp.concatenate([dw_main, dw_ab[:16]], axis=0)[:win_rows]
            dhn = _mm(dproj, w_in_t[l], "nn", f"gdn_proj_dx_{l}")
            dhn = _mm(dpab, w_ab_t[l], "nn", f"gdn_proj_ab_dx_{l}", res=dhn)
        else:
            j = l - n_a
            do = _mm(dh, w_sout[j], "nt", f"sb_out_dx_{j}", out_dtype=BF16)
            gw[("sb_w_out", j)] = _mm(t["o"], dh, "tn", f"sb_out_dw_{j}", out_dtype=BF16)
            dq, dk, dv = _sb_bwd(t["qn"], k_sh, v_sh, do, t["ctab"], f"sb_attn_bwd_{j}")
            dk_sh = dk_sh + dk
            dv_sh = dv_sh + dv
            dqpre, dqn = _headnorm_bwd(dq, t["qpre"], sb_q_norm[j], f"sb_qnorm_bwd_{j}", scale=HEAD ** -0.5, dx_dtype=BF16)
            small[("sb_q_norm", j)] = dqn
            gw[("sb_w_q", j)] = _mm(t["hn"], dqpre, "tn", f"sb_q_dw_{j}", out_dtype=BF16)
            dhn = _mm(dqpre, w_q[j], "nt", f"sb_q_dx_{j}")
        dh, dg = _rms_bwd(dhn, t["h0"], ln_mix[l], dh, f"rms_mix_bwd_{l}")
        small[("ln_mix", l)] = dg
    grad_x = dh[None]

    gparts = []
    for key, r in pieces:
        g = gw[key]
        if key[0] == "ple_w_proj":
            g = g.reshape(NDEV, -1, d)
        else:
            g = g.reshape(NDEV, -1, g.shape[-1])
        padr = r - g.shape[1]
        if padr:
            g = jnp.pad(g, ((0, 0), (0, padr), (0, 0)))
        gparts.append(g)
    gpack = jnp.concatenate(gparts, axis=1).reshape(NDEV * r_pack, d)
    recv = _exchange(gpack, "comm_scatter_grads").reshape(NDEV, r_pack, d)
    gsum = _sum_slots(recv, "grad_sum")
    gshard = {}
    off = 0
    for key, r in pieces:
        gshard[key] = gsum[off:off + r]
        off += r

    def col_back(key, n_valid):
        return jnp.transpose(gshard[key][:n_valid])

    g_gdn_w_in = jnp.stack([col_back(("gdn_w_in", l), win_cols) for l in range(n_a)])
    g_gdn_w_out = jnp.stack([gshard[("gdn_w_out", l)] for l in range(n_a)])
    g_w_kv = col_back(("w_kv", 0), w_kv.shape[1])
    g_sb_w_q = jnp.stack([gshard[("sb_w_q", j)] for j in range(n_b)])
    g_sb_w_out = jnp.stack([gshard[("sb_w_out", j)] for j in range(n_b)])
    g_ffn_w_in = jnp.stack([col_back(("ffn_w_in", l), ffn_w_in.shape[2]) for l in range(depth)])
    g_ffn_w_out = jnp.stack([gshard[("ffn_w_out", l)] for l in range(depth)])
    g_ple_w_proj = jnp.stack([jnp.transpose(gshard[("ple_w_proj", l)].reshape(-1, pd)) for l in range(depth)])
    g_ple_w_gate = jnp.stack([gshard[("ple_w_gate", l)] for l in range(depth)])

    def vec_rows(v):
        return v.reshape(-1, HEAD)

    small_items = []
    for name_, cnt in (("ln_mix", depth), ("ln_ffn", depth), ("ln_ple", depth)):
        for l in range(cnt):
            small_items.append(((name_, l), vec_rows(small[(name_, l)])))
    for l in range(n_a):
        small_items.append((("gdn_conv", l), small[("gdn_conv", l)].reshape(-1, HEAD)))
        small_items.append((("gdn_a_log", l), small[("gdn_a_log", l)]))
        small_items.append((("gdn_dt_bias", l), small[("gdn_dt_bias", l)]))
        small_items.append((("gdn_norm", l), small[("gdn_norm", l)]))
    small_items.append(("kv_norm", vec_rows(small["kv_norm"])))
    small_items.append(("k_norm", small["k_norm"]))
    for j in range(n_b):
        small_items.append((("sb_q_norm", j), small[("sb_q_norm", j)]))
    spack = jnp.concatenate([_pad_rows(a, 8) for _, a in small_items], axis=0)
    sg = _all_gather(spack, "comm_gather_small").reshape(NDEV, spack.shape[0], HEAD)
    ssum = _sum_slots(sg, "small_sum")
    sm = {}
    off = 0
    for key, a in small_items:
        sm[key] = ssum[off:off + a.shape[0]]
        off += a.shape[0] + (-a.shape[0]) % 8

    g_ln_mix = jnp.stack([sm[("ln_mix", l)].reshape(d) for l in range(depth)])
    g_ln_ffn = jnp.stack([sm[("ln_ffn", l)].reshape(d) for l in range(depth)])
    g_ln_ple = jnp.stack([sm[("ln_ple", l)].reshape(d) for l in range(depth)])
    conv_loc = gdn_conv.shape[2]
    g_conv_full = jnp.stack([sm[("gdn_conv", l)].reshape(gdn_conv.shape[1], 3 * d) for l in range(n_a)])
    g_gdn_conv = lax.dynamic_slice_in_dim(g_conv_full, me * conv_loc, conv_loc, axis=2)
    g_a_log = jnp.stack([sm[("gdn_a_log", l)][0, :nh] for l in range(n_a)])
    g_dt_bias = jnp.stack([sm[("gdn_dt_bias", l)][0, :nh] for l in range(n_a)])
    g_gdn_norm = jnp.stack([sm[("gdn_norm", l)][0] for l in range(n_a)])
    g_kv_norm = sm["kv_norm"].reshape(d)
    g_k_norm = sm["k_norm"][0]
    g_sb_q_norm = jnp.stack([sm[("sb_q_norm", j)][0] for j in range(n_b)])

    grads = [g_ln_mix, g_ln_ffn, g_ln_ple, g_gdn_w_in, g_gdn_conv, g_a_log, g_dt_bias, g_gdn_norm, g_gdn_w_out,
             g_kv_norm, g_w_kv, g_k_norm, g_sb_w_q, g_sb_q_norm, g_sb_w_out, g_ffn_w_in, g_ffn_w_out, g_ple_w_proj,
             g_ple_w_gate]
    weights = [ln_mix, ln_ffn, ln_ple, gdn_w_in, gdn_conv, gdn_a_log, gdn_dt_bias, gdn_norm, gdn_w_out, kv_norm, w_kv,
               k_norm, sb_w_q, sb_q_norm, sb_w_out, ffn_w_in, ffn_w_out, ple_w_proj, ple_w_gate]
    moms = [m_ln_mix, m_ln_ffn, m_ln_ple, m_gdn_w_in, m_gdn_conv, m_gdn_a_log, m_gdn_dt_bias, m_gdn_norm, m_gdn_w_out,
            m_kv_norm, m_w_kv, m_k_norm, m_sb_w_q, m_sb_q_norm, m_sb_w_out, m_ffn_w_in, m_ffn_w_out, m_ple_w_proj,
            m_ple_w_gate]
    vels = [v_ln_mix, v_ln_ffn, v_ln_ple, v_gdn_w_in, v_gdn_conv, v_gdn_a_log, v_gdn_dt_bias, v_gdn_norm, v_gdn_w_out,
            v_kv_norm, v_w_kv, v_k_norm, v_sb_w_q, v_sb_q_norm, v_sb_w_out, v_ffn_w_in, v_ffn_w_out, v_ple_w_proj,
            v_ple_w_gate]

    deltas, new_m, new_v = [], [], []
    small_idx = [i for i, w in enumerate(weights) if w.size < 8 * HEAD * 16]
    for i, (w, g, m, v) in enumerate(zip(weights, grads, moms, vels)):
        if i in small_idx:
            deltas.append(None), new_m.append(None), new_v.append(None)
            continue
        shp = w.shape
        two = lambda a: a.reshape(-1, shp[-1])
        dl, nm, nv = _adamw(two(w), two(g), two(m), two(v), f"adamw_{i}")
        deltas.append(dl.reshape(shp)), new_m.append(nm.reshape(shp)), new_v.append(nv.reshape(shp))

    def flat_pack(arrs):
        flat = jnp.concatenate([a.reshape(-1) for a in arrs])
        pad = (-flat.shape[0]) % (8 * HEAD)
        return jnp.pad(flat, (0, pad)).reshape(-1, HEAD)

    sw = flat_pack([weights[i] for i in small_idx])
    sgr = flat_pack([grads[i] for i in small_idx])
    smo = flat_pack([moms[i] for i in small_idx])
    sve = flat_pack([vels[i] for i in small_idx])
    sdl, snm, snv = _adamw(sw, sgr, smo, sve, "adamw_small")
    off = 0
    for i in small_idx:
        n = weights[i].size
        shp = weights[i].shape
        deltas[i] = sdl.reshape(-1)[off:off + n].reshape(shp)
        new_m[i] = snm.reshape(-1)[off:off + n].reshape(shp)
        new_v[i] = snv.reshape(-1)[off:off + n].reshape(shp)
        off += n

    return (loss, grad_x, *grads, *deltas, *new_m, *new_v)
```

```python
import functools
import math

import jax
import jax.numpy as jnp
from jax import lax
from jax.experimental import pallas as pl
from jax.experimental.pallas import tpu as pltpu

F32 = jnp.float32
BF16 = jnp.bfloat16
NDEV = 8
HEAD = 128
CHUNK = 64
SBLK = 128
EPS = 1e-6
LR, B1, B2, ADAM_EPS, WD, STEP = 0.001, 0.9, 0.999, 1e-08, 0.01, 10
NEG = -1e30
MM_VMEM_BUDGET = 36 * 1024 * 1024

NN = (((1,), (0,)), ((), ()))
NT = (((1,), (1,)), ((), ()))
TN = (((0,), (0,)), ((), ()))
BNN = (((2,), (1,)), ((0,), (0,)))
BNT = (((2,), (2,)), ((0,), (0,)))
BTN = (((1,), (1,)), ((0,), (0,)))
MESH = pl.DeviceIdType.MESH


def _dot(a, b, dims=NN):
    return lax.dot_general(a.astype(BF16), b.astype(BF16), dims, preferred_element_type=F32)


def _dot_hilo(a, b01, dims=NN):
    hi = a.astype(BF16)
    lo = (a - hi.astype(F32)).astype(BF16)
    return (lax.dot_general(hi, b01, dims, preferred_element_type=F32)
            + lax.dot_general(lo, b01, dims, preferred_element_type=F32))


def _pick(dim, cands):
    for c in cands:
        if dim % c == 0:
            return c
    return dim


def _params(sem, vmem_mb=48):
    return pltpu.CompilerParams(dimension_semantics=sem, vmem_limit_bytes=vmem_mb * 1024 * 1024)


def _silu(x):
    return x * jax.nn.sigmoid(x)


def _dsilu(x):
    s = jax.nn.sigmoid(x)
    return s * (1.0 + x * (1.0 - s))


def _mm(a, b, mode, name, out_dtype=F32, res=None):
    if mode == "nn":
        (m, k), n = a.shape, b.shape[1]
    elif mode == "nt":
        (m, k), n = a.shape, b.shape[0]
    else:
        (k, m), n = a.shape, b.shape[1]
    tn = _pick(n, (512, 256, 128))
    tk = k if k <= 4096 else _pick(k, (2048, 1024, 512, 256, 128))
    nk = k // tk
    out_b = jnp.dtype(out_dtype).itemsize + (res.dtype.itemsize if res is not None else 0)
    for tm in (1024, 512, 256, 128, m):
        need = 2 * (tm * tk * a.dtype.itemsize + tk * tn * b.dtype.itemsize + tm * tn * out_b) + 4 * tm * tn
        if m % tm == 0 and need <= MM_VMEM_BUDGET:
            break
    dims = {"nn": NN, "nt": NT, "tn": TN}[mode]
    if mode == "tn":
        a_spec = pl.BlockSpec((tk, tm), lambda i, j, kk: (kk, i))
    else:
        a_spec = pl.BlockSpec((tm, tk), lambda i, j, kk: (i, kk))
    if mode == "nt":
        b_spec = pl.BlockSpec((tn, tk), lambda i, j, kk: (j, kk))
    else:
        b_spec = pl.BlockSpec((tk, tn), lambda i, j, kk: (kk, j))
    mn_spec = pl.BlockSpec((tm, tn), lambda i, j, kk: (i, j))
    has_res = res is not None

    def body(*refs):
        if has_res:
            a_ref, b_ref, r_ref, o_ref, acc = refs
        else:
            a_ref, b_ref, o_ref, acc = refs
        kk = pl.program_id(2)

        @pl.when(kk == 0)
        def _():
            acc[...] = jnp.zeros_like(acc)

        acc[...] += _dot(a_ref[...], b_ref[...], dims)

        @pl.when(kk == nk - 1)
        def _():
            r = acc[...]
            if has_res:
                r = r + r_ref[...].astype(F32)
            o_ref[...] = r.astype(out_dtype)

    ins = [a, b] + ([res] if has_res else [])
    in_specs = [a_spec, b_spec] + ([mn_spec] if has_res else [])
    return pl.pallas_call(
        body, name=name, grid=(m // tm, n // tn, nk), in_specs=in_specs, out_specs=mn_spec,
        out_shape=jax.ShapeDtypeStruct((m, n), out_dtype), scratch_shapes=[pltpu.VMEM((tm, tn), F32)],
        compiler_params=_params(("parallel", "parallel", "arbitrary")))(*ins)


def _rms_fwd(h, g, name):
    s, d = h.shape
    tm = _pick(s, (512, 256, 128))

    def body(h_ref, g_ref, o_ref):
        x = h_ref[...]
        r = lax.rsqrt(jnp.mean(x * x, axis=-1, keepdims=True) + EPS)
        o_ref[...] = (x * r * g_ref[...]).astype(BF16)

    return pl.pallas_call(
        body, name=name, grid=(s // tm,),
        in_specs=[pl.BlockSpec((tm, d), lambda i: (i, 0)), pl.BlockSpec((1, d), lambda i: (0, 0))],
        out_specs=pl.BlockSpec((tm, d), lambda i: (i, 0)),
        out_shape=jax.ShapeDtypeStruct((s, d), BF16), compiler_params=_params(("parallel",)))(h, g.reshape(1, d))


def _rms_bwd(dy, x, g, dres, name, after=None):
    s, d = x.shape
    tm = _pick(s, (512, 256, 128))

    def body(dy_ref, x_ref, g_ref, dr_ref, *rest):
        dx_ref, dg_ref = rest[-2:]

        @pl.when(pl.program_id(0) == 0)
        def _():
            dg_ref[...] = jnp.zeros_like(dg_ref)

        xv = x_ref[...]
        dyv = dy_ref[...].astype(F32)
        r = lax.rsqrt(jnp.mean(xv * xv, axis=-1, keepdims=True) + EPS)
        gdy = dyv * g_ref[...]
        mean_t = jnp.mean(xv * gdy, axis=-1, keepdims=True)
        dx_ref[...] = dr_ref[...] + r * gdy - xv * (r * r * r) * mean_t
        dg_ref[...] += jnp.sum(dyv * xv * r, axis=0, keepdims=True)

    row = pl.BlockSpec((tm, d), lambda i: (i, 0))
    vec = pl.BlockSpec((1, d), lambda i: (0, 0))
    ins, in_specs = [dy, x, g.reshape(1, d), dres], [row, row, vec, row]
    if after is not None:
        ins.append(after)
        in_specs.append(pl.BlockSpec(memory_space=pl.ANY))
    return pl.pallas_call(
        body, name=name, grid=(s // tm,), in_specs=in_specs, out_specs=[row, vec],
        out_shape=[jax.ShapeDtypeStruct((s, d), F32), jax.ShapeDtypeStruct((1, d), F32)],
        compiler_params=_params(("arbitrary",)))(*ins)


def _headnorm_fwd(x, g, name, scale=1.0, gate=None, gate_col0=0, out_dtype=BF16, width=None, head_major=False):
    if head_major:
        s, d = x.shape[1], x.shape[0] * HEAD
    else:
        s, d = x.shape[0], (width or x.shape[1])
    nh = d // HEAD
    tm = _pick(s, (256, 128))
    has_gate = gate is not None
    gb = gate_col0 // d

    def body(*refs):
        if has_gate:
            x_ref, g_ref, gt_ref, o_ref = refs
        else:
            x_ref, g_ref, o_ref = refs
        gv = g_ref[...]
        for h in range(nh):
            sl = slice(h * HEAD, (h + 1) * HEAD)
            xv = (x_ref[h] if head_major else x_ref[:, sl]).astype(F32)
            r = lax.rsqrt(jnp.mean(xv * xv, axis=-1, keepdims=True) + EPS)
            y = xv * r * gv
            if scale != 1.0:
                y = y * scale
            if has_gate:
                y = y * _silu(gt_ref[:, sl])
            o_ref[:, sl] = y.astype(out_dtype)

    row = pl.BlockSpec((tm, d), lambda i: (i, 0))
    hm = pl.BlockSpec((nh, tm, HEAD), lambda i: (0, i, 0))
    ins = [x, g.reshape(1, HEAD)]
    in_specs = [hm if head_major else row, pl.BlockSpec((1, HEAD), lambda i: (0, 0))]
    if has_gate:
        ins.append(gate)
        in_specs.append(pl.BlockSpec((tm, d), lambda i: (i, gb)))
    return pl.pallas_call(
        body, name=name, grid=(s // tm,), in_specs=in_specs, out_specs=row,
        out_shape=jax.ShapeDtypeStruct((s, d), out_dtype), compiler_params=_params(("parallel",)))(*ins)


def _headnorm_bwd(dy, x, g, name, scale=1.0, gate=None, gate_col0=0, dx_dtype=F32, head_major=False):
    s, d = dy.shape
    nh = d // HEAD
    tm = _pick(s, (256, 128))
    has_gate = gate is not None
    gb = gate_col0 // d

    def body(*refs):
        if has_gate:
            dy_ref, x_ref, g_ref, gt_ref, dx_ref, dg_ref, dgt_ref = refs
        else:
            dy_ref, x_ref, g_ref, dx_ref, dg_ref = refs

        @pl.when(pl.program_id(0) == 0)
        def _():
            dg_ref[...] = jnp.zeros_like(dg_ref)

        gv = g_ref[...]
        dg_acc = jnp.zeros((1, HEAD), F32)
        for h in range(nh):
            sl = slice(h * HEAD, (h + 1) * HEAD)
            xv = (x_ref[h] if head_major else x_ref[:, sl]).astype(F32)
            dyv = dy_ref[:, sl].astype(F32)
            r = lax.rsqrt(jnp.mean(xv * xv, axis=-1, keepdims=True) + EPS)
            if has_gate:
                gt = gt_ref[:, sl]
                dgt_ref[:, sl] = (dyv * (xv * r * gv) * _dsilu(gt)).astype(dgt_ref.dtype)
                dn = dyv * _silu(gt)
            else:
                dn = dyv
            if scale != 1.0:
                dn = dn * scale
            gdn = dn * gv
            mean_t = jnp.mean(xv * gdn, axis=-1, keepdims=True)
            dxv = (r * gdn - xv * (r * r * r) * mean_t).astype(dx_dtype)
            if head_major:
                dx_ref[h] = dxv
            else:
                dx_ref[:, sl] = dxv
            dg_acc = dg_acc + jnp.sum(dn * xv * r, axis=0, keepdims=True)
        dg_ref[...] += dg_acc

    row = pl.BlockSpec((tm, d), lambda i: (i, 0))
    hm = pl.BlockSpec((nh, tm, HEAD), lambda i: (0, i, 0))
    vec = pl.BlockSpec((1, HEAD), lambda i: (0, 0))
    ins = [dy, x, g.reshape(1, HEAD)]
    in_specs = [row, hm if head_major else row, vec]
    out_specs = [hm if head_major else row, vec]
    dx_shape = (nh, s, HEAD) if head_major else (s, d)
    out_shape = [jax.ShapeDtypeStruct(dx_shape, dx_dtype), jax.ShapeDtypeStruct((1, HEAD), F32)]
    if has_gate:
        ins.append(gate)
        in_specs.append(pl.BlockSpec((tm, d), lambda i: (i, gb)))
        out_specs.append(row)
        out_shape.append(jax.ShapeDtypeStruct((s, d), BF16))
    return pl.pallas_call(
        body, name=name, grid=(s // tm,), in_specs=in_specs, out_specs=out_specs, out_shape=out_shape,
        compiler_params=_params(("arbitrary",)))(*ins)


def _swiglu_fwd(hn, wf_t, name):
    s, d = hn.shape
    f = wf_t.shape[0] // 2
    tm = _pick(s, (512, 256, 128))
    tn = _pick(f, (512, 256, 128))
    nj = f // tn

    def body(a_ref, wg_ref, wu_ref, act_ref, g_ref, u_ref):
        a = a_ref[...]
        g = _dot(a, wg_ref[...], NT)
        u = _dot(a, wu_ref[...], NT)
        act_ref[...] = (_silu(g) * u).astype(BF16)
        g_ref[...] = g.astype(BF16)
        u_ref[...] = u.astype(BF16)

    o_spec = pl.BlockSpec((tm, tn), lambda i, j: (i, j))
    sds = jax.ShapeDtypeStruct((s, f), BF16)
    return pl.pallas_call(
        body, name=name, grid=(s // tm, nj),
        in_specs=[pl.BlockSpec((tm, d), lambda i, j: (i, 0)), pl.BlockSpec((tn, d), lambda i, j: (j, 0)),
                  pl.BlockSpec((tn, d), lambda i, j: (j + nj, 0))],
        out_specs=[o_spec, o_spec, o_spec], out_shape=[sds, sds, sds],
        compiler_params=_params(("parallel", "parallel")))(hn, wf_t, wf_t)


def _swiglu_bwd(dh, w_out, g, u, name):
    s, d = dh.shape
    f = w_out.shape[0]
    tm = _pick(s, (512, 256, 128))
    tn = _pick(f, (512, 256, 128))

    def body(dh_ref, w_ref, g_ref, u_ref, dg_ref, du_ref):
        dact = _dot(dh_ref[...], w_ref[...], NT)
        gv = g_ref[...].astype(F32)
        uv = u_ref[...].astype(F32)
        dg_ref[...] = (dact * uv * _dsilu(gv)).astype(BF16)
        du_ref[...] = (dact * _silu(gv)).astype(BF16)

    o_spec = pl.BlockSpec((tm, tn), lambda i, j: (i, j))
    sds = jax.ShapeDtypeStruct((s, f), BF16)
    return pl.pallas_call(
        body, name=name, grid=(s // tm, f // tn),
        in_specs=[pl.BlockSpec((tm, d), lambda i, j: (i, 0)), pl.BlockSpec((tn, d), lambda i, j: (j, 0)), o_spec, o_spec],
        out_specs=[o_spec, o_spec], out_shape=[sds, sds],
        compiler_params=_params(("parallel", "parallel")))(dh, w_out, g, u)


def _ple_fwd(h, hn, p, w_gate, wp_t, name):
    s, d = h.shape
    pd = p.shape[1]
    tm = _pick(s, (512, 256, 128))
    tn = _pick(d, (512, 256, 128))

    def body(h_ref, hn_ref, p_ref, wg_ref, wp_ref, o_ref, gp_ref, pp_ref):
        gpre = _dot(hn_ref[...], wg_ref[...], NN)
        pp = _dot(p_ref[...], wp_ref[...], NT)
        o_ref[...] = h_ref[...] + pp * jax.nn.sigmoid(gpre)
        gp_ref[...] = gpre.astype(BF16)
        pp_ref[...] = pp.astype(BF16)

    mn = pl.BlockSpec((tm, tn), lambda i, j: (i, j))
    return pl.pallas_call(
        body, name=name, grid=(s // tm, d // tn),
        in_specs=[mn, pl.BlockSpec((tm, d), lambda i, j: (i, 0)), pl.BlockSpec((tm, pd), lambda i, j: (i, 0)),
                  pl.BlockSpec((d, tn), lambda i, j: (0, j)), pl.BlockSpec((tn, pd), lambda i, j: (j, 0))],
        out_specs=[mn, mn, mn],
        out_shape=[jax.ShapeDtypeStruct((s, d), F32), jax.ShapeDtypeStruct((s, d), BF16), jax.ShapeDtypeStruct((s, d), BF16)],
        compiler_params=_params(("parallel", "parallel")))(h, hn, p, w_gate, wp_t)


def _ple_bwd(dh, gpre, pp, name):
    s, d = dh.shape
    tm = _pick(s, (512, 256, 128))

    def body(dh_ref, gp_ref, pp_ref, dgp_ref, dpp_ref):
        dv = dh_ref[...]
        sig = jax.nn.sigmoid(gp_ref[...].astype(F32))
        ppv = pp_ref[...].astype(F32)
        dpp_ref[...] = (dv * sig).astype(BF16)
        dgp_ref[...] = (dv * ppv * sig * (1.0 - sig)).astype(BF16)

    row = pl.BlockSpec((tm, d), lambda i: (i, 0))
    sds = jax.ShapeDtypeStruct((s, d), BF16)
    return pl.pallas_call(
        body, name=name, grid=(s // tm,), in_specs=[row, row, row], out_specs=[row, row], out_shape=[sds, sds],
        compiler_params=_params(("parallel",)))(dh, gpre, pp)


def _loss_fwd_bwd(y, t, name):
    s, d = y.shape
    tm = _pick(s, (512, 256, 128))

    def body(y_ref, t_ref, dy_ref, l_ref):
        @pl.when(pl.program_id(0) == 0)
        def _():
            l_ref[...] = jnp.zeros_like(l_ref)

        e = y_ref[...] - t_ref[...]
        dy_ref[...] = e * (1.0 / d)
        l_ref[...] += jnp.sum(e * e, axis=0, keepdims=True) * (0.5 / d)

    row = pl.BlockSpec((tm, d), lambda i: (i, 0))
    vec = pl.BlockSpec((1, d), lambda i: (0, 0))
    return pl.pallas_call(
        body, name=name, grid=(s // tm,), in_specs=[row, row], out_specs=[row, vec],
        out_shape=[jax.ShapeDtypeStruct((s, d), F32), jax.ShapeDtypeStruct((1, d), F32)],
        compiler_params=_params(("arbitrary",)))(y, t)


PADR = 8


def _conv_fwd(proj, w_conv, d, name):
    s = proj.shape[0]
    nh = d // HEAD
    kw = w_conv.shape[0]
    qscale = HEAD ** -0.5

    def body(x_ref, w_ref, o_ref, xp):
        kind = pl.program_id(0) // nh
        xp[0:PADR, :] = jnp.zeros((PADR, HEAD), F32)
        xp[PADR:, :] = x_ref[...]
        acc = jnp.zeros((s, HEAD), F32)
        for j in range(kw):
            acc = acc + w_ref[j:j + 1, :] * xp[PADR - (kw - 1) + j:PADR - (kw - 1) + j + s, :]
        a = _silu(acc)
        r = lax.rsqrt(jnp.sum(a * a, axis=-1, keepdims=True) + EPS)
        fac = jnp.where(kind == 0, r * qscale, jnp.where(kind == 1, r, jnp.ones_like(r)))
        o_ref[...] = a * fac

    blk = pl.BlockSpec((s, HEAD), lambda c: (0, c))
    hm = pl.BlockSpec((None, s, HEAD), lambda c: (c, 0, 0))
    return pl.pallas_call(
        body, name=name, grid=(3 * nh,), in_specs=[blk, pl.BlockSpec((kw, HEAD), lambda c: (0, c))], out_specs=hm,
        out_shape=jax.ShapeDtypeStruct((3 * nh, s, HEAD), F32), scratch_shapes=[pltpu.VMEM((s + PADR, HEAD), F32)],
        compiler_params=_params(("parallel",)))(proj, w_conv)


def _conv_bwd(dqkv, proj, w_conv, d, name):
    s = proj.shape[0]
    nh = d // HEAD
    kw = w_conv.shape[0]
    qscale = HEAD ** -0.5

    def body(dy_ref, x_ref, w_ref, dx_ref, dw_ref, xp, dp):
        kind = pl.program_id(0) // nh
        xp[0:PADR, :] = jnp.zeros((PADR, HEAD), F32)
        xp[PADR:, :] = x_ref[...]
        acc = jnp.zeros((s, HEAD), F32)
        for j in range(kw):
            acc = acc + w_ref[j:j + 1, :] * xp[PADR - (kw - 1) + j:PADR - (kw - 1) + j + s, :]
        a = _silu(acc)
        dy = dy_ref[...]
        r = lax.rsqrt(jnp.sum(a * a, axis=-1, keepdims=True) + EPS)
        sc = jnp.where(kind == 0, qscale, 1.0)
        dyn = dy * sc
        da_norm = r * dyn - a * (r * r * r) * jnp.sum(a * dyn, axis=-1, keepdims=True)
        da = jnp.where(kind == 2, dy, da_norm)
        dacc = da * _dsilu(acc)
        dp[0:s, :] = dacc
        dp[s:, :] = jnp.zeros((PADR, HEAD), F32)
        dx = jnp.zeros((s, HEAD), F32)
        for j in range(kw):
            sh = kw - 1 - j
            dx = dx + w_ref[j:j + 1, :] * dp[sh:sh + s, :]
            dw_ref[j:j + 1, :] = jnp.sum(dacc * xp[PADR - sh:PADR - sh + s, :], axis=0, keepdims=True)
        dx_ref[...] = dx.astype(BF16)

    blk = pl.BlockSpec((s, HEAD), lambda c: (0, c))
    hm = pl.BlockSpec((None, s, HEAD), lambda c: (c, 0, 0))
    wblk = pl.BlockSpec((kw, HEAD), lambda c: (0, c))
    return pl.pallas_call(
        body, name=name, grid=(3 * nh,), in_specs=[hm, blk, wblk], out_specs=[blk, wblk],
        out_shape=[jax.ShapeDtypeStruct((s, 3 * d), BF16), jax.ShapeDtypeStruct((kw, 3 * d), F32)],
        scratch_shapes=[pltpu.VMEM((s + PADR, HEAD), F32), pltpu.VMEM((s + PADR, HEAD), F32)],
        compiler_params=_params(("parallel",)))(dqkv, proj, w_conv)


def _softplus(x):
    return jnp.maximum(x, 0.0) + jnp.log(1.0 + jnp.exp(-jnp.abs(x)))


def _gates_fwd(pab, a_log, dt_bias, nh, name):
    s = pab.shape[0]
    tm = _pick(s, (512, 256, 128))

    def body(x_ref, al_ref, dt_ref, o_ref):
        x = x_ref[...]
        lane = lax.broadcasted_iota(jnp.int32, x.shape, 1)
        g = -jnp.exp(al_ref[...]) * _softplus(x + dt_ref[...])
        o_ref[...] = jnp.where(lane < nh, g, jnp.where(lane < 2 * nh, jax.nn.sigmoid(x), 0.0))

    row = pl.BlockSpec((tm, HEAD), lambda i: (i, 0))
    vec = pl.BlockSpec((1, HEAD), lambda i: (0, 0))
    return pl.pallas_call(
        body, name=name, grid=(s // tm,), in_specs=[row, vec, vec], out_specs=row,
        out_shape=jax.ShapeDtypeStruct((s, HEAD), F32), compiler_params=_params(("parallel",)))(pab, a_log, dt_bias)


def _gates_bwd(dgb, pab, a_log, dt_bias, nh, name):
    s = pab.shape[0]
    tm = _pick(s, (512, 256, 128))

    def body(d_ref, x_ref, al_ref, dt_ref, dx_ref, dal_ref, ddt_ref):
        @pl.when(pl.program_id(0) == 0)
        def _():
            dal_ref[...] = jnp.zeros_like(dal_ref)
            ddt_ref[...] = jnp.zeros_like(ddt_ref)

        x = x_ref[...]
        dv = d_ref[...]
        lane = lax.broadcasted_iota(jnp.int32, x.shape, 1)
        ea = jnp.exp(al_ref[...])
        xs = x + dt_ref[...]
        g = -ea * _softplus(xs)
        dxs = jnp.where(lane < nh, dv * (-ea) * jax.nn.sigmoid(xs), 0.0)
        sg = jax.nn.sigmoid(x)
        dxb = jnp.where((lane >= nh) & (lane < 2 * nh), dv * sg * (1.0 - sg), 0.0)
        dx_ref[...] = (dxs + dxb).astype(BF16)
        dal_ref[...] += jnp.sum(jnp.where(lane < nh, dv * g, 0.0), axis=0, keepdims=True)
        ddt_ref[...] += jnp.sum(dxs, axis=0, keepdims=True)

    row = pl.BlockSpec((tm, HEAD), lambda i: (i, 0))
    vec = pl.BlockSpec((1, HEAD), lambda i: (0, 0))
    return pl.pallas_call(
        body, name=name, grid=(s // tm,), in_specs=[row, row, vec, vec], out_specs=[row, vec, vec],
        out_shape=[jax.ShapeDtypeStruct((s, HEAD), BF16), jax.ShapeDtypeStruct((1, HEAD), F32),
                   jax.ShapeDtypeStruct((1, HEAD), F32)],
        compiler_params=_params(("arbitrary",)))(dgb, pab, a_log, dt_bias)


def _tri_inv(a_low, eye_f):
    n = -a_low
    p = eye_f + n
    steps = int(math.log2(a_low.shape[-1])) - 1
    for _ in range(steps):
        n = _dot(n, n, BNN)
        p = p + _dot(p, n, BNN)
    return p


def _lane_col(x, lane, idx):
    return jnp.sum(jnp.where(lane == idx, x, 0.0), axis=1, keepdims=True)


def _head_cols(gbv, lo, nh):
    lane = lax.broadcasted_iota(jnp.int32, gbv.shape, 1)
    return jnp.stack([_lane_col(gbv, lane, lo + h) for h in range(nh)], axis=0)


def _gdn_chunk(q, k, v, g_col, beta_col, st):
    c = q.shape[1]
    r_i = lax.broadcasted_iota(jnp.int32, (c, c), 0)
    c_i = lax.broadcasted_iota(jnp.int32, (c, c), 1)
    incl = c_i <= r_i
    strict = c_i < r_i
    eye = c_i == r_i
    g_row = jnp.sum(jnp.where(eye, g_col, 0.0), axis=1, keepdims=True)
    gc_col = jnp.sum(jnp.where(incl, g_row, 0.0), axis=2, keepdims=True)
    gc_row = jnp.sum(jnp.where(eye, gc_col, 0.0), axis=1, keepdims=True)
    g_last = jnp.sum(g_col, axis=1, keepdims=True)
    decay = jnp.exp(jnp.where(incl, gc_col - gc_row, NEG))
    kk = _dot(k, k, BNT)
    a_low = jnp.where(strict, beta_col * kk * decay, 0.0)
    t_inv = _tri_inv(a_low, eye.astype(F32))
    e_g = jnp.exp(gc_col)
    bk = beta_col * e_g
    rhs = jnp.concatenate([v * beta_col, k * bk], axis=2)
    sol = _dot(t_inv, rhs, BNN)
    u, w = sol[:, :, :HEAD], sol[:, :, HEAD:]
    qk_raw = _dot(q, k, BNT)
    qk = qk_raw * decay
    q_dec = q * e_g
    e2 = jnp.exp(g_last - gc_col)
    k_dec = k * e2
    gl = jnp.exp(g_last)
    ws = _dot(jnp.concatenate([w, q_dec], axis=1), st, BNN)
    v_new = u - ws[:, :c]
    o = ws[:, c:] + _dot(qk, v_new, BNN)
    st_new = st * gl + _dot(k_dec, v_new, BTN)
    inter = dict(incl=incl, strict=strict, eye=eye, decay=decay, kk=kk, t_inv=t_inv, e_g=e_g, bk=bk, sol=sol, w=w,
                 qk_raw=qk_raw, qk=qk, q_dec=q_dec, e2=e2, k_dec=k_dec, gl=gl, v_new=v_new, c_i=c_i, r_i=r_i)
    return o, st_new, inter


def _gdn_fwd(qkv, gb, nh, name):
    s = qkv.shape[1]
    nc = s // CHUNK

    def body(q_ref, k_ref, v_ref, gb_ref, o_ref, st_ref, state):
        @pl.when(pl.program_id(0) == 0)
        def _():
            state[...] = jnp.zeros_like(state)

        gbv = gb_ref[...]
        st = state[...]
        st_ref[...] = st
        o, st_new, _ = _gdn_chunk(q_ref[...], k_ref[...], v_ref[...], _head_cols(gbv, 0, nh), _head_cols(gbv, nh, nh), st)
        o_ref[...] = o
        state[...] = st_new

    def qspec(part):
        return pl.BlockSpec((nh, CHUNK, HEAD), lambda n: (part, n, 0))

    return pl.pallas_call(
        body, name=name, grid=(nc,),
        in_specs=[qspec(0), qspec(1), qspec(2), pl.BlockSpec((CHUNK, HEAD), lambda n: (n, 0))],
        out_specs=[qspec(0), pl.BlockSpec((None, nh, HEAD, HEAD), lambda n: (n, 0, 0, 0))],
        out_shape=[jax.ShapeDtypeStruct((nh, s, HEAD), F32), jax.ShapeDtypeStruct((nc, nh, HEAD, HEAD), F32)],
        scratch_shapes=[pltpu.VMEM((nh, HEAD, HEAD), F32)],
        compiler_params=_params(("arbitrary",)))(qkv, qkv, qkv, gb)


def _gdn_bwd(qkv, gb, do, states, nh, name):
    s = qkv.shape[1]
    nc = s // CHUNK
    c = CHUNK

    def body(q_ref, k_ref, v_ref, gb_ref, do_ref, st_ref, dqkv_ref, dgb_ref, dstate):
        @pl.when(pl.program_id(0) == 0)
        def _():
            dstate[...] = jnp.zeros_like(dstate)

        gbv = gb_ref[...]
        lane = lax.broadcasted_iota(jnp.int32, gbv.shape, 1)
        q, k, v = q_ref[...], k_ref[...], v_ref[...]
        beta_col = _head_cols(gbv, nh, nh)
        st = st_ref[...]
        dst = dstate[...]
        dov = do_ref[...]
        _, _, it = _gdn_chunk(q, k, v, _head_cols(gbv, 0, nh), beta_col, st)
        incl, strict, eye, decay = it["incl"], it["strict"], it["eye"], it["decay"]
        dv_new = _dot(it["qk"], dov, BTN) + _dot(it["k_dec"], dst, BNN)
        d_qk = _dot(dov, it["v_new"], BNT)
        dd = _dot(jnp.concatenate([dov, -dv_new], axis=1), st, BNT)
        dq_dec, dw = dd[:, :c], dd[:, c:]
        dst_new = _dot(it["q_dec"], dov, BTN) + it["gl"] * dst - _dot(it["w"], dv_new, BTN)
        dgl = jnp.sum(jnp.sum(dst * st, axis=2, keepdims=True), axis=1, keepdims=True)
        dk_dec = _dot(it["v_new"], dst, BNT)
        dsol = jnp.concatenate([dv_new, dw], axis=2)
        drhs = _dot(it["t_inv"], dsol, BTN)
        d_a = jnp.where(strict, -_dot(drhs, it["sol"], BNT), 0.0)
        drhs_u, drhs_w = drhs[:, :, :HEAD], drhs[:, :, HEAD:]
        dvh = beta_col * drhs_u
        rw_k = jnp.sum(drhs_w * k, axis=2, keepdims=True)
        dbeta = jnp.sum(drhs_u * v, axis=2, keepdims=True) + it["e_g"] * rw_k
        dkh = it["bk"] * drhs_w
        dgc_col = it["bk"] * rw_k
        dkk = d_a * beta_col * decay
        dbeta = dbeta + jnp.sum(d_a * it["kk"] * decay, axis=2, keepdims=True)
        ddecay = d_a * beta_col * it["kk"]
        dkh = dkh + _dot(dkk, k, BNN) + _dot(dkk, k, BTN)
        dqk_raw = d_qk * decay
        ddecay = ddecay + d_qk * it["qk_raw"]
        dqh = _dot(dqk_raw, k, BNN)
        dkh = dkh + _dot(dqk_raw, q, BTN)
        ddm = jnp.where(incl, ddecay * decay, 0.0)
        dgc_col = dgc_col + jnp.sum(ddm, axis=2, keepdims=True)
        dgc_row = -jnp.sum(ddm, axis=1, keepdims=True)
        dqh = dqh + dq_dec * it["e_g"]
        dgc_col = dgc_col + jnp.sum(dq_dec * it["q_dec"], axis=2, keepdims=True)
        dkh = dkh + dk_dec * it["e2"]
        tmp = jnp.sum(dk_dec * it["k_dec"], axis=2, keepdims=True)
        dgc_col = dgc_col - tmp
        dg_last = jnp.sum(tmp, axis=1, keepdims=True) + dgl * it["gl"]
        dgc_tot_row = dgc_row + jnp.sum(jnp.where(eye, dgc_col, 0.0), axis=1, keepdims=True)
        dg_col = jnp.sum(jnp.where(it["c_i"] >= it["r_i"], dgc_tot_row, 0.0), axis=2, keepdims=True) + dg_last
        dqkv_ref[0] = dqh
        dqkv_ref[1] = dkh
        dqkv_ref[2] = dvh
        dstate[...] = dst_new
        dgb_acc = jnp.zeros(gbv.shape, F32)
        for h in range(nh):
            dgb_acc = jnp.where(lane == h, dg_col[h], jnp.where(lane == nh + h, dbeta[h], dgb_acc))
        dgb_ref[...] = dgb_acc

    def rev(part):
        return pl.BlockSpec((nh, CHUNK, HEAD), lambda n: (part, nc - 1 - n, 0))

    gspec = pl.BlockSpec((CHUNK, HEAD), lambda n: (nc - 1 - n, 0))
    dqkv, dgb = pl.pallas_call(
        body, name=name, grid=(nc,),
        in_specs=[rev(0), rev(1), rev(2), gspec, rev(0),
                  pl.BlockSpec((None, nh, HEAD, HEAD), lambda n: (nc - 1 - n, 0, 0, 0))],
        out_specs=[pl.BlockSpec((3, nh, CHUNK, HEAD), lambda n: (0, 0, nc - 1 - n, 0)), gspec],
        out_shape=[jax.ShapeDtypeStruct((3, nh, s, HEAD), F32), jax.ShapeDtypeStruct((s, HEAD), F32)],
        scratch_shapes=[pltpu.VMEM((nh, HEAD, HEAD), F32)],
        compiler_params=_params(("arbitrary",)))(qkv, qkv, qkv, gb, do, states)
    return dqkv.reshape(3 * nh, s, HEAD), dgb


SB_TQ = 512


def _tri01(rel):
    j_i = lax.broadcasted_iota(jnp.int32, (SBLK, SBLK), 0)
    s_i = lax.broadcasted_iota(jnp.int32, (SBLK, SBLK), 1)
    return rel(j_i, s_i).astype(BF16)


def _sb_scores(qt, kblk, mask, csum, rhs01):
    z = _dot(qt, kblk, NT)
    e = jnp.exp(-jnp.abs(z))
    sp = jnp.maximum(z, 0.0) + jnp.log(1.0 + e)
    ln = -sp if mask is None else jnp.where(mask, -sp, 0.0)
    st = _dot_hilo(ln, rhs01)
    wgt = jnp.exp((z - sp) + st[:, :SBLK] + csum)
    if mask is not None:
        wgt = jnp.where(mask, wgt, 0.0)
    return z, e, wgt, st


def _band_mask(rows, j, row0):
    r_i = lax.broadcasted_iota(jnp.int32, (rows, SBLK), 0)
    c_i = lax.broadcasted_iota(jnp.int32, (rows, SBLK), 1)
    return (j * SBLK + c_i) < (row0 + r_i)


def _sb_fwd(q, k, v, name):
    s, d = q.shape
    nh = d // HEAD
    tq = min(SB_TQ, s)
    nb = tq // SBLK

    def body(q_ref, k_ref, v_ref, o_ref, c_ref, acc, cs):
        qb = pl.program_id(1)
        lane = lax.broadcasted_iota(jnp.int32, (tq, HEAD), 1)
        rhs01 = jnp.concatenate([_tri01(lambda j, t: j > t), jnp.ones((SBLK, SBLK), BF16)], axis=1)
        acc[...] = jnp.zeros_like(acc)
        cs[...] = jnp.zeros_like(cs)
        c_ref[...] = jnp.zeros_like(c_ref)

        def process(rs, kb, mask):
            off = pl.multiple_of(kb * SBLK, SBLK)
            csum = cs[rs, :]
            _, _, wgt, st = _sb_scores(q_ref[rs, :], k_ref[pl.ds(off, SBLK), :], mask, csum, rhs01)
            acc[rs, :] += _dot(wgt, v_ref[pl.ds(off, SBLK), :])
            c_ref[rs, :] = jnp.where(lane[rs, :] == kb, csum, c_ref[rs, :])
            cs[rs, :] = csum + st[:, SBLK:]

        for j in reversed(range(nb)):
            process(slice(j * SBLK, tq), qb * nb + j, _band_mask(tq - j * SBLK, j, j * SBLK))

        def step(it, carry):
            process(slice(0, tq), qb * nb - 1 - it, None)
            return carry

        lax.fori_loop(0, qb * nb, step, 0)
        o_ref[...] = acc[...].astype(BF16)

    qspec = pl.BlockSpec((tq, HEAD), lambda h, i: (i, h))
    kspec = pl.BlockSpec((s, HEAD), lambda h, i: (0, h))
    return pl.pallas_call(
        body, name=name, grid=(nh, s // tq), in_specs=[qspec, kspec, kspec], out_specs=[qspec, qspec],
        out_shape=[jax.ShapeDtypeStruct((s, d), BF16), jax.ShapeDtypeStruct((s, d), F32)],
        scratch_shapes=[pltpu.VMEM((tq, HEAD), F32), pltpu.VMEM((tq, HEAD), F32)],
        compiler_params=_params(("parallel", "arbitrary")))(q, k, v)


def _sb_bwd(q, k, v, do, ctab, name):
    s, d = q.shape
    nh = d // HEAD
    tq = min(SB_TQ, s)
    nb = tq // SBLK

    def body(q_ref, k_ref, v_ref, do_ref, c_ref, dq_ref, dk_ref, dv_ref, ps):
        qb = pl.program_id(1)

        @pl.when(qb == 0)
        def _():
            dk_ref[...] = jnp.zeros_like(dk_ref)
            dv_ref[...] = jnp.zeros_like(dv_ref)

        dq_ref[...] = jnp.zeros_like(dq_ref)
        ps[...] = jnp.zeros_like(ps)
        lane = lax.broadcasted_iota(jnp.int32, (tq, HEAD), 1)
        after = _tri01(lambda j, t: j > t)
        rhs_pre = jnp.concatenate([_tri01(lambda j, t: j < t), jnp.ones((SBLK, SBLK), BF16)], axis=1)

        def process(rs, kb, mask):
            off = pl.multiple_of(kb * SBLK, SBLK)
            kblk = k_ref[pl.ds(off, SBLK), :]
            vblk = v_ref[pl.ds(off, SBLK), :]
            qt = q_ref[rs, :]
            dot_ = do_ref[rs, :]
            csum = _lane_col(c_ref[rs, :], lane[rs, :], kb)
            z, e, wgt, _ = _sb_scores(qt, kblk, mask, csum, after)
            dlw = _dot(dot_, vblk, NT) * wgt
            pt = _dot_hilo(dlw, rhs_pre)
            pfx = ps[rs, :]
            r = 1.0 / (1.0 + e)
            sig = jnp.where(z >= 0.0, r, e * r)
            dz = dlw * (1.0 - sig) - sig * (pfx + pt[:, :SBLK])
            if mask is not None:
                dz = jnp.where(mask, dz, 0.0)
            dq_ref[rs, :] += _dot(dz, kblk)
            dk_ref[pl.ds(off, SBLK), :] += _dot(dz, qt, TN)
            dv_ref[pl.ds(off, SBLK), :] += _dot(wgt, dot_, TN)
            ps[rs, :] = pfx + pt[:, SBLK:]

        def step(kb, carry):
            process(slice(0, tq), kb, None)
            return carry

        lax.fori_loop(0, qb * nb, step, 0)
        for j in range(nb):
            process(slice(j * SBLK, tq), qb * nb + j, _band_mask(tq - j * SBLK, j, j * SBLK))

    qspec = pl.BlockSpec((tq, HEAD), lambda h, i: (i, h))
    kspec = pl.BlockSpec((s, HEAD), lambda h, i: (0, h))
    sds = jax.ShapeDtypeStruct((s, d), F32)
    return pl.pallas_call(
        body, name=name, grid=(nh, s // tq), in_specs=[qspec, kspec, kspec, qspec, qspec],
        out_specs=[qspec, kspec, kspec], out_shape=[sds, sds, sds],
        scratch_shapes=[pltpu.VMEM((tq, HEAD), F32)],
        compiler_params=_params(("parallel", "arbitrary")))(q, k, v, do, ctab)


def _my_index():
    return 4 * lax.axis_index("x") + 2 * lax.axis_index("y") + lax.axis_index("c")


def _all_gather(x_shard, name):
    m_per, n = x_shard.shape

    def body(x_ref, out_ref, send_sems, recv_sems, local_sem):
        x, y, c = lax.axis_index("x"), lax.axis_index("y"), lax.axis_index("c")
        me, sibling = (x, y, c), (x, y, 1 - c)
        chips = [(1 - x, y), (x, 1 - y), (1 - x, 1 - y)]

        def rows(px, py, pc):
            return out_ref.at[pl.ds((4 * px + 2 * py + pc) * m_per, m_per), :]

        def copy(k, block, to, src=None):
            return pltpu.make_async_remote_copy(
                src_ref=rows(*block) if src is None else src, dst_ref=rows(*block),
                send_sem=send_sems.at[k], recv_sem=recv_sems.at[k], device_id=to, device_id_type=MESH)

        mine = pltpu.make_async_copy(x_ref, rows(*me), local_sem)
        mine.start()
        first = [copy(0, me, sibling, src=x_ref)]
        first += [copy(1 + j, me, (*chip, c), src=x_ref) for j, chip in enumerate(chips)]
        for cp in first:
            cp.start()
        passed = [copy(4 + j, (*chip, c), sibling) for j, chip in enumerate(chips)]
        for j, chip in enumerate(chips):
            copy(1 + j, (*chip, c), me).wait_recv()
            passed[j].start()
        copy(0, sibling, me).wait_recv()
        for j, chip in enumerate(chips):
            copy(4 + j, (*chip, 1 - c), me).wait_recv()
        for cp in first + passed:
            cp.wait_send()
        mine.wait()

    return pl.pallas_call(
        body, name=name, out_shape=jax.ShapeDtypeStruct((NDEV * m_per, n), x_shard.dtype),
        in_specs=[pl.BlockSpec(memory_space=pl.ANY)], out_specs=pl.BlockSpec(memory_space=pl.ANY),
        scratch_shapes=[pltpu.SemaphoreType.DMA((7,)), pltpu.SemaphoreType.DMA((7,)), pltpu.SemaphoreType.DMA],
    )(x_shard)


HBM_SPEC = pl.BlockSpec(memory_space=pltpu.HBM)
SEM_SPEC = pl.BlockSpec(memory_space=pltpu.SEMAPHORE)
ANY_SPEC = pl.BlockSpec(memory_space=pl.ANY)
EFFECT = pltpu.SideEffectType.DATAFLOW_SIDE_EFFECTING


def _peer_copies(src_ref, land_ref, send_sems, recv_sems, rows, scatter):
    x, y, c = lax.axis_index("x"), lax.axis_index("y"), lax.axis_index("c")
    me = 4 * x + 2 * y + c
    copies = []
    for k in range(1, NDEV):
        px, py, pc = x ^ ((k >> 2) & 1), y ^ ((k >> 1) & 1), c ^ (k & 1)
        src = src_ref.at[pl.ds((4 * px + 2 * py + pc) * rows, rows), :] if scatter else src_ref
        copies.append(pltpu.make_async_remote_copy(
            src_ref=src, dst_ref=land_ref.at[pl.ds(me * rows, rows), :], send_sem=send_sems.at[k - 1],
            recv_sem=recv_sems.at[k - 1], device_id=(px, py, pc), device_id_type=MESH))
    return copies


def _send_start(src, scatter, after, name):
    rows = src.shape[0] // NDEV if scatter else src.shape[0]
    land = pltpu.with_memory_space_constraint(lax.empty((NDEV * rows, src.shape[1]), src.dtype), pltpu.HBM)

    def body(src_ref, land_ref, after_ref, send_sems, recv_sems, src_thru, land_thru, token):
        for cp in _peer_copies(src_ref, land_ref, send_sems, recv_sems, rows, scatter):
            cp.start()
        token[...] = jnp.zeros_like(token)

    return pl.pallas_call(
        body, name=name,
        out_shape=(pltpu.SemaphoreType.DMA((NDEV - 1,)), pltpu.SemaphoreType.DMA((NDEV - 1,)),
                   pltpu.HBM(src.shape, src.dtype), pltpu.HBM(land.shape, land.dtype), jax.ShapeDtypeStruct((8, HEAD), F32)),
        in_specs=(HBM_SPEC, HBM_SPEC, ANY_SPEC),
        out_specs=(SEM_SPEC, SEM_SPEC, HBM_SPEC, HBM_SPEC, pl.BlockSpec(memory_space=pltpu.VMEM)),
        input_output_aliases={0: 2, 1: 3}, compiler_params=pltpu.CompilerParams(has_side_effects=EFFECT),
    )(pltpu.with_memory_space_constraint(src, pltpu.HBM), land, after)


def _send_wait(started, scatter, after, name):
    send_sems, recv_sems, src_thru, land_thru, _ = started
    rows = land_thru.shape[0] // NDEV

    def body(src_ref, land_ref, send_sems, recv_sems, after_ref, src_dead, got_ref):
        for cp in _peer_copies(src_ref, land_ref, send_sems, recv_sems, rows, scatter):
            cp.wait_send()
            cp.wait_recv()

    return pl.pallas_call(
        body, name=name,
        out_shape=(pltpu.HBM(src_thru.shape, src_thru.dtype), pltpu.HBM(land_thru.shape, land_thru.dtype)),
        in_specs=(HBM_SPEC, HBM_SPEC, SEM_SPEC, SEM_SPEC, ANY_SPEC), out_specs=(HBM_SPEC, HBM_SPEC),
        input_output_aliases={0: 0, 1: 1}, compiler_params=pltpu.CompilerParams(has_side_effects=EFFECT),
    )(src_thru, land_thru, send_sems, recv_sems, after)[1]


def _sum_slots(x, name):
    _, r, c = x.shape
    tr = _pick(r, (512, 256, 128, 64, 32, 16, 8))

    def body(x_ref, o_ref):
        acc = x_ref[0].astype(F32)
        for i in range(1, NDEV):
            acc = acc + x_ref[i].astype(F32)
        o_ref[...] = acc

    return pl.pallas_call(
        body, name=name, grid=(r // tr,), in_specs=[pl.BlockSpec((NDEV, tr, c), lambda i: (0, i, 0))],
        out_specs=pl.BlockSpec((tr, c), lambda i: (i, 0)), out_shape=jax.ShapeDtypeStruct((r, c), F32),
        compiler_params=_params(("parallel",)))(x)


def _adamw(w, g, m, v, name):
    r, c = w.shape
    tr = _pick(r, (256, 128, 64, 32, 16, 8))
    c1 = 1.0 - B1 ** STEP
    c2 = 1.0 - B2 ** STEP

    def body(w_ref, g_ref, m_ref, v_ref, d_ref, nm_ref, nv_ref):
        gv = g_ref[...]
        nm = B1 * m_ref[...] + (1.0 - B1) * gv
        nv = B2 * v_ref[...] + (1.0 - B2) * (gv * gv)
        d_ref[...] = -LR * ((nm / c1) / (jnp.sqrt(nv / c2) + ADAM_EPS) + WD * w_ref[...])
        nm_ref[...] = nm
        nv_ref[...] = nv

    blk = pl.BlockSpec((tr, c), lambda i: (i, 0))
    sds = jax.ShapeDtypeStruct((r, c), F32)
    return pl.pallas_call(
        body, name=name, grid=(r // tr,), in_specs=[blk] * 4, out_specs=[blk] * 3, out_shape=[sds] * 3,
        compiler_params=_params(("parallel",)))(w, g, m, v)


def _pad_rows(a, mult):
    r = a.shape[0]
    pad = (-r) % mult
    return a if pad == 0 else jnp.pad(a, ((0, pad), (0, 0)))


def _pad_lanes(v, width=HEAD):
    return jnp.pad(v.reshape(1, -1), ((0, 0), (0, width - v.shape[-1])))


def kernel(x, p, ln_mix, ln_ffn, ln_ple, gdn_w_in, gdn_conv, gdn_a_log, gdn_dt_bias, gdn_norm, gdn_w_out, kv_norm, w_kv, k_norm, sb_w_q, sb_q_norm, sb_w_out, ffn_w_in, ffn_w_out, ple_w_proj, ple_w_gate, loss_target, m_ln_mix, m_ln_ffn, m_ln_ple, m_gdn_w_in, m_gdn_conv, m_gdn_a_log, m_gdn_dt_bias, m_gdn_norm, m_gdn_w_out, m_kv_norm, m_w_kv, m_k_norm, m_sb_w_q, m_sb_q_norm, m_sb_w_out, m_ffn_w_in, m_ffn_w_out, m_ple_w_proj, m_ple_w_gate, v_ln_mix, v_ln_ffn, v_ln_ple, v_gdn_w_in, v_gdn_conv, v_gdn_a_log, v_gdn_dt_bias, v_gdn_norm, v_gdn_w_out, v_kv_norm, v_w_kv, v_k_norm, v_sb_w_q, v_sb_q_norm, v_sb_w_out, v_ffn_w_in, v_ffn_w_out, v_ple_w_proj, v_ple_w_gate):
    s, d = x.shape[1], x.shape[2]
    nh = d // HEAD
    depth = ln_mix.shape[0]
    n_a = gdn_w_in.shape[0]
    n_b = sb_w_q.shape[0]
    me = _my_index()
    win_cols = gdn_w_in.shape[2]
    win_rows = 4 * d + 2 * nh

    def col_t(w):
        return jnp.transpose(w).astype(BF16)

    local = {}
    for l in range(n_a):
        local[("gdn_w_in", l)] = col_t(gdn_w_in[l])
        local[("gdn_w_out", l)] = gdn_w_out[l].astype(BF16)
    local[("w_kv", 0)] = col_t(w_kv)
    for j in range(n_b):
        local[("sb_w_q", j)] = sb_w_q[j].astype(BF16)
        local[("sb_w_out", j)] = sb_w_out[j].astype(BF16)
    for l in range(depth):
        local[("ffn_w_in", l)] = col_t(ffn_w_in[l])
        local[("ffn_w_out", l)] = ffn_w_out[l].astype(BF16)
        local[("ple_w_proj", l)] = col_t(ple_w_proj[l]).reshape(-1, d)
        local[("ple_w_gate", l)] = ple_w_gate[l].astype(BF16)
    local = {key: _pad_rows(a, 16) for key, a in local.items()}

    chunks = []
    for l in range(depth):
        mix = [("gdn_w_in", l), ("gdn_w_out", l)] if l < n_a else [("sb_w_q", l - n_a), ("sb_w_out", l - n_a)]
        rest = [("ffn_w_in", l), ("ffn_w_out", l), ("ple_w_proj", l), ("ple_w_gate", l)]
        if l == n_a - 1:
            rest.append(("w_kv", 0))
        chunks += [(f"a{l}", mix), (f"f{l}", rest)]
    chunk_keys = dict(chunks)
    chunk_rows = {name: sum(local[k].shape[0] for k in keys) for name, keys in chunks}

    token = jnp.zeros((8, HEAD), F32)
    w_started, w_pack = {}, {}
    for name, keys in chunks:
        w_pack[name] = jnp.concatenate([local[k] for k in keys], axis=0)
        w_started[name] = _send_start(w_pack[name], False, token, f"comm_wstart_{name}")
        token = w_started[name][4]

    full = {}

    def fetch(name, after):
        land = _send_wait(w_started[name], False, after, f"comm_wwait_{name}")
        land = lax.dynamic_update_slice(land, w_pack[name], (me * chunk_rows[name], 0))
        land = land.reshape(NDEV, chunk_rows[name], d)
        off = 0
        for key in chunk_keys[name]:
            r = local[key].shape[0]
            full[key] = land[:, off:off + r, :]
            off += r

    def whole(key, valid=None):
        a = full[key]
        if valid is not None:
            a = a[:, :valid, :]
        return a.reshape(-1, d)

    pd = p.shape[-1]
    w_in_t, w_ab_t, w_gout, w_q, w_sout, wf_t, w_fout, wp_t, w_pg = {}, {}, {}, {}, {}, {}, {}, {}, {}
    wkv_t = None

    conv_rows = n_a * gdn_conv.shape[1]
    conv_sh = _pad_rows(gdn_conv.reshape(conv_rows, -1), 8)
    conv_g = _all_gather(conv_sh, "comm_gather_conv").reshape(NDEV, conv_sh.shape[0], -1)
    conv_full = jnp.transpose(conv_g[:, :conv_rows, :], (1, 0, 2)).reshape(n_a, gdn_conv.shape[1], 3 * d)

    h = x[0]
    sv = []
    kv_sv = None
    k_sh = v_sh = None
    for l in range(depth):
        t = {}
        t["h0"] = h
        hn = _rms_fwd(h, ln_mix[l], f"rms_mix_{l}")
        t["hn"] = hn
        fetch(f"a{l}", token if l == 0 else hn)
        if l < n_a:
            wt = whole(("gdn_w_in", l), win_cols)
            w_in_t[l] = wt[:4 * d]
            w_ab_t[l] = jnp.pad(wt[4 * d:], ((0, HEAD - 2 * nh), (0, 0)))
            w_gout[l] = whole(("gdn_w_out", l))
        else:
            w_q[l - n_a] = whole(("sb_w_q", l - n_a))
            w_sout[l - n_a] = whole(("sb_w_out", l - n_a))
        if l < n_a:
            proj = _mm(hn, w_in_t[l], "nt", f"gdn_proj_{l}")
            pab = _mm(hn, w_ab_t[l], "nt", f"gdn_proj_ab_{l}")
            qkv = _conv_fwd(proj, conv_full[l], d, f"gdn_conv_{l}")
            al, dtb = _pad_lanes(gdn_a_log[l]), _pad_lanes(gdn_dt_bias[l])
            gb = _gates_fwd(pab, al, dtb, nh, f"gdn_gates_{l}")
            o_raw, states = _gdn_fwd(qkv, gb, nh, f"gdn_rule_{l}")
            o2 = _headnorm_fwd(o_raw, gdn_norm[l], f"gdn_outnorm_{l}", gate=proj, gate_col0=3 * d, head_major=True)
            h = _mm(o2, w_gout[l], "nn", f"gdn_out_{l}", res=h)
            t.update(proj=proj, pab=pab, qkv=qkv, gb=gb, o_raw=o_raw, states=states, o2=o2, al=al, dtb=dtb)
        else:
            j = l - n_a
            qpre = _mm(hn, w_q[j], "nn", f"sb_qproj_{j}")
            qn = _headnorm_fwd(qpre, sb_q_norm[j], f"sb_qnorm_{j}", scale=HEAD ** -0.5)
            o, ctab = _sb_fwd(qn, k_sh, v_sh, f"sb_attn_{j}")
            h = _mm(o, w_sout[j], "nn", f"sb_out_{j}", res=h)
            t.update(qpre=qpre, qn=qn, o=o, ctab=ctab)
        t["h1"] = h
        hn2 = _rms_fwd(h, ln_ffn[l], f"rms_ffn_{l}")
        fetch(f"f{l}", hn2)
        wf_t[l] = whole(("ffn_w_in", l))
        w_fout[l] = whole(("ffn_w_out", l))
        wp_t[l] = full[("ple_w_proj", l)].reshape(d, pd)
        w_pg[l] = whole(("ple_w_gate", l))
        if l == n_a - 1:
            wkv_t = whole(("w_kv", 0))
        act, gs, us = _swiglu_fwd(hn2, wf_t[l], f"ffn_in_{l}")
        h = _mm(act, w_fout[l], "nn", f"ffn_out_{l}", res=h)
        t.update(hn2=hn2, act=act, gs=gs, us=us, h2=h)
        hn3 = _rms_fwd(h, ln_ple[l], f"rms_ple_{l}")
        h, gpre, pp = _ple_fwd(h, hn3, p[l, 0], w_pg[l], wp_t[l], f"ple_{l}")
        t.update(hn3=hn3, gpre=gpre, pp=pp)
        sv.append(t)
        if l == n_a - 1:
            kvn = _rms_fwd(h, kv_norm, "rms_kv")
            kv = _mm(kvn, wkv_t, "nt", "kv_proj")
            k_sh = _headnorm_fwd(kv, k_norm, "k_norm", width=d)
            v_sh = kv[:, d:].astype(BF16)
            kv_sv = dict(h=h, kvn=kvn, kv=kv)

    dh, loss_vec = _loss_fwd_bwd(h, loss_target[0], "loss")
    loss = lax.psum(jnp.sum(loss_vec), ("x", "y", "c"))

    gw = {}
    small = {}
    g_started, g_pack = {}, {}

    def scatter_start(name):
        gparts = []
        for key in chunk_keys[name]:
            g = gw[key]
            g = g.reshape(NDEV, -1, d) if key[0] == "ple_w_proj" else g.reshape(NDEV, -1, g.shape[-1])
            padr = local[key].shape[0] - g.shape[1]
            if padr:
                g = jnp.pad(g, ((0, 0), (0, padr), (0, 0)))
            gparts.append(g)
        g_pack[name] = jnp.concatenate(gparts, axis=1).reshape(NDEV * chunk_rows[name], d)
        g_started[name] = _send_start(g_pack[name], True, gparts[0], f"comm_gstart_{name}")
        return g_started[name][4]

    dk_sh = jnp.zeros((s, d), F32)
    dv_sh = jnp.zeros((s, d), F32)
    for l in reversed(range(depth)):
        t = sv[l]
        if l == n_a - 1:
            dkv_k, dkn = _headnorm_bwd(dk_sh, kv_sv["kv"], k_norm, "k_norm_bwd", dx_dtype=BF16)
            dkv = jnp.concatenate([dkv_k, dv_sh.astype(BF16)], axis=1)
            gw[("w_kv", 0)] = _mm(dkv, kv_sv["kvn"], "tn", "kv_dw", out_dtype=BF16)
            dkvn = _mm(dkv, wkv_t, "nn", "kv_dx")
            dh, dg = _rms_bwd(dkvn, kv_sv["h"], kv_norm, dh, "rms_kv_bwd")
            small["kv_norm"] = dg
            small["k_norm"] = dkn
        dgp, dpp = _ple_bwd(dh, t["gpre"], t["pp"], f"ple_bwd_{l}")
        gw[("ple_w_gate", l)] = _mm(t["hn3"], dgp, "tn", f"ple_dwg_{l}", out_dtype=BF16)
        gw[("ple_w_proj", l)] = _mm(dpp, p[l, 0], "tn", f"ple_dwp_{l}", out_dtype=BF16)
        dhn3 = _mm(dgp, w_pg[l], "nt", f"ple_dx_{l}")
        dh, dg = _rms_bwd(dhn3, t["h2"], ln_ple[l], dh, f"rms_ple_bwd_{l}")
        small[("ln_ple", l)] = dg
        dgs, dus = _swiglu_bwd(dh, w_fout[l], t["gs"], t["us"], f"ffn_bwd_act_{l}")
        gw[("ffn_w_out", l)] = _mm(t["act"], dh, "tn", f"ffn_dwo_{l}", out_dtype=BF16)
        f = dgs.shape[1]
        dwg = _mm(dgs, t["hn2"], "tn", f"ffn_dwg_{l}", out_dtype=BF16)
        dwu = _mm(dus, t["hn2"], "tn", f"ffn_dwu_{l}", out_dtype=BF16)
        gw[("ffn_w_in", l)] = jnp.concatenate([dwg, dwu], axis=0)
        dhn2 = _mm(dgs, wf_t[l][:f], "nn", f"ffn_dxg_{l}")
        dhn2 = _mm(dus, wf_t[l][f:], "nn", f"ffn_dxu_{l}", res=dhn2)
        dh, dg = _rms_bwd(dhn2, t["h1"], ln_ffn[l], dh, f"rms_ffn_bwd_{l}", after=scatter_start(f"f{l}"))
        small[("ln_ffn", l)] = dg
        if l < n_a:
            do2 = _mm(dh, w_gout[l], "nt", f"gdn_out_dx_{l}")
            gw[("gdn_w_out", l)] = _mm(t["o2"], dh, "tn", f"gdn_out_dw_{l}", out_dtype=BF16)
            do_raw, dgn, dgate = _headnorm_bwd(do2, t["o_raw"], gdn_norm[l], f"gdn_outnorm_bwd_{l}",
                                               gate=t["proj"], gate_col0=3 * d, head_major=True)
            small[("gdn_norm", l)] = dgn
            dqkv, dgb = _gdn_bwd(t["qkv"], t["gb"], do_raw, t["states"], nh, f"gdn_rule_bwd_{l}")
            dpab, dal, ddt = _gates_bwd(dgb, t["pab"], t["al"], t["dtb"], nh, f"gdn_gates_bwd_{l}")
            small[("gdn_a_log", l)] = dal
            small[("gdn_dt_bias", l)] = ddt
            dproj_qkv, dconv = _conv_bwd(dqkv, t["proj"], conv_full[l], d, f"gdn_conv_bwd_{l}")
            small[("gdn_conv", l)] = dconv
            dproj = jnp.concatenate([dproj_qkv, dgate], axis=1)
            dw_main = _mm(dproj, t["hn"], "tn", f"gdn_proj_dw_{l}", out_dtype=BF16)
            dw_ab = _mm(dpab, t["hn"], "tn", f"gdn_proj_ab_dw_{l}", out_dtype=BF16)
            gw[("gdn_w_in", l)] = jnp.concatenate([dw_main, dw_ab[:16]], axis=0)[:win_rows]
            dhn = _mm(dproj, w_in_t[l], "nn", f"gdn_proj_dx_{l}")
            dhn = _mm(dpab, w_ab_t[l], "nn", f"gdn_proj_ab_dx_{l}", res=dhn)
        else:
            j = l - n_a
            do = _mm(dh, w_sout[j], "nt", f"sb_out_dx_{j}", out_dtype=BF16)
            gw[("sb_w_out", j)] = _mm(t["o"], dh, "tn", f"sb_out_dw_{j}", out_dtype=BF16)
            dq, dk, dv = _sb_bwd(t["qn"], k_sh, v_sh, do, t["ctab"], f"sb_attn_bwd_{j}")
            dk_sh = dk_sh + dk
            dv_sh = dv_sh + dv
            dqpre, dqn = _headnorm_bwd(dq, t["qpre"], sb_q_norm[j], f"sb_qnorm_bwd_{j}", scale=HEAD ** -0.5, dx_dtype=BF16)
            small[("sb_q_norm", j)] = dqn
            gw[("sb_w_q", j)] = _mm(t["hn"], dqpre, "tn", f"sb_q_dw_{j}", out_dtype=BF16)
            dhn = _mm(dqpre, w_q[j], "nt", f"sb_q_dx_{j}")
        dh, dg = _rms_bwd(dhn, t["h0"], ln_mix[l], dh, f"rms_mix_bwd_{l}", after=scatter_start(f"a{l}"))
        small[("ln_mix", l)] = dg
    grad_x = dh[None]

    gshard = {}
    for name, keys in reversed(chunks):
        rc = chunk_rows[name]
        recv = _send_wait(g_started[name], True, dh, f"comm_gwait_{name}")
        own = lax.dynamic_slice_in_dim(g_pack[name], me * rc, rc, axis=0)
        recv = lax.dynamic_update_slice(recv, own, (me * rc, 0))
        gsum = _sum_slots(recv.reshape(NDEV, rc, d), f"grad_sum_{name}")
        off = 0
        for key in keys:
            r = local[key].shape[0]
            gshard[key] = gsum[off:off + r]
            off += r

    def col_back(key, n_valid):
        return jnp.transpose(gshard[key][:n_valid])

    g_gdn_w_in = jnp.stack([col_back(("gdn_w_in", l), win_cols) for l in range(n_a)])
    g_gdn_w_out = jnp.stack([gshard[("gdn_w_out", l)] for l in range(n_a)])
    g_w_kv = col_back(("w_kv", 0), w_kv.shape[1])
    g_sb_w_q = jnp.stack([gshard[("sb_w_q", j)] for j in range(n_b)])
    g_sb_w_out = jnp.stack([gshard[("sb_w_out", j)] for j in range(n_b)])
    g_ffn_w_in = jnp.stack([col_back(("ffn_w_in", l), ffn_w_in.shape[2]) for l in range(depth)])
    g_ffn_w_out = jnp.stack([gshard[("ffn_w_out", l)] for l in range(depth)])
    g_ple_w_proj = jnp.stack([jnp.transpose(gshard[("ple_w_proj", l)].reshape(-1, pd)) for l in range(depth)])
    g_ple_w_gate = jnp.stack([gshard[("ple_w_gate", l)] for l in range(depth)])

    def vec_rows(v):
        return v.reshape(-1, HEAD)

    small_items = []
    for name_, cnt in (("ln_mix", depth), ("ln_ffn", depth), ("ln_ple", depth)):
        for l in range(cnt):
            small_items.append(((name_, l), vec_rows(small[(name_, l)])))
    for l in range(n_a):
        small_items.append((("gdn_conv", l), small[("gdn_conv", l)].reshape(-1, HEAD)))
        small_items.append((("gdn_a_log", l), small[("gdn_a_log", l)]))
        small_items.append((("gdn_dt_bias", l), small[("gdn_dt_bias", l)]))
        small_items.append((("gdn_norm", l), small[("gdn_norm", l)]))
    small_items.append(("kv_norm", vec_rows(small["kv_norm"])))
    small_items.append(("k_norm", small["k_norm"]))
    for j in range(n_b):
        small_items.append((("sb_q_norm", j), small[("sb_q_norm", j)]))
    spack = jnp.concatenate([_pad_rows(a, 8) for _, a in small_items], axis=0)
    sg = _all_gather(spack, "comm_gather_small").reshape(NDEV, spack.shape[0], HEAD)
    ssum = _sum_slots(sg, "small_sum")
    sm = {}
    off = 0
    for key, a in small_items:
        sm[key] = ssum[off:off + a.shape[0]]
        off += a.shape[0] + (-a.shape[0]) % 8

    g_ln_mix = jnp.stack([sm[("ln_mix", l)].reshape(d) for l in range(depth)])
    g_ln_ffn = jnp.stack([sm[("ln_ffn", l)].reshape(d) for l in range(depth)])
    g_ln_ple = jnp.stack([sm[("ln_ple", l)].reshape(d) for l in range(depth)])
    conv_loc = gdn_conv.shape[2]
    g_conv_full = jnp.stack([sm[("gdn_conv", l)].reshape(gdn_conv.shape[1], 3 * d) for l in range(n_a)])
    g_gdn_conv = lax.dynamic_slice_in_dim(g_conv_full, me * conv_loc, conv_loc, axis=2)
    g_a_log = jnp.stack([sm[("gdn_a_log", l)][0, :nh] for l in range(n_a)])
    g_dt_bias = jnp.stack([sm[("gdn_dt_bias", l)][0, :nh] for l in range(n_a)])
    g_gdn_norm = jnp.stack([sm[("gdn_norm", l)][0] for l in range(n_a)])
    g_kv_norm = sm["kv_norm"].reshape(d)
    g_k_norm = sm["k_norm"][0]
    g_sb_q_norm = jnp.stack([sm[("sb_q_norm", j)][0] for j in range(n_b)])

    grads = [g_ln_mix, g_ln_ffn, g_ln_ple, g_gdn_w_in, g_gdn_conv, g_a_log, g_dt_bias, g_gdn_norm, g_gdn_w_out,
             g_kv_norm, g_w_kv, g_k_norm, g_sb_w_q, g_sb_q_norm, g_sb_w_out, g_ffn_w_in, g_ffn_w_out, g_ple_w_proj,
             g_ple_w_gate]
    weights = [ln_mix, ln_ffn, ln_ple, gdn_w_in, gdn_conv, gdn_a_log, gdn_dt_bias, gdn_norm, gdn_w_out, kv_norm, w_kv,
               k_norm, sb_w_q, sb_q_norm, sb_w_out, ffn_w_in, ffn_w_out, ple_w_proj, ple_w_gate]
    moms = [m_ln_mix, m_ln_ffn, m_ln_ple, m_gdn_w_in, m_gdn_conv, m_gdn_a_log, m_gdn_dt_bias, m_gdn_norm, m_gdn_w_out,
            m_kv_norm, m_w_kv, m_k_norm, m_sb_w_q, m_sb_q_norm, m_sb_w_out, m_ffn_w_in, m_ffn_w_out, m_ple_w_proj,
            m_ple_w_gate]
    vels = [v_ln_mix, v_ln_ffn, v_ln_ple, v_gdn_w_in, v_gdn_conv, v_gdn_a_log, v_gdn_dt_bias, v_gdn_norm, v_gdn_w_out,
            v_kv_norm, v_w_kv, v_k_norm, v_sb_w_q, v_sb_q_norm, v_sb_w_out, v_ffn_w_in, v_ffn_w_out, v_ple_w_proj,
            v_ple_w_gate]

    deltas, new_m, new_v = [], [], []
    small_idx = [i for i, w in enumerate(weights) if w.size < 8 * HEAD * 16]
    for i, (w, g, m, v) in enumerate(zip(weights, grads, moms, vels)):
        if i in small_idx:
            deltas.append(None), new_m.append(None), new_v.append(None)
            continue
        shp = w.shape
        two = lambda a: a.reshape(-1, shp[-1])
        dl, nm, nv = _adamw(two(w), two(g), two(m), two(v), f"adamw_{i}")
        deltas.append(dl.reshape(shp)), new_m.append(nm.reshape(shp)), new_v.append(nv.reshape(shp))

    def flat_pack(arrs):
        flat = jnp.concatenate([a.reshape(-1) for a in arrs])
        pad = (-flat.shape[0]) % (8 * HEAD)
        return jnp.pad(flat, (0, pad)).reshape(-1, HEAD)

    sw = flat_pack([weights[i] for i in small_idx])
    sgr = flat_pack([grads[i] for i in small_idx])
    smo = flat_pack([moms[i] for i in small_idx])
    sve = flat_pack([vels[i] for i in small_idx])
    sdl, snm, snv = _adamw(sw, sgr, smo, sve, "adamw_small")
    off = 0
    for i in small_idx:
        n = weights[i].size
        shp = weights[i].shape
        deltas[i] = sdl.reshape(-1)[off:off + n].reshape(shp)
        new_m[i] = snm.reshape(-1)[off:off + n].reshape(shp)
        new_v[i] = snv.reshape(-1)[off:off + n].reshape(shp)
        off += n

    return (loss, grad_x, *grads, *deltas, *new_m, *new_v)
```

```python
import functools
import math

import jax
import jax.numpy as jnp
from jax import lax
from jax.experimental import pallas as pl
from jax.experimental.pallas import tpu as pltpu

F32 = jnp.float32
BF16 = jnp.bfloat16
NDEV = 8
HEAD = 128
CHUNK = 64
SBLK = 256
EPS = 1e-6
LR, B1, B2, ADAM_EPS, WD, STEP = 0.001, 0.9, 0.999, 1e-08, 0.01, 10
NEG = -1e30
MM_VMEM_BUDGET = 36 * 1024 * 1024

NN = (((1,), (0,)), ((), ()))
NT = (((1,), (1,)), ((), ()))
TN = (((0,), (0,)), ((), ()))
BNN = (((2,), (1,)), ((0,), (0,)))
BNT = (((2,), (2,)), ((0,), (0,)))
BTN = (((1,), (1,)), ((0,), (0,)))
MESH = pl.DeviceIdType.MESH


def _dot(a, b, dims=NN):
    return lax.dot_general(a.astype(BF16), b.astype(BF16), dims, preferred_element_type=F32)


def _dot_hilo(a, b01, dims=NN):
    hi = a.astype(BF16)
    lo = (a - hi.astype(F32)).astype(BF16)
    return (lax.dot_general(hi, b01, dims, preferred_element_type=F32)
            + lax.dot_general(lo, b01, dims, preferred_element_type=F32))


def _pick(dim, cands):
    for c in cands:
        if dim % c == 0:
            return c
    return dim


def _params(sem, vmem_mb=48):
    return pltpu.CompilerParams(dimension_semantics=sem, vmem_limit_bytes=vmem_mb * 1024 * 1024)


def _silu(x):
    return x * jax.nn.sigmoid(x)


def _dsilu(x):
    s = jax.nn.sigmoid(x)
    return s * (1.0 + x * (1.0 - s))


def _mm(a, b, mode, name, out_dtype=F32, res=None):
    if mode == "nn":
        (m, k), n = a.shape, b.shape[1]
    elif mode == "nt":
        (m, k), n = a.shape, b.shape[0]
    else:
        (k, m), n = a.shape, b.shape[1]
    tn = _pick(n, (512, 256, 128))
    tk = k if k <= 4096 else _pick(k, (2048, 1024, 512, 256, 128))
    nk = k // tk
    out_b = jnp.dtype(out_dtype).itemsize + (res.dtype.itemsize if res is not None else 0)
    for tm in (1024, 512, 256, 128, m):
        need = 2 * (tm * tk * a.dtype.itemsize + tk * tn * b.dtype.itemsize + tm * tn * out_b) + 4 * tm * tn
        if m % tm == 0 and need <= MM_VMEM_BUDGET:
            break
    dims = {"nn": NN, "nt": NT, "tn": TN}[mode]
    if mode == "tn":
        a_spec = pl.BlockSpec((tk, tm), lambda i, j, kk: (kk, i))
    else:
        a_spec = pl.BlockSpec((tm, tk), lambda i, j, kk: (i, kk))
    if mode == "nt":
        b_spec = pl.BlockSpec((tn, tk), lambda i, j, kk: (j, kk))
    else:
        b_spec = pl.BlockSpec((tk, tn), lambda i, j, kk: (kk, j))
    mn_spec = pl.BlockSpec((tm, tn), lambda i, j, kk: (i, j))
    has_res = res is not None

    def body(*refs):
        if has_res:
            a_ref, b_ref, r_ref, o_ref, acc = refs
        else:
            a_ref, b_ref, o_ref, acc = refs
        kk = pl.program_id(2)

        @pl.when(kk == 0)
        def _():
            acc[...] = jnp.zeros_like(acc)

        acc[...] += _dot(a_ref[...], b_ref[...], dims)

        @pl.when(kk == nk - 1)
        def _():
            r = acc[...]
            if has_res:
                r = r + r_ref[...].astype(F32)
            o_ref[...] = r.astype(out_dtype)

    ins = [a, b] + ([res] if has_res else [])
    in_specs = [a_spec, b_spec] + ([mn_spec] if has_res else [])
    return pl.pallas_call(
        body, name=name, grid=(m // tm, n // tn, nk), in_specs=in_specs, out_specs=mn_spec,
        out_shape=jax.ShapeDtypeStruct((m, n), out_dtype), scratch_shapes=[pltpu.VMEM((tm, tn), F32)],
        compiler_params=_params(("parallel", "parallel", "arbitrary")))(*ins)


def _rms_fwd(h, g, name):
    s, d = h.shape
    tm = _pick(s, (512, 256, 128))

    def body(h_ref, g_ref, o_ref):
        x = h_ref[...]
        r = lax.rsqrt(jnp.mean(x * x, axis=-1, keepdims=True) + EPS)
        o_ref[...] = (x * r * g_ref[...]).astype(BF16)

    return pl.pallas_call(
        body, name=name, grid=(s // tm,),
        in_specs=[pl.BlockSpec((tm, d), lambda i: (i, 0)), pl.BlockSpec((1, d), lambda i: (0, 0))],
        out_specs=pl.BlockSpec((tm, d), lambda i: (i, 0)),
        out_shape=jax.ShapeDtypeStruct((s, d), BF16), compiler_params=_params(("parallel",)))(h, g.reshape(1, d))


def _rms_bwd(dy, x, g, dres, name, after=None):
    s, d = x.shape
    tm = _pick(s, (512, 256, 128))

    def body(dy_ref, x_ref, g_ref, dr_ref, *rest):
        dx_ref, dg_ref = rest[-2:]

        @pl.when(pl.program_id(0) == 0)
        def _():
            dg_ref[...] = jnp.zeros_like(dg_ref)

        xv = x_ref[...]
        dyv = dy_ref[...].astype(F32)
        r = lax.rsqrt(jnp.mean(xv * xv, axis=-1, keepdims=True) + EPS)
        gdy = dyv * g_ref[...]
        mean_t = jnp.mean(xv * gdy, axis=-1, keepdims=True)
        dx_ref[...] = dr_ref[...] + r * gdy - xv * (r * r * r) * mean_t
        dg_ref[...] += jnp.sum(dyv * xv * r, axis=0, keepdims=True)

    row = pl.BlockSpec((tm, d), lambda i: (i, 0))
    vec = pl.BlockSpec((1, d), lambda i: (0, 0))
    ins, in_specs = [dy, x, g.reshape(1, d), dres], [row, row, vec, row]
    if after is not None:
        ins.append(after)
        in_specs.append(pl.BlockSpec(memory_space=pl.ANY))
    return pl.pallas_call(
        body, name=name, grid=(s // tm,), in_specs=in_specs, out_specs=[row, vec],
        out_shape=[jax.ShapeDtypeStruct((s, d), F32), jax.ShapeDtypeStruct((1, d), F32)],
        compiler_params=_params(("arbitrary",)))(*ins)


def _headnorm_fwd(x, g, name, scale=1.0, gate=None, gate_col0=0, out_dtype=BF16, width=None, head_major=False):
    if head_major:
        s, d = x.shape[1], x.shape[0] * HEAD
    else:
        s, d = x.shape[0], (width or x.shape[1])
    nh = d // HEAD
    tm = _pick(s, (256, 128))
    has_gate = gate is not None
    gb = gate_col0 // d

    def body(*refs):
        if has_gate:
            x_ref, g_ref, gt_ref, o_ref = refs
        else:
            x_ref, g_ref, o_ref = refs
        gv = g_ref[...]
        for h in range(nh):
            sl = slice(h * HEAD, (h + 1) * HEAD)
            xv = (x_ref[h] if head_major else x_ref[:, sl]).astype(F32)
            r = lax.rsqrt(jnp.mean(xv * xv, axis=-1, keepdims=True) + EPS)
            y = xv * r * gv
            if scale != 1.0:
                y = y * scale
            if has_gate:
                y = y * _silu(gt_ref[:, sl])
            o_ref[:, sl] = y.astype(out_dtype)

    row = pl.BlockSpec((tm, d), lambda i: (i, 0))
    hm = pl.BlockSpec((nh, tm, HEAD), lambda i: (0, i, 0))
    ins = [x, g.reshape(1, HEAD)]
    in_specs = [hm if head_major else row, pl.BlockSpec((1, HEAD), lambda i: (0, 0))]
    if has_gate:
        ins.append(gate)
        in_specs.append(pl.BlockSpec((tm, d), lambda i: (i, gb)))
    return pl.pallas_call(
        body, name=name, grid=(s // tm,), in_specs=in_specs, out_specs=row,
        out_shape=jax.ShapeDtypeStruct((s, d), out_dtype), compiler_params=_params(("parallel",)))(*ins)


def _headnorm_bwd(dy, x, g, name, scale=1.0, gate=None, gate_col0=0, dx_dtype=F32, head_major=False):
    s, d = dy.shape
    nh = d // HEAD
    tm = _pick(s, (256, 128))
    has_gate = gate is not None
    gb = gate_col0 // d

    def body(*refs):
        if has_gate:
            dy_ref, x_ref, g_ref, gt_ref, dx_ref, dg_ref, dgt_ref = refs
        else:
            dy_ref, x_ref, g_ref, dx_ref, dg_ref = refs

        @pl.when(pl.program_id(0) == 0)
        def _():
            dg_ref[...] = jnp.zeros_like(dg_ref)

        gv = g_ref[...]
        dg_acc = jnp.zeros((1, HEAD), F32)
        for h in range(nh):
            sl = slice(h * HEAD, (h + 1) * HEAD)
            xv = (x_ref[h] if head_major else x_ref[:, sl]).astype(F32)
            dyv = dy_ref[:, sl].astype(F32)
            r = lax.rsqrt(jnp.mean(xv * xv, axis=-1, keepdims=True) + EPS)
            if has_gate:
                gt = gt_ref[:, sl]
                dgt_ref[:, sl] = (dyv * (xv * r * gv) * _dsilu(gt)).astype(dgt_ref.dtype)
                dn = dyv * _silu(gt)
            else:
                dn = dyv
            if scale != 1.0:
                dn = dn * scale
            gdn = dn * gv
            mean_t = jnp.mean(xv * gdn, axis=-1, keepdims=True)
            dxv = (r * gdn - xv * (r * r * r) * mean_t).astype(dx_dtype)
            if head_major:
                dx_ref[h] = dxv
            else:
                dx_ref[:, sl] = dxv
            dg_acc = dg_acc + jnp.sum(dn * xv * r, axis=0, keepdims=True)
        dg_ref[...] += dg_acc

    row = pl.BlockSpec((tm, d), lambda i: (i, 0))
    hm = pl.BlockSpec((nh, tm, HEAD), lambda i: (0, i, 0))
    vec = pl.BlockSpec((1, HEAD), lambda i: (0, 0))
    ins = [dy, x, g.reshape(1, HEAD)]
    in_specs = [row, hm if head_major else row, vec]
    out_specs = [hm if head_major else row, vec]
    dx_shape = (nh, s, HEAD) if head_major else (s, d)
    out_shape = [jax.ShapeDtypeStruct(dx_shape, dx_dtype), jax.ShapeDtypeStruct((1, HEAD), F32)]
    if has_gate:
        ins.append(gate)
        in_specs.append(pl.BlockSpec((tm, d), lambda i: (i, gb)))
        out_specs.append(row)
        out_shape.append(jax.ShapeDtypeStruct((s, d), BF16))
    return pl.pallas_call(
        body, name=name, grid=(s // tm,), in_specs=in_specs, out_specs=out_specs, out_shape=out_shape,
        compiler_params=_params(("arbitrary",)))(*ins)


def _swiglu_fwd(hn, wf_t, name):
    s, d = hn.shape
    f = wf_t.shape[0] // 2
    tm = _pick(s, (512, 256, 128))
    tn = _pick(f, (512, 256, 128))
    nj = f // tn

    def body(a_ref, wg_ref, wu_ref, act_ref, g_ref, u_ref):
        a = a_ref[...]
        g = _dot(a, wg_ref[...], NT)
        u = _dot(a, wu_ref[...], NT)
        act_ref[...] = (_silu(g) * u).astype(BF16)
        g_ref[...] = g.astype(BF16)
        u_ref[...] = u.astype(BF16)

    o_spec = pl.BlockSpec((tm, tn), lambda i, j: (i, j))
    sds = jax.ShapeDtypeStruct((s, f), BF16)
    return pl.pallas_call(
        body, name=name, grid=(s // tm, nj),
        in_specs=[pl.BlockSpec((tm, d), lambda i, j: (i, 0)), pl.BlockSpec((tn, d), lambda i, j: (j, 0)),
                  pl.BlockSpec((tn, d), lambda i, j: (j + nj, 0))],
        out_specs=[o_spec, o_spec, o_spec], out_shape=[sds, sds, sds],
        compiler_params=_params(("parallel", "parallel")))(hn, wf_t, wf_t)


def _swiglu_bwd(dh, w_out, g, u, name):
    s, d = dh.shape
    f = w_out.shape[0]
    tm = _pick(s, (512, 256, 128))
    tn = _pick(f, (512, 256, 128))

    def body(dh_ref, w_ref, g_ref, u_ref, dg_ref, du_ref):
        dact = _dot(dh_ref[...], w_ref[...], NT)
        gv = g_ref[...].astype(F32)
        uv = u_ref[...].astype(F32)
        dg_ref[...] = (dact * uv * _dsilu(gv)).astype(BF16)
        du_ref[...] = (dact * _silu(gv)).astype(BF16)

    o_spec = pl.BlockSpec((tm, tn), lambda i, j: (i, j))
    sds = jax.ShapeDtypeStruct((s, f), BF16)
    return pl.pallas_call(
        body, name=name, grid=(s // tm, f // tn),
        in_specs=[pl.BlockSpec((tm, d), lambda i, j: (i, 0)), pl.BlockSpec((tn, d), lambda i, j: (j, 0)), o_spec, o_spec],
        out_specs=[o_spec, o_spec], out_shape=[sds, sds],
        compiler_params=_params(("parallel", "parallel")))(dh, w_out, g, u)


def _ple_fwd(h, hn, p, w_gate, wp_t, name):
    s, d = h.shape
    pd = p.shape[1]
    tm = _pick(s, (512, 256, 128))
    tn = _pick(d, (512, 256, 128))

    def body(h_ref, hn_ref, p_ref, wg_ref, wp_ref, o_ref, gp_ref, pp_ref):
        gpre = _dot(hn_ref[...], wg_ref[...], NN)
        pp = _dot(p_ref[...], wp_ref[...], NT)
        o_ref[...] = h_ref[...] + pp * jax.nn.sigmoid(gpre)
        gp_ref[...] = gpre.astype(BF16)
        pp_ref[...] = pp.astype(BF16)

    mn = pl.BlockSpec((tm, tn), lambda i, j: (i, j))
    return pl.pallas_call(
        body, name=name, grid=(s // tm, d // tn),
        in_specs=[mn, pl.BlockSpec((tm, d), lambda i, j: (i, 0)), pl.BlockSpec((tm, pd), lambda i, j: (i, 0)),
                  pl.BlockSpec((d, tn), lambda i, j: (0, j)), pl.BlockSpec((tn, pd), lambda i, j: (j, 0))],
        out_specs=[mn, mn, mn],
        out_shape=[jax.ShapeDtypeStruct((s, d), F32), jax.ShapeDtypeStruct((s, d), BF16), jax.ShapeDtypeStruct((s, d), BF16)],
        compiler_params=_params(("parallel", "parallel")))(h, hn, p, w_gate, wp_t)


def _ple_bwd(dh, gpre, pp, name):
    s, d = dh.shape
    tm = _pick(s, (512, 256, 128))

    def body(dh_ref, gp_ref, pp_ref, dgp_ref, dpp_ref):
        dv = dh_ref[...]
        sig = jax.nn.sigmoid(gp_ref[...].astype(F32))
        ppv = pp_ref[...].astype(F32)
        dpp_ref[...] = (dv * sig).astype(BF16)
        dgp_ref[...] = (dv * ppv * sig * (1.0 - sig)).astype(BF16)

    row = pl.BlockSpec((tm, d), lambda i: (i, 0))
    sds = jax.ShapeDtypeStruct((s, d), BF16)
    return pl.pallas_call(
        body, name=name, grid=(s // tm,), in_specs=[row, row, row], out_specs=[row, row], out_shape=[sds, sds],
        compiler_params=_params(("parallel",)))(dh, gpre, pp)


def _loss_fwd_bwd(y, t, name):
    s, d = y.shape
    tm = _pick(s, (512, 256, 128))

    def body(y_ref, t_ref, dy_ref, l_ref):
        @pl.when(pl.program_id(0) == 0)
        def _():
            l_ref[...] = jnp.zeros_like(l_ref)

        e = y_ref[...] - t_ref[...]
        dy_ref[...] = e * (1.0 / d)
        l_ref[...] += jnp.sum(e * e, axis=0, keepdims=True) * (0.5 / d)

    row = pl.BlockSpec((tm, d), lambda i: (i, 0))
    vec = pl.BlockSpec((1, d), lambda i: (0, 0))
    return pl.pallas_call(
        body, name=name, grid=(s // tm,), in_specs=[row, row], out_specs=[row, vec],
        out_shape=[jax.ShapeDtypeStruct((s, d), F32), jax.ShapeDtypeStruct((1, d), F32)],
        compiler_params=_params(("arbitrary",)))(y, t)


PADR = 8


def _conv_fwd(proj, w_conv, d, name):
    s = proj.shape[0]
    nh = d // HEAD
    kw = w_conv.shape[0]
    qscale = HEAD ** -0.5

    def body(x_ref, w_ref, o_ref, xp):
        kind = pl.program_id(0) // nh
        xp[0:PADR, :] = jnp.zeros((PADR, HEAD), F32)
        xp[PADR:, :] = x_ref[...]
        acc = jnp.zeros((s, HEAD), F32)
        for j in range(kw):
            acc = acc + w_ref[j:j + 1, :] * xp[PADR - (kw - 1) + j:PADR - (kw - 1) + j + s, :]
        a = _silu(acc)
        r = lax.rsqrt(jnp.sum(a * a, axis=-1, keepdims=True) + EPS)
        fac = jnp.where(kind == 0, r * qscale, jnp.where(kind == 1, r, jnp.ones_like(r)))
        o_ref[...] = a * fac

    blk = pl.BlockSpec((s, HEAD), lambda c: (0, c))
    hm = pl.BlockSpec((None, s, HEAD), lambda c: (c, 0, 0))
    return pl.pallas_call(
        body, name=name, grid=(3 * nh,), in_specs=[blk, pl.BlockSpec((kw, HEAD), lambda c: (0, c))], out_specs=hm,
        out_shape=jax.ShapeDtypeStruct((3 * nh, s, HEAD), F32), scratch_shapes=[pltpu.VMEM((s + PADR, HEAD), F32)],
        compiler_params=_params(("parallel",)))(proj, w_conv)


def _conv_bwd(dqkv, proj, w_conv, d, name):
    s = proj.shape[0]
    nh = d // HEAD
    kw = w_conv.shape[0]
    qscale = HEAD ** -0.5

    def body(dy_ref, x_ref, w_ref, dx_ref, dw_ref, xp, dp):
        kind = pl.program_id(0) // nh
        xp[0:PADR, :] = jnp.zeros((PADR, HEAD), F32)
        xp[PADR:, :] = x_ref[...]
        acc = jnp.zeros((s, HEAD), F32)
        for j in range(kw):
            acc = acc + w_ref[j:j + 1, :] * xp[PADR - (kw - 1) + j:PADR - (kw - 1) + j + s, :]
        a = _silu(acc)
        dy = dy_ref[...]
        r = lax.rsqrt(jnp.sum(a * a, axis=-1, keepdims=True) + EPS)
        sc = jnp.where(kind == 0, qscale, 1.0)
        dyn = dy * sc
        da_norm = r * dyn - a * (r * r * r) * jnp.sum(a * dyn, axis=-1, keepdims=True)
        da = jnp.where(kind == 2, dy, da_norm)
        dacc = da * _dsilu(acc)
        dp[0:s, :] = dacc
        dp[s:, :] = jnp.zeros((PADR, HEAD), F32)
        dx = jnp.zeros((s, HEAD), F32)
        for j in range(kw):
            sh = kw - 1 - j
            dx = dx + w_ref[j:j + 1, :] * dp[sh:sh + s, :]
            dw_ref[j:j + 1, :] = jnp.sum(dacc * xp[PADR - sh:PADR - sh + s, :], axis=0, keepdims=True)
        dx_ref[...] = dx.astype(BF16)

    blk = pl.BlockSpec((s, HEAD), lambda c: (0, c))
    hm = pl.BlockSpec((None, s, HEAD), lambda c: (c, 0, 0))
    wblk = pl.BlockSpec((kw, HEAD), lambda c: (0, c))
    return pl.pallas_call(
        body, name=name, grid=(3 * nh,), in_specs=[hm, blk, wblk], out_specs=[blk, wblk],
        out_shape=[jax.ShapeDtypeStruct((s, 3 * d), BF16), jax.ShapeDtypeStruct((kw, 3 * d), F32)],
        scratch_shapes=[pltpu.VMEM((s + PADR, HEAD), F32), pltpu.VMEM((s + PADR, HEAD), F32)],
        compiler_params=_params(("parallel",)))(dqkv, proj, w_conv)


def _softplus(x):
    return jnp.maximum(x, 0.0) + jnp.log(1.0 + jnp.exp(-jnp.abs(x)))


def _gates_fwd(pab, a_log, dt_bias, nh, name):
    s = pab.shape[0]
    tm = _pick(s, (512, 256, 128))

    def body(x_ref, al_ref, dt_ref, o_ref):
        x = x_ref[...]
        lane = lax.broadcasted_iota(jnp.int32, x.shape, 1)
        g = -jnp.exp(al_ref[...]) * _softplus(x + dt_ref[...])
        o_ref[...] = jnp.where(lane < nh, g, jnp.where(lane < 2 * nh, jax.nn.sigmoid(x), 0.0))

    row = pl.BlockSpec((tm, HEAD), lambda i: (i, 0))
    vec = pl.BlockSpec((1, HEAD), lambda i: (0, 0))
    return pl.pallas_call(
        body, name=name, grid=(s // tm,), in_specs=[row, vec, vec], out_specs=row,
        out_shape=jax.ShapeDtypeStruct((s, HEAD), F32), compiler_params=_params(("parallel",)))(pab, a_log, dt_bias)


def _gates_bwd(dgb, pab, a_log, dt_bias, nh, name):
    s = pab.shape[0]
    tm = _pick(s, (512, 256, 128))

    def body(d_ref, x_ref, al_ref, dt_ref, dx_ref, dal_ref, ddt_ref):
        @pl.when(pl.program_id(0) == 0)
        def _():
            dal_ref[...] = jnp.zeros_like(dal_ref)
            ddt_ref[...] = jnp.zeros_like(ddt_ref)

        x = x_ref[...]
        dv = d_ref[...]
        lane = lax.broadcasted_iota(jnp.int32, x.shape, 1)
        ea = jnp.exp(al_ref[...])
        xs = x + dt_ref[...]
        g = -ea * _softplus(xs)
        dxs = jnp.where(lane < nh, dv * (-ea) * jax.nn.sigmoid(xs), 0.0)
        sg = jax.nn.sigmoid(x)
        dxb = jnp.where((lane >= nh) & (lane < 2 * nh), dv * sg * (1.0 - sg), 0.0)
        dx_ref[...] = (dxs + dxb).astype(BF16)
        dal_ref[...] += jnp.sum(jnp.where(lane < nh, dv * g, 0.0), axis=0, keepdims=True)
        ddt_ref[...] += jnp.sum(dxs, axis=0, keepdims=True)

    row = pl.BlockSpec((tm, HEAD), lambda i: (i, 0))
    vec = pl.BlockSpec((1, HEAD), lambda i: (0, 0))
    return pl.pallas_call(
        body, name=name, grid=(s // tm,), in_specs=[row, row, vec, vec], out_specs=[row, vec, vec],
        out_shape=[jax.ShapeDtypeStruct((s, HEAD), BF16), jax.ShapeDtypeStruct((1, HEAD), F32),
                   jax.ShapeDtypeStruct((1, HEAD), F32)],
        compiler_params=_params(("arbitrary",)))(dgb, pab, a_log, dt_bias)


def _tri_inv(a_low, eye_f):
    n = -a_low
    p = eye_f + n
    steps = int(math.log2(a_low.shape[-1])) - 1
    for _ in range(steps):
        n = _dot(n, n, BNN)
        p = p + _dot(p, n, BNN)
    return p


def _lane_col(x, lane, idx):
    return jnp.sum(jnp.where(lane == idx, x, 0.0), axis=1, keepdims=True)


def _head_cols(gbv, lo, nh):
    lane = lax.broadcasted_iota(jnp.int32, gbv.shape, 1)
    return jnp.stack([_lane_col(gbv, lane, lo + h) for h in range(nh)], axis=0)


def _gdn_chunk(q, k, v, g_col, beta_col, st):
    c = q.shape[1]
    r_i = lax.broadcasted_iota(jnp.int32, (c, c), 0)
    c_i = lax.broadcasted_iota(jnp.int32, (c, c), 1)
    incl = c_i <= r_i
    strict = c_i < r_i
    eye = c_i == r_i
    g_row = jnp.sum(jnp.where(eye, g_col, 0.0), axis=1, keepdims=True)
    gc_col = jnp.sum(jnp.where(incl, g_row, 0.0), axis=2, keepdims=True)
    gc_row = jnp.sum(jnp.where(eye, gc_col, 0.0), axis=1, keepdims=True)
    g_last = jnp.sum(g_col, axis=1, keepdims=True)
    decay = jnp.exp(jnp.where(incl, gc_col - gc_row, NEG))
    kk = _dot(k, k, BNT)
    a_low = jnp.where(strict, beta_col * kk * decay, 0.0)
    t_inv = _tri_inv(a_low, eye.astype(F32))
    e_g = jnp.exp(gc_col)
    bk = beta_col * e_g
    rhs = jnp.concatenate([v * beta_col, k * bk], axis=2)
    sol = _dot(t_inv, rhs, BNN)
    u, w = sol[:, :, :HEAD], sol[:, :, HEAD:]
    qk_raw = _dot(q, k, BNT)
    qk = qk_raw * decay
    q_dec = q * e_g
    e2 = jnp.exp(g_last - gc_col)
    k_dec = k * e2
    gl = jnp.exp(g_last)
    ws = _dot(jnp.concatenate([w, q_dec], axis=1), st, BNN)
    v_new = u - ws[:, :c]
    o = ws[:, c:] + _dot(qk, v_new, BNN)
    st_new = st * gl + _dot(k_dec, v_new, BTN)
    inter = dict(incl=incl, strict=strict, eye=eye, decay=decay, kk=kk, t_inv=t_inv, e_g=e_g, bk=bk, sol=sol, w=w,
                 qk_raw=qk_raw, qk=qk, q_dec=q_dec, e2=e2, k_dec=k_dec, gl=gl, v_new=v_new, c_i=c_i, r_i=r_i)
    return o, st_new, inter


def _gdn_fwd(qkv, gb, nh, name):
    s = qkv.shape[1]
    nc = s // CHUNK

    def body(q_ref, k_ref, v_ref, gb_ref, o_ref, st_ref, state):
        @pl.when(pl.program_id(0) == 0)
        def _():
            state[...] = jnp.zeros_like(state)

        gbv = gb_ref[...]
        st = state[...]
        st_ref[...] = st
        o, st_new, _ = _gdn_chunk(q_ref[...], k_ref[...], v_ref[...], _head_cols(gbv, 0, nh), _head_cols(gbv, nh, nh), st)
        o_ref[...] = o
        state[...] = st_new

    def qspec(part):
        return pl.BlockSpec((nh, CHUNK, HEAD), lambda n: (part, n, 0))

    return pl.pallas_call(
        body, name=name, grid=(nc,),
        in_specs=[qspec(0), qspec(1), qspec(2), pl.BlockSpec((CHUNK, HEAD), lambda n: (n, 0))],
        out_specs=[qspec(0), pl.BlockSpec((None, nh, HEAD, HEAD), lambda n: (n, 0, 0, 0))],
        out_shape=[jax.ShapeDtypeStruct((nh, s, HEAD), F32), jax.ShapeDtypeStruct((nc, nh, HEAD, HEAD), F32)],
        scratch_shapes=[pltpu.VMEM((nh, HEAD, HEAD), F32)],
        compiler_params=_params(("arbitrary",)))(qkv, qkv, qkv, gb)


def _gdn_bwd(qkv, gb, do, states, nh, name):
    s = qkv.shape[1]
    nc = s // CHUNK
    c = CHUNK

    def body(q_ref, k_ref, v_ref, gb_ref, do_ref, st_ref, dqkv_ref, dgb_ref, dstate):
        @pl.when(pl.program_id(0) == 0)
        def _():
            dstate[...] = jnp.zeros_like(dstate)

        gbv = gb_ref[...]
        lane = lax.broadcasted_iota(jnp.int32, gbv.shape, 1)
        q, k, v = q_ref[...], k_ref[...], v_ref[...]
        beta_col = _head_cols(gbv, nh, nh)
        st = st_ref[...]
        dst = dstate[...]
        dov = do_ref[...]
        _, _, it = _gdn_chunk(q, k, v, _head_cols(gbv, 0, nh), beta_col, st)
        incl, strict, eye, decay = it["incl"], it["strict"], it["eye"], it["decay"]
        dv_new = _dot(it["qk"], dov, BTN) + _dot(it["k_dec"], dst, BNN)
        d_qk = _dot(dov, it["v_new"], BNT)
        dd = _dot(jnp.concatenate([dov, -dv_new], axis=1), st, BNT)
        dq_dec, dw = dd[:, :c], dd[:, c:]
        dst_new = _dot(it["q_dec"], dov, BTN) + it["gl"] * dst - _dot(it["w"], dv_new, BTN)
        dgl = jnp.sum(jnp.sum(dst * st, axis=2, keepdims=True), axis=1, keepdims=True)
        dk_dec = _dot(it["v_new"], dst, BNT)
        dsol = jnp.concatenate([dv_new, dw], axis=2)
        drhs = _dot(it["t_inv"], dsol, BTN)
        d_a = jnp.where(strict, -_dot(drhs, it["sol"], BNT), 0.0)
        drhs_u, drhs_w = drhs[:, :, :HEAD], drhs[:, :, HEAD:]
        dvh = beta_col * drhs_u
        rw_k = jnp.sum(drhs_w * k, axis=2, keepdims=True)
        dbeta = jnp.sum(drhs_u * v, axis=2, keepdims=True) + it["e_g"] * rw_k
        dkh = it["bk"] * drhs_w
        dgc_col = it["bk"] * rw_k
        dkk = d_a * beta_col * decay
        dbeta = dbeta + jnp.sum(d_a * it["kk"] * decay, axis=2, keepdims=True)
        ddecay = d_a * beta_col * it["kk"]
        dkh = dkh + _dot(dkk, k, BNN) + _dot(dkk, k, BTN)
        dqk_raw = d_qk * decay
        ddecay = ddecay + d_qk * it["qk_raw"]
        dqh = _dot(dqk_raw, k, BNN)
        dkh = dkh + _dot(dqk_raw, q, BTN)
        ddm = jnp.where(incl, ddecay * decay, 0.0)
        dgc_col = dgc_col + jnp.sum(ddm, axis=2, keepdims=True)
        dgc_row = -jnp.sum(ddm, axis=1, keepdims=True)
        dqh = dqh + dq_dec * it["e_g"]
        dgc_col = dgc_col + jnp.sum(dq_dec * it["q_dec"], axis=2, keepdims=True)
        dkh = dkh + dk_dec * it["e2"]
        tmp = jnp.sum(dk_dec * it["k_dec"], axis=2, keepdims=True)
        dgc_col = dgc_col - tmp
        dg_last = jnp.sum(tmp, axis=1, keepdims=True) + dgl * it["gl"]
        dgc_tot_row = dgc_row + jnp.sum(jnp.where(eye, dgc_col, 0.0), axis=1, keepdims=True)
        dg_col = jnp.sum(jnp.where(it["c_i"] >= it["r_i"], dgc_tot_row, 0.0), axis=2, keepdims=True) + dg_last
        dqkv_ref[0] = dqh
        dqkv_ref[1] = dkh
        dqkv_ref[2] = dvh
        dstate[...] = dst_new
        dgb_acc = jnp.zeros(gbv.shape, F32)
        for h in range(nh):
            dgb_acc = jnp.where(lane == h, dg_col[h], jnp.where(lane == nh + h, dbeta[h], dgb_acc))
        dgb_ref[...] = dgb_acc

    def rev(part):
        return pl.BlockSpec((nh, CHUNK, HEAD), lambda n: (part, nc - 1 - n, 0))

    gspec = pl.BlockSpec((CHUNK, HEAD), lambda n: (nc - 1 - n, 0))
    dqkv, dgb = pl.pallas_call(
        body, name=name, grid=(nc,),
        in_specs=[rev(0), rev(1), rev(2), gspec, rev(0),
                  pl.BlockSpec((None, nh, HEAD, HEAD), lambda n: (nc - 1 - n, 0, 0, 0))],
        out_specs=[pl.BlockSpec((3, nh, CHUNK, HEAD), lambda n: (0, 0, nc - 1 - n, 0)), gspec],
        out_shape=[jax.ShapeDtypeStruct((3, nh, s, HEAD), F32), jax.ShapeDtypeStruct((s, HEAD), F32)],
        scratch_shapes=[pltpu.VMEM((nh, HEAD, HEAD), F32)],
        compiler_params=_params(("arbitrary",)))(qkv, qkv, qkv, gb, do, states)
    return dqkv.reshape(3 * nh, s, HEAD), dgb


SB_TQ = 512


def _tri01(rel):
    j_i = lax.broadcasted_iota(jnp.int32, (SBLK, SBLK), 0)
    s_i = lax.broadcasted_iota(jnp.int32, (SBLK, SBLK), 1)
    return rel(j_i, s_i).astype(BF16)


def _sb_scores(qt, kblk, mask, csum, rhs01):
    z = _dot(qt, kblk, NT)
    e = jnp.exp(-jnp.abs(z))
    sp = jnp.maximum(z, 0.0) + jnp.log(1.0 + e)
    ln = -sp if mask is None else jnp.where(mask, -sp, 0.0)
    st = _dot_hilo(ln, rhs01)
    wgt = jnp.exp((z - sp) + st[:, :SBLK] + csum)
    if mask is not None:
        wgt = jnp.where(mask, wgt, 0.0)
    return z, e, wgt, st


def _band_mask(rows, j, row0):
    r_i = lax.broadcasted_iota(jnp.int32, (rows, SBLK), 0)
    c_i = lax.broadcasted_iota(jnp.int32, (rows, SBLK), 1)
    return (j * SBLK + c_i) < (row0 + r_i)


def _sb_fwd(q, k, v, name):
    s, d = q.shape
    nh = d // HEAD
    tq = min(SB_TQ, s)
    nb = tq // SBLK

    def body(q_ref, k_ref, v_ref, o_ref, c_ref, acc, cs):
        qb = pl.program_id(1)
        lane = lax.broadcasted_iota(jnp.int32, (tq, HEAD), 1)
        rhs01 = jnp.concatenate([_tri01(lambda j, t: j > t), jnp.ones((SBLK, SBLK), BF16)], axis=1)
        acc[...] = jnp.zeros_like(acc)
        cs[...] = jnp.zeros_like(cs)
        c_ref[...] = jnp.zeros_like(c_ref)

        def process(rs, kb, mask):
            off = pl.multiple_of(kb * SBLK, SBLK)
            csum = cs[rs, :]
            _, _, wgt, st = _sb_scores(q_ref[rs, :], k_ref[pl.ds(off, SBLK), :], mask, csum, rhs01)
            acc[rs, :] += _dot(wgt, v_ref[pl.ds(off, SBLK), :])
            c_ref[rs, :] = jnp.where(lane[rs, :] == kb, csum[:, :HEAD], c_ref[rs, :])
            cs[rs, :] = csum + st[:, SBLK:]

        for j in reversed(range(nb)):
            process(slice(j * SBLK, tq), qb * nb + j, _band_mask(tq - j * SBLK, j, j * SBLK))

        def step(it, carry):
            process(slice(0, tq), qb * nb - 1 - it, None)
            return carry

        lax.fori_loop(0, qb * nb, step, 0)
        o_ref[...] = acc[...].astype(BF16)

    qspec = pl.BlockSpec((tq, HEAD), lambda h, i: (i, h))
    kspec = pl.BlockSpec((s, HEAD), lambda h, i: (0, h))
    return pl.pallas_call(
        body, name=name, grid=(nh, s // tq), in_specs=[qspec, kspec, kspec], out_specs=[qspec, qspec],
        out_shape=[jax.ShapeDtypeStruct((s, d), BF16), jax.ShapeDtypeStruct((s, d), F32)],
        scratch_shapes=[pltpu.VMEM((tq, HEAD), F32), pltpu.VMEM((tq, SBLK), F32)],
        compiler_params=_params(("parallel", "arbitrary")))(q, k, v)


def _sb_bwd(q, k, v, do, ctab, name):
    s, d = q.shape
    nh = d // HEAD
    tq = min(SB_TQ, s)
    nb = tq // SBLK

    def body(q_ref, k_ref, v_ref, do_ref, c_ref, dq_ref, dk_ref, dv_ref, ps):
        qb = pl.program_id(1)

        @pl.when(qb == 0)
        def _():
            dk_ref[...] = jnp.zeros_like(dk_ref)
            dv_ref[...] = jnp.zeros_like(dv_ref)

        dq_ref[...] = jnp.zeros_like(dq_ref)
        ps[...] = jnp.zeros_like(ps)
        lane = lax.broadcasted_iota(jnp.int32, (tq, HEAD), 1)
        after = _tri01(lambda j, t: j > t)
        rhs_pre = jnp.concatenate([_tri01(lambda j, t: j < t), jnp.ones((SBLK, SBLK), BF16)], axis=1)

        def process(rs, kb, mask):
            off = pl.multiple_of(kb * SBLK, SBLK)
            kblk = k_ref[pl.ds(off, SBLK), :]
            vblk = v_ref[pl.ds(off, SBLK), :]
            qt = q_ref[rs, :]
            dot_ = do_ref[rs, :]
            csum = _lane_col(c_ref[rs, :], lane[rs, :], kb)
            z, e, wgt, _ = _sb_scores(qt, kblk, mask, csum, after)
            dlw = _dot(dot_, vblk, NT) * wgt
            pt = _dot_hilo(dlw, rhs_pre)
            pfx = ps[rs, :]
            r = 1.0 / (1.0 + e)
            sig = jnp.where(z >= 0.0, r, e * r)
            dz = dlw * (1.0 - sig) - sig * (pfx + pt[:, :SBLK])
            if mask is not None:
                dz = jnp.where(mask, dz, 0.0)
            dq_ref[rs, :] += _dot(dz, kblk)
            dk_ref[pl.ds(off, SBLK), :] += _dot(dz, qt, TN)
            dv_ref[pl.ds(off, SBLK), :] += _dot(wgt, dot_, TN)
            ps[rs, :] = pfx + pt[:, SBLK:]

        def step(kb, carry):
            process(slice(0, tq), kb, None)
            return carry

        lax.fori_loop(0, qb * nb, step, 0)
        for j in range(nb):
            process(slice(j * SBLK, tq), qb * nb + j, _band_mask(tq - j * SBLK, j, j * SBLK))

    qspec = pl.BlockSpec((tq, HEAD), lambda h, i: (i, h))
    kspec = pl.BlockSpec((s, HEAD), lambda h, i: (0, h))
    sds = jax.ShapeDtypeStruct((s, d), F32)
    return pl.pallas_call(
        body, name=name, grid=(nh, s // tq), in_specs=[qspec, kspec, kspec, qspec, qspec],
        out_specs=[qspec, kspec, kspec], out_shape=[sds, sds, sds],
        scratch_shapes=[pltpu.VMEM((tq, SBLK), F32)],
        compiler_params=_params(("parallel", "arbitrary")))(q, k, v, do, ctab)


def _my_index():
    return 4 * lax.axis_index("x") + 2 * lax.axis_index("y") + lax.axis_index("c")


def _all_gather(x_shard, name):
    m_per, n = x_shard.shape

    def body(x_ref, out_ref, send_sems, recv_sems, local_sem):
        x, y, c = lax.axis_index("x"), lax.axis_index("y"), lax.axis_index("c")
        me, sibling = (x, y, c), (x, y, 1 - c)
        chips = [(1 - x, y), (x, 1 - y), (1 - x, 1 - y)]

        def rows(px, py, pc):
            return out_ref.at[pl.ds((4 * px + 2 * py + pc) * m_per, m_per), :]

        def copy(k, block, to, src=None):
            return pltpu.make_async_remote_copy(
                src_ref=rows(*block) if src is None else src, dst_ref=rows(*block),
                send_sem=send_sems.at[k], recv_sem=recv_sems.at[k], device_id=to, device_id_type=MESH)

        mine = pltpu.make_async_copy(x_ref, rows(*me), local_sem)
        mine.start()
        first = [copy(0, me, sibling, src=x_ref)]
        first += [copy(1 + j, me, (*chip, c), src=x_ref) for j, chip in enumerate(chips)]
        for cp in first:
            cp.start()
        passed = [copy(4 + j, (*chip, c), sibling) for j, chip in enumerate(chips)]
        for j, chip in enumerate(chips):
            copy(1 + j, (*chip, c), me).wait_recv()
            passed[j].start()
        copy(0, sibling, me).wait_recv()
        for j, chip in enumerate(chips):
            copy(4 + j, (*chip, 1 - c), me).wait_recv()
        for cp in first + passed:
            cp.wait_send()
        mine.wait()

    return pl.pallas_call(
        body, name=name, out_shape=jax.ShapeDtypeStruct((NDEV * m_per, n), x_shard.dtype),
        in_specs=[pl.BlockSpec(memory_space=pl.ANY)], out_specs=pl.BlockSpec(memory_space=pl.ANY),
        scratch_shapes=[pltpu.SemaphoreType.DMA((7,)), pltpu.SemaphoreType.DMA((7,)), pltpu.SemaphoreType.DMA],
    )(x_shard)


HBM_SPEC = pl.BlockSpec(memory_space=pltpu.HBM)
SEM_SPEC = pl.BlockSpec(memory_space=pltpu.SEMAPHORE)
ANY_SPEC = pl.BlockSpec(memory_space=pl.ANY)
EFFECT = pltpu.SideEffectType.DATAFLOW_SIDE_EFFECTING


def _peer_copies(src_ref, land_ref, send_sems, recv_sems, rows, scatter):
    x, y, c = lax.axis_index("x"), lax.axis_index("y"), lax.axis_index("c")
    me = 4 * x + 2 * y + c
    copies = []
    for k in range(1, NDEV):
        px, py, pc = x ^ ((k >> 2) & 1), y ^ ((k >> 1) & 1), c ^ (k & 1)
        src = src_ref.at[pl.ds((4 * px + 2 * py + pc) * rows, rows), :] if scatter else src_ref
        copies.append(pltpu.make_async_remote_copy(
            src_ref=src, dst_ref=land_ref.at[pl.ds(me * rows, rows), :], send_sem=send_sems.at[k - 1],
            recv_sem=recv_sems.at[k - 1], device_id=(px, py, pc), device_id_type=MESH))
    return copies


def _send_start(src, scatter, after, name):
    rows = src.shape[0] // NDEV if scatter else src.shape[0]
    land = pltpu.with_memory_space_constraint(lax.empty((NDEV * rows, src.shape[1]), src.dtype), pltpu.HBM)

    def body(src_ref, land_ref, after_ref, send_sems, recv_sems, src_thru, land_thru, token):
        for cp in _peer_copies(src_ref, land_ref, send_sems, recv_sems, rows, scatter):
            cp.start()
        token[...] = jnp.zeros_like(token)

    return pl.pallas_call(
        body, name=name,
        out_shape=(pltpu.SemaphoreType.DMA((NDEV - 1,)), pltpu.SemaphoreType.DMA((NDEV - 1,)),
                   pltpu.HBM(src.shape, src.dtype), pltpu.HBM(land.shape, land.dtype), jax.ShapeDtypeStruct((8, HEAD), F32)),
        in_specs=(HBM_SPEC, HBM_SPEC, ANY_SPEC),
        out_specs=(SEM_SPEC, SEM_SPEC, HBM_SPEC, HBM_SPEC, pl.BlockSpec(memory_space=pltpu.VMEM)),
        input_output_aliases={0: 2, 1: 3}, compiler_params=pltpu.CompilerParams(has_side_effects=EFFECT),
    )(pltpu.with_memory_space_constraint(src, pltpu.HBM), land, after)


def _send_wait(started, scatter, after, name):
    send_sems, recv_sems, src_thru, land_thru, _ = started
    rows = land_thru.shape[0] // NDEV

    def body(src_ref, land_ref, send_sems, recv_sems, after_ref, src_dead, got_ref):
        for cp in _peer_copies(src_ref, land_ref, send_sems, recv_sems, rows, scatter):
            cp.wait_send()
            cp.wait_recv()

    return pl.pallas_call(
        body, name=name,
        out_shape=(pltpu.HBM(src_thru.shape, src_thru.dtype), pltpu.HBM(land_thru.shape, land_thru.dtype)),
        in_specs=(HBM_SPEC, HBM_SPEC, SEM_SPEC, SEM_SPEC, ANY_SPEC), out_specs=(HBM_SPEC, HBM_SPEC),
        input_output_aliases={0: 0, 1: 1}, compiler_params=pltpu.CompilerParams(has_side_effects=EFFECT),
    )(src_thru, land_thru, send_sems, recv_sems, after)[1]


def _sum_slots(x, name):
    _, r, c = x.shape
    tr = _pick(r, (512, 256, 128, 64, 32, 16, 8))

    def body(x_ref, o_ref):
        acc = x_ref[0].astype(F32)
        for i in range(1, NDEV):
            acc = acc + x_ref[i].astype(F32)
        o_ref[...] = acc

    return pl.pallas_call(
        body, name=name, grid=(r // tr,), in_specs=[pl.BlockSpec((NDEV, tr, c), lambda i: (0, i, 0))],
        out_specs=pl.BlockSpec((tr, c), lambda i: (i, 0)), out_shape=jax.ShapeDtypeStruct((r, c), F32),
        compiler_params=_params(("parallel",)))(x)


def _adamw(w, g, m, v, name):
    r, c = w.shape
    tr = _pick(r, (256, 128, 64, 32, 16, 8))
    c1 = 1.0 - B1 ** STEP
    c2 = 1.0 - B2 ** STEP

    def body(w_ref, g_ref, m_ref, v_ref, d_ref, nm_ref, nv_ref):
        gv = g_ref[...]
        nm = B1 * m_ref[...] + (1.0 - B1) * gv
        nv = B2 * v_ref[...] + (1.0 - B2) * (gv * gv)
        d_ref[...] = -LR * ((nm / c1) / (jnp.sqrt(nv / c2) + ADAM_EPS) + WD * w_ref[...])
        nm_ref[...] = nm
        nv_ref[...] = nv

    blk = pl.BlockSpec((tr, c), lambda i: (i, 0))
    sds = jax.ShapeDtypeStruct((r, c), F32)
    return pl.pallas_call(
        body, name=name, grid=(r // tr,), in_specs=[blk] * 4, out_specs=[blk] * 3, out_shape=[sds] * 3,
        compiler_params=_params(("parallel",)))(w, g, m, v)


def _pad_rows(a, mult):
    r = a.shape[0]
    pad = (-r) % mult
    return a if pad == 0 else jnp.pad(a, ((0, pad), (0, 0)))


def _pad_lanes(v, width=HEAD):
    return jnp.pad(v.reshape(1, -1), ((0, 0), (0, width - v.shape[-1])))


def kernel(x, p, ln_mix, ln_ffn, ln_ple, gdn_w_in, gdn_conv, gdn_a_log, gdn_dt_bias, gdn_norm, gdn_w_out, kv_norm, w_kv, k_norm, sb_w_q, sb_q_norm, sb_w_out, ffn_w_in, ffn_w_out, ple_w_proj, ple_w_gate, loss_target, m_ln_mix, m_ln_ffn, m_ln_ple, m_gdn_w_in, m_gdn_conv, m_gdn_a_log, m_gdn_dt_bias, m_gdn_norm, m_gdn_w_out, m_kv_norm, m_w_kv, m_k_norm, m_sb_w_q, m_sb_q_norm, m_sb_w_out, m_ffn_w_in, m_ffn_w_out, m_ple_w_proj, m_ple_w_gate, v_ln_mix, v_ln_ffn, v_ln_ple, v_gdn_w_in, v_gdn_conv, v_gdn_a_log, v_gdn_dt_bias, v_gdn_norm, v_gdn_w_out, v_kv_norm, v_w_kv, v_k_norm, v_sb_w_q, v_sb_q_norm, v_sb_w_out, v_ffn_w_in, v_ffn_w_out, v_ple_w_proj, v_ple_w_gate):
    s, d = x.shape[1], x.shape[2]
    nh = d // HEAD
    depth = ln_mix.shape[0]
    n_a = gdn_w_in.shape[0]
    n_b = sb_w_q.shape[0]
    me = _my_index()
    win_cols = gdn_w_in.shape[2]
    win_rows = 4 * d + 2 * nh

    def col_t(w):
        return jnp.transpose(w).astype(BF16)

    local = {}
    for l in range(n_a):
        local[("gdn_w_in", l)] = col_t(gdn_w_in[l])
        local[("gdn_w_out", l)] = gdn_w_out[l].astype(BF16)
    local[("w_kv", 0)] = col_t(w_kv)
    for j in range(n_b):
        local[("sb_w_q", j)] = sb_w_q[j].astype(BF16)
        local[("sb_w_out", j)] = sb_w_out[j].astype(BF16)
    for l in range(depth):
        local[("ffn_w_in", l)] = col_t(ffn_w_in[l])
        local[("ffn_w_out", l)] = ffn_w_out[l].astype(BF16)
        local[("ple_w_proj", l)] = col_t(ple_w_proj[l]).reshape(-1, d)
        local[("ple_w_gate", l)] = ple_w_gate[l].astype(BF16)
    local = {key: _pad_rows(a, 16) for key, a in local.items()}

    chunks = []
    for l in range(depth):
        mix = [("gdn_w_in", l), ("gdn_w_out", l)] if l < n_a else [("sb_w_q", l - n_a), ("sb_w_out", l - n_a)]
        rest = [("ffn_w_in", l), ("ffn_w_out", l), ("ple_w_proj", l), ("ple_w_gate", l)]
        if l == n_a - 1:
            rest.append(("w_kv", 0))
        chunks += [(f"a{l}", mix), (f"f{l}", rest)]
    chunk_keys = dict(chunks)
    chunk_rows = {name: sum(local[k].shape[0] for k in keys) for name, keys in chunks}

    conv_rows = n_a * gdn_conv.shape[1]
    conv_sh = _pad_rows(gdn_conv.reshape(conv_rows, -1), 8)
    conv_g = _all_gather(conv_sh, "comm_gather_conv")
    token = conv_g
    conv_g = conv_g.reshape(NDEV, conv_sh.shape[0], -1)
    conv_full = jnp.transpose(conv_g[:, :conv_rows, :], (1, 0, 2)).reshape(n_a, gdn_conv.shape[1], 3 * d)

    w_started, w_pack = {}, {}
    for name, keys in chunks:
        w_pack[name] = jnp.concatenate([local[k] for k in keys], axis=0)
        w_started[name] = _send_start(w_pack[name], False, token, f"comm_wstart_{name}")
        token = w_started[name][4]

    full = {}

    def fetch(name, after):
        land = _send_wait(w_started[name], False, after, f"comm_wwait_{name}")
        land = lax.dynamic_update_slice(land, w_pack[name], (me * chunk_rows[name], 0))
        land = land.reshape(NDEV, chunk_rows[name], d)
        off = 0
        for key in chunk_keys[name]:
            r = local[key].shape[0]
            full[key] = land[:, off:off + r, :]
            off += r

    def whole(key, valid=None):
        a = full[key]
        if valid is not None:
            a = a[:, :valid, :]
        return a.reshape(-1, d)

    pd = p.shape[-1]
    w_in_t, w_ab_t, w_gout, w_q, w_sout, wf_t, w_fout, wp_t, w_pg = {}, {}, {}, {}, {}, {}, {}, {}, {}
    wkv_t = None

    h = x[0]
    sv = []
    kv_sv = None
    k_sh = v_sh = None
    for l in range(depth):
        t = {}
        t["h0"] = h
        hn = _rms_fwd(h, ln_mix[l], f"rms_mix_{l}")
        t["hn"] = hn
        fetch(f"a{l}", token if l == 0 else hn)
        if l < n_a:
            wt = whole(("gdn_w_in", l), win_cols)
            w_in_t[l] = wt[:4 * d]
            w_ab_t[l] = jnp.pad(wt[4 * d:], ((0, HEAD - 2 * nh), (0, 0)))
            w_gout[l] = whole(("gdn_w_out", l))
        else:
            w_q[l - n_a] = whole(("sb_w_q", l - n_a))
            w_sout[l - n_a] = whole(("sb_w_out", l - n_a))
        if l < n_a:
            proj = _mm(hn, w_in_t[l], "nt", f"gdn_proj_{l}")
            pab = _mm(hn, w_ab_t[l], "nt", f"gdn_proj_ab_{l}")
            qkv = _conv_fwd(proj, conv_full[l], d, f"gdn_conv_{l}")
            al, dtb = _pad_lanes(gdn_a_log[l]), _pad_lanes(gdn_dt_bias[l])
            gb = _gates_fwd(pab, al, dtb, nh, f"gdn_gates_{l}")
            o_raw, states = _gdn_fwd(qkv, gb, nh, f"gdn_rule_{l}")
            o2 = _headnorm_fwd(o_raw, gdn_norm[l], f"gdn_outnorm_{l}", gate=proj, gate_col0=3 * d, head_major=True)
            h = _mm(o2, w_gout[l], "nn", f"gdn_out_{l}", res=h)
            t.update(proj=proj, pab=pab, qkv=qkv, gb=gb, o_raw=o_raw, states=states, o2=o2, al=al, dtb=dtb)
        else:
            j = l - n_a
            qpre = _mm(hn, w_q[j], "nn", f"sb_qproj_{j}")
            qn = _headnorm_fwd(qpre, sb_q_norm[j], f"sb_qnorm_{j}", scale=HEAD ** -0.5)
            o, ctab = _sb_fwd(qn, k_sh, v_sh, f"sb_attn_{j}")
            h = _mm(o, w_sout[j], "nn", f"sb_out_{j}", res=h)
            t.update(qpre=qpre, qn=qn, o=o, ctab=ctab)
        t["h1"] = h
        hn2 = _rms_fwd(h, ln_ffn[l], f"rms_ffn_{l}")
        fetch(f"f{l}", hn2)
        wf_t[l] = whole(("ffn_w_in", l))
        w_fout[l] = whole(("ffn_w_out", l))
        wp_t[l] = full[("ple_w_proj", l)].reshape(d, pd)
        w_pg[l] = whole(("ple_w_gate", l))
        if l == n_a - 1:
            wkv_t = whole(("w_kv", 0))
        act, gs, us = _swiglu_fwd(hn2, wf_t[l], f"ffn_in_{l}")
        h = _mm(act, w_fout[l], "nn", f"ffn_out_{l}", res=h)
        t.update(hn2=hn2, act=act, gs=gs, us=us, h2=h)
        hn3 = _rms_fwd(h, ln_ple[l], f"rms_ple_{l}")
        h, gpre, pp = _ple_fwd(h, hn3, p[l, 0], w_pg[l], wp_t[l], f"ple_{l}")
        t.update(hn3=hn3, gpre=gpre, pp=pp)
        sv.append(t)
        if l == n_a - 1:
            kvn = _rms_fwd(h, kv_norm, "rms_kv")
            kv = _mm(kvn, wkv_t, "nt", "kv_proj")
            k_sh = _headnorm_fwd(kv, k_norm, "k_norm", width=d)
            v_sh = kv[:, d:].astype(BF16)
            kv_sv = dict(h=h, kvn=kvn, kv=kv)

    dh, loss_vec = _loss_fwd_bwd(h, loss_target[0], "loss")
    loss = lax.psum(jnp.sum(loss_vec), ("x", "y", "c"))

    gw = {}
    small = {}
    g_started, g_pack = {}, {}

    def scatter_start(name):
        gparts = []
        for key in chunk_keys[name]:
            g = gw[key]
            g = g.reshape(NDEV, -1, d) if key[0] == "ple_w_proj" else g.reshape(NDEV, -1, g.shape[-1])
            padr = local[key].shape[0] - g.shape[1]
            if padr:
                g = jnp.pad(g, ((0, 0), (0, padr), (0, 0)))
            gparts.append(g)
        g_pack[name] = jnp.concatenate(gparts, axis=1).reshape(NDEV * chunk_rows[name], d)
        g_started[name] = _send_start(g_pack[name], True, gparts[0], f"comm_gstart_{name}")
        return g_started[name][4]

    dk_sh = jnp.zeros((s, d), F32)
    dv_sh = jnp.zeros((s, d), F32)
    for l in reversed(range(depth)):
        t = sv[l]
        if l == n_a - 1:
            dkv_k, dkn = _headnorm_bwd(dk_sh, kv_sv["kv"], k_norm, "k_norm_bwd", dx_dtype=BF16)
            dkv = jnp.concatenate([dkv_k, dv_sh.astype(BF16)], axis=1)
            gw[("w_kv", 0)] = _mm(dkv, kv_sv["kvn"], "tn", "kv_dw", out_dtype=BF16)
            dkvn = _mm(dkv, wkv_t, "nn", "kv_dx")
            dh, dg = _rms_bwd(dkvn, kv_sv["h"], kv_norm, dh, "rms_kv_bwd")
            small["kv_norm"] = dg
            small["k_norm"] = dkn
        dgp, dpp = _ple_bwd(dh, t["gpre"], t["pp"], f"ple_bwd_{l}")
        gw[("ple_w_gate", l)] = _mm(t["hn3"], dgp, "tn", f"ple_dwg_{l}", out_dtype=BF16)
        gw[("ple_w_proj", l)] = _mm(dpp, p[l, 0], "tn", f"ple_dwp_{l}", out_dtype=BF16)
        dhn3 = _mm(dgp, w_pg[l], "nt", f"ple_dx_{l}")
        dh, dg = _rms_bwd(dhn3, t["h2"], ln_ple[l], dh, f"rms_ple_bwd_{l}")
        small[("ln_ple", l)] = dg
        dgs, dus = _swiglu_bwd(dh, w_fout[l], t["gs"], t["us"], f"ffn_bwd_act_{l}")
        gw[("ffn_w_out", l)] = _mm(t["act"], dh, "tn", f"ffn_dwo_{l}", out_dtype=BF16)
        f = dgs.shape[1]
        dwg = _mm(dgs, t["hn2"], "tn", f"ffn_dwg_{l}", out_dtype=BF16)
        dwu = _mm(dus, t["hn2"], "tn", f"ffn_dwu_{l}", out_dtype=BF16)
        gw[("ffn_w_in", l)] = jnp.concatenate([dwg, dwu], axis=0)
        dhn2 = _mm(dgs, wf_t[l][:f], "nn", f"ffn_dxg_{l}")
        dhn2 = _mm(dus, wf_t[l][f:], "nn", f"ffn_dxu_{l}", res=dhn2)
        dh, dg = _rms_bwd(dhn2, t["h1"], ln_ffn[l], dh, f"rms_ffn_bwd_{l}", after=scatter_start(f"f{l}"))
        small[("ln_ffn", l)] = dg
        if l < n_a:
            do2 = _mm(dh, w_gout[l], "nt", f"gdn_out_dx_{l}")
            gw[("gdn_w_out", l)] = _mm(t["o2"], dh, "tn", f"gdn_out_dw_{l}", out_dtype=BF16)
            do_raw, dgn, dgate = _headnorm_bwd(do2, t["o_raw"], gdn_norm[l], f"gdn_outnorm_bwd_{l}",
                                               gate=t["proj"], gate_col0=3 * d, head_major=True)
            small[("gdn_norm", l)] = dgn
            dqkv, dgb = _gdn_bwd(t["qkv"], t["gb"], do_raw, t["states"], nh, f"gdn_rule_bwd_{l}")
            dpab, dal, ddt = _gates_bwd(dgb, t["pab"], t["al"], t["dtb"], nh, f"gdn_gates_bwd_{l}")
            small[("gdn_a_log", l)] = dal
            small[("gdn_dt_bias", l)] = ddt
            dproj_qkv, dconv = _conv_bwd(dqkv, t["proj"], conv_full[l], d, f"gdn_conv_bwd_{l}")
            small[("gdn_conv", l)] = dconv
            dproj = jnp.concatenate([dproj_qkv, dgate], axis=1)
            dw_main = _mm(dproj, t["hn"], "tn", f"gdn_proj_dw_{l}", out_dtype=BF16)
            dw_ab = _mm(dpab, t["hn"], "tn", f"gdn_proj_ab_dw_{l}", out_dtype=BF16)
            gw[("gdn_w_in", l)] = jnp.concatenate([dw_main, dw_ab[:16]], axis=0)[:win_rows]
            dhn = _mm(dproj, w_in_t[l], "nn", f"gdn_proj_dx_{l}")
            dhn = _mm(dpab, w_ab_t[l], "nn", f"gdn_proj_ab_dx_{l}", res=dhn)
        else:
            j = l - n_a
            do = _mm(dh, w_sout[j], "nt", f"sb_out_dx_{j}", out_dtype=BF16)
            gw[("sb_w_out", j)] = _mm(t["o"], dh, "tn", f"sb_out_dw_{j}", out_dtype=BF16)
            dq, dk, dv = _sb_bwd(t["qn"], k_sh, v_sh, do, t["ctab"], f"sb_attn_bwd_{j}")
            dk_sh = dk_sh + dk
            dv_sh = dv_sh + dv
            dqpre, dqn = _headnorm_bwd(dq, t["qpre"], sb_q_norm[j], f"sb_qnorm_bwd_{j}", scale=HEAD ** -0.5, dx_dtype=BF16)
            small[("sb_q_norm", j)] = dqn
            gw[("sb_w_q", j)] = _mm(t["hn"], dqpre, "tn", f"sb_q_dw_{j}", out_dtype=BF16)
            dhn = _mm(dqpre, w_q[j], "nt", f"sb_q_dx_{j}")
        dh, dg = _rms_bwd(dhn, t["h0"], ln_mix[l], dh, f"rms_mix_bwd_{l}", after=scatter_start(f"a{l}"))
        small[("ln_mix", l)] = dg
    grad_x = dh[None]

    gshard = {}
    for name, keys in reversed(chunks):
        rc = chunk_rows[name]
        recv = _send_wait(g_started[name], True, dh, f"comm_gwait_{name}")
        own = lax.dynamic_slice_in_dim(g_pack[name], me * rc, rc, axis=0)
        recv = lax.dynamic_update_slice(recv, own, (me * rc, 0))
        gsum = _sum_slots(recv.reshape(NDEV, rc, d), f"grad_sum_{name}")
        off = 0
        for key in keys:
            r = local[key].shape[0]
            gshard[key] = gsum[off:off + r]
            off += r

    def col_back(key, n_valid):
        return jnp.transpose(gshard[key][:n_valid])

    g_gdn_w_in = jnp.stack([col_back(("gdn_w_in", l), win_cols) for l in range(n_a)])
    g_gdn_w_out = jnp.stack([gshard[("gdn_w_out", l)] for l in range(n_a)])
    g_w_kv = col_back(("w_kv", 0), w_kv.shape[1])
    g_sb_w_q = jnp.stack([gshard[("sb_w_q", j)] for j in range(n_b)])
    g_sb_w_out = jnp.stack([gshard[("sb_w_out", j)] for j in range(n_b)])
    g_ffn_w_in = jnp.stack([col_back(("ffn_w_in", l), ffn_w_in.shape[2]) for l in range(depth)])
    g_ffn_w_out = jnp.stack([gshard[("ffn_w_out", l)] for l in range(depth)])
    g_ple_w_proj = jnp.stack([jnp.transpose(gshard[("ple_w_proj", l)].reshape(-1, pd)) for l in range(depth)])
    g_ple_w_gate = jnp.stack([gshard[("ple_w_gate", l)] for l in range(depth)])

    def vec_rows(v):
        return v.reshape(-1, HEAD)

    small_items = []
    for name_, cnt in (("ln_mix", depth), ("ln_ffn", depth), ("ln_ple", depth)):
        for l in range(cnt):
            small_items.append(((name_, l), vec_rows(small[(name_, l)])))
    for l in range(n_a):
        small_items.append((("gdn_conv", l), small[("gdn_conv", l)].reshape(-1, HEAD)))
        small_items.append((("gdn_a_log", l), small[("gdn_a_log", l)]))
        small_items.append((("gdn_dt_bias", l), small[("gdn_dt_bias", l)]))
        small_items.append((("gdn_norm", l), small[("gdn_norm", l)]))
    small_items.append(("kv_norm", vec_rows(small["kv_norm"])))
    small_items.append(("k_norm", small["k_norm"]))
    for j in range(n_b):
        small_items.append((("sb_q_norm", j), small[("sb_q_norm", j)]))
    spack = jnp.concatenate([_pad_rows(a, 8) for _, a in small_items], axis=0)
    sg = _all_gather(spack, "comm_gather_small").reshape(NDEV, spack.shape[0], HEAD)
    ssum = _sum_slots(sg, "small_sum")
    sm = {}
    off = 0
    for key, a in small_items:
        sm[key] = ssum[off:off + a.shape[0]]
        off += a.shape[0] + (-a.shape[0]) % 8

    g_ln_mix = jnp.stack([sm[("ln_mix", l)].reshape(d) for l in range(depth)])
    g_ln_ffn = jnp.stack([sm[("ln_ffn", l)].reshape(d) for l in range(depth)])
    g_ln_ple = jnp.stack([sm[("ln_ple", l)].reshape(d) for l in range(depth)])
    conv_loc = gdn_conv.shape[2]
    g_conv_full = jnp.stack([sm[("gdn_conv", l)].reshape(gdn_conv.shape[1], 3 * d) for l in range(n_a)])
    g_gdn_conv = lax.dynamic_slice_in_dim(g_conv_full, me * conv_loc, conv_loc, axis=2)
    g_a_log = jnp.stack([sm[("gdn_a_log", l)][0, :nh] for l in range(n_a)])
    g_dt_bias = jnp.stack([sm[("gdn_dt_bias", l)][0, :nh] for l in range(n_a)])
    g_gdn_norm = jnp.stack([sm[("gdn_norm", l)][0] for l in range(n_a)])
    g_kv_norm = sm["kv_norm"].reshape(d)
    g_k_norm = sm["k_norm"][0]
    g_sb_q_norm = jnp.stack([sm[("sb_q_norm", j)][0] for j in range(n_b)])

    grads = [g_ln_mix, g_ln_ffn, g_ln_ple, g_gdn_w_in, g_gdn_conv, g_a_log, g_dt_bias, g_gdn_norm, g_gdn_w_out,
             g_kv_norm, g_w_kv, g_k_norm, g_sb_w_q, g_sb_q_norm, g_sb_w_out, g_ffn_w_in, g_ffn_w_out, g_ple_w_proj,
             g_ple_w_gate]
    weights = [ln_mix, ln_ffn, ln_ple, gdn_w_in, gdn_conv, gdn_a_log, gdn_dt_bias, gdn_norm, gdn_w_out, kv_norm, w_kv,
               k_norm, sb_w_q, sb_q_norm, sb_w_out, ffn_w_in, ffn_w_out, ple_w_proj, ple_w_gate]
    moms = [m_ln_mix, m_ln_ffn, m_ln_ple, m_gdn_w_in, m_gdn_conv, m_gdn_a_log, m_gdn_dt_bias, m_gdn_norm, m_gdn_w_out,
            m_kv_norm, m_w_kv, m_k_norm, m_sb_w_q, m_sb_q_norm, m_sb_w_out, m_ffn_w_in, m_ffn_w_out, m_ple_w_proj,
            m_ple_w_gate]
    vels = [v_ln_mix, v_ln_ffn, v_ln_ple, v_gdn_w_in, v_gdn_conv, v_gdn_a_log, v_gdn_dt_bias, v_gdn_norm, v_gdn_w_out,
            v_kv_norm, v_w_kv, v_k_norm, v_sb_w_q, v_sb_q_norm, v_sb_w_out, v_ffn_w_in, v_ffn_w_out, v_ple_w_proj,
            v_ple_w_gate]

    deltas, new_m, new_v = [], [], []
    small_idx = [i for i, w in enumerate(weights) if w.size < 8 * HEAD * 16]
    for i, (w, g, m, v) in enumerate(zip(weights, grads, moms, vels)):
        if i in small_idx:
            deltas.append(None), new_m.append(None), new_v.append(None)
            continue
        shp = w.shape
        two = lambda a: a.reshape(-1, shp[-1])
        dl, nm, nv = _adamw(two(w), two(g), two(m), two(v), f"adamw_{i}")
        deltas.append(dl.reshape(shp)), new_m.append(nm.reshape(shp)), new_v.append(nv.reshape(shp))

    def flat_pack(arrs):
        flat = jnp.concatenate([a.reshape(-1) for a in arrs])
        pad = (-flat.shape[0]) % (8 * HEAD)
        return jnp.pad(flat, (0, pad)).reshape(-1, HEAD)

    sw = flat_pack([weights[i] for i in small_idx])
    sgr = flat_pack([grads[i] for i in small_idx])
    smo = flat_pack([moms[i] for i in small_idx])
    sve = flat_pack([vels[i] for i in small_idx])
    sdl, snm, snv = _adamw(sw, sgr, smo, sve, "adamw_small")
    off = 0
    for i in small_idx:
        n = weights[i].size
        shp = weights[i].shape
        deltas[i] = sdl.reshape(-1)[off:off + n].reshape(shp)
        new_m[i] = snm.reshape(-1)[off:off + n].reshape(shp)
        new_v[i] = snv.reshape(-1)[off:off + n].reshape(shp)
        off += n

    return (loss, grad_x, *grads, *deltas, *new_m, *new_v)
```

```python
import functools
import math

import jax
import jax.numpy as jnp
from jax import lax
from jax.experimental import pallas as pl
from jax.experimental.pallas import tpu as pltpu

F32 = jnp.float32
BF16 = jnp.bfloat16
NDEV = 8
HEAD = 128
CHUNK = 64
SBLK = 256
EPS = 1e-6
LR, B1, B2, ADAM_EPS, WD, STEP = 0.001, 0.9, 0.999, 1e-08, 0.01, 10
NEG = -1e30
MM_VMEM_BUDGET = 36 * 1024 * 1024

NN = (((1,), (0,)), ((), ()))
NT = (((1,), (1,)), ((), ()))
TN = (((0,), (0,)), ((), ()))
BNN = (((2,), (1,)), ((0,), (0,)))
BNT = (((2,), (2,)), ((0,), (0,)))
BTN = (((1,), (1,)), ((0,), (0,)))
MESH = pl.DeviceIdType.MESH


def _dot(a, b, dims=NN):
    return lax.dot_general(a.astype(BF16), b.astype(BF16), dims, preferred_element_type=F32)


def _dot_hilo(a, b01, dims=NN):
    hi = a.astype(BF16)
    lo = (a - hi.astype(F32)).astype(BF16)
    return (lax.dot_general(hi, b01, dims, preferred_element_type=F32)
            + lax.dot_general(lo, b01, dims, preferred_element_type=F32))


def _pick(dim, cands):
    for c in cands:
        if dim % c == 0:
            return c
    return dim


def _params(sem, vmem_mb=48):
    return pltpu.CompilerParams(dimension_semantics=sem, vmem_limit_bytes=vmem_mb * 1024 * 1024)


def _silu(x):
    return x * jax.nn.sigmoid(x)


def _dsilu(x):
    s = jax.nn.sigmoid(x)
    return s * (1.0 + x * (1.0 - s))


def _mm(a, b, mode, name, out_dtype=F32, res=None):
    if mode == "nn":
        (m, k), n = a.shape, b.shape[1]
    elif mode == "nt":
        (m, k), n = a.shape, b.shape[0]
    else:
        (k, m), n = a.shape, b.shape[1]
    tn = _pick(n, (512, 256, 128))
    tk = k if k <= 4096 else _pick(k, (2048, 1024, 512, 256, 128))
    nk = k // tk
    out_b = jnp.dtype(out_dtype).itemsize + (res.dtype.itemsize if res is not None else 0)
    for tm in (1024, 512, 256, 128, m):
        need = 2 * (tm * tk * a.dtype.itemsize + tk * tn * b.dtype.itemsize + tm * tn * out_b) + 4 * tm * tn
        if m % tm == 0 and need <= MM_VMEM_BUDGET:
            break
    dims = {"nn": NN, "nt": NT, "tn": TN}[mode]
    if mode == "tn":
        a_spec = pl.BlockSpec((tk, tm), lambda i, j, kk: (kk, i))
    else:
        a_spec = pl.BlockSpec((tm, tk), lambda i, j, kk: (i, kk))
    if mode == "nt":
        b_spec = pl.BlockSpec((tn, tk), lambda i, j, kk: (j, kk))
    else:
        b_spec = pl.BlockSpec((tk, tn), lambda i, j, kk: (kk, j))
    mn_spec = pl.BlockSpec((tm, tn), lambda i, j, kk: (i, j))
    has_res = res is not None

    def body(*refs):
        if has_res:
            a_ref, b_ref, r_ref, o_ref, acc = refs
        else:
            a_ref, b_ref, o_ref, acc = refs
        kk = pl.program_id(2)

        @pl.when(kk == 0)
        def _():
            acc[...] = jnp.zeros_like(acc)

        acc[...] += _dot(a_ref[...], b_ref[...], dims)

        @pl.when(kk == nk - 1)
        def _():
            r = acc[...]
            if has_res:
                r = r + r_ref[...].astype(F32)
            o_ref[...] = r.astype(out_dtype)

    ins = [a, b] + ([res] if has_res else [])
    in_specs = [a_spec, b_spec] + ([mn_spec] if has_res else [])
    return pl.pallas_call(
        body, name=name, grid=(m // tm, n // tn, nk), in_specs=in_specs, out_specs=mn_spec,
        out_shape=jax.ShapeDtypeStruct((m, n), out_dtype), scratch_shapes=[pltpu.VMEM((tm, tn), F32)],
        compiler_params=_params(("parallel", "parallel", "arbitrary")))(*ins)


def _rms_fwd(h, g, name):
    s, d = h.shape
    tm = _pick(s, (512, 256, 128))

    def body(h_ref, g_ref, o_ref):
        x = h_ref[...]
        r = lax.rsqrt(jnp.mean(x * x, axis=-1, keepdims=True) + EPS)
        o_ref[...] = (x * r * g_ref[...]).astype(BF16)

    return pl.pallas_call(
        body, name=name, grid=(s // tm,),
        in_specs=[pl.BlockSpec((tm, d), lambda i: (i, 0)), pl.BlockSpec((1, d), lambda i: (0, 0))],
        out_specs=pl.BlockSpec((tm, d), lambda i: (i, 0)),
        out_shape=jax.ShapeDtypeStruct((s, d), BF16), compiler_params=_params(("parallel",)))(h, g.reshape(1, d))


def _rms_bwd(dy, x, g, dres, name, after=None):
    s, d = x.shape
    tm = _pick(s, (512, 256, 128))

    def body(dy_ref, x_ref, g_ref, dr_ref, *rest):
        dx_ref, dg_ref = rest[-2:]

        @pl.when(pl.program_id(0) == 0)
        def _():
            dg_ref[...] = jnp.zeros_like(dg_ref)

        xv = x_ref[...]
        dyv = dy_ref[...].astype(F32)
        r = lax.rsqrt(jnp.mean(xv * xv, axis=-1, keepdims=True) + EPS)
        gdy = dyv * g_ref[...]
        mean_t = jnp.mean(xv * gdy, axis=-1, keepdims=True)
        dx_ref[...] = dr_ref[...] + r * gdy - xv * (r * r * r) * mean_t
        dg_ref[...] += jnp.sum(dyv * xv * r, axis=0, keepdims=True)

    row = pl.BlockSpec((tm, d), lambda i: (i, 0))
    vec = pl.BlockSpec((1, d), lambda i: (0, 0))
    ins, in_specs = [dy, x, g.reshape(1, d), dres], [row, row, vec, row]
    if after is not None:
        ins.append(after)
        in_specs.append(pl.BlockSpec(memory_space=pl.ANY))
    return pl.pallas_call(
        body, name=name, grid=(s // tm,), in_specs=in_specs, out_specs=[row, vec],
        out_shape=[jax.ShapeDtypeStruct((s, d), F32), jax.ShapeDtypeStruct((1, d), F32)],
        compiler_params=_params(("arbitrary",)))(*ins)


def _headnorm_fwd(x, g, name, scale=1.0, gate=None, gate_col0=0, out_dtype=BF16, width=None, head_major=False):
    if head_major:
        s, d = x.shape[1], x.shape[0] * HEAD
    else:
        s, d = x.shape[0], (width or x.shape[1])
    nh = d // HEAD
    tm = _pick(s, (256, 128))
    has_gate = gate is not None
    gb = gate_col0 // d

    def body(*refs):
        if has_gate:
            x_ref, g_ref, gt_ref, o_ref = refs
        else:
            x_ref, g_ref, o_ref = refs
        gv = g_ref[...]
        for h in range(nh):
            sl = slice(h * HEAD, (h + 1) * HEAD)
            xv = (x_ref[h] if head_major else x_ref[:, sl]).astype(F32)
            r = lax.rsqrt(jnp.mean(xv * xv, axis=-1, keepdims=True) + EPS)
            y = xv * r * gv
            if scale != 1.0:
                y = y * scale
            if has_gate:
                y = y * _silu(gt_ref[:, sl])
            o_ref[:, sl] = y.astype(out_dtype)

    row = pl.BlockSpec((tm, d), lambda i: (i, 0))
    hm = pl.BlockSpec((nh, tm, HEAD), lambda i: (0, i, 0))
    ins = [x, g.reshape(1, HEAD)]
    in_specs = [hm if head_major else row, pl.BlockSpec((1, HEAD), lambda i: (0, 0))]
    if has_gate:
        ins.append(gate)
        in_specs.append(pl.BlockSpec((tm, d), lambda i: (i, gb)))
    return pl.pallas_call(
        body, name=name, grid=(s // tm,), in_specs=in_specs, out_specs=row,
        out_shape=jax.ShapeDtypeStruct((s, d), out_dtype), compiler_params=_params(("parallel",)))(*ins)


def _headnorm_bwd(dy, x, g, name, scale=1.0, gate=None, gate_col0=0, dx_dtype=F32, head_major=False):
    s, d = dy.shape
    nh = d // HEAD
    tm = _pick(s, (256, 128))
    has_gate = gate is not None
    gb = gate_col0 // d

    def body(*refs):
        if has_gate:
            dy_ref, x_ref, g_ref, gt_ref, dx_ref, dg_ref, dgt_ref = refs
        else:
            dy_ref, x_ref, g_ref, dx_ref, dg_ref = refs

        @pl.when(pl.program_id(0) == 0)
        def _():
            dg_ref[...] = jnp.zeros_like(dg_ref)

        gv = g_ref[...]
        dg_acc = jnp.zeros((1, HEAD), F32)
        for h in range(nh):
            sl = slice(h * HEAD, (h + 1) * HEAD)
            xv = (x_ref[h] if head_major else x_ref[:, sl]).astype(F32)
            dyv = dy_ref[:, sl].astype(F32)
            r = lax.rsqrt(jnp.mean(xv * xv, axis=-1, keepdims=True) + EPS)
            if has_gate:
                gt = gt_ref[:, sl]
                dgt_ref[:, sl] = (dyv * (xv * r * gv) * _dsilu(gt)).astype(dgt_ref.dtype)
                dn = dyv * _silu(gt)
            else:
                dn = dyv
            if scale != 1.0:
                dn = dn * scale
            gdn = dn * gv
            mean_t = jnp.mean(xv * gdn, axis=-1, keepdims=True)
            dxv = (r * gdn - xv * (r * r * r) * mean_t).astype(dx_dtype)
            if head_major:
                dx_ref[h] = dxv
            else:
                dx_ref[:, sl] = dxv
            dg_acc = dg_acc + jnp.sum(dn * xv * r, axis=0, keepdims=True)
        dg_ref[...] += dg_acc

    row = pl.BlockSpec((tm, d), lambda i: (i, 0))
    hm = pl.BlockSpec((nh, tm, HEAD), lambda i: (0, i, 0))
    vec = pl.BlockSpec((1, HEAD), lambda i: (0, 0))
    ins = [dy, x, g.reshape(1, HEAD)]
    in_specs = [row, hm if head_major else row, vec]
    out_specs = [hm if head_major else row, vec]
    dx_shape = (nh, s, HEAD) if head_major else (s, d)
    out_shape = [jax.ShapeDtypeStruct(dx_shape, dx_dtype), jax.ShapeDtypeStruct((1, HEAD), F32)]
    if has_gate:
        ins.append(gate)
        in_specs.append(pl.BlockSpec((tm, d), lambda i: (i, gb)))
        out_specs.append(row)
        out_shape.append(jax.ShapeDtypeStruct((s, d), BF16))
    return pl.pallas_call(
        body, name=name, grid=(s // tm,), in_specs=in_specs, out_specs=out_specs, out_shape=out_shape,
        compiler_params=_params(("arbitrary",)))(*ins)


def _swiglu_fwd(hn, wf_t, name):
    s, d = hn.shape
    f = wf_t.shape[0] // 2
    tm = _pick(s, (1024, 512, 256, 128))
    tn = _pick(f, (512, 256, 128))
    nj = f // tn

    def body(a_ref, wg_ref, wu_ref, act_ref, g_ref, u_ref):
        a = a_ref[...]
        g = _dot(a, wg_ref[...], NT)
        u = _dot(a, wu_ref[...], NT)
        act_ref[...] = (_silu(g) * u).astype(BF16)
        g_ref[...] = g.astype(BF16)
        u_ref[...] = u.astype(BF16)

    o_spec = pl.BlockSpec((tm, tn), lambda i, j: (i, j))
    sds = jax.ShapeDtypeStruct((s, f), BF16)
    return pl.pallas_call(
        body, name=name, grid=(s // tm, nj),
        in_specs=[pl.BlockSpec((tm, d), lambda i, j: (i, 0)), pl.BlockSpec((tn, d), lambda i, j: (j, 0)),
                  pl.BlockSpec((tn, d), lambda i, j: (j + nj, 0))],
        out_specs=[o_spec, o_spec, o_spec], out_shape=[sds, sds, sds],
        compiler_params=_params(("parallel", "parallel")))(hn, wf_t, wf_t)


def _swiglu_bwd(dh, w_out, g, u, name):
    s, d = dh.shape
    f = w_out.shape[0]
    tm = _pick(s, (1024, 512, 256, 128))
    tn = _pick(f, (512, 256, 128))

    def body(dh_ref, w_ref, g_ref, u_ref, dg_ref, du_ref):
        dact = _dot(dh_ref[...], w_ref[...], NT)
        gv = g_ref[...].astype(F32)
        uv = u_ref[...].astype(F32)
        dg_ref[...] = (dact * uv * _dsilu(gv)).astype(BF16)
        du_ref[...] = (dact * _silu(gv)).astype(BF16)

    o_spec = pl.BlockSpec((tm, tn), lambda i, j: (i, j))
    sds = jax.ShapeDtypeStruct((s, f), BF16)
    return pl.pallas_call(
        body, name=name, grid=(s // tm, f // tn),
        in_specs=[pl.BlockSpec((tm, d), lambda i, j: (i, 0)), pl.BlockSpec((tn, d), lambda i, j: (j, 0)), o_spec, o_spec],
        out_specs=[o_spec, o_spec], out_shape=[sds, sds],
        compiler_params=_params(("parallel", "parallel")))(dh, w_out, g, u)


def _ple_fwd(h, hn, p, w_gate, wp_t, name):
    s, d = h.shape
    pd = p.shape[1]
    tm = _pick(s, (512, 256, 128))
    tn = _pick(d, (512, 256, 128))

    def body(h_ref, hn_ref, p_ref, wg_ref, wp_ref, o_ref, gp_ref, pp_ref):
        gpre = _dot(hn_ref[...], wg_ref[...], NN)
        pp = _dot(p_ref[...], wp_ref[...], NT)
        o_ref[...] = h_ref[...] + pp * jax.nn.sigmoid(gpre)
        gp_ref[...] = gpre.astype(BF16)
        pp_ref[...] = pp.astype(BF16)

    mn = pl.BlockSpec((tm, tn), lambda i, j: (i, j))
    return pl.pallas_call(
        body, name=name, grid=(s // tm, d // tn),
        in_specs=[mn, pl.BlockSpec((tm, d), lambda i, j: (i, 0)), pl.BlockSpec((tm, pd), lambda i, j: (i, 0)),
                  pl.BlockSpec((d, tn), lambda i, j: (0, j)), pl.BlockSpec((tn, pd), lambda i, j: (j, 0))],
        out_specs=[mn, mn, mn],
        out_shape=[jax.ShapeDtypeStruct((s, d), F32), jax.ShapeDtypeStruct((s, d), BF16), jax.ShapeDtypeStruct((s, d), BF16)],
        compiler_params=_params(("parallel", "parallel")))(h, hn, p, w_gate, wp_t)


def _ple_bwd(dh, gpre, pp, name):
    s, d = dh.shape
    tm = _pick(s, (512, 256, 128))

    def body(dh_ref, gp_ref, pp_ref, dgp_ref, dpp_ref):
        dv = dh_ref[...]
        sig = jax.nn.sigmoid(gp_ref[...].astype(F32))
        ppv = pp_ref[...].astype(F32)
        dpp_ref[...] = (dv * sig).astype(BF16)
        dgp_ref[...] = (dv * ppv * sig * (1.0 - sig)).astype(BF16)

    row = pl.BlockSpec((tm, d), lambda i: (i, 0))
    sds = jax.ShapeDtypeStruct((s, d), BF16)
    return pl.pallas_call(
        body, name=name, grid=(s // tm,), in_specs=[row, row, row], out_specs=[row, row], out_shape=[sds, sds],
        compiler_params=_params(("parallel",)))(dh, gpre, pp)


def _loss_fwd_bwd(y, t, name):
    s, d = y.shape
    tm = _pick(s, (512, 256, 128))

    def body(y_ref, t_ref, dy_ref, l_ref):
        @pl.when(pl.program_id(0) == 0)
        def _():
            l_ref[...] = jnp.zeros_like(l_ref)

        e = y_ref[...] - t_ref[...]
        dy_ref[...] = e * (1.0 / d)
        l_ref[...] += jnp.sum(e * e, axis=0, keepdims=True) * (0.5 / d)

    row = pl.BlockSpec((tm, d), lambda i: (i, 0))
    vec = pl.BlockSpec((1, d), lambda i: (0, 0))
    return pl.pallas_call(
        body, name=name, grid=(s // tm,), in_specs=[row, row], out_specs=[row, vec],
        out_shape=[jax.ShapeDtypeStruct((s, d), F32), jax.ShapeDtypeStruct((1, d), F32)],
        compiler_params=_params(("arbitrary",)))(y, t)


PADR = 8


def _conv_fwd(proj, w_conv, d, name):
    s = proj.shape[0]
    nh = d // HEAD
    kw = w_conv.shape[0]
    qscale = HEAD ** -0.5

    def body(x_ref, w_ref, o_ref, xp):
        kind = pl.program_id(0) // nh
        xp[0:PADR, :] = jnp.zeros((PADR, HEAD), F32)
        xp[PADR:, :] = x_ref[...]
        acc = jnp.zeros((s, HEAD), F32)
        for j in range(kw):
            acc = acc + w_ref[j:j + 1, :] * xp[PADR - (kw - 1) + j:PADR - (kw - 1) + j + s, :]
        a = _silu(acc)
        r = lax.rsqrt(jnp.sum(a * a, axis=-1, keepdims=True) + EPS)
        fac = jnp.where(kind == 0, r * qscale, jnp.where(kind == 1, r, jnp.ones_like(r)))
        o_ref[...] = a * fac

    blk = pl.BlockSpec((s, HEAD), lambda c: (0, c))
    hm = pl.BlockSpec((None, s, HEAD), lambda c: (c, 0, 0))
    return pl.pallas_call(
        body, name=name, grid=(3 * nh,), in_specs=[blk, pl.BlockSpec((kw, HEAD), lambda c: (0, c))], out_specs=hm,
        out_shape=jax.ShapeDtypeStruct((3 * nh, s, HEAD), F32), scratch_shapes=[pltpu.VMEM((s + PADR, HEAD), F32)],
        compiler_params=_params(("parallel",)))(proj, w_conv)


def _conv_bwd(dqkv, proj, w_conv, d, name):
    s = proj.shape[0]
    nh = d // HEAD
    kw = w_conv.shape[0]
    qscale = HEAD ** -0.5

    def body(dy_ref, x_ref, w_ref, dx_ref, dw_ref, xp, dp):
        kind = pl.program_id(0) // nh
        xp[0:PADR, :] = jnp.zeros((PADR, HEAD), F32)
        xp[PADR:, :] = x_ref[...]
        acc = jnp.zeros((s, HEAD), F32)
        for j in range(kw):
            acc = acc + w_ref[j:j + 1, :] * xp[PADR - (kw - 1) + j:PADR - (kw - 1) + j + s, :]
        a = _silu(acc)
        dy = dy_ref[...]
        r = lax.rsqrt(jnp.sum(a * a, axis=-1, keepdims=True) + EPS)
        sc = jnp.where(kind == 0, qscale, 1.0)
        dyn = dy * sc
        da_norm = r * dyn - a * (r * r * r) * jnp.sum(a * dyn, axis=-1, keepdims=True)
        da = jnp.where(kind == 2, dy, da_norm)
        dacc = da * _dsilu(acc)
        dp[0:s, :] = dacc
        dp[s:, :] = jnp.zeros((PADR, HEAD), F32)
        dx = jnp.zeros((s, HEAD), F32)
        for j in range(kw):
            sh = kw - 1 - j
            dx = dx + w_ref[j:j + 1, :] * dp[sh:sh + s, :]
            dw_ref[j:j + 1, :] = jnp.sum(dacc * xp[PADR - sh:PADR - sh + s, :], axis=0, keepdims=True)
        dx_ref[...] = dx.astype(BF16)

    blk = pl.BlockSpec((s, HEAD), lambda c: (0, c))
    hm = pl.BlockSpec((None, s, HEAD), lambda c: (c, 0, 0))
    wblk = pl.BlockSpec((kw, HEAD), lambda c: (0, c))
    return pl.pallas_call(
        body, name=name, grid=(3 * nh,), in_specs=[hm, blk, wblk], out_specs=[blk, wblk],
        out_shape=[jax.ShapeDtypeStruct((s, 3 * d), BF16), jax.ShapeDtypeStruct((kw, 3 * d), F32)],
        scratch_shapes=[pltpu.VMEM((s + PADR, HEAD), F32), pltpu.VMEM((s + PADR, HEAD), F32)],
        compiler_params=_params(("parallel",)))(dqkv, proj, w_conv)


def _softplus(x):
    return jnp.maximum(x, 0.0) + jnp.log(1.0 + jnp.exp(-jnp.abs(x)))


def _gates_fwd(pab, a_log, dt_bias, nh, name):
    s = pab.shape[0]
    tm = _pick(s, (512, 256, 128))

    def body(x_ref, al_ref, dt_ref, o_ref):
        x = x_ref[...]
        lane = lax.broadcasted_iota(jnp.int32, x.shape, 1)
        g = -jnp.exp(al_ref[...]) * _softplus(x + dt_ref[...])
        o_ref[...] = jnp.where(lane < nh, g, jnp.where(lane < 2 * nh, jax.nn.sigmoid(x), 0.0))

    row = pl.BlockSpec((tm, HEAD), lambda i: (i, 0))
    vec = pl.BlockSpec((1, HEAD), lambda i: (0, 0))
    return pl.pallas_call(
        body, name=name, grid=(s // tm,), in_specs=[row, vec, vec], out_specs=row,
        out_shape=jax.ShapeDtypeStruct((s, HEAD), F32), compiler_params=_params(("parallel",)))(pab, a_log, dt_bias)


def _gates_bwd(dgb, pab, a_log, dt_bias, nh, name):
    s = pab.shape[0]
    tm = _pick(s, (512, 256, 128))

    def body(d_ref, x_ref, al_ref, dt_ref, dx_ref, dal_ref, ddt_ref):
        @pl.when(pl.program_id(0) == 0)
        def _():
            dal_ref[...] = jnp.zeros_like(dal_ref)
            ddt_ref[...] = jnp.zeros_like(ddt_ref)

        x = x_ref[...]
        dv = d_ref[...]
        lane = lax.broadcasted_iota(jnp.int32, x.shape, 1)
        ea = jnp.exp(al_ref[...])
        xs = x + dt_ref[...]
        g = -ea * _softplus(xs)
        dxs = jnp.where(lane < nh, dv * (-ea) * jax.nn.sigmoid(xs), 0.0)
        sg = jax.nn.sigmoid(x)
        dxb = jnp.where((lane >= nh) & (lane < 2 * nh), dv * sg * (1.0 - sg), 0.0)
        dx_ref[...] = (dxs + dxb).astype(BF16)
        dal_ref[...] += jnp.sum(jnp.where(lane < nh, dv * g, 0.0), axis=0, keepdims=True)
        ddt_ref[...] += jnp.sum(dxs, axis=0, keepdims=True)

    row = pl.BlockSpec((tm, HEAD), lambda i: (i, 0))
    vec = pl.BlockSpec((1, HEAD), lambda i: (0, 0))
    return pl.pallas_call(
        body, name=name, grid=(s // tm,), in_specs=[row, row, vec, vec], out_specs=[row, vec, vec],
        out_shape=[jax.ShapeDtypeStruct((s, HEAD), BF16), jax.ShapeDtypeStruct((1, HEAD), F32),
                   jax.ShapeDtypeStruct((1, HEAD), F32)],
        compiler_params=_params(("arbitrary",)))(dgb, pab, a_log, dt_bias)


def _tri_inv(a_low, eye_f):
    n = -a_low
    p = eye_f + n
    steps = int(math.log2(a_low.shape[-1])) - 1
    for _ in range(steps):
        n = _dot(n, n, BNN)
        p = p + _dot(p, n, BNN)
    return p


def _lane_col(x, lane, idx):
    return jnp.sum(jnp.where(lane == idx, x, 0.0), axis=1, keepdims=True)


def _head_cols(gbv, lo, nh):
    lane = lax.broadcasted_iota(jnp.int32, gbv.shape, 1)
    return jnp.stack([_lane_col(gbv, lane, lo + h) for h in range(nh)], axis=0)


def _gdn_chunk(q, k, v, g_col, beta_col, st):
    c = q.shape[1]
    r_i = lax.broadcasted_iota(jnp.int32, (c, c), 0)
    c_i = lax.broadcasted_iota(jnp.int32, (c, c), 1)
    incl = c_i <= r_i
    strict = c_i < r_i
    eye = c_i == r_i
    g_row = jnp.sum(jnp.where(eye, g_col, 0.0), axis=1, keepdims=True)
    gc_col = jnp.sum(jnp.where(incl, g_row, 0.0), axis=2, keepdims=True)
    gc_row = jnp.sum(jnp.where(eye, gc_col, 0.0), axis=1, keepdims=True)
    g_last = jnp.sum(g_col, axis=1, keepdims=True)
    decay = jnp.exp(jnp.where(incl, gc_col - gc_row, NEG))
    kk = _dot(k, k, BNT)
    a_low = jnp.where(strict, beta_col * kk * decay, 0.0)
    t_inv = _tri_inv(a_low, eye.astype(F32))
    e_g = jnp.exp(gc_col)
    bk = beta_col * e_g
    rhs = jnp.concatenate([v * beta_col, k * bk], axis=2)
    sol = _dot(t_inv, rhs, BNN)
    u, w = sol[:, :, :HEAD], sol[:, :, HEAD:]
    qk_raw = _dot(q, k, BNT)
    qk = qk_raw * decay
    q_dec = q * e_g
    e2 = jnp.exp(g_last - gc_col)
    k_dec = k * e2
    gl = jnp.exp(g_last)
    ws = _dot(jnp.concatenate([w, q_dec], axis=1), st, BNN)
    v_new = u - ws[:, :c]
    o = ws[:, c:] + _dot(qk, v_new, BNN)
    st_new = st * gl + _dot(k_dec, v_new, BTN)
    inter = dict(incl=incl, strict=strict, eye=eye, decay=decay, kk=kk, t_inv=t_inv, e_g=e_g, bk=bk, sol=sol, w=w,
                 qk_raw=qk_raw, qk=qk, q_dec=q_dec, e2=e2, k_dec=k_dec, gl=gl, v_new=v_new, c_i=c_i, r_i=r_i)
    return o, st_new, inter


def _gdn_fwd(qkv, gb, nh, name):
    s = qkv.shape[1]
    nc = s // CHUNK

    def body(q_ref, k_ref, v_ref, gb_ref, o_ref, st_ref, state):
        @pl.when(pl.program_id(0) == 0)
        def _():
            state[...] = jnp.zeros_like(state)

        gbv = gb_ref[...]
        st = state[...]
        st_ref[...] = st
        o, st_new, _ = _gdn_chunk(q_ref[...], k_ref[...], v_ref[...], _head_cols(gbv, 0, nh), _head_cols(gbv, nh, nh), st)
        o_ref[...] = o
        state[...] = st_new

    def qspec(part):
        return pl.BlockSpec((nh, CHUNK, HEAD), lambda n: (part, n, 0))

    return pl.pallas_call(
        body, name=name, grid=(nc,),
        in_specs=[qspec(0), qspec(1), qspec(2), pl.BlockSpec((CHUNK, HEAD), lambda n: (n, 0))],
        out_specs=[qspec(0), pl.BlockSpec((None, nh, HEAD, HEAD), lambda n: (n, 0, 0, 0))],
        out_shape=[jax.ShapeDtypeStruct((nh, s, HEAD), F32), jax.ShapeDtypeStruct((nc, nh, HEAD, HEAD), F32)],
        scratch_shapes=[pltpu.VMEM((nh, HEAD, HEAD), F32)],
        compiler_params=_params(("arbitrary",)))(qkv, qkv, qkv, gb)


def _gdn_bwd(qkv, gb, do, states, nh, name):
    s = qkv.shape[1]
    nc = s // CHUNK
    c = CHUNK

    def body(q_ref, k_ref, v_ref, gb_ref, do_ref, st_ref, dqkv_ref, dgb_ref, dstate):
        @pl.when(pl.program_id(0) == 0)
        def _():
            dstate[...] = jnp.zeros_like(dstate)

        gbv = gb_ref[...]
        lane = lax.broadcasted_iota(jnp.int32, gbv.shape, 1)
        q, k, v = q_ref[...], k_ref[...], v_ref[...]
        beta_col = _head_cols(gbv, nh, nh)
        st = st_ref[...]
        dst = dstate[...]
        dov = do_ref[...]
        _, _, it = _gdn_chunk(q, k, v, _head_cols(gbv, 0, nh), beta_col, st)
        incl, strict, eye, decay = it["incl"], it["strict"], it["eye"], it["decay"]
        dv_new = _dot(it["qk"], dov, BTN) + _dot(it["k_dec"], dst, BNN)
        d_qk = _dot(dov, it["v_new"], BNT)
        dd = _dot(jnp.concatenate([dov, -dv_new], axis=1), st, BNT)
        dq_dec, dw = dd[:, :c], dd[:, c:]
        dst_new = _dot(it["q_dec"], dov, BTN) + it["gl"] * dst - _dot(it["w"], dv_new, BTN)
        dgl = jnp.sum(jnp.sum(dst * st, axis=2, keepdims=True), axis=1, keepdims=True)
        dk_dec = _dot(it["v_new"], dst, BNT)
        dsol = jnp.concatenate([dv_new, dw], axis=2)
        drhs = _dot(it["t_inv"], dsol, BTN)
        d_a = jnp.where(strict, -_dot(drhs, it["sol"], BNT), 0.0)
        drhs_u, drhs_w = drhs[:, :, :HEAD], drhs[:, :, HEAD:]
        dvh = beta_col * drhs_u
        rw_k = jnp.sum(drhs_w * k, axis=2, keepdims=True)
        dbeta = jnp.sum(drhs_u * v, axis=2, keepdims=True) + it["e_g"] * rw_k
        dkh = it["bk"] * drhs_w
        dgc_col = it["bk"] * rw_k
        dkk = d_a * beta_col * decay
        dbeta = dbeta + jnp.sum(d_a * it["kk"] * decay, axis=2, keepdims=True)
        ddecay = d_a * beta_col * it["kk"]
        dkh = dkh + _dot(dkk, k, BNN) + _dot(dkk, k, BTN)
        dqk_raw = d_qk * decay
        ddecay = ddecay + d_qk * it["qk_raw"]
        dqh = _dot(dqk_raw, k, BNN)
        dkh = dkh + _dot(dqk_raw, q, BTN)
        ddm = jnp.where(incl, ddecay * decay, 0.0)
        dgc_col = dgc_col + jnp.sum(ddm, axis=2, keepdims=True)
        dgc_row = -jnp.sum(ddm, axis=1, keepdims=True)
        dqh = dqh + dq_dec * it["e_g"]
        dgc_col = dgc_col + jnp.sum(dq_dec * it["q_dec"], axis=2, keepdims=True)
        dkh = dkh + dk_dec * it["e2"]
        tmp = jnp.sum(dk_dec * it["k_dec"], axis=2, keepdims=True)
        dgc_col = dgc_col - tmp
        dg_last = jnp.sum(tmp, axis=1, keepdims=True) + dgl * it["gl"]
        dgc_tot_row = dgc_row + jnp.sum(jnp.where(eye, dgc_col, 0.0), axis=1, keepdims=True)
        dg_col = jnp.sum(jnp.where(it["c_i"] >= it["r_i"], dgc_tot_row, 0.0), axis=2, keepdims=True) + dg_last
        dqkv_ref[0] = dqh
        dqkv_ref[1] = dkh
        dqkv_ref[2] = dvh
        dstate[...] = dst_new
        dgb_acc = jnp.zeros(gbv.shape, F32)
        for h in range(nh):
            dgb_acc = jnp.where(lane == h, dg_col[h], jnp.where(lane == nh + h, dbeta[h], dgb_acc))
        dgb_ref[...] = dgb_acc

    def rev(part):
        return pl.BlockSpec((nh, CHUNK, HEAD), lambda n: (part, nc - 1 - n, 0))

    gspec = pl.BlockSpec((CHUNK, HEAD), lambda n: (nc - 1 - n, 0))
    dqkv, dgb = pl.pallas_call(
        body, name=name, grid=(nc,),
        in_specs=[rev(0), rev(1), rev(2), gspec, rev(0),
                  pl.BlockSpec((None, nh, HEAD, HEAD), lambda n: (nc - 1 - n, 0, 0, 0))],
        out_specs=[pl.BlockSpec((3, nh, CHUNK, HEAD), lambda n: (0, 0, nc - 1 - n, 0)), gspec],
        out_shape=[jax.ShapeDtypeStruct((3, nh, s, HEAD), F32), jax.ShapeDtypeStruct((s, HEAD), F32)],
        scratch_shapes=[pltpu.VMEM((nh, HEAD, HEAD), F32)],
        compiler_params=_params(("arbitrary",)))(qkv, qkv, qkv, gb, do, states)
    return dqkv.reshape(3 * nh, s, HEAD), dgb


SB_TQ = 512


def _tri01(rel):
    j_i = lax.broadcasted_iota(jnp.int32, (SBLK, SBLK), 0)
    s_i = lax.broadcasted_iota(jnp.int32, (SBLK, SBLK), 1)
    return rel(j_i, s_i).astype(BF16)


SB_HP = 2


def _each(fn, *lists):
    return [fn(*xs) for xs in zip(*lists)]


def _sb_scores(qts, kblks, mask, csums, rhs01):
    zs = _each(lambda qt, kb: _dot(qt, kb, NT), qts, kblks)
    es = _each(lambda z: jnp.exp(-jnp.abs(z)), zs)
    sps = _each(lambda z, e: jnp.maximum(z, 0.0) + jnp.log(1.0 + e), zs, es)
    lns = _each(lambda sp: -sp if mask is None else jnp.where(mask, -sp, 0.0), sps)
    sts = _each(lambda ln: _dot_hilo(ln, rhs01), lns)
    wgts = _each(lambda z, sp, st, cs: jnp.exp((z - sp) + st[:, :SBLK] + cs), zs, sps, sts, csums)
    if mask is not None:
        wgts = _each(lambda w: jnp.where(mask, w, 0.0), wgts)
    return zs, es, wgts, sts


def _band_mask(rows, j, row0):
    r_i = lax.broadcasted_iota(jnp.int32, (rows, SBLK), 0)
    c_i = lax.broadcasted_iota(jnp.int32, (rows, SBLK), 1)
    return (j * SBLK + c_i) < (row0 + r_i)


def _sb_fwd(q, k, v, name):
    s, d = q.shape
    nh = d // HEAD
    tq = min(SB_TQ, s)
    nb = tq // SBLK

    hp = SB_HP
    heads = [slice(h * HEAD, (h + 1) * HEAD) for h in range(hp)]

    def body(q_ref, k_ref, v_ref, o_ref, c_ref, acc, cs):
        qb = pl.program_id(1)
        lane = lax.broadcasted_iota(jnp.int32, (tq, HEAD), 1)
        rhs01 = jnp.concatenate([_tri01(lambda j, t: j > t), jnp.ones((SBLK, SBLK), BF16)], axis=1)
        acc[...] = jnp.zeros_like(acc)
        cs[...] = jnp.zeros_like(cs)
        c_ref[...] = jnp.zeros_like(c_ref)

        def process(rs, kb, mask):
            keys = pl.ds(pl.multiple_of(kb * SBLK, SBLK), SBLK)
            csums = [cs[h, rs, :] for h in range(hp)]
            _, _, wgts, sts = _sb_scores([q_ref[rs, hs] for hs in heads], [k_ref[keys, hs] for hs in heads], mask, csums, rhs01)
            pvs = _each(lambda w, hs: _dot(w, v_ref[keys, hs]), wgts, heads)
            for h, hs in enumerate(heads):
                acc[h, rs, :] += pvs[h]
                c_ref[rs, hs] = jnp.where(lane[rs, :] == kb, csums[h][:, :HEAD], c_ref[rs, hs])
                cs[h, rs, :] = csums[h] + sts[h][:, SBLK:]

        for j in reversed(range(nb)):
            process(slice(j * SBLK, tq), qb * nb + j, _band_mask(tq - j * SBLK, j, j * SBLK))

        def step(it, carry):
            process(slice(0, tq), qb * nb - 1 - it, None)
            return carry

        lax.fori_loop(0, qb * nb, step, 0)
        for h, hs in enumerate(heads):
            o_ref[:, hs] = acc[h].astype(BF16)

    qspec = pl.BlockSpec((tq, hp * HEAD), lambda h, i: (i, h))
    kspec = pl.BlockSpec((s, hp * HEAD), lambda h, i: (0, h))
    return pl.pallas_call(
        body, name=name, grid=(nh // hp, s // tq), in_specs=[qspec, kspec, kspec], out_specs=[qspec, qspec],
        out_shape=[jax.ShapeDtypeStruct((s, d), BF16), jax.ShapeDtypeStruct((s, d), F32)],
        scratch_shapes=[pltpu.VMEM((hp, tq, HEAD), F32), pltpu.VMEM((hp, tq, SBLK), F32)],
        compiler_params=_params(("parallel", "arbitrary")))(q, k, v)


def _sb_bwd(q, k, v, do, ctab, name):
    s, d = q.shape
    nh = d // HEAD
    tq = min(SB_TQ, s)
    nb = tq // SBLK

    hp = SB_HP
    heads = [slice(h * HEAD, (h + 1) * HEAD) for h in range(hp)]

    def body(q_ref, k_ref, v_ref, do_ref, c_ref, dq_ref, dk_ref, dv_ref, ps):
        qb = pl.program_id(1)

        @pl.when(qb == 0)
        def _():
            dk_ref[...] = jnp.zeros_like(dk_ref)
            dv_ref[...] = jnp.zeros_like(dv_ref)

        dq_ref[...] = jnp.zeros_like(dq_ref)
        ps[...] = jnp.zeros_like(ps)
        lane = lax.broadcasted_iota(jnp.int32, (tq, HEAD), 1)
        after = _tri01(lambda j, t: j > t)
        rhs_pre = jnp.concatenate([_tri01(lambda j, t: j < t), jnp.ones((SBLK, SBLK), BF16)], axis=1)

        def process(rs, kb, mask):
            keys = pl.ds(pl.multiple_of(kb * SBLK, SBLK), SBLK)
            kblks = [k_ref[keys, hs] for hs in heads]
            qts = [q_ref[rs, hs] for hs in heads]
            dots = [do_ref[rs, hs] for hs in heads]
            csums = [_lane_col(c_ref[rs, hs], lane[rs, :], kb) for hs in heads]
            zs, es, wgts, _ = _sb_scores(qts, kblks, mask, csums, after)
            dlws = _each(lambda dt, hs, w: _dot(dt, v_ref[keys, hs], NT) * w, dots, heads, wgts)
            pts = _each(lambda dlw: _dot_hilo(dlw, rhs_pre), dlws)
            pfxs = [ps[h, rs, :] for h in range(hp)]
            rs_ = _each(lambda e: 1.0 / (1.0 + e), es)
            sigs = _each(lambda z, e, r: jnp.where(z >= 0.0, r, e * r), zs, es, rs_)
            dzs = _each(lambda dlw, sig, pfx, pt: dlw * (1.0 - sig) - sig * (pfx + pt[:, :SBLK]), dlws, sigs, pfxs, pts)
            if mask is not None:
                dzs = _each(lambda dz: jnp.where(mask, dz, 0.0), dzs)
            dqs = _each(lambda dz, kb_: _dot(dz, kb_), dzs, kblks)
            dks = _each(lambda dz, qt: _dot(dz, qt, TN), dzs, qts)
            dvs = _each(lambda w, dt: _dot(w, dt, TN), wgts, dots)
            for h, hs in enumerate(heads):
                dq_ref[rs, hs] += dqs[h]
                dk_ref[keys, hs] += dks[h]
                dv_ref[keys, hs] += dvs[h]
                ps[h, rs, :] = pfxs[h] + pts[h][:, SBLK:]

        def step(kb, carry):
            process(slice(0, tq), kb, None)
            return carry

        lax.fori_loop(0, qb * nb, step, 0)
        for j in range(nb):
            process(slice(j * SBLK, tq), qb * nb + j, _band_mask(tq - j * SBLK, j, j * SBLK))

    qspec = pl.BlockSpec((tq, hp * HEAD), lambda h, i: (i, h))
    kspec = pl.BlockSpec((s, hp * HEAD), lambda h, i: (0, h))
    sds = jax.ShapeDtypeStruct((s, d), F32)
    return pl.pallas_call(
        body, name=name, grid=(nh // hp, s // tq), in_specs=[qspec, kspec, kspec, qspec, qspec],
        out_specs=[qspec, kspec, kspec], out_shape=[sds, sds, sds],
        scratch_shapes=[pltpu.VMEM((hp, tq, SBLK), F32)],
        compiler_params=_params(("parallel", "arbitrary")))(q, k, v, do, ctab)


def _my_index():
    return 4 * lax.axis_index("x") + 2 * lax.axis_index("y") + lax.axis_index("c")


def _all_gather(x_shard, name):
    m_per, n = x_shard.shape

    def body(x_ref, out_ref, send_sems, recv_sems, local_sem):
        x, y, c = lax.axis_index("x"), lax.axis_index("y"), lax.axis_index("c")
        me, sibling = (x, y, c), (x, y, 1 - c)
        chips = [(1 - x, y), (x, 1 - y), (1 - x, 1 - y)]

        def rows(px, py, pc):
            return out_ref.at[pl.ds((4 * px + 2 * py + pc) * m_per, m_per), :]

        def copy(k, block, to, src=None):
            return pltpu.make_async_remote_copy(
                src_ref=rows(*block) if src is None else src, dst_ref=rows(*block),
                send_sem=send_sems.at[k], recv_sem=recv_sems.at[k], device_id=to, device_id_type=MESH)

        mine = pltpu.make_async_copy(x_ref, rows(*me), local_sem)
        mine.start()
        first = [copy(0, me, sibling, src=x_ref)]
        first += [copy(1 + j, me, (*chip, c), src=x_ref) for j, chip in enumerate(chips)]
        for cp in first:
            cp.start()
        passed = [copy(4 + j, (*chip, c), sibling) for j, chip in enumerate(chips)]
        for j, chip in enumerate(chips):
            copy(1 + j, (*chip, c), me).wait_recv()
            passed[j].start()
        copy(0, sibling, me).wait_recv()
        for j, chip in enumerate(chips):
            copy(4 + j, (*chip, 1 - c), me).wait_recv()
        for cp in first + passed:
            cp.wait_send()
        mine.wait()

    return pl.pallas_call(
        body, name=name, out_shape=jax.ShapeDtypeStruct((NDEV * m_per, n), x_shard.dtype),
        in_specs=[pl.BlockSpec(memory_space=pl.ANY)], out_specs=pl.BlockSpec(memory_space=pl.ANY),
        scratch_shapes=[pltpu.SemaphoreType.DMA((7,)), pltpu.SemaphoreType.DMA((7,)), pltpu.SemaphoreType.DMA],
    )(x_shard)


HBM_SPEC = pl.BlockSpec(memory_space=pltpu.HBM)
SEM_SPEC = pl.BlockSpec(memory_space=pltpu.SEMAPHORE)
ANY_SPEC = pl.BlockSpec(memory_space=pl.ANY)
EFFECT = pltpu.SideEffectType.DATAFLOW_SIDE_EFFECTING


def _peer_copies(src_ref, land_ref, send_sems, recv_sems, rows, scatter):
    x, y, c = lax.axis_index("x"), lax.axis_index("y"), lax.axis_index("c")
    me = 4 * x + 2 * y + c
    copies = []
    for k in range(1, NDEV):
        px, py, pc = x ^ ((k >> 2) & 1), y ^ ((k >> 1) & 1), c ^ (k & 1)
        src = src_ref.at[pl.ds((4 * px + 2 * py + pc) * rows, rows), :] if scatter else src_ref
        copies.append(pltpu.make_async_remote_copy(
            src_ref=src, dst_ref=land_ref.at[pl.ds(me * rows, rows), :], send_sem=send_sems.at[k - 1],
            recv_sem=recv_sems.at[k - 1], device_id=(px, py, pc), device_id_type=MESH))
    return copies


def _send_start(src, scatter, after, name):
    rows = src.shape[0] // NDEV if scatter else src.shape[0]
    land = pltpu.with_memory_space_constraint(lax.empty((NDEV * rows, src.shape[1]), src.dtype), pltpu.HBM)

    def body(src_ref, land_ref, after_ref, send_sems, recv_sems, src_thru, land_thru, token):
        for cp in _peer_copies(src_ref, land_ref, send_sems, recv_sems, rows, scatter):
            cp.start()
        token[...] = jnp.zeros_like(token)

    return pl.pallas_call(
        body, name=name,
        out_shape=(pltpu.SemaphoreType.DMA((NDEV - 1,)), pltpu.SemaphoreType.DMA((NDEV - 1,)),
                   pltpu.HBM(src.shape, src.dtype), pltpu.HBM(land.shape, land.dtype), jax.ShapeDtypeStruct((8, HEAD), F32)),
        in_specs=(HBM_SPEC, HBM_SPEC, ANY_SPEC),
        out_specs=(SEM_SPEC, SEM_SPEC, HBM_SPEC, HBM_SPEC, pl.BlockSpec(memory_space=pltpu.VMEM)),
        input_output_aliases={0: 2, 1: 3}, compiler_params=pltpu.CompilerParams(has_side_effects=EFFECT),
    )(pltpu.with_memory_space_constraint(src, pltpu.HBM), land, after)


def _send_wait(started, scatter, after, name):
    send_sems, recv_sems, src_thru, land_thru, _ = started
    rows = land_thru.shape[0] // NDEV

    def body(src_ref, land_ref, send_sems, recv_sems, after_ref, src_dead, got_ref):
        for cp in _peer_copies(src_ref, land_ref, send_sems, recv_sems, rows, scatter):
            cp.wait_send()
            cp.wait_recv()

    return pl.pallas_call(
        body, name=name,
        out_shape=(pltpu.HBM(src_thru.shape, src_thru.dtype), pltpu.HBM(land_thru.shape, land_thru.dtype)),
        in_specs=(HBM_SPEC, HBM_SPEC, SEM_SPEC, SEM_SPEC, ANY_SPEC), out_specs=(HBM_SPEC, HBM_SPEC),
        input_output_aliases={0: 0, 1: 1}, compiler_params=pltpu.CompilerParams(has_side_effects=EFFECT),
    )(src_thru, land_thru, send_sems, recv_sems, after)[1]


def _sum_slots(x, name):
    _, r, c = x.shape
    tr = _pick(r, (512, 256, 128, 64, 32, 16, 8))

    def body(x_ref, o_ref):
        acc = x_ref[0].astype(F32)
        for i in range(1, NDEV):
            acc = acc + x_ref[i].astype(F32)
        o_ref[...] = acc

    return pl.pallas_call(
        body, name=name, grid=(r // tr,), in_specs=[pl.BlockSpec((NDEV, tr, c), lambda i: (0, i, 0))],
        out_specs=pl.BlockSpec((tr, c), lambda i: (i, 0)), out_shape=jax.ShapeDtypeStruct((r, c), F32),
        compiler_params=_params(("parallel",)))(x)


def _adamw(w, g, m, v, name):
    r, c = w.shape
    tr = _pick(r, (256, 128, 64, 32, 16, 8))
    c1 = 1.0 - B1 ** STEP
    c2 = 1.0 - B2 ** STEP

    def body(w_ref, g_ref, m_ref, v_ref, d_ref, nm_ref, nv_ref):
        gv = g_ref[...]
        nm = B1 * m_ref[...] + (1.0 - B1) * gv
        nv = B2 * v_ref[...] + (1.0 - B2) * (gv * gv)
        d_ref[...] = -LR * ((nm / c1) / (jnp.sqrt(nv / c2) + ADAM_EPS) + WD * w_ref[...])
        nm_ref[...] = nm
        nv_ref[...] = nv

    blk = pl.BlockSpec((tr, c), lambda i: (i, 0))
    sds = jax.ShapeDtypeStruct((r, c), F32)
    return pl.pallas_call(
        body, name=name, grid=(r // tr,), in_specs=[blk] * 4, out_specs=[blk] * 3, out_shape=[sds] * 3,
        compiler_params=_params(("parallel",)))(w, g, m, v)


def _pad_rows(a, mult):
    r = a.shape[0]
    pad = (-r) % mult
    return a if pad == 0 else jnp.pad(a, ((0, pad), (0, 0)))


def _pad_lanes(v, width=HEAD):
    return jnp.pad(v.reshape(1, -1), ((0, 0), (0, width - v.shape[-1])))


def kernel(x, p, ln_mix, ln_ffn, ln_ple, gdn_w_in, gdn_conv, gdn_a_log, gdn_dt_bias, gdn_norm, gdn_w_out, kv_norm, w_kv, k_norm, sb_w_q, sb_q_norm, sb_w_out, ffn_w_in, ffn_w_out, ple_w_proj, ple_w_gate, loss_target, m_ln_mix, m_ln_ffn, m_ln_ple, m_gdn_w_in, m_gdn_conv, m_gdn_a_log, m_gdn_dt_bias, m_gdn_norm, m_gdn_w_out, m_kv_norm, m_w_kv, m_k_norm, m_sb_w_q, m_sb_q_norm, m_sb_w_out, m_ffn_w_in, m_ffn_w_out, m_ple_w_proj, m_ple_w_gate, v_ln_mix, v_ln_ffn, v_ln_ple, v_gdn_w_in, v_gdn_conv, v_gdn_a_log, v_gdn_dt_bias, v_gdn_norm, v_gdn_w_out, v_kv_norm, v_w_kv, v_k_norm, v_sb_w_q, v_sb_q_norm, v_sb_w_out, v_ffn_w_in, v_ffn_w_out, v_ple_w_proj, v_ple_w_gate):
    s, d = x.shape[1], x.shape[2]
    nh = d // HEAD
    depth = ln_mix.shape[0]
    n_a = gdn_w_in.shape[0]
    n_b = sb_w_q.shape[0]
    me = _my_index()
    win_cols = gdn_w_in.shape[2]
    win_rows = 4 * d + 2 * nh

    def col_t(w):
        return jnp.transpose(w).astype(BF16)

    local = {}
    for l in range(n_a):
        local[("gdn_w_in", l)] = col_t(gdn_w_in[l])
        local[("gdn_w_out", l)] = gdn_w_out[l].astype(BF16)
    local[("w_kv", 0)] = col_t(w_kv)
    for j in range(n_b):
        local[("sb_w_q", j)] = sb_w_q[j].astype(BF16)
        local[("sb_w_out", j)] = sb_w_out[j].astype(BF16)
    for l in range(depth):
        local[("ffn_w_in", l)] = col_t(ffn_w_in[l])
        local[("ffn_w_out", l)] = ffn_w_out[l].astype(BF16)
        local[("ple_w_proj", l)] = col_t(ple_w_proj[l]).reshape(-1, d)
        local[("ple_w_gate", l)] = ple_w_gate[l].astype(BF16)
    local = {key: _pad_rows(a, 16) for key, a in local.items()}

    chunks = []
    for l in range(depth):
        mix = [("gdn_w_in", l), ("gdn_w_out", l)] if l < n_a else [("sb_w_q", l - n_a), ("sb_w_out", l - n_a)]
        rest = [("ffn_w_in", l), ("ffn_w_out", l), ("ple_w_proj", l), ("ple_w_gate", l)]
        if l == n_a - 1:
            rest.append(("w_kv", 0))
        chunks += [(f"a{l}", mix), (f"f{l}", rest)]
    chunk_keys = dict(chunks)
    chunk_rows = {name: sum(local[k].shape[0] for k in keys) for name, keys in chunks}

    conv_rows = n_a * gdn_conv.shape[1]
    conv_sh = _pad_rows(gdn_conv.reshape(conv_rows, -1), 8)
    conv_g = _all_gather(conv_sh, "comm_gather_conv")
    token = conv_g
    conv_g = conv_g.reshape(NDEV, conv_sh.shape[0], -1)
    conv_full = jnp.transpose(conv_g[:, :conv_rows, :], (1, 0, 2)).reshape(n_a, gdn_conv.shape[1], 3 * d)

    w_started, w_pack = {}, {}
    for name, keys in chunks:
        w_pack[name] = jnp.concatenate([local[k] for k in keys], axis=0)
        w_started[name] = _send_start(w_pack[name], False, token, f"comm_wstart_{name}")
        token = w_started[name][4]

    full = {}

    def fetch(name, after):
        land = _send_wait(w_started[name], False, after, f"comm_wwait_{name}")
        land = lax.dynamic_update_slice(land, w_pack[name], (me * chunk_rows[name], 0))
        land = land.reshape(NDEV, chunk_rows[name], d)
        off = 0
        for key in chunk_keys[name]:
            r = local[key].shape[0]
            full[key] = land[:, off:off + r, :]
            off += r

    def whole(key, valid=None):
        a = full[key]
        if valid is not None:
            a = a[:, :valid, :]
        return a.reshape(-1, d)

    pd = p.shape[-1]
    w_in_t, w_ab_t, w_gout, w_q, w_sout, wf_t, w_fout, wp_t, w_pg = {}, {}, {}, {}, {}, {}, {}, {}, {}
    wkv_t = None

    h = x[0]
    sv = []
    kv_sv = None
    k_sh = v_sh = None
    for l in range(depth):
        t = {}
        t["h0"] = h
        hn = _rms_fwd(h, ln_mix[l], f"rms_mix_{l}")
        t["hn"] = hn
        fetch(f"a{l}", token if l == 0 else hn)
        if l < n_a:
            wt = whole(("gdn_w_in", l), win_cols)
            w_in_t[l] = wt[:4 * d]
            w_ab_t[l] = jnp.pad(wt[4 * d:], ((0, HEAD - 2 * nh), (0, 0)))
            w_gout[l] = whole(("gdn_w_out", l))
        else:
            w_q[l - n_a] = whole(("sb_w_q", l - n_a))
            w_sout[l - n_a] = whole(("sb_w_out", l - n_a))
        if l < n_a:
            proj = _mm(hn, w_in_t[l], "nt", f"gdn_proj_{l}")
            pab = _mm(hn, w_ab_t[l], "nt", f"gdn_proj_ab_{l}")
            qkv = _conv_fwd(proj, conv_full[l], d, f"gdn_conv_{l}")
            al, dtb = _pad_lanes(gdn_a_log[l]), _pad_lanes(gdn_dt_bias[l])
            gb = _gates_fwd(pab, al, dtb, nh, f"gdn_gates_{l}")
            o_raw, states = _gdn_fwd(qkv, gb, nh, f"gdn_rule_{l}")
            o2 = _headnorm_fwd(o_raw, gdn_norm[l], f"gdn_outnorm_{l}", gate=proj, gate_col0=3 * d, head_major=True)
            h = _mm(o2, w_gout[l], "nn", f"gdn_out_{l}", res=h)
            t.update(proj=proj, pab=pab, qkv=qkv, gb=gb, o_raw=o_raw, states=states, o2=o2, al=al, dtb=dtb)
        else:
            j = l - n_a
            qpre = _mm(hn, w_q[j], "nn", f"sb_qproj_{j}")
            qn = _headnorm_fwd(qpre, sb_q_norm[j], f"sb_qnorm_{j}", scale=HEAD ** -0.5)
            o, ctab = _sb_fwd(qn, k_sh, v_sh, f"sb_attn_{j}")
            h = _mm(o, w_sout[j], "nn", f"sb_out_{j}", res=h)
            t.update(qpre=qpre, qn=qn, o=o, ctab=ctab)
        t["h1"] = h
        hn2 = _rms_fwd(h, ln_ffn[l], f"rms_ffn_{l}")
        fetch(f"f{l}", hn2)
        wf_t[l] = whole(("ffn_w_in", l))
        w_fout[l] = whole(("ffn_w_out", l))
        wp_t[l] = full[("ple_w_proj", l)].reshape(d, pd)
        w_pg[l] = whole(("ple_w_gate", l))
        if l == n_a - 1:
            wkv_t = whole(("w_kv", 0))
        act, gs, us = _swiglu_fwd(hn2, wf_t[l], f"ffn_in_{l}")
        h = _mm(act, w_fout[l], "nn", f"ffn_out_{l}", res=h)
        t.update(hn2=hn2, act=act, gs=gs, us=us, h2=h)
        hn3 = _rms_fwd(h, ln_ple[l], f"rms_ple_{l}")
        h, gpre, pp = _ple_fwd(h, hn3, p[l, 0], w_pg[l], wp_t[l], f"ple_{l}")
        t.update(hn3=hn3, gpre=gpre, pp=pp)
        sv.append(t)
        if l == n_a - 1:
            kvn = _rms_fwd(h, kv_norm, "rms_kv")
            kv = _mm(kvn, wkv_t, "nt", "kv_proj")
            k_sh = _headnorm_fwd(kv, k_norm, "k_norm", width=d)
            v_sh = kv[:, d:].astype(BF16)
            kv_sv = dict(h=h, kvn=kvn, kv=kv)

    dh, loss_vec = _loss_fwd_bwd(h, loss_target[0], "loss")
    loss = lax.psum(jnp.sum(loss_vec), ("x", "y", "c"))

    gw = {}
    small = {}
    g_started, g_pack = {}, {}

    def scatter_start(name):
        gparts = []
        for key in chunk_keys[name]:
            g = gw[key]
            g = g.reshape(NDEV, -1, d) if key[0] == "ple_w_proj" else g.reshape(NDEV, -1, g.shape[-1])
            padr = local[key].shape[0] - g.shape[1]
            if padr:
                g = jnp.pad(g, ((0, 0), (0, padr), (0, 0)))
            gparts.append(g)
        g_pack[name] = jnp.concatenate(gparts, axis=1).reshape(NDEV * chunk_rows[name], d)
        g_started[name] = _send_start(g_pack[name], True, gparts[0], f"comm_gstart_{name}")
        return g_started[name][4]

    dk_sh = jnp.zeros((s, d), F32)
    dv_sh = jnp.zeros((s, d), F32)
    for l in reversed(range(depth)):
        t = sv[l]
        if l == n_a - 1:
            dkv_k, dkn = _headnorm_bwd(dk_sh, kv_sv["kv"], k_norm, "k_norm_bwd", dx_dtype=BF16)
            dkv = jnp.concatenate([dkv_k, dv_sh.astype(BF16)], axis=1)
            gw[("w_kv", 0)] = _mm(dkv, kv_sv["kvn"], "tn", "kv_dw", out_dtype=BF16)
            dkvn = _mm(dkv, wkv_t, "nn", "kv_dx")
            dh, dg = _rms_bwd(dkvn, kv_sv["h"], kv_norm, dh, "rms_kv_bwd")
            small["kv_norm"] = dg
            small["k_norm"] = dkn
        dgp, dpp = _ple_bwd(dh, t["gpre"], t["pp"], f"ple_bwd_{l}")
        gw[("ple_w_gate", l)] = _mm(t["hn3"], dgp, "tn", f"ple_dwg_{l}", out_dtype=BF16)
        gw[("ple_w_proj", l)] = _mm(dpp, p[l, 0], "tn", f"ple_dwp_{l}", out_dtype=BF16)
        dhn3 = _mm(dgp, w_pg[l], "nt", f"ple_dx_{l}")
        dh, dg = _rms_bwd(dhn3, t["h2"], ln_ple[l], dh, f"rms_ple_bwd_{l}")
        small[("ln_ple", l)] = dg
        dgs, dus = _swiglu_bwd(dh, w_fout[l], t["gs"], t["us"], f"ffn_bwd_act_{l}")
        gw[("ffn_w_out", l)] = _mm(t["act"], dh, "tn", f"ffn_dwo_{l}", out_dtype=BF16)
        f = dgs.shape[1]
        dwg = _mm(dgs, t["hn2"], "tn", f"ffn_dwg_{l}", out_dtype=BF16)
        dwu = _mm(dus, t["hn2"], "tn", f"ffn_dwu_{l}", out_dtype=BF16)
        gw[("ffn_w_in", l)] = jnp.concatenate([dwg, dwu], axis=0)
        dhn2 = _mm(dgs, wf_t[l][:f], "nn", f"ffn_dxg_{l}")
        dhn2 = _mm(dus, wf_t[l][f:], "nn", f"ffn_dxu_{l}", res=dhn2)
        dh, dg = _rms_bwd(dhn2, t["h1"], ln_ffn[l], dh, f"rms_ffn_bwd_{l}", after=scatter_start(f"f{l}"))
        small[("ln_ffn", l)] = dg
        if l < n_a:
            do2 = _mm(dh, w_gout[l], "nt", f"gdn_out_dx_{l}")
            gw[("gdn_w_out", l)] = _mm(t["o2"], dh, "tn", f"gdn_out_dw_{l}", out_dtype=BF16)
            do_raw, dgn, dgate = _headnorm_bwd(do2, t["o_raw"], gdn_norm[l], f"gdn_outnorm_bwd_{l}",
                                               gate=t["proj"], gate_col0=3 * d, head_major=True)
            small[("gdn_norm", l)] = dgn
            dqkv, dgb = _gdn_bwd(t["qkv"], t["gb"], do_raw, t["states"], nh, f"gdn_rule_bwd_{l}")
            dpab, dal, ddt = _gates_bwd(dgb, t["pab"], t["al"], t["dtb"], nh, f"gdn_gates_bwd_{l}")
            small[("gdn_a_log", l)] = dal
            small[("gdn_dt_bias", l)] = ddt
            dproj_qkv, dconv = _conv_bwd(dqkv, t["proj"], conv_full[l], d, f"gdn_conv_bwd_{l}")
            small[("gdn_conv", l)] = dconv
            dproj = jnp.concatenate([dproj_qkv, dgate], axis=1)
            dw_main = _mm(dproj, t["hn"], "tn", f"gdn_proj_dw_{l}", out_dtype=BF16)
            dw_ab = _mm(dpab, t["hn"], "tn", f"gdn_proj_ab_dw_{l}", out_dtype=BF16)
            gw[("gdn_w_in", l)] = jnp.concatenate([dw_main, dw_ab[:16]], axis=0)[:win_rows]
            dhn = _mm(dproj, w_in_t[l], "nn", f"gdn_proj_dx_{l}")
            dhn = _mm(dpab, w_ab_t[l], "nn", f"gdn_proj_ab_dx_{l}", res=dhn)
        else:
            j = l - n_a
            do = _mm(dh, w_sout[j], "nt", f"sb_out_dx_{j}", out_dtype=BF16)
            gw[("sb_w_out", j)] = _mm(t["o"], dh, "tn", f"sb_out_dw_{j}", out_dtype=BF16)
            dq, dk, dv = _sb_bwd(t["qn"], k_sh, v_sh, do, t["ctab"], f"sb_attn_bwd_{j}")
            dk_sh = dk_sh + dk
            dv_sh = dv_sh + dv
            dqpre, dqn = _headnorm_bwd(dq, t["qpre"], sb_q_norm[j], f"sb_qnorm_bwd_{j}", scale=HEAD ** -0.5, dx_dtype=BF16)
            small[("sb_q_norm", j)] = dqn
            gw[("sb_w_q", j)] = _mm(t["hn"], dqpre, "tn", f"sb_q_dw_{j}", out_dtype=BF16)
            dhn = _mm(dqpre, w_q[j], "nt", f"sb_q_dx_{j}")
        dh, dg = _rms_bwd(dhn, t["h0"], ln_mix[l], dh, f"rms_mix_bwd_{l}", after=scatter_start(f"a{l}"))
        small[("ln_mix", l)] = dg
    grad_x = dh[None]

    gshard = {}
    for name, keys in reversed(chunks):
        rc = chunk_rows[name]
        recv = _send_wait(g_started[name], True, dh, f"comm_gwait_{name}")
        own = lax.dynamic_slice_in_dim(g_pack[name], me * rc, rc, axis=0)
        recv = lax.dynamic_update_slice(recv, own, (me * rc, 0))
        gsum = _sum_slots(recv.reshape(NDEV, rc, d), f"grad_sum_{name}")
        off = 0
        for key in keys:
            r = local[key].shape[0]
            gshard[key] = gsum[off:off + r]
            off += r

    def col_back(key, n_valid):
        return jnp.transpose(gshard[key][:n_valid])

    g_gdn_w_in = jnp.stack([col_back(("gdn_w_in", l), win_cols) for l in range(n_a)])
    g_gdn_w_out = jnp.stack([gshard[("gdn_w_out", l)] for l in range(n_a)])
    g_w_kv = col_back(("w_kv", 0), w_kv.shape[1])
    g_sb_w_q = jnp.stack([gshard[("sb_w_q", j)] for j in range(n_b)])
    g_sb_w_out = jnp.stack([gshard[("sb_w_out", j)] for j in range(n_b)])
    g_ffn_w_in = jnp.stack([col_back(("ffn_w_in", l), ffn_w_in.shape[2]) for l in range(depth)])
    g_ffn_w_out = jnp.stack([gshard[("ffn_w_out", l)] for l in range(depth)])
    g_ple_w_proj = jnp.stack([jnp.transpose(gshard[("ple_w_proj", l)].reshape(-1, pd)) for l in range(depth)])
    g_ple_w_gate = jnp.stack([gshard[("ple_w_gate", l)] for l in range(depth)])

    def vec_rows(v):
        return v.reshape(-1, HEAD)

    small_items = []
    for name_, cnt in (("ln_mix", depth), ("ln_ffn", depth), ("ln_ple", depth)):
        for l in range(cnt):
            small_items.append(((name_, l), vec_rows(small[(name_, l)])))
    for l in range(n_a):
        small_items.append((("gdn_conv", l), small[("gdn_conv", l)].reshape(-1, HEAD)))
        small_items.append((("gdn_a_log", l), small[("gdn_a_log", l)]))
        small_items.append((("gdn_dt_bias", l), small[("gdn_dt_bias", l)]))
        small_items.append((("gdn_norm", l), small[("gdn_norm", l)]))
    small_items.append(("kv_norm", vec_rows(small["kv_norm"])))
    small_items.append(("k_norm", small["k_norm"]))
    for j in range(n_b):
        small_items.append((("sb_q_norm", j), small[("sb_q_norm", j)]))
    spack = jnp.concatenate([_pad_rows(a, 8) for _, a in small_items], axis=0)
    sg = _all_gather(spack, "comm_gather_small").reshape(NDEV, spack.shape[0], HEAD)
    ssum = _sum_slots(sg, "small_sum")
    sm = {}
    off = 0
    for key, a in small_items:
        sm[key] = ssum[off:off + a.shape[0]]
        off += a.shape[0] + (-a.shape[0]) % 8

    g_ln_mix = jnp.stack([sm[("ln_mix", l)].reshape(d) for l in range(depth)])
    g_ln_ffn = jnp.stack([sm[("ln_ffn", l)].reshape(d) for l in range(depth)])
    g_ln_ple = jnp.stack([sm[("ln_ple", l)].reshape(d) for l in range(depth)])
    conv_loc = gdn_conv.shape[2]
    g_conv_full = jnp.stack([sm[("gdn_conv", l)].reshape(gdn_conv.shape[1], 3 * d) for l in range(n_a)])
    g_gdn_conv = lax.dynamic_slice_in_dim(g_conv_full, me * conv_loc, conv_loc, axis=2)
    g_a_log = jnp.stack([sm[("gdn_a_log", l)][0, :nh] for l in range(n_a)])
    g_dt_bias = jnp.stack([sm[("gdn_dt_bias", l)][0, :nh] for l in range(n_a)])
    g_gdn_norm = jnp.stack([sm[("gdn_norm", l)][0] for l in range(n_a)])
    g_kv_norm = sm["kv_norm"].reshape(d)
    g_k_norm = sm["k_norm"][0]
    g_sb_q_norm = jnp.stack([sm[("sb_q_norm", j)][0] for j in range(n_b)])

    grads = [g_ln_mix, g_ln_ffn, g_ln_ple, g_gdn_w_in, g_gdn_conv, g_a_log, g_dt_bias, g_gdn_norm, g_gdn_w_out,
             g_kv_norm, g_w_kv, g_k_norm, g_sb_w_q, g_sb_q_norm, g_sb_w_out, g_ffn_w_in, g_ffn_w_out, g_ple_w_proj,
             g_ple_w_gate]
    weights = [ln_mix, ln_ffn, ln_ple, gdn_w_in, gdn_conv, gdn_a_log, gdn_dt_bias, gdn_norm, gdn_w_out, kv_norm, w_kv,
               k_norm, sb_w_q, sb_q_norm, sb_w_out, ffn_w_in, ffn_w_out, ple_w_proj, ple_w_gate]
    moms = [m_ln_mix, m_ln_ffn, m_ln_ple, m_gdn_w_in, m_gdn_conv, m_gdn_a_log, m_gdn_dt_bias, m_gdn_norm, m_gdn_w_out,
            m_kv_norm, m_w_kv, m_k_norm, m_sb_w_q, m_sb_q_norm, m_sb_w_out, m_ffn_w_in, m_ffn_w_out, m_ple_w_proj,
            m_ple_w_gate]
    vels = [v_ln_mix, v_ln_ffn, v_ln_ple, v_gdn_w_in, v_gdn_conv, v_gdn_a_log, v_gdn_dt_bias, v_gdn_norm, v_gdn_w_out,
            v_kv_norm, v_w_kv, v_k_norm, v_sb_w_q, v_sb_q_norm, v_sb_w_out, v_ffn_w_in, v_ffn_w_out, v_ple_w_proj,
            v_ple_w_gate]

    deltas, new_m, new_v = [], [], []
    small_idx = [i for i, w in enumerate(weights) if w.size < 8 * HEAD * 16]
    for i, (w, g, m, v) in enumerate(zip(weights, grads, moms, vels)):
        if i in small_idx:
            deltas.append(None), new_m.append(None), new_v.append(None)
            continue
        shp = w.shape
        two = lambda a: a.reshape(-1, shp[-1])
        dl, nm, nv = _adamw(two(w), two(g), two(m), two(v), f"adamw_{i}")
        deltas.append(dl.reshape(shp)), new_m.append(nm.reshape(shp)), new_v.append(nv.reshape(shp))

    def flat_pack(arrs):
        flat = jnp.concatenate([a.reshape(-1) for a in arrs])
        pad = (-flat.shape[0]) % (8 * HEAD)
        return jnp.pad(flat, (0, pad)).reshape(-1, HEAD)

    sw = flat_pack([weights[i] for i in small_idx])
    sgr = flat_pack([grads[i] for i in small_idx])
    smo = flat_pack([moms[i] for i in small_idx])
    sve = flat_pack([vels[i] for i in small_idx])
    sdl, snm, snv = _adamw(sw, sgr, smo, sve, "adamw_small")
    off = 0
    for i in small_idx:
        n = weights[i].size
        shp = weights[i].shape
        deltas[i] = sdl.reshape(-1)[off:off + n].reshape(shp)
        new_m[i] = snm.reshape(-1)[off:off + n].reshape(shp)
        new_v[i] = snv.reshape(-1)[off:off + n].reshape(shp)
        off += n

    return (loss, grad_x, *grads, *deltas, *new_m, *new_v)
```

```python
import functools
import math

import jax
import jax.numpy as jnp
from jax import lax
from jax.experimental import pallas as pl
from jax.experimental.pallas import tpu as pltpu

F32 = jnp.float32
BF16 = jnp.bfloat16
NDEV = 8
HEAD = 128
CHUNK = 64
SBLK = 256
EPS = 1e-6
LR, B1, B2, ADAM_EPS, WD, STEP = 0.001, 0.9, 0.999, 1e-08, 0.01, 10
NEG = -1e30
MM_VMEM_BUDGET = 40 * 1024 * 1024

NN = (((1,), (0,)), ((), ()))
NT = (((1,), (1,)), ((), ()))
TN = (((0,), (0,)), ((), ()))
BNN = (((2,), (1,)), ((0,), (0,)))
BNT = (((2,), (2,)), ((0,), (0,)))
BTN = (((1,), (1,)), ((0,), (0,)))
MESH = pl.DeviceIdType.MESH


def _dot(a, b, dims=NN):
    return lax.dot_general(a.astype(BF16), b.astype(BF16), dims, preferred_element_type=F32)


def _dot_hilo(a, b01, dims=NN):
    hi = a.astype(BF16)
    lo = (a - hi.astype(F32)).astype(BF16)
    return (lax.dot_general(hi, b01, dims, preferred_element_type=F32)
            + lax.dot_general(lo, b01, dims, preferred_element_type=F32))


def _pick(dim, cands):
    for c in cands:
        if dim % c == 0:
            return c
    return dim


def _params(sem, vmem_mb=48):
    return pltpu.CompilerParams(dimension_semantics=sem, vmem_limit_bytes=vmem_mb * 1024 * 1024)


def _silu(x):
    return x * jax.nn.sigmoid(x)


def _dsilu(x):
    s = jax.nn.sigmoid(x)
    return s * (1.0 + x * (1.0 - s))


def _mm(a, b, mode, name, out_dtype=F32, res=None):
    if mode == "nn":
        (m, k), n = a.shape, b.shape[1]
    elif mode == "nt":
        (m, k), n = a.shape, b.shape[0]
    else:
        (k, m), n = a.shape, b.shape[1]
    tn = _pick(n, (512, 256, 128))
    tk = k if k <= 4096 else _pick(k, (2048, 1024, 512, 256, 128))
    nk = k // tk
    out_b = jnp.dtype(out_dtype).itemsize + (res.dtype.itemsize if res is not None else 0)
    for tm in [t for t in range(min(m, 2048), 127, -128) if m % t == 0] + [m]:
        need = 2 * (tm * tk * a.dtype.itemsize + tk * tn * b.dtype.itemsize + tm * tn * out_b) + 4 * tm * tn
        if need <= MM_VMEM_BUDGET:
            break
    dims = {"nn": NN, "nt": NT, "tn": TN}[mode]
    if mode == "tn":
        a_spec = pl.BlockSpec((tk, tm), lambda i, j, kk: (kk, i))
    else:
        a_spec = pl.BlockSpec((tm, tk), lambda i, j, kk: (i, kk))
    if mode == "nt":
        b_spec = pl.BlockSpec((tn, tk), lambda i, j, kk: (j, kk))
    else:
        b_spec = pl.BlockSpec((tk, tn), lambda i, j, kk: (kk, j))
    mn_spec = pl.BlockSpec((tm, tn), lambda i, j, kk: (i, j))
    has_res = res is not None

    def body(*refs):
        if has_res:
            a_ref, b_ref, r_ref, o_ref, acc = refs
        else:
            a_ref, b_ref, o_ref, acc = refs
        kk = pl.program_id(2)

        @pl.when(kk == 0)
        def _():
            acc[...] = jnp.zeros_like(acc)

        acc[...] += _dot(a_ref[...], b_ref[...], dims)

        @pl.when(kk == nk - 1)
        def _():
            r = acc[...]
            if has_res:
                r = r + r_ref[...].astype(F32)
            o_ref[...] = r.astype(out_dtype)

    ins = [a, b] + ([res] if has_res else [])
    in_specs = [a_spec, b_spec] + ([mn_spec] if has_res else [])
    return pl.pallas_call(
        body, name=name, grid=(m // tm, n // tn, nk), in_specs=in_specs, out_specs=mn_spec,
        out_shape=jax.ShapeDtypeStruct((m, n), out_dtype), scratch_shapes=[pltpu.VMEM((tm, tn), F32)],
        compiler_params=_params(("parallel", "parallel", "arbitrary")))(*ins)


def _rms_fwd(h, g, name):
    s, d = h.shape
    tm = _pick(s, (512, 256, 128))

    def body(h_ref, g_ref, o_ref):
        x = h_ref[...]
        r = lax.rsqrt(jnp.mean(x * x, axis=-1, keepdims=True) + EPS)
        o_ref[...] = (x * r * g_ref[...]).astype(BF16)

    return pl.pallas_call(
        body, name=name, grid=(s // tm,),
        in_specs=[pl.BlockSpec((tm, d), lambda i: (i, 0)), pl.BlockSpec((1, d), lambda i: (0, 0))],
        out_specs=pl.BlockSpec((tm, d), lambda i: (i, 0)),
        out_shape=jax.ShapeDtypeStruct((s, d), BF16), compiler_params=_params(("parallel",)))(h, g.reshape(1, d))


def _rms_bwd(dy, x, g, dres, name, after=None):
    s, d = x.shape
    tm = _pick(s, (512, 256, 128))

    def body(dy_ref, x_ref, g_ref, dr_ref, *rest):
        dx_ref, dxb_ref, dg_ref = rest[-3:]

        @pl.when(pl.program_id(0) == 0)
        def _():
            dg_ref[...] = jnp.zeros_like(dg_ref)

        xv = x_ref[...]
        dyv = dy_ref[...].astype(F32)
        r = lax.rsqrt(jnp.mean(xv * xv, axis=-1, keepdims=True) + EPS)
        gdy = dyv * g_ref[...]
        mean_t = jnp.mean(xv * gdy, axis=-1, keepdims=True)
        dx = dr_ref[...] + r * gdy - xv * (r * r * r) * mean_t
        dx_ref[...] = dx
        dxb_ref[...] = dx.astype(BF16)
        dg_ref[...] += jnp.sum(dyv * xv * r, axis=0, keepdims=True)

    row = pl.BlockSpec((tm, d), lambda i: (i, 0))
    vec = pl.BlockSpec((1, d), lambda i: (0, 0))
    ins, in_specs = [dy, x, g.reshape(1, d), dres], [row, row, vec, row]
    if after is not None:
        ins.append(after)
        in_specs.append(pl.BlockSpec(memory_space=pl.ANY))
    return pl.pallas_call(
        body, name=name, grid=(s // tm,), in_specs=in_specs, out_specs=[row, row, vec],
        out_shape=[jax.ShapeDtypeStruct((s, d), F32), jax.ShapeDtypeStruct((s, d), BF16), jax.ShapeDtypeStruct((1, d), F32)],
        compiler_params=_params(("arbitrary",)))(*ins)


def _headnorm_fwd(x, g, name, scale=1.0, gate=None, gate_col0=0, out_dtype=BF16, width=None, head_major=False):
    if head_major:
        s, d = x.shape[1], x.shape[0] * HEAD
    else:
        s, d = x.shape[0], (width or x.shape[1])
    nh = d // HEAD
    tm = _pick(s, (256, 128))
    has_gate = gate is not None
    gb = gate_col0 // d

    def body(*refs):
        if has_gate:
            x_ref, g_ref, gt_ref, o_ref = refs
        else:
            x_ref, g_ref, o_ref = refs
        gv = g_ref[...]
        for h in range(nh):
            sl = slice(h * HEAD, (h + 1) * HEAD)
            xv = (x_ref[h] if head_major else x_ref[:, sl]).astype(F32)
            r = lax.rsqrt(jnp.mean(xv * xv, axis=-1, keepdims=True) + EPS)
            y = xv * r * gv
            if scale != 1.0:
                y = y * scale
            if has_gate:
                y = y * _silu(gt_ref[:, sl])
            o_ref[:, sl] = y.astype(out_dtype)

    row = pl.BlockSpec((tm, d), lambda i: (i, 0))
    hm = pl.BlockSpec((nh, tm, HEAD), lambda i: (0, i, 0))
    ins = [x, g.reshape(1, HEAD)]
    in_specs = [hm if head_major else row, pl.BlockSpec((1, HEAD), lambda i: (0, 0))]
    if has_gate:
        ins.append(gate)
        in_specs.append(pl.BlockSpec((tm, d), lambda i: (i, gb)))
    return pl.pallas_call(
        body, name=name, grid=(s // tm,), in_specs=in_specs, out_specs=row,
        out_shape=jax.ShapeDtypeStruct((s, d), out_dtype), compiler_params=_params(("parallel",)))(*ins)


def _headnorm_bwd(dy, x, g, name, scale=1.0, gate=None, gate_col0=0, dx_dtype=F32, head_major=False):
    s, d = dy.shape
    nh = d // HEAD
    tm = _pick(s, (256, 128))
    has_gate = gate is not None
    gb = gate_col0 // d

    def body(*refs):
        if has_gate:
            dy_ref, x_ref, g_ref, gt_ref, dx_ref, dg_ref, dgt_ref = refs
        else:
            dy_ref, x_ref, g_ref, dx_ref, dg_ref = refs

        @pl.when(pl.program_id(0) == 0)
        def _():
            dg_ref[...] = jnp.zeros_like(dg_ref)

        gv = g_ref[...]
        dg_acc = jnp.zeros((1, HEAD), F32)
        for h in range(nh):
            sl = slice(h * HEAD, (h + 1) * HEAD)
            xv = (x_ref[h] if head_major else x_ref[:, sl]).astype(F32)
            dyv = dy_ref[:, sl].astype(F32)
            r = lax.rsqrt(jnp.mean(xv * xv, axis=-1, keepdims=True) + EPS)
            if has_gate:
                gt = gt_ref[:, sl]
                dgt_ref[:, sl] = (dyv * (xv * r * gv) * _dsilu(gt)).astype(dgt_ref.dtype)
                dn = dyv * _silu(gt)
            else:
                dn = dyv
            if scale != 1.0:
                dn = dn * scale
            gdn = dn * gv
            mean_t = jnp.mean(xv * gdn, axis=-1, keepdims=True)
            dxv = (r * gdn - xv * (r * r * r) * mean_t).astype(dx_dtype)
            if head_major:
                dx_ref[h] = dxv
            else:
                dx_ref[:, sl] = dxv
            dg_acc = dg_acc + jnp.sum(dn * xv * r, axis=0, keepdims=True)
        dg_ref[...] += dg_acc

    row = pl.BlockSpec((tm, d), lambda i: (i, 0))
    hm = pl.BlockSpec((nh, tm, HEAD), lambda i: (0, i, 0))
    vec = pl.BlockSpec((1, HEAD), lambda i: (0, 0))
    ins = [dy, x, g.reshape(1, HEAD)]
    in_specs = [row, hm if head_major else row, vec]
    out_specs = [hm if head_major else row, vec]
    dx_shape = (nh, s, HEAD) if head_major else (s, d)
    out_shape = [jax.ShapeDtypeStruct(dx_shape, dx_dtype), jax.ShapeDtypeStruct((1, HEAD), F32)]
    if has_gate:
        ins.append(gate)
        in_specs.append(pl.BlockSpec((tm, d), lambda i: (i, gb)))
        out_specs.append(row)
        out_shape.append(jax.ShapeDtypeStruct((s, d), BF16))
    return pl.pallas_call(
        body, name=name, grid=(s // tm,), in_specs=in_specs, out_specs=out_specs, out_shape=out_shape,
        compiler_params=_params(("arbitrary",)))(*ins)


def _swiglu_fwd(hn, wf_t, name):
    s, d = hn.shape
    f = wf_t.shape[0] // 2
    tm = _pick(s, (1024, 512, 256, 128))
    tn = _pick(f, (512, 256, 128))
    nj = f // tn

    def body(a_ref, wg_ref, wu_ref, act_ref, g_ref, u_ref):
        a = a_ref[...]
        g = _dot(a, wg_ref[...], NT)
        u = _dot(a, wu_ref[...], NT)
        act_ref[...] = (_silu(g) * u).astype(BF16)
        g_ref[...] = g.astype(BF16)
        u_ref[...] = u.astype(BF16)

    o_spec = pl.BlockSpec((tm, tn), lambda i, j: (i, j))
    sds = jax.ShapeDtypeStruct((s, f), BF16)
    return pl.pallas_call(
        body, name=name, grid=(s // tm, nj),
        in_specs=[pl.BlockSpec((tm, d), lambda i, j: (i, 0)), pl.BlockSpec((tn, d), lambda i, j: (j, 0)),
                  pl.BlockSpec((tn, d), lambda i, j: (j + nj, 0))],
        out_specs=[o_spec, o_spec, o_spec], out_shape=[sds, sds, sds],
        compiler_params=_params(("parallel", "parallel")))(hn, wf_t, wf_t)


def _swiglu_bwd(dh, w_out, g, u, name):
    s, d = dh.shape
    f = w_out.shape[0]
    tm = _pick(s, (1024, 512, 256, 128))
    tn = _pick(f, (512, 256, 128))

    def body(dh_ref, w_ref, g_ref, u_ref, dg_ref, du_ref):
        dact = _dot(dh_ref[...], w_ref[...], NT)
        gv = g_ref[...].astype(F32)
        uv = u_ref[...].astype(F32)
        dg_ref[...] = (dact * uv * _dsilu(gv)).astype(BF16)
        du_ref[...] = (dact * _silu(gv)).astype(BF16)

    o_spec = pl.BlockSpec((tm, tn), lambda i, j: (i, j))
    sds = jax.ShapeDtypeStruct((s, f), BF16)
    return pl.pallas_call(
        body, name=name, grid=(s // tm, f // tn),
        in_specs=[pl.BlockSpec((tm, d), lambda i, j: (i, 0)), pl.BlockSpec((tn, d), lambda i, j: (j, 0)), o_spec, o_spec],
        out_specs=[o_spec, o_spec], out_shape=[sds, sds],
        compiler_params=_params(("parallel", "parallel")))(dh, w_out, g, u)


def _ple_fwd(h, hn, p, w_gate, wp_t, name):
    s, d = h.shape
    pd = p.shape[1]
    tm = _pick(s, (512, 256, 128))
    tn = _pick(d, (512, 256, 128))

    def body(h_ref, hn_ref, p_ref, wg_ref, wp_ref, o_ref, gp_ref, pp_ref):
        gpre = _dot(hn_ref[...], wg_ref[...], NN)
        pp = _dot(p_ref[...], wp_ref[...], NT)
        o_ref[...] = h_ref[...] + pp * jax.nn.sigmoid(gpre)
        gp_ref[...] = gpre.astype(BF16)
        pp_ref[...] = pp.astype(BF16)

    mn = pl.BlockSpec((tm, tn), lambda i, j: (i, j))
    return pl.pallas_call(
        body, name=name, grid=(s // tm, d // tn),
        in_specs=[mn, pl.BlockSpec((tm, d), lambda i, j: (i, 0)), pl.BlockSpec((tm, pd), lambda i, j: (i, 0)),
                  pl.BlockSpec((d, tn), lambda i, j: (0, j)), pl.BlockSpec((tn, pd), lambda i, j: (j, 0))],
        out_specs=[mn, mn, mn],
        out_shape=[jax.ShapeDtypeStruct((s, d), F32), jax.ShapeDtypeStruct((s, d), BF16), jax.ShapeDtypeStruct((s, d), BF16)],
        compiler_params=_params(("parallel", "parallel")))(h, hn, p, w_gate, wp_t)


def _ple_bwd(dh, gpre, pp, name):
    s, d = dh.shape
    tm = _pick(s, (512, 256, 128))

    def body(dh_ref, gp_ref, pp_ref, dgp_ref, dpp_ref):
        dv = dh_ref[...]
        sig = jax.nn.sigmoid(gp_ref[...].astype(F32))
        ppv = pp_ref[...].astype(F32)
        dpp_ref[...] = (dv * sig).astype(BF16)
        dgp_ref[...] = (dv * ppv * sig * (1.0 - sig)).astype(BF16)

    row = pl.BlockSpec((tm, d), lambda i: (i, 0))
    sds = jax.ShapeDtypeStruct((s, d), BF16)
    return pl.pallas_call(
        body, name=name, grid=(s // tm,), in_specs=[row, row, row], out_specs=[row, row], out_shape=[sds, sds],
        compiler_params=_params(("parallel",)))(dh, gpre, pp)


def _loss_fwd_bwd(y, t, name):
    s, d = y.shape
    tm = _pick(s, (512, 256, 128))

    def body(y_ref, t_ref, dy_ref, l_ref):
        @pl.when(pl.program_id(0) == 0)
        def _():
            l_ref[...] = jnp.zeros_like(l_ref)

        e = y_ref[...] - t_ref[...]
        dy_ref[...] = e * (1.0 / d)
        l_ref[...] += jnp.sum(e * e, axis=0, keepdims=True) * (0.5 / d)

    row = pl.BlockSpec((tm, d), lambda i: (i, 0))
    vec = pl.BlockSpec((1, d), lambda i: (0, 0))
    return pl.pallas_call(
        body, name=name, grid=(s // tm,), in_specs=[row, row], out_specs=[row, vec],
        out_shape=[jax.ShapeDtypeStruct((s, d), F32), jax.ShapeDtypeStruct((1, d), F32)],
        compiler_params=_params(("arbitrary",)))(y, t)


PADR = 8


def _conv_fwd(proj, w_conv, d, name):
    s = proj.shape[0]
    nh = d // HEAD
    kw = w_conv.shape[0]
    qscale = HEAD ** -0.5

    def body(x_ref, w_ref, o_ref, xp):
        kind = pl.program_id(0) // nh
        xp[0:PADR, :] = jnp.zeros((PADR, HEAD), F32)
        xp[PADR:, :] = x_ref[...]
        acc = jnp.zeros((s, HEAD), F32)
        for j in range(kw):
            acc = acc + w_ref[j:j + 1, :] * xp[PADR - (kw - 1) + j:PADR - (kw - 1) + j + s, :]
        a = _silu(acc)
        r = lax.rsqrt(jnp.sum(a * a, axis=-1, keepdims=True) + EPS)
        fac = jnp.where(kind == 0, r * qscale, jnp.where(kind == 1, r, jnp.ones_like(r)))
        o_ref[...] = a * fac

    blk = pl.BlockSpec((s, HEAD), lambda c: (0, c))
    hm = pl.BlockSpec((None, s, HEAD), lambda c: (c, 0, 0))
    return pl.pallas_call(
        body, name=name, grid=(3 * nh,), in_specs=[blk, pl.BlockSpec((kw, HEAD), lambda c: (0, c))], out_specs=hm,
        out_shape=jax.ShapeDtypeStruct((3 * nh, s, HEAD), F32), scratch_shapes=[pltpu.VMEM((s + PADR, HEAD), F32)],
        compiler_params=_params(("parallel",)))(proj, w_conv)


def _conv_bwd(dqkv, proj, w_conv, d, name):
    s = proj.shape[0]
    nh = d // HEAD
    kw = w_conv.shape[0]
    qscale = HEAD ** -0.5

    def body(dy_ref, x_ref, w_ref, dx_ref, dw_ref, xp, dp):
        kind = pl.program_id(0) // nh
        xp[0:PADR, :] = jnp.zeros((PADR, HEAD), F32)
        xp[PADR:, :] = x_ref[...]
        acc = jnp.zeros((s, HEAD), F32)
        for j in range(kw):
            acc = acc + w_ref[j:j + 1, :] * xp[PADR - (kw - 1) + j:PADR - (kw - 1) + j + s, :]
        a = _silu(acc)
        dy = dy_ref[...]
        r = lax.rsqrt(jnp.sum(a * a, axis=-1, keepdims=True) + EPS)
        sc = jnp.where(kind == 0, qscale, 1.0)
        dyn = dy * sc
        da_norm = r * dyn - a * (r * r * r) * jnp.sum(a * dyn, axis=-1, keepdims=True)
        da = jnp.where(kind == 2, dy, da_norm)
        dacc = da * _dsilu(acc)
        dp[0:s, :] = dacc
        dp[s:, :] = jnp.zeros((PADR, HEAD), F32)
        dx = jnp.zeros((s, HEAD), F32)
        for j in range(kw):
            sh = kw - 1 - j
            dx = dx + w_ref[j:j + 1, :] * dp[sh:sh + s, :]
            dw_ref[j:j + 1, :] = jnp.sum(dacc * xp[PADR - sh:PADR - sh + s, :], axis=0, keepdims=True)
        dx_ref[...] = dx.astype(BF16)

    blk = pl.BlockSpec((s, HEAD), lambda c: (0, c))
    hm = pl.BlockSpec((None, s, HEAD), lambda c: (c, 0, 0))
    wblk = pl.BlockSpec((kw, HEAD), lambda c: (0, c))
    return pl.pallas_call(
        body, name=name, grid=(3 * nh,), in_specs=[hm, blk, wblk], out_specs=[blk, wblk],
        out_shape=[jax.ShapeDtypeStruct((s, 3 * d), BF16), jax.ShapeDtypeStruct((kw, 3 * d), F32)],
        scratch_shapes=[pltpu.VMEM((s + PADR, HEAD), F32), pltpu.VMEM((s + PADR, HEAD), F32)],
        compiler_params=_params(("parallel",)))(dqkv, proj, w_conv)


def _softplus(x):
    return jnp.maximum(x, 0.0) + jnp.log(1.0 + jnp.exp(-jnp.abs(x)))


def _gates_fwd(pab, a_log, dt_bias, nh, name):
    s = pab.shape[0]
    tm = _pick(s, (512, 256, 128))

    def body(x_ref, al_ref, dt_ref, o_ref):
        x = x_ref[...]
        lane = lax.broadcasted_iota(jnp.int32, x.shape, 1)
        g = -jnp.exp(al_ref[...]) * _softplus(x + dt_ref[...])
        o_ref[...] = jnp.where(lane < nh, g, jnp.where(lane < 2 * nh, jax.nn.sigmoid(x), 0.0))

    row = pl.BlockSpec((tm, HEAD), lambda i: (i, 0))
    vec = pl.BlockSpec((1, HEAD), lambda i: (0, 0))
    return pl.pallas_call(
        body, name=name, grid=(s // tm,), in_specs=[row, vec, vec], out_specs=row,
        out_shape=jax.ShapeDtypeStruct((s, HEAD), F32), compiler_params=_params(("parallel",)))(pab, a_log, dt_bias)


def _gates_bwd(dgb, pab, a_log, dt_bias, nh, name):
    s = pab.shape[0]
    tm = _pick(s, (512, 256, 128))

    def body(d_ref, x_ref, al_ref, dt_ref, dx_ref, dal_ref, ddt_ref):
        @pl.when(pl.program_id(0) == 0)
        def _():
            dal_ref[...] = jnp.zeros_like(dal_ref)
            ddt_ref[...] = jnp.zeros_like(ddt_ref)

        x = x_ref[...]
        dv = d_ref[...]
        lane = lax.broadcasted_iota(jnp.int32, x.shape, 1)
        ea = jnp.exp(al_ref[...])
        xs = x + dt_ref[...]
        g = -ea * _softplus(xs)
        dxs = jnp.where(lane < nh, dv * (-ea) * jax.nn.sigmoid(xs), 0.0)
        sg = jax.nn.sigmoid(x)
        dxb = jnp.where((lane >= nh) & (lane < 2 * nh), dv * sg * (1.0 - sg), 0.0)
        dx_ref[...] = (dxs + dxb).astype(BF16)
        dal_ref[...] += jnp.sum(jnp.where(lane < nh, dv * g, 0.0), axis=0, keepdims=True)
        ddt_ref[...] += jnp.sum(dxs, axis=0, keepdims=True)

    row = pl.BlockSpec((tm, HEAD), lambda i: (i, 0))
    vec = pl.BlockSpec((1, HEAD), lambda i: (0, 0))
    return pl.pallas_call(
        body, name=name, grid=(s // tm,), in_specs=[row, row, vec, vec], out_specs=[row, vec, vec],
        out_shape=[jax.ShapeDtypeStruct((s, HEAD), BF16), jax.ShapeDtypeStruct((1, HEAD), F32),
                   jax.ShapeDtypeStruct((1, HEAD), F32)],
        compiler_params=_params(("arbitrary",)))(dgb, pab, a_log, dt_bias)


def _tri_inv(a_low, eye_f):
    n = -a_low
    p = eye_f + n
    steps = int(math.log2(a_low.shape[-1])) - 1
    for _ in range(steps):
        n = _dot(n, n, BNN)
        p = p + _dot(p, n, BNN)
    return p


def _lane_col(x, lane, idx):
    return jnp.sum(jnp.where(lane == idx, x, 0.0), axis=1, keepdims=True)


def _head_cols(gbv, lo, nh):
    lane = lax.broadcasted_iota(jnp.int32, gbv.shape, 1)
    return jnp.stack([_lane_col(gbv, lane, lo + h) for h in range(nh)], axis=0)


def _gdn_chunk(q, k, v, g_col, beta_col, st):
    c = q.shape[1]
    r_i = lax.broadcasted_iota(jnp.int32, (c, c), 0)
    c_i = lax.broadcasted_iota(jnp.int32, (c, c), 1)
    incl = c_i <= r_i
    strict = c_i < r_i
    eye = c_i == r_i
    g_row = jnp.sum(jnp.where(eye, g_col, 0.0), axis=1, keepdims=True)
    gc_col = jnp.sum(jnp.where(incl, g_row, 0.0), axis=2, keepdims=True)
    gc_row = jnp.sum(jnp.where(eye, gc_col, 0.0), axis=1, keepdims=True)
    g_last = jnp.sum(g_col, axis=1, keepdims=True)
    decay = jnp.exp(jnp.where(incl, gc_col - gc_row, NEG))
    kk = _dot(k, k, BNT)
    a_low = jnp.where(strict, beta_col * kk * decay, 0.0)
    t_inv = _tri_inv(a_low, eye.astype(F32))
    e_g = jnp.exp(gc_col)
    bk = beta_col * e_g
    rhs = jnp.concatenate([v * beta_col, k * bk], axis=2)
    sol = _dot(t_inv, rhs, BNN)
    u, w = sol[:, :, :HEAD], sol[:, :, HEAD:]
    qk_raw = _dot(q, k, BNT)
    qk = qk_raw * decay
    q_dec = q * e_g
    e2 = jnp.exp(g_last - gc_col)
    k_dec = k * e2
    gl = jnp.exp(g_last)
    ws = _dot(jnp.concatenate([w, q_dec], axis=1), st, BNN)
    v_new = u - ws[:, :c]
    o = ws[:, c:] + _dot(qk, v_new, BNN)
    st_new = st * gl + _dot(k_dec, v_new, BTN)
    inter = dict(incl=incl, strict=strict, eye=eye, decay=decay, kk=kk, t_inv=t_inv, e_g=e_g, bk=bk, sol=sol, w=w,
                 qk_raw=qk_raw, qk=qk, q_dec=q_dec, e2=e2, k_dec=k_dec, gl=gl, v_new=v_new, c_i=c_i, r_i=r_i)
    return o, st_new, inter


def _gdn_fwd(qkv, gb, nh, name):
    s = qkv.shape[1]
    nc = s // CHUNK

    def body(q_ref, k_ref, v_ref, gb_ref, o_ref, st_ref, state):
        @pl.when(pl.program_id(0) == 0)
        def _():
            state[...] = jnp.zeros_like(state)

        gbv = gb_ref[...]
        st = state[...]
        st_ref[...] = st
        o, st_new, _ = _gdn_chunk(q_ref[...], k_ref[...], v_ref[...], _head_cols(gbv, 0, nh), _head_cols(gbv, nh, nh), st)
        o_ref[...] = o
        state[...] = st_new

    def qspec(part):
        return pl.BlockSpec((nh, CHUNK, HEAD), lambda n: (part, n, 0))

    return pl.pallas_call(
        body, name=name, grid=(nc,),
        in_specs=[qspec(0), qspec(1), qspec(2), pl.BlockSpec((CHUNK, HEAD), lambda n: (n, 0))],
        out_specs=[qspec(0), pl.BlockSpec((None, nh, HEAD, HEAD), lambda n: (n, 0, 0, 0))],
        out_shape=[jax.ShapeDtypeStruct((nh, s, HEAD), F32), jax.ShapeDtypeStruct((nc, nh, HEAD, HEAD), F32)],
        scratch_shapes=[pltpu.VMEM((nh, HEAD, HEAD), F32)],
        compiler_params=_params(("arbitrary",)))(qkv, qkv, qkv, gb)


def _gdn_bwd(qkv, gb, do, states, nh, name):
    s = qkv.shape[1]
    nc = s // CHUNK
    c = CHUNK

    def body(q_ref, k_ref, v_ref, gb_ref, do_ref, st_ref, dqkv_ref, dgb_ref, dstate):
        @pl.when(pl.program_id(0) == 0)
        def _():
            dstate[...] = jnp.zeros_like(dstate)

        gbv = gb_ref[...]
        lane = lax.broadcasted_iota(jnp.int32, gbv.shape, 1)
        q, k, v = q_ref[...], k_ref[...], v_ref[...]
        beta_col = _head_cols(gbv, nh, nh)
        st = st_ref[...]
        dst = dstate[...]
        dov = do_ref[...]
        _, _, it = _gdn_chunk(q, k, v, _head_cols(gbv, 0, nh), beta_col, st)
        incl, strict, eye, decay = it["incl"], it["strict"], it["eye"], it["decay"]
        dv_new = _dot(it["qk"], dov, BTN) + _dot(it["k_dec"], dst, BNN)
        d_qk = _dot(dov, it["v_new"], BNT)
        dd = _dot(jnp.concatenate([dov, -dv_new], axis=1), st, BNT)
        dq_dec, dw = dd[:, :c], dd[:, c:]
        dst_new = _dot(it["q_dec"], dov, BTN) + it["gl"] * dst - _dot(it["w"], dv_new, BTN)
        dgl = jnp.sum(jnp.sum(dst * st, axis=2, keepdims=True), axis=1, keepdims=True)
        dk_dec = _dot(it["v_new"], dst, BNT)
        dsol = jnp.concatenate([dv_new, dw], axis=2)
        drhs = _dot(it["t_inv"], dsol, BTN)
        d_a = jnp.where(strict, -_dot(drhs, it["sol"], BNT), 0.0)
        drhs_u, drhs_w = drhs[:, :, :HEAD], drhs[:, :, HEAD:]
        dvh = beta_col * drhs_u
        rw_k = jnp.sum(drhs_w * k, axis=2, keepdims=True)
        dbeta = jnp.sum(drhs_u * v, axis=2, keepdims=True) + it["e_g"] * rw_k
        dkh = it["bk"] * drhs_w
        dgc_col = it["bk"] * rw_k
        dkk = d_a * beta_col * decay
        dbeta = dbeta + jnp.sum(d_a * it["kk"] * decay, axis=2, keepdims=True)
        ddecay = d_a * beta_col * it["kk"]
        dkh = dkh + _dot(dkk, k, BNN) + _dot(dkk, k, BTN)
        dqk_raw = d_qk * decay
        ddecay = ddecay + d_qk * it["qk_raw"]
        dqh = _dot(dqk_raw, k, BNN)
        dkh = dkh + _dot(dqk_raw, q, BTN)
        ddm = jnp.where(incl, ddecay * decay, 0.0)
        dgc_col = dgc_col + jnp.sum(ddm, axis=2, keepdims=True)
        dgc_row = -jnp.sum(ddm, axis=1, keepdims=True)
        dqh = dqh + dq_dec * it["e_g"]
        dgc_col = dgc_col + jnp.sum(dq_dec * it["q_dec"], axis=2, keepdims=True)
        dkh = dkh + dk_dec * it["e2"]
        tmp = jnp.sum(dk_dec * it["k_dec"], axis=2, keepdims=True)
        dgc_col = dgc_col - tmp
        dg_last = jnp.sum(tmp, axis=1, keepdims=True) + dgl * it["gl"]
        dgc_tot_row = dgc_row + jnp.sum(jnp.where(eye, dgc_col, 0.0), axis=1, keepdims=True)
        dg_col = jnp.sum(jnp.where(it["c_i"] >= it["r_i"], dgc_tot_row, 0.0), axis=2, keepdims=True) + dg_last
        dqkv_ref[0] = dqh
        dqkv_ref[1] = dkh
        dqkv_ref[2] = dvh
        dstate[...] = dst_new
        dgb_acc = jnp.zeros(gbv.shape, F32)
        for h in range(nh):
            dgb_acc = jnp.where(lane == h, dg_col[h], jnp.where(lane == nh + h, dbeta[h], dgb_acc))
        dgb_ref[...] = dgb_acc

    def rev(part):
        return pl.BlockSpec((nh, CHUNK, HEAD), lambda n: (part, nc - 1 - n, 0))

    gspec = pl.BlockSpec((CHUNK, HEAD), lambda n: (nc - 1 - n, 0))
    dqkv, dgb = pl.pallas_call(
        body, name=name, grid=(nc,),
        in_specs=[rev(0), rev(1), rev(2), gspec, rev(0),
                  pl.BlockSpec((None, nh, HEAD, HEAD), lambda n: (nc - 1 - n, 0, 0, 0))],
        out_specs=[pl.BlockSpec((3, nh, CHUNK, HEAD), lambda n: (0, 0, nc - 1 - n, 0)), gspec],
        out_shape=[jax.ShapeDtypeStruct((3, nh, s, HEAD), F32), jax.ShapeDtypeStruct((s, HEAD), F32)],
        scratch_shapes=[pltpu.VMEM((nh, HEAD, HEAD), F32)],
        compiler_params=_params(("arbitrary",)))(qkv, qkv, qkv, gb, do, states)
    return dqkv.reshape(3 * nh, s, HEAD), dgb


SB_TQ = 512


def _tri01(rel):
    j_i = lax.broadcasted_iota(jnp.int32, (SBLK, SBLK), 0)
    s_i = lax.broadcasted_iota(jnp.int32, (SBLK, SBLK), 1)
    return rel(j_i, s_i).astype(BF16)


SB_HP = 2


def _each(fn, *lists):
    return [fn(*xs) for xs in zip(*lists)]


def _sb_scores(qts, kblks, mask, csums, rhs01):
    zs = _each(lambda qt, kb: _dot(qt, kb, NT), qts, kblks)
    es = _each(lambda z: jnp.exp(-jnp.abs(z)), zs)
    sps = _each(lambda z, e: jnp.maximum(z, 0.0) + jnp.log(1.0 + e), zs, es)
    lns = _each(lambda sp: -sp if mask is None else jnp.where(mask, -sp, 0.0), sps)
    sts = _each(lambda ln: _dot_hilo(ln, rhs01), lns)
    wgts = _each(lambda z, sp, st, cs: jnp.exp((z - sp) + st[:, :SBLK] + cs), zs, sps, sts, csums)
    if mask is not None:
        wgts = _each(lambda w: jnp.where(mask, w, 0.0), wgts)
    return zs, es, wgts, sts


def _band_mask(rows, j, row0):
    r_i = lax.broadcasted_iota(jnp.int32, (rows, SBLK), 0)
    c_i = lax.broadcasted_iota(jnp.int32, (rows, SBLK), 1)
    return (j * SBLK + c_i) < (row0 + r_i)


def _sb_fwd(q, k, v, name):
    s, d = q.shape
    nh = d // HEAD
    tq = min(SB_TQ, s)
    nb = tq // SBLK

    hp = SB_HP
    heads = [slice(h * HEAD, (h + 1) * HEAD) for h in range(hp)]

    def body(q_ref, k_ref, v_ref, o_ref, c_ref, acc, cs):
        qb = pl.program_id(1)
        lane = lax.broadcasted_iota(jnp.int32, (tq, HEAD), 1)
        rhs01 = jnp.concatenate([_tri01(lambda j, t: j > t), jnp.ones((SBLK, SBLK), BF16)], axis=1)
        acc[...] = jnp.zeros_like(acc)
        cs[...] = jnp.zeros_like(cs)
        c_ref[...] = jnp.zeros_like(c_ref)

        def process(rs, kb, mask):
            keys = pl.ds(pl.multiple_of(kb * SBLK, SBLK), SBLK)
            csums = [cs[h, rs, :] for h in range(hp)]
            _, _, wgts, sts = _sb_scores([q_ref[rs, hs] for hs in heads], [k_ref[keys, hs] for hs in heads], mask, csums, rhs01)
            pvs = _each(lambda w, hs: _dot(w, v_ref[keys, hs]), wgts, heads)
            for h, hs in enumerate(heads):
                acc[h, rs, :] += pvs[h]
                c_ref[rs, hs] = jnp.where(lane[rs, :] == kb, csums[h][:, :HEAD], c_ref[rs, hs])
                cs[h, rs, :] = csums[h] + sts[h][:, SBLK:]

        for j in reversed(range(nb)):
            process(slice(j * SBLK, tq), qb * nb + j, _band_mask(tq - j * SBLK, j, j * SBLK))

        def step(it, carry):
            process(slice(0, tq), qb * nb - 1 - it, None)
            return carry

        lax.fori_loop(0, qb * nb, step, 0)
        for h, hs in enumerate(heads):
            o_ref[:, hs] = acc[h].astype(BF16)

    qspec = pl.BlockSpec((tq, hp * HEAD), lambda h, i: (i, h))
    kspec = pl.BlockSpec((s, hp * HEAD), lambda h, i: (0, h))
    return pl.pallas_call(
        body, name=name, grid=(nh // hp, s // tq), in_specs=[qspec, kspec, kspec], out_specs=[qspec, qspec],
        out_shape=[jax.ShapeDtypeStruct((s, d), BF16), jax.ShapeDtypeStruct((s, d), F32)],
        scratch_shapes=[pltpu.VMEM((hp, tq, HEAD), F32), pltpu.VMEM((hp, tq, SBLK), F32)],
        compiler_params=_params(("parallel", "arbitrary")))(q, k, v)


def _sb_bwd(q, k, v, do, ctab, name):
    s, d = q.shape
    nh = d // HEAD
    tq = min(SB_TQ, s)
    nb = tq // SBLK

    hp = SB_HP
    heads = [slice(h * HEAD, (h + 1) * HEAD) for h in range(hp)]

    def body(q_ref, k_ref, v_ref, do_ref, c_ref, dq_ref, dk_ref, dv_ref, ps):
        qb = pl.program_id(1)

        @pl.when(qb == 0)
        def _():
            dk_ref[...] = jnp.zeros_like(dk_ref)
            dv_ref[...] = jnp.zeros_like(dv_ref)

        dq_ref[...] = jnp.zeros_like(dq_ref)
        ps[...] = jnp.zeros_like(ps)
        lane = lax.broadcasted_iota(jnp.int32, (tq, HEAD), 1)
        after = _tri01(lambda j, t: j > t)
        rhs_pre = jnp.concatenate([_tri01(lambda j, t: j < t), jnp.ones((SBLK, SBLK), BF16)], axis=1)

        def process(rs, kb, mask):
            keys = pl.ds(pl.multiple_of(kb * SBLK, SBLK), SBLK)
            kblks = [k_ref[keys, hs] for hs in heads]
            qts = [q_ref[rs, hs] for hs in heads]
            dots = [do_ref[rs, hs] for hs in heads]
            csums = [_lane_col(c_ref[rs, hs], lane[rs, :], kb) for hs in heads]
            zs, es, wgts, _ = _sb_scores(qts, kblks, mask, csums, after)
            dlws = _each(lambda dt, hs, w: _dot(dt, v_ref[keys, hs], NT) * w, dots, heads, wgts)
            pts = _each(lambda dlw: _dot_hilo(dlw, rhs_pre), dlws)
            pfxs = [ps[h, rs, :] for h in range(hp)]
            rs_ = _each(lambda e: 1.0 / (1.0 + e), es)
            sigs = _each(lambda z, e, r: jnp.where(z >= 0.0, r, e * r), zs, es, rs_)
            dzs = _each(lambda dlw, sig, pfx, pt: dlw * (1.0 - sig) - sig * (pfx + pt[:, :SBLK]), dlws, sigs, pfxs, pts)
            if mask is not None:
                dzs = _each(lambda dz: jnp.where(mask, dz, 0.0), dzs)
            dqs = _each(lambda dz, kb_: _dot(dz, kb_), dzs, kblks)
            dks = _each(lambda dz, qt: _dot(dz, qt, TN), dzs, qts)
            dvs = _each(lambda w, dt: _dot(w, dt, TN), wgts, dots)
            for h, hs in enumerate(heads):
                dq_ref[rs, hs] += dqs[h]
                dk_ref[keys, hs] += dks[h]
                dv_ref[keys, hs] += dvs[h]
                ps[h, rs, :] = pfxs[h] + pts[h][:, SBLK:]

        def step(kb, carry):
            process(slice(0, tq), kb, None)
            return carry

        lax.fori_loop(0, qb * nb, step, 0)
        for j in range(nb):
            process(slice(j * SBLK, tq), qb * nb + j, _band_mask(tq - j * SBLK, j, j * SBLK))

    qspec = pl.BlockSpec((tq, hp * HEAD), lambda h, i: (i, h))
    kspec = pl.BlockSpec((s, hp * HEAD), lambda h, i: (0, h))
    sds = jax.ShapeDtypeStruct((s, d), F32)
    return pl.pallas_call(
        body, name=name, grid=(nh // hp, s // tq), in_specs=[qspec, kspec, kspec, qspec, qspec],
        out_specs=[qspec, kspec, kspec], out_shape=[sds, sds, sds],
        scratch_shapes=[pltpu.VMEM((hp, tq, SBLK), F32)],
        compiler_params=_params(("parallel", "arbitrary")))(q, k, v, do, ctab)


def _my_index():
    return 4 * lax.axis_index("x") + 2 * lax.axis_index("y") + lax.axis_index("c")


def _all_gather(x_shard, name):
    m_per, n = x_shard.shape

    def body(x_ref, out_ref, send_sems, recv_sems, local_sem):
        x, y, c = lax.axis_index("x"), lax.axis_index("y"), lax.axis_index("c")
        me, sibling = (x, y, c), (x, y, 1 - c)
        chips = [(1 - x, y), (x, 1 - y), (1 - x, 1 - y)]

        def rows(px, py, pc):
            return out_ref.at[pl.ds((4 * px + 2 * py + pc) * m_per, m_per), :]

        def copy(k, block, to, src=None):
            return pltpu.make_async_remote_copy(
                src_ref=rows(*block) if src is None else src, dst_ref=rows(*block),
                send_sem=send_sems.at[k], recv_sem=recv_sems.at[k], device_id=to, device_id_type=MESH)

        mine = pltpu.make_async_copy(x_ref, rows(*me), local_sem)
        mine.start()
        first = [copy(0, me, sibling, src=x_ref)]
        first += [copy(1 + j, me, (*chip, c), src=x_ref) for j, chip in enumerate(chips)]
        for cp in first:
            cp.start()
        passed = [copy(4 + j, (*chip, c), sibling) for j, chip in enumerate(chips)]
        for j, chip in enumerate(chips):
            copy(1 + j, (*chip, c), me).wait_recv()
            passed[j].start()
        copy(0, sibling, me).wait_recv()
        for j, chip in enumerate(chips):
            copy(4 + j, (*chip, 1 - c), me).wait_recv()
        for cp in first + passed:
            cp.wait_send()
        mine.wait()

    return pl.pallas_call(
        body, name=name, out_shape=jax.ShapeDtypeStruct((NDEV * m_per, n), x_shard.dtype),
        in_specs=[pl.BlockSpec(memory_space=pl.ANY)], out_specs=pl.BlockSpec(memory_space=pl.ANY),
        scratch_shapes=[pltpu.SemaphoreType.DMA((7,)), pltpu.SemaphoreType.DMA((7,)), pltpu.SemaphoreType.DMA],
    )(x_shard)


HBM_SPEC = pl.BlockSpec(memory_space=pltpu.HBM)
SEM_SPEC = pl.BlockSpec(memory_space=pltpu.SEMAPHORE)
ANY_SPEC = pl.BlockSpec(memory_space=pl.ANY)
EFFECT = pltpu.SideEffectType.DATAFLOW_SIDE_EFFECTING


def _exchange_copies(src_refs, land_refs, send_sems, recv_sems, self_sems, scatter):
    x, y, c = lax.axis_index("x"), lax.axis_index("y"), lax.axis_index("c")
    me = 4 * x + 2 * y + c
    remote, local = [], []
    for p, (src_ref, land_ref) in enumerate(zip(src_refs, land_refs)):
        rows = land_ref.shape[0] // NDEV

        def part(idx):
            return src_ref.at[pl.ds(idx * rows, rows), :] if scatter else src_ref

        slot = land_ref.at[pl.ds(me * rows, rows), :]
        for k in range(1, NDEV):
            px, py, pc = x ^ ((k >> 2) & 1), y ^ ((k >> 1) & 1), c ^ (k & 1)
            remote.append(pltpu.make_async_remote_copy(
                src_ref=part(4 * px + 2 * py + pc), dst_ref=slot, send_sem=send_sems.at[7 * p + k - 1],
                recv_sem=recv_sems.at[7 * p + k - 1], device_id=(px, py, pc), device_id_type=MESH))
        local.append(pltpu.make_async_copy(part(me), slot, self_sems.at[p]))
    return remote, local


def _send_start(srcs, scatter, after, name):
    n = len(srcs)
    lands = []
    for s in srcs:
        rows = s.shape[0] if scatter else NDEV * s.shape[0]
        lands.append(pltpu.with_memory_space_constraint(lax.empty((rows, s.shape[1]), s.dtype), pltpu.HBM))

    def body(*refs):
        src_refs, land_refs = refs[:n], refs[n:2 * n]
        send_sems, recv_sems, self_sems = refs[2 * n + 1:2 * n + 4]
        remote, local = _exchange_copies(src_refs, land_refs, send_sems, recv_sems, self_sems, scatter)
        for cp in remote + local:
            cp.start()
        refs[-1][...] = jnp.zeros_like(refs[-1])

    hbm = lambda a: pltpu.HBM(a.shape, a.dtype)
    out = pl.pallas_call(
        body, name=name,
        out_shape=(pltpu.SemaphoreType.DMA((7 * n,)), pltpu.SemaphoreType.DMA((7 * n,)), pltpu.SemaphoreType.DMA((n,)),
                   *[hbm(s) for s in srcs], *[hbm(a) for a in lands], jax.ShapeDtypeStruct((8, HEAD), F32)),
        in_specs=(HBM_SPEC,) * (2 * n) + (ANY_SPEC,),
        out_specs=(SEM_SPEC,) * 3 + (HBM_SPEC,) * (2 * n) + (pl.BlockSpec(memory_space=pltpu.VMEM),),
        input_output_aliases={i: 3 + i for i in range(2 * n)},
        compiler_params=pltpu.CompilerParams(has_side_effects=EFFECT),
    )(*[pltpu.with_memory_space_constraint(s, pltpu.HBM) for s in srcs], *lands, after)
    return dict(sems=out[:3], srcs=out[3:3 + n], lands=out[3 + n:3 + 2 * n], token=out[-1])


def _send_wait(started, scatter, after, name):
    srcs, lands = started["srcs"], started["lands"]
    n = len(srcs)

    def body(*refs):
        src_refs, land_refs = refs[:n], refs[n:2 * n]
        send_sems, recv_sems, self_sems = refs[2 * n:2 * n + 3]
        remote, local = _exchange_copies(src_refs, land_refs, send_sems, recv_sems, self_sems, scatter)
        for cp in remote:
            cp.wait_send()
            cp.wait_recv()
        for cp in local:
            cp.wait()

    hbm = lambda a: pltpu.HBM(a.shape, a.dtype)
    out = pl.pallas_call(
        body, name=name, out_shape=(*[hbm(s) for s in srcs], *[hbm(a) for a in lands]),
        in_specs=(HBM_SPEC,) * (2 * n) + (SEM_SPEC,) * 3 + (ANY_SPEC,), out_specs=(HBM_SPEC,) * (2 * n),
        input_output_aliases={i: i for i in range(2 * n)},
        compiler_params=pltpu.CompilerParams(has_side_effects=EFFECT),
    )(*srcs, *lands, *started["sems"], after)
    return out[n:]


def _sum_slots(x, name):
    _, r, c = x.shape
    tr = _pick(r, (512, 256, 128, 64, 32, 16, 8))

    def body(x_ref, o_ref):
        acc = x_ref[0].astype(F32)
        for i in range(1, NDEV):
            acc = acc + x_ref[i].astype(F32)
        o_ref[...] = acc

    return pl.pallas_call(
        body, name=name, grid=(r // tr,), in_specs=[pl.BlockSpec((NDEV, tr, c), lambda i: (0, i, 0))],
        out_specs=pl.BlockSpec((tr, c), lambda i: (i, 0)), out_shape=jax.ShapeDtypeStruct((r, c), F32),
        compiler_params=_params(("parallel",)))(x)


def _adamw(w, g, m, v, name):
    r, c = w.shape
    tr = _pick(r, (256, 128, 64, 32, 16, 8))
    c1 = 1.0 - B1 ** STEP
    c2 = 1.0 - B2 ** STEP

    def body(w_ref, g_ref, m_ref, v_ref, d_ref, nm_ref, nv_ref):
        gv = g_ref[...]
        nm = B1 * m_ref[...] + (1.0 - B1) * gv
        nv = B2 * v_ref[...] + (1.0 - B2) * (gv * gv)
        d_ref[...] = -LR * ((nm / c1) / (jnp.sqrt(nv / c2) + ADAM_EPS) + WD * w_ref[...])
        nm_ref[...] = nm
        nv_ref[...] = nv

    blk = pl.BlockSpec((tr, c), lambda i: (i, 0))
    sds = jax.ShapeDtypeStruct((r, c), F32)
    return pl.pallas_call(
        body, name=name, grid=(r // tr,), in_specs=[blk] * 4, out_specs=[blk] * 3, out_shape=[sds] * 3,
        compiler_params=_params(("parallel",)))(w, g, m, v)


def _pad_rows(a, mult):
    r = a.shape[0]
    pad = (-r) % mult
    return a if pad == 0 else jnp.pad(a, ((0, pad), (0, 0)))


def _pad_lanes(v, width=HEAD):
    return jnp.pad(v.reshape(1, -1), ((0, 0), (0, width - v.shape[-1])))


def kernel(x, p, ln_mix, ln_ffn, ln_ple, gdn_w_in, gdn_conv, gdn_a_log, gdn_dt_bias, gdn_norm, gdn_w_out, kv_norm, w_kv, k_norm, sb_w_q, sb_q_norm, sb_w_out, ffn_w_in, ffn_w_out, ple_w_proj, ple_w_gate, loss_target, m_ln_mix, m_ln_ffn, m_ln_ple, m_gdn_w_in, m_gdn_conv, m_gdn_a_log, m_gdn_dt_bias, m_gdn_norm, m_gdn_w_out, m_kv_norm, m_w_kv, m_k_norm, m_sb_w_q, m_sb_q_norm, m_sb_w_out, m_ffn_w_in, m_ffn_w_out, m_ple_w_proj, m_ple_w_gate, v_ln_mix, v_ln_ffn, v_ln_ple, v_gdn_w_in, v_gdn_conv, v_gdn_a_log, v_gdn_dt_bias, v_gdn_norm, v_gdn_w_out, v_kv_norm, v_w_kv, v_k_norm, v_sb_w_q, v_sb_q_norm, v_sb_w_out, v_ffn_w_in, v_ffn_w_out, v_ple_w_proj, v_ple_w_gate):
    s, d = x.shape[1], x.shape[2]
    nh = d // HEAD
    depth = ln_mix.shape[0]
    n_a = gdn_w_in.shape[0]
    n_b = sb_w_q.shape[0]
    me = _my_index()
    win_cols = gdn_w_in.shape[2]
    win_rows = 4 * d + 2 * nh

    def col_t(w):
        return jnp.transpose(w).astype(BF16)

    local = {}
    for l in range(n_a):
        local[("gdn_w_in", l)] = col_t(gdn_w_in[l])
        local[("gdn_w_out", l)] = gdn_w_out[l].astype(BF16)
    local[("w_kv", 0)] = col_t(w_kv)
    for j in range(n_b):
        local[("sb_w_q", j)] = sb_w_q[j].astype(BF16)
        local[("sb_w_out", j)] = sb_w_out[j].astype(BF16)
    for l in range(depth):
        local[("ffn_w_in", l)] = col_t(ffn_w_in[l])
        local[("ffn_w_out", l)] = ffn_w_out[l].astype(BF16)
        local[("ple_w_proj", l)] = col_t(ple_w_proj[l]).reshape(-1, d)
        local[("ple_w_gate", l)] = ple_w_gate[l].astype(BF16)
    local = {key: _pad_rows(a, 16) for key, a in local.items()}

    chunks = []
    for l in range(depth):
        mix = [("gdn_w_in", l), ("gdn_w_out", l)] if l < n_a else [("sb_w_q", l - n_a), ("sb_w_out", l - n_a)]
        rest = [("ffn_w_in", l), ("ffn_w_out", l), ("ple_w_proj", l), ("ple_w_gate", l)]
        if l == n_a - 1:
            rest.append(("w_kv", 0))
        chunks += [(f"a{l}", mix), (f"f{l}", rest)]
    chunk_keys = dict(chunks)

    conv_rows = n_a * gdn_conv.shape[1]
    conv_sh = _pad_rows(gdn_conv.reshape(conv_rows, -1), 8)
    conv_g = _all_gather(conv_sh, "comm_gather_conv")
    token = conv_g
    conv_g = conv_g.reshape(NDEV, conv_sh.shape[0], -1)
    conv_full = jnp.transpose(conv_g[:, :conv_rows, :], (1, 0, 2)).reshape(n_a, gdn_conv.shape[1], 3 * d)

    w_started = {}
    for name, keys in chunks:
        w_started[name] = _send_start([local[k] for k in keys], False, token, f"comm_wstart_{name}")
        token = w_started[name]["token"]

    full = {}

    def fetch(name, after):
        lands = _send_wait(w_started[name], False, after, f"comm_wwait_{name}")
        for key, land in zip(chunk_keys[name], lands):
            full[key] = land

    def whole(key, valid=None):
        a = full[key]
        if valid is not None:
            a = a.reshape(NDEV, -1, d)[:, :valid, :].reshape(-1, d)
        return a

    pd = p.shape[-1]
    w_in_t, w_ab_t, w_gout, w_q, w_sout, wf_t, w_fout, wp_t, w_pg = {}, {}, {}, {}, {}, {}, {}, {}, {}
    wkv_t = None

    h = x[0]
    sv = []
    kv_sv = None
    k_sh = v_sh = None
    for l in range(depth):
        t = {}
        t["h0"] = h
        hn = _rms_fwd(h, ln_mix[l], f"rms_mix_{l}")
        t["hn"] = hn
        fetch(f"a{l}", token if l == 0 else hn)
        if l < n_a:
            wt = whole(("gdn_w_in", l), win_cols)
            w_in_t[l] = wt[:4 * d]
            w_ab_t[l] = jnp.pad(wt[4 * d:], ((0, HEAD - 2 * nh), (0, 0)))
            w_gout[l] = whole(("gdn_w_out", l))
        else:
            w_q[l - n_a] = whole(("sb_w_q", l - n_a))
            w_sout[l - n_a] = whole(("sb_w_out", l - n_a))
        if l < n_a:
            proj = _mm(hn, w_in_t[l], "nt", f"gdn_proj_{l}")
            pab = _mm(hn, w_ab_t[l], "nt", f"gdn_proj_ab_{l}")
            qkv = _conv_fwd(proj, conv_full[l], d, f"gdn_conv_{l}")
            al, dtb = _pad_lanes(gdn_a_log[l]), _pad_lanes(gdn_dt_bias[l])
            gb = _gates_fwd(pab, al, dtb, nh, f"gdn_gates_{l}")
            o_raw, states = _gdn_fwd(qkv, gb, nh, f"gdn_rule_{l}")
            o2 = _headnorm_fwd(o_raw, gdn_norm[l], f"gdn_outnorm_{l}", gate=proj, gate_col0=3 * d, head_major=True)
            h = _mm(o2, w_gout[l], "nn", f"gdn_out_{l}", res=h)
            t.update(proj=proj, pab=pab, qkv=qkv, gb=gb, o_raw=o_raw, states=states, o2=o2, al=al, dtb=dtb)
        else:
            j = l - n_a
            qpre = _mm(hn, w_q[j], "nn", f"sb_qproj_{j}")
            qn = _headnorm_fwd(qpre, sb_q_norm[j], f"sb_qnorm_{j}", scale=HEAD ** -0.5)
            o, ctab = _sb_fwd(qn, k_sh, v_sh, f"sb_attn_{j}")
            h = _mm(o, w_sout[j], "nn", f"sb_out_{j}", res=h)
            t.update(qpre=qpre, qn=qn, o=o, ctab=ctab)
        t["h1"] = h
        hn2 = _rms_fwd(h, ln_ffn[l], f"rms_ffn_{l}")
        fetch(f"f{l}", hn2)
        wf_t[l] = whole(("ffn_w_in", l))
        w_fout[l] = whole(("ffn_w_out", l))
        wp_t[l] = full[("ple_w_proj", l)].reshape(d, pd)
        w_pg[l] = whole(("ple_w_gate", l))
        if l == n_a - 1:
            wkv_t = whole(("w_kv", 0))
        act, gs, us = _swiglu_fwd(hn2, wf_t[l], f"ffn_in_{l}")
        h = _mm(act, w_fout[l], "nn", f"ffn_out_{l}", res=h)
        t.update(hn2=hn2, act=act, gs=gs, us=us, h2=h)
        hn3 = _rms_fwd(h, ln_ple[l], f"rms_ple_{l}")
        h, gpre, pp = _ple_fwd(h, hn3, p[l, 0], w_pg[l], wp_t[l], f"ple_{l}")
        t.update(hn3=hn3, gpre=gpre, pp=pp)
        sv.append(t)
        if l == n_a - 1:
            kvn = _rms_fwd(h, kv_norm, "rms_kv")
            kv = _mm(kvn, wkv_t, "nt", "kv_proj")
            k_sh = _headnorm_fwd(kv, k_norm, "k_norm", width=d)
            v_sh = kv[:, d:].astype(BF16)
            kv_sv = dict(h=h, kvn=kvn, kv=kv)

    dh, loss_vec = _loss_fwd_bwd(h, loss_target[0], "loss")
    loss = lax.psum(jnp.sum(loss_vec), ("x", "y", "c"))

    gw = {}
    small = {}
    g_started = {}

    def scatter_start(name):
        gparts = []
        for key in chunk_keys[name]:
            g = gw[key]
            g = g.reshape(NDEV, -1, d) if key[0] == "ple_w_proj" else g.reshape(NDEV, -1, g.shape[-1])
            padr = local[key].shape[0] - g.shape[1]
            if padr:
                g = jnp.pad(g, ((0, 0), (0, padr), (0, 0)))
            gparts.append(g.reshape(-1, d))
        g_started[name] = _send_start(gparts, True, gparts[0], f"comm_gstart_{name}")
        return g_started[name]["token"]

    dk_sh = jnp.zeros((s, d), F32)
    dv_sh = jnp.zeros((s, d), F32)
    for l in reversed(range(depth)):
        t = sv[l]
        if l == n_a - 1:
            dkv_k, dkn = _headnorm_bwd(dk_sh, kv_sv["kv"], k_norm, "k_norm_bwd", dx_dtype=BF16)
            dkv = jnp.concatenate([dkv_k, dv_sh.astype(BF16)], axis=1)
            gw[("w_kv", 0)] = _mm(dkv, kv_sv["kvn"], "tn", "kv_dw", out_dtype=BF16)
            dkvn = _mm(dkv, wkv_t, "nn", "kv_dx")
            dh, _, dg = _rms_bwd(dkvn, kv_sv["h"], kv_norm, dh, "rms_kv_bwd")
            small["kv_norm"] = dg
            small["k_norm"] = dkn
        dgp, dpp = _ple_bwd(dh, t["gpre"], t["pp"], f"ple_bwd_{l}")
        gw[("ple_w_gate", l)] = _mm(t["hn3"], dgp, "tn", f"ple_dwg_{l}", out_dtype=BF16)
        gw[("ple_w_proj", l)] = _mm(dpp, p[l, 0], "tn", f"ple_dwp_{l}", out_dtype=BF16)
        dhn3 = _mm(dgp, w_pg[l], "nt", f"ple_dx_{l}")
        dh, dhb, dg = _rms_bwd(dhn3, t["h2"], ln_ple[l], dh, f"rms_ple_bwd_{l}")
        small[("ln_ple", l)] = dg
        dgs, dus = _swiglu_bwd(dhb, w_fout[l], t["gs"], t["us"], f"ffn_bwd_act_{l}")
        gw[("ffn_w_out", l)] = _mm(t["act"], dhb, "tn", f"ffn_dwo_{l}", out_dtype=BF16)
        f = dgs.shape[1]
        dwg = _mm(dgs, t["hn2"], "tn", f"ffn_dwg_{l}", out_dtype=BF16)
        dwu = _mm(dus, t["hn2"], "tn", f"ffn_dwu_{l}", out_dtype=BF16)
        gw[("ffn_w_in", l)] = jnp.concatenate([dwg, dwu], axis=0)
        dhn2 = _mm(dgs, wf_t[l][:f], "nn", f"ffn_dxg_{l}")
        dhn2 = _mm(dus, wf_t[l][f:], "nn", f"ffn_dxu_{l}", res=dhn2)
        dh, dhb, dg = _rms_bwd(dhn2, t["h1"], ln_ffn[l], dh, f"rms_ffn_bwd_{l}", after=scatter_start(f"f{l}"))
        small[("ln_ffn", l)] = dg
        if l < n_a:
            do2 = _mm(dhb, w_gout[l], "nt", f"gdn_out_dx_{l}")
            gw[("gdn_w_out", l)] = _mm(t["o2"], dhb, "tn", f"gdn_out_dw_{l}", out_dtype=BF16)
            do_raw, dgn, dgate = _headnorm_bwd(do2, t["o_raw"], gdn_norm[l], f"gdn_outnorm_bwd_{l}",
                                               gate=t["proj"], gate_col0=3 * d, head_major=True)
            small[("gdn_norm", l)] = dgn
            dqkv, dgb = _gdn_bwd(t["qkv"], t["gb"], do_raw, t["states"], nh, f"gdn_rule_bwd_{l}")
            dpab, dal, ddt = _gates_bwd(dgb, t["pab"], t["al"], t["dtb"], nh, f"gdn_gates_bwd_{l}")
            small[("gdn_a_log", l)] = dal
            small[("gdn_dt_bias", l)] = ddt
            dproj_qkv, dconv = _conv_bwd(dqkv, t["proj"], conv_full[l], d, f"gdn_conv_bwd_{l}")
            small[("gdn_conv", l)] = dconv
            dproj = jnp.concatenate([dproj_qkv, dgate], axis=1)
            dw_main = _mm(dproj, t["hn"], "tn", f"gdn_proj_dw_{l}", out_dtype=BF16)
            dw_ab = _mm(dpab, t["hn"], "tn", f"gdn_proj_ab_dw_{l}", out_dtype=BF16)
            gw[("gdn_w_in", l)] = jnp.concatenate([dw_main, dw_ab[:16]], axis=0)[:win_rows]
            dhn = _mm(dproj, w_in_t[l], "nn", f"gdn_proj_dx_{l}")
            dhn = _mm(dpab, w_ab_t[l], "nn", f"gdn_proj_ab_dx_{l}", res=dhn)
        else:
            j = l - n_a
            do = _mm(dhb, w_sout[j], "nt", f"sb_out_dx_{j}", out_dtype=BF16)
            gw[("sb_w_out", j)] = _mm(t["o"], dhb, "tn", f"sb_out_dw_{j}", out_dtype=BF16)
            dq, dk, dv = _sb_bwd(t["qn"], k_sh, v_sh, do, t["ctab"], f"sb_attn_bwd_{j}")
            dk_sh = dk_sh + dk
            dv_sh = dv_sh + dv
            dqpre, dqn = _headnorm_bwd(dq, t["qpre"], sb_q_norm[j], f"sb_qnorm_bwd_{j}", scale=HEAD ** -0.5, dx_dtype=BF16)
            small[("sb_q_norm", j)] = dqn
            gw[("sb_w_q", j)] = _mm(t["hn"], dqpre, "tn", f"sb_q_dw_{j}", out_dtype=BF16)
            dhn = _mm(dqpre, w_q[j], "nt", f"sb_q_dx_{j}")
        dh, _, dg = _rms_bwd(dhn, t["h0"], ln_mix[l], dh, f"rms_mix_bwd_{l}", after=scatter_start(f"a{l}"))
        small[("ln_mix", l)] = dg
    grad_x = dh[None]

    gshard = {}
    for name, keys in reversed(chunks):
        lands = _send_wait(g_started[name], True, dh, f"comm_gwait_{name}")
        for key, land in zip(keys, lands):
            gshard[key] = _sum_slots(land.reshape(NDEV, -1, d), f"grad_sum_{key[0]}_{key[1]}")

    def col_back(key, n_valid):
        return jnp.transpose(gshard[key][:n_valid])

    g_gdn_w_in = jnp.stack([col_back(("gdn_w_in", l), win_cols) for l in range(n_a)])
    g_gdn_w_out = jnp.stack([gshard[("gdn_w_out", l)] for l in range(n_a)])
    g_w_kv = col_back(("w_kv", 0), w_kv.shape[1])
    g_sb_w_q = jnp.stack([gshard[("sb_w_q", j)] for j in range(n_b)])
    g_sb_w_out = jnp.stack([gshard[("sb_w_out", j)] for j in range(n_b)])
    g_ffn_w_in = jnp.stack([col_back(("ffn_w_in", l), ffn_w_in.shape[2]) for l in range(depth)])
    g_ffn_w_out = jnp.stack([gshard[("ffn_w_out", l)] for l in range(depth)])
    g_ple_w_proj = jnp.stack([jnp.transpose(gshard[("ple_w_proj", l)].reshape(-1, pd)) for l in range(depth)])
    g_ple_w_gate = jnp.stack([gshard[("ple_w_gate", l)] for l in range(depth)])

    def vec_rows(v):
        return v.reshape(-1, HEAD)

    small_items = []
    for name_, cnt in (("ln_mix", depth), ("ln_ffn", depth), ("ln_ple", depth)):
        for l in range(cnt):
            small_items.append(((name_, l), vec_rows(small[(name_, l)])))
    for l in range(n_a):
        small_items.append((("gdn_conv", l), small[("gdn_conv", l)].reshape(-1, HEAD)))
        small_items.append((("gdn_a_log", l), small[("gdn_a_log", l)]))
        small_items.append((("gdn_dt_bias", l), small[("gdn_dt_bias", l)]))
        small_items.append((("gdn_norm", l), small[("gdn_norm", l)]))
    small_items.append(("kv_norm", vec_rows(small["kv_norm"])))
    small_items.append(("k_norm", small["k_norm"]))
    for j in range(n_b):
        small_items.append((("sb_q_norm", j), small[("sb_q_norm", j)]))
    spack = jnp.concatenate([_pad_rows(a, 8) for _, a in small_items], axis=0)
    sg = _all_gather(spack, "comm_gather_small").reshape(NDEV, spack.shape[0], HEAD)
    ssum = _sum_slots(sg, "small_sum")
    sm = {}
    off = 0
    for key, a in small_items:
        sm[key] = ssum[off:off + a.shape[0]]
        off += a.shape[0] + (-a.shape[0]) % 8

    g_ln_mix = jnp.stack([sm[("ln_mix", l)].reshape(d) for l in range(depth)])
    g_ln_ffn = jnp.stack([sm[("ln_ffn", l)].reshape(d) for l in range(depth)])
    g_ln_ple = jnp.stack([sm[("ln_ple", l)].reshape(d) for l in range(depth)])
    conv_loc = gdn_conv.shape[2]
    g_conv_full = jnp.stack([sm[("gdn_conv", l)].reshape(gdn_conv.shape[1], 3 * d) for l in range(n_a)])
    g_gdn_conv = lax.dynamic_slice_in_dim(g_conv_full, me * conv_loc, conv_loc, axis=2)
    g_a_log = jnp.stack([sm[("gdn_a_log", l)][0, :nh] for l in range(n_a)])
    g_dt_bias = jnp.stack([sm[("gdn_dt_bias", l)][0, :nh] for l in range(n_a)])
    g_gdn_norm = jnp.stack([sm[("gdn_norm", l)][0] for l in range(n_a)])
    g_kv_norm = sm["kv_norm"].reshape(d)
    g_k_norm = sm["k_norm"][0]
    g_sb_q_norm = jnp.stack([sm[("sb_q_norm", j)][0] for j in range(n_b)])

    grads = [g_ln_mix, g_ln_ffn, g_ln_ple, g_gdn_w_in, g_gdn_conv, g_a_log, g_dt_bias, g_gdn_norm, g_gdn_w_out,
             g_kv_norm, g_w_kv, g_k_norm, g_sb_w_q, g_sb_q_norm, g_sb_w_out, g_ffn_w_in, g_ffn_w_out, g_ple_w_proj,
             g_ple_w_gate]
    weights = [ln_mix, ln_ffn, ln_ple, gdn_w_in, gdn_conv, gdn_a_log, gdn_dt_bias, gdn_norm, gdn_w_out, kv_norm, w_kv,
               k_norm, sb_w_q, sb_q_norm, sb_w_out, ffn_w_in, ffn_w_out, ple_w_proj, ple_w_gate]
    moms = [m_ln_mix, m_ln_ffn, m_ln_ple, m_gdn_w_in, m_gdn_conv, m_gdn_a_log, m_gdn_dt_bias, m_gdn_norm, m_gdn_w_out,
            m_kv_norm, m_w_kv, m_k_norm, m_sb_w_q, m_sb_q_norm, m_sb_w_out, m_ffn_w_in, m_ffn_w_out, m_ple_w_proj,
            m_ple_w_gate]
    vels = [v_ln_mix, v_ln_ffn, v_ln_ple, v_gdn_w_in, v_gdn_conv, v_gdn_a_log, v_gdn_dt_bias, v_gdn_norm, v_gdn_w_out,
            v_kv_norm, v_w_kv, v_k_norm, v_sb_w_q, v_sb_q_norm, v_sb_w_out, v_ffn_w_in, v_ffn_w_out, v_ple_w_proj,
            v_ple_w_gate]

    deltas, new_m, new_v = [], [], []
    small_idx = [i for i, w in enumerate(weights) if w.size < 8 * HEAD * 16]
    for i, (w, g, m, v) in enumerate(zip(weights, grads, moms, vels)):
        if i in small_idx:
            deltas.append(None), new_m.append(None), new_v.append(None)
            continue
        shp = w.shape
        two = lambda a: a.reshape(-1, shp[-1])
        dl, nm, nv = _adamw(two(w), two(g), two(m), two(v), f"adamw_{i}")
        deltas.append(dl.reshape(shp)), new_m.append(nm.reshape(shp)), new_v.append(nv.reshape(shp))

    def flat_pack(arrs):
        flat = jnp.concatenate([a.reshape(-1) for a in arrs])
        pad = (-flat.shape[0]) % (8 * HEAD)
        return jnp.pad(flat, (0, pad)).reshape(-1, HEAD)

    sw = flat_pack([weights[i] for i in small_idx])
    sgr = flat_pack([grads[i] for i in small_idx])
    smo = flat_pack([moms[i] for i in small_idx])
    sve = flat_pack([vels[i] for i in small_idx])
    sdl, snm, snv = _adamw(sw, sgr, smo, sve, "adamw_small")
    off = 0
    for i in small_idx:
        n = weights[i].size
        shp = weights[i].shape
        deltas[i] = sdl.reshape(-1)[off:off + n].reshape(shp)
        new_m[i] = snm.reshape(-1)[off:off + n].reshape(shp)
        new_v[i] = snv.reshape(-1)[off:off + n].reshape(shp)
        off += n

    return (loss, grad_x, *grads, *deltas, *new_m, *new_v)
```

```python
import math

import jax
import jax.numpy as jnp
from jax import lax
from jax.experimental import pallas as pl
from jax.experimental.pallas import tpu as pltpu

F32 = jnp.float32
BF16 = jnp.bfloat16
NDEV = 8
HEAD = 128
CHUNK = 64
SBLK = 256
EPS = 1e-6
LR, B1, B2, ADAM_EPS, WD, STEP = 0.001, 0.9, 0.999, 1e-08, 0.01, 10
NEG = -1e30
MM_VMEM_BUDGET = 40 * 1024 * 1024

NN = (((1,), (0,)), ((), ()))
NT = (((1,), (1,)), ((), ()))
TN = (((0,), (0,)), ((), ()))
BNN = (((2,), (1,)), ((0,), (0,)))
BNT = (((2,), (2,)), ((0,), (0,)))
BTN = (((1,), (1,)), ((0,), (0,)))
MESH = pl.DeviceIdType.MESH


def _dot(a, b, dims=NN):
    return lax.dot_general(a.astype(BF16), b.astype(BF16), dims, preferred_element_type=F32)


def _dot_hilo(a, b01, dims=NN):
    hi = a.astype(BF16)
    lo = (a - hi.astype(F32)).astype(BF16)
    return (lax.dot_general(hi, b01, dims, preferred_element_type=F32)
            + lax.dot_general(lo, b01, dims, preferred_element_type=F32))


def _pick(dim, cands):
    for c in cands:
        if dim % c == 0:
            return c
    return dim


def _params(sem, vmem_mb=48):
    return pltpu.CompilerParams(dimension_semantics=sem, vmem_limit_bytes=vmem_mb * 1024 * 1024)


def _silu(x):
    return x * jax.nn.sigmoid(x)


def _dsilu(x):
    s = jax.nn.sigmoid(x)
    return s * (1.0 + x * (1.0 - s))


def _mm(a, b, mode, name, out_dtype=F32, res=None, norm_g=None, norm_bwd=None, after=None):
    if mode == "nn":
        (m, k), n = a.shape, b.shape[1]
    elif mode == "nt":
        (m, k), n = a.shape, b.shape[0]
    else:
        (k, m), n = a.shape, b.shape[1]
    rows = norm_g is not None or norm_bwd is not None
    tn = n if rows else _pick(n, (512, 256, 128))
    tk = k if k <= 4096 else _pick(k, (2048, 1024, 512, 256, 128))
    nk = k // tk
    out_b = jnp.dtype(out_dtype).itemsize + (res.dtype.itemsize if res is not None else 0)
    out_b += 2 if norm_g is not None else 0
    out_b += 10 if norm_bwd is not None else 0
    for tm in [t for t in range(min(m, 2048), 127, -128) if m % t == 0] + [m]:
        need = 2 * (tm * tk * a.dtype.itemsize + tk * tn * b.dtype.itemsize + tm * tn * out_b) + 4 * tm * tn
        if need <= MM_VMEM_BUDGET:
            break
    dims = {"nn": NN, "nt": NT, "tn": TN}[mode]
    if mode == "tn":
        a_spec = pl.BlockSpec((tk, tm), lambda i, j, kk: (kk, i))
    else:
        a_spec = pl.BlockSpec((tm, tk), lambda i, j, kk: (i, kk))
    if mode == "nt":
        b_spec = pl.BlockSpec((tn, tk), lambda i, j, kk: (j, kk))
    else:
        b_spec = pl.BlockSpec((tk, tn), lambda i, j, kk: (kk, j))
    mn_spec = pl.BlockSpec((tm, tn), lambda i, j, kk: (i, j))
    vec_spec = pl.BlockSpec((1, tn), lambda i, j, kk: (0, j))
    has_res = res is not None
    n_in = 2 + has_res + (1 if norm_g is not None else 0) + (3 if norm_bwd is not None else 0) + (after is not None)

    def body(*refs):
        a_ref, b_ref = refs[:2]
        extra = list(refs[2:n_in])
        outs = refs[n_in:-1]
        acc = refs[-1]
        kk = pl.program_id(2)

        @pl.when(kk == 0)
        def _():
            acc[...] = jnp.zeros_like(acc)

        if norm_bwd is not None:
            @pl.when((kk == 0) & (pl.program_id(0) == 0))
            def _():
                outs[2][...] = jnp.zeros_like(outs[2])

        acc[...] += _dot(a_ref[...], b_ref[...], dims)

        @pl.when(kk == nk - 1)
        def _():
            r = acc[...]
            if has_res:
                r = r + extra.pop(0)[...].astype(F32)
            if norm_g is not None:
                outs[0][...] = r.astype(out_dtype)
                rs = lax.rsqrt(jnp.mean(r * r, axis=-1, keepdims=True) + EPS)
                outs[1][...] = (r * rs * extra.pop(0)[...]).astype(BF16)
            elif norm_bwd is not None:
                xv, gv, dres = extra.pop(0)[...], extra.pop(0)[...], extra.pop(0)[...]
                rs = lax.rsqrt(jnp.mean(xv * xv, axis=-1, keepdims=True) + EPS)
                gdy = r * gv
                dx = dres + rs * gdy - xv * (rs * rs * rs) * jnp.mean(xv * gdy, axis=-1, keepdims=True)
                outs[0][...] = dx
                outs[1][...] = dx.astype(BF16)
                outs[2][...] += jnp.sum(r * xv * rs, axis=0, keepdims=True)
            else:
                outs[0][...] = r.astype(out_dtype)

    ins = [a, b] + ([res] if has_res else [])
    in_specs = [a_spec, b_spec] + ([mn_spec] if has_res else [])
    out_specs, out_shape = [mn_spec], [jax.ShapeDtypeStruct((m, n), out_dtype)]
    sem = ("parallel", "parallel", "arbitrary")
    if norm_g is not None:
        ins.append(norm_g.reshape(1, n))
        in_specs.append(vec_spec)
        out_specs.append(mn_spec)
        out_shape.append(jax.ShapeDtypeStruct((m, n), BF16))
    if norm_bwd is not None:
        x, g, dres = norm_bwd
        ins += [x, g.reshape(1, n), dres]
        in_specs += [mn_spec, vec_spec, mn_spec]
        out_specs += [mn_spec, vec_spec]
        out_shape += [jax.ShapeDtypeStruct((m, n), BF16), jax.ShapeDtypeStruct((1, n), F32)]
        sem = ("arbitrary", "arbitrary", "arbitrary")
    if after is not None:
        ins.append(after)
        in_specs.append(pl.BlockSpec(memory_space=pl.ANY))
    out = pl.pallas_call(
        body, name=name, grid=(m // tm, n // tn, nk), in_specs=in_specs, out_specs=out_specs,
        out_shape=out_shape, scratch_shapes=[pltpu.VMEM((tm, tn), F32)],
        compiler_params=_params(sem))(*ins)
    return out[0] if len(out) == 1 else out


def _rms_fwd(h, g, name):
    s, d = h.shape
    tm = _pick(s, (512, 256, 128))

    def body(h_ref, g_ref, o_ref):
        x = h_ref[...]
        r = lax.rsqrt(jnp.mean(x * x, axis=-1, keepdims=True) + EPS)
        o_ref[...] = (x * r * g_ref[...]).astype(BF16)

    return pl.pallas_call(
        body, name=name, grid=(s // tm,),
        in_specs=[pl.BlockSpec((tm, d), lambda i: (i, 0)), pl.BlockSpec((1, d), lambda i: (0, 0))],
        out_specs=pl.BlockSpec((tm, d), lambda i: (i, 0)),
        out_shape=jax.ShapeDtypeStruct((s, d), BF16), compiler_params=_params(("parallel",)))(h, g.reshape(1, d))


def _headnorm_fwd(x, g, name, scale=1.0, gate=None, gate_col0=0, out_dtype=BF16, width=None, head_major=False):
    if head_major:
        s, d = x.shape[1], x.shape[0] * HEAD
    else:
        s, d = x.shape[0], (width or x.shape[1])
    nh = d // HEAD
    tm = _pick(s, (256, 128))
    has_gate = gate is not None
    gb = gate_col0 // d

    def body(*refs):
        if has_gate:
            x_ref, g_ref, gt_ref, o_ref = refs
        else:
            x_ref, g_ref, o_ref = refs
        gv = g_ref[...]
        for h in range(nh):
            sl = slice(h * HEAD, (h + 1) * HEAD)
            xv = (x_ref[h] if head_major else x_ref[:, sl]).astype(F32)
            r = lax.rsqrt(jnp.mean(xv * xv, axis=-1, keepdims=True) + EPS)
            y = xv * r * gv
            if scale != 1.0:
                y = y * scale
            if has_gate:
                y = y * _silu(gt_ref[:, sl])
            o_ref[:, sl] = y.astype(out_dtype)

    row = pl.BlockSpec((tm, d), lambda i: (i, 0))
    hm = pl.BlockSpec((nh, tm, HEAD), lambda i: (0, i, 0))
    ins = [x, g.reshape(1, HEAD)]
    in_specs = [hm if head_major else row, pl.BlockSpec((1, HEAD), lambda i: (0, 0))]
    if has_gate:
        ins.append(gate)
        in_specs.append(pl.BlockSpec((tm, d), lambda i: (i, gb)))
    return pl.pallas_call(
        body, name=name, grid=(s // tm,), in_specs=in_specs, out_specs=row,
        out_shape=jax.ShapeDtypeStruct((s, d), out_dtype), compiler_params=_params(("parallel",)))(*ins)


def _headnorm_bwd(dy, x, g, name, scale=1.0, gate=None, gate_col0=0, dx_dtype=F32, head_major=False):
    s, d = dy.shape
    nh = d // HEAD
    tm = _pick(s, (256, 128))
    has_gate = gate is not None
    gb = gate_col0 // d

    def body(*refs):
        if has_gate:
            dy_ref, x_ref, g_ref, gt_ref, dx_ref, dg_ref, dgt_ref = refs
        else:
            dy_ref, x_ref, g_ref, dx_ref, dg_ref = refs

        @pl.when(pl.program_id(0) == 0)
        def _():
            dg_ref[...] = jnp.zeros_like(dg_ref)

        gv = g_ref[...]
        dg_acc = jnp.zeros((1, HEAD), F32)
        for h in range(nh):
            sl = slice(h * HEAD, (h + 1) * HEAD)
            xv = (x_ref[h] if head_major else x_ref[:, sl]).astype(F32)
            dyv = dy_ref[:, sl].astype(F32)
            r = lax.rsqrt(jnp.mean(xv * xv, axis=-1, keepdims=True) + EPS)
            if has_gate:
                gt = gt_ref[:, sl]
                dgt_ref[:, sl] = (dyv * (xv * r * gv) * _dsilu(gt)).astype(dgt_ref.dtype)
                dn = dyv * _silu(gt)
            else:
                dn = dyv
            if scale != 1.0:
                dn = dn * scale
            gdn = dn * gv
            mean_t = jnp.mean(xv * gdn, axis=-1, keepdims=True)
            dxv = (r * gdn - xv * (r * r * r) * mean_t).astype(dx_dtype)
            if head_major:
                dx_ref[h] = dxv
            else:
                dx_ref[:, sl] = dxv
            dg_acc = dg_acc + jnp.sum(dn * xv * r, axis=0, keepdims=True)
        dg_ref[...] += dg_acc

    row = pl.BlockSpec((tm, d), lambda i: (i, 0))
    hm = pl.BlockSpec((nh, tm, HEAD), lambda i: (0, i, 0))
    vec = pl.BlockSpec((1, HEAD), lambda i: (0, 0))
    ins = [dy, x, g.reshape(1, HEAD)]
    in_specs = [row, hm if head_major else row, vec]
    out_specs = [hm if head_major else row, vec]
    dx_shape = (nh, s, HEAD) if head_major else (s, d)
    out_shape = [jax.ShapeDtypeStruct(dx_shape, dx_dtype), jax.ShapeDtypeStruct((1, HEAD), F32)]
    if has_gate:
        ins.append(gate)
        in_specs.append(pl.BlockSpec((tm, d), lambda i: (i, gb)))
        out_specs.append(row)
        out_shape.append(jax.ShapeDtypeStruct((s, d), BF16))
    return pl.pallas_call(
        body, name=name, grid=(s // tm,), in_specs=in_specs, out_specs=out_specs, out_shape=out_shape,
        compiler_params=_params(("arbitrary",)))(*ins)


def _swiglu_fwd(hn, wf_t, name):
    s, d = hn.shape
    f = wf_t.shape[0] // 2
    tm = _pick(s, (1024, 512, 256, 128))
    tn = _pick(f, (512, 256, 128))
    nj = f // tn

    def body(a_ref, wg_ref, wu_ref, act_ref, g_ref, u_ref):
        a = a_ref[...]
        g = _dot(a, wg_ref[...], NT)
        u = _dot(a, wu_ref[...], NT)
        act_ref[...] = (_silu(g) * u).astype(BF16)
        g_ref[...] = g.astype(BF16)
        u_ref[...] = u.astype(BF16)

    o_spec = pl.BlockSpec((tm, tn), lambda i, j: (i, j))
    sds = jax.ShapeDtypeStruct((s, f), BF16)
    return pl.pallas_call(
        body, name=name, grid=(s // tm, nj),
        in_specs=[pl.BlockSpec((tm, d), lambda i, j: (i, 0)), pl.BlockSpec((tn, d), lambda i, j: (j, 0)),
                  pl.BlockSpec((tn, d), lambda i, j: (j + nj, 0))],
        out_specs=[o_spec, o_spec, o_spec], out_shape=[sds, sds, sds],
        compiler_params=_params(("parallel", "parallel")))(hn, wf_t, wf_t)


def _swiglu_bwd(dh, w_out, g, u, name):
    s, d = dh.shape
    f = w_out.shape[0]
    tm = _pick(s, (1024, 512, 256, 128))
    tn = _pick(f, (512, 256, 128))

    def body(dh_ref, w_ref, g_ref, u_ref, dg_ref, du_ref):
        dact = _dot(dh_ref[...], w_ref[...], NT)
        gv = g_ref[...].astype(F32)
        uv = u_ref[...].astype(F32)
        dg_ref[...] = (dact * uv * _dsilu(gv)).astype(BF16)
        du_ref[...] = (dact * _silu(gv)).astype(BF16)

    o_spec = pl.BlockSpec((tm, tn), lambda i, j: (i, j))
    sds = jax.ShapeDtypeStruct((s, f), BF16)
    return pl.pallas_call(
        body, name=name, grid=(s // tm, f // tn),
        in_specs=[pl.BlockSpec((tm, d), lambda i, j: (i, 0)), pl.BlockSpec((tn, d), lambda i, j: (j, 0)), o_spec, o_spec],
        out_specs=[o_spec, o_spec], out_shape=[sds, sds],
        compiler_params=_params(("parallel", "parallel")))(dh, w_out, g, u)


def _ple_fwd(h, hn, p, w_gate, wp_t, name):
    s, d = h.shape
    pd = p.shape[1]
    tm = _pick(s, (512, 256, 128))
    tn = _pick(d, (512, 256, 128))

    def body(h_ref, hn_ref, p_ref, wg_ref, wp_ref, o_ref, gp_ref, pp_ref):
        gpre = _dot(hn_ref[...], wg_ref[...], NN)
        pp = _dot(p_ref[...], wp_ref[...], NT)
        o_ref[...] = h_ref[...] + pp * jax.nn.sigmoid(gpre)
        gp_ref[...] = gpre.astype(BF16)
        pp_ref[...] = pp.astype(BF16)

    mn = pl.BlockSpec((tm, tn), lambda i, j: (i, j))
    return pl.pallas_call(
        body, name=name, grid=(s // tm, d // tn),
        in_specs=[mn, pl.BlockSpec((tm, d), lambda i, j: (i, 0)), pl.BlockSpec((tm, pd), lambda i, j: (i, 0)),
                  pl.BlockSpec((d, tn), lambda i, j: (0, j)), pl.BlockSpec((tn, pd), lambda i, j: (j, 0))],
        out_specs=[mn, mn, mn],
        out_shape=[jax.ShapeDtypeStruct((s, d), F32), jax.ShapeDtypeStruct((s, d), BF16), jax.ShapeDtypeStruct((s, d), BF16)],
        compiler_params=_params(("parallel", "parallel")))(h, hn, p, w_gate, wp_t)


def _ple_bwd(dh, gpre, pp, name):
    s, d = dh.shape
    tm = _pick(s, (512, 256, 128))

    def body(dh_ref, gp_ref, pp_ref, dgp_ref, dpp_ref):
        dv = dh_ref[...]
        sig = jax.nn.sigmoid(gp_ref[...].astype(F32))
        ppv = pp_ref[...].astype(F32)
        dpp_ref[...] = (dv * sig).astype(BF16)
        dgp_ref[...] = (dv * ppv * sig * (1.0 - sig)).astype(BF16)

    row = pl.BlockSpec((tm, d), lambda i: (i, 0))
    sds = jax.ShapeDtypeStruct((s, d), BF16)
    return pl.pallas_call(
        body, name=name, grid=(s // tm,), in_specs=[row, row, row], out_specs=[row, row], out_shape=[sds, sds],
        compiler_params=_params(("parallel",)))(dh, gpre, pp)


def _loss_fwd_bwd(y, t, name):
    s, d = y.shape
    tm = _pick(s, (512, 256, 128))

    def body(y_ref, t_ref, dy_ref, l_ref):
        @pl.when(pl.program_id(0) == 0)
        def _():
            l_ref[...] = jnp.zeros_like(l_ref)

        e = y_ref[...] - t_ref[...]
        dy_ref[...] = e * (1.0 / d)
        l_ref[...] += jnp.sum(e * e, axis=0, keepdims=True) * (0.5 / d)

    row = pl.BlockSpec((tm, d), lambda i: (i, 0))
    vec = pl.BlockSpec((1, d), lambda i: (0, 0))
    return pl.pallas_call(
        body, name=name, grid=(s // tm,), in_specs=[row, row], out_specs=[row, vec],
        out_shape=[jax.ShapeDtypeStruct((s, d), F32), jax.ShapeDtypeStruct((1, d), F32)],
        compiler_params=_params(("arbitrary",)))(y, t)


PADR = 8


def _conv_fwd(proj, w_conv, d, name):
    s = proj.shape[0]
    nh = d // HEAD
    kw = w_conv.shape[0]
    qscale = HEAD ** -0.5

    def body(x_ref, w_ref, o_ref, xp):
        kind = pl.program_id(0) // nh
        xp[0:PADR, :] = jnp.zeros((PADR, HEAD), F32)
        xp[PADR:, :] = x_ref[...]
        acc = jnp.zeros((s, HEAD), F32)
        for j in range(kw):
            acc = acc + w_ref[j:j + 1, :] * xp[PADR - (kw - 1) + j:PADR - (kw - 1) + j + s, :]
        a = _silu(acc)
        r = lax.rsqrt(jnp.sum(a * a, axis=-1, keepdims=True) + EPS)
        fac = jnp.where(kind == 0, r * qscale, jnp.where(kind == 1, r, jnp.ones_like(r)))
        o_ref[...] = a * fac

    blk = pl.BlockSpec((s, HEAD), lambda c: (0, c))
    hm = pl.BlockSpec((None, s, HEAD), lambda c: (c, 0, 0))
    return pl.pallas_call(
        body, name=name, grid=(3 * nh,), in_specs=[blk, pl.BlockSpec((kw, HEAD), lambda c: (0, c))], out_specs=hm,
        out_shape=jax.ShapeDtypeStruct((3 * nh, s, HEAD), F32), scratch_shapes=[pltpu.VMEM((s + PADR, HEAD), F32)],
        compiler_params=_params(("parallel",)))(proj, w_conv)


def _conv_bwd(dqkv, proj, w_conv, d, name):
    s = proj.shape[0]
    nh = d // HEAD
    kw = w_conv.shape[0]
    qscale = HEAD ** -0.5

    def body(dy_ref, x_ref, w_ref, dx_ref, dw_ref, xp, dp):
        kind = pl.program_id(0) // nh
        xp[0:PADR, :] = jnp.zeros((PADR, HEAD), F32)
        xp[PADR:, :] = x_ref[...]
        acc = jnp.zeros((s, HEAD), F32)
        for j in range(kw):
            acc = acc + w_ref[j:j + 1, :] * xp[PADR - (kw - 1) + j:PADR - (kw - 1) + j + s, :]
        a = _silu(acc)
        dy = dy_ref[...]
        r = lax.rsqrt(jnp.sum(a * a, axis=-1, keepdims=True) + EPS)
        sc = jnp.where(kind == 0, qscale, 1.0)
        dyn = dy * sc
        da_norm = r * dyn - a * (r * r * r) * jnp.sum(a * dyn, axis=-1, keepdims=True)
        da = jnp.where(kind == 2, dy, da_norm)
        dacc = da * _dsilu(acc)
        dp[0:s, :] = dacc
        dp[s:, :] = jnp.zeros((PADR, HEAD), F32)
        dx = jnp.zeros((s, HEAD), F32)
        for j in range(kw):
            sh = kw - 1 - j
            dx = dx + w_ref[j:j + 1, :] * dp[sh:sh + s, :]
            dw_ref[j:j + 1, :] = jnp.sum(dacc * xp[PADR - sh:PADR - sh + s, :], axis=0, keepdims=True)
        dx_ref[...] = dx.astype(BF16)

    blk = pl.BlockSpec((s, HEAD), lambda c: (0, c))
    hm = pl.BlockSpec((None, s, HEAD), lambda c: (c, 0, 0))
    wblk = pl.BlockSpec((kw, HEAD), lambda c: (0, c))
    return pl.pallas_call(
        body, name=name, grid=(3 * nh,), in_specs=[hm, blk, wblk], out_specs=[blk, wblk],
        out_shape=[jax.ShapeDtypeStruct((s, 3 * d), BF16), jax.ShapeDtypeStruct((kw, 3 * d), F32)],
        scratch_shapes=[pltpu.VMEM((s + PADR, HEAD), F32), pltpu.VMEM((s + PADR, HEAD), F32)],
        compiler_params=_params(("parallel",)))(dqkv, proj, w_conv)


def _softplus(x):
    return jnp.maximum(x, 0.0) + jnp.log(1.0 + jnp.exp(-jnp.abs(x)))


def _gates_fwd(pab, a_log, dt_bias, nh, name):
    s = pab.shape[0]
    tm = _pick(s, (512, 256, 128))

    def body(x_ref, al_ref, dt_ref, o_ref):
        x = x_ref[...]
        lane = lax.broadcasted_iota(jnp.int32, x.shape, 1)
        g = -jnp.exp(al_ref[...]) * _softplus(x + dt_ref[...])
        o_ref[...] = jnp.where(lane < nh, g, jnp.where(lane < 2 * nh, jax.nn.sigmoid(x), 0.0))

    row = pl.BlockSpec((tm, HEAD), lambda i: (i, 0))
    vec = pl.BlockSpec((1, HEAD), lambda i: (0, 0))
    return pl.pallas_call(
        body, name=name, grid=(s // tm,), in_specs=[row, vec, vec], out_specs=row,
        out_shape=jax.ShapeDtypeStruct((s, HEAD), F32), compiler_params=_params(("parallel",)))(pab, a_log, dt_bias)


def _gates_bwd(dgb, pab, a_log, dt_bias, nh, name):
    s = pab.shape[0]
    tm = _pick(s, (512, 256, 128))

    def body(d_ref, x_ref, al_ref, dt_ref, dx_ref, dal_ref, ddt_ref):
        @pl.when(pl.program_id(0) == 0)
        def _():
            dal_ref[...] = jnp.zeros_like(dal_ref)
            ddt_ref[...] = jnp.zeros_like(ddt_ref)

        x = x_ref[...]
        dv = d_ref[...]
        lane = lax.broadcasted_iota(jnp.int32, x.shape, 1)
        ea = jnp.exp(al_ref[...])
        xs = x + dt_ref[...]
        g = -ea * _softplus(xs)
        dxs = jnp.where(lane < nh, dv * (-ea) * jax.nn.sigmoid(xs), 0.0)
        sg = jax.nn.sigmoid(x)
        dxb = jnp.where((lane >= nh) & (lane < 2 * nh), dv * sg * (1.0 - sg), 0.0)
        dx_ref[...] = (dxs + dxb).astype(BF16)
        dal_ref[...] += jnp.sum(jnp.where(lane < nh, dv * g, 0.0), axis=0, keepdims=True)
        ddt_ref[...] += jnp.sum(dxs, axis=0, keepdims=True)

    row = pl.BlockSpec((tm, HEAD), lambda i: (i, 0))
    vec = pl.BlockSpec((1, HEAD), lambda i: (0, 0))
    return pl.pallas_call(
        body, name=name, grid=(s // tm,), in_specs=[row, row, vec, vec], out_specs=[row, vec, vec],
        out_shape=[jax.ShapeDtypeStruct((s, HEAD), BF16), jax.ShapeDtypeStruct((1, HEAD), F32),
                   jax.ShapeDtypeStruct((1, HEAD), F32)],
        compiler_params=_params(("arbitrary",)))(dgb, pab, a_log, dt_bias)


def _tri_inv(a_low, eye_f):
    n = -a_low
    p = eye_f + n
    steps = int(math.log2(a_low.shape[-1])) - 1
    for _ in range(steps):
        n = _dot(n, n, BNN)
        p = p + _dot(p, n, BNN)
    return p


def _lane_col(x, lane, idx):
    return jnp.sum(jnp.where(lane == idx, x, 0.0), axis=1, keepdims=True)


def _head_cols(gbv, lo, nh):
    lane = lax.broadcasted_iota(jnp.int32, gbv.shape, 1)
    return jnp.stack([_lane_col(gbv, lane, lo + h) for h in range(nh)], axis=0)


def _gdn_chunk(q, k, v, g_col, beta_col, st):
    c = q.shape[1]
    r_i = lax.broadcasted_iota(jnp.int32, (c, c), 0)
    c_i = lax.broadcasted_iota(jnp.int32, (c, c), 1)
    incl = c_i <= r_i
    strict = c_i < r_i
    eye = c_i == r_i
    g_row = jnp.sum(jnp.where(eye, g_col, 0.0), axis=1, keepdims=True)
    gc_col = jnp.sum(jnp.where(incl, g_row, 0.0), axis=2, keepdims=True)
    gc_row = jnp.sum(jnp.where(eye, gc_col, 0.0), axis=1, keepdims=True)
    g_last = jnp.sum(g_col, axis=1, keepdims=True)
    decay = jnp.exp(jnp.where(incl, gc_col - gc_row, NEG))
    kk = _dot(k, k, BNT)
    a_low = jnp.where(strict, beta_col * kk * decay, 0.0)
    t_inv = _tri_inv(a_low, eye.astype(F32))
    e_g = jnp.exp(gc_col)
    bk = beta_col * e_g
    rhs = jnp.concatenate([v * beta_col, k * bk], axis=2)
    sol = _dot(t_inv, rhs, BNN)
    u, w = sol[:, :, :HEAD], sol[:, :, HEAD:]
    qk_raw = _dot(q, k, BNT)
    qk = qk_raw * decay
    q_dec = q * e_g
    e2 = jnp.exp(g_last - gc_col)
    k_dec = k * e2
    gl = jnp.exp(g_last)
    ws = _dot(jnp.concatenate([w, q_dec], axis=1), st, BNN)
    v_new = u - ws[:, :c]
    o = ws[:, c:] + _dot(qk, v_new, BNN)
    st_new = st * gl + _dot(k_dec, v_new, BTN)
    inter = dict(incl=incl, strict=strict, eye=eye, decay=decay, kk=kk, t_inv=t_inv, e_g=e_g, bk=bk, sol=sol, w=w,
                 qk_raw=qk_raw, qk=qk, q_dec=q_dec, e2=e2, k_dec=k_dec, gl=gl, v_new=v_new, c_i=c_i, r_i=r_i)
    return o, st_new, inter


def _gdn_fwd(qkv, gb, nh, name):
    s = qkv.shape[1]
    nc = s // CHUNK

    def body(q_ref, k_ref, v_ref, gb_ref, o_ref, st_ref, state):
        @pl.when(pl.program_id(0) == 0)
        def _():
            state[...] = jnp.zeros_like(state)

        gbv = gb_ref[...]
        st = state[...]
        st_ref[...] = st
        o, st_new, _ = _gdn_chunk(q_ref[...], k_ref[...], v_ref[...], _head_cols(gbv, 0, nh), _head_cols(gbv, nh, nh), st)
        o_ref[...] = o
        state[...] = st_new

    def qspec(part):
        return pl.BlockSpec((nh, CHUNK, HEAD), lambda n: (part, n, 0))

    return pl.pallas_call(
        body, name=name, grid=(nc,),
        in_specs=[qspec(0), qspec(1), qspec(2), pl.BlockSpec((CHUNK, HEAD), lambda n: (n, 0))],
        out_specs=[qspec(0), pl.BlockSpec((None, nh, HEAD, HEAD), lambda n: (n, 0, 0, 0))],
        out_shape=[jax.ShapeDtypeStruct((nh, s, HEAD), F32), jax.ShapeDtypeStruct((nc, nh, HEAD, HEAD), F32)],
        scratch_shapes=[pltpu.VMEM((nh, HEAD, HEAD), F32)],
        compiler_params=_params(("arbitrary",)))(qkv, qkv, qkv, gb)


def _gdn_bwd(qkv, gb, do, states, nh, name):
    s = qkv.shape[1]
    nc = s // CHUNK
    c = CHUNK

    def body(q_ref, k_ref, v_ref, gb_ref, do_ref, st_ref, dqkv_ref, dgb_ref, dstate):
        @pl.when(pl.program_id(0) == 0)
        def _():
            dstate[...] = jnp.zeros_like(dstate)

        gbv = gb_ref[...]
        lane = lax.broadcasted_iota(jnp.int32, gbv.shape, 1)
        q, k, v = q_ref[...], k_ref[...], v_ref[...]
        beta_col = _head_cols(gbv, nh, nh)
        st = st_ref[...]
        dst = dstate[...]
        dov = do_ref[...]
        _, _, it = _gdn_chunk(q, k, v, _head_cols(gbv, 0, nh), beta_col, st)
        incl, strict, eye, decay = it["incl"], it["strict"], it["eye"], it["decay"]
        dv_new = _dot(it["qk"], dov, BTN) + _dot(it["k_dec"], dst, BNN)
        d_qk = _dot(dov, it["v_new"], BNT)
        dd = _dot(jnp.concatenate([dov, -dv_new], axis=1), st, BNT)
        dq_dec, dw = dd[:, :c], dd[:, c:]
        dst_new = _dot(it["q_dec"], dov, BTN) + it["gl"] * dst - _dot(it["w"], dv_new, BTN)
        dgl = jnp.sum(jnp.sum(dst * st, axis=2, keepdims=True), axis=1, keepdims=True)
        dk_dec = _dot(it["v_new"], dst, BNT)
        dsol = jnp.concatenate([dv_new, dw], axis=2)
        drhs = _dot(it["t_inv"], dsol, BTN)
        d_a = jnp.where(strict, -_dot(drhs, it["sol"], BNT), 0.0)
        drhs_u, drhs_w = drhs[:, :, :HEAD], drhs[:, :, HEAD:]
        dvh = beta_col * drhs_u
        rw_k = jnp.sum(drhs_w * k, axis=2, keepdims=True)
        dbeta = jnp.sum(drhs_u * v, axis=2, keepdims=True) + it["e_g"] * rw_k
        dkh = it["bk"] * drhs_w
        dgc_col = it["bk"] * rw_k
        dkk = d_a * beta_col * decay
        dbeta = dbeta + jnp.sum(d_a * it["kk"] * decay, axis=2, keepdims=True)
        ddecay = d_a * beta_col * it["kk"]
        dkh = dkh + _dot(dkk, k, BNN) + _dot(dkk, k, BTN)
        dqk_raw = d_qk * decay
        ddecay = ddecay + d_qk * it["qk_raw"]
        dqh = _dot(dqk_raw, k, BNN)
        dkh = dkh + _dot(dqk_raw, q, BTN)
        ddm = jnp.where(incl, ddecay * decay, 0.0)
        dgc_col = dgc_col + jnp.sum(ddm, axis=2, keepdims=True)
        dgc_row = -jnp.sum(ddm, axis=1, keepdims=True)
        dqh = dqh + dq_dec * it["e_g"]
        dgc_col = dgc_col + jnp.sum(dq_dec * it["q_dec"], axis=2, keepdims=True)
        dkh = dkh + dk_dec * it["e2"]
        tmp = jnp.sum(dk_dec * it["k_dec"], axis=2, keepdims=True)
        dgc_col = dgc_col - tmp
        dg_last = jnp.sum(tmp, axis=1, keepdims=True) + dgl * it["gl"]
        dgc_tot_row = dgc_row + jnp.sum(jnp.where(eye, dgc_col, 0.0), axis=1, keepdims=True)
        dg_col = jnp.sum(jnp.where(it["c_i"] >= it["r_i"], dgc_tot_row, 0.0), axis=2, keepdims=True) + dg_last
        dqkv_ref[0] = dqh
        dqkv_ref[1] = dkh
        dqkv_ref[2] = dvh
        dstate[...] = dst_new
        dgb_acc = jnp.zeros(gbv.shape, F32)
        for h in range(nh):
            dgb_acc = jnp.where(lane == h, dg_col[h], jnp.where(lane == nh + h, dbeta[h], dgb_acc))
        dgb_ref[...] = dgb_acc

    def rev(part):
        return pl.BlockSpec((nh, CHUNK, HEAD), lambda n: (part, nc - 1 - n, 0))

    gspec = pl.BlockSpec((CHUNK, HEAD), lambda n: (nc - 1 - n, 0))
    dqkv, dgb = pl.pallas_call(
        body, name=name, grid=(nc,),
        in_specs=[rev(0), rev(1), rev(2), gspec, rev(0),
                  pl.BlockSpec((None, nh, HEAD, HEAD), lambda n: (nc - 1 - n, 0, 0, 0))],
        out_specs=[pl.BlockSpec((3, nh, CHUNK, HEAD), lambda n: (0, 0, nc - 1 - n, 0)), gspec],
        out_shape=[jax.ShapeDtypeStruct((3, nh, s, HEAD), F32), jax.ShapeDtypeStruct((s, HEAD), F32)],
        scratch_shapes=[pltpu.VMEM((nh, HEAD, HEAD), F32)],
        compiler_params=_params(("arbitrary",)))(qkv, qkv, qkv, gb, do, states)
    return dqkv.reshape(3 * nh, s, HEAD), dgb


SB_TQ = 512


def _tri01(rel):
    j_i = lax.broadcasted_iota(jnp.int32, (SBLK, SBLK), 0)
    s_i = lax.broadcasted_iota(jnp.int32, (SBLK, SBLK), 1)
    return rel(j_i, s_i).astype(BF16)


SB_HP = 2


def _each(fn, *lists):
    return [fn(*xs) for xs in zip(*lists)]


def _sb_scores(qts, kblks, mask, csums, rhs01):
    zs = _each(lambda qt, kb: _dot(qt, kb, NT), qts, kblks)
    es = _each(lambda z: jnp.exp(-jnp.abs(z)), zs)
    sps = _each(lambda z, e: jnp.maximum(z, 0.0) + jnp.log(1.0 + e), zs, es)
    lns = _each(lambda sp: -sp if mask is None else jnp.where(mask, -sp, 0.0), sps)
    sts = _each(lambda ln: _dot_hilo(ln, rhs01), lns)
    wgts = _each(lambda z, sp, st, cs: jnp.exp((z - sp) + st + cs), zs, sps, sts, csums)
    if mask is not None:
        wgts = _each(lambda w: jnp.where(mask, w, 0.0), wgts)
    return zs, es, wgts, lns


def _band_mask(rows, j, row0):
    r_i = lax.broadcasted_iota(jnp.int32, (rows, SBLK), 0)
    c_i = lax.broadcasted_iota(jnp.int32, (rows, SBLK), 1)
    return (j * SBLK + c_i) < (row0 + r_i)


def _sb_fwd(q, k, v, name):
    s, d = q.shape
    nh = d // HEAD
    tq = min(SB_TQ, s)
    nb = tq // SBLK

    hp = SB_HP
    heads = [slice(h * HEAD, (h + 1) * HEAD) for h in range(hp)]

    def body(q_ref, k_ref, v_ref, o_ref, c_ref, acc, cs):
        qb = pl.program_id(1)
        lane = lax.broadcasted_iota(jnp.int32, (tq, HEAD), 1)
        after = _tri01(lambda j, t: j > t)
        acc[...] = jnp.zeros_like(acc)
        cs[...] = jnp.zeros_like(cs)
        c_ref[...] = jnp.zeros_like(c_ref)

        def process(rs, kb, mask):
            keys = pl.ds(pl.multiple_of(kb * SBLK, SBLK), SBLK)
            csums = [cs[h, rs, :] for h in range(hp)]
            _, _, wgts, lns = _sb_scores([q_ref[rs, hs] for hs in heads], [k_ref[keys, hs] for hs in heads], mask, csums, after)
            pvs = _each(lambda w, hs: _dot(w, v_ref[keys, hs]), wgts, heads)
            tots = _each(lambda ln: jnp.sum(ln, axis=1, keepdims=True), lns)
            for h, hs in enumerate(heads):
                acc[h, rs, :] += pvs[h]
                c_ref[rs, hs] = jnp.where(lane[rs, :] == kb, csums[h], c_ref[rs, hs])
                cs[h, rs, :] = csums[h] + tots[h]

        for j in reversed(range(nb)):
            process(slice(j * SBLK, tq), qb * nb + j, _band_mask(tq - j * SBLK, j, j * SBLK))

        def step(it, carry):
            process(slice(0, tq), qb * nb - 1 - it, None)
            return carry

        lax.fori_loop(0, qb * nb, step, 0)
        for h, hs in enumerate(heads):
            o_ref[:, hs] = acc[h].astype(BF16)

    qspec = pl.BlockSpec((tq, hp * HEAD), lambda h, i: (i, h))
    kspec = pl.BlockSpec((s, hp * HEAD), lambda h, i: (0, h))
    return pl.pallas_call(
        body, name=name, grid=(nh // hp, s // tq), in_specs=[qspec, kspec, kspec], out_specs=[qspec, qspec],
        out_shape=[jax.ShapeDtypeStruct((s, d), BF16), jax.ShapeDtypeStruct((s, d), F32)],
        scratch_shapes=[pltpu.VMEM((hp, tq, HEAD), F32), pltpu.VMEM((hp, tq, 1), F32)],
        compiler_params=_params(("parallel", "arbitrary")))(q, k, v)


def _sb_bwd(q, k, v, do, ctab, name):
    s, d = q.shape
    nh = d // HEAD
    tq = min(SB_TQ, s)
    nb = tq // SBLK

    hp = SB_HP
    heads = [slice(h * HEAD, (h + 1) * HEAD) for h in range(hp)]

    def body(q_ref, k_ref, v_ref, do_ref, c_ref, dq_ref, dk_ref, dv_ref, ps):
        qb = pl.program_id(1)

        @pl.when(qb == 0)
        def _():
            dk_ref[...] = jnp.zeros_like(dk_ref)
            dv_ref[...] = jnp.zeros_like(dv_ref)

        dq_ref[...] = jnp.zeros_like(dq_ref)
        ps[...] = jnp.zeros_like(ps)
        lane = lax.broadcasted_iota(jnp.int32, (tq, HEAD), 1)
        after = _tri01(lambda j, t: j > t)
        before = _tri01(lambda j, t: j < t)

        def process(rs, kb, mask):
            keys = pl.ds(pl.multiple_of(kb * SBLK, SBLK), SBLK)
            kblks = [k_ref[keys, hs] for hs in heads]
            qts = [q_ref[rs, hs] for hs in heads]
            dots = [do_ref[rs, hs] for hs in heads]
            csums = [_lane_col(c_ref[rs, hs], lane[rs, :], kb) for hs in heads]
            zs, es, wgts, _ = _sb_scores(qts, kblks, mask, csums, after)
            dlws = _each(lambda dt, hs, w: _dot(dt, v_ref[keys, hs], NT) * w, dots, heads, wgts)
            pts = _each(lambda dlw: _dot_hilo(dlw, before), dlws)
            pfxs = [ps[h, rs, :] for h in range(hp)]
            rs_ = _each(lambda e: 1.0 / (1.0 + e), es)
            sigs = _each(lambda z, e, r: jnp.where(z >= 0.0, r, e * r), zs, es, rs_)
            dzs = _each(lambda dlw, sig, pfx, pt: dlw * (1.0 - sig) - sig * (pfx + pt), dlws, sigs, pfxs, pts)
            tots = _each(lambda dlw: jnp.sum(dlw, axis=1, keepdims=True), dlws)
            if mask is not None:
                dzs = _each(lambda dz: jnp.where(mask, dz, 0.0), dzs)
            dqs = _each(lambda dz, kb_: _dot(dz, kb_), dzs, kblks)
            dks = _each(lambda dz, qt: _dot(dz, qt, TN), dzs, qts)
            dvs = _each(lambda w, dt: _dot(w, dt, TN), wgts, dots)
            for h, hs in enumerate(heads):
                dq_ref[rs, hs] += dqs[h]
                dk_ref[keys, hs] += dks[h]
                dv_ref[keys, hs] += dvs[h]
                ps[h, rs, :] = pfxs[h] + tots[h]

        def step(kb, carry):
            process(slice(0, tq), kb, None)
            return carry

        lax.fori_loop(0, qb * nb, step, 0)
        for j in range(nb):
            process(slice(j * SBLK, tq), qb * nb + j, _band_mask(tq - j * SBLK, j, j * SBLK))

    qspec = pl.BlockSpec((tq, hp * HEAD), lambda h, i: (i, h))
    kspec = pl.BlockSpec((s, hp * HEAD), lambda h, i: (0, h))
    sds = jax.ShapeDtypeStruct((s, d), F32)
    return pl.pallas_call(
        body, name=name, grid=(nh // hp, s // tq), in_specs=[qspec, kspec, kspec, qspec, qspec],
        out_specs=[qspec, kspec, kspec], out_shape=[sds, sds, sds],
        scratch_shapes=[pltpu.VMEM((hp, tq, 1), F32)],
        compiler_params=_params(("parallel", "arbitrary")))(q, k, v, do, ctab)


def _my_index():
    return 4 * lax.axis_index("x") + 2 * lax.axis_index("y") + lax.axis_index("c")


def _all_gather(x_shard, name):
    m_per, n = x_shard.shape

    def body(x_ref, out_ref, send_sems, recv_sems, local_sem):
        x, y, c = lax.axis_index("x"), lax.axis_index("y"), lax.axis_index("c")
        me, sibling = (x, y, c), (x, y, 1 - c)
        chips = [(1 - x, y), (x, 1 - y), (1 - x, 1 - y)]

        def rows(px, py, pc):
            return out_ref.at[pl.ds((4 * px + 2 * py + pc) * m_per, m_per), :]

        def copy(k, block, to, src=None):
            return pltpu.make_async_remote_copy(
                src_ref=rows(*block) if src is None else src, dst_ref=rows(*block),
                send_sem=send_sems.at[k], recv_sem=recv_sems.at[k], device_id=to, device_id_type=MESH)

        mine = pltpu.make_async_copy(x_ref, rows(*me), local_sem)
        mine.start()
        first = [copy(0, me, sibling, src=x_ref)]
        first += [copy(1 + j, me, (*chip, c), src=x_ref) for j, chip in enumerate(chips)]
        for cp in first:
            cp.start()
        passed = [copy(4 + j, (*chip, c), sibling) for j, chip in enumerate(chips)]
        for j, chip in enumerate(chips):
            copy(1 + j, (*chip, c), me).wait_recv()
            passed[j].start()
        copy(0, sibling, me).wait_recv()
        for j, chip in enumerate(chips):
            copy(4 + j, (*chip, 1 - c), me).wait_recv()
        for cp in first + passed:
            cp.wait_send()
        mine.wait()

    return pl.pallas_call(
        body, name=name, out_shape=jax.ShapeDtypeStruct((NDEV * m_per, n), x_shard.dtype),
        in_specs=[pl.BlockSpec(memory_space=pl.ANY)], out_specs=pl.BlockSpec(memory_space=pl.ANY),
        scratch_shapes=[pltpu.SemaphoreType.DMA((7,)), pltpu.SemaphoreType.DMA((7,)), pltpu.SemaphoreType.DMA],
    )(x_shard)


HBM_SPEC = pl.BlockSpec(memory_space=pltpu.HBM)
SEM_SPEC = pl.BlockSpec(memory_space=pltpu.SEMAPHORE)
ANY_SPEC = pl.BlockSpec(memory_space=pl.ANY)
EFFECT = pltpu.SideEffectType.DATAFLOW_SIDE_EFFECTING


def _exchange_copies(src_refs, land_refs, send_sems, recv_sems, self_sems, scatter):
    x, y, c = lax.axis_index("x"), lax.axis_index("y"), lax.axis_index("c")
    me = 4 * x + 2 * y + c
    remote, local = [], []
    for p, (src_ref, land_ref) in enumerate(zip(src_refs, land_refs)):
        rows = land_ref.shape[0] // NDEV

        def part(idx):
            return src_ref.at[pl.ds(idx * rows, rows), :] if scatter else src_ref

        slot = land_ref.at[pl.ds(me * rows, rows), :]
        for k in range(1, NDEV):
            px, py, pc = x ^ ((k >> 2) & 1), y ^ ((k >> 1) & 1), c ^ (k & 1)
            remote.append(pltpu.make_async_remote_copy(
                src_ref=part(4 * px + 2 * py + pc), dst_ref=slot, send_sem=send_sems.at[7 * p + k - 1],
                recv_sem=recv_sems.at[7 * p + k - 1], device_id=(px, py, pc), device_id_type=MESH))
        local.append(pltpu.make_async_copy(part(me), slot, self_sems.at[p]))
    return remote, local


def _send_start(srcs, scatter, after, name):
    n = len(srcs)
    lands = []
    for s in srcs:
        rows = s.shape[0] if scatter else NDEV * s.shape[0]
        lands.append(pltpu.with_memory_space_constraint(lax.empty((rows, s.shape[1]), s.dtype), pltpu.HBM))

    def body(*refs):
        src_refs, land_refs = refs[:n], refs[n:2 * n]
        send_sems, recv_sems, self_sems = refs[2 * n + 1:2 * n + 4]
        remote, local = _exchange_copies(src_refs, land_refs, send_sems, recv_sems, self_sems, scatter)
        for cp in remote + local:
            cp.start()
        refs[-1][...] = jnp.zeros_like(refs[-1])

    hbm = lambda a: pltpu.HBM(a.shape, a.dtype)
    out = pl.pallas_call(
        body, name=name,
        out_shape=(pltpu.SemaphoreType.DMA((7 * n,)), pltpu.SemaphoreType.DMA((7 * n,)), pltpu.SemaphoreType.DMA((n,)),
                   *[hbm(s) for s in srcs], *[hbm(a) for a in lands], jax.ShapeDtypeStruct((8, HEAD), F32)),
        in_specs=(HBM_SPEC,) * (2 * n) + (ANY_SPEC,),
        out_specs=(SEM_SPEC,) * 3 + (HBM_SPEC,) * (2 * n) + (pl.BlockSpec(memory_space=pltpu.VMEM),),
        input_output_aliases={i: 3 + i for i in range(2 * n)},
        compiler_params=pltpu.CompilerParams(has_side_effects=EFFECT),
    )(*[pltpu.with_memory_space_constraint(s, pltpu.HBM) for s in srcs], *lands, after)
    return dict(sems=out[:3], srcs=out[3:3 + n], lands=out[3 + n:3 + 2 * n], token=out[-1])


def _send_wait(started, scatter, after, name):
    srcs, lands = started["srcs"], started["lands"]
    n = len(srcs)

    def body(*refs):
        src_refs, land_refs = refs[:n], refs[n:2 * n]
        send_sems, recv_sems, self_sems = refs[2 * n:2 * n + 3]
        remote, local = _exchange_copies(src_refs, land_refs, send_sems, recv_sems, self_sems, scatter)
        for cp in remote:
            cp.wait_send()
            cp.wait_recv()
        for cp in local:
            cp.wait()

    hbm = lambda a: pltpu.HBM(a.shape, a.dtype)
    out = pl.pallas_call(
        body, name=name, out_shape=(*[hbm(s) for s in srcs], *[hbm(a) for a in lands]),
        in_specs=(HBM_SPEC,) * (2 * n) + (SEM_SPEC,) * 3 + (ANY_SPEC,), out_specs=(HBM_SPEC,) * (2 * n),
        input_output_aliases={i: i for i in range(2 * n)},
        compiler_params=pltpu.CompilerParams(has_side_effects=EFFECT),
    )(*srcs, *lands, *started["sems"], after)
    return out[n:]


def _sum_slots(x, name):
    _, r, c = x.shape
    tr = _pick(r, (512, 256, 128, 64, 32, 16, 8))

    def body(x_ref, o_ref):
        acc = x_ref[0].astype(F32)
        for i in range(1, NDEV):
            acc = acc + x_ref[i].astype(F32)
        o_ref[...] = acc

    return pl.pallas_call(
        body, name=name, grid=(r // tr,), in_specs=[pl.BlockSpec((NDEV, tr, c), lambda i: (0, i, 0))],
        out_specs=pl.BlockSpec((tr, c), lambda i: (i, 0)), out_shape=jax.ShapeDtypeStruct((r, c), F32),
        compiler_params=_params(("parallel",)))(x)


def _adamw(w, g, m, v, name):
    r, c = w.shape
    tr = _pick(r, (256, 128, 64, 32, 16, 8))
    c1 = 1.0 - B1 ** STEP
    c2 = 1.0 - B2 ** STEP

    def body(w_ref, g_ref, m_ref, v_ref, d_ref, nm_ref, nv_ref):
        gv = g_ref[...]
        nm = B1 * m_ref[...] + (1.0 - B1) * gv
        nv = B2 * v_ref[...] + (1.0 - B2) * (gv * gv)
        d_ref[...] = -LR * ((nm / c1) / (jnp.sqrt(nv / c2) + ADAM_EPS) + WD * w_ref[...])
        nm_ref[...] = nm
        nv_ref[...] = nv

    blk = pl.BlockSpec((tr, c), lambda i: (i, 0))
    sds = jax.ShapeDtypeStruct((r, c), F32)
    return pl.pallas_call(
        body, name=name, grid=(r // tr,), in_specs=[blk] * 4, out_specs=[blk] * 3, out_shape=[sds] * 3,
        compiler_params=_params(("parallel",)))(w, g, m, v)


def _pad_rows(a, mult):
    r = a.shape[0]
    pad = (-r) % mult
    return a if pad == 0 else jnp.pad(a, ((0, pad), (0, 0)))


def _pad_lanes(v, width=HEAD):
    return jnp.pad(v.reshape(1, -1), ((0, 0), (0, width - v.shape[-1])))


def kernel(x, p, ln_mix, ln_ffn, ln_ple, gdn_w_in, gdn_conv, gdn_a_log, gdn_dt_bias, gdn_norm, gdn_w_out, kv_norm, w_kv, k_norm, sb_w_q, sb_q_norm, sb_w_out, ffn_w_in, ffn_w_out, ple_w_proj, ple_w_gate, loss_target, m_ln_mix, m_ln_ffn, m_ln_ple, m_gdn_w_in, m_gdn_conv, m_gdn_a_log, m_gdn_dt_bias, m_gdn_norm, m_gdn_w_out, m_kv_norm, m_w_kv, m_k_norm, m_sb_w_q, m_sb_q_norm, m_sb_w_out, m_ffn_w_in, m_ffn_w_out, m_ple_w_proj, m_ple_w_gate, v_ln_mix, v_ln_ffn, v_ln_ple, v_gdn_w_in, v_gdn_conv, v_gdn_a_log, v_gdn_dt_bias, v_gdn_norm, v_gdn_w_out, v_kv_norm, v_w_kv, v_k_norm, v_sb_w_q, v_sb_q_norm, v_sb_w_out, v_ffn_w_in, v_ffn_w_out, v_ple_w_proj, v_ple_w_gate):
    s, d = x.shape[1], x.shape[2]
    nh = d // HEAD
    depth = ln_mix.shape[0]
    n_a = gdn_w_in.shape[0]
    n_b = sb_w_q.shape[0]
    me = _my_index()
    win_cols = gdn_w_in.shape[2]
    win_rows = 4 * d + 2 * nh

    def col_t(w):
        return jnp.transpose(w).astype(BF16)

    local = {}
    for l in range(n_a):
        local[("gdn_w_in", l)] = col_t(gdn_w_in[l])
        local[("gdn_w_out", l)] = gdn_w_out[l].astype(BF16)
    local[("w_kv", 0)] = col_t(w_kv)
    for j in range(n_b):
        local[("sb_w_q", j)] = sb_w_q[j].astype(BF16)
        local[("sb_w_out", j)] = sb_w_out[j].astype(BF16)
    for l in range(depth):
        local[("ffn_w_in", l)] = col_t(ffn_w_in[l])
        local[("ffn_w_out", l)] = ffn_w_out[l].astype(BF16)
        local[("ple_w_proj", l)] = col_t(ple_w_proj[l]).reshape(-1, d)
        local[("ple_w_gate", l)] = ple_w_gate[l].astype(BF16)
    local = {key: _pad_rows(a, 16) for key, a in local.items()}

    chunks = []
    for l in range(depth):
        mix = [("gdn_w_in", l), ("gdn_w_out", l)] if l < n_a else [("sb_w_q", l - n_a), ("sb_w_out", l - n_a)]
        rest = [("ffn_w_in", l), ("ffn_w_out", l), ("ple_w_proj", l), ("ple_w_gate", l)]
        if l == n_a - 1:
            rest.append(("w_kv", 0))
        chunks += [(f"a{l}", mix), (f"f{l}", rest)]
    chunk_keys = dict(chunks)

    conv_rows = n_a * gdn_conv.shape[1]
    conv_sh = _pad_rows(gdn_conv.reshape(conv_rows, -1), 8)
    conv_g = _all_gather(conv_sh, "comm_gather_conv")
    token = conv_g
    conv_g = conv_g.reshape(NDEV, conv_sh.shape[0], -1)
    conv_full = jnp.transpose(conv_g[:, :conv_rows, :], (1, 0, 2)).reshape(n_a, gdn_conv.shape[1], 3 * d)

    w_started = {}
    for name, keys in chunks:
        w_started[name] = _send_start([local[k] for k in keys], False, token, f"comm_wstart_{name}")
        token = w_started[name]["token"]

    full = {}

    def fetch(name, after):
        lands = _send_wait(w_started[name], False, after, f"comm_wwait_{name}")
        for key, land in zip(chunk_keys[name], lands):
            full[key] = land

    def whole(key, valid=None):
        a = full[key]
        if valid is not None:
            a = a.reshape(NDEV, -1, d)[:, :valid, :].reshape(-1, d)
        return a

    pd = p.shape[-1]
    w_in_t, w_ab_t, w_gout, w_q, w_sout, wf_t, w_fout, wp_t, w_pg = {}, {}, {}, {}, {}, {}, {}, {}, {}
    wkv_t = None

    h = x[0]
    sv = []
    kv_sv = None
    k_sh = v_sh = None
    for l in range(depth):
        t = {}
        t["h0"] = h
        hn = _rms_fwd(h, ln_mix[l], f"rms_mix_{l}")
        t["hn"] = hn
        fetch(f"a{l}", token if l == 0 else hn)
        if l < n_a:
            wt = whole(("gdn_w_in", l), win_cols)
            w_in_t[l] = wt[:4 * d]
            w_ab_t[l] = jnp.pad(wt[4 * d:], ((0, HEAD - 2 * nh), (0, 0)))
            w_gout[l] = whole(("gdn_w_out", l))
        else:
            w_q[l - n_a] = whole(("sb_w_q", l - n_a))
            w_sout[l - n_a] = whole(("sb_w_out", l - n_a))
        if l < n_a:
            proj = _mm(hn, w_in_t[l], "nt", f"gdn_proj_{l}")
            pab = _mm(hn, w_ab_t[l], "nt", f"gdn_proj_ab_{l}")
            qkv = _conv_fwd(proj, conv_full[l], d, f"gdn_conv_{l}")
            al, dtb = _pad_lanes(gdn_a_log[l]), _pad_lanes(gdn_dt_bias[l])
            gb = _gates_fwd(pab, al, dtb, nh, f"gdn_gates_{l}")
            o_raw, states = _gdn_fwd(qkv, gb, nh, f"gdn_rule_{l}")
            o2 = _headnorm_fwd(o_raw, gdn_norm[l], f"gdn_outnorm_{l}", gate=proj, gate_col0=3 * d, head_major=True)
            h, hn2 = _mm(o2, w_gout[l], "nn", f"gdn_out_{l}", res=h, norm_g=ln_ffn[l])
            t.update(proj=proj, pab=pab, qkv=qkv, gb=gb, o_raw=o_raw, states=states, o2=o2, al=al, dtb=dtb)
        else:
            j = l - n_a
            qpre = _mm(hn, w_q[j], "nn", f"sb_qproj_{j}")
            qn = _headnorm_fwd(qpre, sb_q_norm[j], f"sb_qnorm_{j}", scale=HEAD ** -0.5)
            o, ctab = _sb_fwd(qn, k_sh, v_sh, f"sb_attn_{j}")
            h, hn2 = _mm(o, w_sout[j], "nn", f"sb_out_{j}", res=h, norm_g=ln_ffn[l])
            t.update(qpre=qpre, qn=qn, o=o, ctab=ctab)
        t["h1"] = h
        fetch(f"f{l}", hn2)
        wf_t[l] = whole(("ffn_w_in", l))
        w_fout[l] = whole(("ffn_w_out", l))
        wp_t[l] = full[("ple_w_proj", l)].reshape(d, pd)
        w_pg[l] = whole(("ple_w_gate", l))
        if l == n_a - 1:
            wkv_t = whole(("w_kv", 0))
        act, gs, us = _swiglu_fwd(hn2, wf_t[l], f"ffn_in_{l}")
        h, hn3 = _mm(act, w_fout[l], "nn", f"ffn_out_{l}", res=h, norm_g=ln_ple[l])
        t.update(hn2=hn2, act=act, gs=gs, us=us, h2=h)
        h, gpre, pp = _ple_fwd(h, hn3, p[l, 0], w_pg[l], wp_t[l], f"ple_{l}")
        t.update(hn3=hn3, gpre=gpre, pp=pp)
        sv.append(t)
        if l == n_a - 1:
            kvn = _rms_fwd(h, kv_norm, "rms_kv")
            kv = _mm(kvn, wkv_t, "nt", "kv_proj")
            k_sh = _headnorm_fwd(kv, k_norm, "k_norm", width=d)
            v_sh = kv[:, d:].astype(BF16)
            kv_sv = dict(h=h, kvn=kvn, kv=kv)

    dh, loss_vec = _loss_fwd_bwd(h, loss_target[0], "loss")
    loss = lax.psum(jnp.sum(loss_vec), ("x", "y", "c"))

    gw = {}
    small = {}
    g_started = {}

    def scatter_start(name):
        gparts = []
        for key in chunk_keys[name]:
            g = gw[key]
            g = g.reshape(NDEV, -1, d) if key[0] == "ple_w_proj" else g.reshape(NDEV, -1, g.shape[-1])
            padr = local[key].shape[0] - g.shape[1]
            if padr:
                g = jnp.pad(g, ((0, 0), (0, padr), (0, 0)))
            gparts.append(g.reshape(-1, d))
        g_started[name] = _send_start(gparts, True, gparts[0], f"comm_gstart_{name}")
        return g_started[name]["token"]

    dk_sh = jnp.zeros((s, d), F32)
    dv_sh = jnp.zeros((s, d), F32)
    for l in reversed(range(depth)):
        t = sv[l]
        if l == n_a - 1:
            dkv_k, dkn = _headnorm_bwd(dk_sh, kv_sv["kv"], k_norm, "k_norm_bwd", dx_dtype=BF16)
            dkv = jnp.concatenate([dkv_k, dv_sh.astype(BF16)], axis=1)
            gw[("w_kv", 0)] = _mm(dkv, kv_sv["kvn"], "tn", "kv_dw", out_dtype=BF16)
            dh, _, dg = _mm(dkv, wkv_t, "nn", "kv_dx", norm_bwd=(kv_sv["h"], kv_norm, dh))
            small["kv_norm"] = dg
            small["k_norm"] = dkn
        dgp, dpp = _ple_bwd(dh, t["gpre"], t["pp"], f"ple_bwd_{l}")
        gw[("ple_w_gate", l)] = _mm(t["hn3"], dgp, "tn", f"ple_dwg_{l}", out_dtype=BF16)
        gw[("ple_w_proj", l)] = _mm(dpp, p[l, 0], "tn", f"ple_dwp_{l}", out_dtype=BF16)
        dh, dhb, dg = _mm(dgp, w_pg[l], "nt", f"ple_dx_{l}", norm_bwd=(t["h2"], ln_ple[l], dh))
        small[("ln_ple", l)] = dg
        dgs, dus = _swiglu_bwd(dhb, w_fout[l], t["gs"], t["us"], f"ffn_bwd_act_{l}")
        gw[("ffn_w_out", l)] = _mm(t["act"], dhb, "tn", f"ffn_dwo_{l}", out_dtype=BF16)
        f = dgs.shape[1]
        dwg = _mm(dgs, t["hn2"], "tn", f"ffn_dwg_{l}", out_dtype=BF16)
        dwu = _mm(dus, t["hn2"], "tn", f"ffn_dwu_{l}", out_dtype=BF16)
        gw[("ffn_w_in", l)] = jnp.concatenate([dwg, dwu], axis=0)
        dhn2 = _mm(dgs, wf_t[l][:f], "nn", f"ffn_dxg_{l}")
        dh, dhb, dg = _mm(dus, wf_t[l][f:], "nn", f"ffn_dxu_{l}", res=dhn2, norm_bwd=(t["h1"], ln_ffn[l], dh),
                          after=scatter_start(f"f{l}"))
        small[("ln_ffn", l)] = dg
        if l < n_a:
            do2 = _mm(dhb, w_gout[l], "nt", f"gdn_out_dx_{l}")
            gw[("gdn_w_out", l)] = _mm(t["o2"], dhb, "tn", f"gdn_out_dw_{l}", out_dtype=BF16)
            do_raw, dgn, dgate = _headnorm_bwd(do2, t["o_raw"], gdn_norm[l], f"gdn_outnorm_bwd_{l}",
                                               gate=t["proj"], gate_col0=3 * d, head_major=True)
            small[("gdn_norm", l)] = dgn
            dqkv, dgb = _gdn_bwd(t["qkv"], t["gb"], do_raw, t["states"], nh, f"gdn_rule_bwd_{l}")
            dpab, dal, ddt = _gates_bwd(dgb, t["pab"], t["al"], t["dtb"], nh, f"gdn_gates_bwd_{l}")
            small[("gdn_a_log", l)] = dal
            small[("gdn_dt_bias", l)] = ddt
            dproj_qkv, dconv = _conv_bwd(dqkv, t["proj"], conv_full[l], d, f"gdn_conv_bwd_{l}")
            small[("gdn_conv", l)] = dconv
            dproj = jnp.concatenate([dproj_qkv, dgate], axis=1)
            dw_main = _mm(dproj, t["hn"], "tn", f"gdn_proj_dw_{l}", out_dtype=BF16)
            dw_ab = _mm(dpab, t["hn"], "tn", f"gdn_proj_ab_dw_{l}", out_dtype=BF16)
            gw[("gdn_w_in", l)] = jnp.concatenate([dw_main, dw_ab[:16]], axis=0)[:win_rows]
            dhn_ab = _mm(dpab, w_ab_t[l], "nn", f"gdn_proj_ab_dx_{l}")
            last = dict(a=dproj, b=w_in_t[l], mode="nn", name=f"gdn_proj_dx_{l}", res=dhn_ab)
        else:
            j = l - n_a
            do = _mm(dhb, w_sout[j], "nt", f"sb_out_dx_{j}", out_dtype=BF16)
            gw[("sb_w_out", j)] = _mm(t["o"], dhb, "tn", f"sb_out_dw_{j}", out_dtype=BF16)
            dq, dk, dv = _sb_bwd(t["qn"], k_sh, v_sh, do, t["ctab"], f"sb_attn_bwd_{j}")
            dk_sh = dk_sh + dk
            dv_sh = dv_sh + dv
            dqpre, dqn = _headnorm_bwd(dq, t["qpre"], sb_q_norm[j], f"sb_qnorm_bwd_{j}", scale=HEAD ** -0.5, dx_dtype=BF16)
            small[("sb_q_norm", j)] = dqn
            gw[("sb_w_q", j)] = _mm(t["hn"], dqpre, "tn", f"sb_q_dw_{j}", out_dtype=BF16)
            last = dict(a=dqpre, b=w_q[j], mode="nt", name=f"sb_q_dx_{j}")
        dh, _, dg = _mm(**last, norm_bwd=(t["h0"], ln_mix[l], dh), after=scatter_start(f"a{l}"))
        small[("ln_mix", l)] = dg
    grad_x = dh[None]

    gshard = {}
    for name, keys in reversed(chunks):
        lands = _send_wait(g_started[name], True, dh, f"comm_gwait_{name}")
        for key, land in zip(keys, lands):
            gshard[key] = _sum_slots(land.reshape(NDEV, -1, d), f"grad_sum_{key[0]}_{key[1]}")

    def col_back(key, n_valid):
        return jnp.transpose(gshard[key][:n_valid])

    g_gdn_w_in = jnp.stack([col_back(("gdn_w_in", l), win_cols) for l in range(n_a)])
    g_gdn_w_out = jnp.stack([gshard[("gdn_w_out", l)] for l in range(n_a)])
    g_w_kv = col_back(("w_kv", 0), w_kv.shape[1])
    g_sb_w_q = jnp.stack([gshard[("sb_w_q", j)] for j in range(n_b)])
    g_sb_w_out = jnp.stack([gshard[("sb_w_out", j)] for j in range(n_b)])
    g_ffn_w_in = jnp.stack([col_back(("ffn_w_in", l), ffn_w_in.shape[2]) for l in range(depth)])
    g_ffn_w_out = jnp.stack([gshard[("ffn_w_out", l)] for l in range(depth)])
    g_ple_w_proj = jnp.stack([jnp.transpose(gshard[("ple_w_proj", l)].reshape(-1, pd)) for l in range(depth)])
    g_ple_w_gate = jnp.stack([gshard[("ple_w_gate", l)] for l in range(depth)])

    def vec_rows(v):
        return v.reshape(-1, HEAD)

    small_items = []
    for name_, cnt in (("ln_mix", depth), ("ln_ffn", depth), ("ln_ple", depth)):
        for l in range(cnt):
            small_items.append(((name_, l), vec_rows(small[(name_, l)])))
    for l in range(n_a):
        small_items.append((("gdn_conv", l), small[("gdn_conv", l)].reshape(-1, HEAD)))
        small_items.append((("gdn_a_log", l), small[("gdn_a_log", l)]))
        small_items.append((("gdn_dt_bias", l), small[("gdn_dt_bias", l)]))
        small_items.append((("gdn_norm", l), small[("gdn_norm", l)]))
    small_items.append(("kv_norm", vec_rows(small["kv_norm"])))
    small_items.append(("k_norm", small["k_norm"]))
    for j in range(n_b):
        small_items.append((("sb_q_norm", j), small[("sb_q_norm", j)]))
    spack = jnp.concatenate([_pad_rows(a, 8) for _, a in small_items], axis=0)
    sg = _all_gather(spack, "comm_gather_small").reshape(NDEV, spack.shape[0], HEAD)
    ssum = _sum_slots(sg, "small_sum")
    sm = {}
    off = 0
    for key, a in small_items:
        sm[key] = ssum[off:off + a.shape[0]]
        off += a.shape[0] + (-a.shape[0]) % 8

    g_ln_mix = jnp.stack([sm[("ln_mix", l)].reshape(d) for l in range(depth)])
    g_ln_ffn = jnp.stack([sm[("ln_ffn", l)].reshape(d) for l in range(depth)])
    g_ln_ple = jnp.stack([sm[("ln_ple", l)].reshape(d) for l in range(depth)])
    conv_loc = gdn_conv.shape[2]
    g_conv_full = jnp.stack([sm[("gdn_conv", l)].reshape(gdn_conv.shape[1], 3 * d) for l in range(n_a)])
    g_gdn_conv = lax.dynamic_slice_in_dim(g_conv_full, me * conv_loc, conv_loc, axis=2)
    g_a_log = jnp.stack([sm[("gdn_a_log", l)][0, :nh] for l in range(n_a)])
    g_dt_bias = jnp.stack([sm[("gdn_dt_bias", l)][0, :nh] for l in range(n_a)])
    g_gdn_norm = jnp.stack([sm[("gdn_norm", l)][0] for l in range(n_a)])
    g_kv_norm = sm["kv_norm"].reshape(d)
    g_k_norm = sm["k_norm"][0]
    g_sb_q_norm = jnp.stack([sm[("sb_q_norm", j)][0] for j in range(n_b)])

    grads = [g_ln_mix, g_ln_ffn, g_ln_ple, g_gdn_w_in, g_gdn_conv, g_a_log, g_dt_bias, g_gdn_norm, g_gdn_w_out,
             g_kv_norm, g_w_kv, g_k_norm, g_sb_w_q, g_sb_q_norm, g_sb_w_out, g_ffn_w_in, g_ffn_w_out, g_ple_w_proj,
             g_ple_w_gate]
    weights = [ln_mix, ln_ffn, ln_ple, gdn_w_in, gdn_conv, gdn_a_log, gdn_dt_bias, gdn_norm, gdn_w_out, kv_norm, w_kv,
               k_norm, sb_w_q, sb_q_norm, sb_w_out, ffn_w_in, ffn_w_out, ple_w_proj, ple_w_gate]
    moms = [m_ln_mix, m_ln_ffn, m_ln_ple, m_gdn_w_in, m_gdn_conv, m_gdn_a_log, m_gdn_dt_bias, m_gdn_norm, m_gdn_w_out,
            m_kv_norm, m_w_kv, m_k_norm, m_sb_w_q, m_sb_q_norm, m_sb_w_out, m_ffn_w_in, m_ffn_w_out, m_ple_w_proj,
            m_ple_w_gate]
    vels = [v_ln_mix, v_ln_ffn, v_ln_ple, v_gdn_w_in, v_gdn_conv, v_gdn_a_log, v_gdn_dt_bias, v_gdn_norm, v_gdn_w_out,
            v_kv_norm, v_w_kv, v_k_norm, v_sb_w_q, v_sb_q_norm, v_sb_w_out, v_ffn_w_in, v_ffn_w_out, v_ple_w_proj,
            v_ple_w_gate]

    deltas, new_m, new_v = [], [], []
    small_idx = [i for i, w in enumerate(weights) if w.size < 8 * HEAD * 16]
    for i, (w, g, m, v) in enumerate(zip(weights, grads, moms, vels)):
        if i in small_idx:
            deltas.append(None), new_m.append(None), new_v.append(None)
            continue
        shp = w.shape
        two = lambda a: a.reshape(-1, shp[-1])
        dl, nm, nv = _adamw(two(w), two(g), two(m), two(v), f"adamw_{i}")
        deltas.append(dl.reshape(shp)), new_m.append(nm.reshape(shp)), new_v.append(nv.reshape(shp))

    def flat_pack(arrs):
        flat = jnp.concatenate([a.reshape(-1) for a in arrs])
        pad = (-flat.shape[0]) % (8 * HEAD)
        return jnp.pad(flat, (0, pad)).reshape(-1, HEAD)

    sw = flat_pack([weights[i] for i in small_idx])
    sgr = flat_pack([grads[i] for i in small_idx])
    smo = flat_pack([moms[i] for i in small_idx])
    sve = flat_pack([vels[i] for i in small_idx])
    sdl, snm, snv = _adamw(sw, sgr, smo, sve, "adamw_small")
    off = 0
    for i in small_idx:
        n = weights[i].size
        shp = weights[i].shape
        deltas[i] = sdl.reshape(-1)[off:off + n].reshape(shp)
        new_m[i] = snm.reshape(-1)[off:off + n].reshape(shp)
        new_v[i] = snv.reshape(-1)[off:off + n].reshape(shp)
        off += n

    return (loss, grad_x, *grads, *deltas, *new_m, *new_v)
```

```python
import math

import jax
import jax.numpy as jnp
from jax import lax
from jax.experimental import pallas as pl
from jax.experimental.pallas import tpu as pltpu

F32 = jnp.float32
BF16 = jnp.bfloat16
NDEV = 8
HEAD = 128
CHUNK = 64
SBLK = 256
EPS = 1e-6
LR, B1, B2, ADAM_EPS, WD, STEP = 0.001, 0.9, 0.999, 1e-08, 0.01, 10
NEG = -1e30
MM_VMEM_BUDGET = 40 * 1024 * 1024

NN = (((1,), (0,)), ((), ()))
NT = (((1,), (1,)), ((), ()))
TN = (((0,), (0,)), ((), ()))
BNN = (((2,), (1,)), ((0,), (0,)))
BNT = (((2,), (2,)), ((0,), (0,)))
BTN = (((1,), (1,)), ((0,), (0,)))
MESH = pl.DeviceIdType.MESH


def _dot(a, b, dims=NN):
    return lax.dot_general(a.astype(BF16), b.astype(BF16), dims, preferred_element_type=F32)


def _dot_hilo(a, b01, dims=NN):
    hi = a.astype(BF16)
    lo = (a - hi.astype(F32)).astype(BF16)
    return (lax.dot_general(hi, b01, dims, preferred_element_type=F32)
            + lax.dot_general(lo, b01, dims, preferred_element_type=F32))


def _pick(dim, cands):
    for c in cands:
        if dim % c == 0:
            return c
    return dim


def _params(sem, vmem_mb=48):
    return pltpu.CompilerParams(dimension_semantics=sem, vmem_limit_bytes=vmem_mb * 1024 * 1024)


def _silu(x):
    return x * jax.nn.sigmoid(x)


def _silu_and_grad(x):
    s = jax.nn.sigmoid(x)
    xs = x * s
    return xs, s + xs * (1.0 - s)


def _mm(a, b, mode, name, out_dtype=F32, res=None, norm_g=None, norm_bwd=None, after=None):
    if mode == "nn":
        (m, k), n = a.shape, b.shape[1]
    elif mode == "nt":
        (m, k), n = a.shape, b.shape[0]
    else:
        (k, m), n = a.shape, b.shape[1]
    rows = norm_g is not None or norm_bwd is not None
    tn = n if rows else _pick(n, (512, 256, 128))
    tk = k if k <= 4096 else _pick(k, (2048, 1024, 512, 256, 128))
    nk = k // tk
    out_b = jnp.dtype(out_dtype).itemsize + (res.dtype.itemsize if res is not None else 0)
    out_b += 2 if norm_g is not None else 0
    out_b += 10 if norm_bwd is not None else 0
    for tm in [t for t in range(min(m, 2048), 127, -128) if m % t == 0] + [m]:
        need = 2 * (tm * tk * a.dtype.itemsize + tk * tn * b.dtype.itemsize + tm * tn * out_b) + 4 * tm * tn
        if need <= MM_VMEM_BUDGET:
            break
    dims = {"nn": NN, "nt": NT, "tn": TN}[mode]
    if mode == "tn":
        a_spec = pl.BlockSpec((tk, tm), lambda i, j, kk: (kk, i))
    else:
        a_spec = pl.BlockSpec((tm, tk), lambda i, j, kk: (i, kk))
    if mode == "nt":
        b_spec = pl.BlockSpec((tn, tk), lambda i, j, kk: (j, kk))
    else:
        b_spec = pl.BlockSpec((tk, tn), lambda i, j, kk: (kk, j))
    mn_spec = pl.BlockSpec((tm, tn), lambda i, j, kk: (i, j))
    vec_spec = pl.BlockSpec((1, tn), lambda i, j, kk: (0, j))
    has_res = res is not None
    n_in = 2 + has_res + (1 if norm_g is not None else 0) + (3 if norm_bwd is not None else 0) + (after is not None)

    def body(*refs):
        a_ref, b_ref = refs[:2]
        extra = list(refs[2:n_in])
        outs = refs[n_in:-1]
        acc = refs[-1]
        kk = pl.program_id(2)

        @pl.when(kk == 0)
        def _():
            acc[...] = jnp.zeros_like(acc)

        if norm_bwd is not None:
            @pl.when((kk == 0) & (pl.program_id(0) == 0))
            def _():
                outs[2][...] = jnp.zeros_like(outs[2])

        acc[...] += _dot(a_ref[...], b_ref[...], dims)

        @pl.when(kk == nk - 1)
        def _():
            r = acc[...]
            if has_res:
                r = r + extra.pop(0)[...].astype(F32)
            if norm_g is not None:
                outs[0][...] = r.astype(out_dtype)
                rs = lax.rsqrt(jnp.mean(r * r, axis=-1, keepdims=True) + EPS)
                outs[1][...] = (r * rs * extra.pop(0)[...]).astype(BF16)
            elif norm_bwd is not None:
                xv, gv, dres = extra.pop(0)[...], extra.pop(0)[...], extra.pop(0)[...]
                rs = lax.rsqrt(jnp.mean(xv * xv, axis=-1, keepdims=True) + EPS)
                gdy = r * gv
                dx = dres + rs * gdy - xv * (rs * rs * rs) * jnp.mean(xv * gdy, axis=-1, keepdims=True)
                outs[0][...] = dx
                outs[1][...] = dx.astype(BF16)
                outs[2][...] += jnp.sum(r * xv * rs, axis=0, keepdims=True)
            else:
                outs[0][...] = r.astype(out_dtype)

    ins = [a, b] + ([res] if has_res else [])
    in_specs = [a_spec, b_spec] + ([mn_spec] if has_res else [])
    out_specs, out_shape = [mn_spec], [jax.ShapeDtypeStruct((m, n), out_dtype)]
    sem = ("parallel", "parallel", "arbitrary")
    if norm_g is not None:
        ins.append(norm_g.reshape(1, n))
        in_specs.append(vec_spec)
        out_specs.append(mn_spec)
        out_shape.append(jax.ShapeDtypeStruct((m, n), BF16))
    if norm_bwd is not None:
        x, g, dres = norm_bwd
        ins += [x, g.reshape(1, n), dres]
        in_specs += [mn_spec, vec_spec, mn_spec]
        out_specs += [mn_spec, vec_spec]
        out_shape += [jax.ShapeDtypeStruct((m, n), BF16), jax.ShapeDtypeStruct((1, n), F32)]
        sem = ("arbitrary", "arbitrary", "arbitrary")
    if after is not None:
        ins.append(after)
        in_specs.append(pl.BlockSpec(memory_space=pl.ANY))
    out = pl.pallas_call(
        body, name=name, grid=(m // tm, n // tn, nk), in_specs=in_specs, out_specs=out_specs,
        out_shape=out_shape, scratch_shapes=[pltpu.VMEM((tm, tn), F32)],
        compiler_params=_params(sem))(*ins)
    return out[0] if len(out) == 1 else out


def _rms_fwd(h, g, name):
    s, d = h.shape
    tm = _pick(s, (512, 256, 128))

    def body(h_ref, g_ref, o_ref):
        x = h_ref[...]
        r = lax.rsqrt(jnp.mean(x * x, axis=-1, keepdims=True) + EPS)
        o_ref[...] = (x * r * g_ref[...]).astype(BF16)

    return pl.pallas_call(
        body, name=name, grid=(s // tm,),
        in_specs=[pl.BlockSpec((tm, d), lambda i: (i, 0)), pl.BlockSpec((1, d), lambda i: (0, 0))],
        out_specs=pl.BlockSpec((tm, d), lambda i: (i, 0)),
        out_shape=jax.ShapeDtypeStruct((s, d), BF16), compiler_params=_params(("parallel",)))(h, g.reshape(1, d))


def _headnorm_fwd(x, g, name, scale=1.0, gate=None, gate_col0=0, out_dtype=BF16, width=None, head_major=False):
    if head_major:
        s, d = x.shape[1], x.shape[0] * HEAD
    else:
        s, d = x.shape[0], (width or x.shape[1])
    nh = d // HEAD
    tm = _pick(s, (256, 128))
    has_gate = gate is not None
    gb = gate_col0 // d

    def body(*refs):
        if has_gate:
            x_ref, g_ref, gt_ref, o_ref = refs
        else:
            x_ref, g_ref, o_ref = refs
        gv = g_ref[...]
        for h in range(nh):
            sl = slice(h * HEAD, (h + 1) * HEAD)
            xv = (x_ref[h] if head_major else x_ref[:, sl]).astype(F32)
            r = lax.rsqrt(jnp.mean(xv * xv, axis=-1, keepdims=True) + EPS)
            y = xv * r * gv
            if scale != 1.0:
                y = y * scale
            if has_gate:
                y = y * _silu(gt_ref[:, sl])
            o_ref[:, sl] = y.astype(out_dtype)

    row = pl.BlockSpec((tm, d), lambda i: (i, 0))
    hm = pl.BlockSpec((nh, tm, HEAD), lambda i: (0, i, 0))
    ins = [x, g.reshape(1, HEAD)]
    in_specs = [hm if head_major else row, pl.BlockSpec((1, HEAD), lambda i: (0, 0))]
    if has_gate:
        ins.append(gate)
        in_specs.append(pl.BlockSpec((tm, d), lambda i: (i, gb)))
    return pl.pallas_call(
        body, name=name, grid=(s // tm,), in_specs=in_specs, out_specs=row,
        out_shape=jax.ShapeDtypeStruct((s, d), out_dtype), compiler_params=_params(("parallel",)))(*ins)


def _headnorm_bwd(dy, x, g, name, scale=1.0, gate=None, gate_col0=0, dx_dtype=F32, head_major=False):
    s, d = dy.shape
    nh = d // HEAD
    tm = _pick(s, (256, 128))
    has_gate = gate is not None
    gb = gate_col0 // d

    def body(*refs):
        if has_gate:
            dy_ref, x_ref, g_ref, gt_ref, dx_ref, dg_ref, dgt_ref = refs
        else:
            dy_ref, x_ref, g_ref, dx_ref, dg_ref = refs

        @pl.when(pl.program_id(0) == 0)
        def _():
            dg_ref[...] = jnp.zeros_like(dg_ref)

        gv = g_ref[...]
        dg_acc = jnp.zeros((1, HEAD), F32)
        for h in range(nh):
            sl = slice(h * HEAD, (h + 1) * HEAD)
            xv = (x_ref[h] if head_major else x_ref[:, sl]).astype(F32)
            dyv = dy_ref[:, sl].astype(F32)
            r = lax.rsqrt(jnp.mean(xv * xv, axis=-1, keepdims=True) + EPS)
            if has_gate:
                gt = gt_ref[:, sl]
                act, dact = _silu_and_grad(gt)
                dgt_ref[:, sl] = (dyv * (xv * r * gv) * dact).astype(dgt_ref.dtype)
                dn = dyv * act
            else:
                dn = dyv
            if scale != 1.0:
                dn = dn * scale
            gdn = dn * gv
            mean_t = jnp.mean(xv * gdn, axis=-1, keepdims=True)
            dxv = (r * gdn - xv * (r * r * r) * mean_t).astype(dx_dtype)
            if head_major:
                dx_ref[h] = dxv
            else:
                dx_ref[:, sl] = dxv
            dg_acc = dg_acc + jnp.sum(dn * xv * r, axis=0, keepdims=True)
        dg_ref[...] += dg_acc

    row = pl.BlockSpec((tm, d), lambda i: (i, 0))
    hm = pl.BlockSpec((nh, tm, HEAD), lambda i: (0, i, 0))
    vec = pl.BlockSpec((1, HEAD), lambda i: (0, 0))
    ins = [dy, x, g.reshape(1, HEAD)]
    in_specs = [row, hm if head_major else row, vec]
    out_specs = [hm if head_major else row, vec]
    dx_shape = (nh, s, HEAD) if head_major else (s, d)
    out_shape = [jax.ShapeDtypeStruct(dx_shape, dx_dtype), jax.ShapeDtypeStruct((1, HEAD), F32)]
    if has_gate:
        ins.append(gate)
        in_specs.append(pl.BlockSpec((tm, d), lambda i: (i, gb)))
        out_specs.append(row)
        out_shape.append(jax.ShapeDtypeStruct((s, d), BF16))
    return pl.pallas_call(
        body, name=name, grid=(s // tm,), in_specs=in_specs, out_specs=out_specs, out_shape=out_shape,
        compiler_params=_params(("arbitrary",)))(*ins)


def _swiglu_fwd(hn, wf_t, name):
    s, d = hn.shape
    f = wf_t.shape[0] // 2
    tm = _pick(s, (1024, 512, 256, 128))
    tn = _pick(f, (512, 256, 128))
    nj = f // tn

    def body(a_ref, wg_ref, wu_ref, act_ref, g_ref, u_ref):
        a = a_ref[...]
        g = _dot(a, wg_ref[...], NT)
        u = _dot(a, wu_ref[...], NT)
        act_ref[...] = (_silu(g) * u).astype(BF16)
        g_ref[...] = g.astype(BF16)
        u_ref[...] = u.astype(BF16)

    o_spec = pl.BlockSpec((tm, tn), lambda i, j: (i, j))
    sds = jax.ShapeDtypeStruct((s, f), BF16)
    return pl.pallas_call(
        body, name=name, grid=(s // tm, nj),
        in_specs=[pl.BlockSpec((tm, d), lambda i, j: (i, 0)), pl.BlockSpec((tn, d), lambda i, j: (j, 0)),
                  pl.BlockSpec((tn, d), lambda i, j: (j + nj, 0))],
        out_specs=[o_spec, o_spec, o_spec], out_shape=[sds, sds, sds],
        compiler_params=_params(("parallel", "parallel")))(hn, wf_t, wf_t)


def _swiglu_bwd(dh, w_out, g, u, name):
    s, d = dh.shape
    f = w_out.shape[0]
    tm = _pick(s, (1024, 512, 256, 128))
    tn = _pick(f, (512, 256, 128))

    def body(dh_ref, w_ref, g_ref, u_ref, dg_ref, du_ref):
        dact = _dot(dh_ref[...], w_ref[...], NT)
        gv = g_ref[...].astype(F32)
        uv = u_ref[...].astype(F32)
        sg, dsg = _silu_and_grad(gv)
        dg_ref[...] = (dact * uv * dsg).astype(BF16)
        du_ref[...] = (dact * sg).astype(BF16)

    o_spec = pl.BlockSpec((tm, tn), lambda i, j: (i, j))
    sds = jax.ShapeDtypeStruct((s, f), BF16)
    return pl.pallas_call(
        body, name=name, grid=(s // tm, f // tn),
        in_specs=[pl.BlockSpec((tm, d), lambda i, j: (i, 0)), pl.BlockSpec((tn, d), lambda i, j: (j, 0)), o_spec, o_spec],
        out_specs=[o_spec, o_spec], out_shape=[sds, sds],
        compiler_params=_params(("parallel", "parallel")))(dh, w_out, g, u)


def _ple_fwd(h, hn, p, w_gate, wp_t, name, norm_gs=()):
    s, d = h.shape
    pd = p.shape[1]
    tm = _pick(s, (512, 256, 128))
    ng = len(norm_gs)

    def body(h_ref, hn_ref, p_ref, wg_ref, wp_ref, *rest):
        g_refs, (o_ref, gp_ref, pp_ref), n_refs = rest[:ng], rest[ng:ng + 3], rest[ng + 3:]
        gpre = _dot(hn_ref[...], wg_ref[...], NN)
        pp = _dot(p_ref[...], wp_ref[...], NT)
        o = h_ref[...] + pp * jax.nn.sigmoid(gpre)
        o_ref[...] = o
        gp_ref[...] = gpre.astype(BF16)
        pp_ref[...] = pp.astype(BF16)
        if ng:
            on = o * lax.rsqrt(jnp.mean(o * o, axis=-1, keepdims=True) + EPS)
            for g_ref, n_ref in zip(g_refs, n_refs):
                n_ref[...] = (on * g_ref[...]).astype(BF16)

    row = pl.BlockSpec((tm, d), lambda i: (i, 0))
    vec = pl.BlockSpec((1, d), lambda i: (0, 0))
    bf = jax.ShapeDtypeStruct((s, d), BF16)
    return pl.pallas_call(
        body, name=name, grid=(s // tm,),
        in_specs=[row, row, pl.BlockSpec((tm, pd), lambda i: (i, 0)), pl.BlockSpec((d, d), lambda i: (0, 0)),
                  pl.BlockSpec((d, pd), lambda i: (0, 0))] + [vec] * ng,
        out_specs=[row] * (3 + ng), out_shape=[jax.ShapeDtypeStruct((s, d), F32), bf, bf] + [bf] * ng,
        compiler_params=_params(("parallel",)))(h, hn, p, w_gate, wp_t, *[g.reshape(1, d) for g in norm_gs])


def _ple_bwd(dh, gpre, pp, name):
    s, d = dh.shape
    tm = _pick(s, (512, 256, 128))

    def body(dh_ref, gp_ref, pp_ref, dgp_ref, dpp_ref):
        dv = dh_ref[...]
        sig = jax.nn.sigmoid(gp_ref[...].astype(F32))
        ppv = pp_ref[...].astype(F32)
        dpp_ref[...] = (dv * sig).astype(BF16)
        dgp_ref[...] = (dv * ppv * sig * (1.0 - sig)).astype(BF16)

    row = pl.BlockSpec((tm, d), lambda i: (i, 0))
    sds = jax.ShapeDtypeStruct((s, d), BF16)
    return pl.pallas_call(
        body, name=name, grid=(s // tm,), in_specs=[row, row, row], out_specs=[row, row], out_shape=[sds, sds],
        compiler_params=_params(("parallel",)))(dh, gpre, pp)


def _loss_fwd_bwd(y, t, name):
    s, d = y.shape
    tm = _pick(s, (512, 256, 128))

    def body(y_ref, t_ref, dy_ref, l_ref):
        @pl.when(pl.program_id(0) == 0)
        def _():
            l_ref[...] = jnp.zeros_like(l_ref)

        e = y_ref[...] - t_ref[...]
        dy_ref[...] = e * (1.0 / d)
        l_ref[...] += jnp.sum(e * e, axis=0, keepdims=True) * (0.5 / d)

    row = pl.BlockSpec((tm, d), lambda i: (i, 0))
    vec = pl.BlockSpec((1, d), lambda i: (0, 0))
    return pl.pallas_call(
        body, name=name, grid=(s // tm,), in_specs=[row, row], out_specs=[row, vec],
        out_shape=[jax.ShapeDtypeStruct((s, d), F32), jax.ShapeDtypeStruct((1, d), F32)],
        compiler_params=_params(("arbitrary",)))(y, t)


PADR = 8


def _conv_fwd(proj, w_conv, d, name):
    s = proj.shape[0]
    nh = d // HEAD
    kw = w_conv.shape[0]
    qscale = HEAD ** -0.5

    def body(x_ref, w_ref, o_ref, xp):
        kind = pl.program_id(0) // nh
        xp[0:PADR, :] = jnp.zeros((PADR, HEAD), F32)
        xp[PADR:, :] = x_ref[...]
        acc = jnp.zeros((s, HEAD), F32)
        for j in range(kw):
            acc = acc + w_ref[j:j + 1, :] * xp[PADR - (kw - 1) + j:PADR - (kw - 1) + j + s, :]
        a = _silu(acc)
        r = lax.rsqrt(jnp.sum(a * a, axis=-1, keepdims=True) + EPS)
        fac = jnp.where(kind == 0, r * qscale, jnp.where(kind == 1, r, jnp.ones_like(r)))
        o_ref[...] = a * fac

    blk = pl.BlockSpec((s, HEAD), lambda c: (0, c))
    hm = pl.BlockSpec((None, s, HEAD), lambda c: (c, 0, 0))
    return pl.pallas_call(
        body, name=name, grid=(3 * nh,), in_specs=[blk, pl.BlockSpec((kw, HEAD), lambda c: (0, c))], out_specs=hm,
        out_shape=jax.ShapeDtypeStruct((3 * nh, s, HEAD), F32), scratch_shapes=[pltpu.VMEM((s + PADR, HEAD), F32)],
        compiler_params=_params(("parallel",)))(proj, w_conv)


def _conv_bwd(dqkv, proj, w_conv, d, name):
    s = proj.shape[0]
    nh = d // HEAD
    kw = w_conv.shape[0]
    qscale = HEAD ** -0.5

    def body(dy_ref, x_ref, w_ref, dx_ref, dw_ref, xp, dp):
        kind = pl.program_id(0) // nh
        xp[0:PADR, :] = jnp.zeros((PADR, HEAD), F32)
        xp[PADR:, :] = x_ref[...]
        acc = jnp.zeros((s, HEAD), F32)
        for j in range(kw):
            acc = acc + w_ref[j:j + 1, :] * xp[PADR - (kw - 1) + j:PADR - (kw - 1) + j + s, :]
        a, da_dacc = _silu_and_grad(acc)
        dy = dy_ref[...]
        r = lax.rsqrt(jnp.sum(a * a, axis=-1, keepdims=True) + EPS)
        sc = jnp.where(kind == 0, qscale, 1.0)
        dyn = dy * sc
        da_norm = r * dyn - a * (r * r * r) * jnp.sum(a * dyn, axis=-1, keepdims=True)
        da = jnp.where(kind == 2, dy, da_norm)
        dacc = da * da_dacc
        dp[0:s, :] = dacc
        dp[s:, :] = jnp.zeros((PADR, HEAD), F32)
        dx = jnp.zeros((s, HEAD), F32)
        for j in range(kw):
            sh = kw - 1 - j
            dx = dx + w_ref[j:j + 1, :] * dp[sh:sh + s, :]
            dw_ref[j:j + 1, :] = jnp.sum(dacc * xp[PADR - sh:PADR - sh + s, :], axis=0, keepdims=True)
        dx_ref[...] = dx.astype(BF16)

    blk = pl.BlockSpec((s, HEAD), lambda c: (0, c))
    hm = pl.BlockSpec((None, s, HEAD), lambda c: (c, 0, 0))
    wblk = pl.BlockSpec((kw, HEAD), lambda c: (0, c))
    return pl.pallas_call(
        body, name=name, grid=(3 * nh,), in_specs=[hm, blk, wblk], out_specs=[blk, wblk],
        out_shape=[jax.ShapeDtypeStruct((s, 3 * d), BF16), jax.ShapeDtypeStruct((kw, 3 * d), F32)],
        scratch_shapes=[pltpu.VMEM((s + PADR, HEAD), F32), pltpu.VMEM((s + PADR, HEAD), F32)],
        compiler_params=_params(("parallel",)))(dqkv, proj, w_conv)


def _softplus(x):
    return jnp.maximum(x, 0.0) + jnp.log(1.0 + jnp.exp(-jnp.abs(x)))


def _gates_fwd(pab, a_log, dt_bias, nh, name):
    s = pab.shape[0]
    tm = _pick(s, (512, 256, 128))

    def body(x_ref, al_ref, dt_ref, o_ref):
        x = x_ref[...]
        lane = lax.broadcasted_iota(jnp.int32, x.shape, 1)
        g = -jnp.exp(al_ref[...]) * _softplus(x + dt_ref[...])
        o_ref[...] = jnp.where(lane < nh, g, jnp.where(lane < 2 * nh, jax.nn.sigmoid(x), 0.0))

    row = pl.BlockSpec((tm, HEAD), lambda i: (i, 0))
    vec = pl.BlockSpec((1, HEAD), lambda i: (0, 0))
    return pl.pallas_call(
        body, name=name, grid=(s // tm,), in_specs=[row, vec, vec], out_specs=row,
        out_shape=jax.ShapeDtypeStruct((s, HEAD), F32), compiler_params=_params(("parallel",)))(pab, a_log, dt_bias)


def _gates_bwd(dgb, pab, a_log, dt_bias, nh, name):
    s = pab.shape[0]
    tm = _pick(s, (512, 256, 128))

    def body(d_ref, x_ref, al_ref, dt_ref, dx_ref, dal_ref, ddt_ref):
        @pl.when(pl.program_id(0) == 0)
        def _():
            dal_ref[...] = jnp.zeros_like(dal_ref)
            ddt_ref[...] = jnp.zeros_like(ddt_ref)

        x = x_ref[...]
        dv = d_ref[...]
        lane = lax.broadcasted_iota(jnp.int32, x.shape, 1)
        ea = jnp.exp(al_ref[...])
        xs = x + dt_ref[...]
        g = -ea * _softplus(xs)
        dxs = jnp.where(lane < nh, dv * (-ea) * jax.nn.sigmoid(xs), 0.0)
        sg = jax.nn.sigmoid(x)
        dxb = jnp.where((lane >= nh) & (lane < 2 * nh), dv * sg * (1.0 - sg), 0.0)
        dx_ref[...] = (dxs + dxb).astype(BF16)
        dal_ref[...] += jnp.sum(jnp.where(lane < nh, dv * g, 0.0), axis=0, keepdims=True)
        ddt_ref[...] += jnp.sum(dxs, axis=0, keepdims=True)

    row = pl.BlockSpec((tm, HEAD), lambda i: (i, 0))
    vec = pl.BlockSpec((1, HEAD), lambda i: (0, 0))
    return pl.pallas_call(
        body, name=name, grid=(s // tm,), in_specs=[row, row, vec, vec], out_specs=[row, vec, vec],
        out_shape=[jax.ShapeDtypeStruct((s, HEAD), BF16), jax.ShapeDtypeStruct((1, HEAD), F32),
                   jax.ShapeDtypeStruct((1, HEAD), F32)],
        compiler_params=_params(("arbitrary",)))(dgb, pab, a_log, dt_bias)


def _tri_inv(a_low, eye_f):
    n = -a_low
    p = eye_f + n
    steps = int(math.log2(a_low.shape[-1])) - 1
    for _ in range(steps):
        n = _dot(n, n, BNN)
        p = p + _dot(p, n, BNN)
    return p


def _lane_col(x, lane, idx):
    return jnp.sum(jnp.where(lane == idx, x, 0.0), axis=1, keepdims=True)


def _head_cols(gbv, lo, nh):
    lane = lax.broadcasted_iota(jnp.int32, gbv.shape, 1)
    return jnp.stack([_lane_col(gbv, lane, lo + h) for h in range(nh)], axis=0)


def _gdn_chunk(q, k, v, g_col, beta_col, st):
    c = q.shape[1]
    r_i = lax.broadcasted_iota(jnp.int32, (c, c), 0)
    c_i = lax.broadcasted_iota(jnp.int32, (c, c), 1)
    incl = c_i <= r_i
    strict = c_i < r_i
    eye = c_i == r_i
    g_row = jnp.sum(jnp.where(eye, g_col, 0.0), axis=1, keepdims=True)
    gc_col = jnp.sum(jnp.where(incl, g_row, 0.0), axis=2, keepdims=True)
    gc_row = jnp.sum(jnp.where(eye, gc_col, 0.0), axis=1, keepdims=True)
    g_last = jnp.sum(g_col, axis=1, keepdims=True)
    decay = jnp.exp(jnp.where(incl, gc_col - gc_row, NEG))
    kk = _dot(k, k, BNT)
    a_low = jnp.where(strict, beta_col * kk * decay, 0.0)
    t_inv = _tri_inv(a_low, eye.astype(F32))
    e_g = jnp.exp(gc_col)
    bk = beta_col * e_g
    rhs = jnp.concatenate([v * beta_col, k * bk], axis=2)
    sol = _dot(t_inv, rhs, BNN)
    u, w = sol[:, :, :HEAD], sol[:, :, HEAD:]
    qk_raw = _dot(q, k, BNT)
    qk = qk_raw * decay
    q_dec = q * e_g
    e2 = jnp.exp(g_last - gc_col)
    k_dec = k * e2
    gl = jnp.exp(g_last)
    ws = _dot(jnp.concatenate([w, q_dec], axis=1), st, BNN)
    v_new = u - ws[:, :c]
    o = ws[:, c:] + _dot(qk, v_new, BNN)
    st_new = st * gl + _dot(k_dec, v_new, BTN)
    inter = dict(incl=incl, strict=strict, eye=eye, decay=decay, kk=kk, t_inv=t_inv, e_g=e_g, bk=bk, sol=sol, w=w,
                 qk_raw=qk_raw, qk=qk, q_dec=q_dec, e2=e2, k_dec=k_dec, gl=gl, v_new=v_new, c_i=c_i, r_i=r_i)
    return o, st_new, inter


def _gdn_fwd(qkv, gb, nh, name):
    s = qkv.shape[1]
    nc = s // CHUNK

    def body(q_ref, k_ref, v_ref, gb_ref, o_ref, st_ref, state):
        @pl.when(pl.program_id(0) == 0)
        def _():
            state[...] = jnp.zeros_like(state)

        gbv = gb_ref[...]
        st = state[...]
        st_ref[...] = st
        o, st_new, _ = _gdn_chunk(q_ref[...], k_ref[...], v_ref[...], _head_cols(gbv, 0, nh), _head_cols(gbv, nh, nh), st)
        o_ref[...] = o
        state[...] = st_new

    def qspec(part):
        return pl.BlockSpec((nh, CHUNK, HEAD), lambda n: (part, n, 0))

    return pl.pallas_call(
        body, name=name, grid=(nc,),
        in_specs=[qspec(0), qspec(1), qspec(2), pl.BlockSpec((CHUNK, HEAD), lambda n: (n, 0))],
        out_specs=[qspec(0), pl.BlockSpec((None, nh, HEAD, HEAD), lambda n: (n, 0, 0, 0))],
        out_shape=[jax.ShapeDtypeStruct((nh, s, HEAD), F32), jax.ShapeDtypeStruct((nc, nh, HEAD, HEAD), F32)],
        scratch_shapes=[pltpu.VMEM((nh, HEAD, HEAD), F32)],
        compiler_params=_params(("arbitrary",)))(qkv, qkv, qkv, gb)


def _gdn_bwd(qkv, gb, do, states, nh, name):
    s = qkv.shape[1]
    nc = s // CHUNK
    c = CHUNK

    def body(q_ref, k_ref, v_ref, gb_ref, do_ref, st_ref, dqkv_ref, dgb_ref, dstate):
        @pl.when(pl.program_id(0) == 0)
        def _():
            dstate[...] = jnp.zeros_like(dstate)

        gbv = gb_ref[...]
        lane = lax.broadcasted_iota(jnp.int32, gbv.shape, 1)
        q, k, v = q_ref[...], k_ref[...], v_ref[...]
        beta_col = _head_cols(gbv, nh, nh)
        st = st_ref[...]
        dst = dstate[...]
        dov = do_ref[...]
        _, _, it = _gdn_chunk(q, k, v, _head_cols(gbv, 0, nh), beta_col, st)
        incl, strict, eye, decay = it["incl"], it["strict"], it["eye"], it["decay"]
        dv_new = _dot(it["qk"], dov, BTN) + _dot(it["k_dec"], dst, BNN)
        d_qk = _dot(dov, it["v_new"], BNT)
        dd = _dot(jnp.concatenate([dov, -dv_new], axis=1), st, BNT)
        dq_dec, dw = dd[:, :c], dd[:, c:]
        dst_new = _dot(it["q_dec"], dov, BTN) + it["gl"] * dst - _dot(it["w"], dv_new, BTN)
        dgl = jnp.sum(jnp.sum(dst * st, axis=2, keepdims=True), axis=1, keepdims=True)
        dk_dec = _dot(it["v_new"], dst, BNT)
        dsol = jnp.concatenate([dv_new, dw], axis=2)
        drhs = _dot(it["t_inv"], dsol, BTN)
        d_a = jnp.where(strict, -_dot(drhs, it["sol"], BNT), 0.0)
        drhs_u, drhs_w = drhs[:, :, :HEAD], drhs[:, :, HEAD:]
        dvh = beta_col * drhs_u
        rw_k = jnp.sum(drhs_w * k, axis=2, keepdims=True)
        dbeta = jnp.sum(drhs_u * v, axis=2, keepdims=True) + it["e_g"] * rw_k
        dkh = it["bk"] * drhs_w
        dgc_col = it["bk"] * rw_k
        dkk = d_a * beta_col * decay
        dbeta = dbeta + jnp.sum(d_a * it["kk"] * decay, axis=2, keepdims=True)
        ddecay = d_a * beta_col * it["kk"]
        dkh = dkh + _dot(dkk, k, BNN) + _dot(dkk, k, BTN)
        dqk_raw = d_qk * decay
        ddecay = ddecay + d_qk * it["qk_raw"]
        dqh = _dot(dqk_raw, k, BNN)
        dkh = dkh + _dot(dqk_raw, q, BTN)
        ddm = jnp.where(incl, ddecay * decay, 0.0)
        dgc_col = dgc_col + jnp.sum(ddm, axis=2, keepdims=True)
        dgc_row = -jnp.sum(ddm, axis=1, keepdims=True)
        dqh = dqh + dq_dec * it["e_g"]
        dgc_col = dgc_col + jnp.sum(dq_dec * it["q_dec"], axis=2, keepdims=True)
        dkh = dkh + dk_dec * it["e2"]
        tmp = jnp.sum(dk_dec * it["k_dec"], axis=2, keepdims=True)
        dgc_col = dgc_col - tmp
        dg_last = jnp.sum(tmp, axis=1, keepdims=True) + dgl * it["gl"]
        dgc_tot_row = dgc_row + jnp.sum(jnp.where(eye, dgc_col, 0.0), axis=1, keepdims=True)
        dg_col = jnp.sum(jnp.where(it["c_i"] >= it["r_i"], dgc_tot_row, 0.0), axis=2, keepdims=True) + dg_last
        dqkv_ref[0] = dqh
        dqkv_ref[1] = dkh
        dqkv_ref[2] = dvh
        dstate[...] = dst_new
        dgb_acc = jnp.zeros(gbv.shape, F32)
        for h in range(nh):
            dgb_acc = jnp.where(lane == h, dg_col[h], jnp.where(lane == nh + h, dbeta[h], dgb_acc))
        dgb_ref[...] = dgb_acc

    def rev(part):
        return pl.BlockSpec((nh, CHUNK, HEAD), lambda n: (part, nc - 1 - n, 0))

    gspec = pl.BlockSpec((CHUNK, HEAD), lambda n: (nc - 1 - n, 0))
    dqkv, dgb = pl.pallas_call(
        body, name=name, grid=(nc,),
        in_specs=[rev(0), rev(1), rev(2), gspec, rev(0),
                  pl.BlockSpec((None, nh, HEAD, HEAD), lambda n: (nc - 1 - n, 0, 0, 0))],
        out_specs=[pl.BlockSpec((3, nh, CHUNK, HEAD), lambda n: (0, 0, nc - 1 - n, 0)), gspec],
        out_shape=[jax.ShapeDtypeStruct((3, nh, s, HEAD), F32), jax.ShapeDtypeStruct((s, HEAD), F32)],
        scratch_shapes=[pltpu.VMEM((nh, HEAD, HEAD), F32)],
        compiler_params=_params(("arbitrary",)))(qkv, qkv, qkv, gb, do, states)
    return dqkv.reshape(3 * nh, s, HEAD), dgb


SB_TQ = 512


def _tri01(rel):
    j_i = lax.broadcasted_iota(jnp.int32, (SBLK, SBLK), 0)
    s_i = lax.broadcasted_iota(jnp.int32, (SBLK, SBLK), 1)
    return rel(j_i, s_i).astype(BF16)


SB_HP = 2


def _each(fn, *lists):
    return [fn(*xs) for xs in zip(*lists)]


def _sb_scores(qts, kblks, mask, csums, rhs01):
    zs = _each(lambda qt, kb: _dot(qt, kb, NT), qts, kblks)
    es = _each(lambda z: jnp.exp(-jnp.abs(z)), zs)
    sps = _each(lambda z, e: jnp.maximum(z, 0.0) + jnp.log(1.0 + e), zs, es)
    lns = _each(lambda sp: -sp if mask is None else jnp.where(mask, -sp, 0.0), sps)
    sts = _each(lambda ln: _dot_hilo(ln, rhs01), lns)
    wgts = _each(lambda z, sp, st, cs: jnp.exp((z - sp) + st + cs), zs, sps, sts, csums)
    if mask is not None:
        wgts = _each(lambda w: jnp.where(mask, w, 0.0), wgts)
    return zs, es, wgts, lns


def _band_mask(rows, j, row0):
    r_i = lax.broadcasted_iota(jnp.int32, (rows, SBLK), 0)
    c_i = lax.broadcasted_iota(jnp.int32, (rows, SBLK), 1)
    return (j * SBLK + c_i) < (row0 + r_i)


def _sb_fwd(q, k, v, name):
    s, d = q.shape
    nh = d // HEAD
    tq = min(SB_TQ, s)
    nb = tq // SBLK

    hp = SB_HP
    heads = [slice(h * HEAD, (h + 1) * HEAD) for h in range(hp)]

    def body(q_ref, k_ref, v_ref, o_ref, c_ref, acc, cs):
        qb = pl.program_id(1)
        lane = lax.broadcasted_iota(jnp.int32, (tq, HEAD), 1)
        after = _tri01(lambda j, t: j > t)
        acc[...] = jnp.zeros_like(acc)
        cs[...] = jnp.zeros_like(cs)
        c_ref[...] = jnp.zeros_like(c_ref)

        def process(rs, kb, mask):
            keys = pl.ds(pl.multiple_of(kb * SBLK, SBLK), SBLK)
            csums = [cs[h, rs, :] for h in range(hp)]
            _, _, wgts, lns = _sb_scores([q_ref[rs, hs] for hs in heads], [k_ref[keys, hs] for hs in heads], mask, csums, after)
            pvs = _each(lambda w, hs: _dot(w, v_ref[keys, hs]), wgts, heads)
            tots = _each(lambda ln: jnp.sum(ln, axis=1, keepdims=True), lns)
            for h, hs in enumerate(heads):
                acc[h, rs, :] += pvs[h]
                c_ref[rs, hs] = jnp.where(lane[rs, :] == kb, csums[h], c_ref[rs, hs])
                cs[h, rs, :] = csums[h] + tots[h]

        for j in reversed(range(nb)):
            process(slice(j * SBLK, tq), qb * nb + j, _band_mask(tq - j * SBLK, j, j * SBLK))

        def step(it, carry):
            process(slice(0, tq), qb * nb - 1 - it, None)
            return carry

        lax.fori_loop(0, qb * nb, step, 0)
        for h, hs in enumerate(heads):
            o_ref[:, hs] = acc[h].astype(BF16)

    qspec = pl.BlockSpec((tq, hp * HEAD), lambda h, i: (i, h))
    kspec = pl.BlockSpec((s, hp * HEAD), lambda h, i: (0, h))
    return pl.pallas_call(
        body, name=name, grid=(nh // hp, s // tq), in_specs=[qspec, kspec, kspec], out_specs=[qspec, qspec],
        out_shape=[jax.ShapeDtypeStruct((s, d), BF16), jax.ShapeDtypeStruct((s, d), F32)],
        scratch_shapes=[pltpu.VMEM((hp, tq, HEAD), F32), pltpu.VMEM((hp, tq, 1), F32)],
        compiler_params=_params(("parallel", "arbitrary")))(q, k, v)


def _sb_bwd(q, k, v, do, ctab, name):
    s, d = q.shape
    nh = d // HEAD
    tq = min(SB_TQ, s)
    nb = tq // SBLK

    hp = SB_HP
    heads = [slice(h * HEAD, (h + 1) * HEAD) for h in range(hp)]

    def body(q_ref, k_ref, v_ref, do_ref, c_ref, dq_ref, dk_ref, dv_ref, ps):
        qb = pl.program_id(1)

        @pl.when(qb == 0)
        def _():
            dk_ref[...] = jnp.zeros_like(dk_ref)
            dv_ref[...] = jnp.zeros_like(dv_ref)

        dq_ref[...] = jnp.zeros_like(dq_ref)
        ps[...] = jnp.zeros_like(ps)
        lane = lax.broadcasted_iota(jnp.int32, (tq, HEAD), 1)
        after = _tri01(lambda j, t: j > t)
        before = _tri01(lambda j, t: j < t)

        def process(rs, kb, mask):
            keys = pl.ds(pl.multiple_of(kb * SBLK, SBLK), SBLK)
            kblks = [k_ref[keys, hs] for hs in heads]
            qts = [q_ref[rs, hs] for hs in heads]
            dots = [do_ref[rs, hs] for hs in heads]
            csums = [_lane_col(c_ref[rs, hs], lane[rs, :], kb) for hs in heads]
            zs, es, wgts, _ = _sb_scores(qts, kblks, mask, csums, after)
            dlws = _each(lambda dt, hs, w: _dot(dt, v_ref[keys, hs], NT) * w, dots, heads, wgts)
            pts = _each(lambda dlw: _dot_hilo(dlw, before), dlws)
            pfxs = [ps[h, rs, :] for h in range(hp)]
            rs_ = _each(lambda e: 1.0 / (1.0 + e), es)
            sigs = _each(lambda z, e, r: jnp.where(z >= 0.0, r, e * r), zs, es, rs_)
            dzs = _each(lambda dlw, sig, pfx, pt: dlw * (1.0 - sig) - sig * (pfx + pt), dlws, sigs, pfxs, pts)
            tots = _each(lambda dlw: jnp.sum(dlw, axis=1, keepdims=True), dlws)
            if mask is not None:
                dzs = _each(lambda dz: jnp.where(mask, dz, 0.0), dzs)
            dqs = _each(lambda dz, kb_: _dot(dz, kb_), dzs, kblks)
            dks = _each(lambda dz, qt: _dot(dz, qt, TN), dzs, qts)
            dvs = _each(lambda w, dt: _dot(w, dt, TN), wgts, dots)
            for h, hs in enumerate(heads):
                dq_ref[rs, hs] += dqs[h]
                dk_ref[keys, hs] += dks[h]
                dv_ref[keys, hs] += dvs[h]
                ps[h, rs, :] = pfxs[h] + tots[h]

        def step(kb, carry):
            process(slice(0, tq), kb, None)
            return carry

        lax.fori_loop(0, qb * nb, step, 0)
        for j in range(nb):
            process(slice(j * SBLK, tq), qb * nb + j, _band_mask(tq - j * SBLK, j, j * SBLK))

    qspec = pl.BlockSpec((tq, hp * HEAD), lambda h, i: (i, h))
    kspec = pl.BlockSpec((s, hp * HEAD), lambda h, i: (0, h))
    sds = jax.ShapeDtypeStruct((s, d), F32)
    return pl.pallas_call(
        body, name=name, grid=(nh // hp, s // tq), in_specs=[qspec, kspec, kspec, qspec, qspec],
        out_specs=[qspec, kspec, kspec], out_shape=[sds, sds, sds],
        scratch_shapes=[pltpu.VMEM((hp, tq, 1), F32)],
        compiler_params=_params(("parallel", "arbitrary")))(q, k, v, do, ctab)


def _my_index():
    return 4 * lax.axis_index("x") + 2 * lax.axis_index("y") + lax.axis_index("c")


def _all_gather(x_shard, name):
    m_per, n = x_shard.shape

    def body(x_ref, out_ref, send_sems, recv_sems, local_sem):
        x, y, c = lax.axis_index("x"), lax.axis_index("y"), lax.axis_index("c")
        me, sibling = (x, y, c), (x, y, 1 - c)
        chips = [(1 - x, y), (x, 1 - y), (1 - x, 1 - y)]

        def rows(px, py, pc):
            return out_ref.at[pl.ds((4 * px + 2 * py + pc) * m_per, m_per), :]

        def copy(k, block, to, src=None):
            return pltpu.make_async_remote_copy(
                src_ref=rows(*block) if src is None else src, dst_ref=rows(*block),
                send_sem=send_sems.at[k], recv_sem=recv_sems.at[k], device_id=to, device_id_type=MESH)

        mine = pltpu.make_async_copy(x_ref, rows(*me), local_sem)
        mine.start()
        first = [copy(0, me, sibling, src=x_ref)]
        first += [copy(1 + j, me, (*chip, c), src=x_ref) for j, chip in enumerate(chips)]
        for cp in first:
            cp.start()
        passed = [copy(4 + j, (*chip, c), sibling) for j, chip in enumerate(chips)]
        for j, chip in enumerate(chips):
            copy(1 + j, (*chip, c), me).wait_recv()
            passed[j].start()
        copy(0, sibling, me).wait_recv()
        for j, chip in enumerate(chips):
            copy(4 + j, (*chip, 1 - c), me).wait_recv()
        for cp in first + passed:
            cp.wait_send()
        mine.wait()

    return pl.pallas_call(
        body, name=name, out_shape=jax.ShapeDtypeStruct((NDEV * m_per, n), x_shard.dtype),
        in_specs=[pl.BlockSpec(memory_space=pl.ANY)], out_specs=pl.BlockSpec(memory_space=pl.ANY),
        scratch_shapes=[pltpu.SemaphoreType.DMA((7,)), pltpu.SemaphoreType.DMA((7,)), pltpu.SemaphoreType.DMA],
    )(x_shard)


HBM_SPEC = pl.BlockSpec(memory_space=pltpu.HBM)
SEM_SPEC = pl.BlockSpec(memory_space=pltpu.SEMAPHORE)
ANY_SPEC = pl.BlockSpec(memory_space=pl.ANY)
EFFECT = pltpu.SideEffectType.DATAFLOW_SIDE_EFFECTING


def _exchange_copies(src_refs, land_refs, send_sems, recv_sems, self_sems, scatter):
    x, y, c = lax.axis_index("x"), lax.axis_index("y"), lax.axis_index("c")
    me = 4 * x + 2 * y + c
    remote, local = [], []
    for p, (src_ref, land_ref) in enumerate(zip(src_refs, land_refs)):
        rows = land_ref.shape[0] // NDEV

        def part(idx):
            return src_ref.at[pl.ds(idx * rows, rows), :] if scatter else src_ref

        slot = land_ref.at[pl.ds(me * rows, rows), :]
        for k in range(1, NDEV):
            px, py, pc = x ^ ((k >> 2) & 1), y ^ ((k >> 1) & 1), c ^ (k & 1)
            remote.append(pltpu.make_async_remote_copy(
                src_ref=part(4 * px + 2 * py + pc), dst_ref=slot, send_sem=send_sems.at[7 * p + k - 1],
                recv_sem=recv_sems.at[7 * p + k - 1], device_id=(px, py, pc), device_id_type=MESH))
        local.append(pltpu.make_async_copy(part(me), slot, self_sems.at[p]))
    return remote, local


def _send_start(srcs, scatter, after, name):
    n = len(srcs)
    lands = []
    for s in srcs:
        rows = s.shape[0] if scatter else NDEV * s.shape[0]
        lands.append(pltpu.with_memory_space_constraint(lax.empty((rows, s.shape[1]), s.dtype), pltpu.HBM))

    def body(*refs):
        src_refs, land_refs = refs[:n], refs[n:2 * n]
        send_sems, recv_sems, self_sems = refs[2 * n + 1:2 * n + 4]
        remote, local = _exchange_copies(src_refs, land_refs, send_sems, recv_sems, self_sems, scatter)
        for cp in remote + local:
            cp.start()
        refs[-1][...] = jnp.zeros_like(refs[-1])

    hbm = lambda a: pltpu.HBM(a.shape, a.dtype)
    out = pl.pallas_call(
        body, name=name,
        out_shape=(pltpu.SemaphoreType.DMA((7 * n,)), pltpu.SemaphoreType.DMA((7 * n,)), pltpu.SemaphoreType.DMA((n,)),
                   *[hbm(s) for s in srcs], *[hbm(a) for a in lands], jax.ShapeDtypeStruct((8, HEAD), F32)),
        in_specs=(HBM_SPEC,) * (2 * n) + (ANY_SPEC,),
        out_specs=(SEM_SPEC,) * 3 + (HBM_SPEC,) * (2 * n) + (pl.BlockSpec(memory_space=pltpu.VMEM),),
        input_output_aliases={i: 3 + i for i in range(2 * n)},
        compiler_params=pltpu.CompilerParams(has_side_effects=EFFECT),
    )(*[pltpu.with_memory_space_constraint(s, pltpu.HBM) for s in srcs], *lands, after)
    return dict(sems=out[:3], srcs=out[3:3 + n], lands=out[3 + n:3 + 2 * n], token=out[-1])


def _send_wait(started, scatter, after, name):
    srcs, lands = started["srcs"], started["lands"]
    n = len(srcs)

    def body(*refs):
        src_refs, land_refs = refs[:n], refs[n:2 * n]
        send_sems, recv_sems, self_sems = refs[2 * n:2 * n + 3]
        remote, local = _exchange_copies(src_refs, land_refs, send_sems, recv_sems, self_sems, scatter)
        for cp in remote:
            cp.wait_send()
            cp.wait_recv()
        for cp in local:
            cp.wait()

    hbm = lambda a: pltpu.HBM(a.shape, a.dtype)
    out = pl.pallas_call(
        body, name=name, out_shape=(*[hbm(s) for s in srcs], *[hbm(a) for a in lands]),
        in_specs=(HBM_SPEC,) * (2 * n) + (SEM_SPEC,) * 3 + (ANY_SPEC,), out_specs=(HBM_SPEC,) * (2 * n),
        input_output_aliases={i: i for i in range(2 * n)},
        compiler_params=pltpu.CompilerParams(has_side_effects=EFFECT),
    )(*srcs, *lands, *started["sems"], after)
    return out[n:]


def _sum_slots(x, name):
    _, r, c = x.shape
    tr = _pick(r, (512, 256, 128, 64, 32, 16, 8))

    def body(x_ref, o_ref):
        acc = x_ref[0].astype(F32)
        for i in range(1, NDEV):
            acc = acc + x_ref[i].astype(F32)
        o_ref[...] = acc

    return pl.pallas_call(
        body, name=name, grid=(r // tr,), in_specs=[pl.BlockSpec((NDEV, tr, c), lambda i: (0, i, 0))],
        out_specs=pl.BlockSpec((tr, c), lambda i: (i, 0)), out_shape=jax.ShapeDtypeStruct((r, c), F32),
        compiler_params=_params(("parallel",)))(x)


def _adamw(w, g, m, v, name):
    if w.ndim == 3:
        nl, r, c = w.shape
        tc = _pick(c, (256, 128))
        grid = (nl, c // tc)
        blk = pl.BlockSpec((None, r, tc), lambda i, j: (i, 0, j))
        sem = ("parallel", "parallel")
    else:
        r, c = w.shape
        tr = _pick(r, (256, 128, 64, 32, 16, 8))
        grid = (r // tr,)
        blk = pl.BlockSpec((tr, c), lambda i: (i, 0))
        sem = ("parallel",)
    c1 = 1.0 - B1 ** STEP
    c2 = 1.0 - B2 ** STEP

    def body(w_ref, g_ref, m_ref, v_ref, d_ref, nm_ref, nv_ref):
        gv = g_ref[...]
        nm = B1 * m_ref[...] + (1.0 - B1) * gv
        nv = B2 * v_ref[...] + (1.0 - B2) * (gv * gv)
        d_ref[...] = -LR * ((nm / c1) / (jnp.sqrt(nv / c2) + ADAM_EPS) + WD * w_ref[...])
        nm_ref[...] = nm
        nv_ref[...] = nv

    sds = jax.ShapeDtypeStruct(w.shape, F32)
    return pl.pallas_call(
        body, name=name, grid=grid, in_specs=[blk] * 4, out_specs=[blk] * 3, out_shape=[sds] * 3,
        compiler_params=_params(sem))(w, g, m, v)


def _pad_rows(a, mult):
    r = a.shape[0]
    pad = (-r) % mult
    return a if pad == 0 else jnp.pad(a, ((0, pad), (0, 0)))


def _pad_lanes(v, width=HEAD):
    return jnp.pad(v.reshape(1, -1), ((0, 0), (0, width - v.shape[-1])))


def kernel(x, p, ln_mix, ln_ffn, ln_ple, gdn_w_in, gdn_conv, gdn_a_log, gdn_dt_bias, gdn_norm, gdn_w_out, kv_norm, w_kv, k_norm, sb_w_q, sb_q_norm, sb_w_out, ffn_w_in, ffn_w_out, ple_w_proj, ple_w_gate, loss_target, m_ln_mix, m_ln_ffn, m_ln_ple, m_gdn_w_in, m_gdn_conv, m_gdn_a_log, m_gdn_dt_bias, m_gdn_norm, m_gdn_w_out, m_kv_norm, m_w_kv, m_k_norm, m_sb_w_q, m_sb_q_norm, m_sb_w_out, m_ffn_w_in, m_ffn_w_out, m_ple_w_proj, m_ple_w_gate, v_ln_mix, v_ln_ffn, v_ln_ple, v_gdn_w_in, v_gdn_conv, v_gdn_a_log, v_gdn_dt_bias, v_gdn_norm, v_gdn_w_out, v_kv_norm, v_w_kv, v_k_norm, v_sb_w_q, v_sb_q_norm, v_sb_w_out, v_ffn_w_in, v_ffn_w_out, v_ple_w_proj, v_ple_w_gate):
    s, d = x.shape[1], x.shape[2]
    nh = d // HEAD
    depth = ln_mix.shape[0]
    n_a = gdn_w_in.shape[0]
    n_b = sb_w_q.shape[0]
    me = _my_index()
    win_cols = gdn_w_in.shape[2]
    win_rows = 4 * d + 2 * nh

    def col_t(w):
        return jnp.transpose(w).astype(BF16)

    local = {}
    for l in range(n_a):
        local[("gdn_w_in", l)] = col_t(gdn_w_in[l])
        local[("gdn_w_out", l)] = gdn_w_out[l].astype(BF16)
    local[("w_kv", 0)] = col_t(w_kv)
    for j in range(n_b):
        local[("sb_w_q", j)] = sb_w_q[j].astype(BF16)
        local[("sb_w_out", j)] = sb_w_out[j].astype(BF16)
    for l in range(depth):
        local[("ffn_w_in", l)] = col_t(ffn_w_in[l])
        local[("ffn_w_out", l)] = ffn_w_out[l].astype(BF16)
        local[("ple_w_proj", l)] = col_t(ple_w_proj[l]).reshape(-1, d)
        local[("ple_w_gate", l)] = ple_w_gate[l].astype(BF16)
    local = {key: _pad_rows(a, 16) for key, a in local.items()}

    chunks = []
    for l in range(depth):
        mix = [("gdn_w_in", l), ("gdn_w_out", l)] if l < n_a else [("sb_w_q", l - n_a), ("sb_w_out", l - n_a)]
        rest = [("ffn_w_in", l), ("ffn_w_out", l), ("ple_w_proj", l), ("ple_w_gate", l)]
        if l == n_a - 1:
            rest.append(("w_kv", 0))
        chunks += [(f"a{l}", mix), (f"f{l}", rest)]
    chunk_keys = dict(chunks)

    conv_rows = n_a * gdn_conv.shape[1]
    conv_sh = _pad_rows(gdn_conv.reshape(conv_rows, -1), 8)
    conv_g = _all_gather(conv_sh, "comm_gather_conv")
    token = conv_g
    conv_g = conv_g.reshape(NDEV, conv_sh.shape[0], -1)
    conv_full = jnp.transpose(conv_g[:, :conv_rows, :], (1, 0, 2)).reshape(n_a, gdn_conv.shape[1], 3 * d)

    w_started = {}
    for name, keys in chunks:
        w_started[name] = _send_start([local[k] for k in keys], False, token, f"comm_wstart_{name}")
        token = w_started[name]["token"]

    full = {}

    def fetch(name, after):
        lands = _send_wait(w_started[name], False, after, f"comm_wwait_{name}")
        for key, land in zip(chunk_keys[name], lands):
            full[key] = land

    def whole(key, valid=None):
        a = full[key]
        if valid is not None:
            a = a.reshape(NDEV, -1, d)[:, :valid, :].reshape(-1, d)
        return a

    pd = p.shape[-1]
    w_in_t, w_ab_t, w_gout, w_q, w_sout, wf_t, w_fout, wp_t, w_pg = {}, {}, {}, {}, {}, {}, {}, {}, {}
    wkv_t = None

    h = x[0]
    sv = []
    kv_sv = None
    k_sh = v_sh = None
    for l in range(depth):
        t = {}
        t["h0"] = h
        if l == 0:
            hn = _rms_fwd(h, ln_mix[l], f"rms_mix_{l}")
        t["hn"] = hn
        fetch(f"a{l}", token if l == 0 else hn)
        if l < n_a:
            wt = whole(("gdn_w_in", l), win_cols)
            w_in_t[l] = wt[:4 * d]
            w_ab_t[l] = jnp.pad(wt[4 * d:], ((0, HEAD - 2 * nh), (0, 0)))
            w_gout[l] = whole(("gdn_w_out", l))
        else:
            w_q[l - n_a] = whole(("sb_w_q", l - n_a))
            w_sout[l - n_a] = whole(("sb_w_out", l - n_a))
        if l < n_a:
            proj = _mm(hn, w_in_t[l], "nt", f"gdn_proj_{l}")
            pab = _mm(hn, w_ab_t[l], "nt", f"gdn_proj_ab_{l}")
            qkv = _conv_fwd(proj, conv_full[l], d, f"gdn_conv_{l}")
            al, dtb = _pad_lanes(gdn_a_log[l]), _pad_lanes(gdn_dt_bias[l])
            gb = _gates_fwd(pab, al, dtb, nh, f"gdn_gates_{l}")
            o_raw, states = _gdn_fwd(qkv, gb, nh, f"gdn_rule_{l}")
            o2 = _headnorm_fwd(o_raw, gdn_norm[l], f"gdn_outnorm_{l}", gate=proj, gate_col0=3 * d, head_major=True)
            h, hn2 = _mm(o2, w_gout[l], "nn", f"gdn_out_{l}", res=h, norm_g=ln_ffn[l])
            t.update(proj=proj, pab=pab, qkv=qkv, gb=gb, o_raw=o_raw, states=states, o2=o2, al=al, dtb=dtb)
        else:
            j = l - n_a
            qpre = _mm(hn, w_q[j], "nn", f"sb_qproj_{j}")
            qn = _headnorm_fwd(qpre, sb_q_norm[j], f"sb_qnorm_{j}", scale=HEAD ** -0.5)
            o, ctab = _sb_fwd(qn, k_sh, v_sh, f"sb_attn_{j}")
            h, hn2 = _mm(o, w_sout[j], "nn", f"sb_out_{j}", res=h, norm_g=ln_ffn[l])
            t.update(qpre=qpre, qn=qn, o=o, ctab=ctab)
        t["h1"] = h
        fetch(f"f{l}", hn2)
        wf_t[l] = whole(("ffn_w_in", l))
        w_fout[l] = whole(("ffn_w_out", l))
        wp_t[l] = full[("ple_w_proj", l)].reshape(d, pd)
        w_pg[l] = whole(("ple_w_gate", l))
        if l == n_a - 1:
            wkv_t = whole(("w_kv", 0))
        act, gs, us = _swiglu_fwd(hn2, wf_t[l], f"ffn_in_{l}")
        h, hn3 = _mm(act, w_fout[l], "nn", f"ffn_out_{l}", res=h, norm_g=ln_ple[l])
        t.update(hn2=hn2, act=act, gs=gs, us=us, h2=h)
        gains = ([ln_mix[l + 1]] if l + 1 < depth else []) + ([kv_norm] if l == n_a - 1 else [])
        h, gpre, pp, *normed = _ple_fwd(h, hn3, p[l, 0], w_pg[l], wp_t[l], f"ple_{l}", norm_gs=gains)
        if l + 1 < depth:
            hn = normed[0]
        t.update(hn3=hn3, gpre=gpre, pp=pp)
        sv.append(t)
        if l == n_a - 1:
            kvn = normed[-1]
            kv = _mm(kvn, wkv_t, "nt", "kv_proj")
            k_sh = _headnorm_fwd(kv, k_norm, "k_norm", width=d)
            v_sh = kv[:, d:].astype(BF16)
            kv_sv = dict(h=h, kvn=kvn, kv=kv)

    dh, loss_vec = _loss_fwd_bwd(h, loss_target[0], "loss")
    loss = lax.psum(jnp.sum(loss_vec), ("x", "y", "c"))

    gw = {}
    small = {}
    g_started = {}

    def scatter_start(name):
        gparts = []
        for key in chunk_keys[name]:
            g = gw[key]
            g = g.reshape(NDEV, -1, d) if key[0] == "ple_w_proj" else g.reshape(NDEV, -1, g.shape[-1])
            padr = local[key].shape[0] - g.shape[1]
            if padr:
                g = jnp.pad(g, ((0, 0), (0, padr), (0, 0)))
            gparts.append(g.reshape(-1, d))
        g_started[name] = _send_start(gparts, True, gparts[0], f"comm_gstart_{name}")
        return g_started[name]["token"]

    dk_sh = jnp.zeros((s, d), F32)
    dv_sh = jnp.zeros((s, d), F32)
    for l in reversed(range(depth)):
        t = sv[l]
        if l == n_a - 1:
            dkv_k, dkn = _headnorm_bwd(dk_sh, kv_sv["kv"], k_norm, "k_norm_bwd", dx_dtype=BF16)
            dkv = jnp.concatenate([dkv_k, dv_sh.astype(BF16)], axis=1)
            gw[("w_kv", 0)] = _mm(dkv, kv_sv["kvn"], "tn", "kv_dw", out_dtype=BF16)
            dh, _, dg = _mm(dkv, wkv_t, "nn", "kv_dx", norm_bwd=(kv_sv["h"], kv_norm, dh))
            small["kv_norm"] = dg
            small["k_norm"] = dkn
        dgp, dpp = _ple_bwd(dh, t["gpre"], t["pp"], f"ple_bwd_{l}")
        gw[("ple_w_gate", l)] = _mm(t["hn3"], dgp, "tn", f"ple_dwg_{l}", out_dtype=BF16)
        gw[("ple_w_proj", l)] = _mm(dpp, p[l, 0], "tn", f"ple_dwp_{l}", out_dtype=BF16)
        dh, dhb, dg = _mm(dgp, w_pg[l], "nt", f"ple_dx_{l}", norm_bwd=(t["h2"], ln_ple[l], dh))
        small[("ln_ple", l)] = dg
        dgs, dus = _swiglu_bwd(dhb, w_fout[l], t["gs"], t["us"], f"ffn_bwd_act_{l}")
        gw[("ffn_w_out", l)] = _mm(t["act"], dhb, "tn", f"ffn_dwo_{l}", out_dtype=BF16)
        f = dgs.shape[1]
        dwg = _mm(dgs, t["hn2"], "tn", f"ffn_dwg_{l}", out_dtype=BF16)
        dwu = _mm(dus, t["hn2"], "tn", f"ffn_dwu_{l}", out_dtype=BF16)
        gw[("ffn_w_in", l)] = jnp.concatenate([dwg, dwu], axis=0)
        dhn2 = _mm(dgs, wf_t[l][:f], "nn", f"ffn_dxg_{l}")
        dh, dhb, dg = _mm(dus, wf_t[l][f:], "nn", f"ffn_dxu_{l}", res=dhn2, norm_bwd=(t["h1"], ln_ffn[l], dh),
                          after=scatter_start(f"f{l}"))
        small[("ln_ffn", l)] = dg
        if l < n_a:
            do2 = _mm(dhb, w_gout[l], "nt", f"gdn_out_dx_{l}")
            gw[("gdn_w_out", l)] = _mm(t["o2"], dhb, "tn", f"gdn_out_dw_{l}", out_dtype=BF16)
            do_raw, dgn, dgate = _headnorm_bwd(do2, t["o_raw"], gdn_norm[l], f"gdn_outnorm_bwd_{l}",
                                               gate=t["proj"], gate_col0=3 * d, head_major=True)
            small[("gdn_norm", l)] = dgn
            dqkv, dgb = _gdn_bwd(t["qkv"], t["gb"], do_raw, t["states"], nh, f"gdn_rule_bwd_{l}")
            dpab, dal, ddt = _gates_bwd(dgb, t["pab"], t["al"], t["dtb"], nh, f"gdn_gates_bwd_{l}")
            small[("gdn_a_log", l)] = dal
            small[("gdn_dt_bias", l)] = ddt
            dproj_qkv, dconv = _conv_bwd(dqkv, t["proj"], conv_full[l], d, f"gdn_conv_bwd_{l}")
            small[("gdn_conv", l)] = dconv
            dproj = jnp.concatenate([dproj_qkv, dgate], axis=1)
            dw_main = _mm(dproj, t["hn"], "tn", f"gdn_proj_dw_{l}", out_dtype=BF16)
            dw_ab = _mm(dpab, t["hn"], "tn", f"gdn_proj_ab_dw_{l}", out_dtype=BF16)
            gw[("gdn_w_in", l)] = jnp.concatenate([dw_main, dw_ab[:16]], axis=0)[:win_rows]
            dhn_ab = _mm(dpab, w_ab_t[l], "nn", f"gdn_proj_ab_dx_{l}")
            last = dict(a=dproj, b=w_in_t[l], mode="nn", name=f"gdn_proj_dx_{l}", res=dhn_ab)
        else:
            j = l - n_a
            do = _mm(dhb, w_sout[j], "nt", f"sb_out_dx_{j}", out_dtype=BF16)
            gw[("sb_w_out", j)] = _mm(t["o"], dhb, "tn", f"sb_out_dw_{j}", out_dtype=BF16)
            dq, dk, dv = _sb_bwd(t["qn"], k_sh, v_sh, do, t["ctab"], f"sb_attn_bwd_{j}")
            dk_sh = dk_sh + dk
            dv_sh = dv_sh + dv
            dqpre, dqn = _headnorm_bwd(dq, t["qpre"], sb_q_norm[j], f"sb_qnorm_bwd_{j}", scale=HEAD ** -0.5, dx_dtype=BF16)
            small[("sb_q_norm", j)] = dqn
            gw[("sb_w_q", j)] = _mm(t["hn"], dqpre, "tn", f"sb_q_dw_{j}", out_dtype=BF16)
            last = dict(a=dqpre, b=w_q[j], mode="nt", name=f"sb_q_dx_{j}")
        dh, _, dg = _mm(**last, norm_bwd=(t["h0"], ln_mix[l], dh), after=scatter_start(f"a{l}"))
        small[("ln_mix", l)] = dg
    grad_x = dh[None]

    gshard = {}
    for name, keys in reversed(chunks):
        lands = _send_wait(g_started[name], True, dh, f"comm_gwait_{name}")
        for key, land in zip(keys, lands):
            gshard[key] = _sum_slots(land.reshape(NDEV, -1, d), f"grad_sum_{key[0]}_{key[1]}")

    def col_back(key, n_valid):
        return jnp.transpose(gshard[key][:n_valid])

    gt_gdn_w_in = jnp.stack([gshard[("gdn_w_in", l)][:win_cols] for l in range(n_a)])
    gt_ffn_w_in = jnp.stack([gshard[("ffn_w_in", l)][:ffn_w_in.shape[2]] for l in range(depth)])
    g_gdn_w_in = jnp.transpose(gt_gdn_w_in, (0, 2, 1))
    g_gdn_w_out = jnp.stack([gshard[("gdn_w_out", l)] for l in range(n_a)])
    g_w_kv = col_back(("w_kv", 0), w_kv.shape[1])
    g_sb_w_q = jnp.stack([gshard[("sb_w_q", j)] for j in range(n_b)])
    g_sb_w_out = jnp.stack([gshard[("sb_w_out", j)] for j in range(n_b)])
    g_ffn_w_in = jnp.transpose(gt_ffn_w_in, (0, 2, 1))
    g_ffn_w_out = jnp.stack([gshard[("ffn_w_out", l)] for l in range(depth)])
    g_ple_w_proj = jnp.stack([jnp.transpose(gshard[("ple_w_proj", l)].reshape(-1, pd)) for l in range(depth)])
    g_ple_w_gate = jnp.stack([gshard[("ple_w_gate", l)] for l in range(depth)])

    def vec_rows(v):
        return v.reshape(-1, HEAD)

    small_items = []
    for name_, cnt in (("ln_mix", depth), ("ln_ffn", depth), ("ln_ple", depth)):
        for l in range(cnt):
            small_items.append(((name_, l), vec_rows(small[(name_, l)])))
    for l in range(n_a):
        small_items.append((("gdn_conv", l), small[("gdn_conv", l)].reshape(-1, HEAD)))
        small_items.append((("gdn_a_log", l), small[("gdn_a_log", l)]))
        small_items.append((("gdn_dt_bias", l), small[("gdn_dt_bias", l)]))
        small_items.append((("gdn_norm", l), small[("gdn_norm", l)]))
    small_items.append(("kv_norm", vec_rows(small["kv_norm"])))
    small_items.append(("k_norm", small["k_norm"]))
    for j in range(n_b):
        small_items.append((("sb_q_norm", j), small[("sb_q_norm", j)]))
    spack = jnp.concatenate([_pad_rows(a, 8) for _, a in small_items], axis=0)
    sg = _all_gather(spack, "comm_gather_small").reshape(NDEV, spack.shape[0], HEAD)
    ssum = _sum_slots(sg, "small_sum")
    sm = {}
    off = 0
    for key, a in small_items:
        sm[key] = ssum[off:off + a.shape[0]]
        off += a.shape[0] + (-a.shape[0]) % 8

    g_ln_mix = jnp.stack([sm[("ln_mix", l)].reshape(d) for l in range(depth)])
    g_ln_ffn = jnp.stack([sm[("ln_ffn", l)].reshape(d) for l in range(depth)])
    g_ln_ple = jnp.stack([sm[("ln_ple", l)].reshape(d) for l in range(depth)])
    conv_loc = gdn_conv.shape[2]
    g_conv_full = jnp.stack([sm[("gdn_conv", l)].reshape(gdn_conv.shape[1], 3 * d) for l in range(n_a)])
    g_gdn_conv = lax.dynamic_slice_in_dim(g_conv_full, me * conv_loc, conv_loc, axis=2)
    g_a_log = jnp.stack([sm[("gdn_a_log", l)][0, :nh] for l in range(n_a)])
    g_dt_bias = jnp.stack([sm[("gdn_dt_bias", l)][0, :nh] for l in range(n_a)])
    g_gdn_norm = jnp.stack([sm[("gdn_norm", l)][0] for l in range(n_a)])
    g_kv_norm = sm["kv_norm"].reshape(d)
    g_k_norm = sm["k_norm"][0]
    g_sb_q_norm = jnp.stack([sm[("sb_q_norm", j)][0] for j in range(n_b)])

    grads = [g_ln_mix, g_ln_ffn, g_ln_ple, g_gdn_w_in, g_gdn_conv, g_a_log, g_dt_bias, g_gdn_norm, g_gdn_w_out,
             g_kv_norm, g_w_kv, g_k_norm, g_sb_w_q, g_sb_q_norm, g_sb_w_out, g_ffn_w_in, g_ffn_w_out, g_ple_w_proj,
             g_ple_w_gate]
    weights = [ln_mix, ln_ffn, ln_ple, gdn_w_in, gdn_conv, gdn_a_log, gdn_dt_bias, gdn_norm, gdn_w_out, kv_norm, w_kv,
               k_norm, sb_w_q, sb_q_norm, sb_w_out, ffn_w_in, ffn_w_out, ple_w_proj, ple_w_gate]
    moms = [m_ln_mix, m_ln_ffn, m_ln_ple, m_gdn_w_in, m_gdn_conv, m_gdn_a_log, m_gdn_dt_bias, m_gdn_norm, m_gdn_w_out,
            m_kv_norm, m_w_kv, m_k_norm, m_sb_w_q, m_sb_q_norm, m_sb_w_out, m_ffn_w_in, m_ffn_w_out, m_ple_w_proj,
            m_ple_w_gate]
    vels = [v_ln_mix, v_ln_ffn, v_ln_ple, v_gdn_w_in, v_gdn_conv, v_gdn_a_log, v_gdn_dt_bias, v_gdn_norm, v_gdn_w_out,
            v_kv_norm, v_w_kv, v_k_norm, v_sb_w_q, v_sb_q_norm, v_sb_w_out, v_ffn_w_in, v_ffn_w_out, v_ple_w_proj,
            v_ple_w_gate]

    deltas, new_m, new_v = [], [], []
    small_idx = [i for i, w in enumerate(weights) if w.size < 8 * HEAD * 16]
    transposed = {3: gt_gdn_w_in, 15: gt_ffn_w_in}
    for i, (w, g, m, v) in enumerate(zip(weights, grads, moms, vels)):
        if i in small_idx:
            deltas.append(None), new_m.append(None), new_v.append(None)
            continue
        if i in transposed:
            tr = lambda a: jnp.transpose(a, (0, 2, 1))
            dl, nm, nv = _adamw(tr(w), transposed[i], tr(m), tr(v), f"adamw_{i}")
            deltas.append(tr(dl)), new_m.append(tr(nm)), new_v.append(tr(nv))
            continue
        shp = w.shape
        two = lambda a: a.reshape(-1, shp[-1])
        dl, nm, nv = _adamw(two(w), two(g), two(m), two(v), f"adamw_{i}")
        deltas.append(dl.reshape(shp)), new_m.append(nm.reshape(shp)), new_v.append(nv.reshape(shp))

    def flat_pack(arrs):
        flat = jnp.concatenate([a.reshape(-1) for a in arrs])
        pad = (-flat.shape[0]) % (8 * HEAD)
        return jnp.pad(flat, (0, pad)).reshape(-1, HEAD)

    sw = flat_pack([weights[i] for i in small_idx])
    sgr = flat_pack([grads[i] for i in small_idx])
    smo = flat_pack([moms[i] for i in small_idx])
    sve = flat_pack([vels[i] for i in small_idx])
    sdl, snm, snv = _adamw(sw, sgr, smo, sve, "adamw_small")
    off = 0
    for i in small_idx:
        n = weights[i].size
        shp = weights[i].shape
        deltas[i] = sdl.reshape(-1)[off:off + n].reshape(shp)
        new_m[i] = snm.reshape(-1)[off:off + n].reshape(shp)
        new_v[i] = snv.reshape(-1)[off:off + n].reshape(shp)
        off += n

    return (loss, grad_x, *grads, *deltas, *new_m, *new_v)
```

```python
import math

import jax
import jax.numpy as jnp
from jax import lax
from jax.experimental import pallas as pl
from jax.experimental.pallas import tpu as pltpu

F32 = jnp.float32
BF16 = jnp.bfloat16
NDEV = 8
HEAD = 128
CHUNK = 64
SBLK = 256
EPS = 1e-6
LR, B1, B2, ADAM_EPS, WD, STEP = 0.001, 0.9, 0.999, 1e-08, 0.01, 10
NEG = -1e30
MM_VMEM_BUDGET = 40 * 1024 * 1024

NN = (((1,), (0,)), ((), ()))
NT = (((1,), (1,)), ((), ()))
TN = (((0,), (0,)), ((), ()))
BNN = (((2,), (1,)), ((0,), (0,)))
BNT = (((2,), (2,)), ((0,), (0,)))
BTN = (((1,), (1,)), ((0,), (0,)))
MESH = pl.DeviceIdType.MESH


def _dot(a, b, dims=NN):
    return lax.dot_general(a.astype(BF16), b.astype(BF16), dims, preferred_element_type=F32)


def _dot_hilo(a, b01, dims=NN):
    hi = a.astype(BF16)
    lo = (a - hi.astype(F32)).astype(BF16)
    return (lax.dot_general(hi, b01, dims, preferred_element_type=F32)
            + lax.dot_general(lo, b01, dims, preferred_element_type=F32))


def _pick(dim, cands):
    for c in cands:
        if dim % c == 0:
            return c
    return dim


def _params(sem, vmem_mb=48):
    return pltpu.CompilerParams(dimension_semantics=sem, vmem_limit_bytes=vmem_mb * 1024 * 1024)


def _silu(x):
    return x * jax.nn.sigmoid(x)


def _silu_and_grad(x):
    s = jax.nn.sigmoid(x)
    xs = x * s
    return xs, s + xs * (1.0 - s)


def _mm(a, b, mode, name, out_dtype=F32, res=None, norm_g=None, norm_bwd=None, after=None, b_row0=0):
    if mode == "nn":
        (m, k), n = a.shape, b.shape[1]
    elif mode == "nt":
        (m, k), n = a.shape, b.shape[0]
    else:
        (k, m), n = a.shape, b.shape[1]
    rows = norm_g is not None or norm_bwd is not None
    tn = n if rows else _pick(n, (512, 256, 128))
    tk = k if k <= 4096 else _pick(k, (2048, 1024, 512, 256, 128))
    nk = k // tk
    out_b = jnp.dtype(out_dtype).itemsize + (res.dtype.itemsize if res is not None else 0)
    out_b += 2 if norm_g is not None else 0
    out_b += 10 if norm_bwd is not None else 0
    for tm in [t for t in range(min(m, 2048), 127, -128) if m % t == 0] + [m]:
        need = 2 * (tm * tk * a.dtype.itemsize + tk * tn * b.dtype.itemsize + tm * tn * out_b) + 4 * tm * tn
        if need <= MM_VMEM_BUDGET:
            break
    dims = {"nn": NN, "nt": NT, "tn": TN}[mode]
    if mode == "tn":
        a_spec = pl.BlockSpec((tk, tm), lambda i, j, kk: (kk, i))
    else:
        a_spec = pl.BlockSpec((tm, tk), lambda i, j, kk: (i, kk))
    k0 = b_row0 // tk
    if mode == "nt":
        b_spec = pl.BlockSpec((tn, tk), lambda i, j, kk: (j, kk))
    else:
        b_spec = pl.BlockSpec((tk, tn), lambda i, j, kk: (kk + k0, j))
    mn_spec = pl.BlockSpec((tm, tn), lambda i, j, kk: (i, j))
    vec_spec = pl.BlockSpec((1, tn), lambda i, j, kk: (0, j))
    has_res = res is not None
    n_in = 2 + has_res + (1 if norm_g is not None else 0) + (3 if norm_bwd is not None else 0) + (after is not None)

    def body(*refs):
        a_ref, b_ref = refs[:2]
        extra = list(refs[2:n_in])
        outs = refs[n_in:-1]
        acc = refs[-1]
        kk = pl.program_id(2)

        @pl.when(kk == 0)
        def _():
            acc[...] = jnp.zeros_like(acc)

        if norm_bwd is not None:
            @pl.when((kk == 0) & (pl.program_id(0) == 0))
            def _():
                outs[2][...] = jnp.zeros_like(outs[2])

        acc[...] += _dot(a_ref[...], b_ref[...], dims)

        @pl.when(kk == nk - 1)
        def _():
            r = acc[...]
            if has_res:
                r = r + extra.pop(0)[...].astype(F32)
            if norm_g is not None:
                outs[0][...] = r.astype(out_dtype)
                rs = lax.rsqrt(jnp.mean(r * r, axis=-1, keepdims=True) + EPS)
                outs[1][...] = (r * rs * extra.pop(0)[...]).astype(BF16)
            elif norm_bwd is not None:
                xv, gv, dres = extra.pop(0)[...], extra.pop(0)[...], extra.pop(0)[...]
                rs = lax.rsqrt(jnp.mean(xv * xv, axis=-1, keepdims=True) + EPS)
                gdy = r * gv
                dx = dres + rs * gdy - xv * (rs * rs * rs) * jnp.mean(xv * gdy, axis=-1, keepdims=True)
                outs[0][...] = dx
                outs[1][...] = dx.astype(BF16)
                outs[2][...] += jnp.sum(r * xv * rs, axis=0, keepdims=True)
            else:
                outs[0][...] = r.astype(out_dtype)

    ins = [a, b] + ([res] if has_res else [])
    in_specs = [a_spec, b_spec] + ([mn_spec] if has_res else [])
    out_specs, out_shape = [mn_spec], [jax.ShapeDtypeStruct((m, n), out_dtype)]
    sem = ("parallel", "parallel", "arbitrary")
    if norm_g is not None:
        ins.append(norm_g.reshape(1, n))
        in_specs.append(vec_spec)
        out_specs.append(mn_spec)
        out_shape.append(jax.ShapeDtypeStruct((m, n), BF16))
    if norm_bwd is not None:
        x, g, dres = norm_bwd
        ins += [x, g.reshape(1, n), dres]
        in_specs += [mn_spec, vec_spec, mn_spec]
        out_specs += [mn_spec, vec_spec]
        out_shape += [jax.ShapeDtypeStruct((m, n), BF16), jax.ShapeDtypeStruct((1, n), F32)]
        sem = ("arbitrary", "arbitrary", "arbitrary")
    if after is not None:
        ins.append(after)
        in_specs.append(pl.BlockSpec(memory_space=pl.ANY))
    out = pl.pallas_call(
        body, name=name, grid=(m // tm, n // tn, nk), in_specs=in_specs, out_specs=out_specs,
        out_shape=out_shape, scratch_shapes=[pltpu.VMEM((tm, tn), F32)],
        compiler_params=_params(sem))(*ins)
    return out[0] if len(out) == 1 else out


def _rms_fwd(h, g, name):
    s, d = h.shape
    tm = _pick(s, (512, 256, 128))

    def body(h_ref, g_ref, o_ref):
        x = h_ref[...]
        r = lax.rsqrt(jnp.mean(x * x, axis=-1, keepdims=True) + EPS)
        o_ref[...] = (x * r * g_ref[...]).astype(BF16)

    return pl.pallas_call(
        body, name=name, grid=(s // tm,),
        in_specs=[pl.BlockSpec((tm, d), lambda i: (i, 0)), pl.BlockSpec((1, d), lambda i: (0, 0))],
        out_specs=pl.BlockSpec((tm, d), lambda i: (i, 0)),
        out_shape=jax.ShapeDtypeStruct((s, d), BF16), compiler_params=_params(("parallel",)))(h, g.reshape(1, d))


def _headnorm_fwd(x, g, name, scale=1.0, gate=None, gate_col0=0, out_dtype=BF16, width=None, head_major=False):
    if head_major:
        s, d = x.shape[1], x.shape[0] * HEAD
    else:
        s, d = x.shape[0], (width or x.shape[1])
    nh = d // HEAD
    tm = _pick(s, (256, 128))
    has_gate = gate is not None
    gb = gate_col0 // d

    def body(*refs):
        if has_gate:
            x_ref, g_ref, gt_ref, o_ref = refs
        else:
            x_ref, g_ref, o_ref = refs
        gv = g_ref[...]
        for h in range(nh):
            sl = slice(h * HEAD, (h + 1) * HEAD)
            xv = (x_ref[h] if head_major else x_ref[:, sl]).astype(F32)
            r = lax.rsqrt(jnp.mean(xv * xv, axis=-1, keepdims=True) + EPS)
            y = xv * r * gv
            if scale != 1.0:
                y = y * scale
            if has_gate:
                y = y * _silu(gt_ref[:, sl])
            o_ref[:, sl] = y.astype(out_dtype)

    row = pl.BlockSpec((tm, d), lambda i: (i, 0))
    hm = pl.BlockSpec((nh, tm, HEAD), lambda i: (0, i, 0))
    ins = [x, g.reshape(1, HEAD)]
    in_specs = [hm if head_major else row, pl.BlockSpec((1, HEAD), lambda i: (0, 0))]
    if has_gate:
        ins.append(gate)
        in_specs.append(pl.BlockSpec((tm, d), lambda i: (i, gb)))
    return pl.pallas_call(
        body, name=name, grid=(s // tm,), in_specs=in_specs, out_specs=row,
        out_shape=jax.ShapeDtypeStruct((s, d), out_dtype), compiler_params=_params(("parallel",)))(*ins)


def _headnorm_bwd(dy, x, g, name, scale=1.0, gate=None, gate_col0=0, dx_dtype=F32, head_major=False):
    s, d = dy.shape
    nh = d // HEAD
    tm = _pick(s, (256, 128))
    has_gate = gate is not None
    gb = gate_col0 // d

    def body(*refs):
        if has_gate:
            dy_ref, x_ref, g_ref, gt_ref, dx_ref, dg_ref, dgt_ref = refs
        else:
            dy_ref, x_ref, g_ref, dx_ref, dg_ref = refs

        @pl.when(pl.program_id(0) == 0)
        def _():
            dg_ref[...] = jnp.zeros_like(dg_ref)

        gv = g_ref[...]
        dg_acc = jnp.zeros((1, HEAD), F32)
        for h in range(nh):
            sl = slice(h * HEAD, (h + 1) * HEAD)
            xv = (x_ref[h] if head_major else x_ref[:, sl]).astype(F32)
            dyv = dy_ref[:, sl].astype(F32)
            r = lax.rsqrt(jnp.mean(xv * xv, axis=-1, keepdims=True) + EPS)
            if has_gate:
                gt = gt_ref[:, sl]
                act, dact = _silu_and_grad(gt)
                dgt_ref[:, sl] = (dyv * (xv * r * gv) * dact).astype(dgt_ref.dtype)
                dn = dyv * act
            else:
                dn = dyv
            if scale != 1.0:
                dn = dn * scale
            gdn = dn * gv
            mean_t = jnp.mean(xv * gdn, axis=-1, keepdims=True)
            dxv = (r * gdn - xv * (r * r * r) * mean_t).astype(dx_dtype)
            if head_major:
                dx_ref[h] = dxv
            else:
                dx_ref[:, sl] = dxv
            dg_acc = dg_acc + jnp.sum(dn * xv * r, axis=0, keepdims=True)
        dg_ref[...] += dg_acc

    row = pl.BlockSpec((tm, d), lambda i: (i, 0))
    hm = pl.BlockSpec((nh, tm, HEAD), lambda i: (0, i, 0))
    vec = pl.BlockSpec((1, HEAD), lambda i: (0, 0))
    ins = [dy, x, g.reshape(1, HEAD)]
    in_specs = [row, hm if head_major else row, vec]
    out_specs = [hm if head_major else row, vec]
    dx_shape = (nh, s, HEAD) if head_major else (s, d)
    out_shape = [jax.ShapeDtypeStruct(dx_shape, dx_dtype), jax.ShapeDtypeStruct((1, HEAD), F32)]
    if has_gate:
        ins.append(gate)
        in_specs.append(pl.BlockSpec((tm, d), lambda i: (i, gb)))
        out_specs.append(row)
        out_shape.append(jax.ShapeDtypeStruct((s, d), BF16))
    return pl.pallas_call(
        body, name=name, grid=(s // tm,), in_specs=in_specs, out_specs=out_specs, out_shape=out_shape,
        compiler_params=_params(("arbitrary",)))(*ins)


def _swiglu_fwd(hn, wf_t, name):
    s, d = hn.shape
    f = wf_t.shape[0] // 2
    tm = _pick(s, (1024, 512, 256, 128))
    tn = _pick(f, (512, 256, 128))
    nj = f // tn

    def body(a_ref, wg_ref, wu_ref, act_ref, g_ref, u_ref):
        a = a_ref[...]
        g = _dot(a, wg_ref[...], NT)
        u = _dot(a, wu_ref[...], NT)
        act_ref[...] = (_silu(g) * u).astype(BF16)
        g_ref[...] = g.astype(BF16)
        u_ref[...] = u.astype(BF16)

    o_spec = pl.BlockSpec((tm, tn), lambda i, j: (i, j))
    sds = jax.ShapeDtypeStruct((s, f), BF16)
    return pl.pallas_call(
        body, name=name, grid=(s // tm, nj),
        in_specs=[pl.BlockSpec((tm, d), lambda i, j: (i, 0)), pl.BlockSpec((tn, d), lambda i, j: (j, 0)),
                  pl.BlockSpec((tn, d), lambda i, j: (j + nj, 0))],
        out_specs=[o_spec, o_spec, o_spec], out_shape=[sds, sds, sds],
        compiler_params=_params(("parallel", "parallel")))(hn, wf_t, wf_t)


def _swiglu_bwd(dh, w_out, g, u, name):
    s, d = dh.shape
    f = w_out.shape[0]
    tm = _pick(s, (1024, 512, 256, 128))
    tn = _pick(f, (512, 256, 128))

    def body(dh_ref, w_ref, g_ref, u_ref, dg_ref, du_ref):
        dact = _dot(dh_ref[...], w_ref[...], NT)
        gv = g_ref[...].astype(F32)
        uv = u_ref[...].astype(F32)
        sg, dsg = _silu_and_grad(gv)
        dg_ref[...] = (dact * uv * dsg).astype(BF16)
        du_ref[...] = (dact * sg).astype(BF16)

    o_spec = pl.BlockSpec((tm, tn), lambda i, j: (i, j))
    sds = jax.ShapeDtypeStruct((s, f), BF16)
    return pl.pallas_call(
        body, name=name, grid=(s // tm, f // tn),
        in_specs=[pl.BlockSpec((tm, d), lambda i, j: (i, 0)), pl.BlockSpec((tn, d), lambda i, j: (j, 0)), o_spec, o_spec],
        out_specs=[o_spec, o_spec], out_shape=[sds, sds],
        compiler_params=_params(("parallel", "parallel")))(dh, w_out, g, u)


def _ple_fwd(h, hn, p, w_gate, wp_t, name, norm_gs=()):
    s, d = h.shape
    pd = p.shape[1]
    tm = _pick(s, (512, 256, 128))
    ng = len(norm_gs)

    def body(h_ref, hn_ref, p_ref, wg_ref, wp_ref, *rest):
        g_refs, (o_ref, gp_ref, pp_ref), n_refs = rest[:ng], rest[ng:ng + 3], rest[ng + 3:]
        gpre = _dot(hn_ref[...], wg_ref[...], NN)
        pp = _dot(p_ref[...], wp_ref[...], NT)
        o = h_ref[...] + pp * jax.nn.sigmoid(gpre)
        o_ref[...] = o
        gp_ref[...] = gpre.astype(BF16)
        pp_ref[...] = pp.astype(BF16)
        if ng:
            on = o * lax.rsqrt(jnp.mean(o * o, axis=-1, keepdims=True) + EPS)
            for g_ref, n_ref in zip(g_refs, n_refs):
                n_ref[...] = (on * g_ref[...]).astype(BF16)

    row = pl.BlockSpec((tm, d), lambda i: (i, 0))
    vec = pl.BlockSpec((1, d), lambda i: (0, 0))
    bf = jax.ShapeDtypeStruct((s, d), BF16)
    return pl.pallas_call(
        body, name=name, grid=(s // tm,),
        in_specs=[row, row, pl.BlockSpec((tm, pd), lambda i: (i, 0)), pl.BlockSpec((d, d), lambda i: (0, 0)),
                  pl.BlockSpec((d, pd), lambda i: (0, 0))] + [vec] * ng,
        out_specs=[row] * (3 + ng), out_shape=[jax.ShapeDtypeStruct((s, d), F32), bf, bf] + [bf] * ng,
        compiler_params=_params(("parallel",)))(h, hn, p, w_gate, wp_t, *[g.reshape(1, d) for g in norm_gs])


def _ple_bwd(dh, gpre, pp, name):
    s, d = dh.shape
    tm = _pick(s, (512, 256, 128))

    def body(dh_ref, gp_ref, pp_ref, dgp_ref, dpp_ref):
        dv = dh_ref[...]
        sig = jax.nn.sigmoid(gp_ref[...].astype(F32))
        ppv = pp_ref[...].astype(F32)
        dpp_ref[...] = (dv * sig).astype(BF16)
        dgp_ref[...] = (dv * ppv * sig * (1.0 - sig)).astype(BF16)

    row = pl.BlockSpec((tm, d), lambda i: (i, 0))
    sds = jax.ShapeDtypeStruct((s, d), BF16)
    return pl.pallas_call(
        body, name=name, grid=(s // tm,), in_specs=[row, row, row], out_specs=[row, row], out_shape=[sds, sds],
        compiler_params=_params(("parallel",)))(dh, gpre, pp)


def _loss_fwd_bwd(y, t, name):
    s, d = y.shape
    tm = _pick(s, (512, 256, 128))

    def body(y_ref, t_ref, dy_ref, l_ref):
        @pl.when(pl.program_id(0) == 0)
        def _():
            l_ref[...] = jnp.zeros_like(l_ref)

        e = y_ref[...] - t_ref[...]
        dy_ref[...] = e * (1.0 / d)
        l_ref[...] += jnp.sum(e * e, axis=0, keepdims=True) * (0.5 / d)

    row = pl.BlockSpec((tm, d), lambda i: (i, 0))
    vec = pl.BlockSpec((1, d), lambda i: (0, 0))
    return pl.pallas_call(
        body, name=name, grid=(s // tm,), in_specs=[row, row], out_specs=[row, vec],
        out_shape=[jax.ShapeDtypeStruct((s, d), F32), jax.ShapeDtypeStruct((1, d), F32)],
        compiler_params=_params(("arbitrary",)))(y, t)


PADR = 8


def _conv_fwd(proj, w_conv, d, name):
    s = proj.shape[0]
    nh = d // HEAD
    kw = w_conv.shape[0]
    qscale = HEAD ** -0.5

    def body(x_ref, w_ref, o_ref, xp):
        kind = pl.program_id(0) // nh
        xp[0:PADR, :] = jnp.zeros((PADR, HEAD), F32)
        xp[PADR:, :] = x_ref[...]
        acc = jnp.zeros((s, HEAD), F32)
        for j in range(kw):
            acc = acc + w_ref[j:j + 1, :] * xp[PADR - (kw - 1) + j:PADR - (kw - 1) + j + s, :]
        a = _silu(acc)
        r = lax.rsqrt(jnp.sum(a * a, axis=-1, keepdims=True) + EPS)
        fac = jnp.where(kind == 0, r * qscale, jnp.where(kind == 1, r, jnp.ones_like(r)))
        o_ref[...] = a * fac

    blk = pl.BlockSpec((s, HEAD), lambda c: (0, c))
    hm = pl.BlockSpec((None, s, HEAD), lambda c: (c, 0, 0))
    return pl.pallas_call(
        body, name=name, grid=(3 * nh,), in_specs=[blk, pl.BlockSpec((kw, HEAD), lambda c: (0, c))], out_specs=hm,
        out_shape=jax.ShapeDtypeStruct((3 * nh, s, HEAD), F32), scratch_shapes=[pltpu.VMEM((s + PADR, HEAD), F32)],
        compiler_params=_params(("parallel",)))(proj, w_conv)


def _conv_bwd(dqkv, proj, w_conv, d, name):
    s = proj.shape[0]
    nh = d // HEAD
    kw = w_conv.shape[0]
    qscale = HEAD ** -0.5

    def body(dy_ref, x_ref, w_ref, dx_ref, dw_ref, xp, dp):
        kind = pl.program_id(0) // nh
        xp[0:PADR, :] = jnp.zeros((PADR, HEAD), F32)
        xp[PADR:, :] = x_ref[...]
        acc = jnp.zeros((s, HEAD), F32)
        for j in range(kw):
            acc = acc + w_ref[j:j + 1, :] * xp[PADR - (kw - 1) + j:PADR - (kw - 1) + j + s, :]
        a, da_dacc = _silu_and_grad(acc)
        dy = dy_ref[...]
        r = lax.rsqrt(jnp.sum(a * a, axis=-1, keepdims=True) + EPS)
        sc = jnp.where(kind == 0, qscale, 1.0)
        dyn = dy * sc
        da_norm = r * dyn - a * (r * r * r) * jnp.sum(a * dyn, axis=-1, keepdims=True)
        da = jnp.where(kind == 2, dy, da_norm)
        dacc = da * da_dacc
        dp[0:s, :] = dacc
        dp[s:, :] = jnp.zeros((PADR, HEAD), F32)
        dx = jnp.zeros((s, HEAD), F32)
        for j in range(kw):
            sh = kw - 1 - j
            dx = dx + w_ref[j:j + 1, :] * dp[sh:sh + s, :]
            dw_ref[j:j + 1, :] = jnp.sum(dacc * xp[PADR - sh:PADR - sh + s, :], axis=0, keepdims=True)
        dx_ref[...] = dx.astype(BF16)

    blk = pl.BlockSpec((s, HEAD), lambda c: (0, c))
    hm = pl.BlockSpec((None, s, HEAD), lambda c: (c, 0, 0))
    wblk = pl.BlockSpec((kw, HEAD), lambda c: (0, c))
    return pl.pallas_call(
        body, name=name, grid=(3 * nh,), in_specs=[hm, blk, wblk], out_specs=[blk, wblk],
        out_shape=[jax.ShapeDtypeStruct((s, 3 * d), BF16), jax.ShapeDtypeStruct((kw, 3 * d), F32)],
        scratch_shapes=[pltpu.VMEM((s + PADR, HEAD), F32), pltpu.VMEM((s + PADR, HEAD), F32)],
        compiler_params=_params(("parallel",)))(dqkv, proj, w_conv)


def _softplus(x):
    return jnp.maximum(x, 0.0) + jnp.log(1.0 + jnp.exp(-jnp.abs(x)))


def _gates_fwd(pab, a_log, dt_bias, nh, name):
    s = pab.shape[0]
    tm = _pick(s, (512, 256, 128))

    def body(x_ref, al_ref, dt_ref, o_ref):
        x = x_ref[...]
        lane = lax.broadcasted_iota(jnp.int32, x.shape, 1)
        g = -jnp.exp(al_ref[...]) * _softplus(x + dt_ref[...])
        o_ref[...] = jnp.where(lane < nh, g, jnp.where(lane < 2 * nh, jax.nn.sigmoid(x), 0.0))

    row = pl.BlockSpec((tm, HEAD), lambda i: (i, 0))
    vec = pl.BlockSpec((1, HEAD), lambda i: (0, 0))
    return pl.pallas_call(
        body, name=name, grid=(s // tm,), in_specs=[row, vec, vec], out_specs=row,
        out_shape=jax.ShapeDtypeStruct((s, HEAD), F32), compiler_params=_params(("parallel",)))(pab, a_log, dt_bias)


def _gates_bwd(dgb, pab, a_log, dt_bias, nh, name):
    s = pab.shape[0]
    tm = _pick(s, (512, 256, 128))

    def body(d_ref, x_ref, al_ref, dt_ref, dx_ref, dal_ref, ddt_ref):
        @pl.when(pl.program_id(0) == 0)
        def _():
            dal_ref[...] = jnp.zeros_like(dal_ref)
            ddt_ref[...] = jnp.zeros_like(ddt_ref)

        x = x_ref[...]
        dv = d_ref[...]
        lane = lax.broadcasted_iota(jnp.int32, x.shape, 1)
        ea = jnp.exp(al_ref[...])
        xs = x + dt_ref[...]
        g = -ea * _softplus(xs)
        dxs = jnp.where(lane < nh, dv * (-ea) * jax.nn.sigmoid(xs), 0.0)
        sg = jax.nn.sigmoid(x)
        dxb = jnp.where((lane >= nh) & (lane < 2 * nh), dv * sg * (1.0 - sg), 0.0)
        dx_ref[...] = (dxs + dxb).astype(BF16)
        dal_ref[...] += jnp.sum(jnp.where(lane < nh, dv * g, 0.0), axis=0, keepdims=True)
        ddt_ref[...] += jnp.sum(dxs, axis=0, keepdims=True)

    row = pl.BlockSpec((tm, HEAD), lambda i: (i, 0))
    vec = pl.BlockSpec((1, HEAD), lambda i: (0, 0))
    return pl.pallas_call(
        body, name=name, grid=(s // tm,), in_specs=[row, row, vec, vec], out_specs=[row, vec, vec],
        out_shape=[jax.ShapeDtypeStruct((s, HEAD), BF16), jax.ShapeDtypeStruct((1, HEAD), F32),
                   jax.ShapeDtypeStruct((1, HEAD), F32)],
        compiler_params=_params(("arbitrary",)))(dgb, pab, a_log, dt_bias)


def _tri_inv(a_low, eye_f):
    n = -a_low
    p = eye_f + n
    steps = int(math.log2(a_low.shape[-1])) - 1
    for _ in range(steps):
        n = _dot(n, n, BNN)
        p = p + _dot(p, n, BNN)
    return p


def _lane_col(x, lane, idx):
    return jnp.sum(jnp.where(lane == idx, x, 0.0), axis=1, keepdims=True)


def _head_cols(gbv, lo, nh):
    lane = lax.broadcasted_iota(jnp.int32, gbv.shape, 1)
    return jnp.stack([_lane_col(gbv, lane, lo + h) for h in range(nh)], axis=0)


def _gdn_chunk(q, k, v, g_col, beta_col, st):
    c = q.shape[1]
    r_i = lax.broadcasted_iota(jnp.int32, (c, c), 0)
    c_i = lax.broadcasted_iota(jnp.int32, (c, c), 1)
    incl = c_i <= r_i
    strict = c_i < r_i
    eye = c_i == r_i
    g_row = jnp.sum(jnp.where(eye, g_col, 0.0), axis=1, keepdims=True)
    gc_col = jnp.sum(jnp.where(incl, g_row, 0.0), axis=2, keepdims=True)
    gc_row = jnp.sum(jnp.where(eye, gc_col, 0.0), axis=1, keepdims=True)
    g_last = jnp.sum(g_col, axis=1, keepdims=True)
    decay = jnp.exp(jnp.where(incl, gc_col - gc_row, NEG))
    kk = _dot(k, k, BNT)
    a_low = jnp.where(strict, beta_col * kk * decay, 0.0)
    t_inv = _tri_inv(a_low, eye.astype(F32))
    e_g = jnp.exp(gc_col)
    bk = beta_col * e_g
    rhs = jnp.concatenate([v * beta_col, k * bk], axis=2)
    sol = _dot(t_inv, rhs, BNN)
    u, w = sol[:, :, :HEAD], sol[:, :, HEAD:]
    qk_raw = _dot(q, k, BNT)
    qk = qk_raw * decay
    q_dec = q * e_g
    e2 = jnp.exp(g_last - gc_col)
    k_dec = k * e2
    gl = jnp.exp(g_last)
    ws = _dot(jnp.concatenate([w, q_dec], axis=1), st, BNN)
    v_new = u - ws[:, :c]
    o = ws[:, c:] + _dot(qk, v_new, BNN)
    st_new = st * gl + _dot(k_dec, v_new, BTN)
    inter = dict(incl=incl, strict=strict, eye=eye, decay=decay, kk=kk, t_inv=t_inv, e_g=e_g, bk=bk, sol=sol, w=w,
                 qk_raw=qk_raw, qk=qk, q_dec=q_dec, e2=e2, k_dec=k_dec, gl=gl, v_new=v_new, c_i=c_i, r_i=r_i)
    return o, st_new, inter


def _gdn_fwd(qkv, gb, nh, name):
    s = qkv.shape[1]
    nc = s // CHUNK

    def body(q_ref, k_ref, v_ref, gb_ref, o_ref, st_ref, state):
        @pl.when(pl.program_id(0) == 0)
        def _():
            state[...] = jnp.zeros_like(state)

        gbv = gb_ref[...]
        st = state[...]
        st_ref[...] = st
        o, st_new, _ = _gdn_chunk(q_ref[...], k_ref[...], v_ref[...], _head_cols(gbv, 0, nh), _head_cols(gbv, nh, nh), st)
        o_ref[...] = o
        state[...] = st_new

    def qspec(part):
        return pl.BlockSpec((nh, CHUNK, HEAD), lambda n: (part, n, 0))

    return pl.pallas_call(
        body, name=name, grid=(nc,),
        in_specs=[qspec(0), qspec(1), qspec(2), pl.BlockSpec((CHUNK, HEAD), lambda n: (n, 0))],
        out_specs=[qspec(0), pl.BlockSpec((None, nh, HEAD, HEAD), lambda n: (n, 0, 0, 0))],
        out_shape=[jax.ShapeDtypeStruct((nh, s, HEAD), F32), jax.ShapeDtypeStruct((nc, nh, HEAD, HEAD), F32)],
        scratch_shapes=[pltpu.VMEM((nh, HEAD, HEAD), F32)],
        compiler_params=_params(("arbitrary",)))(qkv, qkv, qkv, gb)


def _gdn_bwd(qkv, gb, do, states, nh, name):
    s = qkv.shape[1]
    nc = s // CHUNK
    c = CHUNK

    def body(q_ref, k_ref, v_ref, gb_ref, do_ref, st_ref, dqkv_ref, dgb_ref, dstate):
        @pl.when(pl.program_id(0) == 0)
        def _():
            dstate[...] = jnp.zeros_like(dstate)

        gbv = gb_ref[...]
        lane = lax.broadcasted_iota(jnp.int32, gbv.shape, 1)
        q, k, v = q_ref[...], k_ref[...], v_ref[...]
        beta_col = _head_cols(gbv, nh, nh)
        st = st_ref[...]
        dst = dstate[...]
        dov = do_ref[...]
        _, _, it = _gdn_chunk(q, k, v, _head_cols(gbv, 0, nh), beta_col, st)
        incl, strict, eye, decay = it["incl"], it["strict"], it["eye"], it["decay"]
        dv_new = _dot(it["qk"], dov, BTN) + _dot(it["k_dec"], dst, BNN)
        d_qk = _dot(dov, it["v_new"], BNT)
        dd = _dot(jnp.concatenate([dov, -dv_new], axis=1), st, BNT)
        dq_dec, dw = dd[:, :c], dd[:, c:]
        dst_new = _dot(it["q_dec"], dov, BTN) + it["gl"] * dst - _dot(it["w"], dv_new, BTN)
        dgl = jnp.sum(jnp.sum(dst * st, axis=2, keepdims=True), axis=1, keepdims=True)
        dk_dec = _dot(it["v_new"], dst, BNT)
        dsol = jnp.concatenate([dv_new, dw], axis=2)
        drhs = _dot(it["t_inv"], dsol, BTN)
        d_a = jnp.where(strict, -_dot(drhs, it["sol"], BNT), 0.0)
        drhs_u, drhs_w = drhs[:, :, :HEAD], drhs[:, :, HEAD:]
        dvh = beta_col * drhs_u
        rw_k = jnp.sum(drhs_w * k, axis=2, keepdims=True)
        dbeta = jnp.sum(drhs_u * v, axis=2, keepdims=True) + it["e_g"] * rw_k
        dkh = it["bk"] * drhs_w
        dgc_col = it["bk"] * rw_k
        dkk = d_a * beta_col * decay
        dbeta = dbeta + jnp.sum(d_a * it["kk"] * decay, axis=2, keepdims=True)
        ddecay = d_a * beta_col * it["kk"]
        dkh = dkh + _dot(dkk, k, BNN) + _dot(dkk, k, BTN)
        dqk_raw = d_qk * decay
        ddecay = ddecay + d_qk * it["qk_raw"]
        dqh = _dot(dqk_raw, k, BNN)
        dkh = dkh + _dot(dqk_raw, q, BTN)
        ddm = jnp.where(incl, ddecay * decay, 0.0)
        dgc_col = dgc_col + jnp.sum(ddm, axis=2, keepdims=True)
        dgc_row = -jnp.sum(ddm, axis=1, keepdims=True)
        dqh = dqh + dq_dec * it["e_g"]
        dgc_col = dgc_col + jnp.sum(dq_dec * it["q_dec"], axis=2, keepdims=True)
        dkh = dkh + dk_dec * it["e2"]
        tmp = jnp.sum(dk_dec * it["k_dec"], axis=2, keepdims=True)
        dgc_col = dgc_col - tmp
        dg_last = jnp.sum(tmp, axis=1, keepdims=True) + dgl * it["gl"]
        dgc_tot_row = dgc_row + jnp.sum(jnp.where(eye, dgc_col, 0.0), axis=1, keepdims=True)
        dg_col = jnp.sum(jnp.where(it["c_i"] >= it["r_i"], dgc_tot_row, 0.0), axis=2, keepdims=True) + dg_last
        dqkv_ref[0] = dqh
        dqkv_ref[1] = dkh
        dqkv_ref[2] = dvh
        dstate[...] = dst_new
        dgb_acc = jnp.zeros(gbv.shape, F32)
        for h in range(nh):
            dgb_acc = jnp.where(lane == h, dg_col[h], jnp.where(lane == nh + h, dbeta[h], dgb_acc))
        dgb_ref[...] = dgb_acc

    def rev(part):
        return pl.BlockSpec((nh, CHUNK, HEAD), lambda n: (part, nc - 1 - n, 0))

    gspec = pl.BlockSpec((CHUNK, HEAD), lambda n: (nc - 1 - n, 0))
    dqkv, dgb = pl.pallas_call(
        body, name=name, grid=(nc,),
        in_specs=[rev(0), rev(1), rev(2), gspec, rev(0),
                  pl.BlockSpec((None, nh, HEAD, HEAD), lambda n: (nc - 1 - n, 0, 0, 0))],
        out_specs=[pl.BlockSpec((3, nh, CHUNK, HEAD), lambda n: (0, 0, nc - 1 - n, 0)), gspec],
        out_shape=[jax.ShapeDtypeStruct((3, nh, s, HEAD), F32), jax.ShapeDtypeStruct((s, HEAD), F32)],
        scratch_shapes=[pltpu.VMEM((nh, HEAD, HEAD), F32)],
        compiler_params=_params(("arbitrary",)))(qkv, qkv, qkv, gb, do, states)
    return dqkv.reshape(3 * nh, s, HEAD), dgb


SB_TQ = 512


def _tri01(rel):
    j_i = lax.broadcasted_iota(jnp.int32, (SBLK, SBLK), 0)
    s_i = lax.broadcasted_iota(jnp.int32, (SBLK, SBLK), 1)
    return rel(j_i, s_i).astype(BF16)


SB_HP = 2


def _each(fn, *lists):
    return [fn(*xs) for xs in zip(*lists)]


def _sb_scores(qts, kblks, mask, csums, rhs01):
    zs = _each(lambda qt, kb: _dot(qt, kb, NT), qts, kblks)
    es = _each(lambda z: jnp.exp(-jnp.abs(z)), zs)
    sps = _each(lambda z, e: jnp.maximum(z, 0.0) + jnp.log(1.0 + e), zs, es)
    lns = _each(lambda sp: -sp if mask is None else jnp.where(mask, -sp, 0.0), sps)
    sts = _each(lambda ln: _dot_hilo(ln, rhs01), lns)
    wgts = _each(lambda z, sp, st, cs: jnp.exp((z - sp) + st + cs), zs, sps, sts, csums)
    if mask is not None:
        wgts = _each(lambda w: jnp.where(mask, w, 0.0), wgts)
    return zs, es, wgts, lns


def _band_mask(rows, j, row0):
    r_i = lax.broadcasted_iota(jnp.int32, (rows, SBLK), 0)
    c_i = lax.broadcasted_iota(jnp.int32, (rows, SBLK), 1)
    return (j * SBLK + c_i) < (row0 + r_i)


def _sb_fwd(q, k, v, name):
    s, d = q.shape
    nh = d // HEAD
    tq = min(SB_TQ, s)
    nb = tq // SBLK

    hp = SB_HP
    heads = [slice(h * HEAD, (h + 1) * HEAD) for h in range(hp)]

    def body(q_ref, k_ref, v_ref, o_ref, c_ref, acc, cs):
        qb = pl.program_id(1)
        lane = lax.broadcasted_iota(jnp.int32, (tq, HEAD), 1)
        after = _tri01(lambda j, t: j > t)
        acc[...] = jnp.zeros_like(acc)
        cs[...] = jnp.zeros_like(cs)
        c_ref[...] = jnp.zeros_like(c_ref)

        def process(rs, kb, mask):
            keys = pl.ds(pl.multiple_of(kb * SBLK, SBLK), SBLK)
            csums = [cs[h, rs, :] for h in range(hp)]
            _, _, wgts, lns = _sb_scores([q_ref[rs, hs] for hs in heads], [k_ref[keys, hs] for hs in heads], mask, csums, after)
            pvs = _each(lambda w, hs: _dot(w, v_ref[keys, hs]), wgts, heads)
            tots = _each(lambda ln: jnp.sum(ln, axis=1, keepdims=True), lns)
            for h, hs in enumerate(heads):
                acc[h, rs, :] += pvs[h]
                c_ref[rs, hs] = jnp.where(lane[rs, :] == kb, csums[h], c_ref[rs, hs])
                cs[h, rs, :] = csums[h] + tots[h]

        for j in reversed(range(nb)):
            process(slice(j * SBLK, tq), qb * nb + j, _band_mask(tq - j * SBLK, j, j * SBLK))

        def step(it, carry):
            process(slice(0, tq), qb * nb - 1 - it, None)
            return carry

        lax.fori_loop(0, qb * nb, step, 0)
        for h, hs in enumerate(heads):
            o_ref[:, hs] = acc[h].astype(BF16)

    qspec = pl.BlockSpec((tq, hp * HEAD), lambda h, i: (i, h))
    kspec = pl.BlockSpec((s, hp * HEAD), lambda h, i: (0, h))
    return pl.pallas_call(
        body, name=name, grid=(nh // hp, s // tq), in_specs=[qspec, kspec, kspec], out_specs=[qspec, qspec],
        out_shape=[jax.ShapeDtypeStruct((s, d), BF16), jax.ShapeDtypeStruct((s, d), F32)],
        scratch_shapes=[pltpu.VMEM((hp, tq, HEAD), F32), pltpu.VMEM((hp, tq, 1), F32)],
        compiler_params=_params(("parallel", "arbitrary")))(q, k, v)


def _sb_bwd(q, k, v, do, ctab, name):
    s, d = q.shape
    nh = d // HEAD
    tq = min(SB_TQ, s)
    nb = tq // SBLK

    hp = SB_HP
    heads = [slice(h * HEAD, (h + 1) * HEAD) for h in range(hp)]

    def body(q_ref, k_ref, v_ref, do_ref, c_ref, dq_ref, dk_ref, dv_ref, ps):
        qb = pl.program_id(1)

        @pl.when(qb == 0)
        def _():
            dk_ref[...] = jnp.zeros_like(dk_ref)
            dv_ref[...] = jnp.zeros_like(dv_ref)

        dq_ref[...] = jnp.zeros_like(dq_ref)
        ps[...] = jnp.zeros_like(ps)
        lane = lax.broadcasted_iota(jnp.int32, (tq, HEAD), 1)
        after = _tri01(lambda j, t: j > t)
        before = _tri01(lambda j, t: j < t)

        def process(rs, kb, mask):
            keys = pl.ds(pl.multiple_of(kb * SBLK, SBLK), SBLK)
            kblks = [k_ref[keys, hs] for hs in heads]
            qts = [q_ref[rs, hs] for hs in heads]
            dots = [do_ref[rs, hs] for hs in heads]
            csums = [_lane_col(c_ref[rs, hs], lane[rs, :], kb) for hs in heads]
            zs, es, wgts, _ = _sb_scores(qts, kblks, mask, csums, after)
            dlws = _each(lambda dt, hs, w: _dot(dt, v_ref[keys, hs], NT) * w, dots, heads, wgts)
            pts = _each(lambda dlw: _dot_hilo(dlw, before), dlws)
            pfxs = [ps[h, rs, :] for h in range(hp)]
            rs_ = _each(lambda e: 1.0 / (1.0 + e), es)
            sigs = _each(lambda z, e, r: jnp.where(z >= 0.0, r, e * r), zs, es, rs_)
            dzs = _each(lambda dlw, sig, pfx, pt: dlw * (1.0 - sig) - sig * (pfx + pt), dlws, sigs, pfxs, pts)
            tots = _each(lambda dlw: jnp.sum(dlw, axis=1, keepdims=True), dlws)
            if mask is not None:
                dzs = _each(lambda dz: jnp.where(mask, dz, 0.0), dzs)
            dqs = _each(lambda dz, kb_: _dot(dz, kb_), dzs, kblks)
            dks = _each(lambda dz, qt: _dot(dz, qt, TN), dzs, qts)
            dvs = _each(lambda w, dt: _dot(w, dt, TN), wgts, dots)
            for h, hs in enumerate(heads):
                dq_ref[rs, hs] += dqs[h]
                dk_ref[keys, hs] += dks[h]
                dv_ref[keys, hs] += dvs[h]
                ps[h, rs, :] = pfxs[h] + tots[h]

        def step(kb, carry):
            process(slice(0, tq), kb, None)
            return carry

        lax.fori_loop(0, qb * nb, step, 0)
        for j in range(nb):
            process(slice(j * SBLK, tq), qb * nb + j, _band_mask(tq - j * SBLK, j, j * SBLK))

    qspec = pl.BlockSpec((tq, hp * HEAD), lambda h, i: (i, h))
    kspec = pl.BlockSpec((s, hp * HEAD), lambda h, i: (0, h))
    sds = jax.ShapeDtypeStruct((s, d), F32)
    return pl.pallas_call(
        body, name=name, grid=(nh // hp, s // tq), in_specs=[qspec, kspec, kspec, qspec, qspec],
        out_specs=[qspec, kspec, kspec], out_shape=[sds, sds, sds],
        scratch_shapes=[pltpu.VMEM((hp, tq, 1), F32)],
        compiler_params=_params(("parallel", "arbitrary")))(q, k, v, do, ctab)


def _my_index():
    return 4 * lax.axis_index("x") + 2 * lax.axis_index("y") + lax.axis_index("c")


def _all_gather(x_shard, name):
    m_per, n = x_shard.shape

    def body(x_ref, out_ref, send_sems, recv_sems, local_sem):
        x, y, c = lax.axis_index("x"), lax.axis_index("y"), lax.axis_index("c")
        me, sibling = (x, y, c), (x, y, 1 - c)
        chips = [(1 - x, y), (x, 1 - y), (1 - x, 1 - y)]

        def rows(px, py, pc):
            return out_ref.at[pl.ds((4 * px + 2 * py + pc) * m_per, m_per), :]

        def copy(k, block, to, src=None):
            return pltpu.make_async_remote_copy(
                src_ref=rows(*block) if src is None else src, dst_ref=rows(*block),
                send_sem=send_sems.at[k], recv_sem=recv_sems.at[k], device_id=to, device_id_type=MESH)

        mine = pltpu.make_async_copy(x_ref, rows(*me), local_sem)
        mine.start()
        first = [copy(0, me, sibling, src=x_ref)]
        first += [copy(1 + j, me, (*chip, c), src=x_ref) for j, chip in enumerate(chips)]
        for cp in first:
            cp.start()
        passed = [copy(4 + j, (*chip, c), sibling) for j, chip in enumerate(chips)]
        for j, chip in enumerate(chips):
            copy(1 + j, (*chip, c), me).wait_recv()
            passed[j].start()
        copy(0, sibling, me).wait_recv()
        for j, chip in enumerate(chips):
            copy(4 + j, (*chip, 1 - c), me).wait_recv()
        for cp in first + passed:
            cp.wait_send()
        mine.wait()

    return pl.pallas_call(
        body, name=name, out_shape=jax.ShapeDtypeStruct((NDEV * m_per, n), x_shard.dtype),
        in_specs=[pl.BlockSpec(memory_space=pl.ANY)], out_specs=pl.BlockSpec(memory_space=pl.ANY),
        scratch_shapes=[pltpu.SemaphoreType.DMA((7,)), pltpu.SemaphoreType.DMA((7,)), pltpu.SemaphoreType.DMA],
    )(x_shard)


HBM_SPEC = pl.BlockSpec(memory_space=pltpu.HBM)
SEM_SPEC = pl.BlockSpec(memory_space=pltpu.SEMAPHORE)
ANY_SPEC = pl.BlockSpec(memory_space=pl.ANY)
EFFECT = pltpu.SideEffectType.DATAFLOW_SIDE_EFFECTING


def _exchange_copies(src_refs, land_refs, send_sems, recv_sems, self_sems, scatter):
    x, y, c = lax.axis_index("x"), lax.axis_index("y"), lax.axis_index("c")
    me = 4 * x + 2 * y + c
    remote, local = [], []
    for p, (src_ref, land_ref) in enumerate(zip(src_refs, land_refs)):
        rows = land_ref.shape[0] // NDEV

        def part(idx):
            return src_ref.at[pl.ds(idx * rows, rows), :] if scatter else src_ref

        slot = land_ref.at[pl.ds(me * rows, rows), :]
        for k in range(1, NDEV):
            px, py, pc = x ^ ((k >> 2) & 1), y ^ ((k >> 1) & 1), c ^ (k & 1)
            remote.append(pltpu.make_async_remote_copy(
                src_ref=part(4 * px + 2 * py + pc), dst_ref=slot, send_sem=send_sems.at[7 * p + k - 1],
                recv_sem=recv_sems.at[7 * p + k - 1], device_id=(px, py, pc), device_id_type=MESH))
        local.append(pltpu.make_async_copy(part(me), slot, self_sems.at[p]))
    return remote, local


def _send_start(srcs, scatter, after, name):
    n = len(srcs)
    lands = []
    for s in srcs:
        rows = s.shape[0] if scatter else NDEV * s.shape[0]
        lands.append(pltpu.with_memory_space_constraint(lax.empty((rows, s.shape[1]), s.dtype), pltpu.HBM))

    def body(*refs):
        src_refs, land_refs = refs[:n], refs[n:2 * n]
        send_sems, recv_sems, self_sems = refs[2 * n + 1:2 * n + 4]
        remote, local = _exchange_copies(src_refs, land_refs, send_sems, recv_sems, self_sems, scatter)
        for cp in remote + local:
            cp.start()
        refs[-1][...] = jnp.zeros_like(refs[-1])

    hbm = lambda a: pltpu.HBM(a.shape, a.dtype)
    out = pl.pallas_call(
        body, name=name,
        out_shape=(pltpu.SemaphoreType.DMA((7 * n,)), pltpu.SemaphoreType.DMA((7 * n,)), pltpu.SemaphoreType.DMA((n,)),
                   *[hbm(s) for s in srcs], *[hbm(a) for a in lands], jax.ShapeDtypeStruct((8, HEAD), F32)),
        in_specs=(HBM_SPEC,) * (2 * n) + (ANY_SPEC,),
        out_specs=(SEM_SPEC,) * 3 + (HBM_SPEC,) * (2 * n) + (pl.BlockSpec(memory_space=pltpu.VMEM),),
        input_output_aliases={i: 3 + i for i in range(2 * n)},
        compiler_params=pltpu.CompilerParams(has_side_effects=EFFECT),
    )(*[pltpu.with_memory_space_constraint(s, pltpu.HBM) for s in srcs], *lands, after)
    return dict(sems=out[:3], srcs=out[3:3 + n], lands=out[3 + n:3 + 2 * n], token=out[-1])


def _send_wait(started, scatter, after, name):
    srcs, lands = started["srcs"], started["lands"]
    n = len(srcs)

    def body(*refs):
        src_refs, land_refs = refs[:n], refs[n:2 * n]
        send_sems, recv_sems, self_sems = refs[2 * n:2 * n + 3]
        remote, local = _exchange_copies(src_refs, land_refs, send_sems, recv_sems, self_sems, scatter)
        for cp in remote:
            cp.wait_send()
            cp.wait_recv()
        for cp in local:
            cp.wait()

    hbm = lambda a: pltpu.HBM(a.shape, a.dtype)
    out = pl.pallas_call(
        body, name=name, out_shape=(*[hbm(s) for s in srcs], *[hbm(a) for a in lands]),
        in_specs=(HBM_SPEC,) * (2 * n) + (SEM_SPEC,) * 3 + (ANY_SPEC,), out_specs=(HBM_SPEC,) * (2 * n),
        input_output_aliases={i: i for i in range(2 * n)},
        compiler_params=pltpu.CompilerParams(has_side_effects=EFFECT),
    )(*srcs, *lands, *started["sems"], after)
    return out[n:]


def _sum_slots(xs, name, rows_out=None):
    _, r, c = xs[0].shape
    ro = rows_out or r
    tc = _pick(c, (128,))

    def body(*refs):
        o_ref = refs[-1]
        for l, x_ref in enumerate(refs[:-1]):
            acc = x_ref[0].astype(F32)
            for i in range(1, NDEV):
                acc = acc + x_ref[i].astype(F32)
            o_ref[l] = acc[:ro]

    return pl.pallas_call(
        body, name=name, grid=(c // tc,), in_specs=[pl.BlockSpec((NDEV, r, tc), lambda j: (0, 0, j))] * len(xs),
        out_specs=pl.BlockSpec((len(xs), ro, tc), lambda j: (0, 0, j)),
        out_shape=jax.ShapeDtypeStruct((len(xs), ro, c), F32), compiler_params=_params(("parallel",)))(*xs)


def _adamw(w, g, m, v, name):
    if w.ndim == 3:
        nl, r, c = w.shape
        tc = _pick(c, (256, 128))
        grid = (nl, c // tc)
        blk = pl.BlockSpec((None, r, tc), lambda i, j: (i, 0, j))
        sem = ("parallel", "parallel")
    else:
        r, c = w.shape
        tr = _pick(r, (256, 128, 64, 32, 16, 8))
        grid = (r // tr,)
        blk = pl.BlockSpec((tr, c), lambda i: (i, 0))
        sem = ("parallel",)
    c1 = 1.0 - B1 ** STEP
    c2 = 1.0 - B2 ** STEP

    def body(w_ref, g_ref, m_ref, v_ref, d_ref, nm_ref, nv_ref):
        gv = g_ref[...]
        nm = B1 * m_ref[...] + (1.0 - B1) * gv
        nv = B2 * v_ref[...] + (1.0 - B2) * (gv * gv)
        d_ref[...] = -LR * ((nm / c1) / (jnp.sqrt(nv / c2) + ADAM_EPS) + WD * w_ref[...])
        nm_ref[...] = nm
        nv_ref[...] = nv

    sds = jax.ShapeDtypeStruct(w.shape, F32)
    return pl.pallas_call(
        body, name=name, grid=grid, in_specs=[blk] * 4, out_specs=[blk] * 3, out_shape=[sds] * 3,
        compiler_params=_params(sem))(w, g, m, v)


def _pad_rows(a, mult):
    r = a.shape[0]
    pad = (-r) % mult
    return a if pad == 0 else jnp.pad(a, ((0, pad), (0, 0)))


def _pad_lanes(v, width=HEAD):
    return jnp.pad(v.reshape(1, -1), ((0, 0), (0, width - v.shape[-1])))


def kernel(x, p, ln_mix, ln_ffn, ln_ple, gdn_w_in, gdn_conv, gdn_a_log, gdn_dt_bias, gdn_norm, gdn_w_out, kv_norm, w_kv, k_norm, sb_w_q, sb_q_norm, sb_w_out, ffn_w_in, ffn_w_out, ple_w_proj, ple_w_gate, loss_target, m_ln_mix, m_ln_ffn, m_ln_ple, m_gdn_w_in, m_gdn_conv, m_gdn_a_log, m_gdn_dt_bias, m_gdn_norm, m_gdn_w_out, m_kv_norm, m_w_kv, m_k_norm, m_sb_w_q, m_sb_q_norm, m_sb_w_out, m_ffn_w_in, m_ffn_w_out, m_ple_w_proj, m_ple_w_gate, v_ln_mix, v_ln_ffn, v_ln_ple, v_gdn_w_in, v_gdn_conv, v_gdn_a_log, v_gdn_dt_bias, v_gdn_norm, v_gdn_w_out, v_kv_norm, v_w_kv, v_k_norm, v_sb_w_q, v_sb_q_norm, v_sb_w_out, v_ffn_w_in, v_ffn_w_out, v_ple_w_proj, v_ple_w_gate):
    s, d = x.shape[1], x.shape[2]
    nh = d // HEAD
    depth = ln_mix.shape[0]
    n_a = gdn_w_in.shape[0]
    n_b = sb_w_q.shape[0]
    me = _my_index()
    win_cols = gdn_w_in.shape[2]
    win_rows = 4 * d + 2 * nh

    def col_t(w):
        return jnp.transpose(w).astype(BF16)

    local = {}
    for l in range(n_a):
        local[("gdn_w_in", l)] = col_t(gdn_w_in[l])
        local[("gdn_w_out", l)] = gdn_w_out[l].astype(BF16)
    local[("w_kv", 0)] = col_t(w_kv)
    for j in range(n_b):
        local[("sb_w_q", j)] = sb_w_q[j].astype(BF16)
        local[("sb_w_out", j)] = sb_w_out[j].astype(BF16)
    for l in range(depth):
        local[("ffn_w_in", l)] = col_t(ffn_w_in[l])
        local[("ffn_w_out", l)] = ffn_w_out[l].astype(BF16)
        local[("ple_w_proj", l)] = col_t(ple_w_proj[l]).reshape(-1, d)
        local[("ple_w_gate", l)] = ple_w_gate[l].astype(BF16)
    local = {key: _pad_rows(a, 16) for key, a in local.items()}

    chunks = []
    for l in range(depth):
        mix = [("gdn_w_in", l), ("gdn_w_out", l)] if l < n_a else [("sb_w_q", l - n_a), ("sb_w_out", l - n_a)]
        rest = [("ffn_w_in", l), ("ffn_w_out", l), ("ple_w_proj", l), ("ple_w_gate", l)]
        if l == n_a - 1:
            rest.append(("w_kv", 0))
        chunks += [(f"a{l}", mix), (f"f{l}", rest)]
    chunk_keys = dict(chunks)

    conv_rows = n_a * gdn_conv.shape[1]
    conv_sh = _pad_rows(gdn_conv.reshape(conv_rows, -1), 8)
    conv_g = _all_gather(conv_sh, "comm_gather_conv")
    token = conv_g
    conv_g = conv_g.reshape(NDEV, conv_sh.shape[0], -1)
    conv_full = jnp.transpose(conv_g[:, :conv_rows, :], (1, 0, 2)).reshape(n_a, gdn_conv.shape[1], 3 * d)

    w_started = {}
    for name, keys in chunks:
        w_started[name] = _send_start([local[k] for k in keys], False, token, f"comm_wstart_{name}")
        token = w_started[name]["token"]

    full = {}

    def fetch(name, after):
        lands = _send_wait(w_started[name], False, after, f"comm_wwait_{name}")
        for key, land in zip(chunk_keys[name], lands):
            full[key] = land

    def whole(key, valid=None):
        a = full[key]
        if valid is not None:
            a = a.reshape(NDEV, -1, d)[:, :valid, :].reshape(-1, d)
        return a

    pd = p.shape[-1]
    w_in_t, w_ab_t, w_gout, w_q, w_sout, wf_t, w_fout, wp_t, w_pg = {}, {}, {}, {}, {}, {}, {}, {}, {}
    wkv_t = None

    h = x[0]
    sv = []
    kv_sv = None
    k_sh = v_sh = None
    for l in range(depth):
        t = {}
        t["h0"] = h
        if l == 0:
            hn = _rms_fwd(h, ln_mix[l], f"rms_mix_{l}")
        t["hn"] = hn
        fetch(f"a{l}", token if l == 0 else hn)
        if l < n_a:
            wt = whole(("gdn_w_in", l), win_cols)
            w_in_t[l] = wt[:4 * d]
            w_ab_t[l] = jnp.pad(wt[4 * d:], ((0, HEAD - 2 * nh), (0, 0)))
            w_gout[l] = whole(("gdn_w_out", l))
        else:
            w_q[l - n_a] = whole(("sb_w_q", l - n_a))
            w_sout[l - n_a] = whole(("sb_w_out", l - n_a))
        if l < n_a:
            proj = _mm(hn, w_in_t[l], "nt", f"gdn_proj_{l}")
            pab = _mm(hn, w_ab_t[l], "nt", f"gdn_proj_ab_{l}")
            qkv = _conv_fwd(proj, conv_full[l], d, f"gdn_conv_{l}")
            al, dtb = _pad_lanes(gdn_a_log[l]), _pad_lanes(gdn_dt_bias[l])
            gb = _gates_fwd(pab, al, dtb, nh, f"gdn_gates_{l}")
            o_raw, states = _gdn_fwd(qkv, gb, nh, f"gdn_rule_{l}")
            o2 = _headnorm_fwd(o_raw, gdn_norm[l], f"gdn_outnorm_{l}", gate=proj, gate_col0=3 * d, head_major=True)
            h, hn2 = _mm(o2, w_gout[l], "nn", f"gdn_out_{l}", res=h, norm_g=ln_ffn[l])
            t.update(proj=proj, pab=pab, qkv=qkv, gb=gb, o_raw=o_raw, states=states, o2=o2, al=al, dtb=dtb)
        else:
            j = l - n_a
            qpre = _mm(hn, w_q[j], "nn", f"sb_qproj_{j}")
            qn = _headnorm_fwd(qpre, sb_q_norm[j], f"sb_qnorm_{j}", scale=HEAD ** -0.5)
            o, ctab = _sb_fwd(qn, k_sh, v_sh, f"sb_attn_{j}")
            h, hn2 = _mm(o, w_sout[j], "nn", f"sb_out_{j}", res=h, norm_g=ln_ffn[l])
            t.update(qpre=qpre, qn=qn, o=o, ctab=ctab)
        t["h1"] = h
        fetch(f"f{l}", hn2)
        wf_t[l] = whole(("ffn_w_in", l))
        w_fout[l] = whole(("ffn_w_out", l))
        wp_t[l] = full[("ple_w_proj", l)].reshape(d, pd)
        w_pg[l] = whole(("ple_w_gate", l))
        if l == n_a - 1:
            wkv_t = whole(("w_kv", 0))
        act, gs, us = _swiglu_fwd(hn2, wf_t[l], f"ffn_in_{l}")
        h, hn3 = _mm(act, w_fout[l], "nn", f"ffn_out_{l}", res=h, norm_g=ln_ple[l])
        t.update(hn2=hn2, act=act, gs=gs, us=us, h2=h)
        gains = ([ln_mix[l + 1]] if l + 1 < depth else []) + ([kv_norm] if l == n_a - 1 else [])
        h, gpre, pp, *normed = _ple_fwd(h, hn3, p[l, 0], w_pg[l], wp_t[l], f"ple_{l}", norm_gs=gains)
        if l + 1 < depth:
            hn = normed[0]
        t.update(hn3=hn3, gpre=gpre, pp=pp)
        sv.append(t)
        if l == n_a - 1:
            kvn = normed[-1]
            kv = _mm(kvn, wkv_t, "nt", "kv_proj")
            k_sh = _headnorm_fwd(kv, k_norm, "k_norm", width=d)
            v_sh = kv[:, d:].astype(BF16)
            kv_sv = dict(h=h, kvn=kvn, kv=kv)

    dh, loss_vec = _loss_fwd_bwd(h, loss_target[0], "loss")
    loss = lax.psum(jnp.sum(loss_vec), ("x", "y", "c"))

    gw = {}
    small = {}
    g_started = {}

    def scatter_start(name):
        gparts = []
        for key in chunk_keys[name]:
            g = gw[key]
            g = g.reshape(NDEV, -1, d) if key[0] == "ple_w_proj" else g.reshape(NDEV, -1, g.shape[-1])
            padr = local[key].shape[0] - g.shape[1]
            if padr:
                g = jnp.pad(g, ((0, 0), (0, padr), (0, 0)))
            gparts.append(g.reshape(-1, d))
        g_started[name] = _send_start(gparts, True, gparts[0], f"comm_gstart_{name}")
        return g_started[name]["token"]

    dk_sh = jnp.zeros((s, d), F32)
    dv_sh = jnp.zeros((s, d), F32)
    for l in reversed(range(depth)):
        t = sv[l]
        if l == n_a - 1:
            dkv_k, dkn = _headnorm_bwd(dk_sh, kv_sv["kv"], k_norm, "k_norm_bwd", dx_dtype=BF16)
            dkv = jnp.concatenate([dkv_k, dv_sh.astype(BF16)], axis=1)
            gw[("w_kv", 0)] = _mm(dkv, kv_sv["kvn"], "tn", "kv_dw", out_dtype=BF16)
            dh, _, dg = _mm(dkv, wkv_t, "nn", "kv_dx", norm_bwd=(kv_sv["h"], kv_norm, dh))
            small["kv_norm"] = dg
            small["k_norm"] = dkn
        dgp, dpp = _ple_bwd(dh, t["gpre"], t["pp"], f"ple_bwd_{l}")
        gw[("ple_w_gate", l)] = _mm(t["hn3"], dgp, "tn", f"ple_dwg_{l}", out_dtype=BF16)
        gw[("ple_w_proj", l)] = _mm(dpp, p[l, 0], "tn", f"ple_dwp_{l}", out_dtype=BF16)
        dh, dhb, dg = _mm(dgp, w_pg[l], "nt", f"ple_dx_{l}", norm_bwd=(t["h2"], ln_ple[l], dh))
        small[("ln_ple", l)] = dg
        dgs, dus = _swiglu_bwd(dhb, w_fout[l], t["gs"], t["us"], f"ffn_bwd_act_{l}")
        gw[("ffn_w_out", l)] = _mm(t["act"], dhb, "tn", f"ffn_dwo_{l}", out_dtype=BF16)
        f = dgs.shape[1]
        dwg = _mm(dgs, t["hn2"], "tn", f"ffn_dwg_{l}", out_dtype=BF16)
        dwu = _mm(dus, t["hn2"], "tn", f"ffn_dwu_{l}", out_dtype=BF16)
        gw[("ffn_w_in", l)] = jnp.concatenate([dwg, dwu], axis=0)
        dhn2 = _mm(dgs, wf_t[l], "nn", f"ffn_dxg_{l}")
        dh, dhb, dg = _mm(dus, wf_t[l], "nn", f"ffn_dxu_{l}", res=dhn2, norm_bwd=(t["h1"], ln_ffn[l], dh),
                          after=scatter_start(f"f{l}"), b_row0=f)
        small[("ln_ffn", l)] = dg
        if l < n_a:
            do2 = _mm(dhb, w_gout[l], "nt", f"gdn_out_dx_{l}")
            gw[("gdn_w_out", l)] = _mm(t["o2"], dhb, "tn", f"gdn_out_dw_{l}", out_dtype=BF16)
            do_raw, dgn, dgate = _headnorm_bwd(do2, t["o_raw"], gdn_norm[l], f"gdn_outnorm_bwd_{l}",
                                               gate=t["proj"], gate_col0=3 * d, head_major=True)
            small[("gdn_norm", l)] = dgn
            dqkv, dgb = _gdn_bwd(t["qkv"], t["gb"], do_raw, t["states"], nh, f"gdn_rule_bwd_{l}")
            dpab, dal, ddt = _gates_bwd(dgb, t["pab"], t["al"], t["dtb"], nh, f"gdn_gates_bwd_{l}")
            small[("gdn_a_log", l)] = dal
            small[("gdn_dt_bias", l)] = ddt
            dproj_qkv, dconv = _conv_bwd(dqkv, t["proj"], conv_full[l], d, f"gdn_conv_bwd_{l}")
            small[("gdn_conv", l)] = dconv
            dproj = jnp.concatenate([dproj_qkv, dgate], axis=1)
            dw_main = _mm(dproj, t["hn"], "tn", f"gdn_proj_dw_{l}", out_dtype=BF16)
            dw_ab = _mm(dpab, t["hn"], "tn", f"gdn_proj_ab_dw_{l}", out_dtype=BF16)
            gw[("gdn_w_in", l)] = jnp.concatenate([dw_main, dw_ab[:16]], axis=0)[:win_rows]
            dhn_ab = _mm(dpab, w_ab_t[l], "nn", f"gdn_proj_ab_dx_{l}")
            last = dict(a=dproj, b=w_in_t[l], mode="nn", name=f"gdn_proj_dx_{l}", res=dhn_ab)
        else:
            j = l - n_a
            do = _mm(dhb, w_sout[j], "nt", f"sb_out_dx_{j}", out_dtype=BF16)
            gw[("sb_w_out", j)] = _mm(t["o"], dhb, "tn", f"sb_out_dw_{j}", out_dtype=BF16)
            dq, dk, dv = _sb_bwd(t["qn"], k_sh, v_sh, do, t["ctab"], f"sb_attn_bwd_{j}")
            dk_sh = dk_sh + dk
            dv_sh = dv_sh + dv
            dqpre, dqn = _headnorm_bwd(dq, t["qpre"], sb_q_norm[j], f"sb_qnorm_bwd_{j}", scale=HEAD ** -0.5, dx_dtype=BF16)
            small[("sb_q_norm", j)] = dqn
            gw[("sb_w_q", j)] = _mm(t["hn"], dqpre, "tn", f"sb_q_dw_{j}", out_dtype=BF16)
            last = dict(a=dqpre, b=w_q[j], mode="nt", name=f"sb_q_dx_{j}")
        dh, _, dg = _mm(**last, norm_bwd=(t["h0"], ln_mix[l], dh), after=scatter_start(f"a{l}"))
        small[("ln_mix", l)] = dg
    grad_x = dh[None]

    landed = {}
    for name, keys in reversed(chunks):
        lands = _send_wait(g_started[name], True, dh, f"comm_gwait_{name}")
        for key, land in zip(keys, lands):
            landed[key] = land.reshape(NDEV, -1, d)

    def summed(wname, count, rows_out=None):
        return _sum_slots([landed[(wname, i)] for i in range(count)], f"grad_sum_{wname}", rows_out)

    gt_gdn_w_in = summed("gdn_w_in", n_a, win_cols)
    gt_ffn_w_in = summed("ffn_w_in", depth)
    g_gdn_w_in = jnp.transpose(gt_gdn_w_in, (0, 2, 1))
    g_gdn_w_out = summed("gdn_w_out", n_a)
    g_w_kv = jnp.transpose(summed("w_kv", 1)[0])
    g_sb_w_q = summed("sb_w_q", n_b)
    g_sb_w_out = summed("sb_w_out", n_b)
    g_ffn_w_in = jnp.transpose(gt_ffn_w_in, (0, 2, 1))
    g_ffn_w_out = summed("ffn_w_out", depth)
    g_ple_w_proj = jnp.transpose(summed("ple_w_proj", depth).reshape(depth, -1, pd), (0, 2, 1))
    g_ple_w_gate = summed("ple_w_gate", depth)

    def vec_rows(v):
        return v.reshape(-1, HEAD)

    small_items = []
    for name_, cnt in (("ln_mix", depth), ("ln_ffn", depth), ("ln_ple", depth)):
        for l in range(cnt):
            small_items.append(((name_, l), vec_rows(small[(name_, l)])))
    for l in range(n_a):
        small_items.append((("gdn_conv", l), small[("gdn_conv", l)].reshape(-1, HEAD)))
        small_items.append((("gdn_a_log", l), small[("gdn_a_log", l)]))
        small_items.append((("gdn_dt_bias", l), small[("gdn_dt_bias", l)]))
        small_items.append((("gdn_norm", l), small[("gdn_norm", l)]))
    small_items.append(("kv_norm", vec_rows(small["kv_norm"])))
    small_items.append(("k_norm", small["k_norm"]))
    for j in range(n_b):
        small_items.append((("sb_q_norm", j), small[("sb_q_norm", j)]))
    spack = jnp.concatenate([_pad_rows(a, 8) for _, a in small_items], axis=0)
    sg = _all_gather(spack, "comm_gather_small").reshape(NDEV, spack.shape[0], HEAD)
    ssum = _sum_slots([sg], "small_sum")[0]
    sm = {}
    off = 0
    for key, a in small_items:
        sm[key] = ssum[off:off + a.shape[0]]
        off += a.shape[0] + (-a.shape[0]) % 8

    g_ln_mix = jnp.stack([sm[("ln_mix", l)].reshape(d) for l in range(depth)])
    g_ln_ffn = jnp.stack([sm[("ln_ffn", l)].reshape(d) for l in range(depth)])
    g_ln_ple = jnp.stack([sm[("ln_ple", l)].reshape(d) for l in range(depth)])
    conv_loc = gdn_conv.shape[2]
    g_conv_full = jnp.stack([sm[("gdn_conv", l)].reshape(gdn_conv.shape[1], 3 * d) for l in range(n_a)])
    g_gdn_conv = lax.dynamic_slice_in_dim(g_conv_full, me * conv_loc, conv_loc, axis=2)
    g_a_log = jnp.stack([sm[("gdn_a_log", l)][0, :nh] for l in range(n_a)])
    g_dt_bias = jnp.stack([sm[("gdn_dt_bias", l)][0, :nh] for l in range(n_a)])
    g_gdn_norm = jnp.stack([sm[("gdn_norm", l)][0] for l in range(n_a)])
    g_kv_norm = sm["kv_norm"].reshape(d)
    g_k_norm = sm["k_norm"][0]
    g_sb_q_norm = jnp.stack([sm[("sb_q_norm", j)][0] for j in range(n_b)])

    grads = [g_ln_mix, g_ln_ffn, g_ln_ple, g_gdn_w_in, g_gdn_conv, g_a_log, g_dt_bias, g_gdn_norm, g_gdn_w_out,
             g_kv_norm, g_w_kv, g_k_norm, g_sb_w_q, g_sb_q_norm, g_sb_w_out, g_ffn_w_in, g_ffn_w_out, g_ple_w_proj,
             g_ple_w_gate]
    weights = [ln_mix, ln_ffn, ln_ple, gdn_w_in, gdn_conv, gdn_a_log, gdn_dt_bias, gdn_norm, gdn_w_out, kv_norm, w_kv,
               k_norm, sb_w_q, sb_q_norm, sb_w_out, ffn_w_in, ffn_w_out, ple_w_proj, ple_w_gate]
    moms = [m_ln_mix, m_ln_ffn, m_ln_ple, m_gdn_w_in, m_gdn_conv, m_gdn_a_log, m_gdn_dt_bias, m_gdn_norm, m_gdn_w_out,
            m_kv_norm, m_w_kv, m_k_norm, m_sb_w_q, m_sb_q_norm, m_sb_w_out, m_ffn_w_in, m_ffn_w_out, m_ple_w_proj,
            m_ple_w_gate]
    vels = [v_ln_mix, v_ln_ffn, v_ln_ple, v_gdn_w_in, v_gdn_conv, v_gdn_a_log, v_gdn_dt_bias, v_gdn_norm, v_gdn_w_out,
            v_kv_norm, v_w_kv, v_k_norm, v_sb_w_q, v_sb_q_norm, v_sb_w_out, v_ffn_w_in, v_ffn_w_out, v_ple_w_proj,
            v_ple_w_gate]

    deltas, new_m, new_v = [], [], []
    small_idx = [i for i, w in enumerate(weights) if w.size < 8 * HEAD * 16]
    transposed = {3: gt_gdn_w_in, 15: gt_ffn_w_in}
    for i, (w, g, m, v) in enumerate(zip(weights, grads, moms, vels)):
        if i in small_idx:
            deltas.append(None), new_m.append(None), new_v.append(None)
            continue
        if i in transposed:
            tr = lambda a: jnp.transpose(a, (0, 2, 1))
            dl, nm, nv = _adamw(tr(w), transposed[i], tr(m), tr(v), f"adamw_{i}")
            deltas.append(tr(dl)), new_m.append(tr(nm)), new_v.append(tr(nv))
            continue
        shp = w.shape
        two = lambda a: a.reshape(-1, shp[-1])
        dl, nm, nv = _adamw(two(w), two(g), two(m), two(v), f"adamw_{i}")
        deltas.append(dl.reshape(shp)), new_m.append(nm.reshape(shp)), new_v.append(nv.reshape(shp))

    def flat_pack(arrs):
        flat = jnp.concatenate([a.reshape(-1) for a in arrs])
        pad = (-flat.shape[0]) % (8 * HEAD)
        return jnp.pad(flat, (0, pad)).reshape(-1, HEAD)

    sw = flat_pack([weights[i] for i in small_idx])
    sgr = flat_pack([grads[i] for i in small_idx])
    smo = flat_pack([moms[i] for i in small_idx])
    sve = flat_pack([vels[i] for i in small_idx])
    sdl, snm, snv = _adamw(sw, sgr, smo, sve, "adamw_small")
    off = 0
    for i in small_idx:
        n = weights[i].size
        shp = weights[i].shape
        deltas[i] = sdl.reshape(-1)[off:off + n].reshape(shp)
        new_m[i] = snm.reshape(-1)[off:off + n].reshape(shp)
        new_v[i] = snv.reshape(-1)[off:off + n].reshape(shp)
        off += n

    return (loss, grad_x, *grads, *deltas, *new_m, *new_v)
```

```python
import math

import jax
import jax.numpy as jnp
from jax import lax
from jax.experimental import pallas as pl
from jax.experimental.pallas import tpu as pltpu

F32 = jnp.float32
BF16 = jnp.bfloat16
NDEV = 8
HEAD = 128
CHUNK = 64
SBLK = 256
EPS = 1e-6
LR, B1, B2, ADAM_EPS, WD, STEP = 0.001, 0.9, 0.999, 1e-08, 0.01, 10
NEG = -1e30
MM_VMEM_BUDGET = 40 * 1024 * 1024

NN = (((1,), (0,)), ((), ()))
NT = (((1,), (1,)), ((), ()))
TN = (((0,), (0,)), ((), ()))
BNN = (((2,), (1,)), ((0,), (0,)))
BNT = (((2,), (2,)), ((0,), (0,)))
BTN = (((1,), (1,)), ((0,), (0,)))
MESH = pl.DeviceIdType.MESH


def _dot(a, b, dims=NN):
    return lax.dot_general(a.astype(BF16), b.astype(BF16), dims, preferred_element_type=F32)


def _dot_hilo(a, b01, dims=NN):
    hi = a.astype(BF16)
    lo = (a - hi.astype(F32)).astype(BF16)
    return (lax.dot_general(hi, b01, dims, preferred_element_type=F32)
            + lax.dot_general(lo, b01, dims, preferred_element_type=F32))


def _pick(dim, cands):
    for c in cands:
        if dim % c == 0:
            return c
    return dim


def _params(sem, vmem_mb=48):
    return pltpu.CompilerParams(dimension_semantics=sem, vmem_limit_bytes=vmem_mb * 1024 * 1024)


def _silu(x):
    return x * jax.nn.sigmoid(x)


def _silu_and_grad(x):
    s = jax.nn.sigmoid(x)
    xs = x * s
    return xs, s + xs * (1.0 - s)


def _mm(a, b, mode, name, out_dtype=F32, res=None, norm_g=None, norm_bwd=None, after=None):
    if mode == "nn":
        (m, k), n = a.shape, b.shape[1]
    elif mode == "nt":
        (m, k), n = a.shape, b.shape[0]
    else:
        (k, m), n = a.shape, b.shape[1]
    rows = norm_g is not None or norm_bwd is not None
    tn = n if rows else _pick(n, (512, 256, 128))
    tk = k if k <= 4096 else max(t for t in range(128, 4097, 128) if k % t == 0)
    nk = k // tk
    out_b = jnp.dtype(out_dtype).itemsize + (res.dtype.itemsize if res is not None else 0)
    out_b += 2 if norm_g is not None else 0
    out_b += 10 if norm_bwd is not None else 0
    for tm in [t for t in range(min(m, 2048), 127, -128) if m % t == 0] + [m]:
        need = 2 * (tm * tk * a.dtype.itemsize + tk * tn * b.dtype.itemsize + tm * tn * out_b) + 4 * tm * tn
        if need <= MM_VMEM_BUDGET:
            break
    dims = {"nn": NN, "nt": NT, "tn": TN}[mode]
    if mode == "tn":
        a_spec = pl.BlockSpec((tk, tm), lambda i, j, kk: (kk, i))
    else:
        a_spec = pl.BlockSpec((tm, tk), lambda i, j, kk: (i, kk))
    if mode == "nt":
        b_spec = pl.BlockSpec((tn, tk), lambda i, j, kk: (j, kk))
    else:
        b_spec = pl.BlockSpec((tk, tn), lambda i, j, kk: (kk, j))
    mn_spec = pl.BlockSpec((tm, tn), lambda i, j, kk: (i, j))
    vec_spec = pl.BlockSpec((1, tn), lambda i, j, kk: (0, j))
    has_res = res is not None
    n_in = 2 + has_res + (1 if norm_g is not None else 0) + (3 if norm_bwd is not None else 0) + (after is not None)

    def body(*refs):
        a_ref, b_ref = refs[:2]
        extra = list(refs[2:n_in])
        outs = refs[n_in:-1]
        acc = refs[-1]
        kk = pl.program_id(2)

        @pl.when(kk == 0)
        def _():
            acc[...] = jnp.zeros_like(acc)

        if norm_bwd is not None:
            @pl.when((kk == 0) & (pl.program_id(0) == 0))
            def _():
                outs[2][...] = jnp.zeros_like(outs[2])

        acc[...] += _dot(a_ref[...], b_ref[...], dims)

        @pl.when(kk == nk - 1)
        def _():
            r = acc[...]
            if has_res:
                r = r + extra.pop(0)[...].astype(F32)
            if norm_g is not None:
                outs[0][...] = r.astype(out_dtype)
                rs = lax.rsqrt(jnp.mean(r * r, axis=-1, keepdims=True) + EPS)
                outs[1][...] = (r * rs * extra.pop(0)[...]).astype(BF16)
            elif norm_bwd is not None:
                xv, gv, dres = extra.pop(0)[...], extra.pop(0)[...], extra.pop(0)[...]
                rs = lax.rsqrt(jnp.mean(xv * xv, axis=-1, keepdims=True) + EPS)
                gdy = r * gv
                dx = dres + rs * gdy - xv * (rs * rs * rs) * jnp.mean(xv * gdy, axis=-1, keepdims=True)
                outs[0][...] = dx
                outs[1][...] = dx.astype(BF16)
                outs[2][...] += jnp.sum(r * xv * rs, axis=0, keepdims=True)
            else:
                outs[0][...] = r.astype(out_dtype)

    ins = [a, b] + ([res] if has_res else [])
    in_specs = [a_spec, b_spec] + ([mn_spec] if has_res else [])
    out_specs, out_shape = [mn_spec], [jax.ShapeDtypeStruct((m, n), out_dtype)]
    sem = ("parallel", "parallel", "arbitrary")
    if norm_g is not None:
        ins.append(norm_g.reshape(1, n))
        in_specs.append(vec_spec)
        out_specs.append(mn_spec)
        out_shape.append(jax.ShapeDtypeStruct((m, n), BF16))
    if norm_bwd is not None:
        x, g, dres = norm_bwd
        ins += [x, g.reshape(1, n), dres]
        in_specs += [mn_spec, vec_spec, mn_spec]
        out_specs += [mn_spec, vec_spec]
        out_shape += [jax.ShapeDtypeStruct((m, n), BF16), jax.ShapeDtypeStruct((1, n), F32)]
        sem = ("arbitrary", "arbitrary", "arbitrary")
    if after is not None:
        ins.append(after)
        in_specs.append(pl.BlockSpec(memory_space=pl.ANY))
    out = pl.pallas_call(
        body, name=name, grid=(m // tm, n // tn, nk), in_specs=in_specs, out_specs=out_specs,
        out_shape=out_shape, scratch_shapes=[pltpu.VMEM((tm, tn), F32)],
        compiler_params=_params(sem))(*ins)
    return out[0] if len(out) == 1 else out


def _rms_fwd(h, g, name):
    s, d = h.shape
    tm = _pick(s, (512, 256, 128))

    def body(h_ref, g_ref, o_ref):
        x = h_ref[...]
        r = lax.rsqrt(jnp.mean(x * x, axis=-1, keepdims=True) + EPS)
        o_ref[...] = (x * r * g_ref[...]).astype(BF16)

    return pl.pallas_call(
        body, name=name, grid=(s // tm,),
        in_specs=[pl.BlockSpec((tm, d), lambda i: (i, 0)), pl.BlockSpec((1, d), lambda i: (0, 0))],
        out_specs=pl.BlockSpec((tm, d), lambda i: (i, 0)),
        out_shape=jax.ShapeDtypeStruct((s, d), BF16), compiler_params=_params(("parallel",)))(h, g.reshape(1, d))


def _headnorm_fwd(x, g, name, scale=1.0, gate=None, gate_col0=0, out_dtype=BF16, width=None, head_major=False):
    if head_major:
        s, d = x.shape[1], x.shape[0] * HEAD
    else:
        s, d = x.shape[0], (width or x.shape[1])
    nh = d // HEAD
    tm = _pick(s, (256, 128))
    has_gate = gate is not None
    gb = gate_col0 // d

    def body(*refs):
        if has_gate:
            x_ref, g_ref, gt_ref, o_ref = refs
        else:
            x_ref, g_ref, o_ref = refs
        gv = g_ref[...]
        for h in range(nh):
            sl = slice(h * HEAD, (h + 1) * HEAD)
            xv = (x_ref[h] if head_major else x_ref[:, sl]).astype(F32)
            r = lax.rsqrt(jnp.mean(xv * xv, axis=-1, keepdims=True) + EPS)
            y = xv * r * gv
            if scale != 1.0:
                y = y * scale
            if has_gate:
                y = y * _silu(gt_ref[:, sl])
            o_ref[:, sl] = y.astype(out_dtype)

    row = pl.BlockSpec((tm, d), lambda i: (i, 0))
    hm = pl.BlockSpec((nh, tm, HEAD), lambda i: (0, i, 0))
    ins = [x, g.reshape(1, HEAD)]
    in_specs = [hm if head_major else row, pl.BlockSpec((1, HEAD), lambda i: (0, 0))]
    if has_gate:
        ins.append(gate)
        in_specs.append(pl.BlockSpec((tm, d), lambda i: (i, gb)))
    return pl.pallas_call(
        body, name=name, grid=(s // tm,), in_specs=in_specs, out_specs=row,
        out_shape=jax.ShapeDtypeStruct((s, d), out_dtype), compiler_params=_params(("parallel",)))(*ins)


def _headnorm_bwd(dy, x, g, name, scale=1.0, gate=None, gate_col0=0, dx_dtype=F32, head_major=False):
    s, d = dy.shape
    nh = d // HEAD
    tm = _pick(s, (256, 128))
    has_gate = gate is not None
    gb = gate_col0 // d

    def body(*refs):
        if has_gate:
            dy_ref, x_ref, g_ref, gt_ref, dx_ref, dg_ref, dgt_ref = refs
        else:
            dy_ref, x_ref, g_ref, dx_ref, dg_ref = refs

        @pl.when(pl.program_id(0) == 0)
        def _():
            dg_ref[...] = jnp.zeros_like(dg_ref)

        gv = g_ref[...]
        dg_acc = jnp.zeros((1, HEAD), F32)
        for h in range(nh):
            sl = slice(h * HEAD, (h + 1) * HEAD)
            xv = (x_ref[h] if head_major else x_ref[:, sl]).astype(F32)
            dyv = dy_ref[:, sl].astype(F32)
            r = lax.rsqrt(jnp.mean(xv * xv, axis=-1, keepdims=True) + EPS)
            if has_gate:
                gt = gt_ref[:, sl]
                act, dact = _silu_and_grad(gt)
                dgt_ref[:, sl] = (dyv * (xv * r * gv) * dact).astype(dgt_ref.dtype)
                dn = dyv * act
            else:
                dn = dyv
            if scale != 1.0:
                dn = dn * scale
            gdn = dn * gv
            mean_t = jnp.mean(xv * gdn, axis=-1, keepdims=True)
            dxv = (r * gdn - xv * (r * r * r) * mean_t).astype(dx_dtype)
            if head_major:
                dx_ref[h] = dxv
            else:
                dx_ref[:, sl] = dxv
            dg_acc = dg_acc + jnp.sum(dn * xv * r, axis=0, keepdims=True)
        dg_ref[...] += dg_acc

    row = pl.BlockSpec((tm, d), lambda i: (i, 0))
    hm = pl.BlockSpec((nh, tm, HEAD), lambda i: (0, i, 0))
    vec = pl.BlockSpec((1, HEAD), lambda i: (0, 0))
    ins = [dy, x, g.reshape(1, HEAD)]
    in_specs = [row, hm if head_major else row, vec]
    out_specs = [hm if head_major else row, vec]
    dx_shape = (nh, s, HEAD) if head_major else (s, d)
    out_shape = [jax.ShapeDtypeStruct(dx_shape, dx_dtype), jax.ShapeDtypeStruct((1, HEAD), F32)]
    if has_gate:
        ins.append(gate)
        in_specs.append(pl.BlockSpec((tm, d), lambda i: (i, gb)))
        out_specs.append(row)
        out_shape.append(jax.ShapeDtypeStruct((s, d), BF16))
    return pl.pallas_call(
        body, name=name, grid=(s // tm,), in_specs=in_specs, out_specs=out_specs, out_shape=out_shape,
        compiler_params=_params(("arbitrary",)))(*ins)


def _swiglu_fwd(hn, wf_t, name):
    s, d = hn.shape
    f = wf_t.shape[0] // 2
    tm = _pick(s, (1024, 512, 256, 128))
    tn = _pick(f, (512, 256, 128))
    nj = f // tn

    def body(a_ref, wg_ref, wu_ref, act_ref, g_ref, u_ref):
        a = a_ref[...]
        g = _dot(a, wg_ref[...], NT)
        u = _dot(a, wu_ref[...], NT)
        act_ref[...] = (_silu(g) * u).astype(BF16)
        g_ref[...] = g.astype(BF16)
        u_ref[...] = u.astype(BF16)

    o_spec = pl.BlockSpec((tm, tn), lambda i, j: (i, j))
    sds = jax.ShapeDtypeStruct((s, f), BF16)
    return pl.pallas_call(
        body, name=name, grid=(s // tm, nj),
        in_specs=[pl.BlockSpec((tm, d), lambda i, j: (i, 0)), pl.BlockSpec((tn, d), lambda i, j: (j, 0)),
                  pl.BlockSpec((tn, d), lambda i, j: (j + nj, 0))],
        out_specs=[o_spec, o_spec, o_spec], out_shape=[sds, sds, sds],
        compiler_params=_params(("parallel", "parallel")))(hn, wf_t, wf_t)


def _swiglu_bwd(dh, w_out, g, u, name):
    s, d = dh.shape
    f = w_out.shape[0]
    tm = _pick(s, (1024, 512, 256, 128))
    tn = _pick(f, (512, 256, 128))

    def body(dh_ref, w_ref, g_ref, u_ref, dgu_ref):
        j = pl.program_id(1)
        dact = _dot(dh_ref[...], w_ref[...], NT)
        gv = g_ref[...].astype(F32)
        uv = u_ref[...].astype(F32)
        sg, dsg = _silu_and_grad(gv)
        dgu_ref[:, pl.ds(pl.multiple_of(j * tn, HEAD), tn)] = (dact * uv * dsg).astype(BF16)
        dgu_ref[:, pl.ds(pl.multiple_of(f + j * tn, HEAD), tn)] = (dact * sg).astype(BF16)

    o_spec = pl.BlockSpec((tm, tn), lambda i, j: (i, j))
    return pl.pallas_call(
        body, name=name, grid=(s // tm, f // tn),
        in_specs=[pl.BlockSpec((tm, d), lambda i, j: (i, 0)), pl.BlockSpec((tn, d), lambda i, j: (j, 0)), o_spec, o_spec],
        out_specs=pl.BlockSpec((tm, 2 * f), lambda i, j: (i, 0)), out_shape=jax.ShapeDtypeStruct((s, 2 * f), BF16),
        compiler_params=_params(("parallel", "arbitrary")))(dh, w_out, g, u)


def _ple_fwd(h, hn, p, w_gate, wp_t, name, norm_gs=()):
    s, d = h.shape
    pd = p.shape[1]
    tm = _pick(s, (512, 256, 128))
    ng = len(norm_gs)

    def body(h_ref, hn_ref, p_ref, wg_ref, wp_ref, *rest):
        g_refs, (o_ref, gp_ref, pp_ref), n_refs = rest[:ng], rest[ng:ng + 3], rest[ng + 3:]
        gpre = _dot(hn_ref[...], wg_ref[...], NN)
        pp = _dot(p_ref[...], wp_ref[...], NT)
        o = h_ref[...] + pp * jax.nn.sigmoid(gpre)
        o_ref[...] = o
        gp_ref[...] = gpre.astype(BF16)
        pp_ref[...] = pp.astype(BF16)
        if ng:
            on = o * lax.rsqrt(jnp.mean(o * o, axis=-1, keepdims=True) + EPS)
            for g_ref, n_ref in zip(g_refs, n_refs):
                n_ref[...] = (on * g_ref[...]).astype(BF16)

    row = pl.BlockSpec((tm, d), lambda i: (i, 0))
    vec = pl.BlockSpec((1, d), lambda i: (0, 0))
    bf = jax.ShapeDtypeStruct((s, d), BF16)
    return pl.pallas_call(
        body, name=name, grid=(s // tm,),
        in_specs=[row, row, pl.BlockSpec((tm, pd), lambda i: (i, 0)), pl.BlockSpec((d, d), lambda i: (0, 0)),
                  pl.BlockSpec((d, pd), lambda i: (0, 0))] + [vec] * ng,
        out_specs=[row] * (3 + ng), out_shape=[jax.ShapeDtypeStruct((s, d), F32), bf, bf] + [bf] * ng,
        compiler_params=_params(("parallel",)))(h, hn, p, w_gate, wp_t, *[g.reshape(1, d) for g in norm_gs])


def _ple_bwd(dh, gpre, pp, name):
    s, d = dh.shape
    tm = _pick(s, (512, 256, 128))

    def body(dh_ref, gp_ref, pp_ref, dgp_ref, dpp_ref):
        dv = dh_ref[...]
        sig = jax.nn.sigmoid(gp_ref[...].astype(F32))
        ppv = pp_ref[...].astype(F32)
        dpp_ref[...] = (dv * sig).astype(BF16)
        dgp_ref[...] = (dv * ppv * sig * (1.0 - sig)).astype(BF16)

    row = pl.BlockSpec((tm, d), lambda i: (i, 0))
    sds = jax.ShapeDtypeStruct((s, d), BF16)
    return pl.pallas_call(
        body, name=name, grid=(s // tm,), in_specs=[row, row, row], out_specs=[row, row], out_shape=[sds, sds],
        compiler_params=_params(("parallel",)))(dh, gpre, pp)


def _loss_fwd_bwd(y, t, name):
    s, d = y.shape
    tm = _pick(s, (512, 256, 128))

    def body(y_ref, t_ref, dy_ref, l_ref):
        @pl.when(pl.program_id(0) == 0)
        def _():
            l_ref[...] = jnp.zeros_like(l_ref)

        e = y_ref[...] - t_ref[...]
        dy_ref[...] = e * (1.0 / d)
        l_ref[...] += jnp.sum(e * e, axis=0, keepdims=True) * (0.5 / d)

    row = pl.BlockSpec((tm, d), lambda i: (i, 0))
    vec = pl.BlockSpec((1, d), lambda i: (0, 0))
    return pl.pallas_call(
        body, name=name, grid=(s // tm,), in_specs=[row, row], out_specs=[row, vec],
        out_shape=[jax.ShapeDtypeStruct((s, d), F32), jax.ShapeDtypeStruct((1, d), F32)],
        compiler_params=_params(("arbitrary",)))(y, t)


PADR = 8


def _conv_fwd(proj, w_conv, d, name):
    s = proj.shape[0]
    nh = d // HEAD
    kw = w_conv.shape[0]
    qscale = HEAD ** -0.5

    def body(x_ref, w_ref, o_ref, xp):
        kind = pl.program_id(0) // nh
        xp[0:PADR, :] = jnp.zeros((PADR, HEAD), F32)
        xp[PADR:, :] = x_ref[...]
        acc = jnp.zeros((s, HEAD), F32)
        for j in range(kw):
            acc = acc + w_ref[j:j + 1, :] * xp[PADR - (kw - 1) + j:PADR - (kw - 1) + j + s, :]
        a = _silu(acc)
        r = lax.rsqrt(jnp.sum(a * a, axis=-1, keepdims=True) + EPS)
        fac = jnp.where(kind == 0, r * qscale, jnp.where(kind == 1, r, jnp.ones_like(r)))
        o_ref[...] = a * fac

    blk = pl.BlockSpec((s, HEAD), lambda c: (0, c))
    hm = pl.BlockSpec((None, s, HEAD), lambda c: (c, 0, 0))
    return pl.pallas_call(
        body, name=name, grid=(3 * nh,), in_specs=[blk, pl.BlockSpec((kw, HEAD), lambda c: (0, c))], out_specs=hm,
        out_shape=jax.ShapeDtypeStruct((3 * nh, s, HEAD), F32), scratch_shapes=[pltpu.VMEM((s + PADR, HEAD), F32)],
        compiler_params=_params(("parallel",)))(proj, w_conv)


def _conv_bwd(dqkv, proj, w_conv, d, name):
    s = proj.shape[0]
    nh = d // HEAD
    kw = w_conv.shape[0]
    qscale = HEAD ** -0.5

    def body(dy_ref, x_ref, w_ref, dx_ref, dw_ref, xp, dp):
        kind = pl.program_id(0) // nh
        xp[0:PADR, :] = jnp.zeros((PADR, HEAD), F32)
        xp[PADR:, :] = x_ref[...]
        acc = jnp.zeros((s, HEAD), F32)
        for j in range(kw):
            acc = acc + w_ref[j:j + 1, :] * xp[PADR - (kw - 1) + j:PADR - (kw - 1) + j + s, :]
        a, da_dacc = _silu_and_grad(acc)
        dy = dy_ref[...]
        r = lax.rsqrt(jnp.sum(a * a, axis=-1, keepdims=True) + EPS)
        sc = jnp.where(kind == 0, qscale, 1.0)
        dyn = dy * sc
        da_norm = r * dyn - a * (r * r * r) * jnp.sum(a * dyn, axis=-1, keepdims=True)
        da = jnp.where(kind == 2, dy, da_norm)
        dacc = da * da_dacc
        dp[0:s, :] = dacc
        dp[s:, :] = jnp.zeros((PADR, HEAD), F32)
        dx = jnp.zeros((s, HEAD), F32)
        for j in range(kw):
            sh = kw - 1 - j
            dx = dx + w_ref[j:j + 1, :] * dp[sh:sh + s, :]
            dw_ref[j:j + 1, :] = jnp.sum(dacc * xp[PADR - sh:PADR - sh + s, :], axis=0, keepdims=True)
        dx_ref[...] = dx.astype(BF16)

    blk = pl.BlockSpec((s, HEAD), lambda c: (0, c))
    hm = pl.BlockSpec((None, s, HEAD), lambda c: (c, 0, 0))
    wblk = pl.BlockSpec((kw, HEAD), lambda c: (0, c))
    return pl.pallas_call(
        body, name=name, grid=(3 * nh,), in_specs=[hm, blk, wblk], out_specs=[blk, wblk],
        out_shape=[jax.ShapeDtypeStruct((s, 3 * d), BF16), jax.ShapeDtypeStruct((kw, 3 * d), F32)],
        scratch_shapes=[pltpu.VMEM((s + PADR, HEAD), F32), pltpu.VMEM((s + PADR, HEAD), F32)],
        compiler_params=_params(("parallel",)))(dqkv, proj, w_conv)


def _softplus(x):
    return jnp.maximum(x, 0.0) + jnp.log(1.0 + jnp.exp(-jnp.abs(x)))


def _gates_fwd(pab, a_log, dt_bias, nh, name):
    s = pab.shape[0]
    tm = _pick(s, (512, 256, 128))

    def body(x_ref, al_ref, dt_ref, o_ref):
        x = x_ref[...]
        lane = lax.broadcasted_iota(jnp.int32, x.shape, 1)
        g = -jnp.exp(al_ref[...]) * _softplus(x + dt_ref[...])
        o_ref[...] = jnp.where(lane < nh, g, jnp.where(lane < 2 * nh, jax.nn.sigmoid(x), 0.0))

    row = pl.BlockSpec((tm, HEAD), lambda i: (i, 0))
    vec = pl.BlockSpec((1, HEAD), lambda i: (0, 0))
    return pl.pallas_call(
        body, name=name, grid=(s // tm,), in_specs=[row, vec, vec], out_specs=row,
        out_shape=jax.ShapeDtypeStruct((s, HEAD), F32), compiler_params=_params(("parallel",)))(pab, a_log, dt_bias)


def _gates_bwd(dgb, pab, a_log, dt_bias, nh, name):
    s = pab.shape[0]
    tm = _pick(s, (512, 256, 128))

    def body(d_ref, x_ref, al_ref, dt_ref, dx_ref, dal_ref, ddt_ref):
        @pl.when(pl.program_id(0) == 0)
        def _():
            dal_ref[...] = jnp.zeros_like(dal_ref)
            ddt_ref[...] = jnp.zeros_like(ddt_ref)

        x = x_ref[...]
        dv = d_ref[...]
        lane = lax.broadcasted_iota(jnp.int32, x.shape, 1)
        ea = jnp.exp(al_ref[...])
        xs = x + dt_ref[...]
        g = -ea * _softplus(xs)
        dxs = jnp.where(lane < nh, dv * (-ea) * jax.nn.sigmoid(xs), 0.0)
        sg = jax.nn.sigmoid(x)
        dxb = jnp.where((lane >= nh) & (lane < 2 * nh), dv * sg * (1.0 - sg), 0.0)
        dx_ref[...] = (dxs + dxb).astype(BF16)
        dal_ref[...] += jnp.sum(jnp.where(lane < nh, dv * g, 0.0), axis=0, keepdims=True)
        ddt_ref[...] += jnp.sum(dxs, axis=0, keepdims=True)

    row = pl.BlockSpec((tm, HEAD), lambda i: (i, 0))
    vec = pl.BlockSpec((1, HEAD), lambda i: (0, 0))
    return pl.pallas_call(
        body, name=name, grid=(s // tm,), in_specs=[row, row, vec, vec], out_specs=[row, vec, vec],
        out_shape=[jax.ShapeDtypeStruct((s, HEAD), BF16), jax.ShapeDtypeStruct((1, HEAD), F32),
                   jax.ShapeDtypeStruct((1, HEAD), F32)],
        compiler_params=_params(("arbitrary",)))(dgb, pab, a_log, dt_bias)


def _tri_inv(a_low, eye_f):
    n = -a_low
    p = eye_f + n
    steps = int(math.log2(a_low.shape[-1])) - 1
    for _ in range(steps):
        n = _dot(n, n, BNN)
        p = p + _dot(p, n, BNN)
    return p


def _lane_col(x, lane, idx):
    return jnp.sum(jnp.where(lane == idx, x, 0.0), axis=1, keepdims=True)


def _head_cols(gbv, lo, nh):
    lane = lax.broadcasted_iota(jnp.int32, gbv.shape, 1)
    return jnp.stack([_lane_col(gbv, lane, lo + h) for h in range(nh)], axis=0)


def _gdn_chunk(q, k, v, g_col, beta_col, st):
    c = q.shape[1]
    r_i = lax.broadcasted_iota(jnp.int32, (c, c), 0)
    c_i = lax.broadcasted_iota(jnp.int32, (c, c), 1)
    incl = c_i <= r_i
    strict = c_i < r_i
    eye = c_i == r_i
    g_row = jnp.sum(jnp.where(eye, g_col, 0.0), axis=1, keepdims=True)
    gc_col = jnp.sum(jnp.where(incl, g_row, 0.0), axis=2, keepdims=True)
    gc_row = jnp.sum(jnp.where(eye, gc_col, 0.0), axis=1, keepdims=True)
    g_last = jnp.sum(g_col, axis=1, keepdims=True)
    decay = jnp.exp(jnp.where(incl, gc_col - gc_row, NEG))
    kk = _dot(k, k, BNT)
    a_low = jnp.where(strict, beta_col * kk * decay, 0.0)
    t_inv = _tri_inv(a_low, eye.astype(F32))
    e_g = jnp.exp(gc_col)
    bk = beta_col * e_g
    rhs = jnp.concatenate([v * beta_col, k * bk], axis=2)
    sol = _dot(t_inv, rhs, BNN)
    u, w = sol[:, :, :HEAD], sol[:, :, HEAD:]
    qk_raw = _dot(q, k, BNT)
    qk = qk_raw * decay
    q_dec = q * e_g
    e2 = jnp.exp(g_last - gc_col)
    k_dec = k * e2
    gl = jnp.exp(g_last)
    ws = _dot(jnp.concatenate([w, q_dec], axis=1), st, BNN)
    v_new = u - ws[:, :c]
    o = ws[:, c:] + _dot(qk, v_new, BNN)
    st_new = st * gl + _dot(k_dec, v_new, BTN)
    inter = dict(incl=incl, strict=strict, eye=eye, decay=decay, kk=kk, t_inv=t_inv, e_g=e_g, bk=bk, sol=sol, w=w,
                 qk_raw=qk_raw, qk=qk, q_dec=q_dec, e2=e2, k_dec=k_dec, gl=gl, v_new=v_new, c_i=c_i, r_i=r_i)
    return o, st_new, inter


def _gdn_fwd(qkv, gb, nh, name):
    s = qkv.shape[1]
    nc = s // CHUNK

    def body(q_ref, k_ref, v_ref, gb_ref, o_ref, st_ref, state):
        @pl.when(pl.program_id(0) == 0)
        def _():
            state[...] = jnp.zeros_like(state)

        gbv = gb_ref[...]
        st = state[...]
        st_ref[...] = st
        o, st_new, _ = _gdn_chunk(q_ref[...], k_ref[...], v_ref[...], _head_cols(gbv, 0, nh), _head_cols(gbv, nh, nh), st)
        o_ref[...] = o
        state[...] = st_new

    def qspec(part):
        return pl.BlockSpec((nh, CHUNK, HEAD), lambda n: (part, n, 0))

    return pl.pallas_call(
        body, name=name, grid=(nc,),
        in_specs=[qspec(0), qspec(1), qspec(2), pl.BlockSpec((CHUNK, HEAD), lambda n: (n, 0))],
        out_specs=[qspec(0), pl.BlockSpec((None, nh, HEAD, HEAD), lambda n: (n, 0, 0, 0))],
        out_shape=[jax.ShapeDtypeStruct((nh, s, HEAD), F32), jax.ShapeDtypeStruct((nc, nh, HEAD, HEAD), F32)],
        scratch_shapes=[pltpu.VMEM((nh, HEAD, HEAD), F32)],
        compiler_params=_params(("arbitrary",)))(qkv, qkv, qkv, gb)


def _gdn_bwd(qkv, gb, do, states, nh, name):
    s = qkv.shape[1]
    nc = s // CHUNK
    c = CHUNK

    def body(q_ref, k_ref, v_ref, gb_ref, do_ref, st_ref, dqkv_ref, dgb_ref, dstate):
        @pl.when(pl.program_id(0) == 0)
        def _():
            dstate[...] = jnp.zeros_like(dstate)

        gbv = gb_ref[...]
        lane = lax.broadcasted_iota(jnp.int32, gbv.shape, 1)
        q, k, v = q_ref[...], k_ref[...], v_ref[...]
        beta_col = _head_cols(gbv, nh, nh)
        st = st_ref[...]
        dst = dstate[...]
        dov = do_ref[...]
        _, _, it = _gdn_chunk(q, k, v, _head_cols(gbv, 0, nh), beta_col, st)
        incl, strict, eye, decay = it["incl"], it["strict"], it["eye"], it["decay"]
        dv_new = _dot(it["qk"], dov, BTN) + _dot(it["k_dec"], dst, BNN)
        d_qk = _dot(dov, it["v_new"], BNT)
        dd = _dot(jnp.concatenate([dov, -dv_new], axis=1), st, BNT)
        dq_dec, dw = dd[:, :c], dd[:, c:]
        dst_new = _dot(it["q_dec"], dov, BTN) + it["gl"] * dst - _dot(it["w"], dv_new, BTN)
        dgl = jnp.sum(jnp.sum(dst * st, axis=2, keepdims=True), axis=1, keepdims=True)
        dk_dec = _dot(it["v_new"], dst, BNT)
        dsol = jnp.concatenate([dv_new, dw], axis=2)
        drhs = _dot(it["t_inv"], dsol, BTN)
        d_a = jnp.where(strict, -_dot(drhs, it["sol"], BNT), 0.0)
        drhs_u, drhs_w = drhs[:, :, :HEAD], drhs[:, :, HEAD:]
        dvh = beta_col * drhs_u
        rw_k = jnp.sum(drhs_w * k, axis=2, keepdims=True)
        dbeta = jnp.sum(drhs_u * v, axis=2, keepdims=True) + it["e_g"] * rw_k
        dkh = it["bk"] * drhs_w
        dgc_col = it["bk"] * rw_k
        dkk = d_a * beta_col * decay
        dbeta = dbeta + jnp.sum(d_a * it["kk"] * decay, axis=2, keepdims=True)
        ddecay = d_a * beta_col * it["kk"]
        dkh = dkh + _dot(dkk, k, BNN) + _dot(dkk, k, BTN)
        dqk_raw = d_qk * decay
        ddecay = ddecay + d_qk * it["qk_raw"]
        dqh = _dot(dqk_raw, k, BNN)
        dkh = dkh + _dot(dqk_raw, q, BTN)
        ddm = jnp.where(incl, ddecay * decay, 0.0)
        dgc_col = dgc_col + jnp.sum(ddm, axis=2, keepdims=True)
        dgc_row = -jnp.sum(ddm, axis=1, keepdims=True)
        dqh = dqh + dq_dec * it["e_g"]
        dgc_col = dgc_col + jnp.sum(dq_dec * it["q_dec"], axis=2, keepdims=True)
        dkh = dkh + dk_dec * it["e2"]
        tmp = jnp.sum(dk_dec * it["k_dec"], axis=2, keepdims=True)
        dgc_col = dgc_col - tmp
        dg_last = jnp.sum(tmp, axis=1, keepdims=True) + dgl * it["gl"]
        dgc_tot_row = dgc_row + jnp.sum(jnp.where(eye, dgc_col, 0.0), axis=1, keepdims=True)
        dg_col = jnp.sum(jnp.where(it["c_i"] >= it["r_i"], dgc_tot_row, 0.0), axis=2, keepdims=True) + dg_last
        dqkv_ref[0] = dqh
        dqkv_ref[1] = dkh
        dqkv_ref[2] = dvh
        dstate[...] = dst_new
        dgb_acc = jnp.zeros(gbv.shape, F32)
        for h in range(nh):
            dgb_acc = jnp.where(lane == h, dg_col[h], jnp.where(lane == nh + h, dbeta[h], dgb_acc))
        dgb_ref[...] = dgb_acc

    def rev(part):
        return pl.BlockSpec((nh, CHUNK, HEAD), lambda n: (part, nc - 1 - n, 0))

    gspec = pl.BlockSpec((CHUNK, HEAD), lambda n: (nc - 1 - n, 0))
    dqkv, dgb = pl.pallas_call(
        body, name=name, grid=(nc,),
        in_specs=[rev(0), rev(1), rev(2), gspec, rev(0),
                  pl.BlockSpec((None, nh, HEAD, HEAD), lambda n: (nc - 1 - n, 0, 0, 0))],
        out_specs=[pl.BlockSpec((3, nh, CHUNK, HEAD), lambda n: (0, 0, nc - 1 - n, 0)), gspec],
        out_shape=[jax.ShapeDtypeStruct((3, nh, s, HEAD), F32), jax.ShapeDtypeStruct((s, HEAD), F32)],
        scratch_shapes=[pltpu.VMEM((nh, HEAD, HEAD), F32)],
        compiler_params=_params(("arbitrary",)))(qkv, qkv, qkv, gb, do, states)
    return dqkv.reshape(3 * nh, s, HEAD), dgb


SB_TQ = 512


def _tri01(rel):
    j_i = lax.broadcasted_iota(jnp.int32, (SBLK, SBLK), 0)
    s_i = lax.broadcasted_iota(jnp.int32, (SBLK, SBLK), 1)
    return rel(j_i, s_i).astype(BF16)


SB_HP = 2


def _each(fn, *lists):
    return [fn(*xs) for xs in zip(*lists)]


def _sb_scores(qts, kblks, mask, csums, rhs01):
    zs = _each(lambda qt, kb: _dot(qt, kb, NT), qts, kblks)
    es = _each(lambda z: jnp.exp(-jnp.abs(z)), zs)
    sps = _each(lambda z, e: jnp.maximum(z, 0.0) + jnp.log(1.0 + e), zs, es)
    lns = _each(lambda sp: -sp if mask is None else jnp.where(mask, -sp, 0.0), sps)
    sts = _each(lambda ln: _dot_hilo(ln, rhs01), lns)
    wgts = _each(lambda z, sp, st, cs: jnp.exp((z - sp) + st + cs), zs, sps, sts, csums)
    if mask is not None:
        wgts = _each(lambda w: jnp.where(mask, w, 0.0), wgts)
    return zs, es, wgts, lns


def _band_mask(rows, j, row0):
    r_i = lax.broadcasted_iota(jnp.int32, (rows, SBLK), 0)
    c_i = lax.broadcasted_iota(jnp.int32, (rows, SBLK), 1)
    return (j * SBLK + c_i) < (row0 + r_i)


def _sb_fwd(q, k, v, name):
    s, d = q.shape
    nh = d // HEAD
    tq = min(SB_TQ, s)
    nb = tq // SBLK

    hp = SB_HP
    heads = [slice(h * HEAD, (h + 1) * HEAD) for h in range(hp)]

    def body(q_ref, k_ref, v_ref, o_ref, c_ref, acc, cs):
        qb = pl.program_id(1)
        lane = lax.broadcasted_iota(jnp.int32, (tq, HEAD), 1)
        after = _tri01(lambda j, t: j > t)
        acc[...] = jnp.zeros_like(acc)
        cs[...] = jnp.zeros_like(cs)
        c_ref[...] = jnp.zeros_like(c_ref)

        def process(rs, kb, mask):
            keys = pl.ds(pl.multiple_of(kb * SBLK, SBLK), SBLK)
            csums = [cs[h, rs, :] for h in range(hp)]
            _, _, wgts, lns = _sb_scores([q_ref[rs, hs] for hs in heads], [k_ref[keys, hs] for hs in heads], mask, csums, after)
            pvs = _each(lambda w, hs: _dot(w, v_ref[keys, hs]), wgts, heads)
            tots = _each(lambda ln: jnp.sum(ln, axis=1, keepdims=True), lns)
            for h, hs in enumerate(heads):
                acc[h, rs, :] += pvs[h]
                c_ref[rs, hs] = jnp.where(lane[rs, :] == kb, csums[h], c_ref[rs, hs])
                cs[h, rs, :] = csums[h] + tots[h]

        for j in reversed(range(nb)):
            process(slice(j * SBLK, tq), qb * nb + j, _band_mask(tq - j * SBLK, j, j * SBLK))

        def step(it, carry):
            process(slice(0, tq), qb * nb - 1 - it, None)
            return carry

        lax.fori_loop(0, qb * nb, step, 0)
        for h, hs in enumerate(heads):
            o_ref[:, hs] = acc[h].astype(BF16)

    qspec = pl.BlockSpec((tq, hp * HEAD), lambda h, i: (i, h))
    kspec = pl.BlockSpec((s, hp * HEAD), lambda h, i: (0, h))
    return pl.pallas_call(
        body, name=name, grid=(nh // hp, s // tq), in_specs=[qspec, kspec, kspec], out_specs=[qspec, qspec],
        out_shape=[jax.ShapeDtypeStruct((s, d), BF16), jax.ShapeDtypeStruct((s, d), F32)],
        scratch_shapes=[pltpu.VMEM((hp, tq, HEAD), F32), pltpu.VMEM((hp, tq, 1), F32)],
        compiler_params=_params(("parallel", "arbitrary")))(q, k, v)


def _sb_bwd(q, k, v, do, ctab, name):
    s, d = q.shape
    nh = d // HEAD
    tq = min(SB_TQ, s)
    nb = tq // SBLK

    hp = SB_HP
    heads = [slice(h * HEAD, (h + 1) * HEAD) for h in range(hp)]

    def body(q_ref, k_ref, v_ref, do_ref, c_ref, dq_ref, dk_ref, dv_ref, ps):
        qb = pl.program_id(1)

        @pl.when(qb == 0)
        def _():
            dk_ref[...] = jnp.zeros_like(dk_ref)
            dv_ref[...] = jnp.zeros_like(dv_ref)

        dq_ref[...] = jnp.zeros_like(dq_ref)
        ps[...] = jnp.zeros_like(ps)
        lane = lax.broadcasted_iota(jnp.int32, (tq, HEAD), 1)
        after = _tri01(lambda j, t: j > t)
        before = _tri01(lambda j, t: j < t)

        def process(rs, kb, mask):
            keys = pl.ds(pl.multiple_of(kb * SBLK, SBLK), SBLK)
            kblks = [k_ref[keys, hs] for hs in heads]
            qts = [q_ref[rs, hs] for hs in heads]
            dots = [do_ref[rs, hs] for hs in heads]
            csums = [_lane_col(c_ref[rs, hs], lane[rs, :], kb) for hs in heads]
            zs, es, wgts, _ = _sb_scores(qts, kblks, mask, csums, after)
            dlws = _each(lambda dt, hs, w: _dot(dt, v_ref[keys, hs], NT) * w, dots, heads, wgts)
            pts = _each(lambda dlw: _dot_hilo(dlw, before), dlws)
            pfxs = [ps[h, rs, :] for h in range(hp)]
            rs_ = _each(lambda e: 1.0 / (1.0 + e), es)
            sigs = _each(lambda z, e, r: jnp.where(z >= 0.0, r, e * r), zs, es, rs_)
            dzs = _each(lambda dlw, sig, pfx, pt: dlw * (1.0 - sig) - sig * (pfx + pt), dlws, sigs, pfxs, pts)
            tots = _each(lambda dlw: jnp.sum(dlw, axis=1, keepdims=True), dlws)
            if mask is not None:
                dzs = _each(lambda dz: jnp.where(mask, dz, 0.0), dzs)
            dqs = _each(lambda dz, kb_: _dot(dz, kb_), dzs, kblks)
            dks = _each(lambda dz, qt: _dot(dz, qt, TN), dzs, qts)
            dvs = _each(lambda w, dt: _dot(w, dt, TN), wgts, dots)
            for h, hs in enumerate(heads):
                dq_ref[rs, hs] += dqs[h]
                dk_ref[keys, hs] += dks[h]
                dv_ref[keys, hs] += dvs[h]
                ps[h, rs, :] = pfxs[h] + tots[h]

        def step(kb, carry):
            process(slice(0, tq), kb, None)
            return carry

        lax.fori_loop(0, qb * nb, step, 0)
        for j in range(nb):
            process(slice(j * SBLK, tq), qb * nb + j, _band_mask(tq - j * SBLK, j, j * SBLK))

    qspec = pl.BlockSpec((tq, hp * HEAD), lambda h, i: (i, h))
    kspec = pl.BlockSpec((s, hp * HEAD), lambda h, i: (0, h))
    sds = jax.ShapeDtypeStruct((s, d), F32)
    return pl.pallas_call(
        body, name=name, grid=(nh // hp, s // tq), in_specs=[qspec, kspec, kspec, qspec, qspec],
        out_specs=[qspec, kspec, kspec], out_shape=[sds, sds, sds],
        scratch_shapes=[pltpu.VMEM((hp, tq, 1), F32)],
        compiler_params=_params(("parallel", "arbitrary")))(q, k, v, do, ctab)


def _my_index():
    return 4 * lax.axis_index("x") + 2 * lax.axis_index("y") + lax.axis_index("c")


def _all_gather(x_shard, name):
    m_per, n = x_shard.shape

    def body(x_ref, out_ref, send_sems, recv_sems, local_sem):
        x, y, c = lax.axis_index("x"), lax.axis_index("y"), lax.axis_index("c")
        me, sibling = (x, y, c), (x, y, 1 - c)
        chips = [(1 - x, y), (x, 1 - y), (1 - x, 1 - y)]

        def rows(px, py, pc):
            return out_ref.at[pl.ds((4 * px + 2 * py + pc) * m_per, m_per), :]

        def copy(k, block, to, src=None):
            return pltpu.make_async_remote_copy(
                src_ref=rows(*block) if src is None else src, dst_ref=rows(*block),
                send_sem=send_sems.at[k], recv_sem=recv_sems.at[k], device_id=to, device_id_type=MESH)

        mine = pltpu.make_async_copy(x_ref, rows(*me), local_sem)
        mine.start()
        first = [copy(0, me, sibling, src=x_ref)]
        first += [copy(1 + j, me, (*chip, c), src=x_ref) for j, chip in enumerate(chips)]
        for cp in first:
            cp.start()
        passed = [copy(4 + j, (*chip, c), sibling) for j, chip in enumerate(chips)]
        for j, chip in enumerate(chips):
            copy(1 + j, (*chip, c), me).wait_recv()
            passed[j].start()
        copy(0, sibling, me).wait_recv()
        for j, chip in enumerate(chips):
            copy(4 + j, (*chip, 1 - c), me).wait_recv()
        for cp in first + passed:
            cp.wait_send()
        mine.wait()

    return pl.pallas_call(
        body, name=name, out_shape=jax.ShapeDtypeStruct((NDEV * m_per, n), x_shard.dtype),
        in_specs=[pl.BlockSpec(memory_space=pl.ANY)], out_specs=pl.BlockSpec(memory_space=pl.ANY),
        scratch_shapes=[pltpu.SemaphoreType.DMA((7,)), pltpu.SemaphoreType.DMA((7,)), pltpu.SemaphoreType.DMA],
    )(x_shard)


HBM_SPEC = pl.BlockSpec(memory_space=pltpu.HBM)
SEM_SPEC = pl.BlockSpec(memory_space=pltpu.SEMAPHORE)
ANY_SPEC = pl.BlockSpec(memory_space=pl.ANY)
EFFECT = pltpu.SideEffectType.DATAFLOW_SIDE_EFFECTING


def _exchange_copies(src_refs, land_refs, send_sems, recv_sems, self_sems, scatter):
    x, y, c = lax.axis_index("x"), lax.axis_index("y"), lax.axis_index("c")
    me = 4 * x + 2 * y + c
    remote, local = [], []
    for p, (src_ref, land_ref) in enumerate(zip(src_refs, land_refs)):
        rows = land_ref.shape[0] // NDEV

        def part(idx):
            return src_ref.at[pl.ds(idx * rows, rows), :] if scatter else src_ref

        slot = land_ref.at[pl.ds(me * rows, rows), :]
        for k in range(1, NDEV):
            px, py, pc = x ^ ((k >> 2) & 1), y ^ ((k >> 1) & 1), c ^ (k & 1)
            remote.append(pltpu.make_async_remote_copy(
                src_ref=part(4 * px + 2 * py + pc), dst_ref=slot, send_sem=send_sems.at[7 * p + k - 1],
                recv_sem=recv_sems.at[7 * p + k - 1], device_id=(px, py, pc), device_id_type=MESH))
        local.append(pltpu.make_async_copy(part(me), slot, self_sems.at[p]))
    return remote, local


def _send_start(srcs, scatter, after, name):
    n = len(srcs)
    lands = []
    for s in srcs:
        rows = s.shape[0] if scatter else NDEV * s.shape[0]
        lands.append(pltpu.with_memory_space_constraint(lax.empty((rows, s.shape[1]), s.dtype), pltpu.HBM))

    def body(*refs):
        src_refs, land_refs = refs[:n], refs[n:2 * n]
        send_sems, recv_sems, self_sems = refs[2 * n + 1:2 * n + 4]
        remote, local = _exchange_copies(src_refs, land_refs, send_sems, recv_sems, self_sems, scatter)
        for cp in remote + local:
            cp.start()
        refs[-1][...] = jnp.zeros_like(refs[-1])

    hbm = lambda a: pltpu.HBM(a.shape, a.dtype)
    out = pl.pallas_call(
        body, name=name,
        out_shape=(pltpu.SemaphoreType.DMA((7 * n,)), pltpu.SemaphoreType.DMA((7 * n,)), pltpu.SemaphoreType.DMA((n,)),
                   *[hbm(s) for s in srcs], *[hbm(a) for a in lands], jax.ShapeDtypeStruct((8, HEAD), F32)),
        in_specs=(HBM_SPEC,) * (2 * n) + (ANY_SPEC,),
        out_specs=(SEM_SPEC,) * 3 + (HBM_SPEC,) * (2 * n) + (pl.BlockSpec(memory_space=pltpu.VMEM),),
        input_output_aliases={i: 3 + i for i in range(2 * n)},
        compiler_params=pltpu.CompilerParams(has_side_effects=EFFECT),
    )(*[pltpu.with_memory_space_constraint(s, pltpu.HBM) for s in srcs], *lands, after)
    return dict(sems=out[:3], srcs=out[3:3 + n], lands=out[3 + n:3 + 2 * n], token=out[-1])


def _send_wait(started, scatter, after, name):
    srcs, lands = started["srcs"], started["lands"]
    n = len(srcs)

    def body(*refs):
        src_refs, land_refs = refs[:n], refs[n:2 * n]
        send_sems, recv_sems, self_sems = refs[2 * n:2 * n + 3]
        remote, local = _exchange_copies(src_refs, land_refs, send_sems, recv_sems, self_sems, scatter)
        for cp in remote:
            cp.wait_send()
            cp.wait_recv()
        for cp in local:
            cp.wait()

    hbm = lambda a: pltpu.HBM(a.shape, a.dtype)
    out = pl.pallas_call(
        body, name=name, out_shape=(*[hbm(s) for s in srcs], *[hbm(a) for a in lands]),
        in_specs=(HBM_SPEC,) * (2 * n) + (SEM_SPEC,) * 3 + (ANY_SPEC,), out_specs=(HBM_SPEC,) * (2 * n),
        input_output_aliases={i: i for i in range(2 * n)},
        compiler_params=pltpu.CompilerParams(has_side_effects=EFFECT),
    )(*srcs, *lands, *started["sems"], after)
    return out[n:]


def _sum_slots(xs, name, rows_out=None):
    _, r, c = xs[0].shape
    ro = rows_out or r
    tc = _pick(c, (128,))

    def body(*refs):
        o_ref = refs[-1]
        for l, x_ref in enumerate(refs[:-1]):
            acc = x_ref[0].astype(F32)
            for i in range(1, NDEV):
                acc = acc + x_ref[i].astype(F32)
            o_ref[l] = acc[:ro]

    return pl.pallas_call(
        body, name=name, grid=(c // tc,), in_specs=[pl.BlockSpec((NDEV, r, tc), lambda j: (0, 0, j))] * len(xs),
        out_specs=pl.BlockSpec((len(xs), ro, tc), lambda j: (0, 0, j)),
        out_shape=jax.ShapeDtypeStruct((len(xs), ro, c), F32), compiler_params=_params(("parallel",)))(*xs)


def _adamw(w, g, m, v, name):
    if w.ndim == 3:
        nl, r, c = w.shape
        tc = _pick(c, (256, 128))
        grid = (nl, c // tc)
        blk = pl.BlockSpec((None, r, tc), lambda i, j: (i, 0, j))
        sem = ("parallel", "parallel")
    else:
        r, c = w.shape
        tr = _pick(r, (256, 128, 64, 32, 16, 8))
        grid = (r // tr,)
        blk = pl.BlockSpec((tr, c), lambda i: (i, 0))
        sem = ("parallel",)
    c1 = 1.0 - B1 ** STEP
    c2 = 1.0 - B2 ** STEP

    def body(w_ref, g_ref, m_ref, v_ref, d_ref, nm_ref, nv_ref):
        gv = g_ref[...]
        nm = B1 * m_ref[...] + (1.0 - B1) * gv
        nv = B2 * v_ref[...] + (1.0 - B2) * (gv * gv)
        d_ref[...] = -LR * ((nm / c1) / (jnp.sqrt(nv / c2) + ADAM_EPS) + WD * w_ref[...])
        nm_ref[...] = nm
        nv_ref[...] = nv

    sds = jax.ShapeDtypeStruct(w.shape, F32)
    return pl.pallas_call(
        body, name=name, grid=grid, in_specs=[blk] * 4, out_specs=[blk] * 3, out_shape=[sds] * 3,
        compiler_params=_params(sem))(w, g, m, v)


def _pad_rows(a, mult):
    r = a.shape[0]
    pad = (-r) % mult
    return a if pad == 0 else jnp.pad(a, ((0, pad), (0, 0)))


def _pad_lanes(v, width=HEAD):
    return jnp.pad(v.reshape(1, -1), ((0, 0), (0, width - v.shape[-1])))


def kernel(x, p, ln_mix, ln_ffn, ln_ple, gdn_w_in, gdn_conv, gdn_a_log, gdn_dt_bias, gdn_norm, gdn_w_out, kv_norm, w_kv, k_norm, sb_w_q, sb_q_norm, sb_w_out, ffn_w_in, ffn_w_out, ple_w_proj, ple_w_gate, loss_target, m_ln_mix, m_ln_ffn, m_ln_ple, m_gdn_w_in, m_gdn_conv, m_gdn_a_log, m_gdn_dt_bias, m_gdn_norm, m_gdn_w_out, m_kv_norm, m_w_kv, m_k_norm, m_sb_w_q, m_sb_q_norm, m_sb_w_out, m_ffn_w_in, m_ffn_w_out, m_ple_w_proj, m_ple_w_gate, v_ln_mix, v_ln_ffn, v_ln_ple, v_gdn_w_in, v_gdn_conv, v_gdn_a_log, v_gdn_dt_bias, v_gdn_norm, v_gdn_w_out, v_kv_norm, v_w_kv, v_k_norm, v_sb_w_q, v_sb_q_norm, v_sb_w_out, v_ffn_w_in, v_ffn_w_out, v_ple_w_proj, v_ple_w_gate):
    s, d = x.shape[1], x.shape[2]
    nh = d // HEAD
    depth = ln_mix.shape[0]
    n_a = gdn_w_in.shape[0]
    n_b = sb_w_q.shape[0]
    me = _my_index()
    win_cols = gdn_w_in.shape[2]
    win_rows = 4 * d + 2 * nh

    def col_t(w):
        return jnp.transpose(w).astype(BF16)

    local = {}
    for l in range(n_a):
        local[("gdn_w_in", l)] = col_t(gdn_w_in[l])
        local[("gdn_w_out", l)] = gdn_w_out[l].astype(BF16)
    local[("w_kv", 0)] = col_t(w_kv)
    for j in range(n_b):
        local[("sb_w_q", j)] = sb_w_q[j].astype(BF16)
        local[("sb_w_out", j)] = sb_w_out[j].astype(BF16)
    for l in range(depth):
        local[("ffn_w_in", l)] = col_t(ffn_w_in[l])
        local[("ffn_w_out", l)] = ffn_w_out[l].astype(BF16)
        local[("ple_w_proj", l)] = col_t(ple_w_proj[l]).reshape(-1, d)
        local[("ple_w_gate", l)] = ple_w_gate[l].astype(BF16)
    local = {key: _pad_rows(a, 16) for key, a in local.items()}

    chunks = []
    for l in range(depth):
        mix = [("gdn_w_in", l), ("gdn_w_out", l)] if l < n_a else [("sb_w_q", l - n_a), ("sb_w_out", l - n_a)]
        rest = [("ffn_w_in", l), ("ffn_w_out", l), ("ple_w_proj", l), ("ple_w_gate", l)]
        if l == n_a - 1:
            rest.append(("w_kv", 0))
        chunks += [(f"a{l}", mix), (f"f{l}", rest)]
    chunk_keys = dict(chunks)

    conv_rows = n_a * gdn_conv.shape[1]
    conv_sh = _pad_rows(gdn_conv.reshape(conv_rows, -1), 8)
    conv_g = _all_gather(conv_sh, "comm_gather_conv")
    token = conv_g
    conv_g = conv_g.reshape(NDEV, conv_sh.shape[0], -1)
    conv_full = jnp.transpose(conv_g[:, :conv_rows, :], (1, 0, 2)).reshape(n_a, gdn_conv.shape[1], 3 * d)

    w_started = {}
    for name, keys in chunks:
        w_started[name] = _send_start([local[k] for k in keys], False, token, f"comm_wstart_{name}")
        token = w_started[name]["token"]

    full = {}

    def fetch(name, after):
        lands = _send_wait(w_started[name], False, after, f"comm_wwait_{name}")
        for key, land in zip(chunk_keys[name], lands):
            full[key] = land

    def whole(key, valid=None):
        a = full[key]
        if valid is not None:
            a = a.reshape(NDEV, -1, d)[:, :valid, :].reshape(-1, d)
        return a

    pd = p.shape[-1]
    w_in_t, w_ab_t, w_gout, w_q, w_sout, wf_t, w_fout, wp_t, w_pg = {}, {}, {}, {}, {}, {}, {}, {}, {}
    wkv_t = None

    h = x[0]
    sv = []
    kv_sv = None
    k_sh = v_sh = None
    for l in range(depth):
        t = {}
        t["h0"] = h
        if l == 0:
            hn = _rms_fwd(h, ln_mix[l], f"rms_mix_{l}")
        t["hn"] = hn
        fetch(f"a{l}", token if l == 0 else hn)
        if l < n_a:
            wt = whole(("gdn_w_in", l), win_cols)
            w_in_t[l] = wt[:4 * d]
            w_ab_t[l] = jnp.pad(wt[4 * d:], ((0, HEAD - 2 * nh), (0, 0)))
            w_gout[l] = whole(("gdn_w_out", l))
        else:
            w_q[l - n_a] = whole(("sb_w_q", l - n_a))
            w_sout[l - n_a] = whole(("sb_w_out", l - n_a))
        if l < n_a:
            proj = _mm(hn, w_in_t[l], "nt", f"gdn_proj_{l}")
            pab = _mm(hn, w_ab_t[l], "nt", f"gdn_proj_ab_{l}")
            qkv = _conv_fwd(proj, conv_full[l], d, f"gdn_conv_{l}")
            al, dtb = _pad_lanes(gdn_a_log[l]), _pad_lanes(gdn_dt_bias[l])
            gb = _gates_fwd(pab, al, dtb, nh, f"gdn_gates_{l}")
            o_raw, states = _gdn_fwd(qkv, gb, nh, f"gdn_rule_{l}")
            o2 = _headnorm_fwd(o_raw, gdn_norm[l], f"gdn_outnorm_{l}", gate=proj, gate_col0=3 * d, head_major=True)
            h, hn2 = _mm(o2, w_gout[l], "nn", f"gdn_out_{l}", res=h, norm_g=ln_ffn[l])
            t.update(proj=proj, pab=pab, qkv=qkv, gb=gb, o_raw=o_raw, states=states, o2=o2, al=al, dtb=dtb)
        else:
            j = l - n_a
            qpre = _mm(hn, w_q[j], "nn", f"sb_qproj_{j}")
            qn = _headnorm_fwd(qpre, sb_q_norm[j], f"sb_qnorm_{j}", scale=HEAD ** -0.5)
            o, ctab = _sb_fwd(qn, k_sh, v_sh, f"sb_attn_{j}")
            h, hn2 = _mm(o, w_sout[j], "nn", f"sb_out_{j}", res=h, norm_g=ln_ffn[l])
            t.update(qpre=qpre, qn=qn, o=o, ctab=ctab)
        t["h1"] = h
        fetch(f"f{l}", hn2)
        wf_t[l] = whole(("ffn_w_in", l))
        w_fout[l] = whole(("ffn_w_out", l))
        wp_t[l] = full[("ple_w_proj", l)].reshape(d, pd)
        w_pg[l] = whole(("ple_w_gate", l))
        if l == n_a - 1:
            wkv_t = whole(("w_kv", 0))
        act, gs, us = _swiglu_fwd(hn2, wf_t[l], f"ffn_in_{l}")
        h, hn3 = _mm(act, w_fout[l], "nn", f"ffn_out_{l}", res=h, norm_g=ln_ple[l])
        t.update(hn2=hn2, act=act, gs=gs, us=us, h2=h)
        gains = ([ln_mix[l + 1]] if l + 1 < depth else []) + ([kv_norm] if l == n_a - 1 else [])
        h, gpre, pp, *normed = _ple_fwd(h, hn3, p[l, 0], w_pg[l], wp_t[l], f"ple_{l}", norm_gs=gains)
        if l + 1 < depth:
            hn = normed[0]
        t.update(hn3=hn3, gpre=gpre, pp=pp)
        sv.append(t)
        if l == n_a - 1:
            kvn = normed[-1]
            kv = _mm(kvn, wkv_t, "nt", "kv_proj")
            k_sh = _headnorm_fwd(kv, k_norm, "k_norm", width=d)
            v_sh = kv[:, d:].astype(BF16)
            kv_sv = dict(h=h, kvn=kvn, kv=kv)

    dh, loss_vec = _loss_fwd_bwd(h, loss_target[0], "loss")
    loss = lax.psum(jnp.sum(loss_vec), ("x", "y", "c"))

    gw = {}
    small = {}
    g_started = {}

    def scatter_start(name):
        gparts = []
        for key in chunk_keys[name]:
            g = gw[key]
            g = g.reshape(NDEV, -1, d) if key[0] == "ple_w_proj" else g.reshape(NDEV, -1, g.shape[-1])
            padr = local[key].shape[0] - g.shape[1]
            if padr:
                g = jnp.pad(g, ((0, 0), (0, padr), (0, 0)))
            gparts.append(g.reshape(-1, d))
        g_started[name] = _send_start(gparts, True, gparts[0], f"comm_gstart_{name}")
        return g_started[name]["token"]

    dkv_sh = None
    for l in reversed(range(depth)):
        t = sv[l]
        if l == n_a - 1:
            dkv_k, dkn = _headnorm_bwd(dkv_sh[0], kv_sv["kv"], k_norm, "k_norm_bwd", dx_dtype=BF16)
            dkv = jnp.concatenate([dkv_k, dkv_sh[1].astype(BF16)], axis=1)
            gw[("w_kv", 0)] = _mm(dkv, kv_sv["kvn"], "tn", "kv_dw", out_dtype=BF16)
            dh, _, dg = _mm(dkv, wkv_t, "nn", "kv_dx", norm_bwd=(kv_sv["h"], kv_norm, dh))
            small["kv_norm"] = dg
            small["k_norm"] = dkn
        dgp, dpp = _ple_bwd(dh, t["gpre"], t["pp"], f"ple_bwd_{l}")
        gw[("ple_w_gate", l)] = _mm(t["hn3"], dgp, "tn", f"ple_dwg_{l}", out_dtype=BF16)
        gw[("ple_w_proj", l)] = _mm(dpp, p[l, 0], "tn", f"ple_dwp_{l}", out_dtype=BF16)
        dh, dhb, dg = _mm(dgp, w_pg[l], "nt", f"ple_dx_{l}", norm_bwd=(t["h2"], ln_ple[l], dh))
        small[("ln_ple", l)] = dg
        dgu = _swiglu_bwd(dhb, w_fout[l], t["gs"], t["us"], f"ffn_bwd_act_{l}")
        gw[("ffn_w_out", l)] = _mm(t["act"], dhb, "tn", f"ffn_dwo_{l}", out_dtype=BF16)
        gw[("ffn_w_in", l)] = _mm(dgu, t["hn2"], "tn", f"ffn_dwi_{l}", out_dtype=BF16)
        dh, dhb, dg = _mm(dgu, wf_t[l], "nn", f"ffn_dx_{l}", norm_bwd=(t["h1"], ln_ffn[l], dh),
                          after=scatter_start(f"f{l}"))
        small[("ln_ffn", l)] = dg
        if l < n_a:
            do2 = _mm(dhb, w_gout[l], "nt", f"gdn_out_dx_{l}")
            gw[("gdn_w_out", l)] = _mm(t["o2"], dhb, "tn", f"gdn_out_dw_{l}", out_dtype=BF16)
            do_raw, dgn, dgate = _headnorm_bwd(do2, t["o_raw"], gdn_norm[l], f"gdn_outnorm_bwd_{l}",
                                               gate=t["proj"], gate_col0=3 * d, head_major=True)
            small[("gdn_norm", l)] = dgn
            dqkv, dgb = _gdn_bwd(t["qkv"], t["gb"], do_raw, t["states"], nh, f"gdn_rule_bwd_{l}")
            dpab, dal, ddt = _gates_bwd(dgb, t["pab"], t["al"], t["dtb"], nh, f"gdn_gates_bwd_{l}")
            small[("gdn_a_log", l)] = dal
            small[("gdn_dt_bias", l)] = ddt
            dproj_qkv, dconv = _conv_bwd(dqkv, t["proj"], conv_full[l], d, f"gdn_conv_bwd_{l}")
            small[("gdn_conv", l)] = dconv
            dproj = jnp.concatenate([dproj_qkv, dgate], axis=1)
            dw_main = _mm(dproj, t["hn"], "tn", f"gdn_proj_dw_{l}", out_dtype=BF16)
            dw_ab = _mm(dpab, t["hn"], "tn", f"gdn_proj_ab_dw_{l}", out_dtype=BF16)
            gw[("gdn_w_in", l)] = jnp.concatenate([dw_main, dw_ab[:16]], axis=0)[:win_rows]
            dhn_ab = _mm(dpab, w_ab_t[l], "nn", f"gdn_proj_ab_dx_{l}")
            last = dict(a=dproj, b=w_in_t[l], mode="nn", name=f"gdn_proj_dx_{l}", res=dhn_ab)
        else:
            j = l - n_a
            do = _mm(dhb, w_sout[j], "nt", f"sb_out_dx_{j}", out_dtype=BF16)
            gw[("sb_w_out", j)] = _mm(t["o"], dhb, "tn", f"sb_out_dw_{j}", out_dtype=BF16)
            dq, dk, dv = _sb_bwd(t["qn"], k_sh, v_sh, do, t["ctab"], f"sb_attn_bwd_{j}")
            dkv_sh = (dk, dv) if dkv_sh is None else (dkv_sh[0] + dk, dkv_sh[1] + dv)
            dqpre, dqn = _headnorm_bwd(dq, t["qpre"], sb_q_norm[j], f"sb_qnorm_bwd_{j}", scale=HEAD ** -0.5, dx_dtype=BF16)
            small[("sb_q_norm", j)] = dqn
            gw[("sb_w_q", j)] = _mm(t["hn"], dqpre, "tn", f"sb_q_dw_{j}", out_dtype=BF16)
            last = dict(a=dqpre, b=w_q[j], mode="nt", name=f"sb_q_dx_{j}")
        dh, _, dg = _mm(**last, norm_bwd=(t["h0"], ln_mix[l], dh), after=scatter_start(f"a{l}"))
        small[("ln_mix", l)] = dg
    grad_x = dh[None]

    landed = {}
    for name, keys in reversed(chunks):
        lands = _send_wait(g_started[name], True, dh, f"comm_gwait_{name}")
        for key, land in zip(keys, lands):
            landed[key] = land.reshape(NDEV, -1, d)

    def summed(wname, count, rows_out=None):
        return _sum_slots([landed[(wname, i)] for i in range(count)], f"grad_sum_{wname}", rows_out)

    gt_gdn_w_in = summed("gdn_w_in", n_a, win_cols)
    gt_ffn_w_in = summed("ffn_w_in", depth)
    g_gdn_w_in = jnp.transpose(gt_gdn_w_in, (0, 2, 1))
    g_gdn_w_out = summed("gdn_w_out", n_a)
    g_w_kv = jnp.transpose(summed("w_kv", 1)[0])
    g_sb_w_q = summed("sb_w_q", n_b)
    g_sb_w_out = summed("sb_w_out", n_b)
    g_ffn_w_in = jnp.transpose(gt_ffn_w_in, (0, 2, 1))
    g_ffn_w_out = summed("ffn_w_out", depth)
    g_ple_w_proj = jnp.transpose(summed("ple_w_proj", depth).reshape(depth, -1, pd), (0, 2, 1))
    g_ple_w_gate = summed("ple_w_gate", depth)

    def vec_rows(v):
        return v.reshape(-1, HEAD)

    small_items = []
    for name_, cnt in (("ln_mix", depth), ("ln_ffn", depth), ("ln_ple", depth)):
        for l in range(cnt):
            small_items.append(((name_, l), vec_rows(small[(name_, l)])))
    for l in range(n_a):
        small_items.append((("gdn_conv", l), small[("gdn_conv", l)].reshape(-1, HEAD)))
        small_items.append((("gdn_a_log", l), small[("gdn_a_log", l)]))
        small_items.append((("gdn_dt_bias", l), small[("gdn_dt_bias", l)]))
        small_items.append((("gdn_norm", l), small[("gdn_norm", l)]))
    small_items.append(("kv_norm", vec_rows(small["kv_norm"])))
    small_items.append(("k_norm", small["k_norm"]))
    for j in range(n_b):
        small_items.append((("sb_q_norm", j), small[("sb_q_norm", j)]))
    spack = jnp.concatenate([_pad_rows(a, 8) for _, a in small_items], axis=0)
    sg = _all_gather(spack, "comm_gather_small").reshape(NDEV, spack.shape[0], HEAD)
    ssum = _sum_slots([sg], "small_sum")[0]
    sm = {}
    off = 0
    for key, a in small_items:
        sm[key] = ssum[off:off + a.shape[0]]
        off += a.shape[0] + (-a.shape[0]) % 8

    g_ln_mix = jnp.stack([sm[("ln_mix", l)].reshape(d) for l in range(depth)])
    g_ln_ffn = jnp.stack([sm[("ln_ffn", l)].reshape(d) for l in range(depth)])
    g_ln_ple = jnp.stack([sm[("ln_ple", l)].reshape(d) for l in range(depth)])
    conv_loc = gdn_conv.shape[2]
    g_conv_full = jnp.stack([sm[("gdn_conv", l)].reshape(gdn_conv.shape[1], 3 * d) for l in range(n_a)])
    g_gdn_conv = lax.dynamic_slice_in_dim(g_conv_full, me * conv_loc, conv_loc, axis=2)
    g_a_log = jnp.stack([sm[("gdn_a_log", l)][0, :nh] for l in range(n_a)])
    g_dt_bias = jnp.stack([sm[("gdn_dt_bias", l)][0, :nh] for l in range(n_a)])
    g_gdn_norm = jnp.stack([sm[("gdn_norm", l)][0] for l in range(n_a)])
    g_kv_norm = sm["kv_norm"].reshape(d)
    g_k_norm = sm["k_norm"][0]
    g_sb_q_norm = jnp.stack([sm[("sb_q_norm", j)][0] for j in range(n_b)])

    grads = [g_ln_mix, g_ln_ffn, g_ln_ple, g_gdn_w_in, g_gdn_conv, g_a_log, g_dt_bias, g_gdn_norm, g_gdn_w_out,
             g_kv_norm, g_w_kv, g_k_norm, g_sb_w_q, g_sb_q_norm, g_sb_w_out, g_ffn_w_in, g_ffn_w_out, g_ple_w_proj,
             g_ple_w_gate]
    weights = [ln_mix, ln_ffn, ln_ple, gdn_w_in, gdn_conv, gdn_a_log, gdn_dt_bias, gdn_norm, gdn_w_out, kv_norm, w_kv,
               k_norm, sb_w_q, sb_q_norm, sb_w_out, ffn_w_in, ffn_w_out, ple_w_proj, ple_w_gate]
    moms = [m_ln_mix, m_ln_ffn, m_ln_ple, m_gdn_w_in, m_gdn_conv, m_gdn_a_log, m_gdn_dt_bias, m_gdn_norm, m_gdn_w_out,
            m_kv_norm, m_w_kv, m_k_norm, m_sb_w_q, m_sb_q_norm, m_sb_w_out, m_ffn_w_in, m_ffn_w_out, m_ple_w_proj,
            m_ple_w_gate]
    vels = [v_ln_mix, v_ln_ffn, v_ln_ple, v_gdn_w_in, v_gdn_conv, v_gdn_a_log, v_gdn_dt_bias, v_gdn_norm, v_gdn_w_out,
            v_kv_norm, v_w_kv, v_k_norm, v_sb_w_q, v_sb_q_norm, v_sb_w_out, v_ffn_w_in, v_ffn_w_out, v_ple_w_proj,
            v_ple_w_gate]

    deltas, new_m, new_v = [], [], []
    small_idx = [i for i, w in enumerate(weights) if w.size < 8 * HEAD * 16]
    transposed = {3: gt_gdn_w_in, 15: gt_ffn_w_in}
    for i, (w, g, m, v) in enumerate(zip(weights, grads, moms, vels)):
        if i in small_idx:
            deltas.append(None), new_m.append(None), new_v.append(None)
            continue
        if i in transposed:
            tr = lambda a: jnp.transpose(a, (0, 2, 1))
            dl, nm, nv = _adamw(tr(w), transposed[i], tr(m), tr(v), f"adamw_{i}")
            deltas.append(tr(dl)), new_m.append(tr(nm)), new_v.append(tr(nv))
            continue
        shp = w.shape
        two = lambda a: a.reshape(-1, shp[-1])
        dl, nm, nv = _adamw(two(w), two(g), two(m), two(v), f"adamw_{i}")
        deltas.append(dl.reshape(shp)), new_m.append(nm.reshape(shp)), new_v.append(nv.reshape(shp))

    def flat_pack(arrs):
        flat = jnp.concatenate([a.reshape(-1) for a in arrs])
        pad = (-flat.shape[0]) % (8 * HEAD)
        return jnp.pad(flat, (0, pad)).reshape(-1, HEAD)

    sw = flat_pack([weights[i] for i in small_idx])
    sgr = flat_pack([grads[i] for i in small_idx])
    smo = flat_pack([moms[i] for i in small_idx])
    sve = flat_pack([vels[i] for i in small_idx])
    sdl, snm, snv = _adamw(sw, sgr, smo, sve, "adamw_small")
    off = 0
    for i in small_idx:
        n = weights[i].size
        shp = weights[i].shape
        deltas[i] = sdl.reshape(-1)[off:off + n].reshape(shp)
        new_m[i] = snm.reshape(-1)[off:off + n].reshape(shp)
        new_v[i] = snv.reshape(-1)[off:off + n].reshape(shp)
        off += n

    return (loss, grad_x, *grads, *deltas, *new_m, *new_v)
```

```python
import math

import jax
import jax.numpy as jnp
from jax import lax
from jax.experimental import pallas as pl
from jax.experimental.pallas import tpu as pltpu

F32 = jnp.float32
BF16 = jnp.bfloat16
NDEV = 8
HEAD = 128
CHUNK = 64
SBLK = 256
EPS = 1e-6
LR, B1, B2, ADAM_EPS, WD, STEP = 0.001, 0.9, 0.999, 1e-08, 0.01, 10
NEG = -1e30
MM_VMEM_BUDGET = 40 * 1024 * 1024

NN = (((1,), (0,)), ((), ()))
NT = (((1,), (1,)), ((), ()))
TN = (((0,), (0,)), ((), ()))
BNN = (((2,), (1,)), ((0,), (0,)))
BNT = (((2,), (2,)), ((0,), (0,)))
BTN = (((1,), (1,)), ((0,), (0,)))
MESH = pl.DeviceIdType.MESH


def _dot(a, b, dims=NN):
    return lax.dot_general(a.astype(BF16), b.astype(BF16), dims, preferred_element_type=F32)


def _dot_hilo(a, b01_twice, one_dot):
    hi = a.astype(BF16)
    lo = (a - hi.astype(F32)).astype(BF16)
    if one_dot:
        return lax.dot_general(jnp.concatenate([hi, lo], axis=1), b01_twice, NN, preferred_element_type=F32)
    b01 = b01_twice[:a.shape[1]]
    return (lax.dot_general(hi, b01, NN, preferred_element_type=F32)
            + lax.dot_general(lo, b01, NN, preferred_element_type=F32))


def _pick(dim, cands):
    for c in cands:
        if dim % c == 0:
            return c
    return dim


def _params(sem, vmem_mb=48):
    return pltpu.CompilerParams(dimension_semantics=sem, vmem_limit_bytes=vmem_mb * 1024 * 1024)


def _silu(x):
    return x * jax.nn.sigmoid(x)


def _silu_and_grad(x):
    s = jax.nn.sigmoid(x)
    xs = x * s
    return xs, s + xs * (1.0 - s)


def _mm(a, b, mode, name, out_dtype=F32, res=None, norm_g=None, norm_bwd=None, after=None):
    if mode == "nn":
        (m, k), n = a.shape, b.shape[1]
    elif mode == "nt":
        (m, k), n = a.shape, b.shape[0]
    else:
        (k, m), n = a.shape, b.shape[1]
    rows = norm_g is not None or norm_bwd is not None
    tn = n if rows else _pick(n, (512, 256, 128))
    tk = k if k <= 4096 else max(t for t in range(128, 4097, 128) if k % t == 0)
    nk = k // tk
    out_b = jnp.dtype(out_dtype).itemsize + (res.dtype.itemsize if res is not None else 0)
    out_b += 2 if norm_g is not None else 0
    out_b += 10 if norm_bwd is not None else 0
    for tm in [t for t in range(min(m, 2048), 127, -128) if m % t == 0] + [m]:
        need = 2 * (tm * tk * a.dtype.itemsize + tk * tn * b.dtype.itemsize + tm * tn * out_b) + 4 * tm * tn
        if need <= MM_VMEM_BUDGET:
            break
    dims = {"nn": NN, "nt": NT, "tn": TN}[mode]
    if mode == "tn":
        a_spec = pl.BlockSpec((tk, tm), lambda i, j, kk: (kk, i))
    else:
        a_spec = pl.BlockSpec((tm, tk), lambda i, j, kk: (i, kk))
    if mode == "nt":
        b_spec = pl.BlockSpec((tn, tk), lambda i, j, kk: (j, kk))
    else:
        b_spec = pl.BlockSpec((tk, tn), lambda i, j, kk: (kk, j))
    mn_spec = pl.BlockSpec((tm, tn), lambda i, j, kk: (i, j))
    vec_spec = pl.BlockSpec((1, tn), lambda i, j, kk: (0, j))
    has_res = res is not None
    n_in = 2 + has_res + (1 if norm_g is not None else 0) + (3 if norm_bwd is not None else 0) + (after is not None)

    def body(*refs):
        a_ref, b_ref = refs[:2]
        extra = list(refs[2:n_in])
        outs = refs[n_in:-1]
        acc = refs[-1]
        kk = pl.program_id(2)

        @pl.when(kk == 0)
        def _():
            acc[...] = jnp.zeros_like(acc)

        if norm_bwd is not None:
            @pl.when((kk == 0) & (pl.program_id(0) == 0))
            def _():
                outs[2][...] = jnp.zeros_like(outs[2])

        acc[...] += _dot(a_ref[...], b_ref[...], dims)

        @pl.when(kk == nk - 1)
        def _():
            r = acc[...]
            if has_res:
                r = r + extra.pop(0)[...].astype(F32)
            if norm_g is not None:
                outs[0][...] = r.astype(out_dtype)
                rs = lax.rsqrt(jnp.mean(r * r, axis=-1, keepdims=True) + EPS)
                outs[1][...] = (r * rs * extra.pop(0)[...]).astype(BF16)
            elif norm_bwd is not None:
                xv, gv, dres = extra.pop(0)[...], extra.pop(0)[...], extra.pop(0)[...]
                rs = lax.rsqrt(jnp.mean(xv * xv, axis=-1, keepdims=True) + EPS)
                gdy = r * gv
                dx = dres + rs * gdy - xv * (rs * rs * rs) * jnp.mean(xv * gdy, axis=-1, keepdims=True)
                outs[0][...] = dx
                outs[1][...] = dx.astype(BF16)
                outs[2][...] += jnp.sum(r * xv * rs, axis=0, keepdims=True)
            else:
                outs[0][...] = r.astype(out_dtype)

    ins = [a, b] + ([res] if has_res else [])
    in_specs = [a_spec, b_spec] + ([mn_spec] if has_res else [])
    out_specs, out_shape = [mn_spec], [jax.ShapeDtypeStruct((m, n), out_dtype)]
    sem = ("parallel", "parallel", "arbitrary")
    if norm_g is not None:
        ins.append(norm_g.reshape(1, n))
        in_specs.append(vec_spec)
        out_specs.append(mn_spec)
        out_shape.append(jax.ShapeDtypeStruct((m, n), BF16))
    if norm_bwd is not None:
        x, g, dres = norm_bwd
        ins += [x, g.reshape(1, n), dres]
        in_specs += [mn_spec, vec_spec, mn_spec]
        out_specs += [mn_spec, vec_spec]
        out_shape += [jax.ShapeDtypeStruct((m, n), BF16), jax.ShapeDtypeStruct((1, n), F32)]
        sem = ("arbitrary", "arbitrary", "arbitrary")
    if after is not None:
        ins.append(after)
        in_specs.append(pl.BlockSpec(memory_space=pl.ANY))
    out = pl.pallas_call(
        body, name=name, grid=(m // tm, n // tn, nk), in_specs=in_specs, out_specs=out_specs,
        out_shape=out_shape, scratch_shapes=[pltpu.VMEM((tm, tn), F32)],
        compiler_params=_params(sem))(*ins)
    return out[0] if len(out) == 1 else out


def _rms_fwd(h, g, name):
    s, d = h.shape
    tm = _pick(s, (512, 256, 128))

    def body(h_ref, g_ref, o_ref):
        x = h_ref[...]
        r = lax.rsqrt(jnp.mean(x * x, axis=-1, keepdims=True) + EPS)
        o_ref[...] = (x * r * g_ref[...]).astype(BF16)

    return pl.pallas_call(
        body, name=name, grid=(s // tm,),
        in_specs=[pl.BlockSpec((tm, d), lambda i: (i, 0)), pl.BlockSpec((1, d), lambda i: (0, 0))],
        out_specs=pl.BlockSpec((tm, d), lambda i: (i, 0)),
        out_shape=jax.ShapeDtypeStruct((s, d), BF16), compiler_params=_params(("parallel",)))(h, g.reshape(1, d))


def _headnorm_fwd(x, g, name, scale=1.0, gate=None, gate_col0=0, out_dtype=BF16, width=None, head_major=False):
    if head_major:
        s, d = x.shape[1], x.shape[0] * HEAD
    else:
        s, d = x.shape[0], (width or x.shape[1])
    nh = d // HEAD
    tm = _pick(s, (256, 128))
    has_gate = gate is not None
    gb = gate_col0 // d

    def body(*refs):
        if has_gate:
            x_ref, g_ref, gt_ref, o_ref = refs
        else:
            x_ref, g_ref, o_ref = refs
        gv = g_ref[...]
        for h in range(nh):
            sl = slice(h * HEAD, (h + 1) * HEAD)
            xv = (x_ref[h] if head_major else x_ref[:, sl]).astype(F32)
            r = lax.rsqrt(jnp.mean(xv * xv, axis=-1, keepdims=True) + EPS)
            y = xv * r * gv
            if scale != 1.0:
                y = y * scale
            if has_gate:
                y = y * _silu(gt_ref[:, sl])
            o_ref[:, sl] = y.astype(out_dtype)

    row = pl.BlockSpec((tm, d), lambda i: (i, 0))
    hm = pl.BlockSpec((nh, tm, HEAD), lambda i: (0, i, 0))
    ins = [x, g.reshape(1, HEAD)]
    in_specs = [hm if head_major else row, pl.BlockSpec((1, HEAD), lambda i: (0, 0))]
    if has_gate:
        ins.append(gate)
        in_specs.append(pl.BlockSpec((tm, d), lambda i: (i, gb)))
    return pl.pallas_call(
        body, name=name, grid=(s // tm,), in_specs=in_specs, out_specs=row,
        out_shape=jax.ShapeDtypeStruct((s, d), out_dtype), compiler_params=_params(("parallel",)))(*ins)


def _headnorm_bwd(dy, x, g, name, scale=1.0, gate=None, gate_col0=0, dx_dtype=F32, head_major=False):
    s, d = dy.shape
    nh = d // HEAD
    tm = _pick(s, (256, 128))
    has_gate = gate is not None
    gb = gate_col0 // d

    def body(*refs):
        if has_gate:
            dy_ref, x_ref, g_ref, gt_ref, dx_ref, dg_ref, dgt_ref = refs
        else:
            dy_ref, x_ref, g_ref, dx_ref, dg_ref = refs

        @pl.when(pl.program_id(0) == 0)
        def _():
            dg_ref[...] = jnp.zeros_like(dg_ref)

        gv = g_ref[...]
        dg_acc = jnp.zeros((1, HEAD), F32)
        for h in range(nh):
            sl = slice(h * HEAD, (h + 1) * HEAD)
            xv = (x_ref[h] if head_major else x_ref[:, sl]).astype(F32)
            dyv = dy_ref[:, sl].astype(F32)
            r = lax.rsqrt(jnp.mean(xv * xv, axis=-1, keepdims=True) + EPS)
            if has_gate:
                gt = gt_ref[:, sl]
                act, dact = _silu_and_grad(gt)
                dgt_ref[:, sl] = (dyv * (xv * r * gv) * dact).astype(dgt_ref.dtype)
                dn = dyv * act
            else:
                dn = dyv
            if scale != 1.0:
                dn = dn * scale
            gdn = dn * gv
            mean_t = jnp.mean(xv * gdn, axis=-1, keepdims=True)
            dxv = (r * gdn - xv * (r * r * r) * mean_t).astype(dx_dtype)
            if head_major:
                dx_ref[h] = dxv
            else:
                dx_ref[:, sl] = dxv
            dg_acc = dg_acc + jnp.sum(dn * xv * r, axis=0, keepdims=True)
        dg_ref[...] += dg_acc

    row = pl.BlockSpec((tm, d), lambda i: (i, 0))
    hm = pl.BlockSpec((nh, tm, HEAD), lambda i: (0, i, 0))
    vec = pl.BlockSpec((1, HEAD), lambda i: (0, 0))
    ins = [dy, x, g.reshape(1, HEAD)]
    in_specs = [row, hm if head_major else row, vec]
    out_specs = [hm if head_major else row, vec]
    dx_shape = (nh, s, HEAD) if head_major else (s, d)
    out_shape = [jax.ShapeDtypeStruct(dx_shape, dx_dtype), jax.ShapeDtypeStruct((1, HEAD), F32)]
    if has_gate:
        ins.append(gate)
        in_specs.append(pl.BlockSpec((tm, d), lambda i: (i, gb)))
        out_specs.append(row)
        out_shape.append(jax.ShapeDtypeStruct((s, d), BF16))
    return pl.pallas_call(
        body, name=name, grid=(s // tm,), in_specs=in_specs, out_specs=out_specs, out_shape=out_shape,
        compiler_params=_params(("arbitrary",)))(*ins)


def _swiglu_fwd(hn, wf_t, name):
    s, d = hn.shape
    f = wf_t.shape[0] // 2
    tm = _pick(s, (1024, 512, 256, 128))
    tn = _pick(f, (512, 256, 128))
    nj = f // tn

    def body(a_ref, wg_ref, wu_ref, act_ref, g_ref, u_ref):
        a = a_ref[...]
        g = _dot(a, wg_ref[...], NT)
        u = _dot(a, wu_ref[...], NT)
        act_ref[...] = (_silu(g) * u).astype(BF16)
        g_ref[...] = g.astype(BF16)
        u_ref[...] = u.astype(BF16)

    o_spec = pl.BlockSpec((tm, tn), lambda i, j: (i, j))
    sds = jax.ShapeDtypeStruct((s, f), BF16)
    return pl.pallas_call(
        body, name=name, grid=(s // tm, nj),
        in_specs=[pl.BlockSpec((tm, d), lambda i, j: (i, 0)), pl.BlockSpec((tn, d), lambda i, j: (j, 0)),
                  pl.BlockSpec((tn, d), lambda i, j: (j + nj, 0))],
        out_specs=[o_spec, o_spec, o_spec], out_shape=[sds, sds, sds],
        compiler_params=_params(("parallel", "parallel")))(hn, wf_t, wf_t)


def _swiglu_bwd(dh, w_out, g, u, name):
    s, d = dh.shape
    f = w_out.shape[0]
    tm = _pick(s, (1024, 512, 256, 128))
    tn = _pick(f, (512, 256, 128))

    def body(dh_ref, w_ref, g_ref, u_ref, dgu_ref):
        j = pl.program_id(1)
        dact = _dot(dh_ref[...], w_ref[...], NT)
        gv = g_ref[...].astype(F32)
        uv = u_ref[...].astype(F32)
        sg, dsg = _silu_and_grad(gv)
        dgu_ref[:, pl.ds(pl.multiple_of(j * tn, HEAD), tn)] = (dact * uv * dsg).astype(BF16)
        dgu_ref[:, pl.ds(pl.multiple_of(f + j * tn, HEAD), tn)] = (dact * sg).astype(BF16)

    o_spec = pl.BlockSpec((tm, tn), lambda i, j: (i, j))
    return pl.pallas_call(
        body, name=name, grid=(s // tm, f // tn),
        in_specs=[pl.BlockSpec((tm, d), lambda i, j: (i, 0)), pl.BlockSpec((tn, d), lambda i, j: (j, 0)), o_spec, o_spec],
        out_specs=pl.BlockSpec((tm, 2 * f), lambda i, j: (i, 0)), out_shape=jax.ShapeDtypeStruct((s, 2 * f), BF16),
        compiler_params=_params(("parallel", "arbitrary")))(dh, w_out, g, u)


def _ple_fwd(h, hn, p, w_gate, wp_t, name, norm_gs=()):
    s, d = h.shape
    pd = p.shape[1]
    tm = _pick(s, (512, 256, 128))
    ng = len(norm_gs)

    def body(h_ref, hn_ref, p_ref, wg_ref, wp_ref, *rest):
        g_refs, (o_ref, gp_ref, pp_ref), n_refs = rest[:ng], rest[ng:ng + 3], rest[ng + 3:]
        gpre = _dot(hn_ref[...], wg_ref[...], NN)
        pp = _dot(p_ref[...], wp_ref[...], NT)
        o = h_ref[...] + pp * jax.nn.sigmoid(gpre)
        o_ref[...] = o
        gp_ref[...] = gpre.astype(BF16)
        pp_ref[...] = pp.astype(BF16)
        if ng:
            on = o * lax.rsqrt(jnp.mean(o * o, axis=-1, keepdims=True) + EPS)
            for g_ref, n_ref in zip(g_refs, n_refs):
                n_ref[...] = (on * g_ref[...]).astype(BF16)

    row = pl.BlockSpec((tm, d), lambda i: (i, 0))
    vec = pl.BlockSpec((1, d), lambda i: (0, 0))
    bf = jax.ShapeDtypeStruct((s, d), BF16)
    return pl.pallas_call(
        body, name=name, grid=(s // tm,),
        in_specs=[row, row, pl.BlockSpec((tm, pd), lambda i: (i, 0)), pl.BlockSpec((d, d), lambda i: (0, 0)),
                  pl.BlockSpec((d, pd), lambda i: (0, 0))] + [vec] * ng,
        out_specs=[row] * (3 + ng), out_shape=[jax.ShapeDtypeStruct((s, d), F32), bf, bf] + [bf] * ng,
        compiler_params=_params(("parallel",)))(h, hn, p, w_gate, wp_t, *[g.reshape(1, d) for g in norm_gs])


def _ple_bwd(dh, gpre, pp, name):
    s, d = dh.shape
    tm = _pick(s, (512, 256, 128))

    def body(dh_ref, gp_ref, pp_ref, dgp_ref, dpp_ref):
        dv = dh_ref[...]
        sig = jax.nn.sigmoid(gp_ref[...].astype(F32))
        ppv = pp_ref[...].astype(F32)
        dpp_ref[...] = (dv * sig).astype(BF16)
        dgp_ref[...] = (dv * ppv * sig * (1.0 - sig)).astype(BF16)

    row = pl.BlockSpec((tm, d), lambda i: (i, 0))
    sds = jax.ShapeDtypeStruct((s, d), BF16)
    return pl.pallas_call(
        body, name=name, grid=(s // tm,), in_specs=[row, row, row], out_specs=[row, row], out_shape=[sds, sds],
        compiler_params=_params(("parallel",)))(dh, gpre, pp)


def _loss_fwd_bwd(y, t, name):
    s, d = y.shape
    tm = _pick(s, (512, 256, 128))

    def body(y_ref, t_ref, dy_ref, l_ref):
        @pl.when(pl.program_id(0) == 0)
        def _():
            l_ref[...] = jnp.zeros_like(l_ref)

        e = y_ref[...] - t_ref[...]
        dy_ref[...] = e * (1.0 / d)
        l_ref[...] += jnp.sum(e * e, axis=0, keepdims=True) * (0.5 / d)

    row = pl.BlockSpec((tm, d), lambda i: (i, 0))
    vec = pl.BlockSpec((1, d), lambda i: (0, 0))
    return pl.pallas_call(
        body, name=name, grid=(s // tm,), in_specs=[row, row], out_specs=[row, vec],
        out_shape=[jax.ShapeDtypeStruct((s, d), F32), jax.ShapeDtypeStruct((1, d), F32)],
        compiler_params=_params(("arbitrary",)))(y, t)


PADR = 8


def _conv_fwd(proj, w_conv, d, name):
    s = proj.shape[0]
    nh = d // HEAD
    kw = w_conv.shape[0]
    qscale = HEAD ** -0.5

    def body(x_ref, w_ref, o_ref, xp):
        kind = pl.program_id(0) // nh
        xp[0:PADR, :] = jnp.zeros((PADR, HEAD), F32)
        xp[PADR:, :] = x_ref[...]
        acc = jnp.zeros((s, HEAD), F32)
        for j in range(kw):
            acc = acc + w_ref[j:j + 1, :] * xp[PADR - (kw - 1) + j:PADR - (kw - 1) + j + s, :]
        a = _silu(acc)
        r = lax.rsqrt(jnp.sum(a * a, axis=-1, keepdims=True) + EPS)
        fac = jnp.where(kind == 0, r * qscale, jnp.where(kind == 1, r, jnp.ones_like(r)))
        o_ref[...] = a * fac

    blk = pl.BlockSpec((s, HEAD), lambda c: (0, c))
    hm = pl.BlockSpec((None, s, HEAD), lambda c: (c, 0, 0))
    return pl.pallas_call(
        body, name=name, grid=(3 * nh,), in_specs=[blk, pl.BlockSpec((kw, HEAD), lambda c: (0, c))], out_specs=hm,
        out_shape=jax.ShapeDtypeStruct((3 * nh, s, HEAD), F32), scratch_shapes=[pltpu.VMEM((s + PADR, HEAD), F32)],
        compiler_params=_params(("parallel",)))(proj, w_conv)


def _conv_bwd(dqkv, proj, w_conv, d, name):
    s = proj.shape[0]
    nh = d // HEAD
    kw = w_conv.shape[0]
    qscale = HEAD ** -0.5

    def body(dy_ref, x_ref, w_ref, dx_ref, dw_ref, xp, dp):
        kind = pl.program_id(0) // nh
        xp[0:PADR, :] = jnp.zeros((PADR, HEAD), F32)
        xp[PADR:, :] = x_ref[...]
        acc = jnp.zeros((s, HEAD), F32)
        for j in range(kw):
            acc = acc + w_ref[j:j + 1, :] * xp[PADR - (kw - 1) + j:PADR - (kw - 1) + j + s, :]
        a, da_dacc = _silu_and_grad(acc)
        dy = dy_ref[...]
        r = lax.rsqrt(jnp.sum(a * a, axis=-1, keepdims=True) + EPS)
        sc = jnp.where(kind == 0, qscale, 1.0)
        dyn = dy * sc
        da_norm = r * dyn - a * (r * r * r) * jnp.sum(a * dyn, axis=-1, keepdims=True)
        da = jnp.where(kind == 2, dy, da_norm)
        dacc = da * da_dacc
        dp[0:s, :] = dacc
        dp[s:, :] = jnp.zeros((PADR, HEAD), F32)
        dx = jnp.zeros((s, HEAD), F32)
        for j in range(kw):
            sh = kw - 1 - j
            dx = dx + w_ref[j:j + 1, :] * dp[sh:sh + s, :]
            dw_ref[j:j + 1, :] = jnp.sum(dacc * xp[PADR - sh:PADR - sh + s, :], axis=0, keepdims=True)
        dx_ref[...] = dx.astype(BF16)

    blk = pl.BlockSpec((s, HEAD), lambda c: (0, c))
    hm = pl.BlockSpec((None, s, HEAD), lambda c: (c, 0, 0))
    wblk = pl.BlockSpec((kw, HEAD), lambda c: (0, c))
    return pl.pallas_call(
        body, name=name, grid=(3 * nh,), in_specs=[hm, blk, wblk], out_specs=[blk, wblk],
        out_shape=[jax.ShapeDtypeStruct((s, 3 * d), BF16), jax.ShapeDtypeStruct((kw, 3 * d), F32)],
        scratch_shapes=[pltpu.VMEM((s + PADR, HEAD), F32), pltpu.VMEM((s + PADR, HEAD), F32)],
        compiler_params=_params(("parallel",)))(dqkv, proj, w_conv)


def _softplus(x):
    return jnp.maximum(x, 0.0) + jnp.log(1.0 + jnp.exp(-jnp.abs(x)))


def _gates_fwd(pab, a_log, dt_bias, nh, name):
    s = pab.shape[0]
    tm = _pick(s, (512, 256, 128))

    def body(x_ref, al_ref, dt_ref, o_ref):
        x = x_ref[...]
        lane = lax.broadcasted_iota(jnp.int32, x.shape, 1)
        g = -jnp.exp(al_ref[...]) * _softplus(x + dt_ref[...])
        o_ref[...] = jnp.where(lane < nh, g, jnp.where(lane < 2 * nh, jax.nn.sigmoid(x), 0.0))

    row = pl.BlockSpec((tm, HEAD), lambda i: (i, 0))
    vec = pl.BlockSpec((1, HEAD), lambda i: (0, 0))
    return pl.pallas_call(
        body, name=name, grid=(s // tm,), in_specs=[row, vec, vec], out_specs=row,
        out_shape=jax.ShapeDtypeStruct((s, HEAD), F32), compiler_params=_params(("parallel",)))(pab, a_log, dt_bias)


def _gates_bwd(dgb, pab, a_log, dt_bias, nh, name):
    s = pab.shape[0]
    tm = _pick(s, (512, 256, 128))

    def body(d_ref, x_ref, al_ref, dt_ref, dx_ref, dal_ref, ddt_ref):
        @pl.when(pl.program_id(0) == 0)
        def _():
            dal_ref[...] = jnp.zeros_like(dal_ref)
            ddt_ref[...] = jnp.zeros_like(ddt_ref)

        x = x_ref[...]
        dv = d_ref[...]
        lane = lax.broadcasted_iota(jnp.int32, x.shape, 1)
        ea = jnp.exp(al_ref[...])
        xs = x + dt_ref[...]
        g = -ea * _softplus(xs)
        dxs = jnp.where(lane < nh, dv * (-ea) * jax.nn.sigmoid(xs), 0.0)
        sg = jax.nn.sigmoid(x)
        dxb = jnp.where((lane >= nh) & (lane < 2 * nh), dv * sg * (1.0 - sg), 0.0)
        dx_ref[...] = (dxs + dxb).astype(BF16)
        dal_ref[...] += jnp.sum(jnp.where(lane < nh, dv * g, 0.0), axis=0, keepdims=True)
        ddt_ref[...] += jnp.sum(dxs, axis=0, keepdims=True)

    row = pl.BlockSpec((tm, HEAD), lambda i: (i, 0))
    vec = pl.BlockSpec((1, HEAD), lambda i: (0, 0))
    return pl.pallas_call(
        body, name=name, grid=(s // tm,), in_specs=[row, row, vec, vec], out_specs=[row, vec, vec],
        out_shape=[jax.ShapeDtypeStruct((s, HEAD), BF16), jax.ShapeDtypeStruct((1, HEAD), F32),
                   jax.ShapeDtypeStruct((1, HEAD), F32)],
        compiler_params=_params(("arbitrary",)))(dgb, pab, a_log, dt_bias)


def _tri_inv(a_low, eye_f):
    n = -a_low
    p = eye_f + n
    steps = int(math.log2(a_low.shape[-1])) - 1
    for _ in range(steps):
        n = _dot(n, n, BNN)
        p = p + _dot(p, n, BNN)
    return p


def _lane_col(x, lane, idx):
    return jnp.sum(jnp.where(lane == idx, x, 0.0), axis=1, keepdims=True)


def _head_cols(gbv, lo, nh):
    lane = lax.broadcasted_iota(jnp.int32, gbv.shape, 1)
    return jnp.stack([_lane_col(gbv, lane, lo + h) for h in range(nh)], axis=0)


def _gdn_chunk(q, k, v, g_col, beta_col, st):
    c = q.shape[1]
    r_i = lax.broadcasted_iota(jnp.int32, (c, c), 0)
    c_i = lax.broadcasted_iota(jnp.int32, (c, c), 1)
    incl = c_i <= r_i
    strict = c_i < r_i
    eye = c_i == r_i
    g_row = jnp.sum(jnp.where(eye, g_col, 0.0), axis=1, keepdims=True)
    gc_col = jnp.sum(jnp.where(incl, g_row, 0.0), axis=2, keepdims=True)
    gc_row = jnp.sum(jnp.where(eye, gc_col, 0.0), axis=1, keepdims=True)
    g_last = jnp.sum(g_col, axis=1, keepdims=True)
    decay = jnp.exp(jnp.where(incl, gc_col - gc_row, NEG))
    kk = _dot(k, k, BNT)
    a_low = jnp.where(strict, beta_col * kk * decay, 0.0)
    t_inv = _tri_inv(a_low, eye.astype(F32))
    e_g = jnp.exp(gc_col)
    bk = beta_col * e_g
    rhs = jnp.concatenate([v * beta_col, k * bk], axis=2)
    sol = _dot(t_inv, rhs, BNN)
    u, w = sol[:, :, :HEAD], sol[:, :, HEAD:]
    qk_raw = _dot(q, k, BNT)
    qk = qk_raw * decay
    q_dec = q * e_g
    e2 = jnp.exp(g_last - gc_col)
    k_dec = k * e2
    gl = jnp.exp(g_last)
    ws = _dot(jnp.concatenate([w, q_dec], axis=1), st, BNN)
    v_new = u - ws[:, :c]
    o = ws[:, c:] + _dot(qk, v_new, BNN)
    st_new = st * gl + _dot(k_dec, v_new, BTN)
    inter = dict(incl=incl, strict=strict, eye=eye, decay=decay, kk=kk, t_inv=t_inv, e_g=e_g, bk=bk, sol=sol, w=w,
                 qk_raw=qk_raw, qk=qk, q_dec=q_dec, e2=e2, k_dec=k_dec, gl=gl, v_new=v_new, c_i=c_i, r_i=r_i)
    return o, st_new, inter


def _gdn_fwd(qkv, gb, nh, name):
    s = qkv.shape[1]
    nc = s // CHUNK

    def body(q_ref, k_ref, v_ref, gb_ref, o_ref, st_ref, state):
        @pl.when(pl.program_id(0) == 0)
        def _():
            state[...] = jnp.zeros_like(state)

        gbv = gb_ref[...]
        st = state[...]
        st_ref[...] = st
        o, st_new, _ = _gdn_chunk(q_ref[...], k_ref[...], v_ref[...], _head_cols(gbv, 0, nh), _head_cols(gbv, nh, nh), st)
        o_ref[...] = o
        state[...] = st_new

    def qspec(part):
        return pl.BlockSpec((nh, CHUNK, HEAD), lambda n: (part, n, 0))

    return pl.pallas_call(
        body, name=name, grid=(nc,),
        in_specs=[qspec(0), qspec(1), qspec(2), pl.BlockSpec((CHUNK, HEAD), lambda n: (n, 0))],
        out_specs=[qspec(0), pl.BlockSpec((None, nh, HEAD, HEAD), lambda n: (n, 0, 0, 0))],
        out_shape=[jax.ShapeDtypeStruct((nh, s, HEAD), F32), jax.ShapeDtypeStruct((nc, nh, HEAD, HEAD), F32)],
        scratch_shapes=[pltpu.VMEM((nh, HEAD, HEAD), F32)],
        compiler_params=_params(("arbitrary",)))(qkv, qkv, qkv, gb)


def _gdn_bwd(qkv, gb, do, states, nh, name):
    s = qkv.shape[1]
    nc = s // CHUNK
    c = CHUNK

    def body(q_ref, k_ref, v_ref, gb_ref, do_ref, st_ref, dqkv_ref, dgb_ref, dstate):
        @pl.when(pl.program_id(0) == 0)
        def _():
            dstate[...] = jnp.zeros_like(dstate)

        gbv = gb_ref[...]
        lane = lax.broadcasted_iota(jnp.int32, gbv.shape, 1)
        q, k, v = q_ref[...], k_ref[...], v_ref[...]
        beta_col = _head_cols(gbv, nh, nh)
        st = st_ref[...]
        dst = dstate[...]
        dov = do_ref[...]
        _, _, it = _gdn_chunk(q, k, v, _head_cols(gbv, 0, nh), beta_col, st)
        incl, strict, eye, decay = it["incl"], it["strict"], it["eye"], it["decay"]
        dv_new = _dot(it["qk"], dov, BTN) + _dot(it["k_dec"], dst, BNN)
        d_qk = _dot(dov, it["v_new"], BNT)
        dd = _dot(jnp.concatenate([dov, -dv_new], axis=1), st, BNT)
        dq_dec, dw = dd[:, :c], dd[:, c:]
        dst_new = _dot(it["q_dec"], dov, BTN) + it["gl"] * dst - _dot(it["w"], dv_new, BTN)
        dgl = jnp.sum(jnp.sum(dst * st, axis=2, keepdims=True), axis=1, keepdims=True)
        dk_dec = _dot(it["v_new"], dst, BNT)
        dsol = jnp.concatenate([dv_new, dw], axis=2)
        drhs = _dot(it["t_inv"], dsol, BTN)
        d_a = jnp.where(strict, -_dot(drhs, it["sol"], BNT), 0.0)
        drhs_u, drhs_w = drhs[:, :, :HEAD], drhs[:, :, HEAD:]
        dvh = beta_col * drhs_u
        rw_k = jnp.sum(drhs_w * k, axis=2, keepdims=True)
        dbeta = jnp.sum(drhs_u * v, axis=2, keepdims=True) + it["e_g"] * rw_k
        dkh = it["bk"] * drhs_w
        dgc_col = it["bk"] * rw_k
        dkk = d_a * beta_col * decay
        dbeta = dbeta + jnp.sum(d_a * it["kk"] * decay, axis=2, keepdims=True)
        ddecay = d_a * beta_col * it["kk"]
        dkh = dkh + _dot(dkk, k, BNN) + _dot(dkk, k, BTN)
        dqk_raw = d_qk * decay
        ddecay = ddecay + d_qk * it["qk_raw"]
        dqh = _dot(dqk_raw, k, BNN)
        dkh = dkh + _dot(dqk_raw, q, BTN)
        ddm = jnp.where(incl, ddecay * decay, 0.0)
        dgc_col = dgc_col + jnp.sum(ddm, axis=2, keepdims=True)
        dgc_row = -jnp.sum(ddm, axis=1, keepdims=True)
        dqh = dqh + dq_dec * it["e_g"]
        dgc_col = dgc_col + jnp.sum(dq_dec * it["q_dec"], axis=2, keepdims=True)
        dkh = dkh + dk_dec * it["e2"]
        tmp = jnp.sum(dk_dec * it["k_dec"], axis=2, keepdims=True)
        dgc_col = dgc_col - tmp
        dg_last = jnp.sum(tmp, axis=1, keepdims=True) + dgl * it["gl"]
        dgc_tot_row = dgc_row + jnp.sum(jnp.where(eye, dgc_col, 0.0), axis=1, keepdims=True)
        dg_col = jnp.sum(jnp.where(it["c_i"] >= it["r_i"], dgc_tot_row, 0.0), axis=2, keepdims=True) + dg_last
        dqkv_ref[0] = dqh
        dqkv_ref[1] = dkh
        dqkv_ref[2] = dvh
        dstate[...] = dst_new
        dgb_acc = jnp.zeros(gbv.shape, F32)
        for h in range(nh):
            dgb_acc = jnp.where(lane == h, dg_col[h], jnp.where(lane == nh + h, dbeta[h], dgb_acc))
        dgb_ref[...] = dgb_acc

    def rev(part):
        return pl.BlockSpec((nh, CHUNK, HEAD), lambda n: (part, nc - 1 - n, 0))

    gspec = pl.BlockSpec((CHUNK, HEAD), lambda n: (nc - 1 - n, 0))
    dqkv, dgb = pl.pallas_call(
        body, name=name, grid=(nc,),
        in_specs=[rev(0), rev(1), rev(2), gspec, rev(0),
                  pl.BlockSpec((None, nh, HEAD, HEAD), lambda n: (nc - 1 - n, 0, 0, 0))],
        out_specs=[pl.BlockSpec((3, nh, CHUNK, HEAD), lambda n: (0, 0, nc - 1 - n, 0)), gspec],
        out_shape=[jax.ShapeDtypeStruct((3, nh, s, HEAD), F32), jax.ShapeDtypeStruct((s, HEAD), F32)],
        scratch_shapes=[pltpu.VMEM((nh, HEAD, HEAD), F32)],
        compiler_params=_params(("arbitrary",)))(qkv, qkv, qkv, gb, do, states)
    return dqkv.reshape(3 * nh, s, HEAD), dgb


SB_TQ_FWD = 1024
SB_TQ = 512


def _tri01(rel):
    j_i = lax.broadcasted_iota(jnp.int32, (2 * SBLK, SBLK), 0) & (SBLK - 1)
    s_i = lax.broadcasted_iota(jnp.int32, (2 * SBLK, SBLK), 1)
    return rel(j_i, s_i).astype(BF16)


SB_HP = 2


def _each(fn, *lists):
    return [fn(*xs) for xs in zip(*lists)]


def _sb_scores(qts, kblks, mask, csums, rhs01, one_dot):
    zs = _each(lambda qt, kb: _dot(qt, kb, NT), qts, kblks)
    es = _each(lambda z: jnp.exp(-jnp.abs(z)), zs)
    sps = _each(lambda z, e: jnp.maximum(z, 0.0) + jnp.log(1.0 + e), zs, es)
    lns = _each(lambda sp: -sp if mask is None else jnp.where(mask, -sp, 0.0), sps)
    sts = _each(lambda ln: _dot_hilo(ln, rhs01, one_dot), lns)
    wgts = _each(lambda z, sp, st, cs: jnp.exp((z - sp) + st + cs), zs, sps, sts, csums)
    if mask is not None:
        wgts = _each(lambda w: jnp.where(mask, w, 0.0), wgts)
    return zs, es, wgts, lns


def _band_mask(rows, j, row0):
    r_i = lax.broadcasted_iota(jnp.int32, (rows, SBLK), 0)
    c_i = lax.broadcasted_iota(jnp.int32, (rows, SBLK), 1)
    return (j * SBLK + c_i) < (row0 + r_i)


def _sb_fwd(q, k, v, name):
    s, d = q.shape
    nh = d // HEAD
    tq = min(SB_TQ_FWD, s)
    nb = tq // SBLK

    hp = SB_HP
    heads = [slice(h * HEAD, (h + 1) * HEAD) for h in range(hp)]

    def body(q_ref, k_ref, v_ref, o_ref, c_ref, acc, cs):
        qb = pl.program_id(1)
        lane = lax.broadcasted_iota(jnp.int32, (tq, HEAD), 1)
        after = _tri01(lambda j, t: j > t)
        acc[...] = jnp.zeros_like(acc)
        cs[...] = jnp.zeros_like(cs)
        c_ref[...] = jnp.zeros_like(c_ref)

        def process(rs, kb, mask):
            keys = pl.ds(pl.multiple_of(kb * SBLK, SBLK), SBLK)
            csums = [cs[h, rs, :] for h in range(hp)]
            _, _, wgts, lns = _sb_scores([q_ref[rs, hs] for hs in heads], [k_ref[keys, hs] for hs in heads], mask, csums, after, True)
            pvs = _each(lambda w, hs: _dot(w, v_ref[keys, hs]), wgts, heads)
            tots = _each(lambda ln: jnp.sum(ln, axis=1, keepdims=True), lns)
            for h, hs in enumerate(heads):
                acc[h, rs, :] += pvs[h]
                c_ref[rs, hs] = jnp.where(lane[rs, :] == kb, csums[h], c_ref[rs, hs])
                cs[h, rs, :] = csums[h] + tots[h]

        for j in reversed(range(nb)):
            process(slice(j * SBLK, tq), qb * nb + j, _band_mask(tq - j * SBLK, j, j * SBLK))

        def step(it, carry):
            process(slice(0, tq), qb * nb - 1 - it, None)
            return carry

        lax.fori_loop(0, qb * nb, step, 0)
        for h, hs in enumerate(heads):
            o_ref[:, hs] = acc[h].astype(BF16)

    qspec = pl.BlockSpec((tq, hp * HEAD), lambda h, i: (i, h))
    kspec = pl.BlockSpec((s, hp * HEAD), lambda h, i: (0, h))
    return pl.pallas_call(
        body, name=name, grid=(nh // hp, s // tq), in_specs=[qspec, kspec, kspec], out_specs=[qspec, qspec],
        out_shape=[jax.ShapeDtypeStruct((s, d), BF16), jax.ShapeDtypeStruct((s, d), F32)],
        scratch_shapes=[pltpu.VMEM((hp, tq, HEAD), F32), pltpu.VMEM((hp, tq, 1), F32)],
        compiler_params=_params(("parallel", "arbitrary")))(q, k, v)


def _sb_bwd(q, k, v, do, ctab, name):
    s, d = q.shape
    nh = d // HEAD
    tq = min(SB_TQ, s)
    nb = tq // SBLK

    hp = SB_HP
    heads = [slice(h * HEAD, (h + 1) * HEAD) for h in range(hp)]

    def body(q_ref, k_ref, v_ref, do_ref, c_ref, dq_ref, dk_ref, dv_ref, ps):
        qb = pl.program_id(1)

        @pl.when(qb == 0)
        def _():
            dk_ref[...] = jnp.zeros_like(dk_ref)
            dv_ref[...] = jnp.zeros_like(dv_ref)

        dq_ref[...] = jnp.zeros_like(dq_ref)
        ps[...] = jnp.zeros_like(ps)
        lane = lax.broadcasted_iota(jnp.int32, (tq, HEAD), 1)
        after = _tri01(lambda j, t: j > t)
        before = _tri01(lambda j, t: j < t)

        def process(rs, kb, mask):
            keys = pl.ds(pl.multiple_of(kb * SBLK, SBLK), SBLK)
            kblks = [k_ref[keys, hs] for hs in heads]
            qts = [q_ref[rs, hs] for hs in heads]
            dots = [do_ref[rs, hs] for hs in heads]
            csums = [_lane_col(c_ref[rs, hs], lane[rs, :], kb) for hs in heads]
            zs, es, wgts, _ = _sb_scores(qts, kblks, mask, csums, after, False)
            dlws = _each(lambda dt, hs, w: _dot(dt, v_ref[keys, hs], NT) * w, dots, heads, wgts)
            pts = _each(lambda dlw: _dot_hilo(dlw, before, False), dlws)
            pfxs = [ps[h, rs, :] for h in range(hp)]
            rs_ = _each(lambda e: 1.0 / (1.0 + e), es)
            sigs = _each(lambda z, e, r: jnp.where(z >= 0.0, r, e * r), zs, es, rs_)
            dzs = _each(lambda dlw, sig, pfx, pt: dlw * (1.0 - sig) - sig * (pfx + pt), dlws, sigs, pfxs, pts)
            tots = _each(lambda dlw: jnp.sum(dlw, axis=1, keepdims=True), dlws)
            if mask is not None:
                dzs = _each(lambda dz: jnp.where(mask, dz, 0.0), dzs)
            dqs = _each(lambda dz, kb_: _dot(dz, kb_), dzs, kblks)
            dks = _each(lambda dz, qt: _dot(dz, qt, TN), dzs, qts)
            dvs = _each(lambda w, dt: _dot(w, dt, TN), wgts, dots)
            for h, hs in enumerate(heads):
                dq_ref[rs, hs] += dqs[h]
                dk_ref[keys, hs] += dks[h]
                dv_ref[keys, hs] += dvs[h]
                ps[h, rs, :] = pfxs[h] + tots[h]

        def step(kb, carry):
            process(slice(0, tq), kb, None)
            return carry

        lax.fori_loop(0, qb * nb, step, 0)
        for j in range(nb):
            process(slice(j * SBLK, tq), qb * nb + j, _band_mask(tq - j * SBLK, j, j * SBLK))

    qspec = pl.BlockSpec((tq, hp * HEAD), lambda h, i: (i, h))
    kspec = pl.BlockSpec((s, hp * HEAD), lambda h, i: (0, h))
    sds = jax.ShapeDtypeStruct((s, d), F32)
    return pl.pallas_call(
        body, name=name, grid=(nh // hp, s // tq), in_specs=[qspec, kspec, kspec, qspec, qspec],
        out_specs=[qspec, kspec, kspec], out_shape=[sds, sds, sds],
        scratch_shapes=[pltpu.VMEM((hp, tq, 1), F32)],
        compiler_params=_params(("parallel", "arbitrary")))(q, k, v, do, ctab)


def _my_index():
    return 4 * lax.axis_index("x") + 2 * lax.axis_index("y") + lax.axis_index("c")


def _all_gather(x_shard, name):
    m_per, n = x_shard.shape

    def body(x_ref, out_ref, send_sems, recv_sems, local_sem):
        x, y, c = lax.axis_index("x"), lax.axis_index("y"), lax.axis_index("c")
        me, sibling = (x, y, c), (x, y, 1 - c)
        chips = [(1 - x, y), (x, 1 - y), (1 - x, 1 - y)]

        def rows(px, py, pc):
            return out_ref.at[pl.ds((4 * px + 2 * py + pc) * m_per, m_per), :]

        def copy(k, block, to, src=None):
            return pltpu.make_async_remote_copy(
                src_ref=rows(*block) if src is None else src, dst_ref=rows(*block),
                send_sem=send_sems.at[k], recv_sem=recv_sems.at[k], device_id=to, device_id_type=MESH)

        mine = pltpu.make_async_copy(x_ref, rows(*me), local_sem)
        mine.start()
        first = [copy(0, me, sibling, src=x_ref)]
        first += [copy(1 + j, me, (*chip, c), src=x_ref) for j, chip in enumerate(chips)]
        for cp in first:
            cp.start()
        passed = [copy(4 + j, (*chip, c), sibling) for j, chip in enumerate(chips)]
        for j, chip in enumerate(chips):
            copy(1 + j, (*chip, c), me).wait_recv()
            passed[j].start()
        copy(0, sibling, me).wait_recv()
        for j, chip in enumerate(chips):
            copy(4 + j, (*chip, 1 - c), me).wait_recv()
        for cp in first + passed:
            cp.wait_send()
        mine.wait()

    return pl.pallas_call(
        body, name=name, out_shape=jax.ShapeDtypeStruct((NDEV * m_per, n), x_shard.dtype),
        in_specs=[pl.BlockSpec(memory_space=pl.ANY)], out_specs=pl.BlockSpec(memory_space=pl.ANY),
        scratch_shapes=[pltpu.SemaphoreType.DMA((7,)), pltpu.SemaphoreType.DMA((7,)), pltpu.SemaphoreType.DMA],
    )(x_shard)


HBM_SPEC = pl.BlockSpec(memory_space=pltpu.HBM)
SEM_SPEC = pl.BlockSpec(memory_space=pltpu.SEMAPHORE)
ANY_SPEC = pl.BlockSpec(memory_space=pl.ANY)
EFFECT = pltpu.SideEffectType.DATAFLOW_SIDE_EFFECTING


def _exchange_copies(src_refs, land_refs, send_sems, recv_sems, self_sems, scatter):
    x, y, c = lax.axis_index("x"), lax.axis_index("y"), lax.axis_index("c")
    me = 4 * x + 2 * y + c
    remote, local = [], []
    for p, (src_ref, land_ref) in enumerate(zip(src_refs, land_refs)):
        rows = land_ref.shape[0] // NDEV

        def part(idx):
            return src_ref.at[pl.ds(idx * rows, rows), :] if scatter else src_ref

        slot = land_ref.at[pl.ds(me * rows, rows), :]
        for k in range(1, NDEV):
            px, py, pc = x ^ ((k >> 2) & 1), y ^ ((k >> 1) & 1), c ^ (k & 1)
            remote.append(pltpu.make_async_remote_copy(
                src_ref=part(4 * px + 2 * py + pc), dst_ref=slot, send_sem=send_sems.at[7 * p + k - 1],
                recv_sem=recv_sems.at[7 * p + k - 1], device_id=(px, py, pc), device_id_type=MESH))
        local.append(pltpu.make_async_copy(part(me), slot, self_sems.at[p]))
    return remote, local


def _send_start(srcs, scatter, after, name):
    n = len(srcs)
    lands = []
    for s in srcs:
        rows = s.shape[0] if scatter else NDEV * s.shape[0]
        lands.append(pltpu.with_memory_space_constraint(lax.empty((rows, s.shape[1]), s.dtype), pltpu.HBM))

    def body(*refs):
        src_refs, land_refs = refs[:n], refs[n:2 * n]
        send_sems, recv_sems, self_sems = refs[2 * n + 1:2 * n + 4]
        remote, local = _exchange_copies(src_refs, land_refs, send_sems, recv_sems, self_sems, scatter)
        for cp in remote + local:
            cp.start()
        refs[-1][...] = jnp.zeros_like(refs[-1])

    hbm = lambda a: pltpu.HBM(a.shape, a.dtype)
    out = pl.pallas_call(
        body, name=name,
        out_shape=(pltpu.SemaphoreType.DMA((7 * n,)), pltpu.SemaphoreType.DMA((7 * n,)), pltpu.SemaphoreType.DMA((n,)),
                   *[hbm(s) for s in srcs], *[hbm(a) for a in lands], jax.ShapeDtypeStruct((8, HEAD), F32)),
        in_specs=(HBM_SPEC,) * (2 * n) + (ANY_SPEC,),
        out_specs=(SEM_SPEC,) * 3 + (HBM_SPEC,) * (2 * n) + (pl.BlockSpec(memory_space=pltpu.VMEM),),
        input_output_aliases={i: 3 + i for i in range(2 * n)},
        compiler_params=pltpu.CompilerParams(has_side_effects=EFFECT),
    )(*[pltpu.with_memory_space_constraint(s, pltpu.HBM) for s in srcs], *lands, after)
    return dict(sems=out[:3], srcs=out[3:3 + n], lands=out[3 + n:3 + 2 * n], token=out[-1])


def _send_wait(started, scatter, after, name):
    srcs, lands = started["srcs"], started["lands"]
    n = len(srcs)

    def body(*refs):
        src_refs, land_refs = refs[:n], refs[n:2 * n]
        send_sems, recv_sems, self_sems = refs[2 * n:2 * n + 3]
        remote, local = _exchange_copies(src_refs, land_refs, send_sems, recv_sems, self_sems, scatter)
        for cp in remote:
            cp.wait_send()
            cp.wait_recv()
        for cp in local:
            cp.wait()

    hbm = lambda a: pltpu.HBM(a.shape, a.dtype)
    out = pl.pallas_call(
        body, name=name, out_shape=(*[hbm(s) for s in srcs], *[hbm(a) for a in lands]),
        in_specs=(HBM_SPEC,) * (2 * n) + (SEM_SPEC,) * 3 + (ANY_SPEC,), out_specs=(HBM_SPEC,) * (2 * n),
        input_output_aliases={i: i for i in range(2 * n)},
        compiler_params=pltpu.CompilerParams(has_side_effects=EFFECT),
    )(*srcs, *lands, *started["sems"], after)
    return out[n:]


def _sum_slots(xs, name, rows_out=None):
    _, r, c = xs[0].shape
    ro = rows_out or r
    tc = _pick(c, (128,))

    def body(*refs):
        o_ref = refs[-1]
        for l, x_ref in enumerate(refs[:-1]):
            acc = x_ref[0].astype(F32)
            for i in range(1, NDEV):
                acc = acc + x_ref[i].astype(F32)
            o_ref[l] = acc[:ro]

    return pl.pallas_call(
        body, name=name, grid=(c // tc,), in_specs=[pl.BlockSpec((NDEV, r, tc), lambda j: (0, 0, j))] * len(xs),
        out_specs=pl.BlockSpec((len(xs), ro, tc), lambda j: (0, 0, j)),
        out_shape=jax.ShapeDtypeStruct((len(xs), ro, c), F32), compiler_params=_params(("parallel",)))(*xs)


def _adamw(w, g, m, v, name):
    if w.ndim == 3:
        nl, r, c = w.shape
        tc = _pick(c, (256, 128))
        grid = (nl, c // tc)
        blk = pl.BlockSpec((None, r, tc), lambda i, j: (i, 0, j))
        sem = ("parallel", "parallel")
    else:
        r, c = w.shape
        tr = _pick(r, (256, 128, 64, 32, 16, 8))
        grid = (r // tr,)
        blk = pl.BlockSpec((tr, c), lambda i: (i, 0))
        sem = ("parallel",)
    c1 = 1.0 - B1 ** STEP
    c2 = 1.0 - B2 ** STEP

    def body(w_ref, g_ref, m_ref, v_ref, d_ref, nm_ref, nv_ref):
        gv = g_ref[...]
        nm = B1 * m_ref[...] + (1.0 - B1) * gv
        nv = B2 * v_ref[...] + (1.0 - B2) * (gv * gv)
        d_ref[...] = -LR * ((nm / c1) / (jnp.sqrt(nv / c2) + ADAM_EPS) + WD * w_ref[...])
        nm_ref[...] = nm
        nv_ref[...] = nv

    sds = jax.ShapeDtypeStruct(w.shape, F32)
    return pl.pallas_call(
        body, name=name, grid=grid, in_specs=[blk] * 4, out_specs=[blk] * 3, out_shape=[sds] * 3,
        compiler_params=_params(sem))(w, g, m, v)


def _pad_rows(a, mult):
    r = a.shape[0]
    pad = (-r) % mult
    return a if pad == 0 else jnp.pad(a, ((0, pad), (0, 0)))


def _pad_lanes(v, width=HEAD):
    return jnp.pad(v.reshape(1, -1), ((0, 0), (0, width - v.shape[-1])))


def kernel(x, p, ln_mix, ln_ffn, ln_ple, gdn_w_in, gdn_conv, gdn_a_log, gdn_dt_bias, gdn_norm, gdn_w_out, kv_norm, w_kv, k_norm, sb_w_q, sb_q_norm, sb_w_out, ffn_w_in, ffn_w_out, ple_w_proj, ple_w_gate, loss_target, m_ln_mix, m_ln_ffn, m_ln_ple, m_gdn_w_in, m_gdn_conv, m_gdn_a_log, m_gdn_dt_bias, m_gdn_norm, m_gdn_w_out, m_kv_norm, m_w_kv, m_k_norm, m_sb_w_q, m_sb_q_norm, m_sb_w_out, m_ffn_w_in, m_ffn_w_out, m_ple_w_proj, m_ple_w_gate, v_ln_mix, v_ln_ffn, v_ln_ple, v_gdn_w_in, v_gdn_conv, v_gdn_a_log, v_gdn_dt_bias, v_gdn_norm, v_gdn_w_out, v_kv_norm, v_w_kv, v_k_norm, v_sb_w_q, v_sb_q_norm, v_sb_w_out, v_ffn_w_in, v_ffn_w_out, v_ple_w_proj, v_ple_w_gate):
    s, d = x.shape[1], x.shape[2]
    nh = d // HEAD
    depth = ln_mix.shape[0]
    n_a = gdn_w_in.shape[0]
    n_b = sb_w_q.shape[0]
    me = _my_index()
    win_cols = gdn_w_in.shape[2]
    win_rows = 4 * d + 2 * nh

    def col_t(w):
        return jnp.transpose(w).astype(BF16)

    local = {}
    for l in range(n_a):
        local[("gdn_w_in", l)] = col_t(gdn_w_in[l])
        local[("gdn_w_out", l)] = gdn_w_out[l].astype(BF16)
    local[("w_kv", 0)] = col_t(w_kv)
    for j in range(n_b):
        local[("sb_w_q", j)] = sb_w_q[j].astype(BF16)
        local[("sb_w_out", j)] = sb_w_out[j].astype(BF16)
    for l in range(depth):
        local[("ffn_w_in", l)] = col_t(ffn_w_in[l])
        local[("ffn_w_out", l)] = ffn_w_out[l].astype(BF16)
        local[("ple_w_proj", l)] = col_t(ple_w_proj[l]).reshape(-1, d)
        local[("ple_w_gate", l)] = ple_w_gate[l].astype(BF16)
    local = {key: _pad_rows(a, 16) for key, a in local.items()}

    chunks = []
    for l in range(depth):
        mix = [("gdn_w_in", l), ("gdn_w_out", l)] if l < n_a else [("sb_w_q", l - n_a), ("sb_w_out", l - n_a)]
        rest = [("ffn_w_in", l), ("ffn_w_out", l), ("ple_w_proj", l), ("ple_w_gate", l)]
        if l == n_a - 1:
            rest.append(("w_kv", 0))
        chunks += [(f"a{l}", mix), (f"f{l}", rest)]
    chunk_keys = dict(chunks)

    conv_rows = n_a * gdn_conv.shape[1]
    conv_sh = _pad_rows(gdn_conv.reshape(conv_rows, -1), 8)
    conv_g = _all_gather(conv_sh, "comm_gather_conv")
    token = conv_g
    conv_g = conv_g.reshape(NDEV, conv_sh.shape[0], -1)
    conv_full = jnp.transpose(conv_g[:, :conv_rows, :], (1, 0, 2)).reshape(n_a, gdn_conv.shape[1], 3 * d)

    w_started = {}
    for name, keys in chunks:
        w_started[name] = _send_start([local[k] for k in keys], False, token, f"comm_wstart_{name}")
        token = w_started[name]["token"]

    full = {}

    def fetch(name, after):
        lands = _send_wait(w_started[name], False, after, f"comm_wwait_{name}")
        for key, land in zip(chunk_keys[name], lands):
            full[key] = land

    def whole(key, valid=None):
        a = full[key]
        if valid is not None:
            a = a.reshape(NDEV, -1, d)[:, :valid, :].reshape(-1, d)
        return a

    pd = p.shape[-1]
    w_in_t, w_ab_t, w_gout, w_q, w_sout, wf_t, w_fout, wp_t, w_pg = {}, {}, {}, {}, {}, {}, {}, {}, {}
    wkv_t = None

    h = x[0]
    sv = []
    kv_sv = None
    k_sh = v_sh = None
    for l in range(depth):
        t = {}
        t["h0"] = h
        if l == 0:
            hn = _rms_fwd(h, ln_mix[l], f"rms_mix_{l}")
        t["hn"] = hn
        fetch(f"a{l}", token if l == 0 else hn)
        if l < n_a:
            wt = whole(("gdn_w_in", l), win_cols)
            w_in_t[l] = wt[:4 * d]
            w_ab_t[l] = jnp.pad(wt[4 * d:], ((0, HEAD - 2 * nh), (0, 0)))
            w_gout[l] = whole(("gdn_w_out", l))
        else:
            w_q[l - n_a] = whole(("sb_w_q", l - n_a))
            w_sout[l - n_a] = whole(("sb_w_out", l - n_a))
        if l < n_a:
            proj = _mm(hn, w_in_t[l], "nt", f"gdn_proj_{l}")
            pab = _mm(hn, w_ab_t[l], "nt", f"gdn_proj_ab_{l}")
            qkv = _conv_fwd(proj, conv_full[l], d, f"gdn_conv_{l}")
            al, dtb = _pad_lanes(gdn_a_log[l]), _pad_lanes(gdn_dt_bias[l])
            gb = _gates_fwd(pab, al, dtb, nh, f"gdn_gates_{l}")
            o_raw, states = _gdn_fwd(qkv, gb, nh, f"gdn_rule_{l}")
            o2 = _headnorm_fwd(o_raw, gdn_norm[l], f"gdn_outnorm_{l}", gate=proj, gate_col0=3 * d, head_major=True)
            h, hn2 = _mm(o2, w_gout[l], "nn", f"gdn_out_{l}", res=h, norm_g=ln_ffn[l])
            t.update(proj=proj, pab=pab, qkv=qkv, gb=gb, o_raw=o_raw, states=states, o2=o2, al=al, dtb=dtb)
        else:
            j = l - n_a
            qpre = _mm(hn, w_q[j], "nn", f"sb_qproj_{j}")
            qn = _headnorm_fwd(qpre, sb_q_norm[j], f"sb_qnorm_{j}", scale=HEAD ** -0.5)
            o, ctab = _sb_fwd(qn, k_sh, v_sh, f"sb_attn_{j}")
            h, hn2 = _mm(o, w_sout[j], "nn", f"sb_out_{j}", res=h, norm_g=ln_ffn[l])
            t.update(qpre=qpre, qn=qn, o=o, ctab=ctab)
        t["h1"] = h
        fetch(f"f{l}", hn2)
        wf_t[l] = whole(("ffn_w_in", l))
        w_fout[l] = whole(("ffn_w_out", l))
        wp_t[l] = full[("ple_w_proj", l)].reshape(d, pd)
        w_pg[l] = whole(("ple_w_gate", l))
        if l == n_a - 1:
            wkv_t = whole(("w_kv", 0))
        act, gs, us = _swiglu_fwd(hn2, wf_t[l], f"ffn_in_{l}")
        h, hn3 = _mm(act, w_fout[l], "nn", f"ffn_out_{l}", res=h, norm_g=ln_ple[l])
        t.update(hn2=hn2, act=act, gs=gs, us=us, h2=h)
        gains = ([ln_mix[l + 1]] if l + 1 < depth else []) + ([kv_norm] if l == n_a - 1 else [])
        h, gpre, pp, *normed = _ple_fwd(h, hn3, p[l, 0], w_pg[l], wp_t[l], f"ple_{l}", norm_gs=gains)
        if l + 1 < depth:
            hn = normed[0]
        t.update(hn3=hn3, gpre=gpre, pp=pp)
        sv.append(t)
        if l == n_a - 1:
            kvn = normed[-1]
            kv = _mm(kvn, wkv_t, "nt", "kv_proj")
            k_sh = _headnorm_fwd(kv, k_norm, "k_norm", width=d)
            v_sh = kv[:, d:].astype(BF16)
            kv_sv = dict(h=h, kvn=kvn, kv=kv)

    dh, loss_vec = _loss_fwd_bwd(h, loss_target[0], "loss")
    loss = lax.psum(jnp.sum(loss_vec), ("x", "y", "c"))

    gw = {}
    small = {}
    g_started = {}

    def scatter_start(name):
        gparts = []
        for key in chunk_keys[name]:
            g = gw[key]
            g = g.reshape(NDEV, -1, d) if key[0] == "ple_w_proj" else g.reshape(NDEV, -1, g.shape[-1])
            padr = local[key].shape[0] - g.shape[1]
            if padr:
                g = jnp.pad(g, ((0, 0), (0, padr), (0, 0)))
            gparts.append(g.reshape(-1, d))
        g_started[name] = _send_start(gparts, True, gparts[0], f"comm_gstart_{name}")
        return g_started[name]["token"]

    dkv_sh = None
    for l in reversed(range(depth)):
        t = sv[l]
        if l == n_a - 1:
            dkv_k, dkn = _headnorm_bwd(dkv_sh[0], kv_sv["kv"], k_norm, "k_norm_bwd", dx_dtype=BF16)
            dkv = jnp.concatenate([dkv_k, dkv_sh[1].astype(BF16)], axis=1)
            gw[("w_kv", 0)] = _mm(dkv, kv_sv["kvn"], "tn", "kv_dw", out_dtype=BF16)
            dh, _, dg = _mm(dkv, wkv_t, "nn", "kv_dx", norm_bwd=(kv_sv["h"], kv_norm, dh))
            small["kv_norm"] = dg
            small["k_norm"] = dkn
        dgp, dpp = _ple_bwd(dh, t["gpre"], t["pp"], f"ple_bwd_{l}")
        gw[("ple_w_gate", l)] = _mm(t["hn3"], dgp, "tn", f"ple_dwg_{l}", out_dtype=BF16)
        gw[("ple_w_proj", l)] = _mm(dpp, p[l, 0], "tn", f"ple_dwp_{l}", out_dtype=BF16)
        dh, dhb, dg = _mm(dgp, w_pg[l], "nt", f"ple_dx_{l}", norm_bwd=(t["h2"], ln_ple[l], dh))
        small[("ln_ple", l)] = dg
        dgu = _swiglu_bwd(dhb, w_fout[l], t["gs"], t["us"], f"ffn_bwd_act_{l}")
        gw[("ffn_w_out", l)] = _mm(t["act"], dhb, "tn", f"ffn_dwo_{l}", out_dtype=BF16)
        gw[("ffn_w_in", l)] = _mm(dgu, t["hn2"], "tn", f"ffn_dwi_{l}", out_dtype=BF16)
        dh, dhb, dg = _mm(dgu, wf_t[l], "nn", f"ffn_dx_{l}", norm_bwd=(t["h1"], ln_ffn[l], dh),
                          after=scatter_start(f"f{l}"))
        small[("ln_ffn", l)] = dg
        if l < n_a:
            do2 = _mm(dhb, w_gout[l], "nt", f"gdn_out_dx_{l}")
            gw[("gdn_w_out", l)] = _mm(t["o2"], dhb, "tn", f"gdn_out_dw_{l}", out_dtype=BF16)
            do_raw, dgn, dgate = _headnorm_bwd(do2, t["o_raw"], gdn_norm[l], f"gdn_outnorm_bwd_{l}",
                                               gate=t["proj"], gate_col0=3 * d, head_major=True)
            small[("gdn_norm", l)] = dgn
            dqkv, dgb = _gdn_bwd(t["qkv"], t["gb"], do_raw, t["states"], nh, f"gdn_rule_bwd_{l}")
            dpab, dal, ddt = _gates_bwd(dgb, t["pab"], t["al"], t["dtb"], nh, f"gdn_gates_bwd_{l}")
            small[("gdn_a_log", l)] = dal
            small[("gdn_dt_bias", l)] = ddt
            dproj_qkv, dconv = _conv_bwd(dqkv, t["proj"], conv_full[l], d, f"gdn_conv_bwd_{l}")
            small[("gdn_conv", l)] = dconv
            dproj = jnp.concatenate([dproj_qkv, dgate], axis=1)
            dw_main = _mm(dproj, t["hn"], "tn", f"gdn_proj_dw_{l}", out_dtype=BF16)
            dw_ab = _mm(dpab, t["hn"], "tn", f"gdn_proj_ab_dw_{l}", out_dtype=BF16)
            gw[("gdn_w_in", l)] = jnp.concatenate([dw_main, dw_ab[:16]], axis=0)[:win_rows]
            dhn_ab = _mm(dpab, w_ab_t[l], "nn", f"gdn_proj_ab_dx_{l}")
            last = dict(a=dproj, b=w_in_t[l], mode="nn", name=f"gdn_proj_dx_{l}", res=dhn_ab)
        else:
            j = l - n_a
            do = _mm(dhb, w_sout[j], "nt", f"sb_out_dx_{j}", out_dtype=BF16)
            gw[("sb_w_out", j)] = _mm(t["o"], dhb, "tn", f"sb_out_dw_{j}", out_dtype=BF16)
            dq, dk, dv = _sb_bwd(t["qn"], k_sh, v_sh, do, t["ctab"], f"sb_attn_bwd_{j}")
            dkv_sh = (dk, dv) if dkv_sh is None else (dkv_sh[0] + dk, dkv_sh[1] + dv)
            dqpre, dqn = _headnorm_bwd(dq, t["qpre"], sb_q_norm[j], f"sb_qnorm_bwd_{j}", scale=HEAD ** -0.5, dx_dtype=BF16)
            small[("sb_q_norm", j)] = dqn
            gw[("sb_w_q", j)] = _mm(t["hn"], dqpre, "tn", f"sb_q_dw_{j}", out_dtype=BF16)
            last = dict(a=dqpre, b=w_q[j], mode="nt", name=f"sb_q_dx_{j}")
        dh, _, dg = _mm(**last, norm_bwd=(t["h0"], ln_mix[l], dh), after=scatter_start(f"a{l}"))
        small[("ln_mix", l)] = dg
    grad_x = dh[None]

    landed = {}
    for name, keys in reversed(chunks):
        lands = _send_wait(g_started[name], True, dh, f"comm_gwait_{name}")
        for key, land in zip(keys, lands):
            landed[key] = land.reshape(NDEV, -1, d)

    def summed(wname, count, rows_out=None):
        return _sum_slots([landed[(wname, i)] for i in range(count)], f"grad_sum_{wname}", rows_out)

    gt_gdn_w_in = summed("gdn_w_in", n_a, win_cols)
    gt_ffn_w_in = summed("ffn_w_in", depth)
    g_gdn_w_in = jnp.transpose(gt_gdn_w_in, (0, 2, 1))
    g_gdn_w_out = summed("gdn_w_out", n_a)
    g_w_kv = jnp.transpose(summed("w_kv", 1)[0])
    g_sb_w_q = summed("sb_w_q", n_b)
    g_sb_w_out = summed("sb_w_out", n_b)
    g_ffn_w_in = jnp.transpose(gt_ffn_w_in, (0, 2, 1))
    g_ffn_w_out = summed("ffn_w_out", depth)
    g_ple_w_proj = jnp.transpose(summed("ple_w_proj", depth).reshape(depth, -1, pd), (0, 2, 1))
    g_ple_w_gate = summed("ple_w_gate", depth)

    def vec_rows(v):
        return v.reshape(-1, HEAD)

    small_items = []
    for name_, cnt in (("ln_mix", depth), ("ln_ffn", depth), ("ln_ple", depth)):
        for l in range(cnt):
            small_items.append(((name_, l), vec_rows(small[(name_, l)])))
    for l in range(n_a):
        small_items.append((("gdn_conv", l), small[("gdn_conv", l)].reshape(-1, HEAD)))
        small_items.append((("gdn_a_log", l), small[("gdn_a_log", l)]))
        small_items.append((("gdn_dt_bias", l), small[("gdn_dt_bias", l)]))
        small_items.append((("gdn_norm", l), small[("gdn_norm", l)]))
    small_items.append(("kv_norm", vec_rows(small["kv_norm"])))
    small_items.append(("k_norm", small["k_norm"]))
    for j in range(n_b):
        small_items.append((("sb_q_norm", j), small[("sb_q_norm", j)]))
    spack = jnp.concatenate([_pad_rows(a, 8) for _, a in small_items], axis=0)
    sg = _all_gather(spack, "comm_gather_small").reshape(NDEV, spack.shape[0], HEAD)
    ssum = _sum_slots([sg], "small_sum")[0]
    sm = {}
    off = 0
    for key, a in small_items:
        sm[key] = ssum[off:off + a.shape[0]]
        off += a.shape[0] + (-a.shape[0]) % 8

    g_ln_mix = jnp.stack([sm[("ln_mix", l)].reshape(d) for l in range(depth)])
    g_ln_ffn = jnp.stack([sm[("ln_ffn", l)].reshape(d) for l in range(depth)])
    g_ln_ple = jnp.stack([sm[("ln_ple", l)].reshape(d) for l in range(depth)])
    conv_loc = gdn_conv.shape[2]
    g_conv_full = jnp.stack([sm[("gdn_conv", l)].reshape(gdn_conv.shape[1], 3 * d) for l in range(n_a)])
    g_gdn_conv = lax.dynamic_slice_in_dim(g_conv_full, me * conv_loc, conv_loc, axis=2)
    g_a_log = jnp.stack([sm[("gdn_a_log", l)][0, :nh] for l in range(n_a)])
    g_dt_bias = jnp.stack([sm[("gdn_dt_bias", l)][0, :nh] for l in range(n_a)])
    g_gdn_norm = jnp.stack([sm[("gdn_norm", l)][0] for l in range(n_a)])
    g_kv_norm = sm["kv_norm"].reshape(d)
    g_k_norm = sm["k_norm"][0]
    g_sb_q_norm = jnp.stack([sm[("sb_q_norm", j)][0] for j in range(n_b)])

    grads = [g_ln_mix, g_ln_ffn, g_ln_ple, g_gdn_w_in, g_gdn_conv, g_a_log, g_dt_bias, g_gdn_norm, g_gdn_w_out,
             g_kv_norm, g_w_kv, g_k_norm, g_sb_w_q, g_sb_q_norm, g_sb_w_out, g_ffn_w_in, g_ffn_w_out, g_ple_w_proj,
             g_ple_w_gate]
    weights = [ln_mix, ln_ffn, ln_ple, gdn_w_in, gdn_conv, gdn_a_log, gdn_dt_bias, gdn_norm, gdn_w_out, kv_norm, w_kv,
               k_norm, sb_w_q, sb_q_norm, sb_w_out, ffn_w_in, ffn_w_out, ple_w_proj, ple_w_gate]
    moms = [m_ln_mix, m_ln_ffn, m_ln_ple, m_gdn_w_in, m_gdn_conv, m_gdn_a_log, m_gdn_dt_bias, m_gdn_norm, m_gdn_w_out,
            m_kv_norm, m_w_kv, m_k_norm, m_sb_w_q, m_sb_q_norm, m_sb_w_out, m_ffn_w_in, m_ffn_w_out, m_ple_w_proj,
            m_ple_w_gate]
    vels = [v_ln_mix, v_ln_ffn, v_ln_ple, v_gdn_w_in, v_gdn_conv, v_gdn_a_log, v_gdn_dt_bias, v_gdn_norm, v_gdn_w_out,
            v_kv_norm, v_w_kv, v_k_norm, v_sb_w_q, v_sb_q_norm, v_sb_w_out, v_ffn_w_in, v_ffn_w_out, v_ple_w_proj,
            v_ple_w_gate]

    deltas, new_m, new_v = [], [], []
    small_idx = [i for i, w in enumerate(weights) if w.size < 8 * HEAD * 16]
    transposed = {3: gt_gdn_w_in, 15: gt_ffn_w_in}
    for i, (w, g, m, v) in enumerate(zip(weights, grads, moms, vels)):
        if i in small_idx:
            deltas.append(None), new_m.append(None), new_v.append(None)
            continue
        if i in transposed:
            tr = lambda a: jnp.transpose(a, (0, 2, 1))
            dl, nm, nv = _adamw(tr(w), transposed[i], tr(m), tr(v), f"adamw_{i}")
            deltas.append(tr(dl)), new_m.append(tr(nm)), new_v.append(tr(nv))
            continue
        shp = w.shape
        two = lambda a: a.reshape(-1, shp[-1])
        dl, nm, nv = _adamw(two(w), two(g), two(m), two(v), f"adamw_{i}")
        deltas.append(dl.reshape(shp)), new_m.append(nm.reshape(shp)), new_v.append(nv.reshape(shp))

    def flat_pack(arrs):
        flat = jnp.concatenate([a.reshape(-1) for a in arrs])
        pad = (-flat.shape[0]) % (8 * HEAD)
        return jnp.pad(flat, (0, pad)).reshape(-1, HEAD)

    sw = flat_pack([weights[i] for i in small_idx])
    sgr = flat_pack([grads[i] for i in small_idx])
    smo = flat_pack([moms[i] for i in small_idx])
    sve = flat_pack([vels[i] for i in small_idx])
    sdl, snm, snv = _adamw(sw, sgr, smo, sve, "adamw_small")
    off = 0
    for i in small_idx:
        n = weights[i].size
        shp = weights[i].shape
        deltas[i] = sdl.reshape(-1)[off:off + n].reshape(shp)
        new_m[i] = snm.reshape(-1)[off:off + n].reshape(shp)
        new_v[i] = snv.reshape(-1)[off:off + n].reshape(shp)
        off += n

    return (loss, grad_x, *grads, *deltas, *new_m, *new_v)
```

```python
import math

import jax
import jax.numpy as jnp
from jax import lax
from jax.experimental import pallas as pl
from jax.experimental.pallas import tpu as pltpu

F32 = jnp.float32
BF16 = jnp.bfloat16
NDEV = 8
HEAD = 128
CHUNK = 64
SBLK = 256
EPS = 1e-6
LR, B1, B2, ADAM_EPS, WD, STEP = 0.001, 0.9, 0.999, 1e-08, 0.01, 10
NEG = -1e30
MM_VMEM_BUDGET = 40 * 1024 * 1024

NN = (((1,), (0,)), ((), ()))
NT = (((1,), (1,)), ((), ()))
TN = (((0,), (0,)), ((), ()))
BNN = (((2,), (1,)), ((0,), (0,)))
BNT = (((2,), (2,)), ((0,), (0,)))
BTN = (((1,), (1,)), ((0,), (0,)))
MESH = pl.DeviceIdType.MESH


def _dot(a, b, dims=NN):
    return lax.dot_general(a.astype(BF16), b.astype(BF16), dims, preferred_element_type=F32)


def _dot_hilo(a, b01_twice, one_dot):
    hi = a.astype(BF16)
    lo = (a - hi.astype(F32)).astype(BF16)
    if one_dot:
        return lax.dot_general(jnp.concatenate([hi, lo], axis=1), b01_twice, NN, preferred_element_type=F32)
    b01 = b01_twice[:a.shape[1]]
    return (lax.dot_general(hi, b01, NN, preferred_element_type=F32)
            + lax.dot_general(lo, b01, NN, preferred_element_type=F32))


def _pick(dim, cands):
    for c in cands:
        if dim % c == 0:
            return c
    return dim


def _params(sem, vmem_mb=48):
    return pltpu.CompilerParams(dimension_semantics=sem, vmem_limit_bytes=vmem_mb * 1024 * 1024)


def _silu(x):
    return x * jax.nn.sigmoid(x)


def _silu_and_grad(x):
    s = jax.nn.sigmoid(x)
    xs = x * s
    return xs, s + xs * (1.0 - s)


def _mm(a, b, mode, name, out_dtype=F32, res=None, norm_g=None, norm_bwd=None, after=None):
    if mode == "nn":
        (m, k), n = a.shape, b.shape[1]
    elif mode == "nt":
        (m, k), n = a.shape, b.shape[0]
    else:
        (k, m), n = a.shape, b.shape[1]
    rows = norm_g is not None or norm_bwd is not None
    tn = n if rows else _pick(n, (512, 256, 128))
    tk = k if k <= 4096 else max(t for t in range(128, 4097, 128) if k % t == 0)
    nk = k // tk
    out_b = jnp.dtype(out_dtype).itemsize + (res.dtype.itemsize if res is not None else 0)
    out_b += 2 if norm_g is not None else 0
    out_b += 10 if norm_bwd is not None else 0
    for tm in [t for t in range(min(m, 2048), 127, -128) if m % t == 0] + [m]:
        need = 2 * (tm * tk * a.dtype.itemsize + tk * tn * b.dtype.itemsize + tm * tn * out_b) + 4 * tm * tn
        if need <= MM_VMEM_BUDGET:
            break
    dims = {"nn": NN, "nt": NT, "tn": TN}[mode]
    if mode == "tn":
        a_spec = pl.BlockSpec((tk, tm), lambda i, j, kk: (kk, i))
    else:
        a_spec = pl.BlockSpec((tm, tk), lambda i, j, kk: (i, kk))
    if mode == "nt":
        b_spec = pl.BlockSpec((tn, tk), lambda i, j, kk: (j, kk))
    else:
        b_spec = pl.BlockSpec((tk, tn), lambda i, j, kk: (kk, j))
    mn_spec = pl.BlockSpec((tm, tn), lambda i, j, kk: (i, j))
    vec_spec = pl.BlockSpec((1, tn), lambda i, j, kk: (0, j))
    has_res = res is not None
    n_in = 2 + has_res + (1 if norm_g is not None else 0) + (3 if norm_bwd is not None else 0) + (after is not None)

    def body(*refs):
        a_ref, b_ref = refs[:2]
        extra = list(refs[2:n_in])
        outs = refs[n_in:-1]
        acc = refs[-1]
        kk = pl.program_id(2)

        @pl.when(kk == 0)
        def _():
            acc[...] = jnp.zeros_like(acc)

        if norm_bwd is not None:
            @pl.when((kk == 0) & (pl.program_id(0) == 0))
            def _():
                outs[2][...] = jnp.zeros_like(outs[2])

        acc[...] += _dot(a_ref[...], b_ref[...], dims)

        @pl.when(kk == nk - 1)
        def _():
            r = acc[...]
            if has_res:
                r = r + extra.pop(0)[...].astype(F32)
            if norm_g is not None:
                outs[0][...] = r.astype(out_dtype)
                rs = lax.rsqrt(jnp.mean(r * r, axis=-1, keepdims=True) + EPS)
                outs[1][...] = (r * rs * extra.pop(0)[...]).astype(BF16)
            elif norm_bwd is not None:
                xv, gv, dres = extra.pop(0)[...], extra.pop(0)[...], extra.pop(0)[...]
                rs = lax.rsqrt(jnp.mean(xv * xv, axis=-1, keepdims=True) + EPS)
                gdy = r * gv
                dx = dres + rs * gdy - xv * (rs * rs * rs) * jnp.mean(xv * gdy, axis=-1, keepdims=True)
                outs[0][...] = dx
                outs[1][...] = dx.astype(BF16)
                outs[2][...] += jnp.sum(r * xv * rs, axis=0, keepdims=True)
            else:
                outs[0][...] = r.astype(out_dtype)

    ins = [a, b] + ([res] if has_res else [])
    in_specs = [a_spec, b_spec] + ([mn_spec] if has_res else [])
    out_specs, out_shape = [mn_spec], [jax.ShapeDtypeStruct((m, n), out_dtype)]
    sem = ("parallel", "parallel", "arbitrary")
    if norm_g is not None:
        ins.append(norm_g.reshape(1, n))
        in_specs.append(vec_spec)
        out_specs.append(mn_spec)
        out_shape.append(jax.ShapeDtypeStruct((m, n), BF16))
    if norm_bwd is not None:
        x, g, dres = norm_bwd
        ins += [x, g.reshape(1, n), dres]
        in_specs += [mn_spec, vec_spec, mn_spec]
        out_specs += [mn_spec, vec_spec]
        out_shape += [jax.ShapeDtypeStruct((m, n), BF16), jax.ShapeDtypeStruct((1, n), F32)]
        sem = ("arbitrary", "arbitrary", "arbitrary")
    if after is not None:
        ins.append(after)
        in_specs.append(pl.BlockSpec(memory_space=pl.ANY))
    out = pl.pallas_call(
        body, name=name, grid=(m // tm, n // tn, nk), in_specs=in_specs, out_specs=out_specs,
        out_shape=out_shape, scratch_shapes=[pltpu.VMEM((tm, tn), F32)],
        compiler_params=_params(sem))(*ins)
    return out[0] if len(out) == 1 else out


def _rms_fwd(h, g, name):
    s, d = h.shape
    tm = _pick(s, (512, 256, 128))

    def body(h_ref, g_ref, o_ref):
        x = h_ref[...]
        r = lax.rsqrt(jnp.mean(x * x, axis=-1, keepdims=True) + EPS)
        o_ref[...] = (x * r * g_ref[...]).astype(BF16)

    return pl.pallas_call(
        body, name=name, grid=(s // tm,),
        in_specs=[pl.BlockSpec((tm, d), lambda i: (i, 0)), pl.BlockSpec((1, d), lambda i: (0, 0))],
        out_specs=pl.BlockSpec((tm, d), lambda i: (i, 0)),
        out_shape=jax.ShapeDtypeStruct((s, d), BF16), compiler_params=_params(("parallel",)))(h, g.reshape(1, d))


def _headnorm_fwd(x, g, name, scale=1.0, gate=None, gate_col0=0, out_dtype=BF16, width=None, head_major=False):
    if head_major:
        s, d = x.shape[1], x.shape[0] * HEAD
    else:
        s, d = x.shape[0], (width or x.shape[1])
    nh = d // HEAD
    tm = _pick(s, (256, 128))
    has_gate = gate is not None
    gb = gate_col0 // d

    def body(*refs):
        if has_gate:
            x_ref, g_ref, gt_ref, o_ref = refs
        else:
            x_ref, g_ref, o_ref = refs
        gv = g_ref[...]
        for h in range(nh):
            sl = slice(h * HEAD, (h + 1) * HEAD)
            xv = (x_ref[h] if head_major else x_ref[:, sl]).astype(F32)
            r = lax.rsqrt(jnp.mean(xv * xv, axis=-1, keepdims=True) + EPS)
            y = xv * r * gv
            if scale != 1.0:
                y = y * scale
            if has_gate:
                y = y * _silu(gt_ref[:, sl])
            o_ref[:, sl] = y.astype(out_dtype)

    row = pl.BlockSpec((tm, d), lambda i: (i, 0))
    hm = pl.BlockSpec((nh, tm, HEAD), lambda i: (0, i, 0))
    ins = [x, g.reshape(1, HEAD)]
    in_specs = [hm if head_major else row, pl.BlockSpec((1, HEAD), lambda i: (0, 0))]
    if has_gate:
        ins.append(gate)
        in_specs.append(pl.BlockSpec((tm, d), lambda i: (i, gb)))
    return pl.pallas_call(
        body, name=name, grid=(s // tm,), in_specs=in_specs, out_specs=row,
        out_shape=jax.ShapeDtypeStruct((s, d), out_dtype), compiler_params=_params(("parallel",)))(*ins)


def _headnorm_bwd(dy, x, g, name, scale=1.0, gate=None, gate_col0=0, dx_dtype=F32, head_major=False):
    s, d = dy.shape
    nh = d // HEAD
    tm = _pick(s, (256, 128))
    has_gate = gate is not None
    gb = gate_col0 // d

    def body(*refs):
        if has_gate:
            dy_ref, x_ref, g_ref, gt_ref, dx_ref, dg_ref, dgt_ref = refs
        else:
            dy_ref, x_ref, g_ref, dx_ref, dg_ref = refs

        @pl.when(pl.program_id(0) == 0)
        def _():
            dg_ref[...] = jnp.zeros_like(dg_ref)

        gv = g_ref[...]
        dg_acc = jnp.zeros((1, HEAD), F32)
        for h in range(nh):
            sl = slice(h * HEAD, (h + 1) * HEAD)
            xv = (x_ref[h] if head_major else x_ref[:, sl]).astype(F32)
            dyv = dy_ref[:, sl].astype(F32)
            r = lax.rsqrt(jnp.mean(xv * xv, axis=-1, keepdims=True) + EPS)
            if has_gate:
                gt = gt_ref[:, sl]
                act, dact = _silu_and_grad(gt)
                dgt_ref[:, sl] = (dyv * (xv * r * gv) * dact).astype(dgt_ref.dtype)
                dn = dyv * act
            else:
                dn = dyv
            if scale != 1.0:
                dn = dn * scale
            gdn = dn * gv
            mean_t = jnp.mean(xv * gdn, axis=-1, keepdims=True)
            dxv = (r * gdn - xv * (r * r * r) * mean_t).astype(dx_dtype)
            if head_major:
                dx_ref[h] = dxv
            else:
                dx_ref[:, sl] = dxv
            dg_acc = dg_acc + jnp.sum(dn * xv * r, axis=0, keepdims=True)
        dg_ref[...] += dg_acc

    row = pl.BlockSpec((tm, d), lambda i: (i, 0))
    hm = pl.BlockSpec((nh, tm, HEAD), lambda i: (0, i, 0))
    vec = pl.BlockSpec((1, HEAD), lambda i: (0, 0))
    ins = [dy, x, g.reshape(1, HEAD)]
    in_specs = [row, hm if head_major else row, vec]
    out_specs = [hm if head_major else row, vec]
    dx_shape = (nh, s, HEAD) if head_major else (s, d)
    out_shape = [jax.ShapeDtypeStruct(dx_shape, dx_dtype), jax.ShapeDtypeStruct((1, HEAD), F32)]
    if has_gate:
        ins.append(gate)
        in_specs.append(pl.BlockSpec((tm, d), lambda i: (i, gb)))
        out_specs.append(row)
        out_shape.append(jax.ShapeDtypeStruct((s, d), BF16))
    return pl.pallas_call(
        body, name=name, grid=(s // tm,), in_specs=in_specs, out_specs=out_specs, out_shape=out_shape,
        compiler_params=_params(("arbitrary",)))(*ins)


def _swiglu_fwd(hn, wf_t, name):
    s, d = hn.shape
    f = wf_t.shape[0] // 2
    tm = _pick(s, (1024, 512, 256, 128))
    tn = _pick(f, (512, 256, 128))
    nj = f // tn

    def body(a_ref, wg_ref, wu_ref, act_ref, g_ref, u_ref):
        a = a_ref[...]
        g = _dot(a, wg_ref[...], NT)
        u = _dot(a, wu_ref[...], NT)
        act_ref[...] = (_silu(g) * u).astype(BF16)
        g_ref[...] = g.astype(BF16)
        u_ref[...] = u.astype(BF16)

    o_spec = pl.BlockSpec((tm, tn), lambda i, j: (i, j))
    sds = jax.ShapeDtypeStruct((s, f), BF16)
    return pl.pallas_call(
        body, name=name, grid=(s // tm, nj),
        in_specs=[pl.BlockSpec((tm, d), lambda i, j: (i, 0)), pl.BlockSpec((tn, d), lambda i, j: (j, 0)),
                  pl.BlockSpec((tn, d), lambda i, j: (j + nj, 0))],
        out_specs=[o_spec, o_spec, o_spec], out_shape=[sds, sds, sds],
        compiler_params=_params(("parallel", "parallel")))(hn, wf_t, wf_t)


def _swiglu_bwd(dh, w_out, g, u, name):
    s, d = dh.shape
    f = w_out.shape[0]
    tm = _pick(s, (1024, 512, 256, 128))
    tn = _pick(f, (512, 256, 128))

    def body(dh_ref, w_ref, g_ref, u_ref, dgu_ref):
        j = pl.program_id(1)
        dact = _dot(dh_ref[...], w_ref[...], NT)
        gv = g_ref[...].astype(F32)
        uv = u_ref[...].astype(F32)
        sg, dsg = _silu_and_grad(gv)
        dgu_ref[:, pl.ds(pl.multiple_of(j * tn, HEAD), tn)] = (dact * uv * dsg).astype(BF16)
        dgu_ref[:, pl.ds(pl.multiple_of(f + j * tn, HEAD), tn)] = (dact * sg).astype(BF16)

    o_spec = pl.BlockSpec((tm, tn), lambda i, j: (i, j))
    return pl.pallas_call(
        body, name=name, grid=(s // tm, f // tn),
        in_specs=[pl.BlockSpec((tm, d), lambda i, j: (i, 0)), pl.BlockSpec((tn, d), lambda i, j: (j, 0)), o_spec, o_spec],
        out_specs=pl.BlockSpec((tm, 2 * f), lambda i, j: (i, 0)), out_shape=jax.ShapeDtypeStruct((s, 2 * f), BF16),
        compiler_params=_params(("parallel", "arbitrary")))(dh, w_out, g, u)


def _ple_fwd(h, hn, p, w_gate, wp_t, name, norm_gs=()):
    s, d = h.shape
    pd = p.shape[1]
    tm = _pick(s, (512, 256, 128))
    ng = len(norm_gs)

    def body(h_ref, hn_ref, p_ref, wg_ref, wp_ref, *rest):
        g_refs, (o_ref, gp_ref, pp_ref), n_refs = rest[:ng], rest[ng:ng + 3], rest[ng + 3:]
        gpre = _dot(hn_ref[...], wg_ref[...], NN)
        pp = _dot(p_ref[...], wp_ref[...], NT)
        o = h_ref[...] + pp * jax.nn.sigmoid(gpre)
        o_ref[...] = o
        gp_ref[...] = gpre.astype(BF16)
        pp_ref[...] = pp.astype(BF16)
        if ng:
            on = o * lax.rsqrt(jnp.mean(o * o, axis=-1, keepdims=True) + EPS)
            for g_ref, n_ref in zip(g_refs, n_refs):
                n_ref[...] = (on * g_ref[...]).astype(BF16)

    row = pl.BlockSpec((tm, d), lambda i: (i, 0))
    vec = pl.BlockSpec((1, d), lambda i: (0, 0))
    bf = jax.ShapeDtypeStruct((s, d), BF16)
    return pl.pallas_call(
        body, name=name, grid=(s // tm,),
        in_specs=[row, row, pl.BlockSpec((tm, pd), lambda i: (i, 0)), pl.BlockSpec((d, d), lambda i: (0, 0)),
                  pl.BlockSpec((d, pd), lambda i: (0, 0))] + [vec] * ng,
        out_specs=[row] * (3 + ng), out_shape=[jax.ShapeDtypeStruct((s, d), F32), bf, bf] + [bf] * ng,
        compiler_params=_params(("parallel",)))(h, hn, p, w_gate, wp_t, *[g.reshape(1, d) for g in norm_gs])


def _ple_bwd(dh, gpre, pp, name):
    s, d = dh.shape
    tm = _pick(s, (512, 256, 128))

    def body(dh_ref, gp_ref, pp_ref, dgp_ref, dpp_ref):
        dv = dh_ref[...]
        sig = jax.nn.sigmoid(gp_ref[...].astype(F32))
        ppv = pp_ref[...].astype(F32)
        dpp_ref[...] = (dv * sig).astype(BF16)
        dgp_ref[...] = (dv * ppv * sig * (1.0 - sig)).astype(BF16)

    row = pl.BlockSpec((tm, d), lambda i: (i, 0))
    sds = jax.ShapeDtypeStruct((s, d), BF16)
    return pl.pallas_call(
        body, name=name, grid=(s // tm,), in_specs=[row, row, row], out_specs=[row, row], out_shape=[sds, sds],
        compiler_params=_params(("parallel",)))(dh, gpre, pp)


def _loss_fwd_bwd(y, t, name):
    s, d = y.shape
    tm = _pick(s, (512, 256, 128))

    def body(y_ref, t_ref, dy_ref, l_ref):
        @pl.when(pl.program_id(0) == 0)
        def _():
            l_ref[...] = jnp.zeros_like(l_ref)

        e = y_ref[...] - t_ref[...]
        dy_ref[...] = e * (1.0 / d)
        l_ref[...] += jnp.sum(e * e, axis=0, keepdims=True) * (0.5 / d)

    row = pl.BlockSpec((tm, d), lambda i: (i, 0))
    vec = pl.BlockSpec((1, d), lambda i: (0, 0))
    return pl.pallas_call(
        body, name=name, grid=(s // tm,), in_specs=[row, row], out_specs=[row, vec],
        out_shape=[jax.ShapeDtypeStruct((s, d), F32), jax.ShapeDtypeStruct((1, d), F32)],
        compiler_params=_params(("arbitrary",)))(y, t)


PADR = 8
CONV_ROWS = 256


def _conv_fwd(proj, w_conv, d, name):
    s = proj.shape[0]
    nh = d // HEAD
    kw = w_conv.shape[0]
    qscale = HEAD ** -0.5

    tr = _pick(s, (CONV_ROWS,))

    def body(x_ref, w_ref, o_ref, xp):
        kind = pl.program_id(0) // nh
        xp[0:PADR, :] = jnp.zeros((PADR, HEAD), F32)
        xp[PADR:, :] = x_ref[...]
        taps = [w_ref[j:j + 1, :] for j in range(kw)]
        for r0 in range(0, s, tr):
            acc = jnp.zeros((tr, HEAD), F32)
            for j in range(kw):
                acc = acc + taps[j] * xp[r0 + PADR - (kw - 1) + j:r0 + PADR - (kw - 1) + j + tr, :]
            a = _silu(acc)
            r = lax.rsqrt(jnp.sum(a * a, axis=-1, keepdims=True) + EPS)
            fac = jnp.where(kind == 0, r * qscale, jnp.where(kind == 1, r, jnp.ones_like(r)))
            o_ref[r0:r0 + tr, :] = a * fac

    blk = pl.BlockSpec((s, HEAD), lambda c: (0, c))
    hm = pl.BlockSpec((None, s, HEAD), lambda c: (c, 0, 0))
    return pl.pallas_call(
        body, name=name, grid=(3 * nh,), in_specs=[blk, pl.BlockSpec((kw, HEAD), lambda c: (0, c))], out_specs=hm,
        out_shape=jax.ShapeDtypeStruct((3 * nh, s, HEAD), F32), scratch_shapes=[pltpu.VMEM((s + PADR, HEAD), F32)],
        compiler_params=_params(("parallel",)))(proj, w_conv)


def _conv_bwd(dqkv, proj, w_conv, d, name):
    s = proj.shape[0]
    nh = d // HEAD
    kw = w_conv.shape[0]
    qscale = HEAD ** -0.5

    tr = _pick(s, (CONV_ROWS,))

    def body(dy_ref, x_ref, w_ref, dx_ref, dw_ref, xp, dp):
        kind = pl.program_id(0) // nh
        xp[0:PADR, :] = jnp.zeros((PADR, HEAD), F32)
        xp[PADR:, :] = x_ref[...]
        dp[s:, :] = jnp.zeros((PADR, HEAD), F32)
        taps = [w_ref[j:j + 1, :] for j in range(kw)]
        sc = jnp.where(kind == 0, qscale, 1.0)
        dws = [jnp.zeros((1, HEAD), F32) for _ in range(kw)]
        for r0 in range(0, s, tr):
            acc = jnp.zeros((tr, HEAD), F32)
            for j in range(kw):
                acc = acc + taps[j] * xp[r0 + PADR - (kw - 1) + j:r0 + PADR - (kw - 1) + j + tr, :]
            a, da_dacc = _silu_and_grad(acc)
            dy = dy_ref[r0:r0 + tr, :]
            r = lax.rsqrt(jnp.sum(a * a, axis=-1, keepdims=True) + EPS)
            dyn = dy * sc
            da_norm = r * dyn - a * (r * r * r) * jnp.sum(a * dyn, axis=-1, keepdims=True)
            dacc = jnp.where(kind == 2, dy, da_norm) * da_dacc
            dp[r0:r0 + tr, :] = dacc
            for j in range(kw):
                sh = kw - 1 - j
                dws[j] = dws[j] + jnp.sum(dacc * xp[r0 + PADR - sh:r0 + PADR - sh + tr, :], axis=0, keepdims=True)
        for j in range(kw):
            dw_ref[j:j + 1, :] = dws[j]
        for r0 in range(0, s, tr):
            dx = jnp.zeros((tr, HEAD), F32)
            for j in range(kw):
                sh = kw - 1 - j
                dx = dx + taps[j] * dp[r0 + sh:r0 + sh + tr, :]
            dx_ref[r0:r0 + tr, :] = dx.astype(BF16)

    blk = pl.BlockSpec((s, HEAD), lambda c: (0, c))
    hm = pl.BlockSpec((None, s, HEAD), lambda c: (c, 0, 0))
    wblk = pl.BlockSpec((kw, HEAD), lambda c: (0, c))
    return pl.pallas_call(
        body, name=name, grid=(3 * nh,), in_specs=[hm, blk, wblk], out_specs=[blk, wblk],
        out_shape=[jax.ShapeDtypeStruct((s, 3 * d), BF16), jax.ShapeDtypeStruct((kw, 3 * d), F32)],
        scratch_shapes=[pltpu.VMEM((s + PADR, HEAD), F32), pltpu.VMEM((s + PADR, HEAD), F32)],
        compiler_params=_params(("parallel",)))(dqkv, proj, w_conv)


def _softplus(x):
    return jnp.maximum(x, 0.0) + jnp.log(1.0 + jnp.exp(-jnp.abs(x)))


def _gates_fwd(pab, a_log, dt_bias, nh, name):
    s = pab.shape[0]
    tm = _pick(s, (512, 256, 128))

    def body(x_ref, al_ref, dt_ref, o_ref):
        x = x_ref[...]
        lane = lax.broadcasted_iota(jnp.int32, x.shape, 1)
        g = -jnp.exp(al_ref[...]) * _softplus(x + dt_ref[...])
        o_ref[...] = jnp.where(lane < nh, g, jnp.where(lane < 2 * nh, jax.nn.sigmoid(x), 0.0))

    row = pl.BlockSpec((tm, HEAD), lambda i: (i, 0))
    vec = pl.BlockSpec((1, HEAD), lambda i: (0, 0))
    return pl.pallas_call(
        body, name=name, grid=(s // tm,), in_specs=[row, vec, vec], out_specs=row,
        out_shape=jax.ShapeDtypeStruct((s, HEAD), F32), compiler_params=_params(("parallel",)))(pab, a_log, dt_bias)


def _gates_bwd(dgb, pab, a_log, dt_bias, nh, name):
    s = pab.shape[0]
    tm = _pick(s, (512, 256, 128))

    def body(d_ref, x_ref, al_ref, dt_ref, dx_ref, dal_ref, ddt_ref):
        @pl.when(pl.program_id(0) == 0)
        def _():
            dal_ref[...] = jnp.zeros_like(dal_ref)
            ddt_ref[...] = jnp.zeros_like(ddt_ref)

        x = x_ref[...]
        dv = d_ref[...]
        lane = lax.broadcasted_iota(jnp.int32, x.shape, 1)
        ea = jnp.exp(al_ref[...])
        xs = x + dt_ref[...]
        g = -ea * _softplus(xs)
        dxs = jnp.where(lane < nh, dv * (-ea) * jax.nn.sigmoid(xs), 0.0)
        sg = jax.nn.sigmoid(x)
        dxb = jnp.where((lane >= nh) & (lane < 2 * nh), dv * sg * (1.0 - sg), 0.0)
        dx_ref[...] = (dxs + dxb).astype(BF16)
        dal_ref[...] += jnp.sum(jnp.where(lane < nh, dv * g, 0.0), axis=0, keepdims=True)
        ddt_ref[...] += jnp.sum(dxs, axis=0, keepdims=True)

    row = pl.BlockSpec((tm, HEAD), lambda i: (i, 0))
    vec = pl.BlockSpec((1, HEAD), lambda i: (0, 0))
    return pl.pallas_call(
        body, name=name, grid=(s // tm,), in_specs=[row, row, vec, vec], out_specs=[row, vec, vec],
        out_shape=[jax.ShapeDtypeStruct((s, HEAD), BF16), jax.ShapeDtypeStruct((1, HEAD), F32),
                   jax.ShapeDtypeStruct((1, HEAD), F32)],
        compiler_params=_params(("arbitrary",)))(dgb, pab, a_log, dt_bias)


def _tri_inv(a_low, eye_f):
    n = -a_low
    p = eye_f + n
    steps = int(math.log2(a_low.shape[-1])) - 1
    for _ in range(steps):
        n = _dot(n, n, BNN)
        p = p + _dot(p, n, BNN)
    return p


def _lane_col(x, lane, idx):
    return jnp.sum(jnp.where(lane == idx, x, 0.0), axis=1, keepdims=True)


def _head_cols(gbv, lo, nh):
    lane = lax.broadcasted_iota(jnp.int32, gbv.shape, 1)
    return jnp.stack([_lane_col(gbv, lane, lo + h) for h in range(nh)], axis=0)


def _gdn_chunk(q, k, v, g_col, beta_col, st):
    c = q.shape[1]
    r_i = lax.broadcasted_iota(jnp.int32, (c, c), 0)
    c_i = lax.broadcasted_iota(jnp.int32, (c, c), 1)
    incl = c_i <= r_i
    strict = c_i < r_i
    eye = c_i == r_i
    g_row = jnp.sum(jnp.where(eye, g_col, 0.0), axis=1, keepdims=True)
    gc_col = jnp.sum(jnp.where(incl, g_row, 0.0), axis=2, keepdims=True)
    gc_row = jnp.sum(jnp.where(eye, gc_col, 0.0), axis=1, keepdims=True)
    g_last = jnp.sum(g_col, axis=1, keepdims=True)
    decay = jnp.exp(jnp.where(incl, gc_col - gc_row, NEG))
    kk = _dot(k, k, BNT)
    a_low = jnp.where(strict, beta_col * kk * decay, 0.0)
    t_inv = _tri_inv(a_low, eye.astype(F32))
    e_g = jnp.exp(gc_col)
    bk = beta_col * e_g
    rhs = jnp.concatenate([v * beta_col, k * bk], axis=2)
    sol = _dot(t_inv, rhs, BNN)
    u, w = sol[:, :, :HEAD], sol[:, :, HEAD:]
    qk_raw = _dot(q, k, BNT)
    qk = qk_raw * decay
    q_dec = q * e_g
    e2 = jnp.exp(g_last - gc_col)
    k_dec = k * e2
    gl = jnp.exp(g_last)
    ws = _dot(jnp.concatenate([w, q_dec], axis=1), st, BNN)
    v_new = u - ws[:, :c]
    o = ws[:, c:] + _dot(qk, v_new, BNN)
    st_new = st * gl + _dot(k_dec, v_new, BTN)
    inter = dict(incl=incl, strict=strict, eye=eye, decay=decay, kk=kk, t_inv=t_inv, e_g=e_g, bk=bk, sol=sol, w=w,
                 qk_raw=qk_raw, qk=qk, q_dec=q_dec, e2=e2, k_dec=k_dec, gl=gl, v_new=v_new, c_i=c_i, r_i=r_i)
    return o, st_new, inter


def _gdn_fwd(qkv, gb, nh, name):
    s = qkv.shape[1]
    nc = s // CHUNK

    def body(q_ref, k_ref, v_ref, gb_ref, o_ref, st_ref, state):
        @pl.when(pl.program_id(0) == 0)
        def _():
            state[...] = jnp.zeros_like(state)

        gbv = gb_ref[...]
        st = state[...]
        st_ref[...] = st
        o, st_new, _ = _gdn_chunk(q_ref[...], k_ref[...], v_ref[...], _head_cols(gbv, 0, nh), _head_cols(gbv, nh, nh), st)
        o_ref[...] = o
        state[...] = st_new

    def qspec(part):
        return pl.BlockSpec((nh, CHUNK, HEAD), lambda n: (part, n, 0))

    return pl.pallas_call(
        body, name=name, grid=(nc,),
        in_specs=[qspec(0), qspec(1), qspec(2), pl.BlockSpec((CHUNK, HEAD), lambda n: (n, 0))],
        out_specs=[qspec(0), pl.BlockSpec((None, nh, HEAD, HEAD), lambda n: (n, 0, 0, 0))],
        out_shape=[jax.ShapeDtypeStruct((nh, s, HEAD), F32), jax.ShapeDtypeStruct((nc, nh, HEAD, HEAD), F32)],
        scratch_shapes=[pltpu.VMEM((nh, HEAD, HEAD), F32)],
        compiler_params=_params(("arbitrary",)))(qkv, qkv, qkv, gb)


def _gdn_bwd(qkv, gb, do, states, nh, name):
    s = qkv.shape[1]
    nc = s // CHUNK
    c = CHUNK

    def body(q_ref, k_ref, v_ref, gb_ref, do_ref, st_ref, dqkv_ref, dgb_ref, dstate):
        @pl.when(pl.program_id(0) == 0)
        def _():
            dstate[...] = jnp.zeros_like(dstate)

        gbv = gb_ref[...]
        lane = lax.broadcasted_iota(jnp.int32, gbv.shape, 1)
        q, k, v = q_ref[...], k_ref[...], v_ref[...]
        beta_col = _head_cols(gbv, nh, nh)
        st = st_ref[...]
        dst = dstate[...]
        dov = do_ref[...]
        _, _, it = _gdn_chunk(q, k, v, _head_cols(gbv, 0, nh), beta_col, st)
        incl, strict, eye, decay = it["incl"], it["strict"], it["eye"], it["decay"]
        dv_new = _dot(it["qk"], dov, BTN) + _dot(it["k_dec"], dst, BNN)
        d_qk = _dot(dov, it["v_new"], BNT)
        dd = _dot(jnp.concatenate([dov, -dv_new], axis=1), st, BNT)
        dq_dec, dw = dd[:, :c], dd[:, c:]
        dst_new = _dot(it["q_dec"], dov, BTN) + it["gl"] * dst - _dot(it["w"], dv_new, BTN)
        dgl = jnp.sum(jnp.sum(dst * st, axis=2, keepdims=True), axis=1, keepdims=True)
        dk_dec = _dot(it["v_new"], dst, BNT)
        dsol = jnp.concatenate([dv_new, dw], axis=2)
        drhs = _dot(it["t_inv"], dsol, BTN)
        d_a = jnp.where(strict, -_dot(drhs, it["sol"], BNT), 0.0)
        drhs_u, drhs_w = drhs[:, :, :HEAD], drhs[:, :, HEAD:]
        dvh = beta_col * drhs_u
        rw_k = jnp.sum(drhs_w * k, axis=2, keepdims=True)
        dbeta = jnp.sum(drhs_u * v, axis=2, keepdims=True) + it["e_g"] * rw_k
        dkh = it["bk"] * drhs_w
        dgc_col = it["bk"] * rw_k
        dkk = d_a * beta_col * decay
        dbeta = dbeta + jnp.sum(d_a * it["kk"] * decay, axis=2, keepdims=True)
        ddecay = d_a * beta_col * it["kk"]
        dkh = dkh + _dot(dkk, k, BNN) + _dot(dkk, k, BTN)
        dqk_raw = d_qk * decay
        ddecay = ddecay + d_qk * it["qk_raw"]
        dqh = _dot(dqk_raw, k, BNN)
        dkh = dkh + _dot(dqk_raw, q, BTN)
        ddm = jnp.where(incl, ddecay * decay, 0.0)
        dgc_col = dgc_col + jnp.sum(ddm, axis=2, keepdims=True)
        dgc_row = -jnp.sum(ddm, axis=1, keepdims=True)
        dqh = dqh + dq_dec * it["e_g"]
        dgc_col = dgc_col + jnp.sum(dq_dec * it["q_dec"], axis=2, keepdims=True)
        dkh = dkh + dk_dec * it["e2"]
        tmp = jnp.sum(dk_dec * it["k_dec"], axis=2, keepdims=True)
        dgc_col = dgc_col - tmp
        dg_last = jnp.sum(tmp, axis=1, keepdims=True) + dgl * it["gl"]
        dgc_tot_row = dgc_row + jnp.sum(jnp.where(eye, dgc_col, 0.0), axis=1, keepdims=True)
        dg_col = jnp.sum(jnp.where(it["c_i"] >= it["r_i"], dgc_tot_row, 0.0), axis=2, keepdims=True) + dg_last
        dqkv_ref[0] = dqh
        dqkv_ref[1] = dkh
        dqkv_ref[2] = dvh
        dstate[...] = dst_new
        dgb_acc = jnp.zeros(gbv.shape, F32)
        for h in range(nh):
            dgb_acc = jnp.where(lane == h, dg_col[h], jnp.where(lane == nh + h, dbeta[h], dgb_acc))
        dgb_ref[...] = dgb_acc

    def rev(part):
        return pl.BlockSpec((nh, CHUNK, HEAD), lambda n: (part, nc - 1 - n, 0))

    gspec = pl.BlockSpec((CHUNK, HEAD), lambda n: (nc - 1 - n, 0))
    dqkv, dgb = pl.pallas_call(
        body, name=name, grid=(nc,),
        in_specs=[rev(0), rev(1), rev(2), gspec, rev(0),
                  pl.BlockSpec((None, nh, HEAD, HEAD), lambda n: (nc - 1 - n, 0, 0, 0))],
        out_specs=[pl.BlockSpec((3, nh, CHUNK, HEAD), lambda n: (0, 0, nc - 1 - n, 0)), gspec],
        out_shape=[jax.ShapeDtypeStruct((3, nh, s, HEAD), F32), jax.ShapeDtypeStruct((s, HEAD), F32)],
        scratch_shapes=[pltpu.VMEM((nh, HEAD, HEAD), F32)],
        compiler_params=_params(("arbitrary",)))(qkv, qkv, qkv, gb, do, states)
    return dqkv.reshape(3 * nh, s, HEAD), dgb


SB_TQ_FWD = 1024
SB_TQ = 512


def _tri01(rel):
    j_i = lax.broadcasted_iota(jnp.int32, (2 * SBLK, SBLK), 0) & (SBLK - 1)
    s_i = lax.broadcasted_iota(jnp.int32, (2 * SBLK, SBLK), 1)
    return rel(j_i, s_i).astype(BF16)


SB_HP = 2


def _each(fn, *lists):
    return [fn(*xs) for xs in zip(*lists)]


def _sb_scores(qts, kblks, mask, csums, rhs01, one_dot):
    zs = _each(lambda qt, kb: _dot(qt, kb, NT), qts, kblks)
    es = _each(lambda z: jnp.exp(-jnp.abs(z)), zs)
    sps = _each(lambda z, e: jnp.maximum(z, 0.0) + jnp.log(1.0 + e), zs, es)
    lns = _each(lambda sp: -sp if mask is None else jnp.where(mask, -sp, 0.0), sps)
    sts = _each(lambda ln: _dot_hilo(ln, rhs01, one_dot), lns)
    wgts = _each(lambda z, sp, st, cs: jnp.exp((z - sp) + st + cs), zs, sps, sts, csums)
    if mask is not None:
        wgts = _each(lambda w: jnp.where(mask, w, 0.0), wgts)
    return zs, es, wgts, lns


def _band_mask(rows, j, row0):
    r_i = lax.broadcasted_iota(jnp.int32, (rows, SBLK), 0)
    c_i = lax.broadcasted_iota(jnp.int32, (rows, SBLK), 1)
    return (j * SBLK + c_i) < (row0 + r_i)


def _sb_fwd(q, k, v, name):
    s, d = q.shape
    nh = d // HEAD
    tq = min(SB_TQ_FWD, s)
    nb = tq // SBLK

    hp = SB_HP
    heads = [slice(h * HEAD, (h + 1) * HEAD) for h in range(hp)]

    def body(q_ref, k_ref, v_ref, o_ref, c_ref, acc, cs):
        qb = pl.program_id(1)
        lane = lax.broadcasted_iota(jnp.int32, (tq, HEAD), 1)
        after = _tri01(lambda j, t: j > t)
        acc[...] = jnp.zeros_like(acc)
        cs[...] = jnp.zeros_like(cs)
        c_ref[...] = jnp.zeros_like(c_ref)

        def process(rs, kb, mask):
            keys = pl.ds(pl.multiple_of(kb * SBLK, SBLK), SBLK)
            csums = [cs[h, rs, :] for h in range(hp)]
            _, _, wgts, lns = _sb_scores([q_ref[rs, hs] for hs in heads], [k_ref[keys, hs] for hs in heads], mask, csums, after, True)
            pvs = _each(lambda w, hs: _dot(w, v_ref[keys, hs]), wgts, heads)
            tots = _each(lambda ln: jnp.sum(ln, axis=1, keepdims=True), lns)
            for h, hs in enumerate(heads):
                acc[h, rs, :] += pvs[h]
                c_ref[rs, hs] = jnp.where(lane[rs, :] == kb, csums[h], c_ref[rs, hs])
                cs[h, rs, :] = csums[h] + tots[h]

        for j in reversed(range(nb)):
            process(slice(j * SBLK, tq), qb * nb + j, _band_mask(tq - j * SBLK, j, j * SBLK))

        def step(it, carry):
            process(slice(0, tq), qb * nb - 1 - it, None)
            return carry

        lax.fori_loop(0, qb * nb, step, 0)
        for h, hs in enumerate(heads):
            o_ref[:, hs] = acc[h].astype(BF16)

    qspec = pl.BlockSpec((tq, hp * HEAD), lambda h, i: (i, h))
    kspec = pl.BlockSpec((s, hp * HEAD), lambda h, i: (0, h))
    return pl.pallas_call(
        body, name=name, grid=(nh // hp, s // tq), in_specs=[qspec, kspec, kspec], out_specs=[qspec, qspec],
        out_shape=[jax.ShapeDtypeStruct((s, d), BF16), jax.ShapeDtypeStruct((s, d), F32)],
        scratch_shapes=[pltpu.VMEM((hp, tq, HEAD), F32), pltpu.VMEM((hp, tq, 1), F32)],
        compiler_params=_params(("parallel", "arbitrary")))(q, k, v)


def _sb_bwd(q, k, v, do, ctab, name):
    s, d = q.shape
    nh = d // HEAD
    tq = min(SB_TQ, s)
    nb = tq // SBLK

    hp = SB_HP
    heads = [slice(h * HEAD, (h + 1) * HEAD) for h in range(hp)]

    def body(q_ref, k_ref, v_ref, do_ref, c_ref, dq_ref, dk_ref, dv_ref, ps):
        qb = pl.program_id(1)

        @pl.when(qb == 0)
        def _():
            dk_ref[...] = jnp.zeros_like(dk_ref)
            dv_ref[...] = jnp.zeros_like(dv_ref)

        dq_ref[...] = jnp.zeros_like(dq_ref)
        ps[...] = jnp.zeros_like(ps)
        lane = lax.broadcasted_iota(jnp.int32, (tq, HEAD), 1)
        after = _tri01(lambda j, t: j > t)
        before = _tri01(lambda j, t: j < t)

        def process(rs, kb, mask):
            keys = pl.ds(pl.multiple_of(kb * SBLK, SBLK), SBLK)
            kblks = [k_ref[keys, hs] for hs in heads]
            qts = [q_ref[rs, hs] for hs in heads]
            dots = [do_ref[rs, hs] for hs in heads]
            csums = [_lane_col(c_ref[rs, hs], lane[rs, :], kb) for hs in heads]
            zs, es, wgts, _ = _sb_scores(qts, kblks, mask, csums, after, False)
            dlws = _each(lambda dt, hs, w: _dot(dt, v_ref[keys, hs], NT) * w, dots, heads, wgts)
            pts = _each(lambda dlw: _dot_hilo(dlw, before, False), dlws)
            pfxs = [ps[h, rs, :] for h in range(hp)]
            rs_ = _each(lambda e: 1.0 / (1.0 + e), es)
            sigs = _each(lambda z, e, r: jnp.where(z >= 0.0, r, e * r), zs, es, rs_)
            dzs = _each(lambda dlw, sig, pfx, pt: dlw * (1.0 - sig) - sig * (pfx + pt), dlws, sigs, pfxs, pts)
            tots = _each(lambda dlw: jnp.sum(dlw, axis=1, keepdims=True), dlws)
            if mask is not None:
                dzs = _each(lambda dz: jnp.where(mask, dz, 0.0), dzs)
            dqs = _each(lambda dz, kb_: _dot(dz, kb_), dzs, kblks)
            dks = _each(lambda dz, qt: _dot(dz, qt, TN), dzs, qts)
            dvs = _each(lambda w, dt: _dot(w, dt, TN), wgts, dots)
            for h, hs in enumerate(heads):
                dq_ref[rs, hs] += dqs[h]
                dk_ref[keys, hs] += dks[h]
                dv_ref[keys, hs] += dvs[h]
                ps[h, rs, :] = pfxs[h] + tots[h]

        def step(kb, carry):
            process(slice(0, tq), kb, None)
            return carry

        lax.fori_loop(0, qb * nb, step, 0)
        for j in range(nb):
            process(slice(j * SBLK, tq), qb * nb + j, _band_mask(tq - j * SBLK, j, j * SBLK))

    qspec = pl.BlockSpec((tq, hp * HEAD), lambda h, i: (i, h))
    kspec = pl.BlockSpec((s, hp * HEAD), lambda h, i: (0, h))
    sds = jax.ShapeDtypeStruct((s, d), F32)
    return pl.pallas_call(
        body, name=name, grid=(nh // hp, s // tq), in_specs=[qspec, kspec, kspec, qspec, qspec],
        out_specs=[qspec, kspec, kspec], out_shape=[sds, sds, sds],
        scratch_shapes=[pltpu.VMEM((hp, tq, 1), F32)],
        compiler_params=_params(("parallel", "arbitrary")))(q, k, v, do, ctab)


def _my_index():
    return 4 * lax.axis_index("x") + 2 * lax.axis_index("y") + lax.axis_index("c")


def _all_gather(x_shard, name):
    m_per, n = x_shard.shape

    def body(x_ref, out_ref, send_sems, recv_sems, local_sem):
        x, y, c = lax.axis_index("x"), lax.axis_index("y"), lax.axis_index("c")
        me, sibling = (x, y, c), (x, y, 1 - c)
        chips = [(1 - x, y), (x, 1 - y), (1 - x, 1 - y)]

        def rows(px, py, pc):
            return out_ref.at[pl.ds((4 * px + 2 * py + pc) * m_per, m_per), :]

        def copy(k, block, to, src=None):
            return pltpu.make_async_remote_copy(
                src_ref=rows(*block) if src is None else src, dst_ref=rows(*block),
                send_sem=send_sems.at[k], recv_sem=recv_sems.at[k], device_id=to, device_id_type=MESH)

        mine = pltpu.make_async_copy(x_ref, rows(*me), local_sem)
        mine.start()
        first = [copy(0, me, sibling, src=x_ref)]
        first += [copy(1 + j, me, (*chip, c), src=x_ref) for j, chip in enumerate(chips)]
        for cp in first:
            cp.start()
        passed = [copy(4 + j, (*chip, c), sibling) for j, chip in enumerate(chips)]
        for j, chip in enumerate(chips):
            copy(1 + j, (*chip, c), me).wait_recv()
            passed[j].start()
        copy(0, sibling, me).wait_recv()
        for j, chip in enumerate(chips):
            copy(4 + j, (*chip, 1 - c), me).wait_recv()
        for cp in first + passed:
            cp.wait_send()
        mine.wait()

    return pl.pallas_call(
        body, name=name, out_shape=jax.ShapeDtypeStruct((NDEV * m_per, n), x_shard.dtype),
        in_specs=[pl.BlockSpec(memory_space=pl.ANY)], out_specs=pl.BlockSpec(memory_space=pl.ANY),
        scratch_shapes=[pltpu.SemaphoreType.DMA((7,)), pltpu.SemaphoreType.DMA((7,)), pltpu.SemaphoreType.DMA],
    )(x_shard)


HBM_SPEC = pl.BlockSpec(memory_space=pltpu.HBM)
SEM_SPEC = pl.BlockSpec(memory_space=pltpu.SEMAPHORE)
ANY_SPEC = pl.BlockSpec(memory_space=pl.ANY)
EFFECT = pltpu.SideEffectType.DATAFLOW_SIDE_EFFECTING


def _exchange_copies(src_refs, land_refs, send_sems, recv_sems, self_sems, scatter):
    x, y, c = lax.axis_index("x"), lax.axis_index("y"), lax.axis_index("c")
    me = 4 * x + 2 * y + c
    remote, local = [], []
    for p, (src_ref, land_ref) in enumerate(zip(src_refs, land_refs)):
        rows = land_ref.shape[0] // NDEV

        def part(idx):
            return src_ref.at[pl.ds(idx * rows, rows), :] if scatter else src_ref

        slot = land_ref.at[pl.ds(me * rows, rows), :]
        for k in range(1, NDEV):
            px, py, pc = x ^ ((k >> 2) & 1), y ^ ((k >> 1) & 1), c ^ (k & 1)
            remote.append(pltpu.make_async_remote_copy(
                src_ref=part(4 * px + 2 * py + pc), dst_ref=slot, send_sem=send_sems.at[7 * p + k - 1],
                recv_sem=recv_sems.at[7 * p + k - 1], device_id=(px, py, pc), device_id_type=MESH))
        local.append(pltpu.make_async_copy(part(me), slot, self_sems.at[p]))
    return remote, local


def _send_start(srcs, scatter, after, name):
    n = len(srcs)
    lands = []
    for s in srcs:
        rows = s.shape[0] if scatter else NDEV * s.shape[0]
        lands.append(pltpu.with_memory_space_constraint(lax.empty((rows, s.shape[1]), s.dtype), pltpu.HBM))

    def body(*refs):
        src_refs, land_refs = refs[:n], refs[n:2 * n]
        send_sems, recv_sems, self_sems = refs[2 * n + 1:2 * n + 4]
        remote, local = _exchange_copies(src_refs, land_refs, send_sems, recv_sems, self_sems, scatter)
        for cp in remote + local:
            cp.start()
        refs[-1][...] = jnp.zeros_like(refs[-1])

    hbm = lambda a: pltpu.HBM(a.shape, a.dtype)
    out = pl.pallas_call(
        body, name=name,
        out_shape=(pltpu.SemaphoreType.DMA((7 * n,)), pltpu.SemaphoreType.DMA((7 * n,)), pltpu.SemaphoreType.DMA((n,)),
                   *[hbm(s) for s in srcs], *[hbm(a) for a in lands], jax.ShapeDtypeStruct((8, HEAD), F32)),
        in_specs=(HBM_SPEC,) * (2 * n) + (ANY_SPEC,),
        out_specs=(SEM_SPEC,) * 3 + (HBM_SPEC,) * (2 * n) + (pl.BlockSpec(memory_space=pltpu.VMEM),),
        input_output_aliases={i: 3 + i for i in range(2 * n)},
        compiler_params=pltpu.CompilerParams(has_side_effects=EFFECT),
    )(*[pltpu.with_memory_space_constraint(s, pltpu.HBM) for s in srcs], *lands, after)
    return dict(sems=out[:3], srcs=out[3:3 + n], lands=out[3 + n:3 + 2 * n], token=out[-1])


def _send_wait(started, scatter, after, name):
    srcs, lands = started["srcs"], started["lands"]
    n = len(srcs)

    def body(*refs):
        src_refs, land_refs = refs[:n], refs[n:2 * n]
        send_sems, recv_sems, self_sems = refs[2 * n:2 * n + 3]
        remote, local = _exchange_copies(src_refs, land_refs, send_sems, recv_sems, self_sems, scatter)
        for cp in remote:
            cp.wait_send()
            cp.wait_recv()
        for cp in local:
            cp.wait()

    hbm = lambda a: pltpu.HBM(a.shape, a.dtype)
    out = pl.pallas_call(
        body, name=name, out_shape=(*[hbm(s) for s in srcs], *[hbm(a) for a in lands]),
        in_specs=(HBM_SPEC,) * (2 * n) + (SEM_SPEC,) * 3 + (ANY_SPEC,), out_specs=(HBM_SPEC,) * (2 * n),
        input_output_aliases={i: i for i in range(2 * n)},
        compiler_params=pltpu.CompilerParams(has_side_effects=EFFECT),
    )(*srcs, *lands, *started["sems"], after)
    return out[n:]


def _sum_slots(xs, name, rows_out=None):
    _, r, c = xs[0].shape
    ro = rows_out or r
    tc = _pick(c, (128,))

    def body(*refs):
        o_ref = refs[-1]
        for l, x_ref in enumerate(refs[:-1]):
            acc = x_ref[0].astype(F32)
            for i in range(1, NDEV):
                acc = acc + x_ref[i].astype(F32)
            o_ref[l] = acc[:ro]

    return pl.pallas_call(
        body, name=name, grid=(c // tc,), in_specs=[pl.BlockSpec((NDEV, r, tc), lambda j: (0, 0, j))] * len(xs),
        out_specs=pl.BlockSpec((len(xs), ro, tc), lambda j: (0, 0, j)),
        out_shape=jax.ShapeDtypeStruct((len(xs), ro, c), F32), compiler_params=_params(("parallel",)))(*xs)


def _adamw(w, g, m, v, name):
    if w.ndim == 3:
        nl, r, c = w.shape
        tc = _pick(c, (256, 128))
        grid = (nl, c // tc)
        blk = pl.BlockSpec((None, r, tc), lambda i, j: (i, 0, j))
        sem = ("parallel", "parallel")
    else:
        r, c = w.shape
        tr = _pick(r, (256, 128, 64, 32, 16, 8))
        grid = (r // tr,)
        blk = pl.BlockSpec((tr, c), lambda i: (i, 0))
        sem = ("parallel",)
    c1 = 1.0 - B1 ** STEP
    c2 = 1.0 - B2 ** STEP

    def body(w_ref, g_ref, m_ref, v_ref, d_ref, nm_ref, nv_ref):
        gv = g_ref[...]
        nm = B1 * m_ref[...] + (1.0 - B1) * gv
        nv = B2 * v_ref[...] + (1.0 - B2) * (gv * gv)
        d_ref[...] = -LR * ((nm / c1) / (jnp.sqrt(nv / c2) + ADAM_EPS) + WD * w_ref[...])
        nm_ref[...] = nm
        nv_ref[...] = nv

    sds = jax.ShapeDtypeStruct(w.shape, F32)
    return pl.pallas_call(
        body, name=name, grid=grid, in_specs=[blk] * 4, out_specs=[blk] * 3, out_shape=[sds] * 3,
        compiler_params=_params(sem))(w, g, m, v)


def _pad_rows(a, mult):
    r = a.shape[0]
    pad = (-r) % mult
    return a if pad == 0 else jnp.pad(a, ((0, pad), (0, 0)))


def _pad_lanes(v, width=HEAD):
    return jnp.pad(v.reshape(1, -1), ((0, 0), (0, width - v.shape[-1])))


def kernel(x, p, ln_mix, ln_ffn, ln_ple, gdn_w_in, gdn_conv, gdn_a_log, gdn_dt_bias, gdn_norm, gdn_w_out, kv_norm, w_kv, k_norm, sb_w_q, sb_q_norm, sb_w_out, ffn_w_in, ffn_w_out, ple_w_proj, ple_w_gate, loss_target, m_ln_mix, m_ln_ffn, m_ln_ple, m_gdn_w_in, m_gdn_conv, m_gdn_a_log, m_gdn_dt_bias, m_gdn_norm, m_gdn_w_out, m_kv_norm, m_w_kv, m_k_norm, m_sb_w_q, m_sb_q_norm, m_sb_w_out, m_ffn_w_in, m_ffn_w_out, m_ple_w_proj, m_ple_w_gate, v_ln_mix, v_ln_ffn, v_ln_ple, v_gdn_w_in, v_gdn_conv, v_gdn_a_log, v_gdn_dt_bias, v_gdn_norm, v_gdn_w_out, v_kv_norm, v_w_kv, v_k_norm, v_sb_w_q, v_sb_q_norm, v_sb_w_out, v_ffn_w_in, v_ffn_w_out, v_ple_w_proj, v_ple_w_gate):
    s, d = x.shape[1], x.shape[2]
    nh = d // HEAD
    depth = ln_mix.shape[0]
    n_a = gdn_w_in.shape[0]
    n_b = sb_w_q.shape[0]
    me = _my_index()
    win_cols = gdn_w_in.shape[2]
    win_rows = 4 * d + 2 * nh

    def col_t(w):
        return jnp.transpose(w).astype(BF16)

    local = {}
    for l in range(n_a):
        local[("gdn_w_in", l)] = col_t(gdn_w_in[l])
        local[("gdn_w_out", l)] = gdn_w_out[l].astype(BF16)
    local[("w_kv", 0)] = col_t(w_kv)
    for j in range(n_b):
        local[("sb_w_q", j)] = sb_w_q[j].astype(BF16)
        local[("sb_w_out", j)] = sb_w_out[j].astype(BF16)
    for l in range(depth):
        local[("ffn_w_in", l)] = col_t(ffn_w_in[l])
        local[("ffn_w_out", l)] = ffn_w_out[l].astype(BF16)
        local[("ple_w_proj", l)] = col_t(ple_w_proj[l]).reshape(-1, d)
        local[("ple_w_gate", l)] = ple_w_gate[l].astype(BF16)
    local = {key: _pad_rows(a, 16) for key, a in local.items()}

    chunks = []
    for l in range(depth):
        mix = [("gdn_w_in", l), ("gdn_w_out", l)] if l < n_a else [("sb_w_q", l - n_a), ("sb_w_out", l - n_a)]
        rest = [("ffn_w_in", l), ("ffn_w_out", l), ("ple_w_proj", l), ("ple_w_gate", l)]
        if l == n_a - 1:
            rest.append(("w_kv", 0))
        chunks += [(f"a{l}", mix), (f"f{l}", rest)]
    chunk_keys = dict(chunks)

    conv_rows = n_a * gdn_conv.shape[1]
    conv_sh = _pad_rows(gdn_conv.reshape(conv_rows, -1), 8)
    conv_g = _all_gather(conv_sh, "comm_gather_conv")
    token = conv_g
    conv_g = conv_g.reshape(NDEV, conv_sh.shape[0], -1)
    conv_full = jnp.transpose(conv_g[:, :conv_rows, :], (1, 0, 2)).reshape(n_a, gdn_conv.shape[1], 3 * d)

    w_started = {}
    for name, keys in chunks:
        w_started[name] = _send_start([local[k] for k in keys], False, token, f"comm_wstart_{name}")
        token = w_started[name]["token"]

    full = {}

    def fetch(name, after):
        lands = _send_wait(w_started[name], False, after, f"comm_wwait_{name}")
        for key, land in zip(chunk_keys[name], lands):
            full[key] = land

    def whole(key, valid=None):
        a = full[key]
        if valid is not None:
            a = a.reshape(NDEV, -1, d)[:, :valid, :].reshape(-1, d)
        return a

    pd = p.shape[-1]
    w_in_t, w_ab_t, w_gout, w_q, w_sout, wf_t, w_fout, wp_t, w_pg = {}, {}, {}, {}, {}, {}, {}, {}, {}
    wkv_t = None

    h = x[0]
    sv = []
    kv_sv = None
    k_sh = v_sh = None
    for l in range(depth):
        t = {}
        t["h0"] = h
        if l == 0:
            hn = _rms_fwd(h, ln_mix[l], f"rms_mix_{l}")
        t["hn"] = hn
        fetch(f"a{l}", token if l == 0 else hn)
        if l < n_a:
            wt = whole(("gdn_w_in", l), win_cols)
            w_in_t[l] = wt[:4 * d]
            w_ab_t[l] = jnp.pad(wt[4 * d:], ((0, HEAD - 2 * nh), (0, 0)))
            w_gout[l] = whole(("gdn_w_out", l))
        else:
            w_q[l - n_a] = whole(("sb_w_q", l - n_a))
            w_sout[l - n_a] = whole(("sb_w_out", l - n_a))
        if l < n_a:
            proj = _mm(hn, w_in_t[l], "nt", f"gdn_proj_{l}")
            pab = _mm(hn, w_ab_t[l], "nt", f"gdn_proj_ab_{l}")
            qkv = _conv_fwd(proj, conv_full[l], d, f"gdn_conv_{l}")
            al, dtb = _pad_lanes(gdn_a_log[l]), _pad_lanes(gdn_dt_bias[l])
            gb = _gates_fwd(pab, al, dtb, nh, f"gdn_gates_{l}")
            o_raw, states = _gdn_fwd(qkv, gb, nh, f"gdn_rule_{l}")
            o2 = _headnorm_fwd(o_raw, gdn_norm[l], f"gdn_outnorm_{l}", gate=proj, gate_col0=3 * d, head_major=True)
            h, hn2 = _mm(o2, w_gout[l], "nn", f"gdn_out_{l}", res=h, norm_g=ln_ffn[l])
            t.update(proj=proj, pab=pab, qkv=qkv, gb=gb, o_raw=o_raw, states=states, o2=o2, al=al, dtb=dtb)
        else:
            j = l - n_a
            qpre = _mm(hn, w_q[j], "nn", f"sb_qproj_{j}")
            qn = _headnorm_fwd(qpre, sb_q_norm[j], f"sb_qnorm_{j}", scale=HEAD ** -0.5)
            o, ctab = _sb_fwd(qn, k_sh, v_sh, f"sb_attn_{j}")
            h, hn2 = _mm(o, w_sout[j], "nn", f"sb_out_{j}", res=h, norm_g=ln_ffn[l])
            t.update(qpre=qpre, qn=qn, o=o, ctab=ctab)
        t["h1"] = h
        fetch(f"f{l}", hn2)
        wf_t[l] = whole(("ffn_w_in", l))
        w_fout[l] = whole(("ffn_w_out", l))
        wp_t[l] = full[("ple_w_proj", l)].reshape(d, pd)
        w_pg[l] = whole(("ple_w_gate", l))
        if l == n_a - 1:
            wkv_t = whole(("w_kv", 0))
        act, gs, us = _swiglu_fwd(hn2, wf_t[l], f"ffn_in_{l}")
        h, hn3 = _mm(act, w_fout[l], "nn", f"ffn_out_{l}", res=h, norm_g=ln_ple[l])
        t.update(hn2=hn2, act=act, gs=gs, us=us, h2=h)
        gains = ([ln_mix[l + 1]] if l + 1 < depth else []) + ([kv_norm] if l == n_a - 1 else [])
        h, gpre, pp, *normed = _ple_fwd(h, hn3, p[l, 0], w_pg[l], wp_t[l], f"ple_{l}", norm_gs=gains)
        if l + 1 < depth:
            hn = normed[0]
        t.update(hn3=hn3, gpre=gpre, pp=pp)
        sv.append(t)
        if l == n_a - 1:
            kvn = normed[-1]
            kv = _mm(kvn, wkv_t, "nt", "kv_proj")
            k_sh = _headnorm_fwd(kv, k_norm, "k_norm", width=d)
            v_sh = kv[:, d:].astype(BF16)
            kv_sv = dict(h=h, kvn=kvn, kv=kv)

    dh, loss_vec = _loss_fwd_bwd(h, loss_target[0], "loss")
    loss = lax.psum(jnp.sum(loss_vec), ("x", "y", "c"))

    gw = {}
    small = {}
    g_started = {}

    def scatter_start(name):
        gparts = []
        for key in chunk_keys[name]:
            g = gw[key]
            g = g.reshape(NDEV, -1, d) if key[0] == "ple_w_proj" else g.reshape(NDEV, -1, g.shape[-1])
            padr = local[key].shape[0] - g.shape[1]
            if padr:
                g = jnp.pad(g, ((0, 0), (0, padr), (0, 0)))
            gparts.append(g.reshape(-1, d))
        g_started[name] = _send_start(gparts, True, gparts[0], f"comm_gstart_{name}")
        return g_started[name]["token"]

    dkv_sh = None
    for l in reversed(range(depth)):
        t = sv[l]
        if l == n_a - 1:
            dkv_k, dkn = _headnorm_bwd(dkv_sh[0], kv_sv["kv"], k_norm, "k_norm_bwd", dx_dtype=BF16)
            dkv = jnp.concatenate([dkv_k, dkv_sh[1].astype(BF16)], axis=1)
            gw[("w_kv", 0)] = _mm(dkv, kv_sv["kvn"], "tn", "kv_dw", out_dtype=BF16)
            dh, _, dg = _mm(dkv, wkv_t, "nn", "kv_dx", norm_bwd=(kv_sv["h"], kv_norm, dh))
            small["kv_norm"] = dg
            small["k_norm"] = dkn
        dgp, dpp = _ple_bwd(dh, t["gpre"], t["pp"], f"ple_bwd_{l}")
        gw[("ple_w_gate", l)] = _mm(t["hn3"], dgp, "tn", f"ple_dwg_{l}", out_dtype=BF16)
        gw[("ple_w_proj", l)] = _mm(dpp, p[l, 0], "tn", f"ple_dwp_{l}", out_dtype=BF16)
        dh, dhb, dg = _mm(dgp, w_pg[l], "nt", f"ple_dx_{l}", norm_bwd=(t["h2"], ln_ple[l], dh))
        small[("ln_ple", l)] = dg
        dgu = _swiglu_bwd(dhb, w_fout[l], t["gs"], t["us"], f"ffn_bwd_act_{l}")
        gw[("ffn_w_out", l)] = _mm(t["act"], dhb, "tn", f"ffn_dwo_{l}", out_dtype=BF16)
        gw[("ffn_w_in", l)] = _mm(dgu, t["hn2"], "tn", f"ffn_dwi_{l}", out_dtype=BF16)
        dh, dhb, dg = _mm(dgu, wf_t[l], "nn", f"ffn_dx_{l}", norm_bwd=(t["h1"], ln_ffn[l], dh),
                          after=scatter_start(f"f{l}"))
        small[("ln_ffn", l)] = dg
        if l < n_a:
            do2 = _mm(dhb, w_gout[l], "nt", f"gdn_out_dx_{l}")
            gw[("gdn_w_out", l)] = _mm(t["o2"], dhb, "tn", f"gdn_out_dw_{l}", out_dtype=BF16)
            do_raw, dgn, dgate = _headnorm_bwd(do2, t["o_raw"], gdn_norm[l], f"gdn_outnorm_bwd_{l}",
                                               gate=t["proj"], gate_col0=3 * d, head_major=True)
            small[("gdn_norm", l)] = dgn
            dqkv, dgb = _gdn_bwd(t["qkv"], t["gb"], do_raw, t["states"], nh, f"gdn_rule_bwd_{l}")
            dpab, dal, ddt = _gates_bwd(dgb, t["pab"], t["al"], t["dtb"], nh, f"gdn_gates_bwd_{l}")
            small[("gdn_a_log", l)] = dal
            small[("gdn_dt_bias", l)] = ddt
            dproj_qkv, dconv = _conv_bwd(dqkv, t["proj"], conv_full[l], d, f"gdn_conv_bwd_{l}")
            small[("gdn_conv", l)] = dconv
            dproj = jnp.concatenate([dproj_qkv, dgate], axis=1)
            dw_main = _mm(dproj, t["hn"], "tn", f"gdn_proj_dw_{l}", out_dtype=BF16)
            dw_ab = _mm(dpab, t["hn"], "tn", f"gdn_proj_ab_dw_{l}", out_dtype=BF16)
            gw[("gdn_w_in", l)] = jnp.concatenate([dw_main, dw_ab[:16]], axis=0)[:win_rows]
            dhn_ab = _mm(dpab, w_ab_t[l], "nn", f"gdn_proj_ab_dx_{l}")
            last = dict(a=dproj, b=w_in_t[l], mode="nn", name=f"gdn_proj_dx_{l}", res=dhn_ab)
        else:
            j = l - n_a
            do = _mm(dhb, w_sout[j], "nt", f"sb_out_dx_{j}", out_dtype=BF16)
            gw[("sb_w_out", j)] = _mm(t["o"], dhb, "tn", f"sb_out_dw_{j}", out_dtype=BF16)
            dq, dk, dv = _sb_bwd(t["qn"], k_sh, v_sh, do, t["ctab"], f"sb_attn_bwd_{j}")
            dkv_sh = (dk, dv) if dkv_sh is None else (dkv_sh[0] + dk, dkv_sh[1] + dv)
            dqpre, dqn = _headnorm_bwd(dq, t["qpre"], sb_q_norm[j], f"sb_qnorm_bwd_{j}", scale=HEAD ** -0.5, dx_dtype=BF16)
            small[("sb_q_norm", j)] = dqn
            gw[("sb_w_q", j)] = _mm(t["hn"], dqpre, "tn", f"sb_q_dw_{j}", out_dtype=BF16)
            last = dict(a=dqpre, b=w_q[j], mode="nt", name=f"sb_q_dx_{j}")
        dh, _, dg = _mm(**last, norm_bwd=(t["h0"], ln_mix[l], dh), after=scatter_start(f"a{l}"))
        small[("ln_mix", l)] = dg
    grad_x = dh[None]

    landed = {}
    for name, keys in reversed(chunks):
        lands = _send_wait(g_started[name], True, dh, f"comm_gwait_{name}")
        for key, land in zip(keys, lands):
            landed[key] = land.reshape(NDEV, -1, d)

    def summed(wname, count, rows_out=None):
        return _sum_slots([landed[(wname, i)] for i in range(count)], f"grad_sum_{wname}", rows_out)

    gt_gdn_w_in = summed("gdn_w_in", n_a, win_cols)
    gt_ffn_w_in = summed("ffn_w_in", depth)
    g_gdn_w_in = jnp.transpose(gt_gdn_w_in, (0, 2, 1))
    g_gdn_w_out = summed("gdn_w_out", n_a)
    g_w_kv = jnp.transpose(summed("w_kv", 1)[0])
    g_sb_w_q = summed("sb_w_q", n_b)
    g_sb_w_out = summed("sb_w_out", n_b)
    g_ffn_w_in = jnp.transpose(gt_ffn_w_in, (0, 2, 1))
    g_ffn_w_out = summed("ffn_w_out", depth)
    g_ple_w_proj = jnp.transpose(summed("ple_w_proj", depth).reshape(depth, -1, pd), (0, 2, 1))
    g_ple_w_gate = summed("ple_w_gate", depth)

    def vec_rows(v):
        return v.reshape(-1, HEAD)

    small_items = []
    for name_, cnt in (("ln_mix", depth), ("ln_ffn", depth), ("ln_ple", depth)):
        for l in range(cnt):
            small_items.append(((name_, l), vec_rows(small[(name_, l)])))
    for l in range(n_a):
        small_items.append((("gdn_conv", l), small[("gdn_conv", l)].reshape(-1, HEAD)))
        small_items.append((("gdn_a_log", l), small[("gdn_a_log", l)]))
        small_items.append((("gdn_dt_bias", l), small[("gdn_dt_bias", l)]))
        small_items.append((("gdn_norm", l), small[("gdn_norm", l)]))
    small_items.append(("kv_norm", vec_rows(small["kv_norm"])))
    small_items.append(("k_norm", small["k_norm"]))
    for j in range(n_b):
        small_items.append((("sb_q_norm", j), small[("sb_q_norm", j)]))
    spack = jnp.concatenate([_pad_rows(a, 8) for _, a in small_items], axis=0)
    sg = _all_gather(spack, "comm_gather_small").reshape(NDEV, spack.shape[0], HEAD)
    ssum = _sum_slots([sg], "small_sum")[0]
    sm = {}
    off = 0
    for key, a in small_items:
        sm[key] = ssum[off:off + a.shape[0]]
        off += a.shape[0] + (-a.shape[0]) % 8

    g_ln_mix = jnp.stack([sm[("ln_mix", l)].reshape(d) for l in range(depth)])
    g_ln_ffn = jnp.stack([sm[("ln_ffn", l)].reshape(d) for l in range(depth)])
    g_ln_ple = jnp.stack([sm[("ln_ple", l)].reshape(d) for l in range(depth)])
    conv_loc = gdn_conv.shape[2]
    g_conv_full = jnp.stack([sm[("gdn_conv", l)].reshape(gdn_conv.shape[1], 3 * d) for l in range(n_a)])
    g_gdn_conv = lax.dynamic_slice_in_dim(g_conv_full, me * conv_loc, conv_loc, axis=2)
    g_a_log = jnp.stack([sm[("gdn_a_log", l)][0, :nh] for l in range(n_a)])
    g_dt_bias = jnp.stack([sm[("gdn_dt_bias", l)][0, :nh] for l in range(n_a)])
    g_gdn_norm = jnp.stack([sm[("gdn_norm", l)][0] for l in range(n_a)])
    g_kv_norm = sm["kv_norm"].reshape(d)
    g_k_norm = sm["k_norm"][0]
    g_sb_q_norm = jnp.stack([sm[("sb_q_norm", j)][0] for j in range(n_b)])

    grads = [g_ln_mix, g_ln_ffn, g_ln_ple, g_gdn_w_in, g_gdn_conv, g_a_log, g_dt_bias, g_gdn_norm, g_gdn_w_out,
             g_kv_norm, g_w_kv, g_k_norm, g_sb_w_q, g_sb_q_norm, g_sb_w_out, g_ffn_w_in, g_ffn_w_out, g_ple_w_proj,
             g_ple_w_gate]
    weights = [ln_mix, ln_ffn, ln_ple, gdn_w_in, gdn_conv, gdn_a_log, gdn_dt_bias, gdn_norm, gdn_w_out, kv_norm, w_kv,
               k_norm, sb_w_q, sb_q_norm, sb_w_out, ffn_w_in, ffn_w_out, ple_w_proj, ple_w_gate]
    moms = [m_ln_mix, m_ln_ffn, m_ln_ple, m_gdn_w_in, m_gdn_conv, m_gdn_a_log, m_gdn_dt_bias, m_gdn_norm, m_gdn_w_out,
            m_kv_norm, m_w_kv, m_k_norm, m_sb_w_q, m_sb_q_norm, m_sb_w_out, m_ffn_w_in, m_ffn_w_out, m_ple_w_proj,
            m_ple_w_gate]
    vels = [v_ln_mix, v_ln_ffn, v_ln_ple, v_gdn_w_in, v_gdn_conv, v_gdn_a_log, v_gdn_dt_bias, v_gdn_norm, v_gdn_w_out,
            v_kv_norm, v_w_kv, v_k_norm, v_sb_w_q, v_sb_q_norm, v_sb_w_out, v_ffn_w_in, v_ffn_w_out, v_ple_w_proj,
            v_ple_w_gate]

    deltas, new_m, new_v = [], [], []
    small_idx = [i for i, w in enumerate(weights) if w.size < 8 * HEAD * 16]
    transposed = {3: gt_gdn_w_in, 15: gt_ffn_w_in}
    for i, (w, g, m, v) in enumerate(zip(weights, grads, moms, vels)):
        if i in small_idx:
            deltas.append(None), new_m.append(None), new_v.append(None)
            continue
        if i in transposed:
            tr = lambda a: jnp.transpose(a, (0, 2, 1))
            dl, nm, nv = _adamw(tr(w), transposed[i], tr(m), tr(v), f"adamw_{i}")
            deltas.append(tr(dl)), new_m.append(tr(nm)), new_v.append(tr(nv))
            continue
        shp = w.shape
        two = lambda a: a.reshape(-1, shp[-1])
        dl, nm, nv = _adamw(two(w), two(g), two(m), two(v), f"adamw_{i}")
        deltas.append(dl.reshape(shp)), new_m.append(nm.reshape(shp)), new_v.append(nv.reshape(shp))

    def flat_pack(arrs):
        flat = jnp.concatenate([a.reshape(-1) for a in arrs])
        pad = (-flat.shape[0]) % (8 * HEAD)
        return jnp.pad(flat, (0, pad)).reshape(-1, HEAD)

    sw = flat_pack([weights[i] for i in small_idx])
    sgr = flat_pack([grads[i] for i in small_idx])
    smo = flat_pack([moms[i] for i in small_idx])
    sve = flat_pack([vels[i] for i in small_idx])
    sdl, snm, snv = _adamw(sw, sgr, smo, sve, "adamw_small")
    off = 0
    for i in small_idx:
        n = weights[i].size
        shp = weights[i].shape
        deltas[i] = sdl.reshape(-1)[off:off + n].reshape(shp)
        new_m[i] = snm.reshape(-1)[off:off + n].reshape(shp)
        new_v[i] = snv.reshape(-1)[off:off + n].reshape(shp)
        off += n

    return (loss, grad_x, *grads, *deltas, *new_m, *new_v)
```

```python
import math

import jax
import jax.numpy as jnp
from jax import lax
from jax.experimental import pallas as pl
from jax.experimental.pallas import tpu as pltpu

F32 = jnp.float32
BF16 = jnp.bfloat16
NDEV = 8
HEAD = 128
CHUNK = 64
SBLK = 256
EPS = 1e-6
LR, B1, B2, ADAM_EPS, WD, STEP = 0.001, 0.9, 0.999, 1e-08, 0.01, 10
NEG = -1e30
MM_VMEM_BUDGET = 40 * 1024 * 1024

NN = (((1,), (0,)), ((), ()))
NT = (((1,), (1,)), ((), ()))
TN = (((0,), (0,)), ((), ()))
BNN = (((2,), (1,)), ((0,), (0,)))
BNT = (((2,), (2,)), ((0,), (0,)))
BTN = (((1,), (1,)), ((0,), (0,)))
MESH = pl.DeviceIdType.MESH


def _dot(a, b, dims=NN):
    return lax.dot_general(a.astype(BF16), b.astype(BF16), dims, preferred_element_type=F32)


def _dot_hilo(a, b01_twice, one_dot):
    hi = a.astype(BF16)
    lo = (a - hi.astype(F32)).astype(BF16)
    if one_dot:
        return lax.dot_general(jnp.concatenate([hi, lo], axis=1), b01_twice, NN, preferred_element_type=F32)
    b01 = b01_twice[:a.shape[1]]
    return (lax.dot_general(hi, b01, NN, preferred_element_type=F32)
            + lax.dot_general(lo, b01, NN, preferred_element_type=F32))


def _pick(dim, cands):
    for c in cands:
        if dim % c == 0:
            return c
    return dim


def _params(sem, vmem_mb=48):
    return pltpu.CompilerParams(dimension_semantics=sem, vmem_limit_bytes=vmem_mb * 1024 * 1024)


def _silu(x):
    return x * jax.nn.sigmoid(x)


def _silu_and_grad(x):
    s = jax.nn.sigmoid(x)
    xs = x * s
    return xs, s + xs * (1.0 - s)


def _mm(a, b, mode, name, out_dtype=F32, res=None, norm_g=None, norm_bwd=None, after=None):
    if mode == "nn":
        (m, k), n = a.shape, b.shape[1]
    elif mode == "nt":
        (m, k), n = a.shape, b.shape[0]
    else:
        (k, m), n = a.shape, b.shape[1]
    rows = norm_g is not None or norm_bwd is not None
    tn = n if rows else _pick(n, (512, 256, 128))
    tk = k if k <= 4096 else max(t for t in range(128, 4097, 128) if k % t == 0)
    nk = k // tk
    out_b = jnp.dtype(out_dtype).itemsize + (res.dtype.itemsize if res is not None else 0)
    out_b += 2 if norm_g is not None else 0
    out_b += 10 if norm_bwd is not None else 0
    for tm in [t for t in range(min(m, 2048), 127, -128) if m % t == 0] + [m]:
        need = 2 * (tm * tk * a.dtype.itemsize + tk * tn * b.dtype.itemsize + tm * tn * out_b) + 4 * tm * tn
        if need <= MM_VMEM_BUDGET:
            break
    dims = {"nn": NN, "nt": NT, "tn": TN}[mode]
    if mode == "tn":
        a_spec = pl.BlockSpec((tk, tm), lambda i, j, kk: (kk, i))
    else:
        a_spec = pl.BlockSpec((tm, tk), lambda i, j, kk: (i, kk))
    if mode == "nt":
        b_spec = pl.BlockSpec((tn, tk), lambda i, j, kk: (j, kk))
    else:
        b_spec = pl.BlockSpec((tk, tn), lambda i, j, kk: (kk, j))
    mn_spec = pl.BlockSpec((tm, tn), lambda i, j, kk: (i, j))
    vec_spec = pl.BlockSpec((1, tn), lambda i, j, kk: (0, j))
    has_res = res is not None
    n_in = 2 + has_res + (1 if norm_g is not None else 0) + (3 if norm_bwd is not None else 0) + (after is not None)

    def body(*refs):
        a_ref, b_ref = refs[:2]
        extra = list(refs[2:n_in])
        outs = refs[n_in:-1]
        acc = refs[-1]
        kk = pl.program_id(2)

        @pl.when(kk == 0)
        def _():
            acc[...] = jnp.zeros_like(acc)

        if norm_bwd is not None:
            @pl.when((kk == 0) & (pl.program_id(0) == 0))
            def _():
                outs[2][...] = jnp.zeros_like(outs[2])

        acc[...] += _dot(a_ref[...], b_ref[...], dims)

        @pl.when(kk == nk - 1)
        def _():
            r = acc[...]
            if has_res:
                r = r + extra.pop(0)[...].astype(F32)
            if norm_g is not None:
                outs[0][...] = r.astype(out_dtype)
                rs = lax.rsqrt(jnp.mean(r * r, axis=-1, keepdims=True) + EPS)
                outs[1][...] = (r * rs * extra.pop(0)[...]).astype(BF16)
            elif norm_bwd is not None:
                xv, gv, dres = extra.pop(0)[...], extra.pop(0)[...], extra.pop(0)[...]
                rs = lax.rsqrt(jnp.mean(xv * xv, axis=-1, keepdims=True) + EPS)
                gdy = r * gv
                dx = dres + rs * gdy - xv * (rs * rs * rs) * jnp.mean(xv * gdy, axis=-1, keepdims=True)
                outs[0][...] = dx
                outs[1][...] = dx.astype(BF16)
                outs[2][...] += jnp.sum(r * xv * rs, axis=0, keepdims=True)
            else:
                outs[0][...] = r.astype(out_dtype)

    ins = [a, b] + ([res] if has_res else [])
    in_specs = [a_spec, b_spec] + ([mn_spec] if has_res else [])
    out_specs, out_shape = [mn_spec], [jax.ShapeDtypeStruct((m, n), out_dtype)]
    sem = ("parallel", "parallel", "arbitrary")
    if norm_g is not None:
        ins.append(norm_g.reshape(1, n))
        in_specs.append(vec_spec)
        out_specs.append(mn_spec)
        out_shape.append(jax.ShapeDtypeStruct((m, n), BF16))
    if norm_bwd is not None:
        x, g, dres = norm_bwd
        ins += [x, g.reshape(1, n), dres]
        in_specs += [mn_spec, vec_spec, mn_spec]
        out_specs += [mn_spec, vec_spec]
        out_shape += [jax.ShapeDtypeStruct((m, n), BF16), jax.ShapeDtypeStruct((1, n), F32)]
        sem = ("arbitrary", "arbitrary", "arbitrary")
    if after is not None:
        ins.append(after)
        in_specs.append(pl.BlockSpec(memory_space=pl.ANY))
    out = pl.pallas_call(
        body, name=name, grid=(m // tm, n // tn, nk), in_specs=in_specs, out_specs=out_specs,
        out_shape=out_shape, scratch_shapes=[pltpu.VMEM((tm, tn), F32)],
        compiler_params=_params(sem))(*ins)
    return out[0] if len(out) == 1 else out


def _rms_fwd(h, g, name):
    s, d = h.shape
    tm = _pick(s, (512, 256, 128))

    def body(h_ref, g_ref, o_ref):
        x = h_ref[...]
        r = lax.rsqrt(jnp.mean(x * x, axis=-1, keepdims=True) + EPS)
        o_ref[...] = (x * r * g_ref[...]).astype(BF16)

    return pl.pallas_call(
        body, name=name, grid=(s // tm,),
        in_specs=[pl.BlockSpec((tm, d), lambda i: (i, 0)), pl.BlockSpec((1, d), lambda i: (0, 0))],
        out_specs=pl.BlockSpec((tm, d), lambda i: (i, 0)),
        out_shape=jax.ShapeDtypeStruct((s, d), BF16), compiler_params=_params(("parallel",)))(h, g.reshape(1, d))


def _headnorm_fwd(x, g, name, scale=1.0, gate=None, gate_col0=0, out_dtype=BF16, width=None, head_major=False):
    if head_major:
        s, d = x.shape[1], x.shape[0] * HEAD
    else:
        s, d = x.shape[0], (width or x.shape[1])
    nh = d // HEAD
    tm = _pick(s, (256, 128))
    has_gate = gate is not None
    gb = gate_col0 // d

    def body(*refs):
        if has_gate:
            x_ref, g_ref, gt_ref, o_ref = refs
        else:
            x_ref, g_ref, o_ref = refs
        gv = g_ref[...]
        for h in range(nh):
            sl = slice(h * HEAD, (h + 1) * HEAD)
            xv = (x_ref[h] if head_major else x_ref[:, sl]).astype(F32)
            r = lax.rsqrt(jnp.mean(xv * xv, axis=-1, keepdims=True) + EPS)
            y = xv * r * gv
            if scale != 1.0:
                y = y * scale
            if has_gate:
                y = y * _silu(gt_ref[:, sl])
            o_ref[:, sl] = y.astype(out_dtype)

    row = pl.BlockSpec((tm, d), lambda i: (i, 0))
    hm = pl.BlockSpec((nh, tm, HEAD), lambda i: (0, i, 0))
    ins = [x, g.reshape(1, HEAD)]
    in_specs = [hm if head_major else row, pl.BlockSpec((1, HEAD), lambda i: (0, 0))]
    if has_gate:
        ins.append(gate)
        in_specs.append(pl.BlockSpec((tm, d), lambda i: (i, gb)))
    return pl.pallas_call(
        body, name=name, grid=(s // tm,), in_specs=in_specs, out_specs=row,
        out_shape=jax.ShapeDtypeStruct((s, d), out_dtype), compiler_params=_params(("parallel",)))(*ins)


def _headnorm_bwd(dy, x, g, name, scale=1.0, gate=None, gate_col0=0, dx_dtype=F32, head_major=False):
    s, d = dy.shape
    nh = d // HEAD
    tm = _pick(s, (256, 128))
    has_gate = gate is not None
    gb = gate_col0 // d

    def body(*refs):
        if has_gate:
            dy_ref, x_ref, g_ref, gt_ref, dx_ref, dg_ref, dgt_ref = refs
        else:
            dy_ref, x_ref, g_ref, dx_ref, dg_ref = refs

        @pl.when(pl.program_id(0) == 0)
        def _():
            dg_ref[...] = jnp.zeros_like(dg_ref)

        gv = g_ref[...]
        dg_acc = jnp.zeros((1, HEAD), F32)
        for h in range(nh):
            sl = slice(h * HEAD, (h + 1) * HEAD)
            xv = (x_ref[h] if head_major else x_ref[:, sl]).astype(F32)
            dyv = dy_ref[:, sl].astype(F32)
            r = lax.rsqrt(jnp.mean(xv * xv, axis=-1, keepdims=True) + EPS)
            if has_gate:
                gt = gt_ref[:, sl]
                act, dact = _silu_and_grad(gt)
                dgt_ref[:, sl] = (dyv * (xv * r * gv) * dact).astype(dgt_ref.dtype)
                dn = dyv * act
            else:
                dn = dyv
            if scale != 1.0:
                dn = dn * scale
            gdn = dn * gv
            mean_t = jnp.mean(xv * gdn, axis=-1, keepdims=True)
            dxv = (r * gdn - xv * (r * r * r) * mean_t).astype(dx_dtype)
            if head_major:
                dx_ref[h] = dxv
            else:
                dx_ref[:, sl] = dxv
            dg_acc = dg_acc + jnp.sum(dn * xv * r, axis=0, keepdims=True)
        dg_ref[...] += dg_acc

    row = pl.BlockSpec((tm, d), lambda i: (i, 0))
    hm = pl.BlockSpec((nh, tm, HEAD), lambda i: (0, i, 0))
    vec = pl.BlockSpec((1, HEAD), lambda i: (0, 0))
    ins = [dy, x, g.reshape(1, HEAD)]
    in_specs = [row, hm if head_major else row, vec]
    out_specs = [hm if head_major else row, vec]
    dx_shape = (nh, s, HEAD) if head_major else (s, d)
    out_shape = [jax.ShapeDtypeStruct(dx_shape, dx_dtype), jax.ShapeDtypeStruct((1, HEAD), F32)]
    if has_gate:
        ins.append(gate)
        in_specs.append(pl.BlockSpec((tm, d), lambda i: (i, gb)))
        out_specs.append(row)
        out_shape.append(jax.ShapeDtypeStruct((s, d), BF16))
    return pl.pallas_call(
        body, name=name, grid=(s // tm,), in_specs=in_specs, out_specs=out_specs, out_shape=out_shape,
        compiler_params=_params(("arbitrary",)))(*ins)


def _swiglu_fwd(hn, wf_t, name):
    s, d = hn.shape
    f = wf_t.shape[0] // 2
    tm = _pick(s, (1024, 512, 256, 128))
    tn = _pick(f, (512, 256, 128))
    nj = f // tn

    def body(a_ref, wg_ref, wu_ref, act_ref, g_ref, u_ref):
        a = a_ref[...]
        g = _dot(a, wg_ref[...], NT)
        u = _dot(a, wu_ref[...], NT)
        act_ref[...] = (_silu(g) * u).astype(BF16)
        g_ref[...] = g.astype(BF16)
        u_ref[...] = u.astype(BF16)

    o_spec = pl.BlockSpec((tm, tn), lambda i, j: (i, j))
    sds = jax.ShapeDtypeStruct((s, f), BF16)
    return pl.pallas_call(
        body, name=name, grid=(s // tm, nj),
        in_specs=[pl.BlockSpec((tm, d), lambda i, j: (i, 0)), pl.BlockSpec((tn, d), lambda i, j: (j, 0)),
                  pl.BlockSpec((tn, d), lambda i, j: (j + nj, 0))],
        out_specs=[o_spec, o_spec, o_spec], out_shape=[sds, sds, sds],
        compiler_params=_params(("parallel", "parallel")))(hn, wf_t, wf_t)


def _swiglu_bwd(dh, w_out, g, u, name):
    s, d = dh.shape
    f = w_out.shape[0]
    tm = _pick(s, (1024, 512, 256, 128))
    tn = _pick(f, (512, 256, 128))

    def body(dh_ref, w_ref, g_ref, u_ref, dgu_ref):
        j = pl.program_id(1)
        dact = _dot(dh_ref[...], w_ref[...], NT)
        gv = g_ref[...].astype(F32)
        uv = u_ref[...].astype(F32)
        sg, dsg = _silu_and_grad(gv)
        dgu_ref[:, pl.ds(pl.multiple_of(j * tn, HEAD), tn)] = (dact * uv * dsg).astype(BF16)
        dgu_ref[:, pl.ds(pl.multiple_of(f + j * tn, HEAD), tn)] = (dact * sg).astype(BF16)

    o_spec = pl.BlockSpec((tm, tn), lambda i, j: (i, j))
    return pl.pallas_call(
        body, name=name, grid=(s // tm, f // tn),
        in_specs=[pl.BlockSpec((tm, d), lambda i, j: (i, 0)), pl.BlockSpec((tn, d), lambda i, j: (j, 0)), o_spec, o_spec],
        out_specs=pl.BlockSpec((tm, 2 * f), lambda i, j: (i, 0)), out_shape=jax.ShapeDtypeStruct((s, 2 * f), BF16),
        compiler_params=_params(("parallel", "arbitrary")))(dh, w_out, g, u)


def _ple_fwd(h, hn, p, w_gate, wp_t, name, norm_gs=()):
    s, d = h.shape
    pd = p.shape[1]
    tm = _pick(s, (512, 256, 128))
    ng = len(norm_gs)

    def body(h_ref, hn_ref, p_ref, wg_ref, wp_ref, *rest):
        g_refs, (o_ref, gp_ref, pp_ref), n_refs = rest[:ng], rest[ng:ng + 3], rest[ng + 3:]
        gpre = _dot(hn_ref[...], wg_ref[...], NN)
        pp = _dot(p_ref[...], wp_ref[...], NT)
        o = h_ref[...] + pp * jax.nn.sigmoid(gpre)
        o_ref[...] = o
        gp_ref[...] = gpre.astype(BF16)
        pp_ref[...] = pp.astype(BF16)
        if ng:
            on = o * lax.rsqrt(jnp.mean(o * o, axis=-1, keepdims=True) + EPS)
            for g_ref, n_ref in zip(g_refs, n_refs):
                n_ref[...] = (on * g_ref[...]).astype(BF16)

    row = pl.BlockSpec((tm, d), lambda i: (i, 0))
    vec = pl.BlockSpec((1, d), lambda i: (0, 0))
    bf = jax.ShapeDtypeStruct((s, d), BF16)
    return pl.pallas_call(
        body, name=name, grid=(s // tm,),
        in_specs=[row, row, pl.BlockSpec((tm, pd), lambda i: (i, 0)), pl.BlockSpec((d, d), lambda i: (0, 0)),
                  pl.BlockSpec((d, pd), lambda i: (0, 0))] + [vec] * ng,
        out_specs=[row] * (3 + ng), out_shape=[jax.ShapeDtypeStruct((s, d), F32), bf, bf] + [bf] * ng,
        compiler_params=_params(("parallel",)))(h, hn, p, w_gate, wp_t, *[g.reshape(1, d) for g in norm_gs])


def _ple_bwd(dh, gpre, pp, name):
    s, d = dh.shape
    tm = _pick(s, (512, 256, 128))

    def body(dh_ref, gp_ref, pp_ref, dgp_ref, dpp_ref):
        dv = dh_ref[...]
        sig = jax.nn.sigmoid(gp_ref[...].astype(F32))
        ppv = pp_ref[...].astype(F32)
        dpp_ref[...] = (dv * sig).astype(BF16)
        dgp_ref[...] = (dv * ppv * sig * (1.0 - sig)).astype(BF16)

    row = pl.BlockSpec((tm, d), lambda i: (i, 0))
    sds = jax.ShapeDtypeStruct((s, d), BF16)
    return pl.pallas_call(
        body, name=name, grid=(s // tm,), in_specs=[row, row, row], out_specs=[row, row], out_shape=[sds, sds],
        compiler_params=_params(("parallel",)))(dh, gpre, pp)


def _loss_fwd_bwd(y, t, name):
    s, d = y.shape
    tm = _pick(s, (512, 256, 128))

    def body(y_ref, t_ref, dy_ref, l_ref):
        @pl.when(pl.program_id(0) == 0)
        def _():
            l_ref[...] = jnp.zeros_like(l_ref)

        e = y_ref[...] - t_ref[...]
        dy_ref[...] = e * (1.0 / d)
        l_ref[...] += jnp.sum(e * e, axis=0, keepdims=True) * (0.5 / d)

    row = pl.BlockSpec((tm, d), lambda i: (i, 0))
    vec = pl.BlockSpec((1, d), lambda i: (0, 0))
    return pl.pallas_call(
        body, name=name, grid=(s // tm,), in_specs=[row, row], out_specs=[row, vec],
        out_shape=[jax.ShapeDtypeStruct((s, d), F32), jax.ShapeDtypeStruct((1, d), F32)],
        compiler_params=_params(("arbitrary",)))(y, t)


PADR = 8
CONV_ROWS = 256
CONV_HEADS = 2


def _conv_fwd(proj, w_conv, d, name):
    s = proj.shape[0]
    nh = d // HEAD
    kw = w_conv.shape[0]
    qscale = HEAD ** -0.5

    tr = _pick(s, (CONV_ROWS,))
    ch = CONV_HEADS
    cw = ch * HEAD

    def body(x_ref, w_ref, o_ref, xp):
        kind = (pl.program_id(0) * ch) // nh
        xp[0:PADR, :] = jnp.zeros((PADR, cw), F32)
        xp[PADR:, :] = x_ref[...]
        for hh in range(ch):
            hs = slice(hh * HEAD, (hh + 1) * HEAD)
            taps = [w_ref[j:j + 1, hs] for j in range(kw)]
            for r0 in range(0, s, tr):
                acc = jnp.zeros((tr, HEAD), F32)
                for j in range(kw):
                    acc = acc + taps[j] * xp[r0 + PADR - (kw - 1) + j:r0 + PADR - (kw - 1) + j + tr, hs]
                a = _silu(acc)
                r = lax.rsqrt(jnp.sum(a * a, axis=-1, keepdims=True) + EPS)
                fac = jnp.where(kind == 0, r * qscale, jnp.where(kind == 1, r, jnp.ones_like(r)))
                o_ref[hh, r0:r0 + tr, :] = a * fac

    blk = pl.BlockSpec((s, cw), lambda c: (0, c))
    hm = pl.BlockSpec((ch, s, HEAD), lambda c: (c, 0, 0))
    return pl.pallas_call(
        body, name=name, grid=(3 * nh // ch,), in_specs=[blk, pl.BlockSpec((kw, cw), lambda c: (0, c))], out_specs=hm,
        out_shape=jax.ShapeDtypeStruct((3 * nh, s, HEAD), F32), scratch_shapes=[pltpu.VMEM((s + PADR, cw), F32)],
        compiler_params=_params(("parallel",)))(proj, w_conv)


def _conv_bwd(dqkv, proj, w_conv, d, name):
    s = proj.shape[0]
    nh = d // HEAD
    kw = w_conv.shape[0]
    qscale = HEAD ** -0.5

    tr = _pick(s, (CONV_ROWS,))
    ch = CONV_HEADS
    cw = ch * HEAD

    def body(dy_ref, x_ref, w_ref, dx_ref, dw_ref, xp, dp):
        kind = (pl.program_id(0) * ch) // nh
        xp[0:PADR, :] = jnp.zeros((PADR, cw), F32)
        xp[PADR:, :] = x_ref[...]
        dp[s:, :] = jnp.zeros((PADR, cw), F32)
        sc = jnp.where(kind == 0, qscale, 1.0)
        for hh in range(ch):
            hs = slice(hh * HEAD, (hh + 1) * HEAD)
            taps = [w_ref[j:j + 1, hs] for j in range(kw)]
            dws = [jnp.zeros((1, HEAD), F32) for _ in range(kw)]
            for r0 in range(0, s, tr):
                acc = jnp.zeros((tr, HEAD), F32)
                for j in range(kw):
                    acc = acc + taps[j] * xp[r0 + PADR - (kw - 1) + j:r0 + PADR - (kw - 1) + j + tr, hs]
                a, da_dacc = _silu_and_grad(acc)
                dy = dy_ref[hh, r0:r0 + tr, :]
                r = lax.rsqrt(jnp.sum(a * a, axis=-1, keepdims=True) + EPS)
                dyn = dy * sc
                da_norm = r * dyn - a * (r * r * r) * jnp.sum(a * dyn, axis=-1, keepdims=True)
                dacc = jnp.where(kind == 2, dy, da_norm) * da_dacc
                dp[r0:r0 + tr, hs] = dacc
                for j in range(kw):
                    sh = kw - 1 - j
                    dws[j] = dws[j] + jnp.sum(dacc * xp[r0 + PADR - sh:r0 + PADR - sh + tr, hs], axis=0, keepdims=True)
            for j in range(kw):
                dw_ref[j:j + 1, hs] = dws[j]
            for r0 in range(0, s, tr):
                dx = jnp.zeros((tr, HEAD), F32)
                for j in range(kw):
                    sh = kw - 1 - j
                    dx = dx + taps[j] * dp[r0 + sh:r0 + sh + tr, hs]
                dx_ref[r0:r0 + tr, hs] = dx.astype(BF16)

    blk = pl.BlockSpec((s, cw), lambda c: (0, c))
    hm = pl.BlockSpec((ch, s, HEAD), lambda c: (c, 0, 0))
    wblk = pl.BlockSpec((kw, cw), lambda c: (0, c))
    return pl.pallas_call(
        body, name=name, grid=(3 * nh // ch,), in_specs=[hm, blk, wblk], out_specs=[blk, wblk],
        out_shape=[jax.ShapeDtypeStruct((s, 3 * d), BF16), jax.ShapeDtypeStruct((kw, 3 * d), F32)],
        scratch_shapes=[pltpu.VMEM((s + PADR, cw), F32), pltpu.VMEM((s + PADR, cw), F32)],
        compiler_params=_params(("parallel",)))(dqkv, proj, w_conv)


def _softplus(x):
    return jnp.maximum(x, 0.0) + jnp.log(1.0 + jnp.exp(-jnp.abs(x)))


def _gates_fwd(pab, a_log, dt_bias, nh, name):
    s = pab.shape[0]
    tm = _pick(s, (512, 256, 128))

    def body(x_ref, al_ref, dt_ref, o_ref):
        x = x_ref[...]
        lane = lax.broadcasted_iota(jnp.int32, x.shape, 1)
        g = -jnp.exp(al_ref[...]) * _softplus(x + dt_ref[...])
        o_ref[...] = jnp.where(lane < nh, g, jnp.where(lane < 2 * nh, jax.nn.sigmoid(x), 0.0))

    row = pl.BlockSpec((tm, HEAD), lambda i: (i, 0))
    vec = pl.BlockSpec((1, HEAD), lambda i: (0, 0))
    return pl.pallas_call(
        body, name=name, grid=(s // tm,), in_specs=[row, vec, vec], out_specs=row,
        out_shape=jax.ShapeDtypeStruct((s, HEAD), F32), compiler_params=_params(("parallel",)))(pab, a_log, dt_bias)


def _gates_bwd(dgb, pab, a_log, dt_bias, nh, name):
    s = pab.shape[0]
    tm = _pick(s, (512, 256, 128))

    def body(d_ref, x_ref, al_ref, dt_ref, dx_ref, dal_ref, ddt_ref):
        @pl.when(pl.program_id(0) == 0)
        def _():
            dal_ref[...] = jnp.zeros_like(dal_ref)
            ddt_ref[...] = jnp.zeros_like(ddt_ref)

        x = x_ref[...]
        dv = d_ref[...]
        lane = lax.broadcasted_iota(jnp.int32, x.shape, 1)
        ea = jnp.exp(al_ref[...])
        xs = x + dt_ref[...]
        g = -ea * _softplus(xs)
        dxs = jnp.where(lane < nh, dv * (-ea) * jax.nn.sigmoid(xs), 0.0)
        sg = jax.nn.sigmoid(x)
        dxb = jnp.where((lane >= nh) & (lane < 2 * nh), dv * sg * (1.0 - sg), 0.0)
        dx_ref[...] = (dxs + dxb).astype(BF16)
        dal_ref[...] += jnp.sum(jnp.where(lane < nh, dv * g, 0.0), axis=0, keepdims=True)
        ddt_ref[...] += jnp.sum(dxs, axis=0, keepdims=True)

    row = pl.BlockSpec((tm, HEAD), lambda i: (i, 0))
    vec = pl.BlockSpec((1, HEAD), lambda i: (0, 0))
    return pl.pallas_call(
        body, name=name, grid=(s // tm,), in_specs=[row, row, vec, vec], out_specs=[row, vec, vec],
        out_shape=[jax.ShapeDtypeStruct((s, HEAD), BF16), jax.ShapeDtypeStruct((1, HEAD), F32),
                   jax.ShapeDtypeStruct((1, HEAD), F32)],
        compiler_params=_params(("arbitrary",)))(dgb, pab, a_log, dt_bias)


def _tri_inv(a_low, eye_f):
    n = -a_low
    p = eye_f + n
    steps = int(math.log2(a_low.shape[-1])) - 1
    for _ in range(steps):
        n = _dot(n, n, BNN)
        p = p + _dot(p, n, BNN)
    return p


def _lane_col(x, lane, idx):
    return jnp.sum(jnp.where(lane == idx, x, 0.0), axis=1, keepdims=True)


def _head_cols(gbv, lo, nh):
    lane = lax.broadcasted_iota(jnp.int32, gbv.shape, 1)
    return jnp.stack([_lane_col(gbv, lane, lo + h) for h in range(nh)], axis=0)


def _gdn_chunk(q, k, v, g_col, beta_col, st):
    c = q.shape[1]
    r_i = lax.broadcasted_iota(jnp.int32, (c, c), 0)
    c_i = lax.broadcasted_iota(jnp.int32, (c, c), 1)
    incl = c_i <= r_i
    strict = c_i < r_i
    eye = c_i == r_i
    g_row = jnp.sum(jnp.where(eye, g_col, 0.0), axis=1, keepdims=True)
    gc_col = jnp.sum(jnp.where(incl, g_row, 0.0), axis=2, keepdims=True)
    gc_row = jnp.sum(jnp.where(eye, gc_col, 0.0), axis=1, keepdims=True)
    g_last = jnp.sum(g_col, axis=1, keepdims=True)
    decay = jnp.exp(jnp.where(incl, gc_col - gc_row, NEG))
    kk = _dot(k, k, BNT)
    a_low = jnp.where(strict, beta_col * kk * decay, 0.0)
    t_inv = _tri_inv(a_low, eye.astype(F32))
    e_g = jnp.exp(gc_col)
    bk = beta_col * e_g
    rhs = jnp.concatenate([v * beta_col, k * bk], axis=2)
    sol = _dot(t_inv, rhs, BNN)
    u, w = sol[:, :, :HEAD], sol[:, :, HEAD:]
    qk_raw = _dot(q, k, BNT)
    qk = qk_raw * decay
    q_dec = q * e_g
    e2 = jnp.exp(g_last - gc_col)
    k_dec = k * e2
    gl = jnp.exp(g_last)
    ws = _dot(jnp.concatenate([w, q_dec], axis=1), st, BNN)
    v_new = u - ws[:, :c]
    o = ws[:, c:] + _dot(qk, v_new, BNN)
    st_new = st * gl + _dot(k_dec, v_new, BTN)
    inter = dict(incl=incl, strict=strict, eye=eye, decay=decay, kk=kk, t_inv=t_inv, e_g=e_g, bk=bk, sol=sol, w=w,
                 qk_raw=qk_raw, qk=qk, q_dec=q_dec, e2=e2, k_dec=k_dec, gl=gl, v_new=v_new, c_i=c_i, r_i=r_i)
    return o, st_new, inter


def _gdn_fwd(qkv, gb, nh, name):
    s = qkv.shape[1]
    nc = s // CHUNK

    def body(q_ref, k_ref, v_ref, gb_ref, o_ref, st_ref, state):
        @pl.when(pl.program_id(0) == 0)
        def _():
            state[...] = jnp.zeros_like(state)

        gbv = gb_ref[...]
        st = state[...]
        st_ref[...] = st
        o, st_new, _ = _gdn_chunk(q_ref[...], k_ref[...], v_ref[...], _head_cols(gbv, 0, nh), _head_cols(gbv, nh, nh), st)
        o_ref[...] = o
        state[...] = st_new

    def qspec(part):
        return pl.BlockSpec((nh, CHUNK, HEAD), lambda n: (part, n, 0))

    return pl.pallas_call(
        body, name=name, grid=(nc,),
        in_specs=[qspec(0), qspec(1), qspec(2), pl.BlockSpec((CHUNK, HEAD), lambda n: (n, 0))],
        out_specs=[qspec(0), pl.BlockSpec((None, nh, HEAD, HEAD), lambda n: (n, 0, 0, 0))],
        out_shape=[jax.ShapeDtypeStruct((nh, s, HEAD), F32), jax.ShapeDtypeStruct((nc, nh, HEAD, HEAD), F32)],
        scratch_shapes=[pltpu.VMEM((nh, HEAD, HEAD), F32)],
        compiler_params=_params(("arbitrary",)))(qkv, qkv, qkv, gb)


def _gdn_bwd(qkv, gb, do, states, nh, name):
    s = qkv.shape[1]
    nc = s // CHUNK
    c = CHUNK

    def body(q_ref, k_ref, v_ref, gb_ref, do_ref, st_ref, dqkv_ref, dgb_ref, dstate):
        @pl.when(pl.program_id(0) == 0)
        def _():
            dstate[...] = jnp.zeros_like(dstate)

        gbv = gb_ref[...]
        lane = lax.broadcasted_iota(jnp.int32, gbv.shape, 1)
        q, k, v = q_ref[...], k_ref[...], v_ref[...]
        beta_col = _head_cols(gbv, nh, nh)
        st = st_ref[...]
        dst = dstate[...]
        dov = do_ref[...]
        _, _, it = _gdn_chunk(q, k, v, _head_cols(gbv, 0, nh), beta_col, st)
        incl, strict, eye, decay = it["incl"], it["strict"], it["eye"], it["decay"]
        dv_new = _dot(it["qk"], dov, BTN) + _dot(it["k_dec"], dst, BNN)
        d_qk = _dot(dov, it["v_new"], BNT)
        dd = _dot(jnp.concatenate([dov, -dv_new], axis=1), st, BNT)
        dq_dec, dw = dd[:, :c], dd[:, c:]
        dst_new = _dot(it["q_dec"], dov, BTN) + it["gl"] * dst - _dot(it["w"], dv_new, BTN)
        dgl = jnp.sum(jnp.sum(dst * st, axis=2, keepdims=True), axis=1, keepdims=True)
        dk_dec = _dot(it["v_new"], dst, BNT)
        dsol = jnp.concatenate([dv_new, dw], axis=2)
        drhs = _dot(it["t_inv"], dsol, BTN)
        d_a = jnp.where(strict, -_dot(drhs, it["sol"], BNT), 0.0)
        drhs_u, drhs_w = drhs[:, :, :HEAD], drhs[:, :, HEAD:]
        dvh = beta_col * drhs_u
        rw_k = jnp.sum(drhs_w * k, axis=2, keepdims=True)
        dbeta = jnp.sum(drhs_u * v, axis=2, keepdims=True) + it["e_g"] * rw_k
        dkh = it["bk"] * drhs_w
        dgc_col = it["bk"] * rw_k
        dkk = d_a * beta_col * decay
        dbeta = dbeta + jnp.sum(d_a * it["kk"] * decay, axis=2, keepdims=True)
        ddecay = d_a * beta_col * it["kk"]
        dkh = dkh + _dot(dkk, k, BNN) + _dot(dkk, k, BTN)
        dqk_raw = d_qk * decay
        ddecay = ddecay + d_qk * it["qk_raw"]
        dqh = _dot(dqk_raw, k, BNN)
        dkh = dkh + _dot(dqk_raw, q, BTN)
        ddm = jnp.where(incl, ddecay * decay, 0.0)
        dgc_col = dgc_col + jnp.sum(ddm, axis=2, keepdims=True)
        dgc_row = -jnp.sum(ddm, axis=1, keepdims=True)
        dqh = dqh + dq_dec * it["e_g"]
        dgc_col = dgc_col + jnp.sum(dq_dec * it["q_dec"], axis=2, keepdims=True)
        dkh = dkh + dk_dec * it["e2"]
        tmp = jnp.sum(dk_dec * it["k_dec"], axis=2, keepdims=True)
        dgc_col = dgc_col - tmp
        dg_last = jnp.sum(tmp, axis=1, keepdims=True) + dgl * it["gl"]
        dgc_tot_row = dgc_row + jnp.sum(jnp.where(eye, dgc_col, 0.0), axis=1, keepdims=True)
        dg_col = jnp.sum(jnp.where(it["c_i"] >= it["r_i"], dgc_tot_row, 0.0), axis=2, keepdims=True) + dg_last
        dqkv_ref[0] = dqh
        dqkv_ref[1] = dkh
        dqkv_ref[2] = dvh
        dstate[...] = dst_new
        dgb_acc = jnp.zeros(gbv.shape, F32)
        for h in range(nh):
            dgb_acc = jnp.where(lane == h, dg_col[h], jnp.where(lane == nh + h, dbeta[h], dgb_acc))
        dgb_ref[...] = dgb_acc

    def rev(part):
        return pl.BlockSpec((nh, CHUNK, HEAD), lambda n: (part, nc - 1 - n, 0))

    gspec = pl.BlockSpec((CHUNK, HEAD), lambda n: (nc - 1 - n, 0))
    dqkv, dgb = pl.pallas_call(
        body, name=name, grid=(nc,),
        in_specs=[rev(0), rev(1), rev(2), gspec, rev(0),
                  pl.BlockSpec((None, nh, HEAD, HEAD), lambda n: (nc - 1 - n, 0, 0, 0))],
        out_specs=[pl.BlockSpec((3, nh, CHUNK, HEAD), lambda n: (0, 0, nc - 1 - n, 0)), gspec],
        out_shape=[jax.ShapeDtypeStruct((3, nh, s, HEAD), F32), jax.ShapeDtypeStruct((s, HEAD), F32)],
        scratch_shapes=[pltpu.VMEM((nh, HEAD, HEAD), F32)],
        compiler_params=_params(("arbitrary",)))(qkv, qkv, qkv, gb, do, states)
    return dqkv.reshape(3 * nh, s, HEAD), dgb


SB_TQ_FWD = 1024
SB_TQ = 512


def _tri01(rel):
    j_i = lax.broadcasted_iota(jnp.int32, (2 * SBLK, SBLK), 0) & (SBLK - 1)
    s_i = lax.broadcasted_iota(jnp.int32, (2 * SBLK, SBLK), 1)
    return rel(j_i, s_i).astype(BF16)


SB_HP = 2


def _each(fn, *lists):
    return [fn(*xs) for xs in zip(*lists)]


def _sb_scores(qts, kblks, mask, csums, rhs01, one_dot):
    zs = _each(lambda qt, kb: _dot(qt, kb, NT), qts, kblks)
    es = _each(lambda z: jnp.exp(-jnp.abs(z)), zs)
    sps = _each(lambda z, e: jnp.maximum(z, 0.0) + jnp.log(1.0 + e), zs, es)
    lns = _each(lambda sp: -sp if mask is None else jnp.where(mask, -sp, 0.0), sps)
    sts = _each(lambda ln: _dot_hilo(ln, rhs01, one_dot), lns)
    wgts = _each(lambda z, sp, st, cs: jnp.exp((z - sp) + st + cs), zs, sps, sts, csums)
    if mask is not None:
        wgts = _each(lambda w: jnp.where(mask, w, 0.0), wgts)
    return zs, es, wgts, lns


def _band_mask(rows, j, row0):
    r_i = lax.broadcasted_iota(jnp.int32, (rows, SBLK), 0)
    c_i = lax.broadcasted_iota(jnp.int32, (rows, SBLK), 1)
    return (j * SBLK + c_i) < (row0 + r_i)


def _sb_fwd(q, k, v, name):
    s, d = q.shape
    nh = d // HEAD
    tq = min(SB_TQ_FWD, s)
    nb = tq // SBLK

    hp = SB_HP
    heads = [slice(h * HEAD, (h + 1) * HEAD) for h in range(hp)]

    def body(q_ref, k_ref, v_ref, o_ref, c_ref, acc, cs):
        qb = pl.program_id(1)
        lane = lax.broadcasted_iota(jnp.int32, (tq, HEAD), 1)
        after = _tri01(lambda j, t: j > t)
        acc[...] = jnp.zeros_like(acc)
        cs[...] = jnp.zeros_like(cs)
        c_ref[...] = jnp.zeros_like(c_ref)

        def process(rs, kb, mask):
            keys = pl.ds(pl.multiple_of(kb * SBLK, SBLK), SBLK)
            csums = [cs[h, rs, :] for h in range(hp)]
            _, _, wgts, lns = _sb_scores([q_ref[rs, hs] for hs in heads], [k_ref[keys, hs] for hs in heads], mask, csums, after, True)
            pvs = _each(lambda w, hs: _dot(w, v_ref[keys, hs]), wgts, heads)
            tots = _each(lambda ln: jnp.sum(ln, axis=1, keepdims=True), lns)
            for h, hs in enumerate(heads):
                acc[h, rs, :] += pvs[h]
                c_ref[rs, hs] = jnp.where(lane[rs, :] == kb, csums[h], c_ref[rs, hs])
                cs[h, rs, :] = csums[h] + tots[h]

        for j in reversed(range(nb)):
            process(slice(j * SBLK, tq), qb * nb + j, _band_mask(tq - j * SBLK, j, j * SBLK))

        def step(it, carry):
            process(slice(0, tq), qb * nb - 1 - it, None)
            return carry

        lax.fori_loop(0, qb * nb, step, 0)
        for h, hs in enumerate(heads):
            o_ref[:, hs] = acc[h].astype(BF16)

    qspec = pl.BlockSpec((tq, hp * HEAD), lambda h, i: (i, h))
    kspec = pl.BlockSpec((s, hp * HEAD), lambda h, i: (0, h))
    return pl.pallas_call(
        body, name=name, grid=(nh // hp, s // tq), in_specs=[qspec, kspec, kspec], out_specs=[qspec, qspec],
        out_shape=[jax.ShapeDtypeStruct((s, d), BF16), jax.ShapeDtypeStruct((s, d), F32)],
        scratch_shapes=[pltpu.VMEM((hp, tq, HEAD), F32), pltpu.VMEM((hp, tq, 1), F32)],
        compiler_params=_params(("parallel", "arbitrary")))(q, k, v)


def _sb_bwd(q, k, v, do, ctab, name):
    s, d = q.shape
    nh = d // HEAD
    tq = min(SB_TQ, s)
    nb = tq // SBLK

    hp = SB_HP
    heads = [slice(h * HEAD, (h + 1) * HEAD) for h in range(hp)]

    def body(q_ref, k_ref, v_ref, do_ref, c_ref, dq_ref, dk_ref, dv_ref, ps):
        qb = pl.program_id(1)

        @pl.when(qb == 0)
        def _():
            dk_ref[...] = jnp.zeros_like(dk_ref)
            dv_ref[...] = jnp.zeros_like(dv_ref)

        dq_ref[...] = jnp.zeros_like(dq_ref)
        ps[...] = jnp.zeros_like(ps)
        lane = lax.broadcasted_iota(jnp.int32, (tq, HEAD), 1)
        after = _tri01(lambda j, t: j > t)
        before = _tri01(lambda j, t: j < t)

        def process(rs, kb, mask):
            keys = pl.ds(pl.multiple_of(kb * SBLK, SBLK), SBLK)
            kblks = [k_ref[keys, hs] for hs in heads]
            qts = [q_ref[rs, hs] for hs in heads]
            dots = [do_ref[rs, hs] for hs in heads]
            csums = [_lane_col(c_ref[rs, hs], lane[rs, :], kb) for hs in heads]
            zs, es, wgts, _ = _sb_scores(qts, kblks, mask, csums, after, False)
            dlws = _each(lambda dt, hs, w: _dot(dt, v_ref[keys, hs], NT) * w, dots, heads, wgts)
            pts = _each(lambda dlw: _dot_hilo(dlw, before, False), dlws)
            pfxs = [ps[h, rs, :] for h in range(hp)]
            rs_ = _each(lambda e: 1.0 / (1.0 + e), es)
            sigs = _each(lambda z, e, r: jnp.where(z >= 0.0, r, e * r), zs, es, rs_)
            dzs = _each(lambda dlw, sig, pfx, pt: dlw * (1.0 - sig) - sig * (pfx + pt), dlws, sigs, pfxs, pts)
            tots = _each(lambda dlw: jnp.sum(dlw, axis=1, keepdims=True), dlws)
            if mask is not None:
                dzs = _each(lambda dz: jnp.where(mask, dz, 0.0), dzs)
            dqs = _each(lambda dz, kb_: _dot(dz, kb_), dzs, kblks)
            dks = _each(lambda dz, qt: _dot(dz, qt, TN), dzs, qts)
            dvs = _each(lambda w, dt: _dot(w, dt, TN), wgts, dots)
            for h, hs in enumerate(heads):
                dq_ref[rs, hs] += dqs[h]
                dk_ref[keys, hs] += dks[h]
                dv_ref[keys, hs] += dvs[h]
                ps[h, rs, :] = pfxs[h] + tots[h]

        def step(kb, carry):
            process(slice(0, tq), kb, None)
            return carry

        lax.fori_loop(0, qb * nb, step, 0)
        for j in range(nb):
            process(slice(j * SBLK, tq), qb * nb + j, _band_mask(tq - j * SBLK, j, j * SBLK))

    qspec = pl.BlockSpec((tq, hp * HEAD), lambda h, i: (i, h))
    kspec = pl.BlockSpec((s, hp * HEAD), lambda h, i: (0, h))
    sds = jax.ShapeDtypeStruct((s, d), F32)
    return pl.pallas_call(
        body, name=name, grid=(nh // hp, s // tq), in_specs=[qspec, kspec, kspec, qspec, qspec],
        out_specs=[qspec, kspec, kspec], out_shape=[sds, sds, sds],
        scratch_shapes=[pltpu.VMEM((hp, tq, 1), F32)],
        compiler_params=_params(("parallel", "arbitrary")))(q, k, v, do, ctab)


def _my_index():
    return 4 * lax.axis_index("x") + 2 * lax.axis_index("y") + lax.axis_index("c")


def _all_gather(x_shard, name):
    m_per, n = x_shard.shape

    def body(x_ref, out_ref, send_sems, recv_sems, local_sem):
        x, y, c = lax.axis_index("x"), lax.axis_index("y"), lax.axis_index("c")
        me, sibling = (x, y, c), (x, y, 1 - c)
        chips = [(1 - x, y), (x, 1 - y), (1 - x, 1 - y)]

        def rows(px, py, pc):
            return out_ref.at[pl.ds((4 * px + 2 * py + pc) * m_per, m_per), :]

        def copy(k, block, to, src=None):
            return pltpu.make_async_remote_copy(
                src_ref=rows(*block) if src is None else src, dst_ref=rows(*block),
                send_sem=send_sems.at[k], recv_sem=recv_sems.at[k], device_id=to, device_id_type=MESH)

        mine = pltpu.make_async_copy(x_ref, rows(*me), local_sem)
        mine.start()
        first = [copy(0, me, sibling, src=x_ref)]
        first += [copy(1 + j, me, (*chip, c), src=x_ref) for j, chip in enumerate(chips)]
        for cp in first:
            cp.start()
        passed = [copy(4 + j, (*chip, c), sibling) for j, chip in enumerate(chips)]
        for j, chip in enumerate(chips):
            copy(1 + j, (*chip, c), me).wait_recv()
            passed[j].start()
        copy(0, sibling, me).wait_recv()
        for j, chip in enumerate(chips):
            copy(4 + j, (*chip, 1 - c), me).wait_recv()
        for cp in first + passed:
            cp.wait_send()
        mine.wait()

    return pl.pallas_call(
        body, name=name, out_shape=jax.ShapeDtypeStruct((NDEV * m_per, n), x_shard.dtype),
        in_specs=[pl.BlockSpec(memory_space=pl.ANY)], out_specs=pl.BlockSpec(memory_space=pl.ANY),
        scratch_shapes=[pltpu.SemaphoreType.DMA((7,)), pltpu.SemaphoreType.DMA((7,)), pltpu.SemaphoreType.DMA],
    )(x_shard)


HBM_SPEC = pl.BlockSpec(memory_space=pltpu.HBM)
SEM_SPEC = pl.BlockSpec(memory_space=pltpu.SEMAPHORE)
ANY_SPEC = pl.BlockSpec(memory_space=pl.ANY)
EFFECT = pltpu.SideEffectType.DATAFLOW_SIDE_EFFECTING


def _exchange_copies(src_refs, land_refs, send_sems, recv_sems, self_sems, scatter):
    x, y, c = lax.axis_index("x"), lax.axis_index("y"), lax.axis_index("c")
    me = 4 * x + 2 * y + c
    remote, local = [], []
    for p, (src_ref, land_ref) in enumerate(zip(src_refs, land_refs)):
        rows = land_ref.shape[0] // NDEV

        def part(idx):
            return src_ref.at[pl.ds(idx * rows, rows), :] if scatter else src_ref

        slot = land_ref.at[pl.ds(me * rows, rows), :]
        for k in range(1, NDEV):
            px, py, pc = x ^ ((k >> 2) & 1), y ^ ((k >> 1) & 1), c ^ (k & 1)
            remote.append(pltpu.make_async_remote_copy(
                src_ref=part(4 * px + 2 * py + pc), dst_ref=slot, send_sem=send_sems.at[7 * p + k - 1],
                recv_sem=recv_sems.at[7 * p + k - 1], device_id=(px, py, pc), device_id_type=MESH))
        local.append(pltpu.make_async_copy(part(me), slot, self_sems.at[p]))
    return remote, local


def _send_start(srcs, scatter, after, name):
    n = len(srcs)
    lands = []
    for s in srcs:
        rows = s.shape[0] if scatter else NDEV * s.shape[0]
        lands.append(pltpu.with_memory_space_constraint(lax.empty((rows, s.shape[1]), s.dtype), pltpu.HBM))

    def body(*refs):
        src_refs, land_refs = refs[:n], refs[n:2 * n]
        send_sems, recv_sems, self_sems = refs[2 * n + 1:2 * n + 4]
        remote, local = _exchange_copies(src_refs, land_refs, send_sems, recv_sems, self_sems, scatter)
        for cp in remote + local:
            cp.start()
        refs[-1][...] = jnp.zeros_like(refs[-1])

    hbm = lambda a: pltpu.HBM(a.shape, a.dtype)
    out = pl.pallas_call(
        body, name=name,
        out_shape=(pltpu.SemaphoreType.DMA((7 * n,)), pltpu.SemaphoreType.DMA((7 * n,)), pltpu.SemaphoreType.DMA((n,)),
                   *[hbm(s) for s in srcs], *[hbm(a) for a in lands], jax.ShapeDtypeStruct((8, HEAD), F32)),
        in_specs=(HBM_SPEC,) * (2 * n) + (ANY_SPEC,),
        out_specs=(SEM_SPEC,) * 3 + (HBM_SPEC,) * (2 * n) + (pl.BlockSpec(memory_space=pltpu.VMEM),),
        input_output_aliases={i: 3 + i for i in range(2 * n)},
        compiler_params=pltpu.CompilerParams(has_side_effects=EFFECT),
    )(*[pltpu.with_memory_space_constraint(s, pltpu.HBM) for s in srcs], *lands, after)
    return dict(sems=out[:3], srcs=out[3:3 + n], lands=out[3 + n:3 + 2 * n], token=out[-1])


def _send_wait(started, scatter, after, name):
    srcs, lands = started["srcs"], started["lands"]
    n = len(srcs)

    def body(*refs):
        src_refs, land_refs = refs[:n], refs[n:2 * n]
        send_sems, recv_sems, self_sems = refs[2 * n:2 * n + 3]
        remote, local = _exchange_copies(src_refs, land_refs, send_sems, recv_sems, self_sems, scatter)
        for cp in remote:
            cp.wait_send()
            cp.wait_recv()
        for cp in local:
            cp.wait()

    hbm = lambda a: pltpu.HBM(a.shape, a.dtype)
    out = pl.pallas_call(
        body, name=name, out_shape=(*[hbm(s) for s in srcs], *[hbm(a) for a in lands]),
        in_specs=(HBM_SPEC,) * (2 * n) + (SEM_SPEC,) * 3 + (ANY_SPEC,), out_specs=(HBM_SPEC,) * (2 * n),
        input_output_aliases={i: i for i in range(2 * n)},
        compiler_params=pltpu.CompilerParams(has_side_effects=EFFECT),
    )(*srcs, *lands, *started["sems"], after)
    return out[n:]


def _sum_slots(xs, name, rows_out=None):
    _, r, c = xs[0].shape
    ro = rows_out or r
    tc = _pick(c, (128,))

    def body(*refs):
        o_ref = refs[-1]
        for l, x_ref in enumerate(refs[:-1]):
            acc = x_ref[0].astype(F32)
            for i in range(1, NDEV):
                acc = acc + x_ref[i].astype(F32)
            o_ref[l] = acc[:ro]

    return pl.pallas_call(
        body, name=name, grid=(c // tc,), in_specs=[pl.BlockSpec((NDEV, r, tc), lambda j: (0, 0, j))] * len(xs),
        out_specs=pl.BlockSpec((len(xs), ro, tc), lambda j: (0, 0, j)),
        out_shape=jax.ShapeDtypeStruct((len(xs), ro, c), F32), compiler_params=_params(("parallel",)))(*xs)


def _adamw(w, g, m, v, name):
    if w.ndim == 3:
        nl, r, c = w.shape
        tc = _pick(c, (256, 128))
        grid = (nl, c // tc)
        blk = pl.BlockSpec((None, r, tc), lambda i, j: (i, 0, j))
        sem = ("parallel", "parallel")
    else:
        r, c = w.shape
        tr = _pick(r, (256, 128, 64, 32, 16, 8))
        grid = (r // tr,)
        blk = pl.BlockSpec((tr, c), lambda i: (i, 0))
        sem = ("parallel",)
    c1 = 1.0 - B1 ** STEP
    c2 = 1.0 - B2 ** STEP

    def body(w_ref, g_ref, m_ref, v_ref, d_ref, nm_ref, nv_ref):
        gv = g_ref[...]
        nm = B1 * m_ref[...] + (1.0 - B1) * gv
        nv = B2 * v_ref[...] + (1.0 - B2) * (gv * gv)
        d_ref[...] = -LR * ((nm / c1) / (jnp.sqrt(nv / c2) + ADAM_EPS) + WD * w_ref[...])
        nm_ref[...] = nm
        nv_ref[...] = nv

    sds = jax.ShapeDtypeStruct(w.shape, F32)
    return pl.pallas_call(
        body, name=name, grid=grid, in_specs=[blk] * 4, out_specs=[blk] * 3, out_shape=[sds] * 3,
        compiler_params=_params(sem))(w, g, m, v)


def _pad_rows(a, mult):
    r = a.shape[0]
    pad = (-r) % mult
    return a if pad == 0 else jnp.pad(a, ((0, pad), (0, 0)))


def _pad_lanes(v, width=HEAD):
    return jnp.pad(v.reshape(1, -1), ((0, 0), (0, width - v.shape[-1])))


def kernel(x, p, ln_mix, ln_ffn, ln_ple, gdn_w_in, gdn_conv, gdn_a_log, gdn_dt_bias, gdn_norm, gdn_w_out, kv_norm, w_kv, k_norm, sb_w_q, sb_q_norm, sb_w_out, ffn_w_in, ffn_w_out, ple_w_proj, ple_w_gate, loss_target, m_ln_mix, m_ln_ffn, m_ln_ple, m_gdn_w_in, m_gdn_conv, m_gdn_a_log, m_gdn_dt_bias, m_gdn_norm, m_gdn_w_out, m_kv_norm, m_w_kv, m_k_norm, m_sb_w_q, m_sb_q_norm, m_sb_w_out, m_ffn_w_in, m_ffn_w_out, m_ple_w_proj, m_ple_w_gate, v_ln_mix, v_ln_ffn, v_ln_ple, v_gdn_w_in, v_gdn_conv, v_gdn_a_log, v_gdn_dt_bias, v_gdn_norm, v_gdn_w_out, v_kv_norm, v_w_kv, v_k_norm, v_sb_w_q, v_sb_q_norm, v_sb_w_out, v_ffn_w_in, v_ffn_w_out, v_ple_w_proj, v_ple_w_gate):
    s, d = x.shape[1], x.shape[2]
    nh = d // HEAD
    depth = ln_mix.shape[0]
    n_a = gdn_w_in.shape[0]
    n_b = sb_w_q.shape[0]
    me = _my_index()
    win_cols = gdn_w_in.shape[2]
    win_rows = 4 * d + 2 * nh

    def col_t(w):
        return jnp.transpose(w).astype(BF16)

    local = {}
    for l in range(n_a):
        local[("gdn_w_in", l)] = col_t(gdn_w_in[l])
        local[("gdn_w_out", l)] = gdn_w_out[l].astype(BF16)
    local[("w_kv", 0)] = col_t(w_kv)
    for j in range(n_b):
        local[("sb_w_q", j)] = sb_w_q[j].astype(BF16)
        local[("sb_w_out", j)] = sb_w_out[j].astype(BF16)
    for l in range(depth):
        local[("ffn_w_in", l)] = col_t(ffn_w_in[l])
        local[("ffn_w_out", l)] = ffn_w_out[l].astype(BF16)
        local[("ple_w_proj", l)] = col_t(ple_w_proj[l]).reshape(-1, d)
        local[("ple_w_gate", l)] = ple_w_gate[l].astype(BF16)
    local = {key: _pad_rows(a, 16) for key, a in local.items()}

    chunks = []
    for l in range(depth):
        mix = [("gdn_w_in", l), ("gdn_w_out", l)] if l < n_a else [("sb_w_q", l - n_a), ("sb_w_out", l - n_a)]
        rest = [("ffn_w_in", l), ("ffn_w_out", l), ("ple_w_proj", l), ("ple_w_gate", l)]
        if l == n_a - 1:
            rest.append(("w_kv", 0))
        chunks += [(f"a{l}", mix), (f"f{l}", rest)]
    chunk_keys = dict(chunks)

    conv_rows = n_a * gdn_conv.shape[1]
    conv_sh = _pad_rows(gdn_conv.reshape(conv_rows, -1), 8)
    conv_g = _all_gather(conv_sh, "comm_gather_conv")
    token = conv_g
    conv_g = conv_g.reshape(NDEV, conv_sh.shape[0], -1)
    conv_full = jnp.transpose(conv_g[:, :conv_rows, :], (1, 0, 2)).reshape(n_a, gdn_conv.shape[1], 3 * d)

    w_started = {}
    for name, keys in chunks:
        w_started[name] = _send_start([local[k] for k in keys], False, token, f"comm_wstart_{name}")
        token = w_started[name]["token"]

    full = {}

    def fetch(name, after):
        lands = _send_wait(w_started[name], False, after, f"comm_wwait_{name}")
        for key, land in zip(chunk_keys[name], lands):
            full[key] = land

    def whole(key, valid=None):
        a = full[key]
        if valid is not None:
            a = a.reshape(NDEV, -1, d)[:, :valid, :].reshape(-1, d)
        return a

    pd = p.shape[-1]
    w_in_t, w_ab_t, w_gout, w_q, w_sout, wf_t, w_fout, wp_t, w_pg = {}, {}, {}, {}, {}, {}, {}, {}, {}
    wkv_t = None

    h = x[0]
    sv = []
    kv_sv = None
    k_sh = v_sh = None
    for l in range(depth):
        t = {}
        t["h0"] = h
        if l == 0:
            hn = _rms_fwd(h, ln_mix[l], f"rms_mix_{l}")
        t["hn"] = hn
        fetch(f"a{l}", token if l == 0 else hn)
        if l < n_a:
            wt = whole(("gdn_w_in", l), win_cols)
            w_in_t[l] = wt[:4 * d]
            w_ab_t[l] = jnp.pad(wt[4 * d:], ((0, HEAD - 2 * nh), (0, 0)))
            w_gout[l] = whole(("gdn_w_out", l))
        else:
            w_q[l - n_a] = whole(("sb_w_q", l - n_a))
            w_sout[l - n_a] = whole(("sb_w_out", l - n_a))
        if l < n_a:
            proj = _mm(hn, w_in_t[l], "nt", f"gdn_proj_{l}")
            pab = _mm(hn, w_ab_t[l], "nt", f"gdn_proj_ab_{l}")
            qkv = _conv_fwd(proj, conv_full[l], d, f"gdn_conv_{l}")
            al, dtb = _pad_lanes(gdn_a_log[l]), _pad_lanes(gdn_dt_bias[l])
            gb = _gates_fwd(pab, al, dtb, nh, f"gdn_gates_{l}")
            o_raw, states = _gdn_fwd(qkv, gb, nh, f"gdn_rule_{l}")
            o2 = _headnorm_fwd(o_raw, gdn_norm[l], f"gdn_outnorm_{l}", gate=proj, gate_col0=3 * d, head_major=True)
            h, hn2 = _mm(o2, w_gout[l], "nn", f"gdn_out_{l}", res=h, norm_g=ln_ffn[l])
            t.update(proj=proj, pab=pab, qkv=qkv, gb=gb, o_raw=o_raw, states=states, o2=o2, al=al, dtb=dtb)
        else:
            j = l - n_a
            qpre = _mm(hn, w_q[j], "nn", f"sb_qproj_{j}")
            qn = _headnorm_fwd(qpre, sb_q_norm[j], f"sb_qnorm_{j}", scale=HEAD ** -0.5)
            o, ctab = _sb_fwd(qn, k_sh, v_sh, f"sb_attn_{j}")
            h, hn2 = _mm(o, w_sout[j], "nn", f"sb_out_{j}", res=h, norm_g=ln_ffn[l])
            t.update(qpre=qpre, qn=qn, o=o, ctab=ctab)
        t["h1"] = h
        fetch(f"f{l}", hn2)
        wf_t[l] = whole(("ffn_w_in", l))
        w_fout[l] = whole(("ffn_w_out", l))
        wp_t[l] = full[("ple_w_proj", l)].reshape(d, pd)
        w_pg[l] = whole(("ple_w_gate", l))
        if l == n_a - 1:
            wkv_t = whole(("w_kv", 0))
        act, gs, us = _swiglu_fwd(hn2, wf_t[l], f"ffn_in_{l}")
        h, hn3 = _mm(act, w_fout[l], "nn", f"ffn_out_{l}", res=h, norm_g=ln_ple[l])
        t.update(hn2=hn2, act=act, gs=gs, us=us, h2=h)
        gains = ([ln_mix[l + 1]] if l + 1 < depth else []) + ([kv_norm] if l == n_a - 1 else [])
        h, gpre, pp, *normed = _ple_fwd(h, hn3, p[l, 0], w_pg[l], wp_t[l], f"ple_{l}", norm_gs=gains)
        if l + 1 < depth:
            hn = normed[0]
        t.update(hn3=hn3, gpre=gpre, pp=pp)
        sv.append(t)
        if l == n_a - 1:
            kvn = normed[-1]
            kv = _mm(kvn, wkv_t, "nt", "kv_proj")
            k_sh = _headnorm_fwd(kv, k_norm, "k_norm", width=d)
            v_sh = kv[:, d:].astype(BF16)
            kv_sv = dict(h=h, kvn=kvn, kv=kv)

    dh, loss_vec = _loss_fwd_bwd(h, loss_target[0], "loss")
    loss = lax.psum(jnp.sum(loss_vec), ("x", "y", "c"))

    gw = {}
    small = {}
    g_started = {}

    def scatter_start(name):
        gparts = []
        for key in chunk_keys[name]:
            g = gw[key]
            g = g.reshape(NDEV, -1, d) if key[0] == "ple_w_proj" else g.reshape(NDEV, -1, g.shape[-1])
            padr = local[key].shape[0] - g.shape[1]
            if padr:
                g = jnp.pad(g, ((0, 0), (0, padr), (0, 0)))
            gparts.append(g.reshape(-1, d))
        g_started[name] = _send_start(gparts, True, gparts[0], f"comm_gstart_{name}")
        return g_started[name]["token"]

    dkv_sh = None
    for l in reversed(range(depth)):
        t = sv[l]
        if l == n_a - 1:
            dkv_k, dkn = _headnorm_bwd(dkv_sh[0], kv_sv["kv"], k_norm, "k_norm_bwd", dx_dtype=BF16)
            dkv = jnp.concatenate([dkv_k, dkv_sh[1].astype(BF16)], axis=1)
            gw[("w_kv", 0)] = _mm(dkv, kv_sv["kvn"], "tn", "kv_dw", out_dtype=BF16)
            dh, _, dg = _mm(dkv, wkv_t, "nn", "kv_dx", norm_bwd=(kv_sv["h"], kv_norm, dh))
            small["kv_norm"] = dg
            small["k_norm"] = dkn
        dgp, dpp = _ple_bwd(dh, t["gpre"], t["pp"], f"ple_bwd_{l}")
        gw[("ple_w_gate", l)] = _mm(t["hn3"], dgp, "tn", f"ple_dwg_{l}", out_dtype=BF16)
        gw[("ple_w_proj", l)] = _mm(dpp, p[l, 0], "tn", f"ple_dwp_{l}", out_dtype=BF16)
        dh, dhb, dg = _mm(dgp, w_pg[l], "nt", f"ple_dx_{l}", norm_bwd=(t["h2"], ln_ple[l], dh))
        small[("ln_ple", l)] = dg
        dgu = _swiglu_bwd(dhb, w_fout[l], t["gs"], t["us"], f"ffn_bwd_act_{l}")
        gw[("ffn_w_out", l)] = _mm(t["act"], dhb, "tn", f"ffn_dwo_{l}", out_dtype=BF16)
        gw[("ffn_w_in", l)] = _mm(dgu, t["hn2"], "tn", f"ffn_dwi_{l}", out_dtype=BF16)
        dh, dhb, dg = _mm(dgu, wf_t[l], "nn", f"ffn_dx_{l}", norm_bwd=(t["h1"], ln_ffn[l], dh),
                          after=scatter_start(f"f{l}"))
        small[("ln_ffn", l)] = dg
        if l < n_a:
            do2 = _mm(dhb, w_gout[l], "nt", f"gdn_out_dx_{l}")
            gw[("gdn_w_out", l)] = _mm(t["o2"], dhb, "tn", f"gdn_out_dw_{l}", out_dtype=BF16)
            do_raw, dgn, dgate = _headnorm_bwd(do2, t["o_raw"], gdn_norm[l], f"gdn_outnorm_bwd_{l}",
                                               gate=t["proj"], gate_col0=3 * d, head_major=True)
            small[("gdn_norm", l)] = dgn
            dqkv, dgb = _gdn_bwd(t["qkv"], t["gb"], do_raw, t["states"], nh, f"gdn_rule_bwd_{l}")
            dpab, dal, ddt = _gates_bwd(dgb, t["pab"], t["al"], t["dtb"], nh, f"gdn_gates_bwd_{l}")
            small[("gdn_a_log", l)] = dal
            small[("gdn_dt_bias", l)] = ddt
            dproj_qkv, dconv = _conv_bwd(dqkv, t["proj"], conv_full[l], d, f"gdn_conv_bwd_{l}")
            small[("gdn_conv", l)] = dconv
            dproj = jnp.concatenate([dproj_qkv, dgate], axis=1)
            dw_main = _mm(dproj, t["hn"], "tn", f"gdn_proj_dw_{l}", out_dtype=BF16)
            dw_ab = _mm(dpab, t["hn"], "tn", f"gdn_proj_ab_dw_{l}", out_dtype=BF16)
            gw[("gdn_w_in", l)] = jnp.concatenate([dw_main, dw_ab[:16]], axis=0)[:win_rows]
            dhn_ab = _mm(dpab, w_ab_t[l], "nn", f"gdn_proj_ab_dx_{l}")
            last = dict(a=dproj, b=w_in_t[l], mode="nn", name=f"gdn_proj_dx_{l}", res=dhn_ab)
        else:
            j = l - n_a
            do = _mm(dhb, w_sout[j], "nt", f"sb_out_dx_{j}", out_dtype=BF16)
            gw[("sb_w_out", j)] = _mm(t["o"], dhb, "tn", f"sb_out_dw_{j}", out_dtype=BF16)
            dq, dk, dv = _sb_bwd(t["qn"], k_sh, v_sh, do, t["ctab"], f"sb_attn_bwd_{j}")
            dkv_sh = (dk, dv) if dkv_sh is None else (dkv_sh[0] + dk, dkv_sh[1] + dv)
            dqpre, dqn = _headnorm_bwd(dq, t["qpre"], sb_q_norm[j], f"sb_qnorm_bwd_{j}", scale=HEAD ** -0.5, dx_dtype=BF16)
            small[("sb_q_norm", j)] = dqn
            gw[("sb_w_q", j)] = _mm(t["hn"], dqpre, "tn", f"sb_q_dw_{j}", out_dtype=BF16)
            last = dict(a=dqpre, b=w_q[j], mode="nt", name=f"sb_q_dx_{j}")
        dh, _, dg = _mm(**last, norm_bwd=(t["h0"], ln_mix[l], dh), after=scatter_start(f"a{l}"))
        small[("ln_mix", l)] = dg
    grad_x = dh[None]

    landed = {}
    for name, keys in reversed(chunks):
        lands = _send_wait(g_started[name], True, dh, f"comm_gwait_{name}")
        for key, land in zip(keys, lands):
            landed[key] = land.reshape(NDEV, -1, d)

    def summed(wname, count, rows_out=None):
        return _sum_slots([landed[(wname, i)] for i in range(count)], f"grad_sum_{wname}", rows_out)

    gt_gdn_w_in = summed("gdn_w_in", n_a, win_cols)
    gt_ffn_w_in = summed("ffn_w_in", depth)
    g_gdn_w_in = jnp.transpose(gt_gdn_w_in, (0, 2, 1))
    g_gdn_w_out = summed("gdn_w_out", n_a)
    g_w_kv = jnp.transpose(summed("w_kv", 1)[0])
    g_sb_w_q = summed("sb_w_q", n_b)
    g_sb_w_out = summed("sb_w_out", n_b)
    g_ffn_w_in = jnp.transpose(gt_ffn_w_in, (0, 2, 1))
    g_ffn_w_out = summed("ffn_w_out", depth)
    g_ple_w_proj = jnp.transpose(summed("ple_w_proj", depth).reshape(depth, -1, pd), (0, 2, 1))
    g_ple_w_gate = summed("ple_w_gate", depth)

    def vec_rows(v):
        return v.reshape(-1, HEAD)

    small_items = []
    for name_, cnt in (("ln_mix", depth), ("ln_ffn", depth), ("ln_ple", depth)):
        for l in range(cnt):
            small_items.append(((name_, l), vec_rows(small[(name_, l)])))
    for l in range(n_a):
        small_items.append((("gdn_conv", l), small[("gdn_conv", l)].reshape(-1, HEAD)))
        small_items.append((("gdn_a_log", l), small[("gdn_a_log", l)]))
        small_items.append((("gdn_dt_bias", l), small[("gdn_dt_bias", l)]))
        small_items.append((("gdn_norm", l), small[("gdn_norm", l)]))
    small_items.append(("kv_norm", vec_rows(small["kv_norm"])))
    small_items.append(("k_norm", small["k_norm"]))
    for j in range(n_b):
        small_items.append((("sb_q_norm", j), small[("sb_q_norm", j)]))
    spack = jnp.concatenate([_pad_rows(a, 8) for _, a in small_items], axis=0)
    sg = _all_gather(spack, "comm_gather_small").reshape(NDEV, spack.shape[0], HEAD)
    ssum = _sum_slots([sg], "small_sum")[0]
    sm = {}
    off = 0
    for key, a in small_items:
        sm[key] = ssum[off:off + a.shape[0]]
        off += a.shape[0] + (-a.shape[0]) % 8

    g_ln_mix = jnp.stack([sm[("ln_mix", l)].reshape(d) for l in range(depth)])
    g_ln_ffn = jnp.stack([sm[("ln_ffn", l)].reshape(d) for l in range(depth)])
    g_ln_ple = jnp.stack([sm[("ln_ple", l)].reshape(d) for l in range(depth)])
    conv_loc = gdn_conv.shape[2]
    g_conv_full = jnp.stack([sm[("gdn_conv", l)].reshape(gdn_conv.shape[1], 3 * d) for l in range(n_a)])
    g_gdn_conv = lax.dynamic_slice_in_dim(g_conv_full, me * conv_loc, conv_loc, axis=2)
    g_a_log = jnp.stack([sm[("gdn_a_log", l)][0, :nh] for l in range(n_a)])
    g_dt_bias = jnp.stack([sm[("gdn_dt_bias", l)][0, :nh] for l in range(n_a)])
    g_gdn_norm = jnp.stack([sm[("gdn_norm", l)][0] for l in range(n_a)])
    g_kv_norm = sm["kv_norm"].reshape(d)
    g_k_norm = sm["k_norm"][0]
    g_sb_q_norm = jnp.stack([sm[("sb_q_norm", j)][0] for j in range(n_b)])

    grads = [g_ln_mix, g_ln_ffn, g_ln_ple, g_gdn_w_in, g_gdn_conv, g_a_log, g_dt_bias, g_gdn_norm, g_gdn_w_out,
             g_kv_norm, g_w_kv, g_k_norm, g_sb_w_q, g_sb_q_norm, g_sb_w_out, g_ffn_w_in, g_ffn_w_out, g_ple_w_proj,
             g_ple_w_gate]
    weights = [ln_mix, ln_ffn, ln_ple, gdn_w_in, gdn_conv, gdn_a_log, gdn_dt_bias, gdn_norm, gdn_w_out, kv_norm, w_kv,
               k_norm, sb_w_q, sb_q_norm, sb_w_out, ffn_w_in, ffn_w_out, ple_w_proj, ple_w_gate]
    moms = [m_ln_mix, m_ln_ffn, m_ln_ple, m_gdn_w_in, m_gdn_conv, m_gdn_a_log, m_gdn_dt_bias, m_gdn_norm, m_gdn_w_out,
            m_kv_norm, m_w_kv, m_k_norm, m_sb_w_q, m_sb_q_norm, m_sb_w_out, m_ffn_w_in, m_ffn_w_out, m_ple_w_proj,
            m_ple_w_gate]
    vels = [v_ln_mix, v_ln_ffn, v_ln_ple, v_gdn_w_in, v_gdn_conv, v_gdn_a_log, v_gdn_dt_bias, v_gdn_norm, v_gdn_w_out,
            v_kv_norm, v_w_kv, v_k_norm, v_sb_w_q, v_sb_q_norm, v_sb_w_out, v_ffn_w_in, v_ffn_w_out, v_ple_w_proj,
            v_ple_w_gate]

    deltas, new_m, new_v = [], [], []
    small_idx = [i for i, w in enumerate(weights) if w.size < 8 * HEAD * 16]
    transposed = {3: gt_gdn_w_in, 15: gt_ffn_w_in}
    for i, (w, g, m, v) in enumerate(zip(weights, grads, moms, vels)):
        if i in small_idx:
            deltas.append(None), new_m.append(None), new_v.append(None)
            continue
        if i in transposed:
            tr = lambda a: jnp.transpose(a, (0, 2, 1))
            dl, nm, nv = _adamw(tr(w), transposed[i], tr(m), tr(v), f"adamw_{i}")
            deltas.append(tr(dl)), new_m.append(tr(nm)), new_v.append(tr(nv))
            continue
        shp = w.shape
        two = lambda a: a.reshape(-1, shp[-1])
        dl, nm, nv = _adamw(two(w), two(g), two(m), two(v), f"adamw_{i}")
        deltas.append(dl.reshape(shp)), new_m.append(nm.reshape(shp)), new_v.append(nv.reshape(shp))

    def flat_pack(arrs):
        flat = jnp.concatenate([a.reshape(-1) for a in arrs])
        pad = (-flat.shape[0]) % (8 * HEAD)
        return jnp.pad(flat, (0, pad)).reshape(-1, HEAD)

    sw = flat_pack([weights[i] for i in small_idx])
    sgr = flat_pack([grads[i] for i in small_idx])
    smo = flat_pack([moms[i] for i in small_idx])
    sve = flat_pack([vels[i] for i in small_idx])
    sdl, snm, snv = _adamw(sw, sgr, smo, sve, "adamw_small")
    off = 0
    for i in small_idx:
        n = weights[i].size
        shp = weights[i].shape
        deltas[i] = sdl.reshape(-1)[off:off + n].reshape(shp)
        new_m[i] = snm.reshape(-1)[off:off + n].reshape(shp)
        new_v[i] = snv.reshape(-1)[off:off + n].reshape(shp)
        off += n

    return (loss, grad_x, *grads, *deltas, *new_m, *new_v)
```

```python
import math

import jax
import jax.numpy as jnp
from jax import lax
from jax.experimental import pallas as pl
from jax.experimental.pallas import tpu as pltpu

F32 = jnp.float32
BF16 = jnp.bfloat16
NDEV = 8
HEAD = 128
CHUNK = 64
SBLK = 256
EPS = 1e-6
LR, B1, B2, ADAM_EPS, WD, STEP = 0.001, 0.9, 0.999, 1e-08, 0.01, 10
NEG = -1e30
MM_VMEM_BUDGET = 40 * 1024 * 1024

NN = (((1,), (0,)), ((), ()))
NT = (((1,), (1,)), ((), ()))
TN = (((0,), (0,)), ((), ()))
BNN = (((2,), (1,)), ((0,), (0,)))
BNT = (((2,), (2,)), ((0,), (0,)))
BTN = (((1,), (1,)), ((0,), (0,)))
MESH = pl.DeviceIdType.MESH


def _dot(a, b, dims=NN):
    return lax.dot_general(a.astype(BF16), b.astype(BF16), dims, preferred_element_type=F32)


def _dot_hilo(a, b01_twice, one_dot):
    hi = a.astype(BF16)
    lo = (a - hi.astype(F32)).astype(BF16)
    if one_dot:
        return lax.dot_general(jnp.concatenate([hi, lo], axis=1), b01_twice, NN, preferred_element_type=F32)
    b01 = b01_twice[:a.shape[1]]
    return (lax.dot_general(hi, b01, NN, preferred_element_type=F32)
            + lax.dot_general(lo, b01, NN, preferred_element_type=F32))


def _pick(dim, cands):
    for c in cands:
        if dim % c == 0:
            return c
    return dim


def _params(sem, vmem_mb=48):
    return pltpu.CompilerParams(dimension_semantics=sem, vmem_limit_bytes=vmem_mb * 1024 * 1024)


def _silu(x):
    return x * jax.nn.sigmoid(x)


def _silu_and_grad(x):
    s = jax.nn.sigmoid(x)
    xs = x * s
    return xs, s + xs * (1.0 - s)


def _mm(a, b, mode, name, out_dtype=F32, res=None, norm_g=None, norm_bwd=None, after=None):
    if mode == "nn":
        (m, k), n = a.shape, b.shape[1]
    elif mode == "nt":
        (m, k), n = a.shape, b.shape[0]
    else:
        (k, m), n = a.shape, b.shape[1]
    rows = norm_g is not None or norm_bwd is not None
    tn = n if rows else _pick(n, (512, 256, 128))
    tk = k if k <= 4096 else max(t for t in range(128, 4097, 128) if k % t == 0)
    nk = k // tk
    out_b = jnp.dtype(out_dtype).itemsize + (res.dtype.itemsize if res is not None else 0)
    out_b += 2 if norm_g is not None else 0
    out_b += 10 if norm_bwd is not None else 0
    for tm in [t for t in range(min(m, 2048), 127, -128) if m % t == 0] + [m]:
        need = 2 * (tm * tk * a.dtype.itemsize + tk * tn * b.dtype.itemsize + tm * tn * out_b) + 4 * tm * tn
        if need <= MM_VMEM_BUDGET:
            break
    dims = {"nn": NN, "nt": NT, "tn": TN}[mode]
    if mode == "tn":
        a_spec = pl.BlockSpec((tk, tm), lambda i, j, kk: (kk, i))
    else:
        a_spec = pl.BlockSpec((tm, tk), lambda i, j, kk: (i, kk))
    if mode == "nt":
        b_spec = pl.BlockSpec((tn, tk), lambda i, j, kk: (j, kk))
    else:
        b_spec = pl.BlockSpec((tk, tn), lambda i, j, kk: (kk, j))
    mn_spec = pl.BlockSpec((tm, tn), lambda i, j, kk: (i, j))
    vec_spec = pl.BlockSpec((1, tn), lambda i, j, kk: (0, j))
    has_res = res is not None
    n_in = 2 + has_res + (1 if norm_g is not None else 0) + (3 if norm_bwd is not None else 0) + (after is not None)

    def body(*refs):
        a_ref, b_ref = refs[:2]
        extra = list(refs[2:n_in])
        outs = refs[n_in:-1]
        acc = refs[-1]
        kk = pl.program_id(2)

        @pl.when(kk == 0)
        def _():
            acc[...] = jnp.zeros_like(acc)

        if norm_bwd is not None:
            @pl.when((kk == 0) & (pl.program_id(0) == 0))
            def _():
                outs[2][...] = jnp.zeros_like(outs[2])

        acc[...] += _dot(a_ref[...], b_ref[...], dims)

        @pl.when(kk == nk - 1)
        def _():
            r = acc[...]
            if has_res:
                r = r + extra.pop(0)[...].astype(F32)
            if norm_g is not None:
                outs[0][...] = r.astype(out_dtype)
                rs = lax.rsqrt(jnp.mean(r * r, axis=-1, keepdims=True) + EPS)
                outs[1][...] = (r * rs * extra.pop(0)[...]).astype(BF16)
            elif norm_bwd is not None:
                xv, gv, dres = extra.pop(0)[...], extra.pop(0)[...], extra.pop(0)[...]
                rs = lax.rsqrt(jnp.mean(xv * xv, axis=-1, keepdims=True) + EPS)
                gdy = r * gv
                dx = dres + rs * gdy - xv * (rs * rs * rs) * jnp.mean(xv * gdy, axis=-1, keepdims=True)
                outs[0][...] = dx
                outs[1][...] = dx.astype(BF16)
                outs[2][...] += jnp.sum(r * xv * rs, axis=0, keepdims=True)
            else:
                outs[0][...] = r.astype(out_dtype)

    ins = [a, b] + ([res] if has_res else [])
    in_specs = [a_spec, b_spec] + ([mn_spec] if has_res else [])
    out_specs, out_shape = [mn_spec], [jax.ShapeDtypeStruct((m, n), out_dtype)]
    sem = ("parallel", "parallel", "arbitrary")
    if norm_g is not None:
        ins.append(norm_g.reshape(1, n))
        in_specs.append(vec_spec)
        out_specs.append(mn_spec)
        out_shape.append(jax.ShapeDtypeStruct((m, n), BF16))
    if norm_bwd is not None:
        x, g, dres = norm_bwd
        ins += [x, g.reshape(1, n), dres]
        in_specs += [mn_spec, vec_spec, mn_spec]
        out_specs += [mn_spec, vec_spec]
        out_shape += [jax.ShapeDtypeStruct((m, n), BF16), jax.ShapeDtypeStruct((1, n), F32)]
        sem = ("arbitrary", "arbitrary", "arbitrary")
    if after is not None:
        ins.append(after)
        in_specs.append(pl.BlockSpec(memory_space=pl.ANY))
    out = pl.pallas_call(
        body, name=name, grid=(m // tm, n // tn, nk), in_specs=in_specs, out_specs=out_specs,
        out_shape=out_shape, scratch_shapes=[pltpu.VMEM((tm, tn), F32)],
        compiler_params=_params(sem))(*ins)
    return out[0] if len(out) == 1 else out


def _rms_fwd(h, g, name):
    s, d = h.shape
    tm = _pick(s, (512, 256, 128))

    def body(h_ref, g_ref, o_ref):
        x = h_ref[...]
        r = lax.rsqrt(jnp.mean(x * x, axis=-1, keepdims=True) + EPS)
        o_ref[...] = (x * r * g_ref[...]).astype(BF16)

    return pl.pallas_call(
        body, name=name, grid=(s // tm,),
        in_specs=[pl.BlockSpec((tm, d), lambda i: (i, 0)), pl.BlockSpec((1, d), lambda i: (0, 0))],
        out_specs=pl.BlockSpec((tm, d), lambda i: (i, 0)),
        out_shape=jax.ShapeDtypeStruct((s, d), BF16), compiler_params=_params(("parallel",)))(h, g.reshape(1, d))


def _headnorm_fwd(x, g, name, scale=1.0, gate=None, gate_col0=0, out_dtype=BF16, width=None, head_major=False):
    if head_major:
        s, d = x.shape[1], x.shape[0] * HEAD
    else:
        s, d = x.shape[0], (width or x.shape[1])
    nh = d // HEAD
    tm = _pick(s, (256, 128))
    has_gate = gate is not None
    gb = gate_col0 // d

    def body(*refs):
        if has_gate:
            x_ref, g_ref, gt_ref, o_ref = refs
        else:
            x_ref, g_ref, o_ref = refs
        gv = g_ref[...]
        for h in range(nh):
            sl = slice(h * HEAD, (h + 1) * HEAD)
            xv = (x_ref[h] if head_major else x_ref[:, sl]).astype(F32)
            r = lax.rsqrt(jnp.mean(xv * xv, axis=-1, keepdims=True) + EPS)
            y = xv * r * gv
            if scale != 1.0:
                y = y * scale
            if has_gate:
                y = y * _silu(gt_ref[:, sl])
            o_ref[:, sl] = y.astype(out_dtype)

    row = pl.BlockSpec((tm, d), lambda i: (i, 0))
    hm = pl.BlockSpec((nh, tm, HEAD), lambda i: (0, i, 0))
    ins = [x, g.reshape(1, HEAD)]
    in_specs = [hm if head_major else row, pl.BlockSpec((1, HEAD), lambda i: (0, 0))]
    if has_gate:
        ins.append(gate)
        in_specs.append(pl.BlockSpec((tm, d), lambda i: (i, gb)))
    return pl.pallas_call(
        body, name=name, grid=(s // tm,), in_specs=in_specs, out_specs=row,
        out_shape=jax.ShapeDtypeStruct((s, d), out_dtype), compiler_params=_params(("parallel",)))(*ins)


def _headnorm_bwd(dy, x, g, name, scale=1.0, gate=None, gate_col0=0, dx_dtype=F32, head_major=False):
    s, d = dy.shape
    nh = d // HEAD
    tm = _pick(s, (256, 128))
    has_gate = gate is not None
    gb = gate_col0 // d

    def body(*refs):
        if has_gate:
            dy_ref, x_ref, g_ref, gt_ref, dx_ref, dg_ref, dgt_ref = refs
        else:
            dy_ref, x_ref, g_ref, dx_ref, dg_ref = refs

        @pl.when(pl.program_id(0) == 0)
        def _():
            dg_ref[...] = jnp.zeros_like(dg_ref)

        gv = g_ref[...]
        dg_acc = jnp.zeros((1, HEAD), F32)
        for h in range(nh):
            sl = slice(h * HEAD, (h + 1) * HEAD)
            xv = (x_ref[h] if head_major else x_ref[:, sl]).astype(F32)
            dyv = dy_ref[:, sl].astype(F32)
            r = lax.rsqrt(jnp.mean(xv * xv, axis=-1, keepdims=True) + EPS)
            if has_gate:
                gt = gt_ref[:, sl]
                act, dact = _silu_and_grad(gt)
                dgt_ref[:, sl] = (dyv * (xv * r * gv) * dact).astype(dgt_ref.dtype)
                dn = dyv * act
            else:
                dn = dyv
            if scale != 1.0:
                dn = dn * scale
            gdn = dn * gv
            mean_t = jnp.mean(xv * gdn, axis=-1, keepdims=True)
            dxv = (r * gdn - xv * (r * r * r) * mean_t).astype(dx_dtype)
            if head_major:
                dx_ref[h] = dxv
            else:
                dx_ref[:, sl] = dxv
            dg_acc = dg_acc + jnp.sum(dn * xv * r, axis=0, keepdims=True)
        dg_ref[...] += dg_acc

    row = pl.BlockSpec((tm, d), lambda i: (i, 0))
    hm = pl.BlockSpec((nh, tm, HEAD), lambda i: (0, i, 0))
    vec = pl.BlockSpec((1, HEAD), lambda i: (0, 0))
    ins = [dy, x, g.reshape(1, HEAD)]
    in_specs = [row, hm if head_major else row, vec]
    out_specs = [hm if head_major else row, vec]
    dx_shape = (nh, s, HEAD) if head_major else (s, d)
    out_shape = [jax.ShapeDtypeStruct(dx_shape, dx_dtype), jax.ShapeDtypeStruct((1, HEAD), F32)]
    if has_gate:
        ins.append(gate)
        in_specs.append(pl.BlockSpec((tm, d), lambda i: (i, gb)))
        out_specs.append(pl.BlockSpec((tm, d), lambda i: (i, gb)))
        out_shape.append(jax.ShapeDtypeStruct((s, gate.shape[1]), BF16))
    return pl.pallas_call(
        body, name=name, grid=(s // tm,), in_specs=in_specs, out_specs=out_specs, out_shape=out_shape,
        compiler_params=_params(("arbitrary",)))(*ins)


def _kv_grad(dks, dvs, kv, g, name):
    s, d = dks[0].shape
    nh = d // HEAD
    tm = _pick(s, (256, 128))
    n = len(dks)

    def body(*refs):
        dk_refs, dv_refs = refs[:n], refs[n:2 * n]
        kv_ref, g_ref, o_ref, dg_ref = refs[2 * n:]

        @pl.when(pl.program_id(0) == 0)
        def _():
            dg_ref[...] = jnp.zeros_like(dg_ref)

        gv = g_ref[...]
        dg_acc = jnp.zeros((1, HEAD), F32)
        for h in range(nh):
            sl = slice(h * HEAD, (h + 1) * HEAD)
            xv = kv_ref[:, sl]
            dyv = sum(r[:, sl] for r in dk_refs)
            r = lax.rsqrt(jnp.mean(xv * xv, axis=-1, keepdims=True) + EPS)
            gdn = dyv * gv
            mean_t = jnp.mean(xv * gdn, axis=-1, keepdims=True)
            o_ref[:, sl] = (r * gdn - xv * (r * r * r) * mean_t).astype(BF16)
            dg_acc = dg_acc + jnp.sum(dyv * xv * r, axis=0, keepdims=True)
        o_ref[:, d:] = sum(r[...] for r in dv_refs).astype(BF16)
        dg_ref[...] += dg_acc

    row = pl.BlockSpec((tm, d), lambda i: (i, 0))
    vec = pl.BlockSpec((1, HEAD), lambda i: (0, 0))
    return pl.pallas_call(
        body, name=name, grid=(s // tm,), in_specs=[row] * (2 * n) + [row, vec],
        out_specs=[pl.BlockSpec((tm, 2 * d), lambda i: (i, 0)), vec],
        out_shape=[jax.ShapeDtypeStruct((s, 2 * d), BF16), jax.ShapeDtypeStruct((1, HEAD), F32)],
        compiler_params=_params(("arbitrary",)))(*dks, *dvs, kv, g.reshape(1, HEAD))


def _swiglu_fwd(hn, wf_t, name):
    s, d = hn.shape
    f = wf_t.shape[0] // 2
    tm = _pick(s, (1024, 512, 256, 128))
    tn = _pick(f, (512, 256, 128))
    nj = f // tn

    def body(a_ref, wg_ref, wu_ref, act_ref, g_ref, u_ref):
        a = a_ref[...]
        g = _dot(a, wg_ref[...], NT)
        u = _dot(a, wu_ref[...], NT)
        act_ref[...] = (_silu(g) * u).astype(BF16)
        g_ref[...] = g.astype(BF16)
        u_ref[...] = u.astype(BF16)

    o_spec = pl.BlockSpec((tm, tn), lambda i, j: (i, j))
    sds = jax.ShapeDtypeStruct((s, f), BF16)
    return pl.pallas_call(
        body, name=name, grid=(s // tm, nj),
        in_specs=[pl.BlockSpec((tm, d), lambda i, j: (i, 0)), pl.BlockSpec((tn, d), lambda i, j: (j, 0)),
                  pl.BlockSpec((tn, d), lambda i, j: (j + nj, 0))],
        out_specs=[o_spec, o_spec, o_spec], out_shape=[sds, sds, sds],
        compiler_params=_params(("parallel", "parallel")))(hn, wf_t, wf_t)


def _swiglu_bwd(dh, w_out, g, u, name):
    s, d = dh.shape
    f = w_out.shape[0]
    tm = _pick(s, (1024, 512, 256, 128))
    tn = _pick(f, (512, 256, 128))

    def body(dh_ref, w_ref, g_ref, u_ref, dgu_ref):
        j = pl.program_id(1)
        dact = _dot(dh_ref[...], w_ref[...], NT)
        gv = g_ref[...].astype(F32)
        uv = u_ref[...].astype(F32)
        sg, dsg = _silu_and_grad(gv)
        dgu_ref[:, pl.ds(pl.multiple_of(j * tn, HEAD), tn)] = (dact * uv * dsg).astype(BF16)
        dgu_ref[:, pl.ds(pl.multiple_of(f + j * tn, HEAD), tn)] = (dact * sg).astype(BF16)

    o_spec = pl.BlockSpec((tm, tn), lambda i, j: (i, j))
    return pl.pallas_call(
        body, name=name, grid=(s // tm, f // tn),
        in_specs=[pl.BlockSpec((tm, d), lambda i, j: (i, 0)), pl.BlockSpec((tn, d), lambda i, j: (j, 0)), o_spec, o_spec],
        out_specs=pl.BlockSpec((tm, 2 * f), lambda i, j: (i, 0)), out_shape=jax.ShapeDtypeStruct((s, 2 * f), BF16),
        compiler_params=_params(("parallel", "arbitrary")))(dh, w_out, g, u)


def _ple_fwd(h, hn, p, w_gate, wp_t, name, norm_gs=()):
    s, d = h.shape
    pd = p.shape[1]
    tm = _pick(s, (512, 256, 128))
    ng = len(norm_gs)

    def body(h_ref, hn_ref, p_ref, wg_ref, wp_ref, *rest):
        g_refs, (o_ref, gp_ref, pp_ref), n_refs = rest[:ng], rest[ng:ng + 3], rest[ng + 3:]
        gpre = _dot(hn_ref[...], wg_ref[...], NN)
        pp = _dot(p_ref[...], wp_ref[...], NT)
        o = h_ref[...] + pp * jax.nn.sigmoid(gpre)
        o_ref[...] = o
        gp_ref[...] = gpre.astype(BF16)
        pp_ref[...] = pp.astype(BF16)
        if ng:
            on = o * lax.rsqrt(jnp.mean(o * o, axis=-1, keepdims=True) + EPS)
            for g_ref, n_ref in zip(g_refs, n_refs):
                n_ref[...] = (on * g_ref[...]).astype(BF16)

    row = pl.BlockSpec((tm, d), lambda i: (i, 0))
    vec = pl.BlockSpec((1, d), lambda i: (0, 0))
    bf = jax.ShapeDtypeStruct((s, d), BF16)
    return pl.pallas_call(
        body, name=name, grid=(s // tm,),
        in_specs=[row, row, pl.BlockSpec((tm, pd), lambda i: (i, 0)), pl.BlockSpec((d, d), lambda i: (0, 0)),
                  pl.BlockSpec((d, pd), lambda i: (0, 0))] + [vec] * ng,
        out_specs=[row] * (3 + ng), out_shape=[jax.ShapeDtypeStruct((s, d), F32), bf, bf] + [bf] * ng,
        compiler_params=_params(("parallel",)))(h, hn, p, w_gate, wp_t, *[g.reshape(1, d) for g in norm_gs])


def _ple_bwd(dh, gpre, pp, name):
    s, d = dh.shape
    tm = _pick(s, (512, 256, 128))

    def body(dh_ref, gp_ref, pp_ref, dgp_ref, dpp_ref):
        dv = dh_ref[...]
        sig = jax.nn.sigmoid(gp_ref[...].astype(F32))
        ppv = pp_ref[...].astype(F32)
        dpp_ref[...] = (dv * sig).astype(BF16)
        dgp_ref[...] = (dv * ppv * sig * (1.0 - sig)).astype(BF16)

    row = pl.BlockSpec((tm, d), lambda i: (i, 0))
    sds = jax.ShapeDtypeStruct((s, d), BF16)
    return pl.pallas_call(
        body, name=name, grid=(s // tm,), in_specs=[row, row, row], out_specs=[row, row], out_shape=[sds, sds],
        compiler_params=_params(("parallel",)))(dh, gpre, pp)


def _loss_fwd_bwd(y, t, name):
    s, d = y.shape
    tm = _pick(s, (512, 256, 128))

    def body(y_ref, t_ref, dy_ref, l_ref):
        @pl.when(pl.program_id(0) == 0)
        def _():
            l_ref[...] = jnp.zeros_like(l_ref)

        e = y_ref[...] - t_ref[...]
        dy_ref[...] = e * (1.0 / d)
        l_ref[...] += jnp.sum(e * e, axis=0, keepdims=True) * (0.5 / d)

    row = pl.BlockSpec((tm, d), lambda i: (i, 0))
    vec = pl.BlockSpec((1, d), lambda i: (0, 0))
    return pl.pallas_call(
        body, name=name, grid=(s // tm,), in_specs=[row, row], out_specs=[row, vec],
        out_shape=[jax.ShapeDtypeStruct((s, d), F32), jax.ShapeDtypeStruct((1, d), F32)],
        compiler_params=_params(("arbitrary",)))(y, t)


PADR = 8
CONV_ROWS = 256


def _conv_fwd(proj, w_conv, d, name):
    s = proj.shape[0]
    nh = d // HEAD
    kw = w_conv.shape[0]
    qscale = HEAD ** -0.5

    tr = _pick(s, (CONV_ROWS,))

    def body(x_ref, w_ref, o_ref, xp):
        kind = pl.program_id(0) // nh
        xp[0:PADR, :] = jnp.zeros((PADR, HEAD), F32)
        xp[PADR:, :] = x_ref[...]
        taps = [w_ref[j:j + 1, :] for j in range(kw)]
        for r0 in range(0, s, tr):
            acc = jnp.zeros((tr, HEAD), F32)
            for j in range(kw):
                acc = acc + taps[j] * xp[r0 + PADR - (kw - 1) + j:r0 + PADR - (kw - 1) + j + tr, :]
            a = _silu(acc)
            r = lax.rsqrt(jnp.sum(a * a, axis=-1, keepdims=True) + EPS)
            fac = jnp.where(kind == 0, r * qscale, jnp.where(kind == 1, r, jnp.ones_like(r)))
            o_ref[r0:r0 + tr, :] = a * fac

    blk = pl.BlockSpec((s, HEAD), lambda c: (0, c))
    hm = pl.BlockSpec((None, s, HEAD), lambda c: (c, 0, 0))
    return pl.pallas_call(
        body, name=name, grid=(3 * nh,), in_specs=[blk, pl.BlockSpec((kw, HEAD), lambda c: (0, c))], out_specs=hm,
        out_shape=jax.ShapeDtypeStruct((3 * nh, s, HEAD), F32), scratch_shapes=[pltpu.VMEM((s + PADR, HEAD), F32)],
        compiler_params=_params(("parallel",)))(proj, w_conv)


def _conv_bwd(dqkv, proj, w_conv, d, name, dproj):
    s = proj.shape[0]
    nh = d // HEAD
    kw = w_conv.shape[0]
    qscale = HEAD ** -0.5

    tr = _pick(s, (CONV_ROWS,))

    def body(dy_ref, x_ref, w_ref, _, dx_ref, dw_ref, xp, dp):
        kind = pl.program_id(0) // nh
        xp[0:PADR, :] = jnp.zeros((PADR, HEAD), F32)
        xp[PADR:, :] = x_ref[...]
        dp[s:, :] = jnp.zeros((PADR, HEAD), F32)
        taps = [w_ref[j:j + 1, :] for j in range(kw)]
        sc = jnp.where(kind == 0, qscale, 1.0)
        dws = [jnp.zeros((1, HEAD), F32) for _ in range(kw)]
        for r0 in range(0, s, tr):
            acc = jnp.zeros((tr, HEAD), F32)
            for j in range(kw):
                acc = acc + taps[j] * xp[r0 + PADR - (kw - 1) + j:r0 + PADR - (kw - 1) + j + tr, :]
            a, da_dacc = _silu_and_grad(acc)
            dy = dy_ref[r0:r0 + tr, :]
            r = lax.rsqrt(jnp.sum(a * a, axis=-1, keepdims=True) + EPS)
            dyn = dy * sc
            da_norm = r * dyn - a * (r * r * r) * jnp.sum(a * dyn, axis=-1, keepdims=True)
            dacc = jnp.where(kind == 2, dy, da_norm) * da_dacc
            dp[r0:r0 + tr, :] = dacc
            for j in range(kw):
                sh = kw - 1 - j
                dws[j] = dws[j] + jnp.sum(dacc * xp[r0 + PADR - sh:r0 + PADR - sh + tr, :], axis=0, keepdims=True)
        for j in range(kw):
            dw_ref[j:j + 1, :] = dws[j]
        for r0 in range(0, s, tr):
            dx = jnp.zeros((tr, HEAD), F32)
            for j in range(kw):
                sh = kw - 1 - j
                dx = dx + taps[j] * dp[r0 + sh:r0 + sh + tr, :]
            dx_ref[r0:r0 + tr, :] = dx.astype(BF16)

    blk = pl.BlockSpec((s, HEAD), lambda c: (0, c))
    hm = pl.BlockSpec((None, s, HEAD), lambda c: (c, 0, 0))
    wblk = pl.BlockSpec((kw, HEAD), lambda c: (0, c))
    return pl.pallas_call(
        body, name=name, grid=(3 * nh,), in_specs=[hm, blk, wblk, pl.BlockSpec(memory_space=pl.ANY)],
        out_specs=[blk, wblk], input_output_aliases={3: 0},
        out_shape=[jax.ShapeDtypeStruct(dproj.shape, BF16), jax.ShapeDtypeStruct((kw, 3 * d), F32)],
        scratch_shapes=[pltpu.VMEM((s + PADR, HEAD), F32), pltpu.VMEM((s + PADR, HEAD), F32)],
        compiler_params=_params(("parallel",)))(dqkv, proj, w_conv, dproj)


def _softplus(x):
    return jnp.maximum(x, 0.0) + jnp.log(1.0 + jnp.exp(-jnp.abs(x)))


def _gates_fwd(pab, a_log, dt_bias, nh, name):
    s = pab.shape[0]
    tm = _pick(s, (512, 256, 128))

    def body(x_ref, al_ref, dt_ref, o_ref):
        x = x_ref[...]
        lane = lax.broadcasted_iota(jnp.int32, x.shape, 1)
        g = -jnp.exp(al_ref[...]) * _softplus(x + dt_ref[...])
        o_ref[...] = jnp.where(lane < nh, g, jnp.where(lane < 2 * nh, jax.nn.sigmoid(x), 0.0))

    row = pl.BlockSpec((tm, HEAD), lambda i: (i, 0))
    vec = pl.BlockSpec((1, HEAD), lambda i: (0, 0))
    return pl.pallas_call(
        body, name=name, grid=(s // tm,), in_specs=[row, vec, vec], out_specs=row,
        out_shape=jax.ShapeDtypeStruct((s, HEAD), F32), compiler_params=_params(("parallel",)))(pab, a_log, dt_bias)


def _gates_bwd(dgb, pab, a_log, dt_bias, nh, name):
    s = pab.shape[0]
    tm = _pick(s, (512, 256, 128))

    def body(d_ref, x_ref, al_ref, dt_ref, dx_ref, dal_ref, ddt_ref):
        @pl.when(pl.program_id(0) == 0)
        def _():
            dal_ref[...] = jnp.zeros_like(dal_ref)
            ddt_ref[...] = jnp.zeros_like(ddt_ref)

        x = x_ref[...]
        dv = d_ref[...]
        lane = lax.broadcasted_iota(jnp.int32, x.shape, 1)
        ea = jnp.exp(al_ref[...])
        xs = x + dt_ref[...]
        g = -ea * _softplus(xs)
        dxs = jnp.where(lane < nh, dv * (-ea) * jax.nn.sigmoid(xs), 0.0)
        sg = jax.nn.sigmoid(x)
        dxb = jnp.where((lane >= nh) & (lane < 2 * nh), dv * sg * (1.0 - sg), 0.0)
        dx_ref[...] = (dxs + dxb).astype(BF16)
        dal_ref[...] += jnp.sum(jnp.where(lane < nh, dv * g, 0.0), axis=0, keepdims=True)
        ddt_ref[...] += jnp.sum(dxs, axis=0, keepdims=True)

    row = pl.BlockSpec((tm, HEAD), lambda i: (i, 0))
    vec = pl.BlockSpec((1, HEAD), lambda i: (0, 0))
    return pl.pallas_call(
        body, name=name, grid=(s // tm,), in_specs=[row, row, vec, vec], out_specs=[row, vec, vec],
        out_shape=[jax.ShapeDtypeStruct((s, HEAD), BF16), jax.ShapeDtypeStruct((1, HEAD), F32),
                   jax.ShapeDtypeStruct((1, HEAD), F32)],
        compiler_params=_params(("arbitrary",)))(dgb, pab, a_log, dt_bias)


def _tri_inv(a_low, eye_f):
    n = -a_low
    p = eye_f + n
    steps = int(math.log2(a_low.shape[-1])) - 1
    for _ in range(steps):
        n = _dot(n, n, BNN)
        p = p + _dot(p, n, BNN)
    return p


def _lane_col(x, lane, idx):
    return jnp.sum(jnp.where(lane == idx, x, 0.0), axis=1, keepdims=True)


def _head_cols(gbv, lo, nh):
    lane = lax.broadcasted_iota(jnp.int32, gbv.shape, 1)
    return jnp.stack([_lane_col(gbv, lane, lo + h) for h in range(nh)], axis=0)


def _gdn_chunk(q, k, v, g_col, beta_col, st):
    c = q.shape[1]
    r_i = lax.broadcasted_iota(jnp.int32, (c, c), 0)
    c_i = lax.broadcasted_iota(jnp.int32, (c, c), 1)
    incl = c_i <= r_i
    strict = c_i < r_i
    eye = c_i == r_i
    g_row = jnp.sum(jnp.where(eye, g_col, 0.0), axis=1, keepdims=True)
    gc_col = jnp.sum(jnp.where(incl, g_row, 0.0), axis=2, keepdims=True)
    gc_row = jnp.sum(jnp.where(eye, gc_col, 0.0), axis=1, keepdims=True)
    g_last = jnp.sum(g_col, axis=1, keepdims=True)
    decay = jnp.exp(jnp.where(incl, gc_col - gc_row, NEG))
    kk = _dot(k, k, BNT)
    a_low = jnp.where(strict, beta_col * kk * decay, 0.0)
    t_inv = _tri_inv(a_low, eye.astype(F32))
    e_g = jnp.exp(gc_col)
    bk = beta_col * e_g
    rhs = jnp.concatenate([v * beta_col, k * bk], axis=2)
    sol = _dot(t_inv, rhs, BNN)
    u, w = sol[:, :, :HEAD], sol[:, :, HEAD:]
    qk_raw = _dot(q, k, BNT)
    qk = qk_raw * decay
    q_dec = q * e_g
    e2 = jnp.exp(g_last - gc_col)
    k_dec = k * e2
    gl = jnp.exp(g_last)
    ws = _dot(jnp.concatenate([w, q_dec], axis=1), st, BNN)
    v_new = u - ws[:, :c]
    o = ws[:, c:] + _dot(qk, v_new, BNN)
    st_new = st * gl + _dot(k_dec, v_new, BTN)
    inter = dict(incl=incl, strict=strict, eye=eye, decay=decay, kk=kk, t_inv=t_inv, e_g=e_g, bk=bk, sol=sol, w=w,
                 qk_raw=qk_raw, qk=qk, q_dec=q_dec, e2=e2, k_dec=k_dec, gl=gl, v_new=v_new, c_i=c_i, r_i=r_i)
    return o, st_new, inter


def _gdn_fwd(qkv, gb, nh, name):
    s = qkv.shape[1]
    nc = s // CHUNK

    def body(q_ref, k_ref, v_ref, gb_ref, o_ref, st_ref, state):
        @pl.when(pl.program_id(0) == 0)
        def _():
            state[...] = jnp.zeros_like(state)

        gbv = gb_ref[...]
        st = state[...]
        st_ref[...] = st
        o, st_new, _ = _gdn_chunk(q_ref[...], k_ref[...], v_ref[...], _head_cols(gbv, 0, nh), _head_cols(gbv, nh, nh), st)
        o_ref[...] = o
        state[...] = st_new

    def qspec(part):
        return pl.BlockSpec((nh, CHUNK, HEAD), lambda n: (part, n, 0))

    return pl.pallas_call(
        body, name=name, grid=(nc,),
        in_specs=[qspec(0), qspec(1), qspec(2), pl.BlockSpec((CHUNK, HEAD), lambda n: (n, 0))],
        out_specs=[qspec(0), pl.BlockSpec((None, nh, HEAD, HEAD), lambda n: (n, 0, 0, 0))],
        out_shape=[jax.ShapeDtypeStruct((nh, s, HEAD), F32), jax.ShapeDtypeStruct((nc, nh, HEAD, HEAD), F32)],
        scratch_shapes=[pltpu.VMEM((nh, HEAD, HEAD), F32)],
        compiler_params=_params(("arbitrary",)))(qkv, qkv, qkv, gb)


def _gdn_bwd(qkv, gb, do, states, nh, name):
    s = qkv.shape[1]
    nc = s // CHUNK
    c = CHUNK

    def body(q_ref, k_ref, v_ref, gb_ref, do_ref, st_ref, dqkv_ref, dgb_ref, dstate):
        @pl.when(pl.program_id(0) == 0)
        def _():
            dstate[...] = jnp.zeros_like(dstate)

        gbv = gb_ref[...]
        lane = lax.broadcasted_iota(jnp.int32, gbv.shape, 1)
        q, k, v = q_ref[...], k_ref[...], v_ref[...]
        beta_col = _head_cols(gbv, nh, nh)
        st = st_ref[...]
        dst = dstate[...]
        dov = do_ref[...]
        _, _, it = _gdn_chunk(q, k, v, _head_cols(gbv, 0, nh), beta_col, st)
        incl, strict, eye, decay = it["incl"], it["strict"], it["eye"], it["decay"]
        dv_new = _dot(it["qk"], dov, BTN) + _dot(it["k_dec"], dst, BNN)
        d_qk = _dot(dov, it["v_new"], BNT)
        dd = _dot(jnp.concatenate([dov, -dv_new], axis=1), st, BNT)
        dq_dec, dw = dd[:, :c], dd[:, c:]
        dst_new = _dot(it["q_dec"], dov, BTN) + it["gl"] * dst - _dot(it["w"], dv_new, BTN)
        dgl = jnp.sum(jnp.sum(dst * st, axis=2, keepdims=True), axis=1, keepdims=True)
        dk_dec = _dot(it["v_new"], dst, BNT)
        dsol = jnp.concatenate([dv_new, dw], axis=2)
        drhs = _dot(it["t_inv"], dsol, BTN)
        d_a = jnp.where(strict, -_dot(drhs, it["sol"], BNT), 0.0)
        drhs_u, drhs_w = drhs[:, :, :HEAD], drhs[:, :, HEAD:]
        dvh = beta_col * drhs_u
        rw_k = jnp.sum(drhs_w * k, axis=2, keepdims=True)
        dbeta = jnp.sum(drhs_u * v, axis=2, keepdims=True) + it["e_g"] * rw_k
        dkh = it["bk"] * drhs_w
        dgc_col = it["bk"] * rw_k
        dkk = d_a * beta_col * decay
        dbeta = dbeta + jnp.sum(d_a * it["kk"] * decay, axis=2, keepdims=True)
        ddecay = d_a * beta_col * it["kk"]
        dkh = dkh + _dot(dkk, k, BNN) + _dot(dkk, k, BTN)
        dqk_raw = d_qk * decay
        ddecay = ddecay + d_qk * it["qk_raw"]
        dqh = _dot(dqk_raw, k, BNN)
        dkh = dkh + _dot(dqk_raw, q, BTN)
        ddm = jnp.where(incl, ddecay * decay, 0.0)
        dgc_col = dgc_col + jnp.sum(ddm, axis=2, keepdims=True)
        dgc_row = -jnp.sum(ddm, axis=1, keepdims=True)
        dqh = dqh + dq_dec * it["e_g"]
        dgc_col = dgc_col + jnp.sum(dq_dec * it["q_dec"], axis=2, keepdims=True)
        dkh = dkh + dk_dec * it["e2"]
        tmp = jnp.sum(dk_dec * it["k_dec"], axis=2, keepdims=True)
        dgc_col = dgc_col - tmp
        dg_last = jnp.sum(tmp, axis=1, keepdims=True) + dgl * it["gl"]
        dgc_tot_row = dgc_row + jnp.sum(jnp.where(eye, dgc_col, 0.0), axis=1, keepdims=True)
        dg_col = jnp.sum(jnp.where(it["c_i"] >= it["r_i"], dgc_tot_row, 0.0), axis=2, keepdims=True) + dg_last
        dqkv_ref[0] = dqh
        dqkv_ref[1] = dkh
        dqkv_ref[2] = dvh
        dstate[...] = dst_new
        dgb_acc = jnp.zeros(gbv.shape, F32)
        for h in range(nh):
            dgb_acc = jnp.where(lane == h, dg_col[h], jnp.where(lane == nh + h, dbeta[h], dgb_acc))
        dgb_ref[...] = dgb_acc

    def rev(part):
        return pl.BlockSpec((nh, CHUNK, HEAD), lambda n: (part, nc - 1 - n, 0))

    gspec = pl.BlockSpec((CHUNK, HEAD), lambda n: (nc - 1 - n, 0))
    dqkv, dgb = pl.pallas_call(
        body, name=name, grid=(nc,),
        in_specs=[rev(0), rev(1), rev(2), gspec, rev(0),
                  pl.BlockSpec((None, nh, HEAD, HEAD), lambda n: (nc - 1 - n, 0, 0, 0))],
        out_specs=[pl.BlockSpec((3, nh, CHUNK, HEAD), lambda n: (0, 0, nc - 1 - n, 0)), gspec],
        out_shape=[jax.ShapeDtypeStruct((3, nh, s, HEAD), F32), jax.ShapeDtypeStruct((s, HEAD), F32)],
        scratch_shapes=[pltpu.VMEM((nh, HEAD, HEAD), F32)],
        compiler_params=_params(("arbitrary",)))(qkv, qkv, qkv, gb, do, states)
    return dqkv.reshape(3 * nh, s, HEAD), dgb


SB_TQ_FWD = 1024
SB_TQ = 512


def _tri01(rel):
    j_i = lax.broadcasted_iota(jnp.int32, (2 * SBLK, SBLK), 0) & (SBLK - 1)
    s_i = lax.broadcasted_iota(jnp.int32, (2 * SBLK, SBLK), 1)
    return rel(j_i, s_i).astype(BF16)


SB_HP = 2


def _each(fn, *lists):
    return [fn(*xs) for xs in zip(*lists)]


def _sb_scores(qts, kblks, mask, csums, rhs01, one_dot):
    zs = _each(lambda qt, kb: _dot(qt, kb, NT), qts, kblks)
    es = _each(lambda z: jnp.exp(-jnp.abs(z)), zs)
    sps = _each(lambda z, e: jnp.maximum(z, 0.0) + jnp.log(1.0 + e), zs, es)
    lns = _each(lambda sp: -sp if mask is None else jnp.where(mask, -sp, 0.0), sps)
    sts = _each(lambda ln: _dot_hilo(ln, rhs01, one_dot), lns)
    wgts = _each(lambda z, sp, st, cs: jnp.exp((z - sp) + st + cs), zs, sps, sts, csums)
    if mask is not None:
        wgts = _each(lambda w: jnp.where(mask, w, 0.0), wgts)
    return zs, es, wgts, lns


def _band_mask(rows, j, row0):
    r_i = lax.broadcasted_iota(jnp.int32, (rows, SBLK), 0)
    c_i = lax.broadcasted_iota(jnp.int32, (rows, SBLK), 1)
    return (j * SBLK + c_i) < (row0 + r_i)


def _sb_fwd(q, k, v, name):
    s, d = q.shape
    nh = d // HEAD
    tq = min(SB_TQ_FWD, s)
    nb = tq // SBLK

    hp = SB_HP
    heads = [slice(h * HEAD, (h + 1) * HEAD) for h in range(hp)]

    def body(q_ref, k_ref, v_ref, o_ref, c_ref, acc, cs):
        qb = pl.program_id(1)
        lane = lax.broadcasted_iota(jnp.int32, (tq, HEAD), 1)
        after = _tri01(lambda j, t: j > t)
        acc[...] = jnp.zeros_like(acc)
        cs[...] = jnp.zeros_like(cs)
        c_ref[...] = jnp.zeros_like(c_ref)

        def process(rs, kb, mask):
            keys = pl.ds(pl.multiple_of(kb * SBLK, SBLK), SBLK)
            csums = [cs[h, rs, :] for h in range(hp)]
            _, _, wgts, lns = _sb_scores([q_ref[rs, hs] for hs in heads], [k_ref[keys, hs] for hs in heads], mask, csums, after, True)
            pvs = _each(lambda w, hs: _dot(w, v_ref[keys, hs]), wgts, heads)
            tots = _each(lambda ln: jnp.sum(ln, axis=1, keepdims=True), lns)
            for h, hs in enumerate(heads):
                acc[h, rs, :] += pvs[h]
                c_ref[rs, hs] = jnp.where(lane[rs, :] == kb, csums[h], c_ref[rs, hs])
                cs[h, rs, :] = csums[h] + tots[h]

        for j in reversed(range(nb)):
            process(slice(j * SBLK, tq), qb * nb + j, _band_mask(tq - j * SBLK, j, j * SBLK))

        def step(it, carry):
            process(slice(0, tq), qb * nb - 1 - it, None)
            return carry

        lax.fori_loop(0, qb * nb, step, 0)
        for h, hs in enumerate(heads):
            o_ref[:, hs] = acc[h].astype(BF16)

    qspec = pl.BlockSpec((tq, hp * HEAD), lambda h, i: (i, h))
    kspec = pl.BlockSpec((s, hp * HEAD), lambda h, i: (0, h))
    return pl.pallas_call(
        body, name=name, grid=(nh // hp, s // tq), in_specs=[qspec, kspec, kspec], out_specs=[qspec, qspec],
        out_shape=[jax.ShapeDtypeStruct((s, d), BF16), jax.ShapeDtypeStruct((s, d), F32)],
        scratch_shapes=[pltpu.VMEM((hp, tq, HEAD), F32), pltpu.VMEM((hp, tq, 1), F32)],
        compiler_params=_params(("parallel", "arbitrary")))(q, k, v)


def _sb_bwd(q, k, v, do, ctab, name):
    s, d = q.shape
    nh = d // HEAD
    tq = min(SB_TQ, s)
    nb = tq // SBLK

    hp = SB_HP
    heads = [slice(h * HEAD, (h + 1) * HEAD) for h in range(hp)]

    def body(q_ref, k_ref, v_ref, do_ref, c_ref, dq_ref, dk_ref, dv_ref, ps):
        qb = pl.program_id(1)

        @pl.when(qb == 0)
        def _():
            dk_ref[...] = jnp.zeros_like(dk_ref)
            dv_ref[...] = jnp.zeros_like(dv_ref)

        dq_ref[...] = jnp.zeros_like(dq_ref)
        ps[...] = jnp.zeros_like(ps)
        lane = lax.broadcasted_iota(jnp.int32, (tq, HEAD), 1)
        after = _tri01(lambda j, t: j > t)
        before = _tri01(lambda j, t: j < t)

        def process(rs, kb, mask):
            keys = pl.ds(pl.multiple_of(kb * SBLK, SBLK), SBLK)
            kblks = [k_ref[keys, hs] for hs in heads]
            qts = [q_ref[rs, hs] for hs in heads]
            dots = [do_ref[rs, hs] for hs in heads]
            csums = [_lane_col(c_ref[rs, hs], lane[rs, :], kb) for hs in heads]
            zs, es, wgts, _ = _sb_scores(qts, kblks, mask, csums, after, False)
            dlws = _each(lambda dt, hs, w: _dot(dt, v_ref[keys, hs], NT) * w, dots, heads, wgts)
            pts = _each(lambda dlw: _dot_hilo(dlw, before, False), dlws)
            pfxs = [ps[h, rs, :] for h in range(hp)]
            rs_ = _each(lambda e: 1.0 / (1.0 + e), es)
            sigs = _each(lambda z, e, r: jnp.where(z >= 0.0, r, e * r), zs, es, rs_)
            dzs = _each(lambda dlw, sig, pfx, pt: dlw * (1.0 - sig) - sig * (pfx + pt), dlws, sigs, pfxs, pts)
            tots = _each(lambda dlw: jnp.sum(dlw, axis=1, keepdims=True), dlws)
            if mask is not None:
                dzs = _each(lambda dz: jnp.where(mask, dz, 0.0), dzs)
            dqs = _each(lambda dz, kb_: _dot(dz, kb_), dzs, kblks)
            dks = _each(lambda dz, qt: _dot(dz, qt, TN), dzs, qts)
            dvs = _each(lambda w, dt: _dot(w, dt, TN), wgts, dots)
            for h, hs in enumerate(heads):
                dq_ref[rs, hs] += dqs[h]
                dk_ref[keys, hs] += dks[h]
                dv_ref[keys, hs] += dvs[h]
                ps[h, rs, :] = pfxs[h] + tots[h]

        def step(kb, carry):
            process(slice(0, tq), kb, None)
            return carry

        lax.fori_loop(0, qb * nb, step, 0)
        for j in range(nb):
            process(slice(j * SBLK, tq), qb * nb + j, _band_mask(tq - j * SBLK, j, j * SBLK))

    qspec = pl.BlockSpec((tq, hp * HEAD), lambda h, i: (i, h))
    kspec = pl.BlockSpec((s, hp * HEAD), lambda h, i: (0, h))
    sds = jax.ShapeDtypeStruct((s, d), F32)
    return pl.pallas_call(
        body, name=name, grid=(nh // hp, s // tq), in_specs=[qspec, kspec, kspec, qspec, qspec],
        out_specs=[qspec, kspec, kspec], out_shape=[sds, sds, sds],
        scratch_shapes=[pltpu.VMEM((hp, tq, 1), F32)],
        compiler_params=_params(("parallel", "arbitrary")))(q, k, v, do, ctab)


def _my_index():
    return 4 * lax.axis_index("x") + 2 * lax.axis_index("y") + lax.axis_index("c")


def _all_gather(x_shard, name):
    m_per, n = x_shard.shape

    def body(x_ref, out_ref, send_sems, recv_sems, local_sem):
        x, y, c = lax.axis_index("x"), lax.axis_index("y"), lax.axis_index("c")
        me, sibling = (x, y, c), (x, y, 1 - c)
        chips = [(1 - x, y), (x, 1 - y), (1 - x, 1 - y)]

        def rows(px, py, pc):
            return out_ref.at[pl.ds((4 * px + 2 * py + pc) * m_per, m_per), :]

        def copy(k, block, to, src=None):
            return pltpu.make_async_remote_copy(
                src_ref=rows(*block) if src is None else src, dst_ref=rows(*block),
                send_sem=send_sems.at[k], recv_sem=recv_sems.at[k], device_id=to, device_id_type=MESH)

        mine = pltpu.make_async_copy(x_ref, rows(*me), local_sem)
        mine.start()
        first = [copy(0, me, sibling, src=x_ref)]
        first += [copy(1 + j, me, (*chip, c), src=x_ref) for j, chip in enumerate(chips)]
        for cp in first:
            cp.start()
        passed = [copy(4 + j, (*chip, c), sibling) for j, chip in enumerate(chips)]
        for j, chip in enumerate(chips):
            copy(1 + j, (*chip, c), me).wait_recv()
            passed[j].start()
        copy(0, sibling, me).wait_recv()
        for j, chip in enumerate(chips):
            copy(4 + j, (*chip, 1 - c), me).wait_recv()
        for cp in first + passed:
            cp.wait_send()
        mine.wait()

    return pl.pallas_call(
        body, name=name, out_shape=jax.ShapeDtypeStruct((NDEV * m_per, n), x_shard.dtype),
        in_specs=[pl.BlockSpec(memory_space=pl.ANY)], out_specs=pl.BlockSpec(memory_space=pl.ANY),
        scratch_shapes=[pltpu.SemaphoreType.DMA((7,)), pltpu.SemaphoreType.DMA((7,)), pltpu.SemaphoreType.DMA],
    )(x_shard)


HBM_SPEC = pl.BlockSpec(memory_space=pltpu.HBM)
SEM_SPEC = pl.BlockSpec(memory_space=pltpu.SEMAPHORE)
ANY_SPEC = pl.BlockSpec(memory_space=pl.ANY)
EFFECT = pltpu.SideEffectType.DATAFLOW_SIDE_EFFECTING


def _exchange_copies(src_refs, land_refs, send_sems, recv_sems, self_sems, scatter):
    x, y, c = lax.axis_index("x"), lax.axis_index("y"), lax.axis_index("c")
    me = 4 * x + 2 * y + c
    remote, local = [], []
    for p, (src_ref, land_ref) in enumerate(zip(src_refs, land_refs)):
        rows = land_ref.shape[0] // NDEV

        def part(idx):
            return src_ref.at[pl.ds(idx * rows, rows), :] if scatter else src_ref

        slot = land_ref.at[pl.ds(me * rows, rows), :]
        for k in range(1, NDEV):
            px, py, pc = x ^ ((k >> 2) & 1), y ^ ((k >> 1) & 1), c ^ (k & 1)
            remote.append(pltpu.make_async_remote_copy(
                src_ref=part(4 * px + 2 * py + pc), dst_ref=slot, send_sem=send_sems.at[7 * p + k - 1],
                recv_sem=recv_sems.at[7 * p + k - 1], device_id=(px, py, pc), device_id_type=MESH))
        local.append(pltpu.make_async_copy(part(me), slot, self_sems.at[p]))
    return remote, local


def _send_start(srcs, scatter, after, name):
    n = len(srcs)
    lands = []
    for s in srcs:
        rows = s.shape[0] if scatter else NDEV * s.shape[0]
        lands.append(pltpu.with_memory_space_constraint(lax.empty((rows, s.shape[1]), s.dtype), pltpu.HBM))

    def body(*refs):
        src_refs, land_refs = refs[:n], refs[n:2 * n]
        send_sems, recv_sems, self_sems = refs[2 * n + 1:2 * n + 4]
        remote, local = _exchange_copies(src_refs, land_refs, send_sems, recv_sems, self_sems, scatter)
        for cp in remote + local:
            cp.start()
        refs[-1][...] = jnp.zeros_like(refs[-1])

    hbm = lambda a: pltpu.HBM(a.shape, a.dtype)
    out = pl.pallas_call(
        body, name=name,
        out_shape=(pltpu.SemaphoreType.DMA((7 * n,)), pltpu.SemaphoreType.DMA((7 * n,)), pltpu.SemaphoreType.DMA((n,)),
                   *[hbm(s) for s in srcs], *[hbm(a) for a in lands], jax.ShapeDtypeStruct((8, HEAD), F32)),
        in_specs=(HBM_SPEC,) * (2 * n) + (ANY_SPEC,),
        out_specs=(SEM_SPEC,) * 3 + (HBM_SPEC,) * (2 * n) + (pl.BlockSpec(memory_space=pltpu.VMEM),),
        input_output_aliases={i: 3 + i for i in range(2 * n)},
        compiler_params=pltpu.CompilerParams(has_side_effects=EFFECT),
    )(*[pltpu.with_memory_space_constraint(s, pltpu.HBM) for s in srcs], *lands, after)
    return dict(sems=out[:3], srcs=out[3:3 + n], lands=out[3 + n:3 + 2 * n], token=out[-1])


def _send_wait(started, scatter, after, name):
    srcs, lands = started["srcs"], started["lands"]
    n = len(srcs)

    def body(*refs):
        src_refs, land_refs = refs[:n], refs[n:2 * n]
        send_sems, recv_sems, self_sems = refs[2 * n:2 * n + 3]
        remote, local = _exchange_copies(src_refs, land_refs, send_sems, recv_sems, self_sems, scatter)
        for cp in remote:
            cp.wait_send()
            cp.wait_recv()
        for cp in local:
            cp.wait()

    hbm = lambda a: pltpu.HBM(a.shape, a.dtype)
    out = pl.pallas_call(
        body, name=name, out_shape=(*[hbm(s) for s in srcs], *[hbm(a) for a in lands]),
        in_specs=(HBM_SPEC,) * (2 * n) + (SEM_SPEC,) * 3 + (ANY_SPEC,), out_specs=(HBM_SPEC,) * (2 * n),
        input_output_aliases={i: i for i in range(2 * n)},
        compiler_params=pltpu.CompilerParams(has_side_effects=EFFECT),
    )(*srcs, *lands, *started["sems"], after)
    return out[n:]


def _sum_slots(xs, name, rows_out=None):
    _, r, c = xs[0].shape
    ro = rows_out or r
    tc = _pick(c, (128,))

    def body(*refs):
        o_ref = refs[-1]
        for l, x_ref in enumerate(refs[:-1]):
            acc = x_ref[0].astype(F32)
            for i in range(1, NDEV):
                acc = acc + x_ref[i].astype(F32)
            o_ref[l] = acc[:ro]

    return pl.pallas_call(
        body, name=name, grid=(c // tc,), in_specs=[pl.BlockSpec((NDEV, r, tc), lambda j: (0, 0, j))] * len(xs),
        out_specs=pl.BlockSpec((len(xs), ro, tc), lambda j: (0, 0, j)),
        out_shape=jax.ShapeDtypeStruct((len(xs), ro, c), F32), compiler_params=_params(("parallel",)))(*xs)


def _adamw(w, g, m, v, name):
    if w.ndim == 3:
        nl, r, c = w.shape
        tc = _pick(c, (256, 128))
        grid = (nl, c // tc)
        blk = pl.BlockSpec((None, r, tc), lambda i, j: (i, 0, j))
        sem = ("parallel", "parallel")
    else:
        r, c = w.shape
        tr = _pick(r, (256, 128, 64, 32, 16, 8))
        grid = (r // tr,)
        blk = pl.BlockSpec((tr, c), lambda i: (i, 0))
        sem = ("parallel",)
    c1 = 1.0 - B1 ** STEP
    c2 = 1.0 - B2 ** STEP

    def body(w_ref, g_ref, m_ref, v_ref, d_ref, nm_ref, nv_ref):
        gv = g_ref[...]
        nm = B1 * m_ref[...] + (1.0 - B1) * gv
        nv = B2 * v_ref[...] + (1.0 - B2) * (gv * gv)
        d_ref[...] = -LR * ((nm / c1) / (jnp.sqrt(nv / c2) + ADAM_EPS) + WD * w_ref[...])
        nm_ref[...] = nm
        nv_ref[...] = nv

    sds = jax.ShapeDtypeStruct(w.shape, F32)
    return pl.pallas_call(
        body, name=name, grid=grid, in_specs=[blk] * 4, out_specs=[blk] * 3, out_shape=[sds] * 3,
        compiler_params=_params(sem))(w, g, m, v)


def _pad_rows(a, mult):
    r = a.shape[0]
    pad = (-r) % mult
    return a if pad == 0 else jnp.pad(a, ((0, pad), (0, 0)))


def _pad_lanes(v, width=HEAD):
    return jnp.pad(v.reshape(1, -1), ((0, 0), (0, width - v.shape[-1])))


def kernel(x, p, ln_mix, ln_ffn, ln_ple, gdn_w_in, gdn_conv, gdn_a_log, gdn_dt_bias, gdn_norm, gdn_w_out, kv_norm, w_kv, k_norm, sb_w_q, sb_q_norm, sb_w_out, ffn_w_in, ffn_w_out, ple_w_proj, ple_w_gate, loss_target, m_ln_mix, m_ln_ffn, m_ln_ple, m_gdn_w_in, m_gdn_conv, m_gdn_a_log, m_gdn_dt_bias, m_gdn_norm, m_gdn_w_out, m_kv_norm, m_w_kv, m_k_norm, m_sb_w_q, m_sb_q_norm, m_sb_w_out, m_ffn_w_in, m_ffn_w_out, m_ple_w_proj, m_ple_w_gate, v_ln_mix, v_ln_ffn, v_ln_ple, v_gdn_w_in, v_gdn_conv, v_gdn_a_log, v_gdn_dt_bias, v_gdn_norm, v_gdn_w_out, v_kv_norm, v_w_kv, v_k_norm, v_sb_w_q, v_sb_q_norm, v_sb_w_out, v_ffn_w_in, v_ffn_w_out, v_ple_w_proj, v_ple_w_gate):
    s, d = x.shape[1], x.shape[2]
    nh = d // HEAD
    depth = ln_mix.shape[0]
    n_a = gdn_w_in.shape[0]
    n_b = sb_w_q.shape[0]
    me = _my_index()
    win_cols = gdn_w_in.shape[2]
    win_rows = 4 * d + 2 * nh

    def col_t(w):
        return jnp.transpose(w).astype(BF16)

    local = {}
    for l in range(n_a):
        local[("gdn_w_in", l)] = col_t(gdn_w_in[l])
        local[("gdn_w_out", l)] = gdn_w_out[l].astype(BF16)
    local[("w_kv", 0)] = col_t(w_kv)
    for j in range(n_b):
        local[("sb_w_q", j)] = sb_w_q[j].astype(BF16)
        local[("sb_w_out", j)] = sb_w_out[j].astype(BF16)
    for l in range(depth):
        local[("ffn_w_in", l)] = col_t(ffn_w_in[l])
        local[("ffn_w_out", l)] = ffn_w_out[l].astype(BF16)
        local[("ple_w_proj", l)] = col_t(ple_w_proj[l]).reshape(-1, d)
        local[("ple_w_gate", l)] = ple_w_gate[l].astype(BF16)
    local = {key: _pad_rows(a, 16) for key, a in local.items()}

    chunks = []
    for l in range(depth):
        mix = [("gdn_w_in", l), ("gdn_w_out", l)] if l < n_a else [("sb_w_q", l - n_a), ("sb_w_out", l - n_a)]
        rest = [("ffn_w_in", l), ("ffn_w_out", l), ("ple_w_proj", l), ("ple_w_gate", l)]
        if l == n_a - 1:
            rest.append(("w_kv", 0))
        chunks += [(f"a{l}", mix), (f"f{l}", rest)]
    chunk_keys = dict(chunks)

    conv_rows = n_a * gdn_conv.shape[1]
    conv_sh = _pad_rows(gdn_conv.reshape(conv_rows, -1), 8)
    conv_g = _all_gather(conv_sh, "comm_gather_conv")
    token = conv_g
    conv_g = conv_g.reshape(NDEV, conv_sh.shape[0], -1)
    conv_full = jnp.transpose(conv_g[:, :conv_rows, :], (1, 0, 2)).reshape(n_a, gdn_conv.shape[1], 3 * d)

    w_started = {}
    for name, keys in chunks:
        w_started[name] = _send_start([local[k] for k in keys], False, token, f"comm_wstart_{name}")
        token = w_started[name]["token"]

    full = {}

    def fetch(name, after):
        lands = _send_wait(w_started[name], False, after, f"comm_wwait_{name}")
        for key, land in zip(chunk_keys[name], lands):
            full[key] = land

    def whole(key, valid=None):
        a = full[key]
        if valid is not None:
            a = a.reshape(NDEV, -1, d)[:, :valid, :].reshape(-1, d)
        return a

    pd = p.shape[-1]
    w_in_t, w_ab_t, w_gout, w_q, w_sout, wf_t, w_fout, wp_t, w_pg = {}, {}, {}, {}, {}, {}, {}, {}, {}
    wkv_t = None

    h = x[0]
    sv = []
    kv_sv = None
    k_sh = v_sh = None
    for l in range(depth):
        t = {}
        t["h0"] = h
        if l == 0:
            hn = _rms_fwd(h, ln_mix[l], f"rms_mix_{l}")
        t["hn"] = hn
        fetch(f"a{l}", token if l == 0 else hn)
        if l < n_a:
            wt = whole(("gdn_w_in", l), win_cols)
            w_in_t[l] = wt[:4 * d]
            w_ab_t[l] = jnp.pad(wt[4 * d:], ((0, HEAD - 2 * nh), (0, 0)))
            w_gout[l] = whole(("gdn_w_out", l))
        else:
            w_q[l - n_a] = whole(("sb_w_q", l - n_a))
            w_sout[l - n_a] = whole(("sb_w_out", l - n_a))
        if l < n_a:
            proj = _mm(hn, w_in_t[l], "nt", f"gdn_proj_{l}")
            pab = _mm(hn, w_ab_t[l], "nt", f"gdn_proj_ab_{l}")
            qkv = _conv_fwd(proj, conv_full[l], d, f"gdn_conv_{l}")
            al, dtb = _pad_lanes(gdn_a_log[l]), _pad_lanes(gdn_dt_bias[l])
            gb = _gates_fwd(pab, al, dtb, nh, f"gdn_gates_{l}")
            o_raw, states = _gdn_fwd(qkv, gb, nh, f"gdn_rule_{l}")
            o2 = _headnorm_fwd(o_raw, gdn_norm[l], f"gdn_outnorm_{l}", gate=proj, gate_col0=3 * d, head_major=True)
            h, hn2 = _mm(o2, w_gout[l], "nn", f"gdn_out_{l}", res=h, norm_g=ln_ffn[l])
            t.update(proj=proj, pab=pab, qkv=qkv, gb=gb, o_raw=o_raw, states=states, o2=o2, al=al, dtb=dtb)
        else:
            j = l - n_a
            qpre = _mm(hn, w_q[j], "nn", f"sb_qproj_{j}")
            qn = _headnorm_fwd(qpre, sb_q_norm[j], f"sb_qnorm_{j}", scale=HEAD ** -0.5)
            o, ctab = _sb_fwd(qn, k_sh, v_sh, f"sb_attn_{j}")
            h, hn2 = _mm(o, w_sout[j], "nn", f"sb_out_{j}", res=h, norm_g=ln_ffn[l])
            t.update(qpre=qpre, qn=qn, o=o, ctab=ctab)
        t["h1"] = h
        fetch(f"f{l}", hn2)
        wf_t[l] = whole(("ffn_w_in", l))
        w_fout[l] = whole(("ffn_w_out", l))
        wp_t[l] = full[("ple_w_proj", l)].reshape(d, pd)
        w_pg[l] = whole(("ple_w_gate", l))
        if l == n_a - 1:
            wkv_t = whole(("w_kv", 0))
        act, gs, us = _swiglu_fwd(hn2, wf_t[l], f"ffn_in_{l}")
        h, hn3 = _mm(act, w_fout[l], "nn", f"ffn_out_{l}", res=h, norm_g=ln_ple[l])
        t.update(hn2=hn2, act=act, gs=gs, us=us, h2=h)
        gains = ([ln_mix[l + 1]] if l + 1 < depth else []) + ([kv_norm] if l == n_a - 1 else [])
        h, gpre, pp, *normed = _ple_fwd(h, hn3, p[l, 0], w_pg[l], wp_t[l], f"ple_{l}", norm_gs=gains)
        if l + 1 < depth:
            hn = normed[0]
        t.update(hn3=hn3, gpre=gpre, pp=pp)
        sv.append(t)
        if l == n_a - 1:
            kvn = normed[-1]
            kv = _mm(kvn, wkv_t, "nt", "kv_proj")
            k_sh = _headnorm_fwd(kv, k_norm, "k_norm", width=d)
            v_sh = kv[:, d:].astype(BF16)
            kv_sv = dict(h=h, kvn=kvn, kv=kv)

    dh, loss_vec = _loss_fwd_bwd(h, loss_target[0], "loss")
    loss = lax.psum(jnp.sum(loss_vec), ("x", "y", "c"))

    gw = {}
    small = {}
    g_started = {}

    def scatter_start(name):
        gparts = []
        for key in chunk_keys[name]:
            g = gw[key]
            g = g.reshape(NDEV, -1, d) if key[0] == "ple_w_proj" else g.reshape(NDEV, -1, g.shape[-1])
            padr = local[key].shape[0] - g.shape[1]
            if padr:
                g = jnp.pad(g, ((0, 0), (0, padr), (0, 0)))
            gparts.append(g.reshape(-1, d))
        g_started[name] = _send_start(gparts, True, gparts[0], f"comm_gstart_{name}")
        return g_started[name]["token"]

    dks, dvs = [], []
    for l in reversed(range(depth)):
        t = sv[l]
        if l == n_a - 1:
            dkv, dkn = _kv_grad(dks, dvs, kv_sv["kv"], k_norm, "k_norm_bwd")
            gw[("w_kv", 0)] = _mm(dkv, kv_sv["kvn"], "tn", "kv_dw", out_dtype=BF16)
            dh, _, dg = _mm(dkv, wkv_t, "nn", "kv_dx", norm_bwd=(kv_sv["h"], kv_norm, dh))
            small["kv_norm"] = dg
            small["k_norm"] = dkn
        dgp, dpp = _ple_bwd(dh, t["gpre"], t["pp"], f"ple_bwd_{l}")
        gw[("ple_w_gate", l)] = _mm(t["hn3"], dgp, "tn", f"ple_dwg_{l}", out_dtype=BF16)
        gw[("ple_w_proj", l)] = _mm(dpp, p[l, 0], "tn", f"ple_dwp_{l}", out_dtype=BF16)
        dh, dhb, dg = _mm(dgp, w_pg[l], "nt", f"ple_dx_{l}", norm_bwd=(t["h2"], ln_ple[l], dh))
        small[("ln_ple", l)] = dg
        dgu = _swiglu_bwd(dhb, w_fout[l], t["gs"], t["us"], f"ffn_bwd_act_{l}")
        gw[("ffn_w_out", l)] = _mm(t["act"], dhb, "tn", f"ffn_dwo_{l}", out_dtype=BF16)
        gw[("ffn_w_in", l)] = _mm(dgu, t["hn2"], "tn", f"ffn_dwi_{l}", out_dtype=BF16)
        dh, dhb, dg = _mm(dgu, wf_t[l], "nn", f"ffn_dx_{l}", norm_bwd=(t["h1"], ln_ffn[l], dh),
                          after=scatter_start(f"f{l}"))
        small[("ln_ffn", l)] = dg
        if l < n_a:
            do2 = _mm(dhb, w_gout[l], "nt", f"gdn_out_dx_{l}")
            gw[("gdn_w_out", l)] = _mm(t["o2"], dhb, "tn", f"gdn_out_dw_{l}", out_dtype=BF16)
            do_raw, dgn, dgate = _headnorm_bwd(do2, t["o_raw"], gdn_norm[l], f"gdn_outnorm_bwd_{l}",
                                               gate=t["proj"], gate_col0=3 * d, head_major=True)
            small[("gdn_norm", l)] = dgn
            dqkv, dgb = _gdn_bwd(t["qkv"], t["gb"], do_raw, t["states"], nh, f"gdn_rule_bwd_{l}")
            dpab, dal, ddt = _gates_bwd(dgb, t["pab"], t["al"], t["dtb"], nh, f"gdn_gates_bwd_{l}")
            small[("gdn_a_log", l)] = dal
            small[("gdn_dt_bias", l)] = ddt
            dproj, dconv = _conv_bwd(dqkv, t["proj"], conv_full[l], d, f"gdn_conv_bwd_{l}", dgate)
            small[("gdn_conv", l)] = dconv
            dw_main = _mm(dproj, t["hn"], "tn", f"gdn_proj_dw_{l}", out_dtype=BF16)
            dw_ab = _mm(dpab, t["hn"], "tn", f"gdn_proj_ab_dw_{l}", out_dtype=BF16)
            gw[("gdn_w_in", l)] = jnp.concatenate([dw_main, dw_ab[:16]], axis=0)[:win_rows]
            dhn_ab = _mm(dpab, w_ab_t[l], "nn", f"gdn_proj_ab_dx_{l}")
            last = dict(a=dproj, b=w_in_t[l], mode="nn", name=f"gdn_proj_dx_{l}", res=dhn_ab)
        else:
            j = l - n_a
            do = _mm(dhb, w_sout[j], "nt", f"sb_out_dx_{j}", out_dtype=BF16)
            gw[("sb_w_out", j)] = _mm(t["o"], dhb, "tn", f"sb_out_dw_{j}", out_dtype=BF16)
            dq, dk, dv = _sb_bwd(t["qn"], k_sh, v_sh, do, t["ctab"], f"sb_attn_bwd_{j}")
            dks.append(dk)
            dvs.append(dv)
            dqpre, dqn = _headnorm_bwd(dq, t["qpre"], sb_q_norm[j], f"sb_qnorm_bwd_{j}", scale=HEAD ** -0.5, dx_dtype=BF16)
            small[("sb_q_norm", j)] = dqn
            gw[("sb_w_q", j)] = _mm(t["hn"], dqpre, "tn", f"sb_q_dw_{j}", out_dtype=BF16)
            last = dict(a=dqpre, b=w_q[j], mode="nt", name=f"sb_q_dx_{j}")
        dh, _, dg = _mm(**last, norm_bwd=(t["h0"], ln_mix[l], dh), after=scatter_start(f"a{l}"))
        small[("ln_mix", l)] = dg
    grad_x = dh[None]

    landed = {}
    for name, keys in reversed(chunks):
        lands = _send_wait(g_started[name], True, dh, f"comm_gwait_{name}")
        for key, land in zip(keys, lands):
            landed[key] = land.reshape(NDEV, -1, d)

    def summed(wname, count, rows_out=None):
        return _sum_slots([landed[(wname, i)] for i in range(count)], f"grad_sum_{wname}", rows_out)

    gt_gdn_w_in = summed("gdn_w_in", n_a, win_cols)
    gt_ffn_w_in = summed("ffn_w_in", depth)
    g_gdn_w_in = jnp.transpose(gt_gdn_w_in, (0, 2, 1))
    g_gdn_w_out = summed("gdn_w_out", n_a)
    g_w_kv = jnp.transpose(summed("w_kv", 1)[0])
    g_sb_w_q = summed("sb_w_q", n_b)
    g_sb_w_out = summed("sb_w_out", n_b)
    g_ffn_w_in = jnp.transpose(gt_ffn_w_in, (0, 2, 1))
    g_ffn_w_out = summed("ffn_w_out", depth)
    g_ple_w_proj = jnp.transpose(summed("ple_w_proj", depth).reshape(depth, -1, pd), (0, 2, 1))
    g_ple_w_gate = summed("ple_w_gate", depth)

    def vec_rows(v):
        return v.reshape(-1, HEAD)

    small_items = []
    for name_, cnt in (("ln_mix", depth), ("ln_ffn", depth), ("ln_ple", depth)):
        for l in range(cnt):
            small_items.append(((name_, l), vec_rows(small[(name_, l)])))
    for l in range(n_a):
        small_items.append((("gdn_conv", l), small[("gdn_conv", l)].reshape(-1, HEAD)))
        small_items.append((("gdn_a_log", l), small[("gdn_a_log", l)]))
        small_items.append((("gdn_dt_bias", l), small[("gdn_dt_bias", l)]))
        small_items.append((("gdn_norm", l), small[("gdn_norm", l)]))
    small_items.append(("kv_norm", vec_rows(small["kv_norm"])))
    small_items.append(("k_norm", small["k_norm"]))
    for j in range(n_b):
        small_items.append((("sb_q_norm", j), small[("sb_q_norm", j)]))
    spack = jnp.concatenate([_pad_rows(a, 8) for _, a in small_items], axis=0)
    sg = _all_gather(spack, "comm_gather_small").reshape(NDEV, spack.shape[0], HEAD)
    ssum = _sum_slots([sg], "small_sum")[0]
    sm = {}
    off = 0
    for key, a in small_items:
        sm[key] = ssum[off:off + a.shape[0]]
        off += a.shape[0] + (-a.shape[0]) % 8

    g_ln_mix = jnp.stack([sm[("ln_mix", l)].reshape(d) for l in range(depth)])
    g_ln_ffn = jnp.stack([sm[("ln_ffn", l)].reshape(d) for l in range(depth)])
    g_ln_ple = jnp.stack([sm[("ln_ple", l)].reshape(d) for l in range(depth)])
    conv_loc = gdn_conv.shape[2]
    g_conv_full = jnp.stack([sm[("gdn_conv", l)].reshape(gdn_conv.shape[1], 3 * d) for l in range(n_a)])
    g_gdn_conv = lax.dynamic_slice_in_dim(g_conv_full, me * conv_loc, conv_loc, axis=2)
    g_a_log = jnp.stack([sm[("gdn_a_log", l)][0, :nh] for l in range(n_a)])
    g_dt_bias = jnp.stack([sm[("gdn_dt_bias", l)][0, :nh] for l in range(n_a)])
    g_gdn_norm = jnp.stack([sm[("gdn_norm", l)][0] for l in range(n_a)])
    g_kv_norm = sm["kv_norm"].reshape(d)
    g_k_norm = sm["k_norm"][0]
    g_sb_q_norm = jnp.stack([sm[("sb_q_norm", j)][0] for j in range(n_b)])

    grads = [g_ln_mix, g_ln_ffn, g_ln_ple, g_gdn_w_in, g_gdn_conv, g_a_log, g_dt_bias, g_gdn_norm, g_gdn_w_out,
             g_kv_norm, g_w_kv, g_k_norm, g_sb_w_q, g_sb_q_norm, g_sb_w_out, g_ffn_w_in, g_ffn_w_out, g_ple_w_proj,
             g_ple_w_gate]
    weights = [ln_mix, ln_ffn, ln_ple, gdn_w_in, gdn_conv, gdn_a_log, gdn_dt_bias, gdn_norm, gdn_w_out, kv_norm, w_kv,
               k_norm, sb_w_q, sb_q_norm, sb_w_out, ffn_w_in, ffn_w_out, ple_w_proj, ple_w_gate]
    moms = [m_ln_mix, m_ln_ffn, m_ln_ple, m_gdn_w_in, m_gdn_conv, m_gdn_a_log, m_gdn_dt_bias, m_gdn_norm, m_gdn_w_out,
            m_kv_norm, m_w_kv, m_k_norm, m_sb_w_q, m_sb_q_norm, m_sb_w_out, m_ffn_w_in, m_ffn_w_out, m_ple_w_proj,
            m_ple_w_gate]
    vels = [v_ln_mix, v_ln_ffn, v_ln_ple, v_gdn_w_in, v_gdn_conv, v_gdn_a_log, v_gdn_dt_bias, v_gdn_norm, v_gdn_w_out,
            v_kv_norm, v_w_kv, v_k_norm, v_sb_w_q, v_sb_q_norm, v_sb_w_out, v_ffn_w_in, v_ffn_w_out, v_ple_w_proj,
            v_ple_w_gate]

    deltas, new_m, new_v = [], [], []
    small_idx = [i for i, w in enumerate(weights) if w.size < 8 * HEAD * 16]
    transposed = {3: gt_gdn_w_in, 15: gt_ffn_w_in}
    for i, (w, g, m, v) in enumerate(zip(weights, grads, moms, vels)):
        if i in small_idx:
            deltas.append(None), new_m.append(None), new_v.append(None)
            continue
        if i in transposed:
            tr = lambda a: jnp.transpose(a, (0, 2, 1))
            dl, nm, nv = _adamw(tr(w), transposed[i], tr(m), tr(v), f"adamw_{i}")
            deltas.append(tr(dl)), new_m.append(tr(nm)), new_v.append(tr(nv))
            continue
        shp = w.shape
        two = lambda a: a.reshape(-1, shp[-1])
        dl, nm, nv = _adamw(two(w), two(g), two(m), two(v), f"adamw_{i}")
        deltas.append(dl.reshape(shp)), new_m.append(nm.reshape(shp)), new_v.append(nv.reshape(shp))

    def flat_pack(arrs):
        flat = jnp.concatenate([a.reshape(-1) for a in arrs])
        pad = (-flat.shape[0]) % (8 * HEAD)
        return jnp.pad(flat, (0, pad)).reshape(-1, HEAD)

    sw = flat_pack([weights[i] for i in small_idx])
    sgr = flat_pack([grads[i] for i in small_idx])
    smo = flat_pack([moms[i] for i in small_idx])
    sve = flat_pack([vels[i] for i in small_idx])
    sdl, snm, snv = _adamw(sw, sgr, smo, sve, "adamw_small")
    off = 0
    for i in small_idx:
        n = weights[i].size
        shp = weights[i].shape
        deltas[i] = sdl.reshape(-1)[off:off + n].reshape(shp)
        new_m[i] = snm.reshape(-1)[off:off + n].reshape(shp)
        new_v[i] = snv.reshape(-1)[off:off + n].reshape(shp)
        off += n

    return (loss, grad_x, *grads, *deltas, *new_m, *new_v)
```

```python
import math

import jax
import jax.numpy as jnp
from jax import lax
from jax.experimental import pallas as pl
from jax.experimental.pallas import tpu as pltpu

F32 = jnp.float32
BF16 = jnp.bfloat16
NDEV = 8
HEAD = 128
CHUNK = 64
SBLK = 256
EPS = 1e-6
LR, B1, B2, ADAM_EPS, WD, STEP = 0.001, 0.9, 0.999, 1e-08, 0.01, 10
NEG = -1e30
MM_VMEM_BUDGET = 40 * 1024 * 1024

NN = (((1,), (0,)), ((), ()))
NT = (((1,), (1,)), ((), ()))
TN = (((0,), (0,)), ((), ()))
BNN = (((2,), (1,)), ((0,), (0,)))
BNT = (((2,), (2,)), ((0,), (0,)))
BTN = (((1,), (1,)), ((0,), (0,)))
MESH = pl.DeviceIdType.MESH


def _dot(a, b, dims=NN):
    return lax.dot_general(a.astype(BF16), b.astype(BF16), dims, preferred_element_type=F32)


def _dot_hilo(a, b01_twice, one_dot):
    hi = a.astype(BF16)
    lo = (a - hi.astype(F32)).astype(BF16)
    if one_dot:
        return lax.dot_general(jnp.concatenate([hi, lo], axis=1), b01_twice, NN, preferred_element_type=F32)
    b01 = b01_twice[:a.shape[1]]
    return (lax.dot_general(hi, b01, NN, preferred_element_type=F32)
            + lax.dot_general(lo, b01, NN, preferred_element_type=F32))


def _pick(dim, cands):
    for c in cands:
        if dim % c == 0:
            return c
    return dim


def _params(sem, vmem_mb=48):
    return pltpu.CompilerParams(dimension_semantics=sem, vmem_limit_bytes=vmem_mb * 1024 * 1024)


def _silu(x):
    return x * jax.nn.sigmoid(x)


def _silu_and_grad(x):
    s = jax.nn.sigmoid(x)
    xs = x * s
    return xs, s + xs * (1.0 - s)


def _mm(a, b, mode, name, out_dtype=F32, res=None, norm_g=None, norm_bwd=None, after=None):
    if mode == "nn":
        (m, k), n = a.shape, b.shape[1]
    elif mode == "nt":
        (m, k), n = a.shape, b.shape[0]
    else:
        (k, m), n = a.shape, b.shape[1]
    rows = norm_g is not None or norm_bwd is not None
    tn = n if rows else _pick(n, (512, 256, 128))
    tk = k if k <= 4096 else max(t for t in range(128, 4097, 128) if k % t == 0)
    nk = k // tk
    out_b = jnp.dtype(out_dtype).itemsize + (res.dtype.itemsize if res is not None else 0)
    out_b += 2 if norm_g is not None else 0
    out_b += 10 if norm_bwd is not None else 0
    for tm in [t for t in range(min(m, 2048), 127, -128) if m % t == 0] + [m]:
        need = 2 * (tm * tk * a.dtype.itemsize + tk * tn * b.dtype.itemsize + tm * tn * out_b) + 4 * tm * tn
        if need <= MM_VMEM_BUDGET:
            break
    dims = {"nn": NN, "nt": NT, "tn": TN}[mode]
    if mode == "tn":
        a_spec = pl.BlockSpec((tk, tm), lambda i, j, kk: (kk, i))
    else:
        a_spec = pl.BlockSpec((tm, tk), lambda i, j, kk: (i, kk))
    if mode == "nt":
        b_spec = pl.BlockSpec((tn, tk), lambda i, j, kk: (j, kk))
    else:
        b_spec = pl.BlockSpec((tk, tn), lambda i, j, kk: (kk, j))
    mn_spec = pl.BlockSpec((tm, tn), lambda i, j, kk: (i, j))
    vec_spec = pl.BlockSpec((1, tn), lambda i, j, kk: (0, j))
    has_res = res is not None
    n_in = 2 + has_res + (1 if norm_g is not None else 0) + (3 if norm_bwd is not None else 0) + (after is not None)

    def body(*refs):
        a_ref, b_ref = refs[:2]
        extra = list(refs[2:n_in])
        outs = refs[n_in:-1]
        acc = refs[-1]
        kk = pl.program_id(2)

        @pl.when(kk == 0)
        def _():
            acc[...] = jnp.zeros_like(acc)

        if norm_bwd is not None:
            @pl.when((kk == 0) & (pl.program_id(0) == 0))
            def _():
                outs[2][...] = jnp.zeros_like(outs[2])

        acc[...] += _dot(a_ref[...], b_ref[...], dims)

        @pl.when(kk == nk - 1)
        def _():
            r = acc[...]
            if has_res:
                r = r + extra.pop(0)[...].astype(F32)
            if norm_g is not None:
                outs[0][...] = r.astype(out_dtype)
                rs = lax.rsqrt(jnp.mean(r * r, axis=-1, keepdims=True) + EPS)
                outs[1][...] = (r * rs * extra.pop(0)[...]).astype(BF16)
            elif norm_bwd is not None:
                xv, gv, dres = extra.pop(0)[...], extra.pop(0)[...], extra.pop(0)[...]
                rs = lax.rsqrt(jnp.mean(xv * xv, axis=-1, keepdims=True) + EPS)
                gdy = r * gv
                dx = dres + rs * gdy - xv * (rs * rs * rs) * jnp.mean(xv * gdy, axis=-1, keepdims=True)
                outs[0][...] = dx
                outs[1][...] = dx.astype(BF16)
                outs[2][...] += jnp.sum(r * xv * rs, axis=0, keepdims=True)
            else:
                outs[0][...] = r.astype(out_dtype)

    ins = [a, b] + ([res] if has_res else [])
    in_specs = [a_spec, b_spec] + ([mn_spec] if has_res else [])
    out_specs, out_shape = [mn_spec], [jax.ShapeDtypeStruct((m, n), out_dtype)]
    sem = ("parallel", "parallel", "arbitrary")
    if norm_g is not None:
        ins.append(norm_g.reshape(1, n))
        in_specs.append(vec_spec)
        out_specs.append(mn_spec)
        out_shape.append(jax.ShapeDtypeStruct((m, n), BF16))
    if norm_bwd is not None:
        x, g, dres = norm_bwd
        ins += [x, g.reshape(1, n), dres]
        in_specs += [mn_spec, vec_spec, mn_spec]
        out_specs += [mn_spec, vec_spec]
        out_shape += [jax.ShapeDtypeStruct((m, n), BF16), jax.ShapeDtypeStruct((1, n), F32)]
        sem = ("arbitrary", "arbitrary", "arbitrary")
    if after is not None:
        ins.append(after)
        in_specs.append(pl.BlockSpec(memory_space=pl.ANY))
    out = pl.pallas_call(
        body, name=name, grid=(m // tm, n // tn, nk), in_specs=in_specs, out_specs=out_specs,
        out_shape=out_shape, scratch_shapes=[pltpu.VMEM((tm, tn), F32)],
        compiler_params=_params(sem))(*ins)
    return out[0] if len(out) == 1 else out


def _rms_fwd(h, g, name):
    s, d = h.shape
    tm = _pick(s, (512, 256, 128))

    def body(h_ref, g_ref, o_ref):
        x = h_ref[...]
        r = lax.rsqrt(jnp.mean(x * x, axis=-1, keepdims=True) + EPS)
        o_ref[...] = (x * r * g_ref[...]).astype(BF16)

    return pl.pallas_call(
        body, name=name, grid=(s // tm,),
        in_specs=[pl.BlockSpec((tm, d), lambda i: (i, 0)), pl.BlockSpec((1, d), lambda i: (0, 0))],
        out_specs=pl.BlockSpec((tm, d), lambda i: (i, 0)),
        out_shape=jax.ShapeDtypeStruct((s, d), BF16), compiler_params=_params(("parallel",)))(h, g.reshape(1, d))


def _headnorm_fwd(x, g, name, scale=1.0, gate=None, gate_col0=0, out_dtype=BF16, width=None, head_major=False):
    if head_major:
        s, d = x.shape[1], x.shape[0] * HEAD
    else:
        s, d = x.shape[0], (width or x.shape[1])
    nh = d // HEAD
    tm = _pick(s, (256, 128))
    has_gate = gate is not None
    gb = gate_col0 // d

    def body(*refs):
        if has_gate:
            x_ref, g_ref, gt_ref, o_ref = refs
        else:
            x_ref, g_ref, o_ref = refs
        gv = g_ref[...]
        for h in range(nh):
            sl = slice(h * HEAD, (h + 1) * HEAD)
            xv = (x_ref[h] if head_major else x_ref[:, sl]).astype(F32)
            r = lax.rsqrt(jnp.mean(xv * xv, axis=-1, keepdims=True) + EPS)
            y = xv * r * gv
            if scale != 1.0:
                y = y * scale
            if has_gate:
                y = y * _silu(gt_ref[:, sl])
            o_ref[:, sl] = y.astype(out_dtype)

    row = pl.BlockSpec((tm, d), lambda i: (i, 0))
    hm = pl.BlockSpec((nh, tm, HEAD), lambda i: (0, i, 0))
    ins = [x, g.reshape(1, HEAD)]
    in_specs = [hm if head_major else row, pl.BlockSpec((1, HEAD), lambda i: (0, 0))]
    if has_gate:
        ins.append(gate)
        in_specs.append(pl.BlockSpec((tm, d), lambda i: (i, gb)))
    return pl.pallas_call(
        body, name=name, grid=(s // tm,), in_specs=in_specs, out_specs=row,
        out_shape=jax.ShapeDtypeStruct((s, d), out_dtype), compiler_params=_params(("parallel",)))(*ins)


def _headnorm_bwd(dy, x, g, name, scale=1.0, gate=None, gate_col0=0, dx_dtype=F32, head_major=False):
    s, d = dy.shape
    nh = d // HEAD
    tm = _pick(s, (256, 128))
    has_gate = gate is not None
    gb = gate_col0 // d

    def body(*refs):
        if has_gate:
            dy_ref, x_ref, g_ref, gt_ref, dx_ref, dg_ref, dgt_ref = refs
        else:
            dy_ref, x_ref, g_ref, dx_ref, dg_ref = refs

        @pl.when(pl.program_id(0) == 0)
        def _():
            dg_ref[...] = jnp.zeros_like(dg_ref)

        gv = g_ref[...]
        dg_acc = jnp.zeros((1, HEAD), F32)
        for h in range(nh):
            sl = slice(h * HEAD, (h + 1) * HEAD)
            xv = (x_ref[h] if head_major else x_ref[:, sl]).astype(F32)
            dyv = dy_ref[:, sl].astype(F32)
            r = lax.rsqrt(jnp.mean(xv * xv, axis=-1, keepdims=True) + EPS)
            if has_gate:
                gt = gt_ref[:, sl]
                act, dact = _silu_and_grad(gt)
                dgt_ref[:, sl] = (dyv * (xv * r * gv) * dact).astype(dgt_ref.dtype)
                dn = dyv * act
            else:
                dn = dyv
            if scale != 1.0:
                dn = dn * scale
            gdn = dn * gv
            mean_t = jnp.mean(xv * gdn, axis=-1, keepdims=True)
            dxv = (r * gdn - xv * (r * r * r) * mean_t).astype(dx_dtype)
            if head_major:
                dx_ref[h] = dxv
            else:
                dx_ref[:, sl] = dxv
            dg_acc = dg_acc + jnp.sum(dn * xv * r, axis=0, keepdims=True)
        dg_ref[...] += dg_acc

    row = pl.BlockSpec((tm, d), lambda i: (i, 0))
    hm = pl.BlockSpec((nh, tm, HEAD), lambda i: (0, i, 0))
    vec = pl.BlockSpec((1, HEAD), lambda i: (0, 0))
    ins = [dy, x, g.reshape(1, HEAD)]
    in_specs = [row, hm if head_major else row, vec]
    out_specs = [hm if head_major else row, vec]
    dx_shape = (nh, s, HEAD) if head_major else (s, d)
    out_shape = [jax.ShapeDtypeStruct(dx_shape, dx_dtype), jax.ShapeDtypeStruct((1, HEAD), F32)]
    if has_gate:
        ins.append(gate)
        in_specs.append(pl.BlockSpec((tm, d), lambda i: (i, gb)))
        out_specs.append(pl.BlockSpec((tm, d), lambda i: (i, gb)))
        out_shape.append(jax.ShapeDtypeStruct((s, gate.shape[1]), BF16))
    return pl.pallas_call(
        body, name=name, grid=(s // tm,), in_specs=in_specs, out_specs=out_specs, out_shape=out_shape,
        compiler_params=_params(("arbitrary",)))(*ins)


def _kv_grad(dks, dvs, kv, g, name):
    s, d = dks[0].shape
    nh = d // HEAD
    tm = _pick(s, (256, 128))
    n = len(dks)

    def body(*refs):
        dk_refs, dv_refs = refs[:n], refs[n:2 * n]
        kv_ref, g_ref, o_ref, dg_ref = refs[2 * n:]

        @pl.when(pl.program_id(0) == 0)
        def _():
            dg_ref[...] = jnp.zeros_like(dg_ref)

        gv = g_ref[...]
        dg_acc = jnp.zeros((1, HEAD), F32)
        for h in range(nh):
            sl = slice(h * HEAD, (h + 1) * HEAD)
            xv = kv_ref[:, sl]
            dyv = sum(r[:, sl] for r in dk_refs)
            r = lax.rsqrt(jnp.mean(xv * xv, axis=-1, keepdims=True) + EPS)
            gdn = dyv * gv
            mean_t = jnp.mean(xv * gdn, axis=-1, keepdims=True)
            o_ref[:, sl] = (r * gdn - xv * (r * r * r) * mean_t).astype(BF16)
            dg_acc = dg_acc + jnp.sum(dyv * xv * r, axis=0, keepdims=True)
        o_ref[:, d:] = sum(r[...] for r in dv_refs).astype(BF16)
        dg_ref[...] += dg_acc

    row = pl.BlockSpec((tm, d), lambda i: (i, 0))
    vec = pl.BlockSpec((1, HEAD), lambda i: (0, 0))
    return pl.pallas_call(
        body, name=name, grid=(s // tm,), in_specs=[row] * (2 * n) + [row, vec],
        out_specs=[pl.BlockSpec((tm, 2 * d), lambda i: (i, 0)), vec],
        out_shape=[jax.ShapeDtypeStruct((s, 2 * d), BF16), jax.ShapeDtypeStruct((1, HEAD), F32)],
        compiler_params=_params(("arbitrary",)))(*dks, *dvs, kv, g.reshape(1, HEAD))


def _swiglu_fwd(hn, wf_t, name):
    s, d = hn.shape
    f = wf_t.shape[0] // 2
    tm = _pick(s, (1024, 512, 256, 128))
    tn = _pick(f, (512, 256, 128))
    nj = f // tn

    def body(a_ref, wg_ref, wu_ref, act_ref, g_ref, u_ref):
        a = a_ref[...]
        g = _dot(a, wg_ref[...], NT)
        u = _dot(a, wu_ref[...], NT)
        act_ref[...] = (_silu(g) * u).astype(BF16)
        g_ref[...] = g.astype(BF16)
        u_ref[...] = u.astype(BF16)

    o_spec = pl.BlockSpec((tm, tn), lambda i, j: (i, j))
    sds = jax.ShapeDtypeStruct((s, f), BF16)
    return pl.pallas_call(
        body, name=name, grid=(s // tm, nj),
        in_specs=[pl.BlockSpec((tm, d), lambda i, j: (i, 0)), pl.BlockSpec((tn, d), lambda i, j: (j, 0)),
                  pl.BlockSpec((tn, d), lambda i, j: (j + nj, 0))],
        out_specs=[o_spec, o_spec, o_spec], out_shape=[sds, sds, sds],
        compiler_params=_params(("parallel", "parallel")))(hn, wf_t, wf_t)


def _swiglu_bwd(dh, w_out, g, u, name):
    s, d = dh.shape
    f = w_out.shape[0]
    tm = _pick(s, (1024, 512, 256, 128))
    tn = _pick(f, (512, 256, 128))

    def body(dh_ref, w_ref, g_ref, u_ref, dgu_ref):
        j = pl.program_id(1)
        dact = _dot(dh_ref[...], w_ref[...], NT)
        gv = g_ref[...].astype(F32)
        uv = u_ref[...].astype(F32)
        sg, dsg = _silu_and_grad(gv)
        dgu_ref[:, pl.ds(pl.multiple_of(j * tn, HEAD), tn)] = (dact * uv * dsg).astype(BF16)
        dgu_ref[:, pl.ds(pl.multiple_of(f + j * tn, HEAD), tn)] = (dact * sg).astype(BF16)

    o_spec = pl.BlockSpec((tm, tn), lambda i, j: (i, j))
    return pl.pallas_call(
        body, name=name, grid=(s // tm, f // tn),
        in_specs=[pl.BlockSpec((tm, d), lambda i, j: (i, 0)), pl.BlockSpec((tn, d), lambda i, j: (j, 0)), o_spec, o_spec],
        out_specs=pl.BlockSpec((tm, 2 * f), lambda i, j: (i, 0)), out_shape=jax.ShapeDtypeStruct((s, 2 * f), BF16),
        compiler_params=_params(("parallel", "arbitrary")))(dh, w_out, g, u)


def _ple_fwd(h, hn, p, w_gate, wp_t, name, norm_gs=()):
    s, d = h.shape
    pd = p.shape[1]
    tm = _pick(s, (512, 256, 128))
    ng = len(norm_gs)

    def body(h_ref, hn_ref, p_ref, wg_ref, wp_ref, *rest):
        g_refs, (o_ref, gp_ref, pp_ref), n_refs = rest[:ng], rest[ng:ng + 3], rest[ng + 3:]
        gpre = _dot(hn_ref[...], wg_ref[...], NN)
        pp = _dot(p_ref[...], wp_ref[...], NT)
        o = h_ref[...] + pp * jax.nn.sigmoid(gpre)
        o_ref[...] = o
        gp_ref[...] = gpre.astype(BF16)
        pp_ref[...] = pp.astype(BF16)
        if ng:
            on = o * lax.rsqrt(jnp.mean(o * o, axis=-1, keepdims=True) + EPS)
            for g_ref, n_ref in zip(g_refs, n_refs):
                n_ref[...] = (on * g_ref[...]).astype(BF16)

    row = pl.BlockSpec((tm, d), lambda i: (i, 0))
    vec = pl.BlockSpec((1, d), lambda i: (0, 0))
    bf = jax.ShapeDtypeStruct((s, d), BF16)
    return pl.pallas_call(
        body, name=name, grid=(s // tm,),
        in_specs=[row, row, pl.BlockSpec((tm, pd), lambda i: (i, 0)), pl.BlockSpec((d, d), lambda i: (0, 0)),
                  pl.BlockSpec((d, pd), lambda i: (0, 0))] + [vec] * ng,
        out_specs=[row] * (3 + ng), out_shape=[jax.ShapeDtypeStruct((s, d), F32), bf, bf] + [bf] * ng,
        compiler_params=_params(("parallel",)))(h, hn, p, w_gate, wp_t, *[g.reshape(1, d) for g in norm_gs])


def _ple_bwd(dh, gpre, pp, name):
    s, d = dh.shape
    tm = _pick(s, (512, 256, 128))

    def body(dh_ref, gp_ref, pp_ref, dgp_ref, dpp_ref):
        dv = dh_ref[...]
        sig = jax.nn.sigmoid(gp_ref[...].astype(F32))
        ppv = pp_ref[...].astype(F32)
        dpp_ref[...] = (dv * sig).astype(BF16)
        dgp_ref[...] = (dv * ppv * sig * (1.0 - sig)).astype(BF16)

    row = pl.BlockSpec((tm, d), lambda i: (i, 0))
    sds = jax.ShapeDtypeStruct((s, d), BF16)
    return pl.pallas_call(
        body, name=name, grid=(s // tm,), in_specs=[row, row, row], out_specs=[row, row], out_shape=[sds, sds],
        compiler_params=_params(("parallel",)))(dh, gpre, pp)


def _loss_fwd_bwd(y, t, name):
    s, d = y.shape
    tm = _pick(s, (512, 256, 128))

    def body(y_ref, t_ref, dy_ref, l_ref):
        @pl.when(pl.program_id(0) == 0)
        def _():
            l_ref[...] = jnp.zeros_like(l_ref)

        e = y_ref[...] - t_ref[...]
        dy_ref[...] = e * (1.0 / d)
        l_ref[...] += jnp.sum(e * e, axis=0, keepdims=True) * (0.5 / d)

    row = pl.BlockSpec((tm, d), lambda i: (i, 0))
    vec = pl.BlockSpec((1, d), lambda i: (0, 0))
    return pl.pallas_call(
        body, name=name, grid=(s // tm,), in_specs=[row, row], out_specs=[row, vec],
        out_shape=[jax.ShapeDtypeStruct((s, d), F32), jax.ShapeDtypeStruct((1, d), F32)],
        compiler_params=_params(("arbitrary",)))(y, t)


PADR = 8
CONV_ROWS = 256


def _conv_fwd(proj, w_conv, d, name):
    s = proj.shape[0]
    nh = d // HEAD
    kw = w_conv.shape[0]
    qscale = HEAD ** -0.5

    tr = _pick(s, (CONV_ROWS,))

    def body(x_ref, w_ref, o_ref, xp):
        kind = pl.program_id(0) // nh
        xp[0:PADR, :] = jnp.zeros((PADR, HEAD), F32)
        xp[PADR:, :] = x_ref[...]
        taps = [w_ref[j:j + 1, :] for j in range(kw)]
        for r0 in range(0, s, tr):
            acc = jnp.zeros((tr, HEAD), F32)
            for j in range(kw):
                acc = acc + taps[j] * xp[r0 + PADR - (kw - 1) + j:r0 + PADR - (kw - 1) + j + tr, :]
            a = _silu(acc)
            r = lax.rsqrt(jnp.sum(a * a, axis=-1, keepdims=True) + EPS)
            fac = jnp.where(kind == 0, r * qscale, jnp.where(kind == 1, r, jnp.ones_like(r)))
            o_ref[r0:r0 + tr, :] = a * fac

    blk = pl.BlockSpec((s, HEAD), lambda c: (0, c))
    hm = pl.BlockSpec((None, s, HEAD), lambda c: (c, 0, 0))
    return pl.pallas_call(
        body, name=name, grid=(3 * nh,), in_specs=[blk, pl.BlockSpec((kw, HEAD), lambda c: (0, c))], out_specs=hm,
        out_shape=jax.ShapeDtypeStruct((3 * nh, s, HEAD), F32), scratch_shapes=[pltpu.VMEM((s + PADR, HEAD), F32)],
        compiler_params=_params(("parallel",)))(proj, w_conv)


def _conv_bwd(dqkv, proj, w_conv, d, name, dproj):
    s = proj.shape[0]
    nh = d // HEAD
    kw = w_conv.shape[0]
    qscale = HEAD ** -0.5

    tr = _pick(s, (CONV_ROWS,))

    def body(dy_ref, x_ref, w_ref, _, dx_ref, dw_ref, xp, dp):
        kind = pl.program_id(0) // nh
        xp[0:PADR, :] = jnp.zeros((PADR, HEAD), F32)
        xp[PADR:, :] = x_ref[...]
        dp[s:, :] = jnp.zeros((PADR, HEAD), F32)
        taps = [w_ref[j:j + 1, :] for j in range(kw)]
        sc = jnp.where(kind == 0, qscale, 1.0)
        dws = [jnp.zeros((1, HEAD), F32) for _ in range(kw)]
        for r0 in range(0, s, tr):
            acc = jnp.zeros((tr, HEAD), F32)
            for j in range(kw):
                acc = acc + taps[j] * xp[r0 + PADR - (kw - 1) + j:r0 + PADR - (kw - 1) + j + tr, :]
            a, da_dacc = _silu_and_grad(acc)
            dy = dy_ref[r0:r0 + tr, :]
            r = lax.rsqrt(jnp.sum(a * a, axis=-1, keepdims=True) + EPS)
            dyn = dy * sc
            da_norm = r * dyn - a * (r * r * r) * jnp.sum(a * dyn, axis=-1, keepdims=True)
            dacc = jnp.where(kind == 2, dy, da_norm) * da_dacc
            dp[r0:r0 + tr, :] = dacc
            for j in range(kw):
                sh = kw - 1 - j
                dws[j] = dws[j] + jnp.sum(dacc * xp[r0 + PADR - sh:r0 + PADR - sh + tr, :], axis=0, keepdims=True)
        for j in range(kw):
            dw_ref[j:j + 1, :] = dws[j]
        for r0 in range(0, s, tr):
            dx = jnp.zeros((tr, HEAD), F32)
            for j in range(kw):
                sh = kw - 1 - j
                dx = dx + taps[j] * dp[r0 + sh:r0 + sh + tr, :]
            dx_ref[r0:r0 + tr, :] = dx.astype(BF16)

    blk = pl.BlockSpec((s, HEAD), lambda c: (0, c))
    hm = pl.BlockSpec((None, s, HEAD), lambda c: (c, 0, 0))
    wblk = pl.BlockSpec((kw, HEAD), lambda c: (0, c))
    return pl.pallas_call(
        body, name=name, grid=(3 * nh,), in_specs=[hm, blk, wblk, pl.BlockSpec(memory_space=pl.ANY)],
        out_specs=[blk, wblk], input_output_aliases={3: 0},
        out_shape=[jax.ShapeDtypeStruct(dproj.shape, BF16), jax.ShapeDtypeStruct((kw, 3 * d), F32)],
        scratch_shapes=[pltpu.VMEM((s + PADR, HEAD), F32), pltpu.VMEM((s + PADR, HEAD), F32)],
        compiler_params=_params(("parallel",)))(dqkv, proj, w_conv, dproj)


def _softplus(x):
    return jnp.maximum(x, 0.0) + jnp.log(1.0 + jnp.exp(-jnp.abs(x)))


def _gates_fwd(pab, a_log, dt_bias, nh, name):
    s = pab.shape[0]
    tm = _pick(s, (512, 256, 128))

    def body(x_ref, al_ref, dt_ref, o_ref):
        x = x_ref[...]
        lane = lax.broadcasted_iota(jnp.int32, x.shape, 1)
        g = -jnp.exp(al_ref[...]) * _softplus(x + dt_ref[...])
        o_ref[...] = jnp.where(lane < nh, g, jnp.where(lane < 2 * nh, jax.nn.sigmoid(x), 0.0))

    row = pl.BlockSpec((tm, HEAD), lambda i: (i, 0))
    vec = pl.BlockSpec((1, HEAD), lambda i: (0, 0))
    return pl.pallas_call(
        body, name=name, grid=(s // tm,), in_specs=[row, vec, vec], out_specs=row,
        out_shape=jax.ShapeDtypeStruct((s, HEAD), F32), compiler_params=_params(("parallel",)))(pab, a_log, dt_bias)


def _gates_bwd(dgb, pab, a_log, dt_bias, nh, name):
    s = pab.shape[0]
    tm = _pick(s, (512, 256, 128))

    def body(d_ref, x_ref, al_ref, dt_ref, dx_ref, dal_ref, ddt_ref):
        @pl.when(pl.program_id(0) == 0)
        def _():
            dal_ref[...] = jnp.zeros_like(dal_ref)
            ddt_ref[...] = jnp.zeros_like(ddt_ref)

        x = x_ref[...]
        dv = d_ref[...]
        lane = lax.broadcasted_iota(jnp.int32, x.shape, 1)
        ea = jnp.exp(al_ref[...])
        xs = x + dt_ref[...]
        g = -ea * _softplus(xs)
        dxs = jnp.where(lane < nh, dv * (-ea) * jax.nn.sigmoid(xs), 0.0)
        sg = jax.nn.sigmoid(x)
        dxb = jnp.where((lane >= nh) & (lane < 2 * nh), dv * sg * (1.0 - sg), 0.0)
        dx_ref[...] = (dxs + dxb).astype(BF16)
        dal_ref[...] += jnp.sum(jnp.where(lane < nh, dv * g, 0.0), axis=0, keepdims=True)
        ddt_ref[...] += jnp.sum(dxs, axis=0, keepdims=True)

    row = pl.BlockSpec((tm, HEAD), lambda i: (i, 0))
    vec = pl.BlockSpec((1, HEAD), lambda i: (0, 0))
    return pl.pallas_call(
        body, name=name, grid=(s // tm,), in_specs=[row, row, vec, vec], out_specs=[row, vec, vec],
        out_shape=[jax.ShapeDtypeStruct((s, HEAD), BF16), jax.ShapeDtypeStruct((1, HEAD), F32),
                   jax.ShapeDtypeStruct((1, HEAD), F32)],
        compiler_params=_params(("arbitrary",)))(dgb, pab, a_log, dt_bias)


def _tri_inv(a_low, eye_f):
    n = -a_low
    p = eye_f + n
    steps = int(math.log2(a_low.shape[-1])) - 1
    for _ in range(steps):
        n = _dot(n, n, BNN)
        p = p + _dot(p, n, BNN)
    return p


def _lane_col(x, lane, idx):
    return jnp.sum(jnp.where(lane == idx, x, 0.0), axis=1, keepdims=True)


def _head_cols(gbv, lo, nh):
    lane = lax.broadcasted_iota(jnp.int32, gbv.shape, 1)
    return jnp.stack([_lane_col(gbv, lane, lo + h) for h in range(nh)], axis=0)


def _gdn_chunk(q, k, v, g_col, beta_col, st):
    c = q.shape[1]
    r_i = lax.broadcasted_iota(jnp.int32, (c, c), 0)
    c_i = lax.broadcasted_iota(jnp.int32, (c, c), 1)
    incl = c_i <= r_i
    strict = c_i < r_i
    eye = c_i == r_i
    g_row = jnp.sum(jnp.where(eye, g_col, 0.0), axis=1, keepdims=True)
    gc_col = jnp.sum(jnp.where(incl, g_row, 0.0), axis=2, keepdims=True)
    gc_row = jnp.sum(jnp.where(eye, gc_col, 0.0), axis=1, keepdims=True)
    g_last = jnp.sum(g_col, axis=1, keepdims=True)
    decay = jnp.exp(jnp.where(incl, gc_col - gc_row, NEG))
    kk = _dot(k, k, BNT)
    a_low = jnp.where(strict, beta_col * kk * decay, 0.0)
    t_inv = _tri_inv(a_low, eye.astype(F32))
    e_g = jnp.exp(gc_col)
    bk = beta_col * e_g
    rhs = jnp.concatenate([v * beta_col, k * bk], axis=2)
    sol = _dot(t_inv, rhs, BNN)
    u, w = sol[:, :, :HEAD], sol[:, :, HEAD:]
    qk_raw = _dot(q, k, BNT)
    qk = qk_raw * decay
    q_dec = q * e_g
    e2 = jnp.exp(g_last - gc_col)
    k_dec = k * e2
    gl = jnp.exp(g_last)
    ws = _dot(jnp.concatenate([w, q_dec], axis=1), st, BNN)
    v_new = u - ws[:, :c]
    o = ws[:, c:] + _dot(qk, v_new, BNN)
    st_new = st * gl + _dot(k_dec, v_new, BTN)
    inter = dict(incl=incl, strict=strict, eye=eye, decay=decay, kk=kk, t_inv=t_inv, e_g=e_g, bk=bk, sol=sol, w=w,
                 qk_raw=qk_raw, qk=qk, q_dec=q_dec, e2=e2, k_dec=k_dec, gl=gl, v_new=v_new, c_i=c_i, r_i=r_i)
    return o, st_new, inter


def _gdn_fwd(qkv, gb, nh, name):
    s = qkv.shape[1]
    nc = s // CHUNK

    def body(q_ref, k_ref, v_ref, gb_ref, o_ref, st_ref, state):
        @pl.when(pl.program_id(0) == 0)
        def _():
            state[...] = jnp.zeros_like(state)

        gbv = gb_ref[...]
        st = state[...]
        st_ref[...] = st
        o, st_new, _ = _gdn_chunk(q_ref[...], k_ref[...], v_ref[...], _head_cols(gbv, 0, nh), _head_cols(gbv, nh, nh), st)
        o_ref[...] = o
        state[...] = st_new

    def qspec(part):
        return pl.BlockSpec((nh, CHUNK, HEAD), lambda n: (part, n, 0))

    return pl.pallas_call(
        body, name=name, grid=(nc,),
        in_specs=[qspec(0), qspec(1), qspec(2), pl.BlockSpec((CHUNK, HEAD), lambda n: (n, 0))],
        out_specs=[qspec(0), pl.BlockSpec((None, nh, HEAD, HEAD), lambda n: (n, 0, 0, 0))],
        out_shape=[jax.ShapeDtypeStruct((nh, s, HEAD), F32), jax.ShapeDtypeStruct((nc, nh, HEAD, HEAD), F32)],
        scratch_shapes=[pltpu.VMEM((nh, HEAD, HEAD), F32)],
        compiler_params=_params(("arbitrary",)))(qkv, qkv, qkv, gb)


def _gdn_bwd(qkv, gb, do, states, nh, name):
    s = qkv.shape[1]
    nc = s // CHUNK
    c = CHUNK

    def body(q_ref, k_ref, v_ref, gb_ref, do_ref, st_ref, dqkv_ref, dgb_ref, dstate):
        @pl.when(pl.program_id(0) == 0)
        def _():
            dstate[...] = jnp.zeros_like(dstate)

        gbv = gb_ref[...]
        lane = lax.broadcasted_iota(jnp.int32, gbv.shape, 1)
        q, k, v = q_ref[...], k_ref[...], v_ref[...]
        beta_col = _head_cols(gbv, nh, nh)
        st = st_ref[...]
        dst = dstate[...]
        dov = do_ref[...]
        _, _, it = _gdn_chunk(q, k, v, _head_cols(gbv, 0, nh), beta_col, st)
        incl, strict, eye, decay = it["incl"], it["strict"], it["eye"], it["decay"]
        dv_new = _dot(it["qk"], dov, BTN) + _dot(it["k_dec"], dst, BNN)
        d_qk = _dot(dov, it["v_new"], BNT)
        dd = _dot(jnp.concatenate([dov, -dv_new], axis=1), st, BNT)
        dq_dec, dw = dd[:, :c], dd[:, c:]
        dst_new = _dot(it["q_dec"], dov, BTN) + it["gl"] * dst - _dot(it["w"], dv_new, BTN)
        dgl = jnp.sum(jnp.sum(dst * st, axis=2, keepdims=True), axis=1, keepdims=True)
        dk_dec = _dot(it["v_new"], dst, BNT)
        dsol = jnp.concatenate([dv_new, dw], axis=2)
        drhs = _dot(it["t_inv"], dsol, BTN)
        d_a = jnp.where(strict, -_dot(drhs, it["sol"], BNT), 0.0)
        drhs_u, drhs_w = drhs[:, :, :HEAD], drhs[:, :, HEAD:]
        dvh = beta_col * drhs_u
        rw_k = jnp.sum(drhs_w * k, axis=2, keepdims=True)
        dbeta = jnp.sum(drhs_u * v, axis=2, keepdims=True) + it["e_g"] * rw_k
        dkh = it["bk"] * drhs_w
        dgc_col = it["bk"] * rw_k
        dkk = d_a * beta_col * decay
        dbeta = dbeta + jnp.sum(d_a * it["kk"] * decay, axis=2, keepdims=True)
        ddecay = d_a * beta_col * it["kk"]
        dkh = dkh + _dot(dkk, k, BNN) + _dot(dkk, k, BTN)
        dqk_raw = d_qk * decay
        ddecay = ddecay + d_qk * it["qk_raw"]
        dqh = _dot(dqk_raw, k, BNN)
        dkh = dkh + _dot(dqk_raw, q, BTN)
        ddm = jnp.where(incl, ddecay * decay, 0.0)
        dgc_col = dgc_col + jnp.sum(ddm, axis=2, keepdims=True)
        dgc_row = -jnp.sum(ddm, axis=1, keepdims=True)
        dqh = dqh + dq_dec * it["e_g"]
        dgc_col = dgc_col + jnp.sum(dq_dec * it["q_dec"], axis=2, keepdims=True)
        dkh = dkh + dk_dec * it["e2"]
        tmp = jnp.sum(dk_dec * it["k_dec"], axis=2, keepdims=True)
        dgc_col = dgc_col - tmp
        dg_last = jnp.sum(tmp, axis=1, keepdims=True) + dgl * it["gl"]
        dgc_tot_row = dgc_row + jnp.sum(jnp.where(eye, dgc_col, 0.0), axis=1, keepdims=True)
        dg_col = jnp.sum(jnp.where(it["c_i"] >= it["r_i"], dgc_tot_row, 0.0), axis=2, keepdims=True) + dg_last
        dqkv_ref[0] = dqh
        dqkv_ref[1] = dkh
        dqkv_ref[2] = dvh
        dstate[...] = dst_new
        dgb_acc = jnp.zeros(gbv.shape, F32)
        for h in range(nh):
            dgb_acc = jnp.where(lane == h, dg_col[h], jnp.where(lane == nh + h, dbeta[h], dgb_acc))
        dgb_ref[...] = dgb_acc

    def rev(part):
        return pl.BlockSpec((nh, CHUNK, HEAD), lambda n: (part, nc - 1 - n, 0))

    gspec = pl.BlockSpec((CHUNK, HEAD), lambda n: (nc - 1 - n, 0))
    dqkv, dgb = pl.pallas_call(
        body, name=name, grid=(nc,),
        in_specs=[rev(0), rev(1), rev(2), gspec, rev(0),
                  pl.BlockSpec((None, nh, HEAD, HEAD), lambda n: (nc - 1 - n, 0, 0, 0))],
        out_specs=[pl.BlockSpec((3, nh, CHUNK, HEAD), lambda n: (0, 0, nc - 1 - n, 0)), gspec],
        out_shape=[jax.ShapeDtypeStruct((3, nh, s, HEAD), F32), jax.ShapeDtypeStruct((s, HEAD), F32)],
        scratch_shapes=[pltpu.VMEM((nh, HEAD, HEAD), F32)],
        compiler_params=_params(("arbitrary",)))(qkv, qkv, qkv, gb, do, states)
    return dqkv.reshape(3 * nh, s, HEAD), dgb


SB_TQ_FWD = 1024
SB_TQ = 512


def _tri01(rel):
    j_i = lax.broadcasted_iota(jnp.int32, (2 * SBLK, SBLK), 0) & (SBLK - 1)
    s_i = lax.broadcasted_iota(jnp.int32, (2 * SBLK, SBLK), 1)
    return rel(j_i, s_i).astype(BF16)


SB_HP = 2


def _each(fn, *lists):
    return [fn(*xs) for xs in zip(*lists)]


def _sb_scores(qts, kblks, mask, csums, rhs01, one_dot):
    zs = _each(lambda qt, kb: _dot(qt, kb, NT), qts, kblks)
    es = _each(lambda z: jnp.exp(-jnp.abs(z)), zs)
    sps = _each(lambda z, e: jnp.maximum(z, 0.0) + jnp.log(1.0 + e), zs, es)
    lns = _each(lambda sp: -sp if mask is None else jnp.where(mask, -sp, 0.0), sps)
    sts = _each(lambda ln: _dot_hilo(ln, rhs01, one_dot), lns)
    wgts = _each(lambda z, sp, st, cs: jnp.exp((z - sp) + st + cs), zs, sps, sts, csums)
    if mask is not None:
        wgts = _each(lambda w: jnp.where(mask, w, 0.0), wgts)
    return zs, es, wgts, lns


def _band_mask(rows, j, row0):
    r_i = lax.broadcasted_iota(jnp.int32, (rows, SBLK), 0)
    c_i = lax.broadcasted_iota(jnp.int32, (rows, SBLK), 1)
    return (j * SBLK + c_i) < (row0 + r_i)


def _sb_fwd(q, k, v, name):
    s, d = q.shape
    nh = d // HEAD
    tq = min(SB_TQ_FWD, s)
    nb = tq // SBLK

    hp = SB_HP
    heads = [slice(h * HEAD, (h + 1) * HEAD) for h in range(hp)]

    def body(q_ref, k_ref, v_ref, o_ref, c_ref, acc, cs):
        qb = pl.program_id(1)
        lane = lax.broadcasted_iota(jnp.int32, (tq, HEAD), 1)
        after = _tri01(lambda j, t: j > t)
        acc[...] = jnp.zeros_like(acc)
        cs[...] = jnp.zeros_like(cs)
        c_ref[...] = jnp.zeros_like(c_ref)

        def process(rs, kb, mask):
            keys = pl.ds(pl.multiple_of(kb * SBLK, SBLK), SBLK)
            csums = [cs[h, rs, :] for h in range(hp)]
            _, _, wgts, lns = _sb_scores([q_ref[rs, hs] for hs in heads], [k_ref[keys, hs] for hs in heads], mask, csums, after, True)
            pvs = _each(lambda w, hs: _dot(w, v_ref[keys, hs]), wgts, heads)
            tots = _each(lambda ln: jnp.sum(ln, axis=1, keepdims=True), lns)
            for h, hs in enumerate(heads):
                acc[h, rs, :] += pvs[h]
                c_ref[rs, hs] = jnp.where(lane[rs, :] == kb, csums[h], c_ref[rs, hs])
                cs[h, rs, :] = csums[h] + tots[h]

        for j in reversed(range(nb)):
            process(slice(j * SBLK, tq), qb * nb + j, _band_mask(tq - j * SBLK, j, j * SBLK))

        def step(it, carry):
            process(slice(0, tq), qb * nb - 1 - it, None)
            return carry

        lax.fori_loop(0, qb * nb, step, 0)
        for h, hs in enumerate(heads):
            o_ref[:, hs] = acc[h].astype(BF16)

    qspec = pl.BlockSpec((tq, hp * HEAD), lambda h, i: (i, h))
    kspec = pl.BlockSpec((s, hp * HEAD), lambda h, i: (0, h))
    return pl.pallas_call(
        body, name=name, grid=(nh // hp, s // tq), in_specs=[qspec, kspec, kspec], out_specs=[qspec, qspec],
        out_shape=[jax.ShapeDtypeStruct((s, d), BF16), jax.ShapeDtypeStruct((s, d), F32)],
        scratch_shapes=[pltpu.VMEM((hp, tq, HEAD), F32), pltpu.VMEM((hp, tq, 1), F32)],
        compiler_params=_params(("parallel", "arbitrary")))(q, k, v)


def _sb_bwd(q, k, v, do, ctab, name):
    s, d = q.shape
    nh = d // HEAD
    tq = min(SB_TQ, s)
    nb = tq // SBLK

    hp = SB_HP
    heads = [slice(h * HEAD, (h + 1) * HEAD) for h in range(hp)]

    def body(q_ref, k_ref, v_ref, do_ref, c_ref, dq_ref, dk_ref, dv_ref, ps):
        qb = pl.program_id(1)

        @pl.when(qb == 0)
        def _():
            dk_ref[...] = jnp.zeros_like(dk_ref)
            dv_ref[...] = jnp.zeros_like(dv_ref)

        dq_ref[...] = jnp.zeros_like(dq_ref)
        ps[...] = jnp.zeros_like(ps)
        lane = lax.broadcasted_iota(jnp.int32, (tq, HEAD), 1)
        after = _tri01(lambda j, t: j > t)
        before = _tri01(lambda j, t: j < t)

        def process(rs, kb, mask):
            keys = pl.ds(pl.multiple_of(kb * SBLK, SBLK), SBLK)
            kblks = [k_ref[keys, hs] for hs in heads]
            qts = [q_ref[rs, hs] for hs in heads]
            dots = [do_ref[rs, hs] for hs in heads]
            csums = [_lane_col(c_ref[rs, hs], lane[rs, :], kb) for hs in heads]
            zs, es, wgts, _ = _sb_scores(qts, kblks, mask, csums, after, False)
            dlws = _each(lambda dt, hs, w: _dot(dt, v_ref[keys, hs], NT) * w, dots, heads, wgts)
            pts = _each(lambda dlw: _dot_hilo(dlw, before, False), dlws)
            pfxs = [ps[h, rs, :] for h in range(hp)]
            rs_ = _each(lambda e: 1.0 / (1.0 + e), es)
            sigs = _each(lambda z, e, r: jnp.where(z >= 0.0, r, e * r), zs, es, rs_)
            dzs = _each(lambda dlw, sig, pfx, pt: dlw * (1.0 - sig) - sig * (pfx + pt), dlws, sigs, pfxs, pts)
            tots = _each(lambda dlw: jnp.sum(dlw, axis=1, keepdims=True), dlws)
            if mask is not None:
                dzs = _each(lambda dz: jnp.where(mask, dz, 0.0), dzs)
            dqs = _each(lambda dz, kb_: _dot(dz, kb_), dzs, kblks)
            dks = _each(lambda dz, qt: _dot(dz, qt, TN), dzs, qts)
            dvs = _each(lambda w, dt: _dot(w, dt, TN), wgts, dots)
            for h, hs in enumerate(heads):
                dq_ref[rs, hs] += dqs[h]
                dk_ref[keys, hs] += dks[h]
                dv_ref[keys, hs] += dvs[h]
                ps[h, rs, :] = pfxs[h] + tots[h]

        def step(kb, carry):
            process(slice(0, tq), kb, None)
            return carry

        lax.fori_loop(0, qb * nb, step, 0)
        for j in range(nb):
            process(slice(j * SBLK, tq), qb * nb + j, _band_mask(tq - j * SBLK, j, j * SBLK))

    qspec = pl.BlockSpec((tq, hp * HEAD), lambda h, i: (i, h))
    kspec = pl.BlockSpec((s, hp * HEAD), lambda h, i: (0, h))
    sds = jax.ShapeDtypeStruct((s, d), F32)
    return pl.pallas_call(
        body, name=name, grid=(nh // hp, s // tq), in_specs=[qspec, kspec, kspec, qspec, qspec],
        out_specs=[qspec, kspec, kspec], out_shape=[sds, sds, sds],
        scratch_shapes=[pltpu.VMEM((hp, tq, 1), F32)],
        compiler_params=_params(("parallel", "arbitrary")))(q, k, v, do, ctab)


def _my_index():
    return 4 * lax.axis_index("x") + 2 * lax.axis_index("y") + lax.axis_index("c")


def _all_gather(x_shard, name):
    m_per, n = x_shard.shape

    def body(x_ref, out_ref, send_sems, recv_sems, local_sem):
        x, y, c = lax.axis_index("x"), lax.axis_index("y"), lax.axis_index("c")
        me, sibling = (x, y, c), (x, y, 1 - c)
        chips = [(1 - x, y), (x, 1 - y), (1 - x, 1 - y)]

        def rows(px, py, pc):
            return out_ref.at[pl.ds((4 * px + 2 * py + pc) * m_per, m_per), :]

        def copy(k, block, to, src=None):
            return pltpu.make_async_remote_copy(
                src_ref=rows(*block) if src is None else src, dst_ref=rows(*block),
                send_sem=send_sems.at[k], recv_sem=recv_sems.at[k], device_id=to, device_id_type=MESH)

        mine = pltpu.make_async_copy(x_ref, rows(*me), local_sem)
        mine.start()
        first = [copy(0, me, sibling, src=x_ref)]
        first += [copy(1 + j, me, (*chip, c), src=x_ref) for j, chip in enumerate(chips)]
        for cp in first:
            cp.start()
        passed = [copy(4 + j, (*chip, c), sibling) for j, chip in enumerate(chips)]
        for j, chip in enumerate(chips):
            copy(1 + j, (*chip, c), me).wait_recv()
            passed[j].start()
        copy(0, sibling, me).wait_recv()
        for j, chip in enumerate(chips):
            copy(4 + j, (*chip, 1 - c), me).wait_recv()
        for cp in first + passed:
            cp.wait_send()
        mine.wait()

    return pl.pallas_call(
        body, name=name, out_shape=jax.ShapeDtypeStruct((NDEV * m_per, n), x_shard.dtype),
        in_specs=[pl.BlockSpec(memory_space=pl.ANY)], out_specs=pl.BlockSpec(memory_space=pl.ANY),
        scratch_shapes=[pltpu.SemaphoreType.DMA((7,)), pltpu.SemaphoreType.DMA((7,)), pltpu.SemaphoreType.DMA],
    )(x_shard)


HBM_SPEC = pl.BlockSpec(memory_space=pltpu.HBM)
SEM_SPEC = pl.BlockSpec(memory_space=pltpu.SEMAPHORE)
ANY_SPEC = pl.BlockSpec(memory_space=pl.ANY)
EFFECT = pltpu.SideEffectType.DATAFLOW_SIDE_EFFECTING


def _exchange_copies(src_refs, land_refs, send_sems, recv_sems, self_sems, scatter):
    x, y, c = lax.axis_index("x"), lax.axis_index("y"), lax.axis_index("c")
    me = 4 * x + 2 * y + c
    remote, local = [], []
    for p, (src_ref, land_ref) in enumerate(zip(src_refs, land_refs)):
        rows = land_ref.shape[0] // NDEV

        def part(idx):
            return src_ref.at[pl.ds(idx * rows, rows), :] if scatter else src_ref

        slot = land_ref.at[pl.ds(me * rows, rows), :]
        for k in range(1, NDEV):
            px, py, pc = x ^ ((k >> 2) & 1), y ^ ((k >> 1) & 1), c ^ (k & 1)
            remote.append(pltpu.make_async_remote_copy(
                src_ref=part(4 * px + 2 * py + pc), dst_ref=slot, send_sem=send_sems.at[7 * p + k - 1],
                recv_sem=recv_sems.at[7 * p + k - 1], device_id=(px, py, pc), device_id_type=MESH))
        local.append(pltpu.make_async_copy(part(me), slot, self_sems.at[p]))
    return remote, local


def _send_start(srcs, scatter, after, name):
    n = len(srcs)
    lands = []
    for s in srcs:
        rows = s.shape[0] if scatter else NDEV * s.shape[0]
        lands.append(pltpu.with_memory_space_constraint(lax.empty((rows, s.shape[1]), s.dtype), pltpu.HBM))

    def body(*refs):
        src_refs, land_refs = refs[:n], refs[n:2 * n]
        send_sems, recv_sems, self_sems = refs[2 * n + 1:2 * n + 4]
        remote, local = _exchange_copies(src_refs, land_refs, send_sems, recv_sems, self_sems, scatter)
        for cp in remote + local:
            cp.start()
        refs[-1][...] = jnp.zeros_like(refs[-1])

    hbm = lambda a: pltpu.HBM(a.shape, a.dtype)
    out = pl.pallas_call(
        body, name=name,
        out_shape=(pltpu.SemaphoreType.DMA((7 * n,)), pltpu.SemaphoreType.DMA((7 * n,)), pltpu.SemaphoreType.DMA((n,)),
                   *[hbm(s) for s in srcs], *[hbm(a) for a in lands], jax.ShapeDtypeStruct((8, HEAD), F32)),
        in_specs=(HBM_SPEC,) * (2 * n) + (ANY_SPEC,),
        out_specs=(SEM_SPEC,) * 3 + (HBM_SPEC,) * (2 * n) + (pl.BlockSpec(memory_space=pltpu.VMEM),),
        input_output_aliases={i: 3 + i for i in range(2 * n)},
        compiler_params=pltpu.CompilerParams(has_side_effects=EFFECT),
    )(*[pltpu.with_memory_space_constraint(s, pltpu.HBM) for s in srcs], *lands, after)
    return dict(sems=out[:3], srcs=out[3:3 + n], lands=out[3 + n:3 + 2 * n], token=out[-1])


def _send_wait(started, scatter, after, name):
    srcs, lands = started["srcs"], started["lands"]
    n = len(srcs)

    def body(*refs):
        src_refs, land_refs = refs[:n], refs[n:2 * n]
        send_sems, recv_sems, self_sems = refs[2 * n:2 * n + 3]
        remote, local = _exchange_copies(src_refs, land_refs, send_sems, recv_sems, self_sems, scatter)
        for cp in remote:
            cp.wait_send()
            cp.wait_recv()
        for cp in local:
            cp.wait()

    hbm = lambda a: pltpu.HBM(a.shape, a.dtype)
    out = pl.pallas_call(
        body, name=name, out_shape=(*[hbm(s) for s in srcs], *[hbm(a) for a in lands]),
        in_specs=(HBM_SPEC,) * (2 * n) + (SEM_SPEC,) * 3 + (ANY_SPEC,), out_specs=(HBM_SPEC,) * (2 * n),
        input_output_aliases={i: i for i in range(2 * n)},
        compiler_params=pltpu.CompilerParams(has_side_effects=EFFECT),
    )(*srcs, *lands, *started["sems"], after)
    return out[n:]


def _sum_slots(xs, name, rows_out=None):
    _, r, c = xs[0].shape
    ro = rows_out or r
    tc = _pick(c, (128,))

    def body(*refs):
        o_ref = refs[-1]
        for l, x_ref in enumerate(refs[:-1]):
            acc = x_ref[0].astype(F32)
            for i in range(1, NDEV):
                acc = acc + x_ref[i].astype(F32)
            o_ref[l] = acc[:ro]

    return pl.pallas_call(
        body, name=name, grid=(c // tc,), in_specs=[pl.BlockSpec((NDEV, r, tc), lambda j: (0, 0, j))] * len(xs),
        out_specs=pl.BlockSpec((len(xs), ro, tc), lambda j: (0, 0, j)),
        out_shape=jax.ShapeDtypeStruct((len(xs), ro, c), F32), compiler_params=_params(("parallel",)))(*xs)


def _adamw(w, g, m, v, name):
    if w.ndim == 3:
        nl, r, c = w.shape
        tc = _pick(c, (256, 128))
        grid = (nl, c // tc)
        blk = pl.BlockSpec((None, r, tc), lambda i, j: (i, 0, j))
        sem = ("parallel", "parallel")
    else:
        r, c = w.shape
        tr = _pick(r, (256, 128, 64, 32, 16, 8))
        grid = (r // tr,)
        blk = pl.BlockSpec((tr, c), lambda i: (i, 0))
        sem = ("parallel",)
    c1 = 1.0 - B1 ** STEP
    c2 = 1.0 - B2 ** STEP

    def body(w_ref, g_ref, m_ref, v_ref, d_ref, nm_ref, nv_ref):
        gv = g_ref[...]
        nm = B1 * m_ref[...] + (1.0 - B1) * gv
        nv = B2 * v_ref[...] + (1.0 - B2) * (gv * gv)
        d_ref[...] = -LR * ((nm / c1) / (jnp.sqrt(nv / c2) + ADAM_EPS) + WD * w_ref[...])
        nm_ref[...] = nm
        nv_ref[...] = nv

    sds = jax.ShapeDtypeStruct(w.shape, F32)
    return pl.pallas_call(
        body, name=name, grid=grid, in_specs=[blk] * 4, out_specs=[blk] * 3, out_shape=[sds] * 3,
        compiler_params=_params(sem))(w, g, m, v)


def _pad_rows(a, mult):
    r = a.shape[0]
    pad = (-r) % mult
    return a if pad == 0 else jnp.pad(a, ((0, pad), (0, 0)))


def _pad_lanes(v, width=HEAD):
    return jnp.pad(v.reshape(1, -1), ((0, 0), (0, width - v.shape[-1])))


def kernel(x, p, ln_mix, ln_ffn, ln_ple, gdn_w_in, gdn_conv, gdn_a_log, gdn_dt_bias, gdn_norm, gdn_w_out, kv_norm, w_kv, k_norm, sb_w_q, sb_q_norm, sb_w_out, ffn_w_in, ffn_w_out, ple_w_proj, ple_w_gate, loss_target, m_ln_mix, m_ln_ffn, m_ln_ple, m_gdn_w_in, m_gdn_conv, m_gdn_a_log, m_gdn_dt_bias, m_gdn_norm, m_gdn_w_out, m_kv_norm, m_w_kv, m_k_norm, m_sb_w_q, m_sb_q_norm, m_sb_w_out, m_ffn_w_in, m_ffn_w_out, m_ple_w_proj, m_ple_w_gate, v_ln_mix, v_ln_ffn, v_ln_ple, v_gdn_w_in, v_gdn_conv, v_gdn_a_log, v_gdn_dt_bias, v_gdn_norm, v_gdn_w_out, v_kv_norm, v_w_kv, v_k_norm, v_sb_w_q, v_sb_q_norm, v_sb_w_out, v_ffn_w_in, v_ffn_w_out, v_ple_w_proj, v_ple_w_gate):
    s, d = x.shape[1], x.shape[2]
    nh = d // HEAD
    depth = ln_mix.shape[0]
    n_a = gdn_w_in.shape[0]
    n_b = sb_w_q.shape[0]
    me = _my_index()
    win_cols = gdn_w_in.shape[2]
    win_rows = 4 * d + 2 * nh

    def col_t(w):
        return jnp.transpose(w).astype(BF16)

    local = {}
    for l in range(n_a):
        local[("gdn_w_in", l)] = col_t(gdn_w_in[l])
        local[("gdn_w_out", l)] = gdn_w_out[l].astype(BF16)
    local[("w_kv", 0)] = col_t(w_kv)
    for j in range(n_b):
        local[("sb_w_q", j)] = sb_w_q[j].astype(BF16)
        local[("sb_w_out", j)] = sb_w_out[j].astype(BF16)
    for l in range(depth):
        local[("ffn_w_in", l)] = col_t(ffn_w_in[l])
        local[("ffn_w_out", l)] = ffn_w_out[l].astype(BF16)
        local[("ple_w_proj", l)] = col_t(ple_w_proj[l]).reshape(-1, d)
        local[("ple_w_gate", l)] = ple_w_gate[l].astype(BF16)
    local = {key: _pad_rows(a, 16) for key, a in local.items()}

    chunks = []
    for l in range(depth):
        mix = [("gdn_w_in", l), ("gdn_w_out", l)] if l < n_a else [("sb_w_q", l - n_a), ("sb_w_out", l - n_a)]
        rest = [("ffn_w_in", l), ("ffn_w_out", l), ("ple_w_proj", l), ("ple_w_gate", l)]
        if l == n_a - 1:
            rest.append(("w_kv", 0))
        chunks += [(f"a{l}", mix), (f"f{l}", rest)]
    chunk_keys = dict(chunks)

    conv_rows = n_a * gdn_conv.shape[1]
    conv_sh = _pad_rows(gdn_conv.reshape(conv_rows, -1), 8)
    conv_g = _all_gather(conv_sh, "comm_gather_conv")
    token = conv_g
    conv_g = conv_g.reshape(NDEV, conv_sh.shape[0], -1)
    conv_full = jnp.transpose(conv_g[:, :conv_rows, :], (1, 0, 2)).reshape(n_a, gdn_conv.shape[1], 3 * d)

    w_started = {}
    for name, keys in chunks:
        w_started[name] = _send_start([local[k] for k in keys], False, token, f"comm_wstart_{name}")
        token = w_started[name]["token"]

    full = {}

    def fetch(name, after):
        lands = _send_wait(w_started[name], False, after, f"comm_wwait_{name}")
        for key, land in zip(chunk_keys[name], lands):
            full[key] = land

    def whole(key, valid=None):
        a = full[key]
        if valid is not None:
            a = a.reshape(NDEV, -1, d)[:, :valid, :].reshape(-1, d)
        return a

    pd = p.shape[-1]
    w_in_t, w_ab_t, w_gout, w_q, w_sout, wf_t, w_fout, wp_t, w_pg = {}, {}, {}, {}, {}, {}, {}, {}, {}
    wkv_t = None

    h = x[0]
    sv = []
    kv_sv = None
    k_sh = v_sh = None
    for l in range(depth):
        t = {}
        t["h0"] = h
        if l == 0:
            hn = _rms_fwd(h, ln_mix[l], f"rms_mix_{l}")
        t["hn"] = hn
        fetch(f"a{l}", token if l == 0 else hn)
        if l < n_a:
            wt = whole(("gdn_w_in", l), win_cols)
            w_in_t[l] = wt[:4 * d]
            w_ab_t[l] = jnp.pad(wt[4 * d:], ((0, HEAD - 2 * nh), (0, 0)))
            w_gout[l] = whole(("gdn_w_out", l))
        else:
            w_q[l - n_a] = whole(("sb_w_q", l - n_a))
            w_sout[l - n_a] = whole(("sb_w_out", l - n_a))
        if l < n_a:
            proj = _mm(hn, w_in_t[l], "nt", f"gdn_proj_{l}")
            pab = _mm(hn, w_ab_t[l], "nt", f"gdn_proj_ab_{l}")
            qkv = _conv_fwd(proj, conv_full[l], d, f"gdn_conv_{l}")
            al, dtb = _pad_lanes(gdn_a_log[l]), _pad_lanes(gdn_dt_bias[l])
            gb = _gates_fwd(pab, al, dtb, nh, f"gdn_gates_{l}")
            o_raw, states = _gdn_fwd(qkv, gb, nh, f"gdn_rule_{l}")
            o2 = _headnorm_fwd(o_raw, gdn_norm[l], f"gdn_outnorm_{l}", gate=proj, gate_col0=3 * d, head_major=True)
            h, hn2 = _mm(o2, w_gout[l], "nn", f"gdn_out_{l}", res=h, norm_g=ln_ffn[l])
            t.update(proj=proj, pab=pab, qkv=qkv, gb=gb, o_raw=o_raw, states=states, o2=o2, al=al, dtb=dtb)
        else:
            j = l - n_a
            qpre = _mm(hn, w_q[j], "nn", f"sb_qproj_{j}")
            qn = _headnorm_fwd(qpre, sb_q_norm[j], f"sb_qnorm_{j}", scale=HEAD ** -0.5)
            o, ctab = _sb_fwd(qn, k_sh, v_sh, f"sb_attn_{j}")
            h, hn2 = _mm(o, w_sout[j], "nn", f"sb_out_{j}", res=h, norm_g=ln_ffn[l])
            t.update(qpre=qpre, qn=qn, o=o, ctab=ctab)
        t["h1"] = h
        fetch(f"f{l}", hn2)
        wf_t[l] = whole(("ffn_w_in", l))
        w_fout[l] = whole(("ffn_w_out", l))
        wp_t[l] = full[("ple_w_proj", l)].reshape(d, pd)
        w_pg[l] = whole(("ple_w_gate", l))
        if l == n_a - 1:
            wkv_t = whole(("w_kv", 0))
        act, gs, us = _swiglu_fwd(hn2, wf_t[l], f"ffn_in_{l}")
        h, hn3 = _mm(act, w_fout[l], "nn", f"ffn_out_{l}", res=h, norm_g=ln_ple[l])
        t.update(hn2=hn2, act=act, gs=gs, us=us, h2=h)
        gains = ([ln_mix[l + 1]] if l + 1 < depth else []) + ([kv_norm] if l == n_a - 1 else [])
        h, gpre, pp, *normed = _ple_fwd(h, hn3, p[l, 0], w_pg[l], wp_t[l], f"ple_{l}", norm_gs=gains)
        if l + 1 < depth:
            hn = normed[0]
        t.update(hn3=hn3, gpre=gpre, pp=pp)
        sv.append(t)
        if l == n_a - 1:
            kvn = normed[-1]
            kv = _mm(kvn, wkv_t, "nt", "kv_proj")
            k_sh = _headnorm_fwd(kv, k_norm, "k_norm", width=d)
            v_sh = kv[:, d:].astype(BF16)
            kv_sv = dict(h=h, kvn=kvn, kv=kv)

    dh, loss_vec = _loss_fwd_bwd(h, loss_target[0], "loss")
    loss = lax.psum(jnp.sum(loss_vec), ("x", "y", "c"))

    gw = {}
    small = {}
    g_started = {}

    def scatter_start(name):
        gparts = []
        for key in chunk_keys[name]:
            g = gw[key]
            g = g.reshape(NDEV, -1, d) if key[0] == "ple_w_proj" else g.reshape(NDEV, -1, g.shape[-1])
            padr = local[key].shape[0] - g.shape[1]
            if padr:
                g = jnp.pad(g, ((0, 0), (0, padr), (0, 0)))
            gparts.append(g.reshape(-1, d))
        g_started[name] = _send_start(gparts, True, jnp.zeros((8, HEAD), F32), f"comm_gstart_{name}")
        return g_started[name]["token"]

    dks, dvs = [], []
    for l in reversed(range(depth)):
        t = sv[l]
        if l == n_a - 1:
            dkv, dkn = _kv_grad(dks, dvs, kv_sv["kv"], k_norm, "k_norm_bwd")
            gw[("w_kv", 0)] = _mm(dkv, kv_sv["kvn"], "tn", "kv_dw", out_dtype=BF16)
            dh, _, dg = _mm(dkv, wkv_t, "nn", "kv_dx", norm_bwd=(kv_sv["h"], kv_norm, dh))
            small["kv_norm"] = dg
            small["k_norm"] = dkn
        dgp, dpp = _ple_bwd(dh, t["gpre"], t["pp"], f"ple_bwd_{l}")
        gw[("ple_w_gate", l)] = _mm(t["hn3"], dgp, "tn", f"ple_dwg_{l}", out_dtype=BF16)
        gw[("ple_w_proj", l)] = _mm(dpp, p[l, 0], "tn", f"ple_dwp_{l}", out_dtype=BF16)
        dh, dhb, dg = _mm(dgp, w_pg[l], "nt", f"ple_dx_{l}", norm_bwd=(t["h2"], ln_ple[l], dh))
        small[("ln_ple", l)] = dg
        dgu = _swiglu_bwd(dhb, w_fout[l], t["gs"], t["us"], f"ffn_bwd_act_{l}")
        gw[("ffn_w_out", l)] = _mm(t["act"], dhb, "tn", f"ffn_dwo_{l}", out_dtype=BF16)
        gw[("ffn_w_in", l)] = _mm(dgu, t["hn2"], "tn", f"ffn_dwi_{l}", out_dtype=BF16)
        dh, dhb, dg = _mm(dgu, wf_t[l], "nn", f"ffn_dx_{l}", norm_bwd=(t["h1"], ln_ffn[l], dh),
                          after=scatter_start(f"f{l}"))
        small[("ln_ffn", l)] = dg
        if l < n_a:
            do2 = _mm(dhb, w_gout[l], "nt", f"gdn_out_dx_{l}")
            gw[("gdn_w_out", l)] = _mm(t["o2"], dhb, "tn", f"gdn_out_dw_{l}", out_dtype=BF16)
            do_raw, dgn, dgate = _headnorm_bwd(do2, t["o_raw"], gdn_norm[l], f"gdn_outnorm_bwd_{l}",
                                               gate=t["proj"], gate_col0=3 * d, head_major=True)
            small[("gdn_norm", l)] = dgn
            dqkv, dgb = _gdn_bwd(t["qkv"], t["gb"], do_raw, t["states"], nh, f"gdn_rule_bwd_{l}")
            dpab, dal, ddt = _gates_bwd(dgb, t["pab"], t["al"], t["dtb"], nh, f"gdn_gates_bwd_{l}")
            small[("gdn_a_log", l)] = dal
            small[("gdn_dt_bias", l)] = ddt
            dproj, dconv = _conv_bwd(dqkv, t["proj"], conv_full[l], d, f"gdn_conv_bwd_{l}", dgate)
            small[("gdn_conv", l)] = dconv
            dw_main = _mm(dproj, t["hn"], "tn", f"gdn_proj_dw_{l}", out_dtype=BF16)
            dw_ab = _mm(dpab, t["hn"], "tn", f"gdn_proj_ab_dw_{l}", out_dtype=BF16)
            gw[("gdn_w_in", l)] = jnp.concatenate([dw_main, dw_ab[:16]], axis=0)[:win_rows]
            dhn_ab = _mm(dpab, w_ab_t[l], "nn", f"gdn_proj_ab_dx_{l}")
            last = dict(a=dproj, b=w_in_t[l], mode="nn", name=f"gdn_proj_dx_{l}", res=dhn_ab)
        else:
            j = l - n_a
            do = _mm(dhb, w_sout[j], "nt", f"sb_out_dx_{j}", out_dtype=BF16)
            gw[("sb_w_out", j)] = _mm(t["o"], dhb, "tn", f"sb_out_dw_{j}", out_dtype=BF16)
            dq, dk, dv = _sb_bwd(t["qn"], k_sh, v_sh, do, t["ctab"], f"sb_attn_bwd_{j}")
            dks.append(dk)
            dvs.append(dv)
            dqpre, dqn = _headnorm_bwd(dq, t["qpre"], sb_q_norm[j], f"sb_qnorm_bwd_{j}", scale=HEAD ** -0.5, dx_dtype=BF16)
            small[("sb_q_norm", j)] = dqn
            gw[("sb_w_q", j)] = _mm(t["hn"], dqpre, "tn", f"sb_q_dw_{j}", out_dtype=BF16)
            last = dict(a=dqpre, b=w_q[j], mode="nt", name=f"sb_q_dx_{j}")
        dh, _, dg = _mm(**last, norm_bwd=(t["h0"], ln_mix[l], dh), after=scatter_start(f"a{l}"))
        small[("ln_mix", l)] = dg
    grad_x = dh[None]

    landed = {}
    for name, keys in reversed(chunks):
        lands = _send_wait(g_started[name], True, dh, f"comm_gwait_{name}")
        for key, land in zip(keys, lands):
            landed[key] = land.reshape(NDEV, -1, d)

    def summed(wname, count, rows_out=None):
        return _sum_slots([landed[(wname, i)] for i in range(count)], f"grad_sum_{wname}", rows_out)

    gt_gdn_w_in = summed("gdn_w_in", n_a, win_cols)
    gt_ffn_w_in = summed("ffn_w_in", depth)
    g_gdn_w_in = jnp.transpose(gt_gdn_w_in, (0, 2, 1))
    g_gdn_w_out = summed("gdn_w_out", n_a)
    g_w_kv = jnp.transpose(summed("w_kv", 1)[0])
    g_sb_w_q = summed("sb_w_q", n_b)
    g_sb_w_out = summed("sb_w_out", n_b)
    g_ffn_w_in = jnp.transpose(gt_ffn_w_in, (0, 2, 1))
    g_ffn_w_out = summed("ffn_w_out", depth)
    g_ple_w_proj = jnp.transpose(summed("ple_w_proj", depth).reshape(depth, -1, pd), (0, 2, 1))
    g_ple_w_gate = summed("ple_w_gate", depth)

    def vec_rows(v):
        return v.reshape(-1, HEAD)

    small_items = []
    for name_, cnt in (("ln_mix", depth), ("ln_ffn", depth), ("ln_ple", depth)):
        for l in range(cnt):
            small_items.append(((name_, l), vec_rows(small[(name_, l)])))
    for l in range(n_a):
        small_items.append((("gdn_conv", l), small[("gdn_conv", l)].reshape(-1, HEAD)))
        small_items.append((("gdn_a_log", l), small[("gdn_a_log", l)]))
        small_items.append((("gdn_dt_bias", l), small[("gdn_dt_bias", l)]))
        small_items.append((("gdn_norm", l), small[("gdn_norm", l)]))
    small_items.append(("kv_norm", vec_rows(small["kv_norm"])))
    small_items.append(("k_norm", small["k_norm"]))
    for j in range(n_b):
        small_items.append((("sb_q_norm", j), small[("sb_q_norm", j)]))
    spack = jnp.concatenate([_pad_rows(a, 8) for _, a in small_items], axis=0)
    sg = _all_gather(spack, "comm_gather_small").reshape(NDEV, spack.shape[0], HEAD)
    ssum = _sum_slots([sg], "small_sum")[0]
    sm = {}
    off = 0
    for key, a in small_items:
        sm[key] = ssum[off:off + a.shape[0]]
        off += a.shape[0] + (-a.shape[0]) % 8

    g_ln_mix = jnp.stack([sm[("ln_mix", l)].reshape(d) for l in range(depth)])
    g_ln_ffn = jnp.stack([sm[("ln_ffn", l)].reshape(d) for l in range(depth)])
    g_ln_ple = jnp.stack([sm[("ln_ple", l)].reshape(d) for l in range(depth)])
    conv_loc = gdn_conv.shape[2]
    g_conv_full = jnp.stack([sm[("gdn_conv", l)].reshape(gdn_conv.shape[1], 3 * d) for l in range(n_a)])
    g_gdn_conv = lax.dynamic_slice_in_dim(g_conv_full, me * conv_loc, conv_loc, axis=2)
    g_a_log = jnp.stack([sm[("gdn_a_log", l)][0, :nh] for l in range(n_a)])
    g_dt_bias = jnp.stack([sm[("gdn_dt_bias", l)][0, :nh] for l in range(n_a)])
    g_gdn_norm = jnp.stack([sm[("gdn_norm", l)][0] for l in range(n_a)])
    g_kv_norm = sm["kv_norm"].reshape(d)
    g_k_norm = sm["k_norm"][0]
    g_sb_q_norm = jnp.stack([sm[("sb_q_norm", j)][0] for j in range(n_b)])

    grads = [g_ln_mix, g_ln_ffn, g_ln_ple, g_gdn_w_in, g_gdn_conv, g_a_log, g_dt_bias, g_gdn_norm, g_gdn_w_out,
             g_kv_norm, g_w_kv, g_k_norm, g_sb_w_q, g_sb_q_norm, g_sb_w_out, g_ffn_w_in, g_ffn_w_out, g_ple_w_proj,
             g_ple_w_gate]
    weights = [ln_mix, ln_ffn, ln_ple, gdn_w_in, gdn_conv, gdn_a_log, gdn_dt_bias, gdn_norm, gdn_w_out, kv_norm, w_kv,
               k_norm, sb_w_q, sb_q_norm, sb_w_out, ffn_w_in, ffn_w_out, ple_w_proj, ple_w_gate]
    moms = [m_ln_mix, m_ln_ffn, m_ln_ple, m_gdn_w_in, m_gdn_conv, m_gdn_a_log, m_gdn_dt_bias, m_gdn_norm, m_gdn_w_out,
            m_kv_norm, m_w_kv, m_k_norm, m_sb_w_q, m_sb_q_norm, m_sb_w_out, m_ffn_w_in, m_ffn_w_out, m_ple_w_proj,
            m_ple_w_gate]
    vels = [v_ln_mix, v_ln_ffn, v_ln_ple, v_gdn_w_in, v_gdn_conv, v_gdn_a_log, v_gdn_dt_bias, v_gdn_norm, v_gdn_w_out,
            v_kv_norm, v_w_kv, v_k_norm, v_sb_w_q, v_sb_q_norm, v_sb_w_out, v_ffn_w_in, v_ffn_w_out, v_ple_w_proj,
            v_ple_w_gate]

    deltas, new_m, new_v = [], [], []
    small_idx = [i for i, w in enumerate(weights) if w.size < 8 * HEAD * 16]
    transposed = {3: gt_gdn_w_in, 15: gt_ffn_w_in}
    for i, (w, g, m, v) in enumerate(zip(weights, grads, moms, vels)):
        if i in small_idx:
            deltas.append(None), new_m.append(None), new_v.append(None)
            continue
        if i in transposed:
            tr = lambda a: jnp.transpose(a, (0, 2, 1))
            dl, nm, nv = _adamw(tr(w), transposed[i], tr(m), tr(v), f"adamw_{i}")
            deltas.append(tr(dl)), new_m.append(tr(nm)), new_v.append(tr(nv))
            continue
        shp = w.shape
        two = lambda a: a.reshape(-1, shp[-1])
        dl, nm, nv = _adamw(two(w), two(g), two(m), two(v), f"adamw_{i}")
        deltas.append(dl.reshape(shp)), new_m.append(nm.reshape(shp)), new_v.append(nv.reshape(shp))

    def flat_pack(arrs):
        flat = jnp.concatenate([a.reshape(-1) for a in arrs])
        pad = (-flat.shape[0]) % (8 * HEAD)
        return jnp.pad(flat, (0, pad)).reshape(-1, HEAD)

    sw = flat_pack([weights[i] for i in small_idx])
    sgr = flat_pack([grads[i] for i in small_idx])
    smo = flat_pack([moms[i] for i in small_idx])
    sve = flat_pack([vels[i] for i in small_idx])
    sdl, snm, snv = _adamw(sw, sgr, smo, sve, "adamw_small")
    off = 0
    for i in small_idx:
        n = weights[i].size
        shp = weights[i].shape
        deltas[i] = sdl.reshape(-1)[off:off + n].reshape(shp)
        new_m[i] = snm.reshape(-1)[off:off + n].reshape(shp)
        new_v[i] = snv.reshape(-1)[off:off + n].reshape(shp)
        off += n

    return (loss, grad_x, *grads, *deltas, *new_m, *new_v)
```

```python
import math

import jax
import jax.numpy as jnp
from jax import lax
from jax.experimental import pallas as pl
from jax.experimental.pallas import tpu as pltpu

F32 = jnp.float32
BF16 = jnp.bfloat16
NDEV = 8
HEAD = 128
CHUNK = 64
SBLK = 256
EPS = 1e-6
LR, B1, B2, ADAM_EPS, WD, STEP = 0.001, 0.9, 0.999, 1e-08, 0.01, 10
NEG = -1e30
MM_VMEM_BUDGET = 40 * 1024 * 1024

NN = (((1,), (0,)), ((), ()))
NT = (((1,), (1,)), ((), ()))
TN = (((0,), (0,)), ((), ()))
BNN = (((2,), (1,)), ((0,), (0,)))
BNT = (((2,), (2,)), ((0,), (0,)))
BTN = (((1,), (1,)), ((0,), (0,)))
MESH = pl.DeviceIdType.MESH


def _dot(a, b, dims=NN):
    return lax.dot_general(a.astype(BF16), b.astype(BF16), dims, preferred_element_type=F32)


def _dot_hilo(a, b01_twice, one_dot):
    hi = a.astype(BF16)
    lo = (a - hi.astype(F32)).astype(BF16)
    if one_dot:
        return lax.dot_general(jnp.concatenate([hi, lo], axis=1), b01_twice, NN, preferred_element_type=F32)
    b01 = b01_twice[:a.shape[1]]
    return (lax.dot_general(hi, b01, NN, preferred_element_type=F32)
            + lax.dot_general(lo, b01, NN, preferred_element_type=F32))


def _pick(dim, cands):
    for c in cands:
        if dim % c == 0:
            return c
    return dim


def _params(sem, vmem_mb=48):
    return pltpu.CompilerParams(dimension_semantics=sem, vmem_limit_bytes=vmem_mb * 1024 * 1024)


def _silu(x):
    return x * jax.nn.sigmoid(x)


def _silu_and_grad(x):
    s = jax.nn.sigmoid(x)
    xs = x * s
    return xs, s + xs * (1.0 - s)


def _mm(a, b, mode, name, out_dtype=F32, res=None, norm_g=None, norm_bwd=None, after=None):
    if mode == "nn":
        (m, k), n = a.shape, b.shape[1]
    elif mode == "nt":
        (m, k), n = a.shape, b.shape[0]
    else:
        (k, m), n = a.shape, b.shape[1]
    rows = norm_g is not None or norm_bwd is not None
    tn = n if rows else _pick(n, (512, 256, 128))
    tk = k if k <= 4096 else max(t for t in range(128, 4097, 128) if k % t == 0)
    nk = k // tk
    out_b = jnp.dtype(out_dtype).itemsize + (res.dtype.itemsize if res is not None else 0)
    out_b += 2 if norm_g is not None else 0
    out_b += 10 if norm_bwd is not None else 0
    for tm in [t for t in range(min(m, 2048), 127, -128) if m % t == 0] + [m]:
        need = 2 * (tm * tk * a.dtype.itemsize + tk * tn * b.dtype.itemsize + tm * tn * out_b) + 4 * tm * tn
        if need <= MM_VMEM_BUDGET:
            break
    dims = {"nn": NN, "nt": NT, "tn": TN}[mode]
    if mode == "tn":
        a_spec = pl.BlockSpec((tk, tm), lambda i, j, kk: (kk, i))
    else:
        a_spec = pl.BlockSpec((tm, tk), lambda i, j, kk: (i, kk))
    if mode == "nt":
        b_spec = pl.BlockSpec((tn, tk), lambda i, j, kk: (j, kk))
    else:
        b_spec = pl.BlockSpec((tk, tn), lambda i, j, kk: (kk, j))
    mn_spec = pl.BlockSpec((tm, tn), lambda i, j, kk: (i, j))
    vec_spec = pl.BlockSpec((1, tn), lambda i, j, kk: (0, j))
    has_res = res is not None
    n_in = 2 + has_res + (1 if norm_g is not None else 0) + (3 if norm_bwd is not None else 0) + (after is not None)

    def body(*refs):
        a_ref, b_ref = refs[:2]
        extra = list(refs[2:n_in])
        outs = refs[n_in:-1]
        acc = refs[-1]
        kk = pl.program_id(2)

        @pl.when(kk == 0)
        def _():
            acc[...] = jnp.zeros_like(acc)

        if norm_bwd is not None:
            @pl.when((kk == 0) & (pl.program_id(0) == 0))
            def _():
                outs[2][...] = jnp.zeros_like(outs[2])

        acc[...] += _dot(a_ref[...], b_ref[...], dims)

        @pl.when(kk == nk - 1)
        def _():
            r = acc[...]
            if has_res:
                r = r + extra.pop(0)[...].astype(F32)
            if norm_g is not None:
                outs[0][...] = r.astype(out_dtype)
                rs = lax.rsqrt(jnp.mean(r * r, axis=-1, keepdims=True) + EPS)
                outs[1][...] = (r * rs * extra.pop(0)[...]).astype(BF16)
            elif norm_bwd is not None:
                xv, gv, dres = extra.pop(0)[...], extra.pop(0)[...], extra.pop(0)[...]
                rs = lax.rsqrt(jnp.mean(xv * xv, axis=-1, keepdims=True) + EPS)
                gdy = r * gv
                dx = dres + rs * gdy - xv * (rs * rs * rs) * jnp.mean(xv * gdy, axis=-1, keepdims=True)
                outs[0][...] = dx
                outs[1][...] = dx.astype(BF16)
                outs[2][...] += jnp.sum(r * xv * rs, axis=0, keepdims=True)
            else:
                outs[0][...] = r.astype(out_dtype)

    ins = [a, b] + ([res] if has_res else [])
    in_specs = [a_spec, b_spec] + ([mn_spec] if has_res else [])
    out_specs, out_shape = [mn_spec], [jax.ShapeDtypeStruct((m, n), out_dtype)]
    sem = ("parallel", "parallel", "arbitrary")
    if norm_g is not None:
        ins.append(norm_g.reshape(1, n))
        in_specs.append(vec_spec)
        out_specs.append(mn_spec)
        out_shape.append(jax.ShapeDtypeStruct((m, n), BF16))
    if norm_bwd is not None:
        x, g, dres = norm_bwd
        ins += [x, g.reshape(1, n), dres]
        in_specs += [mn_spec, vec_spec, mn_spec]
        out_specs += [mn_spec, vec_spec]
        out_shape += [jax.ShapeDtypeStruct((m, n), BF16), jax.ShapeDtypeStruct((1, n), F32)]
        sem = ("arbitrary", "arbitrary", "arbitrary")
    if after is not None:
        ins.append(after)
        in_specs.append(pl.BlockSpec(memory_space=pl.ANY))
    out = pl.pallas_call(
        body, name=name, grid=(m // tm, n // tn, nk), in_specs=in_specs, out_specs=out_specs,
        out_shape=out_shape, scratch_shapes=[pltpu.VMEM((tm, tn), F32)],
        compiler_params=_params(sem))(*ins)
    return out[0] if len(out) == 1 else out


def _rms_fwd(h, g, name):
    s, d = h.shape
    tm = _pick(s, (512, 256, 128))

    def body(h_ref, g_ref, o_ref):
        x = h_ref[...]
        r = lax.rsqrt(jnp.mean(x * x, axis=-1, keepdims=True) + EPS)
        o_ref[...] = (x * r * g_ref[...]).astype(BF16)

    return pl.pallas_call(
        body, name=name, grid=(s // tm,),
        in_specs=[pl.BlockSpec((tm, d), lambda i: (i, 0)), pl.BlockSpec((1, d), lambda i: (0, 0))],
        out_specs=pl.BlockSpec((tm, d), lambda i: (i, 0)),
        out_shape=jax.ShapeDtypeStruct((s, d), BF16), compiler_params=_params(("parallel",)))(h, g.reshape(1, d))


def _headnorm_fwd(x, g, name, scale=1.0, gate=None, gate_col0=0, out_dtype=BF16, width=None, head_major=False):
    if head_major:
        s, d = x.shape[1], x.shape[0] * HEAD
    else:
        s, d = x.shape[0], (width or x.shape[1])
    nh = d // HEAD
    tm = _pick(s, (256, 128))
    has_gate = gate is not None
    gb = gate_col0 // d

    def body(*refs):
        if has_gate:
            x_ref, g_ref, gt_ref, o_ref = refs
        else:
            x_ref, g_ref, o_ref = refs
        gv = g_ref[...]
        for h in range(nh):
            sl = slice(h * HEAD, (h + 1) * HEAD)
            xv = (x_ref[h] if head_major else x_ref[:, sl]).astype(F32)
            r = lax.rsqrt(jnp.mean(xv * xv, axis=-1, keepdims=True) + EPS)
            y = xv * r * gv
            if scale != 1.0:
                y = y * scale
            if has_gate:
                y = y * _silu(gt_ref[:, sl])
            o_ref[:, sl] = y.astype(out_dtype)

    row = pl.BlockSpec((tm, d), lambda i: (i, 0))
    hm = pl.BlockSpec((nh, tm, HEAD), lambda i: (0, i, 0))
    ins = [x, g.reshape(1, HEAD)]
    in_specs = [hm if head_major else row, pl.BlockSpec((1, HEAD), lambda i: (0, 0))]
    if has_gate:
        ins.append(gate)
        in_specs.append(pl.BlockSpec((tm, d), lambda i: (i, gb)))
    return pl.pallas_call(
        body, name=name, grid=(s // tm,), in_specs=in_specs, out_specs=row,
        out_shape=jax.ShapeDtypeStruct((s, d), out_dtype), compiler_params=_params(("parallel",)))(*ins)


def _headnorm_bwd(dy, x, g, name, scale=1.0, gate=None, gate_col0=0, dx_dtype=F32, head_major=False):
    s, d = dy.shape
    nh = d // HEAD
    tm = _pick(s, (256, 128))
    has_gate = gate is not None
    gb = gate_col0 // d

    def body(*refs):
        if has_gate:
            dy_ref, x_ref, g_ref, gt_ref, dx_ref, dg_ref, dgt_ref = refs
        else:
            dy_ref, x_ref, g_ref, dx_ref, dg_ref = refs

        @pl.when(pl.program_id(0) == 0)
        def _():
            dg_ref[...] = jnp.zeros_like(dg_ref)

        gv = g_ref[...]
        dg_acc = jnp.zeros((1, HEAD), F32)
        for h in range(nh):
            sl = slice(h * HEAD, (h + 1) * HEAD)
            xv = (x_ref[h] if head_major else x_ref[:, sl]).astype(F32)
            dyv = dy_ref[:, sl].astype(F32)
            r = lax.rsqrt(jnp.mean(xv * xv, axis=-1, keepdims=True) + EPS)
            if has_gate:
                gt = gt_ref[:, sl]
                act, dact = _silu_and_grad(gt)
                dgt_ref[:, sl] = (dyv * (xv * r * gv) * dact).astype(dgt_ref.dtype)
                dn = dyv * act
            else:
                dn = dyv
            if scale != 1.0:
                dn = dn * scale
            gdn = dn * gv
            mean_t = jnp.mean(xv * gdn, axis=-1, keepdims=True)
            dxv = (r * gdn - xv * (r * r * r) * mean_t).astype(dx_dtype)
            if head_major:
                dx_ref[h] = dxv
            else:
                dx_ref[:, sl] = dxv
            dg_acc = dg_acc + jnp.sum(dn * xv * r, axis=0, keepdims=True)
        dg_ref[...] += dg_acc

    row = pl.BlockSpec((tm, d), lambda i: (i, 0))
    hm = pl.BlockSpec((nh, tm, HEAD), lambda i: (0, i, 0))
    vec = pl.BlockSpec((1, HEAD), lambda i: (0, 0))
    ins = [dy, x, g.reshape(1, HEAD)]
    in_specs = [row, hm if head_major else row, vec]
    out_specs = [hm if head_major else row, vec]
    dx_shape = (nh, s, HEAD) if head_major else (s, d)
    out_shape = [jax.ShapeDtypeStruct(dx_shape, dx_dtype), jax.ShapeDtypeStruct((1, HEAD), F32)]
    if has_gate:
        ins.append(gate)
        in_specs.append(pl.BlockSpec((tm, d), lambda i: (i, gb)))
        out_specs.append(pl.BlockSpec((tm, d), lambda i: (i, gb)))
        out_shape.append(jax.ShapeDtypeStruct((s, gate.shape[1]), BF16))
    return pl.pallas_call(
        body, name=name, grid=(s // tm,), in_specs=in_specs, out_specs=out_specs, out_shape=out_shape,
        compiler_params=_params(("arbitrary",)))(*ins)


def _kv_grad(dks, dvs, kv, g, name):
    s, d = dks[0].shape
    nh = d // HEAD
    tm = _pick(s, (256, 128))
    n = len(dks)

    def body(*refs):
        dk_refs, dv_refs = refs[:n], refs[n:2 * n]
        kv_ref, g_ref, o_ref, dg_ref = refs[2 * n:]

        @pl.when(pl.program_id(0) == 0)
        def _():
            dg_ref[...] = jnp.zeros_like(dg_ref)

        gv = g_ref[...]
        dg_acc = jnp.zeros((1, HEAD), F32)
        for h in range(nh):
            sl = slice(h * HEAD, (h + 1) * HEAD)
            xv = kv_ref[:, sl]
            dyv = sum(r[:, sl] for r in dk_refs)
            r = lax.rsqrt(jnp.mean(xv * xv, axis=-1, keepdims=True) + EPS)
            gdn = dyv * gv
            mean_t = jnp.mean(xv * gdn, axis=-1, keepdims=True)
            o_ref[:, sl] = (r * gdn - xv * (r * r * r) * mean_t).astype(BF16)
            dg_acc = dg_acc + jnp.sum(dyv * xv * r, axis=0, keepdims=True)
        o_ref[:, d:] = sum(r[...] for r in dv_refs).astype(BF16)
        dg_ref[...] += dg_acc

    row = pl.BlockSpec((tm, d), lambda i: (i, 0))
    vec = pl.BlockSpec((1, HEAD), lambda i: (0, 0))
    return pl.pallas_call(
        body, name=name, grid=(s // tm,), in_specs=[row] * (2 * n) + [row, vec],
        out_specs=[pl.BlockSpec((tm, 2 * d), lambda i: (i, 0)), vec],
        out_shape=[jax.ShapeDtypeStruct((s, 2 * d), BF16), jax.ShapeDtypeStruct((1, HEAD), F32)],
        compiler_params=_params(("arbitrary",)))(*dks, *dvs, kv, g.reshape(1, HEAD))


def _swiglu_fwd(hn, wf_t, name):
    s, d = hn.shape
    f = wf_t.shape[0] // 2
    tm = _pick(s, (1024, 512, 256, 128))
    tn = _pick(f, (512, 256, 128))
    nj = f // tn

    def body(a_ref, wg_ref, wu_ref, act_ref, g_ref, u_ref):
        a = a_ref[...]
        g = _dot(a, wg_ref[...], NT)
        u = _dot(a, wu_ref[...], NT)
        act_ref[...] = (_silu(g) * u).astype(BF16)
        g_ref[...] = g.astype(BF16)
        u_ref[...] = u.astype(BF16)

    o_spec = pl.BlockSpec((tm, tn), lambda i, j: (i, j))
    sds = jax.ShapeDtypeStruct((s, f), BF16)
    return pl.pallas_call(
        body, name=name, grid=(s // tm, nj),
        in_specs=[pl.BlockSpec((tm, d), lambda i, j: (i, 0)), pl.BlockSpec((tn, d), lambda i, j: (j, 0)),
                  pl.BlockSpec((tn, d), lambda i, j: (j + nj, 0))],
        out_specs=[o_spec, o_spec, o_spec], out_shape=[sds, sds, sds],
        compiler_params=_params(("parallel", "parallel")))(hn, wf_t, wf_t)


def _swiglu_bwd(dh, w_out, g, u, name):
    s, d = dh.shape
    f = w_out.shape[0]
    tm = _pick(s, (1024, 512, 256, 128))
    tn = _pick(f, (512, 256, 128))

    def body(dh_ref, w_ref, g_ref, u_ref, dgu_ref):
        j = pl.program_id(1)
        dact = _dot(dh_ref[...], w_ref[...], NT)
        gv = g_ref[...].astype(F32)
        uv = u_ref[...].astype(F32)
        sg, dsg = _silu_and_grad(gv)
        dgu_ref[:, pl.ds(pl.multiple_of(j * tn, HEAD), tn)] = (dact * uv * dsg).astype(BF16)
        dgu_ref[:, pl.ds(pl.multiple_of(f + j * tn, HEAD), tn)] = (dact * sg).astype(BF16)

    o_spec = pl.BlockSpec((tm, tn), lambda i, j: (i, j))
    return pl.pallas_call(
        body, name=name, grid=(s // tm, f // tn),
        in_specs=[pl.BlockSpec((tm, d), lambda i, j: (i, 0)), pl.BlockSpec((tn, d), lambda i, j: (j, 0)), o_spec, o_spec],
        out_specs=pl.BlockSpec((tm, 2 * f), lambda i, j: (i, 0)), out_shape=jax.ShapeDtypeStruct((s, 2 * f), BF16),
        compiler_params=_params(("parallel", "arbitrary")))(dh, w_out, g, u)


def _ple_fwd(h, hn, p, w_gate, wp_t, name, norm_gs=()):
    s, d = h.shape
    pd = p.shape[1]
    tm = _pick(s, (512, 256, 128))
    ng = len(norm_gs)

    def body(h_ref, hn_ref, p_ref, wg_ref, wp_ref, *rest):
        g_refs, (o_ref, gp_ref, pp_ref), n_refs = rest[:ng], rest[ng:ng + 3], rest[ng + 3:]
        gpre = _dot(hn_ref[...], wg_ref[...], NN)
        pp = _dot(p_ref[...], wp_ref[...], NT)
        o = h_ref[...] + pp * jax.nn.sigmoid(gpre)
        o_ref[...] = o
        gp_ref[...] = gpre.astype(BF16)
        pp_ref[...] = pp.astype(BF16)
        if ng:
            on = o * lax.rsqrt(jnp.mean(o * o, axis=-1, keepdims=True) + EPS)
            for g_ref, n_ref in zip(g_refs, n_refs):
                n_ref[...] = (on * g_ref[...]).astype(BF16)

    row = pl.BlockSpec((tm, d), lambda i: (i, 0))
    vec = pl.BlockSpec((1, d), lambda i: (0, 0))
    bf = jax.ShapeDtypeStruct((s, d), BF16)
    return pl.pallas_call(
        body, name=name, grid=(s // tm,),
        in_specs=[row, row, pl.BlockSpec((tm, pd), lambda i: (i, 0)), pl.BlockSpec((d, d), lambda i: (0, 0)),
                  pl.BlockSpec((d, pd), lambda i: (0, 0))] + [vec] * ng,
        out_specs=[row] * (3 + ng), out_shape=[jax.ShapeDtypeStruct((s, d), F32), bf, bf] + [bf] * ng,
        compiler_params=_params(("parallel",)))(h, hn, p, w_gate, wp_t, *[g.reshape(1, d) for g in norm_gs])


def _ple_bwd(dh, gpre, pp, name):
    s, d = dh.shape
    tm = _pick(s, (512, 256, 128))

    def body(dh_ref, gp_ref, pp_ref, dgp_ref, dpp_ref):
        dv = dh_ref[...]
        sig = jax.nn.sigmoid(gp_ref[...].astype(F32))
        ppv = pp_ref[...].astype(F32)
        dpp_ref[...] = (dv * sig).astype(BF16)
        dgp_ref[...] = (dv * ppv * sig * (1.0 - sig)).astype(BF16)

    row = pl.BlockSpec((tm, d), lambda i: (i, 0))
    sds = jax.ShapeDtypeStruct((s, d), BF16)
    return pl.pallas_call(
        body, name=name, grid=(s // tm,), in_specs=[row, row, row], out_specs=[row, row], out_shape=[sds, sds],
        compiler_params=_params(("parallel",)))(dh, gpre, pp)


def _loss_fwd_bwd(y, t, name):
    s, d = y.shape
    tm = _pick(s, (512, 256, 128))

    def body(y_ref, t_ref, dy_ref, l_ref):
        @pl.when(pl.program_id(0) == 0)
        def _():
            l_ref[...] = jnp.zeros_like(l_ref)

        e = y_ref[...] - t_ref[...]
        dy_ref[...] = e * (1.0 / d)
        l_ref[...] += jnp.sum(e * e, axis=0, keepdims=True) * (0.5 / d)

    row = pl.BlockSpec((tm, d), lambda i: (i, 0))
    vec = pl.BlockSpec((1, d), lambda i: (0, 0))
    return pl.pallas_call(
        body, name=name, grid=(s // tm,), in_specs=[row, row], out_specs=[row, vec],
        out_shape=[jax.ShapeDtypeStruct((s, d), F32), jax.ShapeDtypeStruct((1, d), F32)],
        compiler_params=_params(("arbitrary",)))(y, t)


PADR = 8
CONV_ROWS = 256


def _conv_fwd(proj, w_conv, d, name):
    s = proj.shape[0]
    nh = d // HEAD
    kw = w_conv.shape[0]
    qscale = HEAD ** -0.5

    tr = _pick(s, (CONV_ROWS,))

    def body(x_ref, w_ref, o_ref, xp):
        kind = pl.program_id(0) // nh
        xp[0:PADR, :] = jnp.zeros((PADR, HEAD), F32)
        xp[PADR:, :] = x_ref[...]
        taps = [w_ref[j:j + 1, :] for j in range(kw)]
        for r0 in range(0, s, tr):
            acc = jnp.zeros((tr, HEAD), F32)
            for j in range(kw):
                acc = acc + taps[j] * xp[r0 + PADR - (kw - 1) + j:r0 + PADR - (kw - 1) + j + tr, :]
            a = _silu(acc)
            r = lax.rsqrt(jnp.sum(a * a, axis=-1, keepdims=True) + EPS)
            fac = jnp.where(kind == 0, r * qscale, jnp.where(kind == 1, r, jnp.ones_like(r)))
            o_ref[r0:r0 + tr, :] = a * fac

    blk = pl.BlockSpec((s, HEAD), lambda c: (0, c))
    hm = pl.BlockSpec((None, s, HEAD), lambda c: (c, 0, 0))
    return pl.pallas_call(
        body, name=name, grid=(3 * nh,), in_specs=[blk, pl.BlockSpec((kw, HEAD), lambda c: (0, c))], out_specs=hm,
        out_shape=jax.ShapeDtypeStruct((3 * nh, s, HEAD), F32), scratch_shapes=[pltpu.VMEM((s + PADR, HEAD), F32)],
        compiler_params=_params(("parallel",)))(proj, w_conv)


def _conv_bwd(dqkv, proj, w_conv, d, name, dproj):
    s = proj.shape[0]
    nh = d // HEAD
    kw = w_conv.shape[0]
    qscale = HEAD ** -0.5

    tr = _pick(s, (CONV_ROWS,))

    def body(dy_ref, x_ref, w_ref, _, dx_ref, dw_ref, xp, dp):
        kind = pl.program_id(0) // nh
        xp[0:PADR, :] = jnp.zeros((PADR, HEAD), F32)
        xp[PADR:, :] = x_ref[...]
        dp[s:, :] = jnp.zeros((PADR, HEAD), F32)
        taps = [w_ref[j:j + 1, :] for j in range(kw)]
        sc = jnp.where(kind == 0, qscale, 1.0)
        dws = [jnp.zeros((1, HEAD), F32) for _ in range(kw)]
        for r0 in range(0, s, tr):
            acc = jnp.zeros((tr, HEAD), F32)
            for j in range(kw):
                acc = acc + taps[j] * xp[r0 + PADR - (kw - 1) + j:r0 + PADR - (kw - 1) + j + tr, :]
            a, da_dacc = _silu_and_grad(acc)
            dy = dy_ref[r0:r0 + tr, :]
            r = lax.rsqrt(jnp.sum(a * a, axis=-1, keepdims=True) + EPS)
            dyn = dy * sc
            da_norm = r * dyn - a * (r * r * r) * jnp.sum(a * dyn, axis=-1, keepdims=True)
            dacc = jnp.where(kind == 2, dy, da_norm) * da_dacc
            dp[r0:r0 + tr, :] = dacc
            for j in range(kw):
                sh = kw - 1 - j
                dws[j] = dws[j] + jnp.sum(dacc * xp[r0 + PADR - sh:r0 + PADR - sh + tr, :], axis=0, keepdims=True)
        for j in range(kw):
            dw_ref[j:j + 1, :] = dws[j]
        for r0 in range(0, s, tr):
            dx = jnp.zeros((tr, HEAD), F32)
            for j in range(kw):
                sh = kw - 1 - j
                dx = dx + taps[j] * dp[r0 + sh:r0 + sh + tr, :]
            dx_ref[r0:r0 + tr, :] = dx.astype(BF16)

    blk = pl.BlockSpec((s, HEAD), lambda c: (0, c))
    hm = pl.BlockSpec((None, s, HEAD), lambda c: (c, 0, 0))
    wblk = pl.BlockSpec((kw, HEAD), lambda c: (0, c))
    return pl.pallas_call(
        body, name=name, grid=(3 * nh,), in_specs=[hm, blk, wblk, pl.BlockSpec(memory_space=pl.ANY)],
        out_specs=[blk, wblk], input_output_aliases={3: 0},
        out_shape=[jax.ShapeDtypeStruct(dproj.shape, BF16), jax.ShapeDtypeStruct((kw, 3 * d), F32)],
        scratch_shapes=[pltpu.VMEM((s + PADR, HEAD), F32), pltpu.VMEM((s + PADR, HEAD), F32)],
        compiler_params=_params(("parallel",)))(dqkv, proj, w_conv, dproj)


def _softplus(x):
    return jnp.maximum(x, 0.0) + jnp.log(1.0 + jnp.exp(-jnp.abs(x)))


def _gates_fwd(pab, a_log, dt_bias, nh, name):
    s = pab.shape[0]
    tm = _pick(s, (512, 256, 128))

    def body(x_ref, al_ref, dt_ref, o_ref):
        x = x_ref[...]
        lane = lax.broadcasted_iota(jnp.int32, x.shape, 1)
        g = -jnp.exp(al_ref[...]) * _softplus(x + dt_ref[...])
        o_ref[...] = jnp.where(lane < nh, g, jnp.where(lane < 2 * nh, jax.nn.sigmoid(x), 0.0))

    row = pl.BlockSpec((tm, HEAD), lambda i: (i, 0))
    vec = pl.BlockSpec((1, HEAD), lambda i: (0, 0))
    return pl.pallas_call(
        body, name=name, grid=(s // tm,), in_specs=[row, vec, vec], out_specs=row,
        out_shape=jax.ShapeDtypeStruct((s, HEAD), F32), compiler_params=_params(("parallel",)))(pab, a_log, dt_bias)


def _gates_bwd(dgb, pab, a_log, dt_bias, nh, name):
    s = pab.shape[0]
    tm = _pick(s, (512, 256, 128))

    def body(d_ref, x_ref, al_ref, dt_ref, dx_ref, dal_ref, ddt_ref):
        @pl.when(pl.program_id(0) == 0)
        def _():
            dal_ref[...] = jnp.zeros_like(dal_ref)
            ddt_ref[...] = jnp.zeros_like(ddt_ref)

        x = x_ref[...]
        dv = d_ref[...]
        lane = lax.broadcasted_iota(jnp.int32, x.shape, 1)
        ea = jnp.exp(al_ref[...])
        xs = x + dt_ref[...]
        g = -ea * _softplus(xs)
        dxs = jnp.where(lane < nh, dv * (-ea) * jax.nn.sigmoid(xs), 0.0)
        sg = jax.nn.sigmoid(x)
        dxb = jnp.where((lane >= nh) & (lane < 2 * nh), dv * sg * (1.0 - sg), 0.0)
        dx_ref[...] = (dxs + dxb).astype(BF16)
        dal_ref[...] += jnp.sum(jnp.where(lane < nh, dv * g, 0.0), axis=0, keepdims=True)
        ddt_ref[...] += jnp.sum(dxs, axis=0, keepdims=True)

    row = pl.BlockSpec((tm, HEAD), lambda i: (i, 0))
    vec = pl.BlockSpec((1, HEAD), lambda i: (0, 0))
    return pl.pallas_call(
        body, name=name, grid=(s // tm,), in_specs=[row, row, vec, vec], out_specs=[row, vec, vec],
        out_shape=[jax.ShapeDtypeStruct((s, HEAD), BF16), jax.ShapeDtypeStruct((1, HEAD), F32),
                   jax.ShapeDtypeStruct((1, HEAD), F32)],
        compiler_params=_params(("arbitrary",)))(dgb, pab, a_log, dt_bias)


def _tri_inv(a_low, eye_f):
    n = -a_low
    p = eye_f + n
    steps = int(math.log2(a_low.shape[-1])) - 1
    for _ in range(steps):
        n = _dot(n, n, BNN)
        p = p + _dot(p, n, BNN)
    return p


def _lane_col(x, lane, idx):
    return jnp.sum(jnp.where(lane == idx, x, 0.0), axis=1, keepdims=True)


def _head_cols(gbv, lo, nh):
    lane = lax.broadcasted_iota(jnp.int32, gbv.shape, 1)
    return jnp.stack([_lane_col(gbv, lane, lo + h) for h in range(nh)], axis=0)


def _gdn_chunk(q, k, v, g_col, beta_col, st):
    c = q.shape[1]
    r_i = lax.broadcasted_iota(jnp.int32, (c, c), 0)
    c_i = lax.broadcasted_iota(jnp.int32, (c, c), 1)
    incl = c_i <= r_i
    strict = c_i < r_i
    eye = c_i == r_i
    g_row = jnp.sum(jnp.where(eye, g_col, 0.0), axis=1, keepdims=True)
    gc_col = jnp.sum(jnp.where(incl, g_row, 0.0), axis=2, keepdims=True)
    gc_row = jnp.sum(jnp.where(eye, gc_col, 0.0), axis=1, keepdims=True)
    g_last = jnp.sum(g_col, axis=1, keepdims=True)
    decay = jnp.exp(jnp.where(incl, gc_col - gc_row, NEG))
    kk = _dot(k, k, BNT)
    a_low = jnp.where(strict, beta_col * kk * decay, 0.0)
    t_inv = _tri_inv(a_low, eye.astype(F32))
    e_g = jnp.exp(gc_col)
    bk = beta_col * e_g
    rhs = jnp.concatenate([v * beta_col, k * bk], axis=2)
    sol = _dot(t_inv, rhs, BNN)
    u, w = sol[:, :, :HEAD], sol[:, :, HEAD:]
    qk_raw = _dot(q, k, BNT)
    qk = qk_raw * decay
    q_dec = q * e_g
    e2 = jnp.exp(g_last - gc_col)
    k_dec = k * e2
    gl = jnp.exp(g_last)
    ws = _dot(jnp.concatenate([w, q_dec], axis=1), st, BNN)
    v_new = u - ws[:, :c]
    o = ws[:, c:] + _dot(qk, v_new, BNN)
    st_new = st * gl + _dot(k_dec, v_new, BTN)
    inter = dict(incl=incl, strict=strict, eye=eye, decay=decay, kk=kk, t_inv=t_inv, e_g=e_g, bk=bk, sol=sol, w=w,
                 qk_raw=qk_raw, qk=qk, q_dec=q_dec, e2=e2, k_dec=k_dec, gl=gl, v_new=v_new, c_i=c_i, r_i=r_i)
    return o, st_new, inter


def _gdn_fwd(qkv, gb, nh, name):
    s = qkv.shape[1]
    nc = s // CHUNK

    def body(q_ref, k_ref, v_ref, gb_ref, o_ref, st_ref, state):
        @pl.when(pl.program_id(0) == 0)
        def _():
            state[...] = jnp.zeros_like(state)

        gbv = gb_ref[...]
        st = state[...]
        st_ref[...] = st
        o, st_new, _ = _gdn_chunk(q_ref[...], k_ref[...], v_ref[...], _head_cols(gbv, 0, nh), _head_cols(gbv, nh, nh), st)
        o_ref[...] = o
        state[...] = st_new

    def qspec(part):
        return pl.BlockSpec((nh, CHUNK, HEAD), lambda n: (part, n, 0))

    return pl.pallas_call(
        body, name=name, grid=(nc,),
        in_specs=[qspec(0), qspec(1), qspec(2), pl.BlockSpec((CHUNK, HEAD), lambda n: (n, 0))],
        out_specs=[qspec(0), pl.BlockSpec((None, nh, HEAD, HEAD), lambda n: (n, 0, 0, 0))],
        out_shape=[jax.ShapeDtypeStruct((nh, s, HEAD), F32), jax.ShapeDtypeStruct((nc, nh, HEAD, HEAD), F32)],
        scratch_shapes=[pltpu.VMEM((nh, HEAD, HEAD), F32)],
        compiler_params=_params(("arbitrary",)))(qkv, qkv, qkv, gb)


def _gdn_bwd(qkv, gb, do, states, nh, name):
    s = qkv.shape[1]
    nc = s // CHUNK
    c = CHUNK

    def body(q_ref, k_ref, v_ref, gb_ref, do_ref, st_ref, dqkv_ref, dgb_ref, dstate):
        @pl.when(pl.program_id(0) == 0)
        def _():
            dstate[...] = jnp.zeros_like(dstate)

        gbv = gb_ref[...]
        lane = lax.broadcasted_iota(jnp.int32, gbv.shape, 1)
        q, k, v = q_ref[...], k_ref[...], v_ref[...]
        beta_col = _head_cols(gbv, nh, nh)
        st = st_ref[...]
        dst = dstate[...]
        dov = do_ref[...]
        _, _, it = _gdn_chunk(q, k, v, _head_cols(gbv, 0, nh), beta_col, st)
        incl, strict, eye, decay = it["incl"], it["strict"], it["eye"], it["decay"]
        dv_new = _dot(it["qk"], dov, BTN) + _dot(it["k_dec"], dst, BNN)
        d_qk = _dot(dov, it["v_new"], BNT)
        dd = _dot(jnp.concatenate([dov, -dv_new], axis=1), st, BNT)
        dq_dec, dw = dd[:, :c], dd[:, c:]
        dst_new = _dot(it["q_dec"], dov, BTN) + it["gl"] * dst - _dot(it["w"], dv_new, BTN)
        dgl = jnp.sum(jnp.sum(dst * st, axis=2, keepdims=True), axis=1, keepdims=True)
        dk_dec = _dot(it["v_new"], dst, BNT)
        dsol = jnp.concatenate([dv_new, dw], axis=2)
        drhs = _dot(it["t_inv"], dsol, BTN)
        d_a = jnp.where(strict, -_dot(drhs, it["sol"], BNT), 0.0)
        drhs_u, drhs_w = drhs[:, :, :HEAD], drhs[:, :, HEAD:]
        dvh = beta_col * drhs_u
        rw_k = jnp.sum(drhs_w * k, axis=2, keepdims=True)
        dbeta = jnp.sum(drhs_u * v, axis=2, keepdims=True) + it["e_g"] * rw_k
        dkh = it["bk"] * drhs_w
        dgc_col = it["bk"] * rw_k
        dkk = d_a * beta_col * decay
        dbeta = dbeta + jnp.sum(d_a * it["kk"] * decay, axis=2, keepdims=True)
        ddecay = d_a * beta_col * it["kk"]
        dkh = dkh + _dot(dkk, k, BNN) + _dot(dkk, k, BTN)
        dqk_raw = d_qk * decay
        ddecay = ddecay + d_qk * it["qk_raw"]
        dqh = _dot(dqk_raw, k, BNN)
        dkh = dkh + _dot(dqk_raw, q, BTN)
        ddm = jnp.where(incl, ddecay * decay, 0.0)
        dgc_col = dgc_col + jnp.sum(ddm, axis=2, keepdims=True)
        dgc_row = -jnp.sum(ddm, axis=1, keepdims=True)
        dqh = dqh + dq_dec * it["e_g"]
        dgc_col = dgc_col + jnp.sum(dq_dec * it["q_dec"], axis=2, keepdims=True)
        dkh = dkh + dk_dec * it["e2"]
        tmp = jnp.sum(dk_dec * it["k_dec"], axis=2, keepdims=True)
        dgc_col = dgc_col - tmp
        dg_last = jnp.sum(tmp, axis=1, keepdims=True) + dgl * it["gl"]
        dgc_tot_row = dgc_row + jnp.sum(jnp.where(eye, dgc_col, 0.0), axis=1, keepdims=True)
        dg_col = jnp.sum(jnp.where(it["c_i"] >= it["r_i"], dgc_tot_row, 0.0), axis=2, keepdims=True) + dg_last
        dqkv_ref[0] = dqh
        dqkv_ref[1] = dkh
        dqkv_ref[2] = dvh
        dstate[...] = dst_new
        dgb_acc = jnp.zeros(gbv.shape, F32)
        for h in range(nh):
            dgb_acc = jnp.where(lane == h, dg_col[h], jnp.where(lane == nh + h, dbeta[h], dgb_acc))
        dgb_ref[...] = dgb_acc

    def rev(part):
        return pl.BlockSpec((nh, CHUNK, HEAD), lambda n: (part, nc - 1 - n, 0))

    gspec = pl.BlockSpec((CHUNK, HEAD), lambda n: (nc - 1 - n, 0))
    dqkv, dgb = pl.pallas_call(
        body, name=name, grid=(nc,),
        in_specs=[rev(0), rev(1), rev(2), gspec, rev(0),
                  pl.BlockSpec((None, nh, HEAD, HEAD), lambda n: (nc - 1 - n, 0, 0, 0))],
        out_specs=[pl.BlockSpec((3, nh, CHUNK, HEAD), lambda n: (0, 0, nc - 1 - n, 0)), gspec],
        out_shape=[jax.ShapeDtypeStruct((3, nh, s, HEAD), F32), jax.ShapeDtypeStruct((s, HEAD), F32)],
        scratch_shapes=[pltpu.VMEM((nh, HEAD, HEAD), F32)],
        compiler_params=_params(("arbitrary",)))(qkv, qkv, qkv, gb, do, states)
    return dqkv.reshape(3 * nh, s, HEAD), dgb


SB_TQ_FWD = 1024
SB_TQ = 512


def _tri01(rel):
    j_i = lax.broadcasted_iota(jnp.int32, (2 * SBLK, SBLK), 0) & (SBLK - 1)
    s_i = lax.broadcasted_iota(jnp.int32, (2 * SBLK, SBLK), 1)
    return rel(j_i, s_i).astype(BF16)


SB_HP = 2


def _each(fn, *lists):
    return [fn(*xs) for xs in zip(*lists)]


def _sb_scores(qts, kblks, mask, csums, rhs01, one_dot):
    zs = _each(lambda qt, kb: _dot(qt, kb, NT), qts, kblks)
    es = _each(lambda z: jnp.exp(-jnp.abs(z)), zs)
    sps = _each(lambda z, e: jnp.maximum(z, 0.0) + jnp.log(1.0 + e), zs, es)
    lns = _each(lambda sp: -sp if mask is None else jnp.where(mask, -sp, 0.0), sps)
    sts = _each(lambda ln: _dot_hilo(ln, rhs01, one_dot), lns)
    wgts = _each(lambda z, sp, st, cs: jnp.exp((z - sp) + st + cs), zs, sps, sts, csums)
    if mask is not None:
        wgts = _each(lambda w: jnp.where(mask, w, 0.0), wgts)
    return zs, es, wgts, lns


def _band_mask(rows, j, row0):
    r_i = lax.broadcasted_iota(jnp.int32, (rows, SBLK), 0)
    c_i = lax.broadcasted_iota(jnp.int32, (rows, SBLK), 1)
    return (j * SBLK + c_i) < (row0 + r_i)


def _sb_fwd(q, k, v, name):
    s, d = q.shape
    nh = d // HEAD
    tq = min(SB_TQ_FWD, s)
    nb = tq // SBLK

    hp = SB_HP
    heads = [slice(h * HEAD, (h + 1) * HEAD) for h in range(hp)]

    def body(q_ref, k_ref, v_ref, o_ref, c_ref, acc, cs):
        qb = pl.program_id(1)
        lane = lax.broadcasted_iota(jnp.int32, (tq, HEAD), 1)
        after = _tri01(lambda j, t: j > t)
        acc[...] = jnp.zeros_like(acc)
        cs[...] = jnp.zeros_like(cs)
        c_ref[...] = jnp.zeros_like(c_ref)

        def process(rs, kb, mask):
            keys = pl.ds(pl.multiple_of(kb * SBLK, SBLK), SBLK)
            csums = [cs[h, rs, :] for h in range(hp)]
            _, _, wgts, lns = _sb_scores([q_ref[rs, hs] for hs in heads], [k_ref[keys, hs] for hs in heads], mask, csums, after, True)
            pvs = _each(lambda w, hs: _dot(w, v_ref[keys, hs]), wgts, heads)
            tots = _each(lambda ln: jnp.sum(ln, axis=1, keepdims=True), lns)
            for h, hs in enumerate(heads):
                acc[h, rs, :] += pvs[h]
                c_ref[rs, hs] = jnp.where(lane[rs, :] == kb, csums[h], c_ref[rs, hs])
                cs[h, rs, :] = csums[h] + tots[h]

        for j in reversed(range(nb)):
            process(slice(j * SBLK, tq), qb * nb + j, _band_mask(tq - j * SBLK, j, j * SBLK))

        def step(it, carry):
            process(slice(0, tq), qb * nb - 1 - it, None)
            return carry

        lax.fori_loop(0, qb * nb, step, 0)
        for h, hs in enumerate(heads):
            o_ref[:, hs] = acc[h].astype(BF16)

    qspec = pl.BlockSpec((tq, hp * HEAD), lambda h, i: (i, h))
    kspec = pl.BlockSpec((s, hp * HEAD), lambda h, i: (0, h))
    return pl.pallas_call(
        body, name=name, grid=(nh // hp, s // tq), in_specs=[qspec, kspec, kspec], out_specs=[qspec, qspec],
        out_shape=[jax.ShapeDtypeStruct((s, d), BF16), jax.ShapeDtypeStruct((s, d), F32)],
        scratch_shapes=[pltpu.VMEM((hp, tq, HEAD), F32), pltpu.VMEM((hp, tq, 1), F32)],
        compiler_params=_params(("parallel", "arbitrary")))(q, k, v)


def _sb_bwd(q, k, v, do, ctab, name):
    s, d = q.shape
    nh = d // HEAD
    tq = min(SB_TQ, s)
    nb = tq // SBLK

    hp = SB_HP
    heads = [slice(h * HEAD, (h + 1) * HEAD) for h in range(hp)]

    def body(q_ref, k_ref, v_ref, do_ref, c_ref, dq_ref, dk_ref, dv_ref, ps):
        qb = pl.program_id(1)

        @pl.when(qb == 0)
        def _():
            dk_ref[...] = jnp.zeros_like(dk_ref)
            dv_ref[...] = jnp.zeros_like(dv_ref)

        dq_ref[...] = jnp.zeros_like(dq_ref)
        ps[...] = jnp.zeros_like(ps)
        lane = lax.broadcasted_iota(jnp.int32, (tq, HEAD), 1)
        after = _tri01(lambda j, t: j > t)
        before = _tri01(lambda j, t: j < t)

        def process(rs, kb, mask):
            keys = pl.ds(pl.multiple_of(kb * SBLK, SBLK), SBLK)
            kblks = [k_ref[keys, hs] for hs in heads]
            qts = [q_ref[rs, hs] for hs in heads]
            dots = [do_ref[rs, hs] for hs in heads]
            csums = [_lane_col(c_ref[rs, hs], lane[rs, :], kb) for hs in heads]
            zs, es, wgts, _ = _sb_scores(qts, kblks, mask, csums, after, False)
            dlws = _each(lambda dt, hs, w: _dot(dt, v_ref[keys, hs], NT) * w, dots, heads, wgts)
            pts = _each(lambda dlw: _dot_hilo(dlw, before, False), dlws)
            pfxs = [ps[h, rs, :] for h in range(hp)]
            rs_ = _each(lambda e: 1.0 / (1.0 + e), es)
            sigs = _each(lambda z, e, r: jnp.where(z >= 0.0, r, e * r), zs, es, rs_)
            dzs = _each(lambda dlw, sig, pfx, pt: dlw * (1.0 - sig) - sig * (pfx + pt), dlws, sigs, pfxs, pts)
            tots = _each(lambda dlw: jnp.sum(dlw, axis=1, keepdims=True), dlws)
            if mask is not None:
                dzs = _each(lambda dz: jnp.where(mask, dz, 0.0), dzs)
            dqs = _each(lambda dz, kb_: _dot(dz, kb_), dzs, kblks)
            dks = _each(lambda dz, qt: _dot(dz, qt, TN), dzs, qts)
            dvs = _each(lambda w, dt: _dot(w, dt, TN), wgts, dots)
            for h, hs in enumerate(heads):
                dq_ref[rs, hs] += dqs[h]
                dk_ref[keys, hs] += dks[h]
                dv_ref[keys, hs] += dvs[h]
                ps[h, rs, :] = pfxs[h] + tots[h]

        def step(kb, carry):
            process(slice(0, tq), kb, None)
            return carry

        lax.fori_loop(0, qb * nb, step, 0)
        for j in range(nb):
            process(slice(j * SBLK, tq), qb * nb + j, _band_mask(tq - j * SBLK, j, j * SBLK))

    qspec = pl.BlockSpec((tq, hp * HEAD), lambda h, i: (i, h))
    kspec = pl.BlockSpec((s, hp * HEAD), lambda h, i: (0, h))
    sds = jax.ShapeDtypeStruct((s, d), F32)
    return pl.pallas_call(
        body, name=name, grid=(nh // hp, s // tq), in_specs=[qspec, kspec, kspec, qspec, qspec],
        out_specs=[qspec, kspec, kspec], out_shape=[sds, sds, sds],
        scratch_shapes=[pltpu.VMEM((hp, tq, 1), F32)],
        compiler_params=_params(("parallel", "arbitrary")))(q, k, v, do, ctab)


def _my_index():
    return 4 * lax.axis_index("x") + 2 * lax.axis_index("y") + lax.axis_index("c")


def _all_gather(x_shard, name):
    m_per, n = x_shard.shape

    def body(x_ref, out_ref, send_sems, recv_sems, local_sem):
        x, y, c = lax.axis_index("x"), lax.axis_index("y"), lax.axis_index("c")
        me, sibling = (x, y, c), (x, y, 1 - c)
        chips = [(1 - x, y), (x, 1 - y), (1 - x, 1 - y)]

        def rows(px, py, pc):
            return out_ref.at[pl.ds((4 * px + 2 * py + pc) * m_per, m_per), :]

        def copy(k, block, to, src=None):
            return pltpu.make_async_remote_copy(
                src_ref=rows(*block) if src is None else src, dst_ref=rows(*block),
                send_sem=send_sems.at[k], recv_sem=recv_sems.at[k], device_id=to, device_id_type=MESH)

        mine = pltpu.make_async_copy(x_ref, rows(*me), local_sem)
        mine.start()
        first = [copy(0, me, sibling, src=x_ref)]
        first += [copy(1 + j, me, (*chip, c), src=x_ref) for j, chip in enumerate(chips)]
        for cp in first:
            cp.start()
        passed = [copy(4 + j, (*chip, c), sibling) for j, chip in enumerate(chips)]
        for j, chip in enumerate(chips):
            copy(1 + j, (*chip, c), me).wait_recv()
            passed[j].start()
        copy(0, sibling, me).wait_recv()
        for j, chip in enumerate(chips):
            copy(4 + j, (*chip, 1 - c), me).wait_recv()
        for cp in first + passed:
            cp.wait_send()
        mine.wait()

    return pl.pallas_call(
        body, name=name, out_shape=jax.ShapeDtypeStruct((NDEV * m_per, n), x_shard.dtype),
        in_specs=[pl.BlockSpec(memory_space=pl.ANY)], out_specs=pl.BlockSpec(memory_space=pl.ANY),
        scratch_shapes=[pltpu.SemaphoreType.DMA((7,)), pltpu.SemaphoreType.DMA((7,)), pltpu.SemaphoreType.DMA],
    )(x_shard)


HBM_SPEC = pl.BlockSpec(memory_space=pltpu.HBM)
SEM_SPEC = pl.BlockSpec(memory_space=pltpu.SEMAPHORE)
ANY_SPEC = pl.BlockSpec(memory_space=pl.ANY)
EFFECT = pltpu.SideEffectType.DATAFLOW_SIDE_EFFECTING


def _exchange_copies(src_refs, land_refs, send_sems, recv_sems, self_sems, scatter):
    x, y, c = lax.axis_index("x"), lax.axis_index("y"), lax.axis_index("c")
    me = 4 * x + 2 * y + c
    remote, local = [], []
    for p, (src_ref, land_ref) in enumerate(zip(src_refs, land_refs)):
        rows = land_ref.shape[0] // NDEV

        def part(idx):
            return src_ref.at[pl.ds(idx * rows, rows), :] if scatter else src_ref

        slot = land_ref.at[pl.ds(me * rows, rows), :]
        for k in range(1, NDEV):
            px, py, pc = x ^ ((k >> 2) & 1), y ^ ((k >> 1) & 1), c ^ (k & 1)
            remote.append(pltpu.make_async_remote_copy(
                src_ref=part(4 * px + 2 * py + pc), dst_ref=slot, send_sem=send_sems.at[7 * p + k - 1],
                recv_sem=recv_sems.at[7 * p + k - 1], device_id=(px, py, pc), device_id_type=MESH))
        local.append(pltpu.make_async_copy(part(me), slot, self_sems.at[p]))
    return remote, local


def _send_start(srcs, scatter, after, name):
    n = len(srcs)
    lands = []
    for s in srcs:
        rows = s.shape[0] if scatter else NDEV * s.shape[0]
        lands.append(pltpu.with_memory_space_constraint(lax.empty((rows, s.shape[1]), s.dtype), pltpu.HBM))

    def body(*refs):
        src_refs, land_refs = refs[:n], refs[n:2 * n]
        send_sems, recv_sems, self_sems = refs[2 * n + 1:2 * n + 4]
        remote, local = _exchange_copies(src_refs, land_refs, send_sems, recv_sems, self_sems, scatter)
        for cp in remote + local:
            cp.start()
        refs[-1][...] = jnp.zeros_like(refs[-1])

    hbm = lambda a: pltpu.HBM(a.shape, a.dtype)
    out = pl.pallas_call(
        body, name=name,
        out_shape=(pltpu.SemaphoreType.DMA((7 * n,)), pltpu.SemaphoreType.DMA((7 * n,)), pltpu.SemaphoreType.DMA((n,)),
                   *[hbm(s) for s in srcs], *[hbm(a) for a in lands], jax.ShapeDtypeStruct((8, HEAD), F32)),
        in_specs=(HBM_SPEC,) * (2 * n) + (ANY_SPEC,),
        out_specs=(SEM_SPEC,) * 3 + (HBM_SPEC,) * (2 * n) + (pl.BlockSpec(memory_space=pltpu.VMEM),),
        input_output_aliases={i: 3 + i for i in range(2 * n)},
        compiler_params=pltpu.CompilerParams(has_side_effects=EFFECT),
    )(*[pltpu.with_memory_space_constraint(s, pltpu.HBM) for s in srcs], *lands, after)
    return dict(sems=out[:3], srcs=out[3:3 + n], lands=out[3 + n:3 + 2 * n], token=out[-1])


def _send_wait(started, scatter, after, name):
    srcs, lands = started["srcs"], started["lands"]
    n = len(srcs)

    def body(*refs):
        src_refs, land_refs = refs[:n], refs[n:2 * n]
        send_sems, recv_sems, self_sems = refs[2 * n:2 * n + 3]
        remote, local = _exchange_copies(src_refs, land_refs, send_sems, recv_sems, self_sems, scatter)
        for cp in remote:
            cp.wait_send()
            cp.wait_recv()
        for cp in local:
            cp.wait()

    hbm = lambda a: pltpu.HBM(a.shape, a.dtype)
    out = pl.pallas_call(
        body, name=name, out_shape=(*[hbm(s) for s in srcs], *[hbm(a) for a in lands]),
        in_specs=(HBM_SPEC,) * (2 * n) + (SEM_SPEC,) * 3 + (ANY_SPEC,), out_specs=(HBM_SPEC,) * (2 * n),
        input_output_aliases={i: i for i in range(2 * n)},
        compiler_params=pltpu.CompilerParams(has_side_effects=EFFECT),
    )(*srcs, *lands, *started["sems"], after)
    return out[n:]


def _sum_slots(xs, name, rows_out=None):
    _, r, c = xs[0].shape
    ro = rows_out or r
    tc = _pick(c, (128,))

    def body(*refs):
        o_ref = refs[-1]
        for l, x_ref in enumerate(refs[:-1]):
            acc = x_ref[0].astype(F32)
            for i in range(1, NDEV):
                acc = acc + x_ref[i].astype(F32)
            o_ref[l] = acc[:ro]

    return pl.pallas_call(
        body, name=name, grid=(c // tc,), in_specs=[pl.BlockSpec((NDEV, r, tc), lambda j: (0, 0, j))] * len(xs),
        out_specs=pl.BlockSpec((len(xs), ro, tc), lambda j: (0, 0, j)),
        out_shape=jax.ShapeDtypeStruct((len(xs), ro, c), F32), compiler_params=_params(("parallel",)))(*xs)


def _adamw(w, g, m, v, name):
    if w.ndim == 3:
        nl, r, c = w.shape
        tc = _pick(c, (256, 128))
        grid = (nl, c // tc)
        blk = pl.BlockSpec((None, r, tc), lambda i, j: (i, 0, j))
        sem = ("parallel", "parallel")
    else:
        r, c = w.shape
        tr = _pick(r, (256, 128, 64, 32, 16, 8))
        grid = (r // tr,)
        blk = pl.BlockSpec((tr, c), lambda i: (i, 0))
        sem = ("parallel",)
    c1 = 1.0 - B1 ** STEP
    c2 = 1.0 - B2 ** STEP

    def body(w_ref, g_ref, m_ref, v_ref, d_ref, nm_ref, nv_ref):
        gv = g_ref[...]
        nm = B1 * m_ref[...] + (1.0 - B1) * gv
        nv = B2 * v_ref[...] + (1.0 - B2) * (gv * gv)
        d_ref[...] = -LR * ((nm / c1) / (jnp.sqrt(nv / c2) + ADAM_EPS) + WD * w_ref[...])
        nm_ref[...] = nm
        nv_ref[...] = nv

    sds = jax.ShapeDtypeStruct(w.shape, F32)
    return pl.pallas_call(
        body, name=name, grid=grid, in_specs=[blk] * 4, out_specs=[blk] * 3, out_shape=[sds] * 3,
        compiler_params=_params(sem))(w, g, m, v)


def _pad_rows(a, mult):
    r = a.shape[0]
    pad = (-r) % mult
    return a if pad == 0 else jnp.pad(a, ((0, pad), (0, 0)))


def _pad_lanes(v, width=HEAD):
    return jnp.pad(v.reshape(1, -1), ((0, 0), (0, width - v.shape[-1])))


def kernel(x, p, ln_mix, ln_ffn, ln_ple, gdn_w_in, gdn_conv, gdn_a_log, gdn_dt_bias, gdn_norm, gdn_w_out, kv_norm, w_kv, k_norm, sb_w_q, sb_q_norm, sb_w_out, ffn_w_in, ffn_w_out, ple_w_proj, ple_w_gate, loss_target, m_ln_mix, m_ln_ffn, m_ln_ple, m_gdn_w_in, m_gdn_conv, m_gdn_a_log, m_gdn_dt_bias, m_gdn_norm, m_gdn_w_out, m_kv_norm, m_w_kv, m_k_norm, m_sb_w_q, m_sb_q_norm, m_sb_w_out, m_ffn_w_in, m_ffn_w_out, m_ple_w_proj, m_ple_w_gate, v_ln_mix, v_ln_ffn, v_ln_ple, v_gdn_w_in, v_gdn_conv, v_gdn_a_log, v_gdn_dt_bias, v_gdn_norm, v_gdn_w_out, v_kv_norm, v_w_kv, v_k_norm, v_sb_w_q, v_sb_q_norm, v_sb_w_out, v_ffn_w_in, v_ffn_w_out, v_ple_w_proj, v_ple_w_gate):
    s, d = x.shape[1], x.shape[2]
    nh = d // HEAD
    depth = ln_mix.shape[0]
    n_a = gdn_w_in.shape[0]
    n_b = sb_w_q.shape[0]
    me = _my_index()
    win_cols = gdn_w_in.shape[2]
    win_rows = 4 * d + 2 * nh

    def col_t(w):
        return jnp.transpose(w).astype(BF16)

    local = {}
    for l in range(n_a):
        local[("gdn_w_in", l)] = col_t(gdn_w_in[l])
        local[("gdn_w_out", l)] = gdn_w_out[l].astype(BF16)
    local[("w_kv", 0)] = col_t(w_kv)
    for j in range(n_b):
        local[("sb_w_q", j)] = sb_w_q[j].astype(BF16)
        local[("sb_w_out", j)] = sb_w_out[j].astype(BF16)
    for l in range(depth):
        local[("ffn_w_in", l)] = col_t(ffn_w_in[l])
        local[("ffn_w_out", l)] = ffn_w_out[l].astype(BF16)
        local[("ple_w_proj", l)] = col_t(ple_w_proj[l]).reshape(-1, d)
        local[("ple_w_gate", l)] = ple_w_gate[l].astype(BF16)
    local = {key: _pad_rows(a, 16) for key, a in local.items()}

    chunks = []
    for l in range(depth):
        mix = [("gdn_w_in", l), ("gdn_w_out", l)] if l < n_a else [("sb_w_q", l - n_a), ("sb_w_out", l - n_a)]
        rest = [("ffn_w_in", l), ("ffn_w_out", l), ("ple_w_proj", l), ("ple_w_gate", l)]
        if l == n_a - 1:
            rest.append(("w_kv", 0))
        chunks += [(f"a{l}", mix), (f"f{l}", rest)]
    chunk_keys = dict(chunks)

    conv_rows = n_a * gdn_conv.shape[1]
    conv_sh = _pad_rows(gdn_conv.reshape(conv_rows, -1), 8)
    extra = {chunks[0][0]: [conv_sh]}

    token = jnp.zeros((8, HEAD), F32)
    w_started = {}
    for name, keys in chunks:
        w_started[name] = _send_start([local[k] for k in keys] + extra.get(name, []), False, token, f"comm_wstart_{name}")
        token = w_started[name]["token"]

    full = {}

    def fetch(name, after):
        lands = _send_wait(w_started[name], False, after, f"comm_wwait_{name}")
        for key, land in zip(chunk_keys[name], lands):
            full[key] = land
        return lands[len(chunk_keys[name]):]

    def whole(key, valid=None):
        a = full[key]
        if valid is not None:
            a = a.reshape(NDEV, -1, d)[:, :valid, :].reshape(-1, d)
        return a

    pd = p.shape[-1]
    w_in_t, w_ab_t, w_gout, w_q, w_sout, wf_t, w_fout, wp_t, w_pg = {}, {}, {}, {}, {}, {}, {}, {}, {}
    wkv_t = None

    h = x[0]
    sv = []
    kv_sv = None
    k_sh = v_sh = None
    for l in range(depth):
        t = {}
        t["h0"] = h
        if l == 0:
            hn = _rms_fwd(h, ln_mix[l], f"rms_mix_{l}")
        t["hn"] = hn
        extras = fetch(f"a{l}", token if l == 0 else hn)
        if l == 0:
            conv_g = extras[0].reshape(NDEV, conv_sh.shape[0], -1)
            conv_full = jnp.transpose(conv_g[:, :conv_rows, :], (1, 0, 2)).reshape(n_a, gdn_conv.shape[1], 3 * d)
        if l < n_a:
            wt = whole(("gdn_w_in", l), win_cols)
            w_in_t[l] = wt[:4 * d]
            w_ab_t[l] = jnp.pad(wt[4 * d:], ((0, HEAD - 2 * nh), (0, 0)))
            w_gout[l] = whole(("gdn_w_out", l))
        else:
            w_q[l - n_a] = whole(("sb_w_q", l - n_a))
            w_sout[l - n_a] = whole(("sb_w_out", l - n_a))
        if l < n_a:
            proj = _mm(hn, w_in_t[l], "nt", f"gdn_proj_{l}")
            pab = _mm(hn, w_ab_t[l], "nt", f"gdn_proj_ab_{l}")
            qkv = _conv_fwd(proj, conv_full[l], d, f"gdn_conv_{l}")
            al, dtb = _pad_lanes(gdn_a_log[l]), _pad_lanes(gdn_dt_bias[l])
            gb = _gates_fwd(pab, al, dtb, nh, f"gdn_gates_{l}")
            o_raw, states = _gdn_fwd(qkv, gb, nh, f"gdn_rule_{l}")
            o2 = _headnorm_fwd(o_raw, gdn_norm[l], f"gdn_outnorm_{l}", gate=proj, gate_col0=3 * d, head_major=True)
            h, hn2 = _mm(o2, w_gout[l], "nn", f"gdn_out_{l}", res=h, norm_g=ln_ffn[l])
            t.update(proj=proj, pab=pab, qkv=qkv, gb=gb, o_raw=o_raw, states=states, o2=o2, al=al, dtb=dtb)
        else:
            j = l - n_a
            qpre = _mm(hn, w_q[j], "nn", f"sb_qproj_{j}")
            qn = _headnorm_fwd(qpre, sb_q_norm[j], f"sb_qnorm_{j}", scale=HEAD ** -0.5)
            o, ctab = _sb_fwd(qn, k_sh, v_sh, f"sb_attn_{j}")
            h, hn2 = _mm(o, w_sout[j], "nn", f"sb_out_{j}", res=h, norm_g=ln_ffn[l])
            t.update(qpre=qpre, qn=qn, o=o, ctab=ctab)
        t["h1"] = h
        fetch(f"f{l}", hn2)
        wf_t[l] = whole(("ffn_w_in", l))
        w_fout[l] = whole(("ffn_w_out", l))
        wp_t[l] = full[("ple_w_proj", l)].reshape(d, pd)
        w_pg[l] = whole(("ple_w_gate", l))
        if l == n_a - 1:
            wkv_t = whole(("w_kv", 0))
        act, gs, us = _swiglu_fwd(hn2, wf_t[l], f"ffn_in_{l}")
        h, hn3 = _mm(act, w_fout[l], "nn", f"ffn_out_{l}", res=h, norm_g=ln_ple[l])
        t.update(hn2=hn2, act=act, gs=gs, us=us, h2=h)
        gains = ([ln_mix[l + 1]] if l + 1 < depth else []) + ([kv_norm] if l == n_a - 1 else [])
        h, gpre, pp, *normed = _ple_fwd(h, hn3, p[l, 0], w_pg[l], wp_t[l], f"ple_{l}", norm_gs=gains)
        if l + 1 < depth:
            hn = normed[0]
        t.update(hn3=hn3, gpre=gpre, pp=pp)
        sv.append(t)
        if l == n_a - 1:
            kvn = normed[-1]
            kv = _mm(kvn, wkv_t, "nt", "kv_proj")
            k_sh = _headnorm_fwd(kv, k_norm, "k_norm", width=d)
            v_sh = kv[:, d:].astype(BF16)
            kv_sv = dict(h=h, kvn=kvn, kv=kv)

    dh, loss_vec = _loss_fwd_bwd(h, loss_target[0], "loss")
    loss = lax.psum(jnp.sum(loss_vec), ("x", "y", "c"))

    gw = {}
    small = {}
    g_started = {}

    def scatter_start(name):
        gparts = []
        for key in chunk_keys[name]:
            g = gw[key]
            g = g.reshape(NDEV, -1, d) if key[0] == "ple_w_proj" else g.reshape(NDEV, -1, g.shape[-1])
            padr = local[key].shape[0] - g.shape[1]
            if padr:
                g = jnp.pad(g, ((0, 0), (0, padr), (0, 0)))
            gparts.append(g.reshape(-1, d))
        g_started[name] = _send_start(gparts, True, jnp.zeros((8, HEAD), F32), f"comm_gstart_{name}")
        return g_started[name]["token"]

    dks, dvs = [], []
    for l in reversed(range(depth)):
        t = sv[l]
        if l == n_a - 1:
            dkv, dkn = _kv_grad(dks, dvs, kv_sv["kv"], k_norm, "k_norm_bwd")
            gw[("w_kv", 0)] = _mm(dkv, kv_sv["kvn"], "tn", "kv_dw", out_dtype=BF16)
            dh, _, dg = _mm(dkv, wkv_t, "nn", "kv_dx", norm_bwd=(kv_sv["h"], kv_norm, dh))
            small["kv_norm"] = dg
            small["k_norm"] = dkn
        dgp, dpp = _ple_bwd(dh, t["gpre"], t["pp"], f"ple_bwd_{l}")
        gw[("ple_w_gate", l)] = _mm(t["hn3"], dgp, "tn", f"ple_dwg_{l}", out_dtype=BF16)
        gw[("ple_w_proj", l)] = _mm(dpp, p[l, 0], "tn", f"ple_dwp_{l}", out_dtype=BF16)
        dh, dhb, dg = _mm(dgp, w_pg[l], "nt", f"ple_dx_{l}", norm_bwd=(t["h2"], ln_ple[l], dh))
        small[("ln_ple", l)] = dg
        dgu = _swiglu_bwd(dhb, w_fout[l], t["gs"], t["us"], f"ffn_bwd_act_{l}")
        gw[("ffn_w_out", l)] = _mm(t["act"], dhb, "tn", f"ffn_dwo_{l}", out_dtype=BF16)
        gw[("ffn_w_in", l)] = _mm(dgu, t["hn2"], "tn", f"ffn_dwi_{l}", out_dtype=BF16)
        dh, dhb, dg = _mm(dgu, wf_t[l], "nn", f"ffn_dx_{l}", norm_bwd=(t["h1"], ln_ffn[l], dh),
                          after=scatter_start(f"f{l}"))
        small[("ln_ffn", l)] = dg
        if l < n_a:
            do2 = _mm(dhb, w_gout[l], "nt", f"gdn_out_dx_{l}")
            gw[("gdn_w_out", l)] = _mm(t["o2"], dhb, "tn", f"gdn_out_dw_{l}", out_dtype=BF16)
            do_raw, dgn, dgate = _headnorm_bwd(do2, t["o_raw"], gdn_norm[l], f"gdn_outnorm_bwd_{l}",
                                               gate=t["proj"], gate_col0=3 * d, head_major=True)
            small[("gdn_norm", l)] = dgn
            dqkv, dgb = _gdn_bwd(t["qkv"], t["gb"], do_raw, t["states"], nh, f"gdn_rule_bwd_{l}")
            dpab, dal, ddt = _gates_bwd(dgb, t["pab"], t["al"], t["dtb"], nh, f"gdn_gates_bwd_{l}")
            small[("gdn_a_log", l)] = dal
            small[("gdn_dt_bias", l)] = ddt
            dproj, dconv = _conv_bwd(dqkv, t["proj"], conv_full[l], d, f"gdn_conv_bwd_{l}", dgate)
            small[("gdn_conv", l)] = dconv
            dw_main = _mm(dproj, t["hn"], "tn", f"gdn_proj_dw_{l}", out_dtype=BF16)
            dw_ab = _mm(dpab, t["hn"], "tn", f"gdn_proj_ab_dw_{l}", out_dtype=BF16)
            gw[("gdn_w_in", l)] = jnp.concatenate([dw_main, dw_ab[:16]], axis=0)[:win_rows]
            dhn_ab = _mm(dpab, w_ab_t[l], "nn", f"gdn_proj_ab_dx_{l}")
            last = dict(a=dproj, b=w_in_t[l], mode="nn", name=f"gdn_proj_dx_{l}", res=dhn_ab)
        else:
            j = l - n_a
            do = _mm(dhb, w_sout[j], "nt", f"sb_out_dx_{j}", out_dtype=BF16)
            gw[("sb_w_out", j)] = _mm(t["o"], dhb, "tn", f"sb_out_dw_{j}", out_dtype=BF16)
            dq, dk, dv = _sb_bwd(t["qn"], k_sh, v_sh, do, t["ctab"], f"sb_attn_bwd_{j}")
            dks.append(dk)
            dvs.append(dv)
            dqpre, dqn = _headnorm_bwd(dq, t["qpre"], sb_q_norm[j], f"sb_qnorm_bwd_{j}", scale=HEAD ** -0.5, dx_dtype=BF16)
            small[("sb_q_norm", j)] = dqn
            gw[("sb_w_q", j)] = _mm(t["hn"], dqpre, "tn", f"sb_q_dw_{j}", out_dtype=BF16)
            last = dict(a=dqpre, b=w_q[j], mode="nt", name=f"sb_q_dx_{j}")
        dh, _, dg = _mm(**last, norm_bwd=(t["h0"], ln_mix[l], dh), after=scatter_start(f"a{l}"))
        small[("ln_mix", l)] = dg
    grad_x = dh[None]

    landed = {}
    for name, keys in reversed(chunks):
        lands = _send_wait(g_started[name], True, dh, f"comm_gwait_{name}")
        for key, land in zip(keys, lands):
            landed[key] = land.reshape(NDEV, -1, d)

    def summed(wname, count, rows_out=None):
        return _sum_slots([landed[(wname, i)] for i in range(count)], f"grad_sum_{wname}", rows_out)

    gt_gdn_w_in = summed("gdn_w_in", n_a, win_cols)
    gt_ffn_w_in = summed("ffn_w_in", depth)
    g_gdn_w_in = jnp.transpose(gt_gdn_w_in, (0, 2, 1))
    g_gdn_w_out = summed("gdn_w_out", n_a)
    g_w_kv = jnp.transpose(summed("w_kv", 1)[0])
    g_sb_w_q = summed("sb_w_q", n_b)
    g_sb_w_out = summed("sb_w_out", n_b)
    g_ffn_w_in = jnp.transpose(gt_ffn_w_in, (0, 2, 1))
    g_ffn_w_out = summed("ffn_w_out", depth)
    g_ple_w_proj = jnp.transpose(summed("ple_w_proj", depth).reshape(depth, -1, pd), (0, 2, 1))
    g_ple_w_gate = summed("ple_w_gate", depth)

    def vec_rows(v):
        return v.reshape(-1, HEAD)

    small_items = []
    for name_, cnt in (("ln_mix", depth), ("ln_ffn", depth), ("ln_ple", depth)):
        for l in range(cnt):
            small_items.append(((name_, l), vec_rows(small[(name_, l)])))
    for l in range(n_a):
        small_items.append((("gdn_conv", l), small[("gdn_conv", l)].reshape(-1, HEAD)))
        small_items.append((("gdn_a_log", l), small[("gdn_a_log", l)]))
        small_items.append((("gdn_dt_bias", l), small[("gdn_dt_bias", l)]))
        small_items.append((("gdn_norm", l), small[("gdn_norm", l)]))
    small_items.append(("kv_norm", vec_rows(small["kv_norm"])))
    small_items.append(("k_norm", small["k_norm"]))
    for j in range(n_b):
        small_items.append((("sb_q_norm", j), small[("sb_q_norm", j)]))
    spack = jnp.concatenate([_pad_rows(a, 8) for _, a in small_items], axis=0)
    sg = _all_gather(spack, "comm_gather_small").reshape(NDEV, spack.shape[0], HEAD)
    ssum = _sum_slots([sg], "small_sum")[0]
    sm = {}
    off = 0
    for key, a in small_items:
        sm[key] = ssum[off:off + a.shape[0]]
        off += a.shape[0] + (-a.shape[0]) % 8

    g_ln_mix = jnp.stack([sm[("ln_mix", l)].reshape(d) for l in range(depth)])
    g_ln_ffn = jnp.stack([sm[("ln_ffn", l)].reshape(d) for l in range(depth)])
    g_ln_ple = jnp.stack([sm[("ln_ple", l)].reshape(d) for l in range(depth)])
    conv_loc = gdn_conv.shape[2]
    g_conv_full = jnp.stack([sm[("gdn_conv", l)].reshape(gdn_conv.shape[1], 3 * d) for l in range(n_a)])
    g_gdn_conv = lax.dynamic_slice_in_dim(g_conv_full, me * conv_loc, conv_loc, axis=2)
    g_a_log = jnp.stack([sm[("gdn_a_log", l)][0, :nh] for l in range(n_a)])
    g_dt_bias = jnp.stack([sm[("gdn_dt_bias", l)][0, :nh] for l in range(n_a)])
    g_gdn_norm = jnp.stack([sm[("gdn_norm", l)][0] for l in range(n_a)])
    g_kv_norm = sm["kv_norm"].reshape(d)
    g_k_norm = sm["k_norm"][0]
    g_sb_q_norm = jnp.stack([sm[("sb_q_norm", j)][0] for j in range(n_b)])

    grads = [g_ln_mix, g_ln_ffn, g_ln_ple, g_gdn_w_in, g_gdn_conv, g_a_log, g_dt_bias, g_gdn_norm, g_gdn_w_out,
             g_kv_norm, g_w_kv, g_k_norm, g_sb_w_q, g_sb_q_norm, g_sb_w_out, g_ffn_w_in, g_ffn_w_out, g_ple_w_proj,
             g_ple_w_gate]
    weights = [ln_mix, ln_ffn, ln_ple, gdn_w_in, gdn_conv, gdn_a_log, gdn_dt_bias, gdn_norm, gdn_w_out, kv_norm, w_kv,
               k_norm, sb_w_q, sb_q_norm, sb_w_out, ffn_w_in, ffn_w_out, ple_w_proj, ple_w_gate]
    moms = [m_ln_mix, m_ln_ffn, m_ln_ple, m_gdn_w_in, m_gdn_conv, m_gdn_a_log, m_gdn_dt_bias, m_gdn_norm, m_gdn_w_out,
            m_kv_norm, m_w_kv, m_k_norm, m_sb_w_q, m_sb_q_norm, m_sb_w_out, m_ffn_w_in, m_ffn_w_out, m_ple_w_proj,
            m_ple_w_gate]
    vels = [v_ln_mix, v_ln_ffn, v_ln_ple, v_gdn_w_in, v_gdn_conv, v_gdn_a_log, v_gdn_dt_bias, v_gdn_norm, v_gdn_w_out,
            v_kv_norm, v_w_kv, v_k_norm, v_sb_w_q, v_sb_q_norm, v_sb_w_out, v_ffn_w_in, v_ffn_w_out, v_ple_w_proj,
            v_ple_w_gate]

    deltas, new_m, new_v = [], [], []
    small_idx = [i for i, w in enumerate(weights) if w.size < 8 * HEAD * 16]
    transposed = {3: gt_gdn_w_in, 15: gt_ffn_w_in}
    for i, (w, g, m, v) in enumerate(zip(weights, grads, moms, vels)):
        if i in small_idx:
            deltas.append(None), new_m.append(None), new_v.append(None)
            continue
        if i in transposed:
            tr = lambda a: jnp.transpose(a, (0, 2, 1))
            dl, nm, nv = _adamw(tr(w), transposed[i], tr(m), tr(v), f"adamw_{i}")
            deltas.append(tr(dl)), new_m.append(tr(nm)), new_v.append(tr(nv))
            continue
        shp = w.shape
        two = lambda a: a.reshape(-1, shp[-1])
        dl, nm, nv = _adamw(two(w), two(g), two(m), two(v), f"adamw_{i}")
        deltas.append(dl.reshape(shp)), new_m.append(nm.reshape(shp)), new_v.append(nv.reshape(shp))

    def flat_pack(arrs):
        flat = jnp.concatenate([a.reshape(-1) for a in arrs])
        pad = (-flat.shape[0]) % (8 * HEAD)
        return jnp.pad(flat, (0, pad)).reshape(-1, HEAD)

    sw = flat_pack([weights[i] for i in small_idx])
    sgr = flat_pack([grads[i] for i in small_idx])
    smo = flat_pack([moms[i] for i in small_idx])
    sve = flat_pack([vels[i] for i in small_idx])
    sdl, snm, snv = _adamw(sw, sgr, smo, sve, "adamw_small")
    off = 0
    for i in small_idx:
        n = weights[i].size
        shp = weights[i].shape
        deltas[i] = sdl.reshape(-1)[off:off + n].reshape(shp)
        new_m[i] = snm.reshape(-1)[off:off + n].reshape(shp)
        new_v[i] = snv.reshape(-1)[off:off + n].reshape(shp)
        off += n

    return (loss, grad_x, *grads, *deltas, *new_m, *new_v)
```

```python
import math

import jax
import jax.numpy as jnp
from jax import lax
from jax.experimental import pallas as pl
from jax.experimental.pallas import tpu as pltpu

F32 = jnp.float32
BF16 = jnp.bfloat16
NDEV = 8
HEAD = 128
CHUNK = 128
SBLK = 256
EPS = 1e-6
LR, B1, B2, ADAM_EPS, WD, STEP = 0.001, 0.9, 0.999, 1e-08, 0.01, 10
NEG = -1e30
MM_VMEM_BUDGET = 40 * 1024 * 1024

NN = (((1,), (0,)), ((), ()))
NT = (((1,), (1,)), ((), ()))
TN = (((0,), (0,)), ((), ()))
BNN = (((2,), (1,)), ((0,), (0,)))
BNT = (((2,), (2,)), ((0,), (0,)))
BTN = (((1,), (1,)), ((0,), (0,)))
MESH = pl.DeviceIdType.MESH


def _dot(a, b, dims=NN):
    return lax.dot_general(a.astype(BF16), b.astype(BF16), dims, preferred_element_type=F32)


def _dot_hilo(a, b01_twice, one_dot):
    hi = a.astype(BF16)
    lo = (a - hi.astype(F32)).astype(BF16)
    if one_dot:
        return lax.dot_general(jnp.concatenate([hi, lo], axis=1), b01_twice, NN, preferred_element_type=F32)
    b01 = b01_twice[:a.shape[1]]
    return (lax.dot_general(hi, b01, NN, preferred_element_type=F32)
            + lax.dot_general(lo, b01, NN, preferred_element_type=F32))


def _pick(dim, cands):
    for c in cands:
        if dim % c == 0:
            return c
    return dim


def _params(sem, vmem_mb=48):
    return pltpu.CompilerParams(dimension_semantics=sem, vmem_limit_bytes=vmem_mb * 1024 * 1024)


def _silu(x):
    return x * jax.nn.sigmoid(x)


def _silu_and_grad(x):
    s = jax.nn.sigmoid(x)
    xs = x * s
    return xs, s + xs * (1.0 - s)


def _mm(a, b, mode, name, out_dtype=F32, res=None, norm_g=None, norm_bwd=None, after=None):
    if mode == "nn":
        (m, k), n = a.shape, b.shape[1]
    elif mode == "nt":
        (m, k), n = a.shape, b.shape[0]
    else:
        (k, m), n = a.shape, b.shape[1]
    rows = norm_g is not None or norm_bwd is not None
    tn = n if rows else _pick(n, (512, 256, 128))
    tk = k if k <= 4096 else max(t for t in range(128, 4097, 128) if k % t == 0)
    nk = k // tk
    out_b = jnp.dtype(out_dtype).itemsize + (res.dtype.itemsize if res is not None else 0)
    out_b += 2 if norm_g is not None else 0
    out_b += 10 if norm_bwd is not None else 0
    for tm in [t for t in range(min(m, 2048), 127, -128) if m % t == 0] + [m]:
        need = 2 * (tm * tk * a.dtype.itemsize + tk * tn * b.dtype.itemsize + tm * tn * out_b) + 4 * tm * tn
        if need <= MM_VMEM_BUDGET:
            break
    dims = {"nn": NN, "nt": NT, "tn": TN}[mode]
    if mode == "tn":
        a_spec = pl.BlockSpec((tk, tm), lambda i, j, kk: (kk, i))
    else:
        a_spec = pl.BlockSpec((tm, tk), lambda i, j, kk: (i, kk))
    if mode == "nt":
        b_spec = pl.BlockSpec((tn, tk), lambda i, j, kk: (j, kk))
    else:
        b_spec = pl.BlockSpec((tk, tn), lambda i, j, kk: (kk, j))
    mn_spec = pl.BlockSpec((tm, tn), lambda i, j, kk: (i, j))
    vec_spec = pl.BlockSpec((1, tn), lambda i, j, kk: (0, j))
    has_res = res is not None
    n_in = 2 + has_res + (1 if norm_g is not None else 0) + (3 if norm_bwd is not None else 0) + (after is not None)

    def body(*refs):
        a_ref, b_ref = refs[:2]
        extra = list(refs[2:n_in])
        outs = refs[n_in:-1]
        acc = refs[-1]
        kk = pl.program_id(2)

        @pl.when(kk == 0)
        def _():
            acc[...] = jnp.zeros_like(acc)

        if norm_bwd is not None:
            @pl.when((kk == 0) & (pl.program_id(0) == 0))
            def _():
                outs[2][...] = jnp.zeros_like(outs[2])

        acc[...] += _dot(a_ref[...], b_ref[...], dims)

        @pl.when(kk == nk - 1)
        def _():
            r = acc[...]
            if has_res:
                r = r + extra.pop(0)[...].astype(F32)
            if norm_g is not None:
                outs[0][...] = r.astype(out_dtype)
                rs = lax.rsqrt(jnp.mean(r * r, axis=-1, keepdims=True) + EPS)
                outs[1][...] = (r * rs * extra.pop(0)[...]).astype(BF16)
            elif norm_bwd is not None:
                xv, gv, dres = extra.pop(0)[...], extra.pop(0)[...], extra.pop(0)[...]
                rs = lax.rsqrt(jnp.mean(xv * xv, axis=-1, keepdims=True) + EPS)
                gdy = r * gv
                dx = dres + rs * gdy - xv * (rs * rs * rs) * jnp.mean(xv * gdy, axis=-1, keepdims=True)
                outs[0][...] = dx
                outs[1][...] = dx.astype(BF16)
                outs[2][...] += jnp.sum(r * xv * rs, axis=0, keepdims=True)
            else:
                outs[0][...] = r.astype(out_dtype)

    ins = [a, b] + ([res] if has_res else [])
    in_specs = [a_spec, b_spec] + ([mn_spec] if has_res else [])
    out_specs, out_shape = [mn_spec], [jax.ShapeDtypeStruct((m, n), out_dtype)]
    sem = ("parallel", "parallel", "arbitrary")
    if norm_g is not None:
        ins.append(norm_g.reshape(1, n))
        in_specs.append(vec_spec)
        out_specs.append(mn_spec)
        out_shape.append(jax.ShapeDtypeStruct((m, n), BF16))
    if norm_bwd is not None:
        x, g, dres = norm_bwd
        ins += [x, g.reshape(1, n), dres]
        in_specs += [mn_spec, vec_spec, mn_spec]
        out_specs += [mn_spec, vec_spec]
        out_shape += [jax.ShapeDtypeStruct((m, n), BF16), jax.ShapeDtypeStruct((1, n), F32)]
        sem = ("arbitrary", "arbitrary", "arbitrary")
    if after is not None:
        ins.append(after)
        in_specs.append(pl.BlockSpec(memory_space=pl.ANY))
    out = pl.pallas_call(
        body, name=name, grid=(m // tm, n // tn, nk), in_specs=in_specs, out_specs=out_specs,
        out_shape=out_shape, scratch_shapes=[pltpu.VMEM((tm, tn), F32)],
        compiler_params=_params(sem))(*ins)
    return out[0] if len(out) == 1 else out


def _rms_fwd(h, g, name):
    s, d = h.shape
    tm = _pick(s, (512, 256, 128))

    def body(h_ref, g_ref, o_ref):
        x = h_ref[...]
        r = lax.rsqrt(jnp.mean(x * x, axis=-1, keepdims=True) + EPS)
        o_ref[...] = (x * r * g_ref[...]).astype(BF16)

    return pl.pallas_call(
        body, name=name, grid=(s // tm,),
        in_specs=[pl.BlockSpec((tm, d), lambda i: (i, 0)), pl.BlockSpec((1, d), lambda i: (0, 0))],
        out_specs=pl.BlockSpec((tm, d), lambda i: (i, 0)),
        out_shape=jax.ShapeDtypeStruct((s, d), BF16), compiler_params=_params(("parallel",)))(h, g.reshape(1, d))


def _headnorm_fwd(x, g, name, scale=1.0, gate=None, gate_col0=0, out_dtype=BF16, width=None, head_major=False):
    if head_major:
        s, d = x.shape[1], x.shape[0] * HEAD
    else:
        s, d = x.shape[0], (width or x.shape[1])
    nh = d // HEAD
    tm = _pick(s, (256, 128))
    has_gate = gate is not None
    gb = gate_col0 // d

    def body(*refs):
        if has_gate:
            x_ref, g_ref, gt_ref, o_ref = refs
        else:
            x_ref, g_ref, o_ref = refs
        gv = g_ref[...]
        for h in range(nh):
            sl = slice(h * HEAD, (h + 1) * HEAD)
            xv = (x_ref[h] if head_major else x_ref[:, sl]).astype(F32)
            r = lax.rsqrt(jnp.mean(xv * xv, axis=-1, keepdims=True) + EPS)
            y = xv * r * gv
            if scale != 1.0:
                y = y * scale
            if has_gate:
                y = y * _silu(gt_ref[:, sl])
            o_ref[:, sl] = y.astype(out_dtype)

    row = pl.BlockSpec((tm, d), lambda i: (i, 0))
    hm = pl.BlockSpec((nh, tm, HEAD), lambda i: (0, i, 0))
    ins = [x, g.reshape(1, HEAD)]
    in_specs = [hm if head_major else row, pl.BlockSpec((1, HEAD), lambda i: (0, 0))]
    if has_gate:
        ins.append(gate)
        in_specs.append(pl.BlockSpec((tm, d), lambda i: (i, gb)))
    return pl.pallas_call(
        body, name=name, grid=(s // tm,), in_specs=in_specs, out_specs=row,
        out_shape=jax.ShapeDtypeStruct((s, d), out_dtype), compiler_params=_params(("parallel",)))(*ins)


def _headnorm_bwd(dy, x, g, name, scale=1.0, gate=None, gate_col0=0, dx_dtype=F32, head_major=False):
    s, d = dy.shape
    nh = d // HEAD
    tm = _pick(s, (256, 128))
    has_gate = gate is not None
    gb = gate_col0 // d

    def body(*refs):
        if has_gate:
            dy_ref, x_ref, g_ref, gt_ref, dx_ref, dg_ref, dgt_ref = refs
        else:
            dy_ref, x_ref, g_ref, dx_ref, dg_ref = refs

        @pl.when(pl.program_id(0) == 0)
        def _():
            dg_ref[...] = jnp.zeros_like(dg_ref)

        gv = g_ref[...]
        dg_acc = jnp.zeros((1, HEAD), F32)
        for h in range(nh):
            sl = slice(h * HEAD, (h + 1) * HEAD)
            xv = (x_ref[h] if head_major else x_ref[:, sl]).astype(F32)
            dyv = dy_ref[:, sl].astype(F32)
            r = lax.rsqrt(jnp.mean(xv * xv, axis=-1, keepdims=True) + EPS)
            if has_gate:
                gt = gt_ref[:, sl]
                act, dact = _silu_and_grad(gt)
                dgt_ref[:, sl] = (dyv * (xv * r * gv) * dact).astype(dgt_ref.dtype)
                dn = dyv * act
            else:
                dn = dyv
            if scale != 1.0:
                dn = dn * scale
            gdn = dn * gv
            mean_t = jnp.mean(xv * gdn, axis=-1, keepdims=True)
            dxv = (r * gdn - xv * (r * r * r) * mean_t).astype(dx_dtype)
            if head_major:
                dx_ref[h] = dxv
            else:
                dx_ref[:, sl] = dxv
            dg_acc = dg_acc + jnp.sum(dn * xv * r, axis=0, keepdims=True)
        dg_ref[...] += dg_acc

    row = pl.BlockSpec((tm, d), lambda i: (i, 0))
    hm = pl.BlockSpec((nh, tm, HEAD), lambda i: (0, i, 0))
    vec = pl.BlockSpec((1, HEAD), lambda i: (0, 0))
    ins = [dy, x, g.reshape(1, HEAD)]
    in_specs = [row, hm if head_major else row, vec]
    out_specs = [hm if head_major else row, vec]
    dx_shape = (nh, s, HEAD) if head_major else (s, d)
    out_shape = [jax.ShapeDtypeStruct(dx_shape, dx_dtype), jax.ShapeDtypeStruct((1, HEAD), F32)]
    if has_gate:
        ins.append(gate)
        in_specs.append(pl.BlockSpec((tm, d), lambda i: (i, gb)))
        out_specs.append(pl.BlockSpec((tm, d), lambda i: (i, gb)))
        out_shape.append(jax.ShapeDtypeStruct((s, gate.shape[1]), BF16))
    return pl.pallas_call(
        body, name=name, grid=(s // tm,), in_specs=in_specs, out_specs=out_specs, out_shape=out_shape,
        compiler_params=_params(("arbitrary",)))(*ins)


def _kv_grad(dks, dvs, kv, g, name):
    s, d = dks[0].shape
    nh = d // HEAD
    tm = _pick(s, (256, 128))
    n = len(dks)

    def body(*refs):
        dk_refs, dv_refs = refs[:n], refs[n:2 * n]
        kv_ref, g_ref, o_ref, dg_ref = refs[2 * n:]

        @pl.when(pl.program_id(0) == 0)
        def _():
            dg_ref[...] = jnp.zeros_like(dg_ref)

        gv = g_ref[...]
        dg_acc = jnp.zeros((1, HEAD), F32)
        for h in range(nh):
            sl = slice(h * HEAD, (h + 1) * HEAD)
            xv = kv_ref[:, sl]
            dyv = sum(r[:, sl] for r in dk_refs)
            r = lax.rsqrt(jnp.mean(xv * xv, axis=-1, keepdims=True) + EPS)
            gdn = dyv * gv
            mean_t = jnp.mean(xv * gdn, axis=-1, keepdims=True)
            o_ref[:, sl] = (r * gdn - xv * (r * r * r) * mean_t).astype(BF16)
            dg_acc = dg_acc + jnp.sum(dyv * xv * r, axis=0, keepdims=True)
        o_ref[:, d:] = sum(r[...] for r in dv_refs).astype(BF16)
        dg_ref[...] += dg_acc

    row = pl.BlockSpec((tm, d), lambda i: (i, 0))
    vec = pl.BlockSpec((1, HEAD), lambda i: (0, 0))
    return pl.pallas_call(
        body, name=name, grid=(s // tm,), in_specs=[row] * (2 * n) + [row, vec],
        out_specs=[pl.BlockSpec((tm, 2 * d), lambda i: (i, 0)), vec],
        out_shape=[jax.ShapeDtypeStruct((s, 2 * d), BF16), jax.ShapeDtypeStruct((1, HEAD), F32)],
        compiler_params=_params(("arbitrary",)))(*dks, *dvs, kv, g.reshape(1, HEAD))


def _swiglu_fwd(hn, wf_t, name):
    s, d = hn.shape
    f = wf_t.shape[0] // 2
    tm = _pick(s, (1024, 512, 256, 128))
    tn = _pick(f, (512, 256, 128))
    nj = f // tn

    def body(a_ref, wg_ref, wu_ref, act_ref, g_ref, u_ref):
        a = a_ref[...]
        g = _dot(a, wg_ref[...], NT)
        u = _dot(a, wu_ref[...], NT)
        act_ref[...] = (_silu(g) * u).astype(BF16)
        g_ref[...] = g.astype(BF16)
        u_ref[...] = u.astype(BF16)

    o_spec = pl.BlockSpec((tm, tn), lambda i, j: (i, j))
    sds = jax.ShapeDtypeStruct((s, f), BF16)
    return pl.pallas_call(
        body, name=name, grid=(s // tm, nj),
        in_specs=[pl.BlockSpec((tm, d), lambda i, j: (i, 0)), pl.BlockSpec((tn, d), lambda i, j: (j, 0)),
                  pl.BlockSpec((tn, d), lambda i, j: (j + nj, 0))],
        out_specs=[o_spec, o_spec, o_spec], out_shape=[sds, sds, sds],
        compiler_params=_params(("parallel", "parallel")))(hn, wf_t, wf_t)


def _swiglu_bwd(dh, w_out, g, u, name):
    s, d = dh.shape
    f = w_out.shape[0]
    tm = _pick(s, (1024, 512, 256, 128))
    tn = _pick(f, (512, 256, 128))

    def body(dh_ref, w_ref, g_ref, u_ref, dgu_ref):
        j = pl.program_id(1)
        dact = _dot(dh_ref[...], w_ref[...], NT)
        gv = g_ref[...].astype(F32)
        uv = u_ref[...].astype(F32)
        sg, dsg = _silu_and_grad(gv)
        dgu_ref[:, pl.ds(pl.multiple_of(j * tn, HEAD), tn)] = (dact * uv * dsg).astype(BF16)
        dgu_ref[:, pl.ds(pl.multiple_of(f + j * tn, HEAD), tn)] = (dact * sg).astype(BF16)

    o_spec = pl.BlockSpec((tm, tn), lambda i, j: (i, j))
    return pl.pallas_call(
        body, name=name, grid=(s // tm, f // tn),
        in_specs=[pl.BlockSpec((tm, d), lambda i, j: (i, 0)), pl.BlockSpec((tn, d), lambda i, j: (j, 0)), o_spec, o_spec],
        out_specs=pl.BlockSpec((tm, 2 * f), lambda i, j: (i, 0)), out_shape=jax.ShapeDtypeStruct((s, 2 * f), BF16),
        compiler_params=_params(("parallel", "arbitrary")))(dh, w_out, g, u)


def _ple_fwd(h, hn, p, w_gate, wp_t, name, norm_gs=()):
    s, d = h.shape
    pd = p.shape[1]
    tm = _pick(s, (512, 256, 128))
    ng = len(norm_gs)

    def body(h_ref, hn_ref, p_ref, wg_ref, wp_ref, *rest):
        g_refs, (o_ref, gp_ref, pp_ref), n_refs = rest[:ng], rest[ng:ng + 3], rest[ng + 3:]
        gpre = _dot(hn_ref[...], wg_ref[...], NN)
        pp = _dot(p_ref[...], wp_ref[...], NT)
        o = h_ref[...] + pp * jax.nn.sigmoid(gpre)
        o_ref[...] = o
        gp_ref[...] = gpre.astype(BF16)
        pp_ref[...] = pp.astype(BF16)
        if ng:
            on = o * lax.rsqrt(jnp.mean(o * o, axis=-1, keepdims=True) + EPS)
            for g_ref, n_ref in zip(g_refs, n_refs):
                n_ref[...] = (on * g_ref[...]).astype(BF16)

    row = pl.BlockSpec((tm, d), lambda i: (i, 0))
    vec = pl.BlockSpec((1, d), lambda i: (0, 0))
    bf = jax.ShapeDtypeStruct((s, d), BF16)
    return pl.pallas_call(
        body, name=name, grid=(s // tm,),
        in_specs=[row, row, pl.BlockSpec((tm, pd), lambda i: (i, 0)), pl.BlockSpec((d, d), lambda i: (0, 0)),
                  pl.BlockSpec((d, pd), lambda i: (0, 0))] + [vec] * ng,
        out_specs=[row] * (3 + ng), out_shape=[jax.ShapeDtypeStruct((s, d), F32), bf, bf] + [bf] * ng,
        compiler_params=_params(("parallel",)))(h, hn, p, w_gate, wp_t, *[g.reshape(1, d) for g in norm_gs])


def _ple_bwd(dh, gpre, pp, name):
    s, d = dh.shape
    tm = _pick(s, (512, 256, 128))

    def body(dh_ref, gp_ref, pp_ref, dgp_ref, dpp_ref):
        dv = dh_ref[...]
        sig = jax.nn.sigmoid(gp_ref[...].astype(F32))
        ppv = pp_ref[...].astype(F32)
        dpp_ref[...] = (dv * sig).astype(BF16)
        dgp_ref[...] = (dv * ppv * sig * (1.0 - sig)).astype(BF16)

    row = pl.BlockSpec((tm, d), lambda i: (i, 0))
    sds = jax.ShapeDtypeStruct((s, d), BF16)
    return pl.pallas_call(
        body, name=name, grid=(s // tm,), in_specs=[row, row, row], out_specs=[row, row], out_shape=[sds, sds],
        compiler_params=_params(("parallel",)))(dh, gpre, pp)


def _loss_fwd_bwd(y, t, name):
    s, d = y.shape
    tm = _pick(s, (512, 256, 128))

    def body(y_ref, t_ref, dy_ref, l_ref):
        @pl.when(pl.program_id(0) == 0)
        def _():
            l_ref[...] = jnp.zeros_like(l_ref)

        e = y_ref[...] - t_ref[...]
        dy_ref[...] = e * (1.0 / d)
        l_ref[...] += jnp.sum(e * e, axis=0, keepdims=True) * (0.5 / d)

    row = pl.BlockSpec((tm, d), lambda i: (i, 0))
    vec = pl.BlockSpec((1, d), lambda i: (0, 0))
    return pl.pallas_call(
        body, name=name, grid=(s // tm,), in_specs=[row, row], out_specs=[row, vec],
        out_shape=[jax.ShapeDtypeStruct((s, d), F32), jax.ShapeDtypeStruct((1, d), F32)],
        compiler_params=_params(("arbitrary",)))(y, t)


PADR = 8
CONV_ROWS = 256


def _conv_fwd(proj, w_conv, d, name):
    s = proj.shape[0]
    nh = d // HEAD
    kw = w_conv.shape[0]
    qscale = HEAD ** -0.5

    tr = _pick(s, (CONV_ROWS,))

    def body(x_ref, w_ref, o_ref, xp):
        kind = pl.program_id(0) // nh
        xp[0:PADR, :] = jnp.zeros((PADR, HEAD), F32)
        xp[PADR:, :] = x_ref[...]
        taps = [w_ref[j:j + 1, :] for j in range(kw)]
        for r0 in range(0, s, tr):
            acc = jnp.zeros((tr, HEAD), F32)
            for j in range(kw):
                acc = acc + taps[j] * xp[r0 + PADR - (kw - 1) + j:r0 + PADR - (kw - 1) + j + tr, :]
            a = _silu(acc)
            r = lax.rsqrt(jnp.sum(a * a, axis=-1, keepdims=True) + EPS)
            fac = jnp.where(kind == 0, r * qscale, jnp.where(kind == 1, r, jnp.ones_like(r)))
            o_ref[r0:r0 + tr, :] = a * fac

    blk = pl.BlockSpec((s, HEAD), lambda c: (0, c))
    hm = pl.BlockSpec((None, s, HEAD), lambda c: (c, 0, 0))
    return pl.pallas_call(
        body, name=name, grid=(3 * nh,), in_specs=[blk, pl.BlockSpec((kw, HEAD), lambda c: (0, c))], out_specs=hm,
        out_shape=jax.ShapeDtypeStruct((3 * nh, s, HEAD), F32), scratch_shapes=[pltpu.VMEM((s + PADR, HEAD), F32)],
        compiler_params=_params(("parallel",)))(proj, w_conv)


def _conv_bwd(dqkv, proj, w_conv, d, name, dproj):
    s = proj.shape[0]
    nh = d // HEAD
    kw = w_conv.shape[0]
    qscale = HEAD ** -0.5

    tr = _pick(s, (CONV_ROWS,))

    def body(dy_ref, x_ref, w_ref, _, dx_ref, dw_ref, xp, dp):
        kind = pl.program_id(0) // nh
        xp[0:PADR, :] = jnp.zeros((PADR, HEAD), F32)
        xp[PADR:, :] = x_ref[...]
        dp[s:, :] = jnp.zeros((PADR, HEAD), F32)
        taps = [w_ref[j:j + 1, :] for j in range(kw)]
        sc = jnp.where(kind == 0, qscale, 1.0)
        dws = [jnp.zeros((1, HEAD), F32) for _ in range(kw)]
        for r0 in range(0, s, tr):
            acc = jnp.zeros((tr, HEAD), F32)
            for j in range(kw):
                acc = acc + taps[j] * xp[r0 + PADR - (kw - 1) + j:r0 + PADR - (kw - 1) + j + tr, :]
            a, da_dacc = _silu_and_grad(acc)
            dy = dy_ref[r0:r0 + tr, :]
            r = lax.rsqrt(jnp.sum(a * a, axis=-1, keepdims=True) + EPS)
            dyn = dy * sc
            da_norm = r * dyn - a * (r * r * r) * jnp.sum(a * dyn, axis=-1, keepdims=True)
            dacc = jnp.where(kind == 2, dy, da_norm) * da_dacc
            dp[r0:r0 + tr, :] = dacc
            for j in range(kw):
                sh = kw - 1 - j
                dws[j] = dws[j] + jnp.sum(dacc * xp[r0 + PADR - sh:r0 + PADR - sh + tr, :], axis=0, keepdims=True)
        for j in range(kw):
            dw_ref[j:j + 1, :] = dws[j]
        for r0 in range(0, s, tr):
            dx = jnp.zeros((tr, HEAD), F32)
            for j in range(kw):
                sh = kw - 1 - j
                dx = dx + taps[j] * dp[r0 + sh:r0 + sh + tr, :]
            dx_ref[r0:r0 + tr, :] = dx.astype(BF16)

    blk = pl.BlockSpec((s, HEAD), lambda c: (0, c))
    hm = pl.BlockSpec((None, s, HEAD), lambda c: (c, 0, 0))
    wblk = pl.BlockSpec((kw, HEAD), lambda c: (0, c))
    return pl.pallas_call(
        body, name=name, grid=(3 * nh,), in_specs=[hm, blk, wblk, pl.BlockSpec(memory_space=pl.ANY)],
        out_specs=[blk, wblk], input_output_aliases={3: 0},
        out_shape=[jax.ShapeDtypeStruct(dproj.shape, BF16), jax.ShapeDtypeStruct((kw, 3 * d), F32)],
        scratch_shapes=[pltpu.VMEM((s + PADR, HEAD), F32), pltpu.VMEM((s + PADR, HEAD), F32)],
        compiler_params=_params(("parallel",)))(dqkv, proj, w_conv, dproj)


def _softplus(x):
    return jnp.maximum(x, 0.0) + jnp.log(1.0 + jnp.exp(-jnp.abs(x)))


def _gates_fwd(pab, a_log, dt_bias, nh, name):
    s = pab.shape[0]
    tm = _pick(s, (512, 256, 128))

    def body(x_ref, al_ref, dt_ref, o_ref):
        x = x_ref[...]
        lane = lax.broadcasted_iota(jnp.int32, x.shape, 1)
        g = -jnp.exp(al_ref[...]) * _softplus(x + dt_ref[...])
        o_ref[...] = jnp.where(lane < nh, g, jnp.where(lane < 2 * nh, jax.nn.sigmoid(x), 0.0))

    row = pl.BlockSpec((tm, HEAD), lambda i: (i, 0))
    vec = pl.BlockSpec((1, HEAD), lambda i: (0, 0))
    return pl.pallas_call(
        body, name=name, grid=(s // tm,), in_specs=[row, vec, vec], out_specs=row,
        out_shape=jax.ShapeDtypeStruct((s, HEAD), F32), compiler_params=_params(("parallel",)))(pab, a_log, dt_bias)


def _gates_bwd(dgb, pab, a_log, dt_bias, nh, name):
    s = pab.shape[0]
    tm = _pick(s, (512, 256, 128))

    def body(d_ref, x_ref, al_ref, dt_ref, dx_ref, dal_ref, ddt_ref):
        @pl.when(pl.program_id(0) == 0)
        def _():
            dal_ref[...] = jnp.zeros_like(dal_ref)
            ddt_ref[...] = jnp.zeros_like(ddt_ref)

        x = x_ref[...]
        dv = d_ref[...]
        lane = lax.broadcasted_iota(jnp.int32, x.shape, 1)
        ea = jnp.exp(al_ref[...])
        xs = x + dt_ref[...]
        g = -ea * _softplus(xs)
        dxs = jnp.where(lane < nh, dv * (-ea) * jax.nn.sigmoid(xs), 0.0)
        sg = jax.nn.sigmoid(x)
        dxb = jnp.where((lane >= nh) & (lane < 2 * nh), dv * sg * (1.0 - sg), 0.0)
        dx_ref[...] = (dxs + dxb).astype(BF16)
        dal_ref[...] += jnp.sum(jnp.where(lane < nh, dv * g, 0.0), axis=0, keepdims=True)
        ddt_ref[...] += jnp.sum(dxs, axis=0, keepdims=True)

    row = pl.BlockSpec((tm, HEAD), lambda i: (i, 0))
    vec = pl.BlockSpec((1, HEAD), lambda i: (0, 0))
    return pl.pallas_call(
        body, name=name, grid=(s // tm,), in_specs=[row, row, vec, vec], out_specs=[row, vec, vec],
        out_shape=[jax.ShapeDtypeStruct((s, HEAD), BF16), jax.ShapeDtypeStruct((1, HEAD), F32),
                   jax.ShapeDtypeStruct((1, HEAD), F32)],
        compiler_params=_params(("arbitrary",)))(dgb, pab, a_log, dt_bias)


def _tri_inv(a_low, eye_f):
    n = -a_low
    p = eye_f + n
    steps = int(math.log2(a_low.shape[-1])) - 1
    for _ in range(steps):
        n = _dot(n, n, BNN)
        p = p + _dot(p, n, BNN)
    return p


def _lane_col(x, lane, idx):
    return jnp.sum(jnp.where(lane == idx, x, 0.0), axis=1, keepdims=True)


def _head_cols(gbv, lo, nh):
    lane = lax.broadcasted_iota(jnp.int32, gbv.shape, 1)
    return jnp.stack([_lane_col(gbv, lane, lo + h) for h in range(nh)], axis=0)


def _gdn_chunk(q, k, v, g_col, beta_col, st):
    c = q.shape[1]
    r_i = lax.broadcasted_iota(jnp.int32, (c, c), 0)
    c_i = lax.broadcasted_iota(jnp.int32, (c, c), 1)
    incl = c_i <= r_i
    strict = c_i < r_i
    eye = c_i == r_i
    g_row = jnp.sum(jnp.where(eye, g_col, 0.0), axis=1, keepdims=True)
    gc_col = jnp.sum(jnp.where(incl, g_row, 0.0), axis=2, keepdims=True)
    gc_row = jnp.sum(jnp.where(eye, gc_col, 0.0), axis=1, keepdims=True)
    g_last = jnp.sum(g_col, axis=1, keepdims=True)
    decay = jnp.exp(jnp.where(incl, gc_col - gc_row, NEG))
    kk = _dot(k, k, BNT)
    a_low = jnp.where(strict, beta_col * kk * decay, 0.0)
    t_inv = _tri_inv(a_low, eye.astype(F32))
    e_g = jnp.exp(gc_col)
    bk = beta_col * e_g
    rhs = jnp.concatenate([v * beta_col, k * bk], axis=2)
    sol = _dot(t_inv, rhs, BNN)
    u, w = sol[:, :, :HEAD], sol[:, :, HEAD:]
    qk_raw = _dot(q, k, BNT)
    qk = qk_raw * decay
    q_dec = q * e_g
    e2 = jnp.exp(g_last - gc_col)
    k_dec = k * e2
    gl = jnp.exp(g_last)
    ws = _dot(jnp.concatenate([w, q_dec], axis=1), st, BNN)
    v_new = u - ws[:, :c]
    o = ws[:, c:] + _dot(qk, v_new, BNN)
    st_new = st * gl + _dot(k_dec, v_new, BTN)
    inter = dict(incl=incl, strict=strict, eye=eye, decay=decay, kk=kk, t_inv=t_inv, e_g=e_g, bk=bk, sol=sol, w=w,
                 qk_raw=qk_raw, qk=qk, q_dec=q_dec, e2=e2, k_dec=k_dec, gl=gl, v_new=v_new, c_i=c_i, r_i=r_i)
    return o, st_new, inter


def _gdn_fwd(qkv, gb, nh, name):
    s = qkv.shape[1]
    nc = s // CHUNK

    def body(q_ref, k_ref, v_ref, gb_ref, o_ref, st_ref, state):
        @pl.when(pl.program_id(0) == 0)
        def _():
            state[...] = jnp.zeros_like(state)

        gbv = gb_ref[...]
        st = state[...]
        st_ref[...] = st
        o, st_new, _ = _gdn_chunk(q_ref[...], k_ref[...], v_ref[...], _head_cols(gbv, 0, nh), _head_cols(gbv, nh, nh), st)
        o_ref[...] = o
        state[...] = st_new

    def qspec(part):
        return pl.BlockSpec((nh, CHUNK, HEAD), lambda n: (part, n, 0))

    return pl.pallas_call(
        body, name=name, grid=(nc,),
        in_specs=[qspec(0), qspec(1), qspec(2), pl.BlockSpec((CHUNK, HEAD), lambda n: (n, 0))],
        out_specs=[qspec(0), pl.BlockSpec((None, nh, HEAD, HEAD), lambda n: (n, 0, 0, 0))],
        out_shape=[jax.ShapeDtypeStruct((nh, s, HEAD), F32), jax.ShapeDtypeStruct((nc, nh, HEAD, HEAD), F32)],
        scratch_shapes=[pltpu.VMEM((nh, HEAD, HEAD), F32)],
        compiler_params=_params(("arbitrary",)))(qkv, qkv, qkv, gb)


def _gdn_bwd(qkv, gb, do, states, nh, name):
    s = qkv.shape[1]
    nc = s // CHUNK
    c = CHUNK

    def body(q_ref, k_ref, v_ref, gb_ref, do_ref, st_ref, dqkv_ref, dgb_ref, dstate):
        @pl.when(pl.program_id(0) == 0)
        def _():
            dstate[...] = jnp.zeros_like(dstate)

        gbv = gb_ref[...]
        lane = lax.broadcasted_iota(jnp.int32, gbv.shape, 1)
        q, k, v = q_ref[...], k_ref[...], v_ref[...]
        beta_col = _head_cols(gbv, nh, nh)
        st = st_ref[...]
        dst = dstate[...]
        dov = do_ref[...]
        _, _, it = _gdn_chunk(q, k, v, _head_cols(gbv, 0, nh), beta_col, st)
        incl, strict, eye, decay = it["incl"], it["strict"], it["eye"], it["decay"]
        dv_new = _dot(it["qk"], dov, BTN) + _dot(it["k_dec"], dst, BNN)
        d_qk = _dot(dov, it["v_new"], BNT)
        dd = _dot(jnp.concatenate([dov, -dv_new], axis=1), st, BNT)
        dq_dec, dw = dd[:, :c], dd[:, c:]
        dst_new = _dot(it["q_dec"], dov, BTN) + it["gl"] * dst - _dot(it["w"], dv_new, BTN)
        dgl = jnp.sum(jnp.sum(dst * st, axis=2, keepdims=True), axis=1, keepdims=True)
        dk_dec = _dot(it["v_new"], dst, BNT)
        dsol = jnp.concatenate([dv_new, dw], axis=2)
        drhs = _dot(it["t_inv"], dsol, BTN)
        d_a = jnp.where(strict, -_dot(drhs, it["sol"], BNT), 0.0)
        drhs_u, drhs_w = drhs[:, :, :HEAD], drhs[:, :, HEAD:]
        dvh = beta_col * drhs_u
        rw_k = jnp.sum(drhs_w * k, axis=2, keepdims=True)
        dbeta = jnp.sum(drhs_u * v, axis=2, keepdims=True) + it["e_g"] * rw_k
        dkh = it["bk"] * drhs_w
        dgc_col = it["bk"] * rw_k
        dkk = d_a * beta_col * decay
        dbeta = dbeta + jnp.sum(d_a * it["kk"] * decay, axis=2, keepdims=True)
        ddecay = d_a * beta_col * it["kk"]
        dkh = dkh + _dot(dkk, k, BNN) + _dot(dkk, k, BTN)
        dqk_raw = d_qk * decay
        ddecay = ddecay + d_qk * it["qk_raw"]
        dqh = _dot(dqk_raw, k, BNN)
        dkh = dkh + _dot(dqk_raw, q, BTN)
        ddm = jnp.where(incl, ddecay * decay, 0.0)
        dgc_col = dgc_col + jnp.sum(ddm, axis=2, keepdims=True)
        dgc_row = -jnp.sum(ddm, axis=1, keepdims=True)
        dqh = dqh + dq_dec * it["e_g"]
        dgc_col = dgc_col + jnp.sum(dq_dec * it["q_dec"], axis=2, keepdims=True)
        dkh = dkh + dk_dec * it["e2"]
        tmp = jnp.sum(dk_dec * it["k_dec"], axis=2, keepdims=True)
        dgc_col = dgc_col - tmp
        dg_last = jnp.sum(tmp, axis=1, keepdims=True) + dgl * it["gl"]
        dgc_tot_row = dgc_row + jnp.sum(jnp.where(eye, dgc_col, 0.0), axis=1, keepdims=True)
        dg_col = jnp.sum(jnp.where(it["c_i"] >= it["r_i"], dgc_tot_row, 0.0), axis=2, keepdims=True) + dg_last
        dqkv_ref[0] = dqh
        dqkv_ref[1] = dkh
        dqkv_ref[2] = dvh
        dstate[...] = dst_new
        dgb_acc = jnp.zeros(gbv.shape, F32)
        for h in range(nh):
            dgb_acc = jnp.where(lane == h, dg_col[h], jnp.where(lane == nh + h, dbeta[h], dgb_acc))
        dgb_ref[...] = dgb_acc

    def rev(part):
        return pl.BlockSpec((nh, CHUNK, HEAD), lambda n: (part, nc - 1 - n, 0))

    gspec = pl.BlockSpec((CHUNK, HEAD), lambda n: (nc - 1 - n, 0))
    dqkv, dgb = pl.pallas_call(
        body, name=name, grid=(nc,),
        in_specs=[rev(0), rev(1), rev(2), gspec, rev(0),
                  pl.BlockSpec((None, nh, HEAD, HEAD), lambda n: (nc - 1 - n, 0, 0, 0))],
        out_specs=[pl.BlockSpec((3, nh, CHUNK, HEAD), lambda n: (0, 0, nc - 1 - n, 0)), gspec],
        out_shape=[jax.ShapeDtypeStruct((3, nh, s, HEAD), F32), jax.ShapeDtypeStruct((s, HEAD), F32)],
        scratch_shapes=[pltpu.VMEM((nh, HEAD, HEAD), F32)],
        compiler_params=_params(("arbitrary",)))(qkv, qkv, qkv, gb, do, states)
    return dqkv.reshape(3 * nh, s, HEAD), dgb


SB_TQ_FWD = 1024
SB_TQ = 512


def _tri01(rel):
    j_i = lax.broadcasted_iota(jnp.int32, (2 * SBLK, SBLK), 0) & (SBLK - 1)
    s_i = lax.broadcasted_iota(jnp.int32, (2 * SBLK, SBLK), 1)
    return rel(j_i, s_i).astype(BF16)


SB_HP = 2


def _each(fn, *lists):
    return [fn(*xs) for xs in zip(*lists)]


def _sb_scores(qts, kblks, mask, csums, rhs01, one_dot):
    zs = _each(lambda qt, kb: _dot(qt, kb, NT), qts, kblks)
    es = _each(lambda z: jnp.exp(-jnp.abs(z)), zs)
    sps = _each(lambda z, e: jnp.maximum(z, 0.0) + jnp.log(1.0 + e), zs, es)
    lns = _each(lambda sp: -sp if mask is None else jnp.where(mask, -sp, 0.0), sps)
    sts = _each(lambda ln: _dot_hilo(ln, rhs01, one_dot), lns)
    wgts = _each(lambda z, sp, st, cs: jnp.exp((z - sp) + st + cs), zs, sps, sts, csums)
    if mask is not None:
        wgts = _each(lambda w: jnp.where(mask, w, 0.0), wgts)
    return zs, es, wgts, lns


def _band_mask(rows, j, row0):
    r_i = lax.broadcasted_iota(jnp.int32, (rows, SBLK), 0)
    c_i = lax.broadcasted_iota(jnp.int32, (rows, SBLK), 1)
    return (j * SBLK + c_i) < (row0 + r_i)


def _sb_fwd(q, k, v, name):
    s, d = q.shape
    nh = d // HEAD
    tq = min(SB_TQ_FWD, s)
    nb = tq // SBLK

    hp = SB_HP
    heads = [slice(h * HEAD, (h + 1) * HEAD) for h in range(hp)]

    def body(q_ref, k_ref, v_ref, o_ref, c_ref, acc, cs):
        qb = pl.program_id(1)
        lane = lax.broadcasted_iota(jnp.int32, (tq, HEAD), 1)
        after = _tri01(lambda j, t: j > t)
        acc[...] = jnp.zeros_like(acc)
        cs[...] = jnp.zeros_like(cs)
        c_ref[...] = jnp.zeros_like(c_ref)

        def process(rs, kb, mask):
            keys = pl.ds(pl.multiple_of(kb * SBLK, SBLK), SBLK)
            csums = [cs[h, rs, :] for h in range(hp)]
            _, _, wgts, lns = _sb_scores([q_ref[rs, hs] for hs in heads], [k_ref[keys, hs] for hs in heads], mask, csums, after, True)
            pvs = _each(lambda w, hs: _dot(w, v_ref[keys, hs]), wgts, heads)
            tots = _each(lambda ln: jnp.sum(ln, axis=1, keepdims=True), lns)
            for h, hs in enumerate(heads):
                acc[h, rs, :] += pvs[h]
                c_ref[rs, hs] = jnp.where(lane[rs, :] == kb, csums[h], c_ref[rs, hs])
                cs[h, rs, :] = csums[h] + tots[h]

        for j in reversed(range(nb)):
            process(slice(j * SBLK, tq), qb * nb + j, _band_mask(tq - j * SBLK, j, j * SBLK))

        def step(it, carry):
            process(slice(0, tq), qb * nb - 1 - it, None)
            return carry

        lax.fori_loop(0, qb * nb, step, 0)
        for h, hs in enumerate(heads):
            o_ref[:, hs] = acc[h].astype(BF16)

    qspec = pl.BlockSpec((tq, hp * HEAD), lambda h, i: (i, h))
    kspec = pl.BlockSpec((s, hp * HEAD), lambda h, i: (0, h))
    return pl.pallas_call(
        body, name=name, grid=(nh // hp, s // tq), in_specs=[qspec, kspec, kspec], out_specs=[qspec, qspec],
        out_shape=[jax.ShapeDtypeStruct((s, d), BF16), jax.ShapeDtypeStruct((s, d), F32)],
        scratch_shapes=[pltpu.VMEM((hp, tq, HEAD), F32), pltpu.VMEM((hp, tq, 1), F32)],
        compiler_params=_params(("parallel", "arbitrary")))(q, k, v)


def _sb_bwd(q, k, v, do, ctab, name):
    s, d = q.shape
    nh = d // HEAD
    tq = min(SB_TQ, s)
    nb = tq // SBLK

    hp = SB_HP
    heads = [slice(h * HEAD, (h + 1) * HEAD) for h in range(hp)]

    def body(q_ref, k_ref, v_ref, do_ref, c_ref, dq_ref, dk_ref, dv_ref, ps):
        qb = pl.program_id(1)

        @pl.when(qb == 0)
        def _():
            dk_ref[...] = jnp.zeros_like(dk_ref)
            dv_ref[...] = jnp.zeros_like(dv_ref)

        dq_ref[...] = jnp.zeros_like(dq_ref)
        ps[...] = jnp.zeros_like(ps)
        lane = lax.broadcasted_iota(jnp.int32, (tq, HEAD), 1)
        after = _tri01(lambda j, t: j > t)
        before = _tri01(lambda j, t: j < t)

        def process(rs, kb, mask):
            keys = pl.ds(pl.multiple_of(kb * SBLK, SBLK), SBLK)
            kblks = [k_ref[keys, hs] for hs in heads]
            qts = [q_ref[rs, hs] for hs in heads]
            dots = [do_ref[rs, hs] for hs in heads]
            csums = [_lane_col(c_ref[rs, hs], lane[rs, :], kb) for hs in heads]
            zs, es, wgts, _ = _sb_scores(qts, kblks, mask, csums, after, False)
            dlws = _each(lambda dt, hs, w: _dot(dt, v_ref[keys, hs], NT) * w, dots, heads, wgts)
            pts = _each(lambda dlw: _dot_hilo(dlw, before, False), dlws)
            pfxs = [ps[h, rs, :] for h in range(hp)]
            rs_ = _each(lambda e: 1.0 / (1.0 + e), es)
            sigs = _each(lambda z, e, r: jnp.where(z >= 0.0, r, e * r), zs, es, rs_)
            dzs = _each(lambda dlw, sig, pfx, pt: dlw * (1.0 - sig) - sig * (pfx + pt), dlws, sigs, pfxs, pts)
            tots = _each(lambda dlw: jnp.sum(dlw, axis=1, keepdims=True), dlws)
            if mask is not None:
                dzs = _each(lambda dz: jnp.where(mask, dz, 0.0), dzs)
            dqs = _each(lambda dz, kb_: _dot(dz, kb_), dzs, kblks)
            dks = _each(lambda dz, qt: _dot(dz, qt, TN), dzs, qts)
            dvs = _each(lambda w, dt: _dot(w, dt, TN), wgts, dots)
            for h, hs in enumerate(heads):
                dq_ref[rs, hs] += dqs[h]
                dk_ref[keys, hs] += dks[h]
                dv_ref[keys, hs] += dvs[h]
                ps[h, rs, :] = pfxs[h] + tots[h]

        def step(kb, carry):
            process(slice(0, tq), kb, None)
            return carry

        lax.fori_loop(0, qb * nb, step, 0)
        for j in range(nb):
            process(slice(j * SBLK, tq), qb * nb + j, _band_mask(tq - j * SBLK, j, j * SBLK))

    qspec = pl.BlockSpec((tq, hp * HEAD), lambda h, i: (i, h))
    kspec = pl.BlockSpec((s, hp * HEAD), lambda h, i: (0, h))
    sds = jax.ShapeDtypeStruct((s, d), F32)
    return pl.pallas_call(
        body, name=name, grid=(nh // hp, s // tq), in_specs=[qspec, kspec, kspec, qspec, qspec],
        out_specs=[qspec, kspec, kspec], out_shape=[sds, sds, sds],
        scratch_shapes=[pltpu.VMEM((hp, tq, 1), F32)],
        compiler_params=_params(("parallel", "arbitrary")))(q, k, v, do, ctab)


def _my_index():
    return 4 * lax.axis_index("x") + 2 * lax.axis_index("y") + lax.axis_index("c")


def _all_gather(x_shard, name):
    m_per, n = x_shard.shape

    def body(x_ref, out_ref, send_sems, recv_sems, local_sem):
        x, y, c = lax.axis_index("x"), lax.axis_index("y"), lax.axis_index("c")
        me, sibling = (x, y, c), (x, y, 1 - c)
        chips = [(1 - x, y), (x, 1 - y), (1 - x, 1 - y)]

        def rows(px, py, pc):
            return out_ref.at[pl.ds((4 * px + 2 * py + pc) * m_per, m_per), :]

        def copy(k, block, to, src=None):
            return pltpu.make_async_remote_copy(
                src_ref=rows(*block) if src is None else src, dst_ref=rows(*block),
                send_sem=send_sems.at[k], recv_sem=recv_sems.at[k], device_id=to, device_id_type=MESH)

        mine = pltpu.make_async_copy(x_ref, rows(*me), local_sem)
        mine.start()
        first = [copy(0, me, sibling, src=x_ref)]
        first += [copy(1 + j, me, (*chip, c), src=x_ref) for j, chip in enumerate(chips)]
        for cp in first:
            cp.start()
        passed = [copy(4 + j, (*chip, c), sibling) for j, chip in enumerate(chips)]
        for j, chip in enumerate(chips):
            copy(1 + j, (*chip, c), me).wait_recv()
            passed[j].start()
        copy(0, sibling, me).wait_recv()
        for j, chip in enumerate(chips):
            copy(4 + j, (*chip, 1 - c), me).wait_recv()
        for cp in first + passed:
            cp.wait_send()
        mine.wait()

    return pl.pallas_call(
        body, name=name, out_shape=jax.ShapeDtypeStruct((NDEV * m_per, n), x_shard.dtype),
        in_specs=[pl.BlockSpec(memory_space=pl.ANY)], out_specs=pl.BlockSpec(memory_space=pl.ANY),
        scratch_shapes=[pltpu.SemaphoreType.DMA((7,)), pltpu.SemaphoreType.DMA((7,)), pltpu.SemaphoreType.DMA],
    )(x_shard)


HBM_SPEC = pl.BlockSpec(memory_space=pltpu.HBM)
SEM_SPEC = pl.BlockSpec(memory_space=pltpu.SEMAPHORE)
ANY_SPEC = pl.BlockSpec(memory_space=pl.ANY)
EFFECT = pltpu.SideEffectType.DATAFLOW_SIDE_EFFECTING


def _exchange_copies(src_refs, land_refs, send_sems, recv_sems, self_sems, scatter):
    x, y, c = lax.axis_index("x"), lax.axis_index("y"), lax.axis_index("c")
    me = 4 * x + 2 * y + c
    remote, local = [], []
    for p, (src_ref, land_ref) in enumerate(zip(src_refs, land_refs)):
        rows = land_ref.shape[0] // NDEV

        def part(idx):
            return src_ref.at[pl.ds(idx * rows, rows), :] if scatter else src_ref

        slot = land_ref.at[pl.ds(me * rows, rows), :]
        for k in range(1, NDEV):
            px, py, pc = x ^ ((k >> 2) & 1), y ^ ((k >> 1) & 1), c ^ (k & 1)
            remote.append(pltpu.make_async_remote_copy(
                src_ref=part(4 * px + 2 * py + pc), dst_ref=slot, send_sem=send_sems.at[7 * p + k - 1],
                recv_sem=recv_sems.at[7 * p + k - 1], device_id=(px, py, pc), device_id_type=MESH))
        local.append(pltpu.make_async_copy(part(me), slot, self_sems.at[p]))
    return remote, local


def _send_start(srcs, scatter, after, name):
    n = len(srcs)
    lands = []
    for s in srcs:
        rows = s.shape[0] if scatter else NDEV * s.shape[0]
        lands.append(pltpu.with_memory_space_constraint(lax.empty((rows, s.shape[1]), s.dtype), pltpu.HBM))

    def body(*refs):
        src_refs, land_refs = refs[:n], refs[n:2 * n]
        send_sems, recv_sems, self_sems = refs[2 * n + 1:2 * n + 4]
        remote, local = _exchange_copies(src_refs, land_refs, send_sems, recv_sems, self_sems, scatter)
        for cp in remote + local:
            cp.start()
        refs[-1][...] = jnp.zeros_like(refs[-1])

    hbm = lambda a: pltpu.HBM(a.shape, a.dtype)
    out = pl.pallas_call(
        body, name=name,
        out_shape=(pltpu.SemaphoreType.DMA((7 * n,)), pltpu.SemaphoreType.DMA((7 * n,)), pltpu.SemaphoreType.DMA((n,)),
                   *[hbm(s) for s in srcs], *[hbm(a) for a in lands], jax.ShapeDtypeStruct((8, HEAD), F32)),
        in_specs=(HBM_SPEC,) * (2 * n) + (ANY_SPEC,),
        out_specs=(SEM_SPEC,) * 3 + (HBM_SPEC,) * (2 * n) + (pl.BlockSpec(memory_space=pltpu.VMEM),),
        input_output_aliases={i: 3 + i for i in range(2 * n)},
        compiler_params=pltpu.CompilerParams(has_side_effects=EFFECT),
    )(*[pltpu.with_memory_space_constraint(s, pltpu.HBM) for s in srcs], *lands, after)
    return dict(sems=out[:3], srcs=out[3:3 + n], lands=out[3 + n:3 + 2 * n], token=out[-1])


def _send_wait(started, scatter, after, name):
    srcs, lands = started["srcs"], started["lands"]
    n = len(srcs)

    def body(*refs):
        src_refs, land_refs = refs[:n], refs[n:2 * n]
        send_sems, recv_sems, self_sems = refs[2 * n:2 * n + 3]
        remote, local = _exchange_copies(src_refs, land_refs, send_sems, recv_sems, self_sems, scatter)
        for cp in remote:
            cp.wait_send()
            cp.wait_recv()
        for cp in local:
            cp.wait()

    hbm = lambda a: pltpu.HBM(a.shape, a.dtype)
    out = pl.pallas_call(
        body, name=name, out_shape=(*[hbm(s) for s in srcs], *[hbm(a) for a in lands]),
        in_specs=(HBM_SPEC,) * (2 * n) + (SEM_SPEC,) * 3 + (ANY_SPEC,), out_specs=(HBM_SPEC,) * (2 * n),
        input_output_aliases={i: i for i in range(2 * n)},
        compiler_params=pltpu.CompilerParams(has_side_effects=EFFECT),
    )(*srcs, *lands, *started["sems"], after)
    return out[n:]


def _sum_slots(xs, name, rows_out=None):
    _, r, c = xs[0].shape
    ro = rows_out or r
    tc = _pick(c, (128,))

    def body(*refs):
        o_ref = refs[-1]
        for l, x_ref in enumerate(refs[:-1]):
            acc = x_ref[0].astype(F32)
            for i in range(1, NDEV):
                acc = acc + x_ref[i].astype(F32)
            o_ref[l] = acc[:ro]

    return pl.pallas_call(
        body, name=name, grid=(c // tc,), in_specs=[pl.BlockSpec((NDEV, r, tc), lambda j: (0, 0, j))] * len(xs),
        out_specs=pl.BlockSpec((len(xs), ro, tc), lambda j: (0, 0, j)),
        out_shape=jax.ShapeDtypeStruct((len(xs), ro, c), F32), compiler_params=_params(("parallel",)))(*xs)


def _adamw(w, g, m, v, name):
    if w.ndim == 3:
        nl, r, c = w.shape
        tc = _pick(c, (256, 128))
        grid = (nl, c // tc)
        blk = pl.BlockSpec((None, r, tc), lambda i, j: (i, 0, j))
        sem = ("parallel", "parallel")
    else:
        r, c = w.shape
        tr = _pick(r, (256, 128, 64, 32, 16, 8))
        grid = (r // tr,)
        blk = pl.BlockSpec((tr, c), lambda i: (i, 0))
        sem = ("parallel",)
    c1 = 1.0 - B1 ** STEP
    c2 = 1.0 - B2 ** STEP

    def body(w_ref, g_ref, m_ref, v_ref, d_ref, nm_ref, nv_ref):
        gv = g_ref[...]
        nm = B1 * m_ref[...] + (1.0 - B1) * gv
        nv = B2 * v_ref[...] + (1.0 - B2) * (gv * gv)
        d_ref[...] = -LR * ((nm / c1) / (jnp.sqrt(nv / c2) + ADAM_EPS) + WD * w_ref[...])
        nm_ref[...] = nm
        nv_ref[...] = nv

    sds = jax.ShapeDtypeStruct(w.shape, F32)
    return pl.pallas_call(
        body, name=name, grid=grid, in_specs=[blk] * 4, out_specs=[blk] * 3, out_shape=[sds] * 3,
        compiler_params=_params(sem))(w, g, m, v)


def _pad_rows(a, mult):
    r = a.shape[0]
    pad = (-r) % mult
    return a if pad == 0 else jnp.pad(a, ((0, pad), (0, 0)))


def _pad_lanes(v, width=HEAD):
    return jnp.pad(v.reshape(1, -1), ((0, 0), (0, width - v.shape[-1])))


def kernel(x, p, ln_mix, ln_ffn, ln_ple, gdn_w_in, gdn_conv, gdn_a_log, gdn_dt_bias, gdn_norm, gdn_w_out, kv_norm, w_kv, k_norm, sb_w_q, sb_q_norm, sb_w_out, ffn_w_in, ffn_w_out, ple_w_proj, ple_w_gate, loss_target, m_ln_mix, m_ln_ffn, m_ln_ple, m_gdn_w_in, m_gdn_conv, m_gdn_a_log, m_gdn_dt_bias, m_gdn_norm, m_gdn_w_out, m_kv_norm, m_w_kv, m_k_norm, m_sb_w_q, m_sb_q_norm, m_sb_w_out, m_ffn_w_in, m_ffn_w_out, m_ple_w_proj, m_ple_w_gate, v_ln_mix, v_ln_ffn, v_ln_ple, v_gdn_w_in, v_gdn_conv, v_gdn_a_log, v_gdn_dt_bias, v_gdn_norm, v_gdn_w_out, v_kv_norm, v_w_kv, v_k_norm, v_sb_w_q, v_sb_q_norm, v_sb_w_out, v_ffn_w_in, v_ffn_w_out, v_ple_w_proj, v_ple_w_gate):
    s, d = x.shape[1], x.shape[2]
    nh = d // HEAD
    depth = ln_mix.shape[0]
    n_a = gdn_w_in.shape[0]
    n_b = sb_w_q.shape[0]
    me = _my_index()
    win_cols = gdn_w_in.shape[2]
    win_rows = 4 * d + 2 * nh

    def col_t(w):
        return jnp.transpose(w).astype(BF16)

    local = {}
    for l in range(n_a):
        local[("gdn_w_in", l)] = col_t(gdn_w_in[l])
        local[("gdn_w_out", l)] = gdn_w_out[l].astype(BF16)
    local[("w_kv", 0)] = col_t(w_kv)
    for j in range(n_b):
        local[("sb_w_q", j)] = sb_w_q[j].astype(BF16)
        local[("sb_w_out", j)] = sb_w_out[j].astype(BF16)
    for l in range(depth):
        local[("ffn_w_in", l)] = col_t(ffn_w_in[l])
        local[("ffn_w_out", l)] = ffn_w_out[l].astype(BF16)
        local[("ple_w_proj", l)] = col_t(ple_w_proj[l]).reshape(-1, d)
        local[("ple_w_gate", l)] = ple_w_gate[l].astype(BF16)
    local = {key: _pad_rows(a, 16) for key, a in local.items()}

    chunks = []
    for l in range(depth):
        mix = [("gdn_w_in", l), ("gdn_w_out", l)] if l < n_a else [("sb_w_q", l - n_a), ("sb_w_out", l - n_a)]
        rest = [("ffn_w_in", l), ("ffn_w_out", l), ("ple_w_proj", l), ("ple_w_gate", l)]
        if l == n_a - 1:
            rest.append(("w_kv", 0))
        chunks += [(f"a{l}", mix), (f"f{l}", rest)]
    chunk_keys = dict(chunks)

    conv_rows = n_a * gdn_conv.shape[1]
    conv_sh = _pad_rows(gdn_conv.reshape(conv_rows, -1), 8)
    conv_g = _all_gather(conv_sh, "comm_gather_conv")
    token = conv_g
    conv_g = conv_g.reshape(NDEV, conv_sh.shape[0], -1)
    conv_full = jnp.transpose(conv_g[:, :conv_rows, :], (1, 0, 2)).reshape(n_a, gdn_conv.shape[1], 3 * d)

    w_started = {}
    for name, keys in chunks:
        w_started[name] = _send_start([local[k] for k in keys], False, token, f"comm_wstart_{name}")
        token = w_started[name]["token"]

    full = {}

    def fetch(name, after):
        lands = _send_wait(w_started[name], False, after, f"comm_wwait_{name}")
        for key, land in zip(chunk_keys[name], lands):
            full[key] = land

    def whole(key, valid=None):
        a = full[key]
        if valid is not None:
            a = a.reshape(NDEV, -1, d)[:, :valid, :].reshape(-1, d)
        return a

    pd = p.shape[-1]
    w_in_t, w_ab_t, w_gout, w_q, w_sout, wf_t, w_fout, wp_t, w_pg = {}, {}, {}, {}, {}, {}, {}, {}, {}
    wkv_t = None

    h = x[0]
    sv = []
    kv_sv = None
    k_sh = v_sh = None
    for l in range(depth):
        t = {}
        t["h0"] = h
        if l == 0:
            hn = _rms_fwd(h, ln_mix[l], f"rms_mix_{l}")
        t["hn"] = hn
        fetch(f"a{l}", token if l == 0 else hn)
        if l < n_a:
            wt = whole(("gdn_w_in", l), win_cols)
            w_in_t[l] = wt[:4 * d]
            w_ab_t[l] = jnp.pad(wt[4 * d:], ((0, HEAD - 2 * nh), (0, 0)))
            w_gout[l] = whole(("gdn_w_out", l))
        else:
            w_q[l - n_a] = whole(("sb_w_q", l - n_a))
            w_sout[l - n_a] = whole(("sb_w_out", l - n_a))
        if l < n_a:
            proj = _mm(hn, w_in_t[l], "nt", f"gdn_proj_{l}")
            pab = _mm(hn, w_ab_t[l], "nt", f"gdn_proj_ab_{l}")
            qkv = _conv_fwd(proj, conv_full[l], d, f"gdn_conv_{l}")
            al, dtb = _pad_lanes(gdn_a_log[l]), _pad_lanes(gdn_dt_bias[l])
            gb = _gates_fwd(pab, al, dtb, nh, f"gdn_gates_{l}")
            o_raw, states = _gdn_fwd(qkv, gb, nh, f"gdn_rule_{l}")
            o2 = _headnorm_fwd(o_raw, gdn_norm[l], f"gdn_outnorm_{l}", gate=proj, gate_col0=3 * d, head_major=True)
            h, hn2 = _mm(o2, w_gout[l], "nn", f"gdn_out_{l}", res=h, norm_g=ln_ffn[l])
            t.update(proj=proj, pab=pab, qkv=qkv, gb=gb, o_raw=o_raw, states=states, o2=o2, al=al, dtb=dtb)
        else:
            j = l - n_a
            qpre = _mm(hn, w_q[j], "nn", f"sb_qproj_{j}")
            qn = _headnorm_fwd(qpre, sb_q_norm[j], f"sb_qnorm_{j}", scale=HEAD ** -0.5)
            o, ctab = _sb_fwd(qn, k_sh, v_sh, f"sb_attn_{j}")
            h, hn2 = _mm(o, w_sout[j], "nn", f"sb_out_{j}", res=h, norm_g=ln_ffn[l])
            t.update(qpre=qpre, qn=qn, o=o, ctab=ctab)
        t["h1"] = h
        fetch(f"f{l}", hn2)
        wf_t[l] = whole(("ffn_w_in", l))
        w_fout[l] = whole(("ffn_w_out", l))
        wp_t[l] = full[("ple_w_proj", l)].reshape(d, pd)
        w_pg[l] = whole(("ple_w_gate", l))
        if l == n_a - 1:
            wkv_t = whole(("w_kv", 0))
        act, gs, us = _swiglu_fwd(hn2, wf_t[l], f"ffn_in_{l}")
        h, hn3 = _mm(act, w_fout[l], "nn", f"ffn_out_{l}", res=h, norm_g=ln_ple[l])
        t.update(hn2=hn2, act=act, gs=gs, us=us, h2=h)
        gains = ([ln_mix[l + 1]] if l + 1 < depth else []) + ([kv_norm] if l == n_a - 1 else [])
        h, gpre, pp, *normed = _ple_fwd(h, hn3, p[l, 0], w_pg[l], wp_t[l], f"ple_{l}", norm_gs=gains)
        if l + 1 < depth:
            hn = normed[0]
        t.update(hn3=hn3, gpre=gpre, pp=pp)
        sv.append(t)
        if l == n_a - 1:
            kvn = normed[-1]
            kv = _mm(kvn, wkv_t, "nt", "kv_proj")
            k_sh = _headnorm_fwd(kv, k_norm, "k_norm", width=d)
            v_sh = kv[:, d:].astype(BF16)
            kv_sv = dict(h=h, kvn=kvn, kv=kv)

    dh, loss_vec = _loss_fwd_bwd(h, loss_target[0], "loss")
    loss = lax.psum(jnp.sum(loss_vec), ("x", "y", "c"))

    gw = {}
    small = {}
    g_started = {}

    def scatter_start(name):
        gparts = []
        for key in chunk_keys[name]:
            g = gw[key]
            g = g.reshape(NDEV, -1, d) if key[0] == "ple_w_proj" else g.reshape(NDEV, -1, g.shape[-1])
            padr = local[key].shape[0] - g.shape[1]
            if padr:
                g = jnp.pad(g, ((0, 0), (0, padr), (0, 0)))
            gparts.append(g.reshape(-1, d))
        g_started[name] = _send_start(gparts, True, jnp.zeros((8, HEAD), F32), f"comm_gstart_{name}")
        return g_started[name]["token"]

    dks, dvs = [], []
    for l in reversed(range(depth)):
        t = sv[l]
        if l == n_a - 1:
            dkv, dkn = _kv_grad(dks, dvs, kv_sv["kv"], k_norm, "k_norm_bwd")
            gw[("w_kv", 0)] = _mm(dkv, kv_sv["kvn"], "tn", "kv_dw", out_dtype=BF16)
            dh, _, dg = _mm(dkv, wkv_t, "nn", "kv_dx", norm_bwd=(kv_sv["h"], kv_norm, dh))
            small["kv_norm"] = dg
            small["k_norm"] = dkn
        dgp, dpp = _ple_bwd(dh, t["gpre"], t["pp"], f"ple_bwd_{l}")
        gw[("ple_w_gate", l)] = _mm(t["hn3"], dgp, "tn", f"ple_dwg_{l}", out_dtype=BF16)
        gw[("ple_w_proj", l)] = _mm(dpp, p[l, 0], "tn", f"ple_dwp_{l}", out_dtype=BF16)
        dh, dhb, dg = _mm(dgp, w_pg[l], "nt", f"ple_dx_{l}", norm_bwd=(t["h2"], ln_ple[l], dh))
        small[("ln_ple", l)] = dg
        dgu = _swiglu_bwd(dhb, w_fout[l], t["gs"], t["us"], f"ffn_bwd_act_{l}")
        gw[("ffn_w_out", l)] = _mm(t["act"], dhb, "tn", f"ffn_dwo_{l}", out_dtype=BF16)
        gw[("ffn_w_in", l)] = _mm(dgu, t["hn2"], "tn", f"ffn_dwi_{l}", out_dtype=BF16)
        dh, dhb, dg = _mm(dgu, wf_t[l], "nn", f"ffn_dx_{l}", norm_bwd=(t["h1"], ln_ffn[l], dh),
                          after=scatter_start(f"f{l}"))
        small[("ln_ffn", l)] = dg
        if l < n_a:
            do2 = _mm(dhb, w_gout[l], "nt", f"gdn_out_dx_{l}")
            gw[("gdn_w_out", l)] = _mm(t["o2"], dhb, "tn", f"gdn_out_dw_{l}", out_dtype=BF16)
            do_raw, dgn, dgate = _headnorm_bwd(do2, t["o_raw"], gdn_norm[l], f"gdn_outnorm_bwd_{l}",
                                               gate=t["proj"], gate_col0=3 * d, head_major=True)
            small[("gdn_norm", l)] = dgn
            dqkv, dgb = _gdn_bwd(t["qkv"], t["gb"], do_raw, t["states"], nh, f"gdn_rule_bwd_{l}")
            dpab, dal, ddt = _gates_bwd(dgb, t["pab"], t["al"], t["dtb"], nh, f"gdn_gates_bwd_{l}")
            small[("gdn_a_log", l)] = dal
            small[("gdn_dt_bias", l)] = ddt
            dproj, dconv = _conv_bwd(dqkv, t["proj"], conv_full[l], d, f"gdn_conv_bwd_{l}", dgate)
            small[("gdn_conv", l)] = dconv
            dw_main = _mm(dproj, t["hn"], "tn", f"gdn_proj_dw_{l}", out_dtype=BF16)
            dw_ab = _mm(dpab, t["hn"], "tn", f"gdn_proj_ab_dw_{l}", out_dtype=BF16)
            gw[("gdn_w_in", l)] = jnp.concatenate([dw_main, dw_ab[:16]], axis=0)[:win_rows]
            dhn_ab = _mm(dpab, w_ab_t[l], "nn", f"gdn_proj_ab_dx_{l}")
            last = dict(a=dproj, b=w_in_t[l], mode="nn", name=f"gdn_proj_dx_{l}", res=dhn_ab)
        else:
            j = l - n_a
            do = _mm(dhb, w_sout[j], "nt", f"sb_out_dx_{j}", out_dtype=BF16)
            gw[("sb_w_out", j)] = _mm(t["o"], dhb, "tn", f"sb_out_dw_{j}", out_dtype=BF16)
            dq, dk, dv = _sb_bwd(t["qn"], k_sh, v_sh, do, t["ctab"], f"sb_attn_bwd_{j}")
            dks.append(dk)
            dvs.append(dv)
            dqpre, dqn = _headnorm_bwd(dq, t["qpre"], sb_q_norm[j], f"sb_qnorm_bwd_{j}", scale=HEAD ** -0.5, dx_dtype=BF16)
            small[("sb_q_norm", j)] = dqn
            gw[("sb_w_q", j)] = _mm(t["hn"], dqpre, "tn", f"sb_q_dw_{j}", out_dtype=BF16)
            last = dict(a=dqpre, b=w_q[j], mode="nt", name=f"sb_q_dx_{j}")
        dh, _, dg = _mm(**last, norm_bwd=(t["h0"], ln_mix[l], dh), after=scatter_start(f"a{l}"))
        small[("ln_mix", l)] = dg
    grad_x = dh[None]

    landed = {}
    for name, keys in reversed(chunks):
        lands = _send_wait(g_started[name], True, dh, f"comm_gwait_{name}")
        for key, land in zip(keys, lands):
            landed[key] = land.reshape(NDEV, -1, d)

    def summed(wname, count, rows_out=None):
        return _sum_slots([landed[(wname, i)] for i in range(count)], f"grad_sum_{wname}", rows_out)

    gt_gdn_w_in = summed("gdn_w_in", n_a, win_cols)
    gt_ffn_w_in = summed("ffn_w_in", depth)
    g_gdn_w_in = jnp.transpose(gt_gdn_w_in, (0, 2, 1))
    g_gdn_w_out = summed("gdn_w_out", n_a)
    g_w_kv = jnp.transpose(summed("w_kv", 1)[0])
    g_sb_w_q = summed("sb_w_q", n_b)
    g_sb_w_out = summed("sb_w_out", n_b)
    g_ffn_w_in = jnp.transpose(gt_ffn_w_in, (0, 2, 1))
    g_ffn_w_out = summed("ffn_w_out", depth)
    g_ple_w_proj = jnp.transpose(summed("ple_w_proj", depth).reshape(depth, -1, pd), (0, 2, 1))
    g_ple_w_gate = summed("ple_w_gate", depth)

    def vec_rows(v):
        return v.reshape(-1, HEAD)

    small_items = []
    for name_, cnt in (("ln_mix", depth), ("ln_ffn", depth), ("ln_ple", depth)):
        for l in range(cnt):
            small_items.append(((name_, l), vec_rows(small[(name_, l)])))
    for l in range(n_a):
        small_items.append((("gdn_conv", l), small[("gdn_conv", l)].reshape(-1, HEAD)))
        small_items.append((("gdn_a_log", l), small[("gdn_a_log", l)]))
        small_items.append((("gdn_dt_bias", l), small[("gdn_dt_bias", l)]))
        small_items.append((("gdn_norm", l), small[("gdn_norm", l)]))
    small_items.append(("kv_norm", vec_rows(small["kv_norm"])))
    small_items.append(("k_norm", small["k_norm"]))
    for j in range(n_b):
        small_items.append((("sb_q_norm", j), small[("sb_q_norm", j)]))
    spack = jnp.concatenate([_pad_rows(a, 8) for _, a in small_items], axis=0)
    sg = _all_gather(spack, "comm_gather_small").reshape(NDEV, spack.shape[0], HEAD)
    ssum = _sum_slots([sg], "small_sum")[0]
    sm = {}
    off = 0
    for key, a in small_items:
        sm[key] = ssum[off:off + a.shape[0]]
        off += a.shape[0] + (-a.shape[0]) % 8

    g_ln_mix = jnp.stack([sm[("ln_mix", l)].reshape(d) for l in range(depth)])
    g_ln_ffn = jnp.stack([sm[("ln_ffn", l)].reshape(d) for l in range(depth)])
    g_ln_ple = jnp.stack([sm[("ln_ple", l)].reshape(d) for l in range(depth)])
    conv_loc = gdn_conv.shape[2]
    g_conv_full = jnp.stack([sm[("gdn_conv", l)].reshape(gdn_conv.shape[1], 3 * d) for l in range(n_a)])
    g_gdn_conv = lax.dynamic_slice_in_dim(g_conv_full, me * conv_loc, conv_loc, axis=2)
    g_a_log = jnp.stack([sm[("gdn_a_log", l)][0, :nh] for l in range(n_a)])
    g_dt_bias = jnp.stack([sm[("gdn_dt_bias", l)][0, :nh] for l in range(n_a)])
    g_gdn_norm = jnp.stack([sm[("gdn_norm", l)][0] for l in range(n_a)])
    g_kv_norm = sm["kv_norm"].reshape(d)
    g_k_norm = sm["k_norm"][0]
    g_sb_q_norm = jnp.stack([sm[("sb_q_norm", j)][0] for j in range(n_b)])

    grads = [g_ln_mix, g_ln_ffn, g_ln_ple, g_gdn_w_in, g_gdn_conv, g_a_log, g_dt_bias, g_gdn_norm, g_gdn_w_out,
             g_kv_norm, g_w_kv, g_k_norm, g_sb_w_q, g_sb_q_norm, g_sb_w_out, g_ffn_w_in, g_ffn_w_out, g_ple_w_proj,
             g_ple_w_gate]
    weights = [ln_mix, ln_ffn, ln_ple, gdn_w_in, gdn_conv, gdn_a_log, gdn_dt_bias, gdn_norm, gdn_w_out, kv_norm, w_kv,
               k_norm, sb_w_q, sb_q_norm, sb_w_out, ffn_w_in, ffn_w_out, ple_w_proj, ple_w_gate]
    moms = [m_ln_mix, m_ln_ffn, m_ln_ple, m_gdn_w_in, m_gdn_conv, m_gdn_a_log, m_gdn_dt_bias, m_gdn_norm, m_gdn_w_out,
            m_kv_norm, m_w_kv, m_k_norm, m_sb_w_q, m_sb_q_norm, m_sb_w_out, m_ffn_w_in, m_ffn_w_out, m_ple_w_proj,
            m_ple_w_gate]
    vels = [v_ln_mix, v_ln_ffn, v_ln_ple, v_gdn_w_in, v_gdn_conv, v_gdn_a_log, v_gdn_dt_bias, v_gdn_norm, v_gdn_w_out,
            v_kv_norm, v_w_kv, v_k_norm, v_sb_w_q, v_sb_q_norm, v_sb_w_out, v_ffn_w_in, v_ffn_w_out, v_ple_w_proj,
            v_ple_w_gate]

    deltas, new_m, new_v = [], [], []
    small_idx = [i for i, w in enumerate(weights) if w.size < 8 * HEAD * 16]
    transposed = {3: gt_gdn_w_in, 15: gt_ffn_w_in}
    for i, (w, g, m, v) in enumerate(zip(weights, grads, moms, vels)):
        if i in small_idx:
            deltas.append(None), new_m.append(None), new_v.append(None)
            continue
        if i in transposed:
            tr = lambda a: jnp.transpose(a, (0, 2, 1))
            dl, nm, nv = _adamw(tr(w), transposed[i], tr(m), tr(v), f"adamw_{i}")
            deltas.append(tr(dl)), new_m.append(tr(nm)), new_v.append(tr(nv))
            continue
        shp = w.shape
        two = lambda a: a.reshape(-1, shp[-1])
        dl, nm, nv = _adamw(two(w), two(g), two(m), two(v), f"adamw_{i}")
        deltas.append(dl.reshape(shp)), new_m.append(nm.reshape(shp)), new_v.append(nv.reshape(shp))

    def flat_pack(arrs):
        flat = jnp.concatenate([a.reshape(-1) for a in arrs])
        pad = (-flat.shape[0]) % (8 * HEAD)
        return jnp.pad(flat, (0, pad)).reshape(-1, HEAD)

    sw = flat_pack([weights[i] for i in small_idx])
    sgr = flat_pack([grads[i] for i in small_idx])
    smo = flat_pack([moms[i] for i in small_idx])
    sve = flat_pack([vels[i] for i in small_idx])
    sdl, snm, snv = _adamw(sw, sgr, smo, sve, "adamw_small")
    off = 0
    for i in small_idx:
        n = weights[i].size
        shp = weights[i].shape
        deltas[i] = sdl.reshape(-1)[off:off + n].reshape(shp)
        new_m[i] = snm.reshape(-1)[off:off + n].reshape(shp)
        new_v[i] = snv.reshape(-1)[off:off + n].reshape(shp)
        off += n

    return (loss, grad_x, *grads, *deltas, *new_m, *new_v)
```

```python
import math

import jax
import jax.numpy as jnp
from jax import lax
from jax.experimental import pallas as pl
from jax.experimental.pallas import tpu as pltpu

F32 = jnp.float32
BF16 = jnp.bfloat16
NDEV = 8
HEAD = 128
CHUNK = 64
SBLK = 256
EPS = 1e-6
LR, B1, B2, ADAM_EPS, WD, STEP = 0.001, 0.9, 0.999, 1e-08, 0.01, 10
NEG = -1e30
MM_VMEM_BUDGET = 40 * 1024 * 1024

NN = (((1,), (0,)), ((), ()))
NT = (((1,), (1,)), ((), ()))
TN = (((0,), (0,)), ((), ()))
BNN = (((2,), (1,)), ((0,), (0,)))
BNT = (((2,), (2,)), ((0,), (0,)))
BTN = (((1,), (1,)), ((0,), (0,)))
MESH = pl.DeviceIdType.MESH


def _dot(a, b, dims=NN):
    return lax.dot_general(a.astype(BF16), b.astype(BF16), dims, preferred_element_type=F32)


def _dot_hilo(a, b01_twice, one_dot):
    hi = a.astype(BF16)
    lo = (a - hi.astype(F32)).astype(BF16)
    if one_dot:
        return lax.dot_general(jnp.concatenate([hi, lo], axis=1), b01_twice, NN, preferred_element_type=F32)
    b01 = b01_twice[:a.shape[1]]
    return (lax.dot_general(hi, b01, NN, preferred_element_type=F32)
            + lax.dot_general(lo, b01, NN, preferred_element_type=F32))


def _pick(dim, cands):
    for c in cands:
        if dim % c == 0:
            return c
    return dim


def _params(sem, vmem_mb=48):
    return pltpu.CompilerParams(dimension_semantics=sem, vmem_limit_bytes=vmem_mb * 1024 * 1024)


def _silu(x):
    return x * jax.nn.sigmoid(x)


def _silu_and_grad(x):
    s = jax.nn.sigmoid(x)
    xs = x * s
    return xs, s + xs * (1.0 - s)


def _mm(a, b, mode, name, out_dtype=F32, res=None, norm_g=None, norm_bwd=None, after=None):
    if mode == "nn":
        (m, k), n = a.shape, b.shape[1]
    elif mode == "nt":
        (m, k), n = a.shape, b.shape[0]
    else:
        (k, m), n = a.shape, b.shape[1]
    rows = norm_g is not None or norm_bwd is not None
    tn = n if rows else _pick(n, (512, 256, 128))
    tk = k if k <= 4096 else max(t for t in range(128, 4097, 128) if k % t == 0)
    nk = k // tk
    out_b = jnp.dtype(out_dtype).itemsize + (res.dtype.itemsize if res is not None else 0)
    out_b += 2 if norm_g is not None else 0
    out_b += 10 if norm_bwd is not None else 0
    for tm in [t for t in range(min(m, 2048), 127, -128) if m % t == 0] + [m]:
        need = 2 * (tm * tk * a.dtype.itemsize + tk * tn * b.dtype.itemsize + tm * tn * out_b) + 4 * tm * tn
        if need <= MM_VMEM_BUDGET:
            break
    dims = {"nn": NN, "nt": NT, "tn": TN}[mode]
    if mode == "tn":
        a_spec = pl.BlockSpec((tk, tm), lambda i, j, kk: (kk, i))
    else:
        a_spec = pl.BlockSpec((tm, tk), lambda i, j, kk: (i, kk))
    if mode == "nt":
        b_spec = pl.BlockSpec((tn, tk), lambda i, j, kk: (j, kk))
    else:
        b_spec = pl.BlockSpec((tk, tn), lambda i, j, kk: (kk, j))
    mn_spec = pl.BlockSpec((tm, tn), lambda i, j, kk: (i, j))
    vec_spec = pl.BlockSpec((1, tn), lambda i, j, kk: (0, j))
    has_res = res is not None
    n_in = 2 + has_res + (1 if norm_g is not None else 0) + (3 if norm_bwd is not None else 0) + (after is not None)

    def body(*refs):
        a_ref, b_ref = refs[:2]
        extra = list(refs[2:n_in])
        outs = refs[n_in:-1]
        acc = refs[-1]
        kk = pl.program_id(2)

        @pl.when(kk == 0)
        def _():
            acc[...] = jnp.zeros_like(acc)

        if norm_bwd is not None:
            @pl.when((kk == 0) & (pl.program_id(0) == 0))
            def _():
                outs[2][...] = jnp.zeros_like(outs[2])

        acc[...] += _dot(a_ref[...], b_ref[...], dims)

        @pl.when(kk == nk - 1)
        def _():
            r = acc[...]
            if has_res:
                r = r + extra.pop(0)[...].astype(F32)
            if norm_g is not None:
                outs[0][...] = r.astype(out_dtype)
                rs = lax.rsqrt(jnp.mean(r * r, axis=-1, keepdims=True) + EPS)
                outs[1][...] = (r * rs * extra.pop(0)[...]).astype(BF16)
            elif norm_bwd is not None:
                xv, gv, dres = extra.pop(0)[...], extra.pop(0)[...], extra.pop(0)[...]
                rs = lax.rsqrt(jnp.mean(xv * xv, axis=-1, keepdims=True) + EPS)
                gdy = r * gv
                dx = dres + rs * gdy - xv * (rs * rs * rs) * jnp.mean(xv * gdy, axis=-1, keepdims=True)
                outs[0][...] = dx
                outs[1][...] = dx.astype(BF16)
                outs[2][...] += jnp.sum(r * xv * rs, axis=0, keepdims=True)
            else:
                outs[0][...] = r.astype(out_dtype)

    ins = [a, b] + ([res] if has_res else [])
    in_specs = [a_spec, b_spec] + ([mn_spec] if has_res else [])
    out_specs, out_shape = [mn_spec], [jax.ShapeDtypeStruct((m, n), out_dtype)]
    sem = ("parallel", "parallel", "arbitrary")
    if norm_g is not None:
        ins.append(norm_g.reshape(1, n))
        in_specs.append(vec_spec)
        out_specs.append(mn_spec)
        out_shape.append(jax.ShapeDtypeStruct((m, n), BF16))
    if norm_bwd is not None:
        x, g, dres = norm_bwd
        ins += [x, g.reshape(1, n), dres]
        in_specs += [mn_spec, vec_spec, mn_spec]
        out_specs += [mn_spec, vec_spec]
        out_shape += [jax.ShapeDtypeStruct((m, n), BF16), jax.ShapeDtypeStruct((1, n), F32)]
        sem = ("arbitrary", "arbitrary", "arbitrary")
    if after is not None:
        ins.append(after)
        in_specs.append(pl.BlockSpec(memory_space=pl.ANY))
    out = pl.pallas_call(
        body, name=name, grid=(m // tm, n // tn, nk), in_specs=in_specs, out_specs=out_specs,
        out_shape=out_shape, scratch_shapes=[pltpu.VMEM((tm, tn), F32)],
        compiler_params=_params(sem))(*ins)
    return out[0] if len(out) == 1 else out


def _rms_fwd(h, g, name):
    s, d = h.shape
    tm = _pick(s, (512, 256, 128))

    def body(h_ref, g_ref, o_ref):
        x = h_ref[...]
        r = lax.rsqrt(jnp.mean(x * x, axis=-1, keepdims=True) + EPS)
        o_ref[...] = (x * r * g_ref[...]).astype(BF16)

    return pl.pallas_call(
        body, name=name, grid=(s // tm,),
        in_specs=[pl.BlockSpec((tm, d), lambda i: (i, 0)), pl.BlockSpec((1, d), lambda i: (0, 0))],
        out_specs=pl.BlockSpec((tm, d), lambda i: (i, 0)),
        out_shape=jax.ShapeDtypeStruct((s, d), BF16), compiler_params=_params(("parallel",)))(h, g.reshape(1, d))


def _headnorm_fwd(x, g, name, scale=1.0, gate=None, gate_col0=0, out_dtype=BF16, width=None, head_major=False):
    if head_major:
        s, d = x.shape[1], x.shape[0] * HEAD
    else:
        s, d = x.shape[0], (width or x.shape[1])
    nh = d // HEAD
    tm = _pick(s, (256, 128))
    has_gate = gate is not None
    gb = gate_col0 // d

    def body(*refs):
        if has_gate:
            x_ref, g_ref, gt_ref, o_ref = refs
        else:
            x_ref, g_ref, o_ref = refs
        gv = g_ref[...]
        for h in range(nh):
            sl = slice(h * HEAD, (h + 1) * HEAD)
            xv = (x_ref[h] if head_major else x_ref[:, sl]).astype(F32)
            r = lax.rsqrt(jnp.mean(xv * xv, axis=-1, keepdims=True) + EPS)
            y = xv * r * gv
            if scale != 1.0:
                y = y * scale
            if has_gate:
                y = y * _silu(gt_ref[:, sl])
            o_ref[:, sl] = y.astype(out_dtype)

    row = pl.BlockSpec((tm, d), lambda i: (i, 0))
    hm = pl.BlockSpec((nh, tm, HEAD), lambda i: (0, i, 0))
    ins = [x, g.reshape(1, HEAD)]
    in_specs = [hm if head_major else row, pl.BlockSpec((1, HEAD), lambda i: (0, 0))]
    if has_gate:
        ins.append(gate)
        in_specs.append(pl.BlockSpec((tm, d), lambda i: (i, gb)))
    return pl.pallas_call(
        body, name=name, grid=(s // tm,), in_specs=in_specs, out_specs=row,
        out_shape=jax.ShapeDtypeStruct((s, d), out_dtype), compiler_params=_params(("parallel",)))(*ins)


def _headnorm_bwd(dy, x, g, name, scale=1.0, gate=None, gate_col0=0, dx_dtype=F32, head_major=False):
    s, d = dy.shape
    nh = d // HEAD
    tm = _pick(s, (256, 128))
    has_gate = gate is not None
    gb = gate_col0 // d

    def body(*refs):
        if has_gate:
            dy_ref, x_ref, g_ref, gt_ref, dx_ref, dg_ref, dgt_ref = refs
        else:
            dy_ref, x_ref, g_ref, dx_ref, dg_ref = refs

        @pl.when(pl.program_id(0) == 0)
        def _():
            dg_ref[...] = jnp.zeros_like(dg_ref)

        gv = g_ref[...]
        dg_acc = jnp.zeros((1, HEAD), F32)
        for h in range(nh):
            sl = slice(h * HEAD, (h + 1) * HEAD)
            xv = (x_ref[h] if head_major else x_ref[:, sl]).astype(F32)
            dyv = dy_ref[:, sl].astype(F32)
            r = lax.rsqrt(jnp.mean(xv * xv, axis=-1, keepdims=True) + EPS)
            if has_gate:
                gt = gt_ref[:, sl]
                act, dact = _silu_and_grad(gt)
                dgt_ref[:, sl] = (dyv * (xv * r * gv) * dact).astype(dgt_ref.dtype)
                dn = dyv * act
            else:
                dn = dyv
            if scale != 1.0:
                dn = dn * scale
            gdn = dn * gv
            mean_t = jnp.mean(xv * gdn, axis=-1, keepdims=True)
            dxv = (r * gdn - xv * (r * r * r) * mean_t).astype(dx_dtype)
            if head_major:
                dx_ref[h] = dxv
            else:
                dx_ref[:, sl] = dxv
            dg_acc = dg_acc + jnp.sum(dn * xv * r, axis=0, keepdims=True)
        dg_ref[...] += dg_acc

    row = pl.BlockSpec((tm, d), lambda i: (i, 0))
    hm = pl.BlockSpec((nh, tm, HEAD), lambda i: (0, i, 0))
    vec = pl.BlockSpec((1, HEAD), lambda i: (0, 0))
    ins = [dy, x, g.reshape(1, HEAD)]
    in_specs = [row, hm if head_major else row, vec]
    out_specs = [hm if head_major else row, vec]
    dx_shape = (nh, s, HEAD) if head_major else (s, d)
    out_shape = [jax.ShapeDtypeStruct(dx_shape, dx_dtype), jax.ShapeDtypeStruct((1, HEAD), F32)]
    if has_gate:
        ins.append(gate)
        in_specs.append(pl.BlockSpec((tm, d), lambda i: (i, gb)))
        out_specs.append(pl.BlockSpec((tm, d), lambda i: (i, gb)))
        out_shape.append(jax.ShapeDtypeStruct((s, gate.shape[1]), BF16))
    return pl.pallas_call(
        body, name=name, grid=(s // tm,), in_specs=in_specs, out_specs=out_specs, out_shape=out_shape,
        compiler_params=_params(("arbitrary",)))(*ins)


def _kv_grad(dks, dvs, kv, g, name):
    s, d = dks[0].shape
    nh = d // HEAD
    tm = _pick(s, (256, 128))
    n = len(dks)

    def body(*refs):
        dk_refs, dv_refs = refs[:n], refs[n:2 * n]
        kv_ref, g_ref, o_ref, dg_ref = refs[2 * n:]

        @pl.when(pl.program_id(0) == 0)
        def _():
            dg_ref[...] = jnp.zeros_like(dg_ref)

        gv = g_ref[...]
        dg_acc = jnp.zeros((1, HEAD), F32)
        for h in range(nh):
            sl = slice(h * HEAD, (h + 1) * HEAD)
            xv = kv_ref[:, sl]
            dyv = sum(r[:, sl] for r in dk_refs)
            r = lax.rsqrt(jnp.mean(xv * xv, axis=-1, keepdims=True) + EPS)
            gdn = dyv * gv
            mean_t = jnp.mean(xv * gdn, axis=-1, keepdims=True)
            o_ref[:, sl] = (r * gdn - xv * (r * r * r) * mean_t).astype(BF16)
            dg_acc = dg_acc + jnp.sum(dyv * xv * r, axis=0, keepdims=True)
        o_ref[:, d:] = sum(r[...] for r in dv_refs).astype(BF16)
        dg_ref[...] += dg_acc

    row = pl.BlockSpec((tm, d), lambda i: (i, 0))
    vec = pl.BlockSpec((1, HEAD), lambda i: (0, 0))
    return pl.pallas_call(
        body, name=name, grid=(s // tm,), in_specs=[row] * (2 * n) + [row, vec],
        out_specs=[pl.BlockSpec((tm, 2 * d), lambda i: (i, 0)), vec],
        out_shape=[jax.ShapeDtypeStruct((s, 2 * d), BF16), jax.ShapeDtypeStruct((1, HEAD), F32)],
        compiler_params=_params(("arbitrary",)))(*dks, *dvs, kv, g.reshape(1, HEAD))


def _swiglu_fwd(hn, wf_t, name):
    s, d = hn.shape
    f = wf_t.shape[0] // 2
    tm = _pick(s, (1024, 512, 256, 128))
    tn = _pick(f, (512, 256, 128))
    nj = f // tn

    def body(a_ref, wg_ref, wu_ref, act_ref, g_ref, u_ref):
        a = a_ref[...]
        g = _dot(a, wg_ref[...], NT)
        u = _dot(a, wu_ref[...], NT)
        act_ref[...] = (_silu(g) * u).astype(BF16)
        g_ref[...] = g.astype(BF16)
        u_ref[...] = u.astype(BF16)

    o_spec = pl.BlockSpec((tm, tn), lambda i, j: (i, j))
    sds = jax.ShapeDtypeStruct((s, f), BF16)
    return pl.pallas_call(
        body, name=name, grid=(s // tm, nj),
        in_specs=[pl.BlockSpec((tm, d), lambda i, j: (i, 0)), pl.BlockSpec((tn, d), lambda i, j: (j, 0)),
                  pl.BlockSpec((tn, d), lambda i, j: (j + nj, 0))],
        out_specs=[o_spec, o_spec, o_spec], out_shape=[sds, sds, sds],
        compiler_params=_params(("parallel", "parallel")))(hn, wf_t, wf_t)


def _swiglu_bwd(dh, w_out, g, u, name):
    s, d = dh.shape
    f = w_out.shape[0]
    tm = _pick(s, (1024, 512, 256, 128))
    tn = _pick(f, (512, 256, 128))

    def body(dh_ref, w_ref, g_ref, u_ref, dgu_ref):
        j = pl.program_id(1)
        dact = _dot(dh_ref[...], w_ref[...], NT)
        gv = g_ref[...].astype(F32)
        uv = u_ref[...].astype(F32)
        sg, dsg = _silu_and_grad(gv)
        dgu_ref[:, pl.ds(pl.multiple_of(j * tn, HEAD), tn)] = (dact * uv * dsg).astype(BF16)
        dgu_ref[:, pl.ds(pl.multiple_of(f + j * tn, HEAD), tn)] = (dact * sg).astype(BF16)

    o_spec = pl.BlockSpec((tm, tn), lambda i, j: (i, j))
    return pl.pallas_call(
        body, name=name, grid=(s // tm, f // tn),
        in_specs=[pl.BlockSpec((tm, d), lambda i, j: (i, 0)), pl.BlockSpec((tn, d), lambda i, j: (j, 0)), o_spec, o_spec],
        out_specs=pl.BlockSpec((tm, 2 * f), lambda i, j: (i, 0)), out_shape=jax.ShapeDtypeStruct((s, 2 * f), BF16),
        compiler_params=_params(("parallel", "arbitrary")))(dh, w_out, g, u)


def _ple_fwd(h, hn, p, w_gate, wp_t, name, norm_gs=()):
    s, d = h.shape
    pd = p.shape[1]
    tm = _pick(s, (512, 256, 128))
    ng = len(norm_gs)

    def body(h_ref, hn_ref, p_ref, wg_ref, wp_ref, *rest):
        g_refs, (o_ref, gp_ref, pp_ref), n_refs = rest[:ng], rest[ng:ng + 3], rest[ng + 3:]
        gpre = _dot(hn_ref[...], wg_ref[...], NN)
        pp = _dot(p_ref[...], wp_ref[...], NT)
        o = h_ref[...] + pp * jax.nn.sigmoid(gpre)
        o_ref[...] = o
        gp_ref[...] = gpre.astype(BF16)
        pp_ref[...] = pp.astype(BF16)
        if ng:
            on = o * lax.rsqrt(jnp.mean(o * o, axis=-1, keepdims=True) + EPS)
            for g_ref, n_ref in zip(g_refs, n_refs):
                n_ref[...] = (on * g_ref[...]).astype(BF16)

    row = pl.BlockSpec((tm, d), lambda i: (i, 0))
    vec = pl.BlockSpec((1, d), lambda i: (0, 0))
    bf = jax.ShapeDtypeStruct((s, d), BF16)
    return pl.pallas_call(
        body, name=name, grid=(s // tm,),
        in_specs=[row, row, pl.BlockSpec((tm, pd), lambda i: (i, 0)), pl.BlockSpec((d, d), lambda i: (0, 0)),
                  pl.BlockSpec((d, pd), lambda i: (0, 0))] + [vec] * ng,
        out_specs=[row] * (3 + ng), out_shape=[jax.ShapeDtypeStruct((s, d), F32), bf, bf] + [bf] * ng,
        compiler_params=_params(("parallel",)))(h, hn, p, w_gate, wp_t, *[g.reshape(1, d) for g in norm_gs])


def _ple_bwd(dh, gpre, pp, name):
    s, d = dh.shape
    tm = _pick(s, (512, 256, 128))

    def body(dh_ref, gp_ref, pp_ref, dgp_ref, dpp_ref):
        dv = dh_ref[...]
        sig = jax.nn.sigmoid(gp_ref[...].astype(F32))
        ppv = pp_ref[...].astype(F32)
        dpp_ref[...] = (dv * sig).astype(BF16)
        dgp_ref[...] = (dv * ppv * sig * (1.0 - sig)).astype(BF16)

    row = pl.BlockSpec((tm, d), lambda i: (i, 0))
    sds = jax.ShapeDtypeStruct((s, d), BF16)
    return pl.pallas_call(
        body, name=name, grid=(s // tm,), in_specs=[row, row, row], out_specs=[row, row], out_shape=[sds, sds],
        compiler_params=_params(("parallel",)))(dh, gpre, pp)


def _loss_fwd_bwd(y, t, name):
    s, d = y.shape
    tm = _pick(s, (512, 256, 128))

    def body(y_ref, t_ref, dy_ref, l_ref):
        @pl.when(pl.program_id(0) == 0)
        def _():
            l_ref[...] = jnp.zeros_like(l_ref)

        e = y_ref[...] - t_ref[...]
        dy_ref[...] = e * (1.0 / d)
        l_ref[...] += jnp.sum(e * e, axis=0, keepdims=True) * (0.5 / d)

    row = pl.BlockSpec((tm, d), lambda i: (i, 0))
    vec = pl.BlockSpec((1, d), lambda i: (0, 0))
    return pl.pallas_call(
        body, name=name, grid=(s // tm,), in_specs=[row, row], out_specs=[row, vec],
        out_shape=[jax.ShapeDtypeStruct((s, d), F32), jax.ShapeDtypeStruct((1, d), F32)],
        compiler_params=_params(("arbitrary",)))(y, t)


PADR = 8
CONV_ROWS = 256


def _conv_fwd(proj, w_conv, d, name):
    s = proj.shape[0]
    nh = d // HEAD
    kw = w_conv.shape[0]
    qscale = HEAD ** -0.5

    tr = _pick(s, (CONV_ROWS,))

    def body(x_ref, w_ref, o_ref, xp):
        kind = pl.program_id(0) // nh
        xp[0:PADR, :] = jnp.zeros((PADR, HEAD), F32)
        xp[PADR:, :] = x_ref[...]
        taps = [w_ref[j:j + 1, :] for j in range(kw)]
        for r0 in range(0, s, tr):
            acc = jnp.zeros((tr, HEAD), F32)
            for j in range(kw):
                acc = acc + taps[j] * xp[r0 + PADR - (kw - 1) + j:r0 + PADR - (kw - 1) + j + tr, :]
            a = _silu(acc)
            r = lax.rsqrt(jnp.sum(a * a, axis=-1, keepdims=True) + EPS)
            fac = jnp.where(kind == 0, r * qscale, jnp.where(kind == 1, r, jnp.ones_like(r)))
            o_ref[r0:r0 + tr, :] = a * fac

    blk = pl.BlockSpec((s, HEAD), lambda c: (0, c))
    hm = pl.BlockSpec((None, s, HEAD), lambda c: (c, 0, 0))
    return pl.pallas_call(
        body, name=name, grid=(3 * nh,), in_specs=[blk, pl.BlockSpec((kw, HEAD), lambda c: (0, c))], out_specs=hm,
        out_shape=jax.ShapeDtypeStruct((3 * nh, s, HEAD), F32), scratch_shapes=[pltpu.VMEM((s + PADR, HEAD), F32)],
        compiler_params=_params(("parallel",)))(proj, w_conv)


def _conv_bwd(dqkv, proj, w_conv, d, name, dproj):
    s = proj.shape[0]
    nh = d // HEAD
    kw = w_conv.shape[0]
    qscale = HEAD ** -0.5

    tr = _pick(s, (CONV_ROWS,))

    def body(dy_ref, x_ref, w_ref, _, dx_ref, dw_ref, xp, dp):
        kind = pl.program_id(0) // nh
        xp[0:PADR, :] = jnp.zeros((PADR, HEAD), F32)
        xp[PADR:, :] = x_ref[...]
        dp[s:, :] = jnp.zeros((PADR, HEAD), F32)
        taps = [w_ref[j:j + 1, :] for j in range(kw)]
        sc = jnp.where(kind == 0, qscale, 1.0)
        dws = [jnp.zeros((1, HEAD), F32) for _ in range(kw)]
        for r0 in range(0, s, tr):
            acc = jnp.zeros((tr, HEAD), F32)
            for j in range(kw):
                acc = acc + taps[j] * xp[r0 + PADR - (kw - 1) + j:r0 + PADR - (kw - 1) + j + tr, :]
            a, da_dacc = _silu_and_grad(acc)
            dy = dy_ref[r0:r0 + tr, :]
            r = lax.rsqrt(jnp.sum(a * a, axis=-1, keepdims=True) + EPS)
            dyn = dy * sc
            da_norm = r * dyn - a * (r * r * r) * jnp.sum(a * dyn, axis=-1, keepdims=True)
            dacc = jnp.where(kind == 2, dy, da_norm) * da_dacc
            dp[r0:r0 + tr, :] = dacc
            for j in range(kw):
                sh = kw - 1 - j
                dws[j] = dws[j] + jnp.sum(dacc * xp[r0 + PADR - sh:r0 + PADR - sh + tr, :], axis=0, keepdims=True)
        for j in range(kw):
            dw_ref[j:j + 1, :] = dws[j]
        for r0 in range(0, s, tr):
            dx = jnp.zeros((tr, HEAD), F32)
            for j in range(kw):
                sh = kw - 1 - j
                dx = dx + taps[j] * dp[r0 + sh:r0 + sh + tr, :]
            dx_ref[r0:r0 + tr, :] = dx.astype(BF16)

    blk = pl.BlockSpec((s, HEAD), lambda c: (0, c))
    hm = pl.BlockSpec((None, s, HEAD), lambda c: (c, 0, 0))
    wblk = pl.BlockSpec((kw, HEAD), lambda c: (0, c))
    return pl.pallas_call(
        body, name=name, grid=(3 * nh,), in_specs=[hm, blk, wblk, pl.BlockSpec(memory_space=pl.ANY)],
        out_specs=[blk, wblk], input_output_aliases={3: 0},
        out_shape=[jax.ShapeDtypeStruct(dproj.shape, BF16), jax.ShapeDtypeStruct((kw, 3 * d), F32)],
        scratch_shapes=[pltpu.VMEM((s + PADR, HEAD), F32), pltpu.VMEM((s + PADR, HEAD), F32)],
        compiler_params=_params(("parallel",)))(dqkv, proj, w_conv, dproj)


def _softplus(x):
    return jnp.maximum(x, 0.0) + jnp.log(1.0 + jnp.exp(-jnp.abs(x)))


def _gates_fwd(pab, a_log, dt_bias, nh, name):
    s = pab.shape[0]
    tm = _pick(s, (512, 256, 128))

    def body(x_ref, al_ref, dt_ref, o_ref):
        x = x_ref[...]
        lane = lax.broadcasted_iota(jnp.int32, x.shape, 1)
        g = -jnp.exp(al_ref[...]) * _softplus(x + dt_ref[...])
        o_ref[...] = jnp.where(lane < nh, g, jnp.where(lane < 2 * nh, jax.nn.sigmoid(x), 0.0))

    row = pl.BlockSpec((tm, HEAD), lambda i: (i, 0))
    vec = pl.BlockSpec((1, HEAD), lambda i: (0, 0))
    return pl.pallas_call(
        body, name=name, grid=(s // tm,), in_specs=[row, vec, vec], out_specs=row,
        out_shape=jax.ShapeDtypeStruct((s, HEAD), F32), compiler_params=_params(("parallel",)))(pab, a_log, dt_bias)


def _gates_bwd(dgb, pab, a_log, dt_bias, nh, name):
    s = pab.shape[0]
    tm = _pick(s, (512, 256, 128))

    def body(d_ref, x_ref, al_ref, dt_ref, dx_ref, dal_ref, ddt_ref):
        @pl.when(pl.program_id(0) == 0)
        def _():
            dal_ref[...] = jnp.zeros_like(dal_ref)
            ddt_ref[...] = jnp.zeros_like(ddt_ref)

        x = x_ref[...]
        dv = d_ref[...]
        lane = lax.broadcasted_iota(jnp.int32, x.shape, 1)
        ea = jnp.exp(al_ref[...])
        xs = x + dt_ref[...]
        g = -ea * _softplus(xs)
        dxs = jnp.where(lane < nh, dv * (-ea) * jax.nn.sigmoid(xs), 0.0)
        sg = jax.nn.sigmoid(x)
        dxb = jnp.where((lane >= nh) & (lane < 2 * nh), dv * sg * (1.0 - sg), 0.0)
        dx_ref[...] = (dxs + dxb).astype(BF16)
        dal_ref[...] += jnp.sum(jnp.where(lane < nh, dv * g, 0.0), axis=0, keepdims=True)
        ddt_ref[...] += jnp.sum(dxs, axis=0, keepdims=True)

    row = pl.BlockSpec((tm, HEAD), lambda i: (i, 0))
    vec = pl.BlockSpec((1, HEAD), lambda i: (0, 0))
    return pl.pallas_call(
        body, name=name, grid=(s // tm,), in_specs=[row, row, vec, vec], out_specs=[row, vec, vec],
        out_shape=[jax.ShapeDtypeStruct((s, HEAD), BF16), jax.ShapeDtypeStruct((1, HEAD), F32),
                   jax.ShapeDtypeStruct((1, HEAD), F32)],
        compiler_params=_params(("arbitrary",)))(dgb, pab, a_log, dt_bias)


def _tri_inv(a_low, eye_f):
    n = -a_low
    p = eye_f + n
    steps = int(math.log2(a_low.shape[-1])) - 1
    for _ in range(steps):
        n = _dot(n, n, BNN)
        p = p + _dot(p, n, BNN)
    return p


def _lane_col(x, lane, idx):
    return jnp.sum(jnp.where(lane == idx, x, 0.0), axis=1, keepdims=True)


def _head_cols(gbv, lo, nh):
    lane = lax.broadcasted_iota(jnp.int32, gbv.shape, 1)
    return jnp.stack([_lane_col(gbv, lane, lo + h) for h in range(nh)], axis=0)


def _gdn_chunk(q, k, v, g_col, beta_col, st):
    c = q.shape[1]
    r_i = lax.broadcasted_iota(jnp.int32, (c, c), 0)
    c_i = lax.broadcasted_iota(jnp.int32, (c, c), 1)
    incl = c_i <= r_i
    strict = c_i < r_i
    eye = c_i == r_i
    g_row = jnp.sum(jnp.where(eye, g_col, 0.0), axis=1, keepdims=True)
    gc_col = jnp.sum(jnp.where(incl, g_row, 0.0), axis=2, keepdims=True)
    gc_row = jnp.sum(jnp.where(eye, gc_col, 0.0), axis=1, keepdims=True)
    g_last = jnp.sum(g_col, axis=1, keepdims=True)
    decay = jnp.exp(jnp.where(incl, gc_col - gc_row, NEG))
    kk = _dot(k, k, BNT)
    a_low = jnp.where(strict, beta_col * kk * decay, 0.0)
    t_inv = _tri_inv(a_low, eye.astype(F32))
    e_g = jnp.exp(gc_col)
    bk = beta_col * e_g
    rhs = jnp.concatenate([v * beta_col, k * bk], axis=2)
    sol = _dot(t_inv, rhs, BNN)
    u, w = sol[:, :, :HEAD], sol[:, :, HEAD:]
    qk_raw = _dot(q, k, BNT)
    qk = qk_raw * decay
    q_dec = q * e_g
    e2 = jnp.exp(g_last - gc_col)
    k_dec = k * e2
    gl = jnp.exp(g_last)
    ws = _dot(jnp.concatenate([w, q_dec], axis=1), st, BNN)
    v_new = u - ws[:, :c]
    o = ws[:, c:] + _dot(qk, v_new, BNN)
    st_new = st * gl + _dot(k_dec, v_new, BTN)
    inter = dict(incl=incl, strict=strict, eye=eye, decay=decay, kk=kk, t_inv=t_inv, e_g=e_g, bk=bk, sol=sol, w=w,
                 qk_raw=qk_raw, qk=qk, q_dec=q_dec, e2=e2, k_dec=k_dec, gl=gl, v_new=v_new, c_i=c_i, r_i=r_i)
    return o, st_new, inter


def _gdn_fwd(qkv, gb, nh, name):
    s = qkv.shape[1]
    nc = s // CHUNK

    def body(q_ref, k_ref, v_ref, gb_ref, o_ref, st_ref, state):
        @pl.when(pl.program_id(0) == 0)
        def _():
            state[...] = jnp.zeros_like(state)

        gbv = gb_ref[...]
        st = state[...]
        st_ref[...] = st
        o, st_new, _ = _gdn_chunk(q_ref[...], k_ref[...], v_ref[...], _head_cols(gbv, 0, nh), _head_cols(gbv, nh, nh), st)
        o_ref[...] = o
        state[...] = st_new

    def qspec(part):
        return pl.BlockSpec((nh, CHUNK, HEAD), lambda n: (part, n, 0))

    return pl.pallas_call(
        body, name=name, grid=(nc,),
        in_specs=[qspec(0), qspec(1), qspec(2), pl.BlockSpec((CHUNK, HEAD), lambda n: (n, 0))],
        out_specs=[qspec(0), pl.BlockSpec((None, nh, HEAD, HEAD), lambda n: (n, 0, 0, 0))],
        out_shape=[jax.ShapeDtypeStruct((nh, s, HEAD), F32), jax.ShapeDtypeStruct((nc, nh, HEAD, HEAD), F32)],
        scratch_shapes=[pltpu.VMEM((nh, HEAD, HEAD), F32)],
        compiler_params=_params(("arbitrary",)))(qkv, qkv, qkv, gb)


def _gdn_bwd(qkv, gb, do, states, nh, name):
    s = qkv.shape[1]
    nc = s // CHUNK
    c = CHUNK

    def body(q_ref, k_ref, v_ref, gb_ref, do_ref, st_ref, dqkv_ref, dgb_ref, dstate):
        @pl.when(pl.program_id(0) == 0)
        def _():
            dstate[...] = jnp.zeros_like(dstate)

        gbv = gb_ref[...]
        lane = lax.broadcasted_iota(jnp.int32, gbv.shape, 1)
        q, k, v = q_ref[...], k_ref[...], v_ref[...]
        beta_col = _head_cols(gbv, nh, nh)
        st = st_ref[...]
        dst = dstate[...]
        dov = do_ref[...]
        _, _, it = _gdn_chunk(q, k, v, _head_cols(gbv, 0, nh), beta_col, st)
        incl, strict, eye, decay = it["incl"], it["strict"], it["eye"], it["decay"]
        dv_new = _dot(it["qk"], dov, BTN) + _dot(it["k_dec"], dst, BNN)
        d_qk = _dot(dov, it["v_new"], BNT)
        dd = _dot(jnp.concatenate([dov, -dv_new], axis=1), st, BNT)
        dq_dec, dw = dd[:, :c], dd[:, c:]
        dst_new = _dot(it["q_dec"], dov, BTN) + it["gl"] * dst - _dot(it["w"], dv_new, BTN)
        dgl = jnp.sum(jnp.sum(dst * st, axis=2, keepdims=True), axis=1, keepdims=True)
        dk_dec = _dot(it["v_new"], dst, BNT)
        dsol = jnp.concatenate([dv_new, dw], axis=2)
        drhs = _dot(it["t_inv"], dsol, BTN)
        d_a = jnp.where(strict, -_dot(drhs, it["sol"], BNT), 0.0)
        drhs_u, drhs_w = drhs[:, :, :HEAD], drhs[:, :, HEAD:]
        dvh = beta_col * drhs_u
        rw_k = jnp.sum(drhs_w * k, axis=2, keepdims=True)
        dbeta = jnp.sum(drhs_u * v, axis=2, keepdims=True) + it["e_g"] * rw_k
        dkh = it["bk"] * drhs_w
        dgc_col = it["bk"] * rw_k
        dkk = d_a * beta_col * decay
        dbeta = dbeta + jnp.sum(d_a * it["kk"] * decay, axis=2, keepdims=True)
        ddecay = d_a * beta_col * it["kk"]
        dkh = dkh + _dot(dkk, k, BNN) + _dot(dkk, k, BTN)
        dqk_raw = d_qk * decay
        ddecay = ddecay + d_qk * it["qk_raw"]
        dqh = _dot(dqk_raw, k, BNN)
        dkh = dkh + _dot(dqk_raw, q, BTN)
        ddm = jnp.where(incl, ddecay * decay, 0.0)
        dgc_col = dgc_col + jnp.sum(ddm, axis=2, keepdims=True)
        dgc_row = -jnp.sum(ddm, axis=1, keepdims=True)
        dqh = dqh + dq_dec * it["e_g"]
        dgc_col = dgc_col + jnp.sum(dq_dec * it["q_dec"], axis=2, keepdims=True)
        dkh = dkh + dk_dec * it["e2"]
        tmp = jnp.sum(dk_dec * it["k_dec"], axis=2, keepdims=True)
        dgc_col = dgc_col - tmp
        dg_last = jnp.sum(tmp, axis=1, keepdims=True) + dgl * it["gl"]
        dgc_tot_row = dgc_row + jnp.sum(jnp.where(eye, dgc_col, 0.0), axis=1, keepdims=True)
        dg_col = jnp.sum(jnp.where(it["c_i"] >= it["r_i"], dgc_tot_row, 0.0), axis=2, keepdims=True) + dg_last
        dqkv_ref[0] = dqh
        dqkv_ref[1] = dkh
        dqkv_ref[2] = dvh
        dstate[...] = dst_new
        dgb_acc = jnp.zeros(gbv.shape, F32)
        for h in range(nh):
            dgb_acc = jnp.where(lane == h, dg_col[h], jnp.where(lane == nh + h, dbeta[h], dgb_acc))
        dgb_ref[...] = dgb_acc

    def rev(part):
        return pl.BlockSpec((nh, CHUNK, HEAD), lambda n: (part, nc - 1 - n, 0))

    gspec = pl.BlockSpec((CHUNK, HEAD), lambda n: (nc - 1 - n, 0))
    dqkv, dgb = pl.pallas_call(
        body, name=name, grid=(nc,),
        in_specs=[rev(0), rev(1), rev(2), gspec, rev(0),
                  pl.BlockSpec((None, nh, HEAD, HEAD), lambda n: (nc - 1 - n, 0, 0, 0))],
        out_specs=[pl.BlockSpec((3, nh, CHUNK, HEAD), lambda n: (0, 0, nc - 1 - n, 0)), gspec],
        out_shape=[jax.ShapeDtypeStruct((3, nh, s, HEAD), F32), jax.ShapeDtypeStruct((s, HEAD), F32)],
        scratch_shapes=[pltpu.VMEM((nh, HEAD, HEAD), F32)],
        compiler_params=_params(("arbitrary",)))(qkv, qkv, qkv, gb, do, states)
    return dqkv.reshape(3 * nh, s, HEAD), dgb


SB_TQ_FWD = 1024
SB_TQ = 512


def _tri01(rel):
    j_i = lax.broadcasted_iota(jnp.int32, (2 * SBLK, SBLK), 0) & (SBLK - 1)
    s_i = lax.broadcasted_iota(jnp.int32, (2 * SBLK, SBLK), 1)
    return rel(j_i, s_i).astype(BF16)


SB_HP = 2


def _each(fn, *lists):
    return [fn(*xs) for xs in zip(*lists)]


def _sb_scores(qts, kblks, mask, csums, rhs01, one_dot):
    zs = _each(lambda qt, kb: _dot(qt, kb, NT), qts, kblks)
    es = _each(lambda z: jnp.exp(-jnp.abs(z)), zs)
    sps = _each(lambda z, e: jnp.maximum(z, 0.0) + jnp.log(1.0 + e), zs, es)
    lns = _each(lambda sp: -sp if mask is None else jnp.where(mask, -sp, 0.0), sps)
    sts = _each(lambda ln: _dot_hilo(ln, rhs01, one_dot), lns)
    wgts = _each(lambda z, sp, st, cs: jnp.exp((z - sp) + st + cs), zs, sps, sts, csums)
    if mask is not None:
        wgts = _each(lambda w: jnp.where(mask, w, 0.0), wgts)
    return zs, es, wgts, lns


def _band_mask(rows, j, row0):
    r_i = lax.broadcasted_iota(jnp.int32, (rows, SBLK), 0)
    c_i = lax.broadcasted_iota(jnp.int32, (rows, SBLK), 1)
    return (j * SBLK + c_i) < (row0 + r_i)


def _sb_fwd(q, k, v, name):
    s, d = q.shape
    nh = d // HEAD
    tq = min(SB_TQ_FWD, s)
    nb = tq // SBLK

    hp = SB_HP
    heads = [slice(h * HEAD, (h + 1) * HEAD) for h in range(hp)]

    def body(q_ref, k_ref, v_ref, o_ref, c_ref, acc, cs):
        qb = pl.program_id(1)
        lane = lax.broadcasted_iota(jnp.int32, (tq, HEAD), 1)
        after = _tri01(lambda j, t: j > t)
        acc[...] = jnp.zeros_like(acc)
        cs[...] = jnp.zeros_like(cs)
        c_ref[...] = jnp.zeros_like(c_ref)

        def process(rs, kb, mask):
            keys = pl.ds(pl.multiple_of(kb * SBLK, SBLK), SBLK)
            csums = [cs[h, rs, :] for h in range(hp)]
            _, _, wgts, lns = _sb_scores([q_ref[rs, hs] for hs in heads], [k_ref[keys, hs] for hs in heads], mask, csums, after, True)
            pvs = _each(lambda w, hs: _dot(w, v_ref[keys, hs]), wgts, heads)
            tots = _each(lambda ln: jnp.sum(ln, axis=1, keepdims=True), lns)
            for h, hs in enumerate(heads):
                acc[h, rs, :] += pvs[h]
                c_ref[rs, hs] = jnp.where(lane[rs, :] == kb, csums[h], c_ref[rs, hs])
                cs[h, rs, :] = csums[h] + tots[h]

        for j in reversed(range(nb)):
            process(slice(j * SBLK, tq), qb * nb + j, _band_mask(tq - j * SBLK, j, j * SBLK))

        def step(it, carry):
            process(slice(0, tq), qb * nb - 1 - it, None)
            return carry

        lax.fori_loop(0, qb * nb, step, 0)
        for h, hs in enumerate(heads):
            o_ref[:, hs] = acc[h].astype(BF16)

    qspec = pl.BlockSpec((tq, hp * HEAD), lambda h, i: (i, h))
    kspec = pl.BlockSpec((s, hp * HEAD), lambda h, i: (0, h))
    return pl.pallas_call(
        body, name=name, grid=(nh // hp, s // tq), in_specs=[qspec, kspec, kspec], out_specs=[qspec, qspec],
        out_shape=[jax.ShapeDtypeStruct((s, d), BF16), jax.ShapeDtypeStruct((s, d), F32)],
        scratch_shapes=[pltpu.VMEM((hp, tq, HEAD), F32), pltpu.VMEM((hp, tq, 1), F32)],
        compiler_params=_params(("parallel", "arbitrary")))(q, k, v)


def _sb_bwd(q, k, v, do, ctab, name):
    s, d = q.shape
    nh = d // HEAD
    tq = min(SB_TQ, s)
    nb = tq // SBLK

    hp = SB_HP
    heads = [slice(h * HEAD, (h + 1) * HEAD) for h in range(hp)]

    def body(q_ref, k_ref, v_ref, do_ref, c_ref, dq_ref, dk_ref, dv_ref, ps):
        qb = pl.program_id(1)

        @pl.when(qb == 0)
        def _():
            dk_ref[...] = jnp.zeros_like(dk_ref)
            dv_ref[...] = jnp.zeros_like(dv_ref)

        dq_ref[...] = jnp.zeros_like(dq_ref)
        ps[...] = jnp.zeros_like(ps)
        lane = lax.broadcasted_iota(jnp.int32, (tq, HEAD), 1)
        after = _tri01(lambda j, t: j > t)
        before = _tri01(lambda j, t: j < t)

        def process(rs, kb, mask):
            keys = pl.ds(pl.multiple_of(kb * SBLK, SBLK), SBLK)
            kblks = [k_ref[keys, hs] for hs in heads]
            qts = [q_ref[rs, hs] for hs in heads]
            dots = [do_ref[rs, hs] for hs in heads]
            csums = [_lane_col(c_ref[rs, hs], lane[rs, :], kb) for hs in heads]
            zs, es, wgts, _ = _sb_scores(qts, kblks, mask, csums, after, False)
            dlws = _each(lambda dt, hs, w: _dot(dt, v_ref[keys, hs], NT) * w, dots, heads, wgts)
            pts = _each(lambda dlw: _dot_hilo(dlw, before, False), dlws)
            pfxs = [ps[h, rs, :] for h in range(hp)]
            rs_ = _each(lambda e: 1.0 / (1.0 + e), es)
            sigs = _each(lambda z, e, r: jnp.where(z >= 0.0, r, e * r), zs, es, rs_)
            dzs = _each(lambda dlw, sig, pfx, pt: dlw * (1.0 - sig) - sig * (pfx + pt), dlws, sigs, pfxs, pts)
            tots = _each(lambda dlw: jnp.sum(dlw, axis=1, keepdims=True), dlws)
            if mask is not None:
                dzs = _each(lambda dz: jnp.where(mask, dz, 0.0), dzs)
            dqs = _each(lambda dz, kb_: _dot(dz, kb_), dzs, kblks)
            dks = _each(lambda dz, qt: _dot(dz, qt, TN), dzs, qts)
            dvs = _each(lambda w, dt: _dot(w, dt, TN), wgts, dots)
            for h, hs in enumerate(heads):
                dq_ref[rs, hs] += dqs[h]
                dk_ref[keys, hs] += dks[h]
                dv_ref[keys, hs] += dvs[h]
                ps[h, rs, :] = pfxs[h] + tots[h]

        def step(kb, carry):
            process(slice(0, tq), kb, None)
            return carry

        lax.fori_loop(0, qb * nb, step, 0)
        for j in range(nb):
            process(slice(j * SBLK, tq), qb * nb + j, _band_mask(tq - j * SBLK, j, j * SBLK))

    qspec = pl.BlockSpec((tq, hp * HEAD), lambda h, i: (i, h))
    kspec = pl.BlockSpec((s, hp * HEAD), lambda h, i: (0, h))
    sds = jax.ShapeDtypeStruct((s, d), F32)
    return pl.pallas_call(
        body, name=name, grid=(nh // hp, s // tq), in_specs=[qspec, kspec, kspec, qspec, qspec],
        out_specs=[qspec, kspec, kspec], out_shape=[sds, sds, sds],
        scratch_shapes=[pltpu.VMEM((hp, tq, 1), F32)],
        compiler_params=_params(("parallel", "arbitrary")))(q, k, v, do, ctab)


def _my_index():
    return 4 * lax.axis_index("x") + 2 * lax.axis_index("y") + lax.axis_index("c")


def _all_gather(x_shard, name):
    m_per, n = x_shard.shape

    def body(x_ref, out_ref, send_sems, recv_sems, local_sem):
        x, y, c = lax.axis_index("x"), lax.axis_index("y"), lax.axis_index("c")
        me, sibling = (x, y, c), (x, y, 1 - c)
        chips = [(1 - x, y), (x, 1 - y), (1 - x, 1 - y)]

        def rows(px, py, pc):
            return out_ref.at[pl.ds((4 * px + 2 * py + pc) * m_per, m_per), :]

        def copy(k, block, to, src=None):
            return pltpu.make_async_remote_copy(
                src_ref=rows(*block) if src is None else src, dst_ref=rows(*block),
                send_sem=send_sems.at[k], recv_sem=recv_sems.at[k], device_id=to, device_id_type=MESH)

        mine = pltpu.make_async_copy(x_ref, rows(*me), local_sem)
        mine.start()
        first = [copy(0, me, sibling, src=x_ref)]
        first += [copy(1 + j, me, (*chip, c), src=x_ref) for j, chip in enumerate(chips)]
        for cp in first:
            cp.start()
        passed = [copy(4 + j, (*chip, c), sibling) for j, chip in enumerate(chips)]
        for j, chip in enumerate(chips):
            copy(1 + j, (*chip, c), me).wait_recv()
            passed[j].start()
        copy(0, sibling, me).wait_recv()
        for j, chip in enumerate(chips):
            copy(4 + j, (*chip, 1 - c), me).wait_recv()
        for cp in first + passed:
            cp.wait_send()
        mine.wait()

    return pl.pallas_call(
        body, name=name, out_shape=jax.ShapeDtypeStruct((NDEV * m_per, n), x_shard.dtype),
        in_specs=[pl.BlockSpec(memory_space=pl.ANY)], out_specs=pl.BlockSpec(memory_space=pl.ANY),
        scratch_shapes=[pltpu.SemaphoreType.DMA((7,)), pltpu.SemaphoreType.DMA((7,)), pltpu.SemaphoreType.DMA],
    )(x_shard)


HBM_SPEC = pl.BlockSpec(memory_space=pltpu.HBM)
SEM_SPEC = pl.BlockSpec(memory_space=pltpu.SEMAPHORE)
ANY_SPEC = pl.BlockSpec(memory_space=pl.ANY)
EFFECT = pltpu.SideEffectType.DATAFLOW_SIDE_EFFECTING


def _exchange_copies(src_refs, land_refs, send_sems, recv_sems, self_sems, scatter):
    x, y, c = lax.axis_index("x"), lax.axis_index("y"), lax.axis_index("c")
    me = 4 * x + 2 * y + c
    remote, local = [], []
    for p, (src_ref, land_ref) in enumerate(zip(src_refs, land_refs)):
        rows = land_ref.shape[0] // NDEV

        def part(idx):
            return src_ref.at[pl.ds(idx * rows, rows), :] if scatter else src_ref

        slot = land_ref.at[pl.ds(me * rows, rows), :]
        for k in range(1, NDEV):
            px, py, pc = x ^ ((k >> 2) & 1), y ^ ((k >> 1) & 1), c ^ (k & 1)
            remote.append(pltpu.make_async_remote_copy(
                src_ref=part(4 * px + 2 * py + pc), dst_ref=slot, send_sem=send_sems.at[7 * p + k - 1],
                recv_sem=recv_sems.at[7 * p + k - 1], device_id=(px, py, pc), device_id_type=MESH))
        local.append(pltpu.make_async_copy(part(me), slot, self_sems.at[p]))
    return remote, local


def _send_start(srcs, scatter, after, name):
    n = len(srcs)
    lands = []
    for s in srcs:
        rows = s.shape[0] if scatter else NDEV * s.shape[0]
        lands.append(pltpu.with_memory_space_constraint(lax.empty((rows, s.shape[1]), s.dtype), pltpu.HBM))

    def body(*refs):
        src_refs, land_refs = refs[:n], refs[n:2 * n]
        send_sems, recv_sems, self_sems = refs[2 * n + 1:2 * n + 4]
        remote, local = _exchange_copies(src_refs, land_refs, send_sems, recv_sems, self_sems, scatter)
        for cp in remote + local:
            cp.start()
        refs[-1][...] = jnp.zeros_like(refs[-1])

    hbm = lambda a: pltpu.HBM(a.shape, a.dtype)
    out = pl.pallas_call(
        body, name=name,
        out_shape=(pltpu.SemaphoreType.DMA((7 * n,)), pltpu.SemaphoreType.DMA((7 * n,)), pltpu.SemaphoreType.DMA((n,)),
                   *[hbm(s) for s in srcs], *[hbm(a) for a in lands], jax.ShapeDtypeStruct((8, HEAD), F32)),
        in_specs=(HBM_SPEC,) * (2 * n) + (ANY_SPEC,),
        out_specs=(SEM_SPEC,) * 3 + (HBM_SPEC,) * (2 * n) + (pl.BlockSpec(memory_space=pltpu.VMEM),),
        input_output_aliases={i: 3 + i for i in range(2 * n)},
        compiler_params=pltpu.CompilerParams(has_side_effects=EFFECT),
    )(*[pltpu.with_memory_space_constraint(s, pltpu.HBM) for s in srcs], *lands, after)
    return dict(sems=out[:3], srcs=out[3:3 + n], lands=out[3 + n:3 + 2 * n], token=out[-1])


def _send_wait(started, scatter, after, name):
    srcs, lands = started["srcs"], started["lands"]
    n = len(srcs)

    def body(*refs):
        src_refs, land_refs = refs[:n], refs[n:2 * n]
        send_sems, recv_sems, self_sems = refs[2 * n:2 * n + 3]
        remote, local = _exchange_copies(src_refs, land_refs, send_sems, recv_sems, self_sems, scatter)
        for cp in remote:
            cp.wait_send()
            cp.wait_recv()
        for cp in local:
            cp.wait()

    hbm = lambda a: pltpu.HBM(a.shape, a.dtype)
    out = pl.pallas_call(
        body, name=name, out_shape=(*[hbm(s) for s in srcs], *[hbm(a) for a in lands]),
        in_specs=(HBM_SPEC,) * (2 * n) + (SEM_SPEC,) * 3 + (ANY_SPEC,), out_specs=(HBM_SPEC,) * (2 * n),
        input_output_aliases={i: i for i in range(2 * n)},
        compiler_params=pltpu.CompilerParams(has_side_effects=EFFECT),
    )(*srcs, *lands, *started["sems"], after)
    return out[n:]


def _sum_slots(xs, name, rows_out=None):
    _, r, c = xs[0].shape
    ro = rows_out or r
    tc = _pick(c, (128,))

    def body(*refs):
        o_ref = refs[-1]
        for l, x_ref in enumerate(refs[:-1]):
            acc = x_ref[0].astype(F32)
            for i in range(1, NDEV):
                acc = acc + x_ref[i].astype(F32)
            o_ref[l] = acc[:ro]

    return pl.pallas_call(
        body, name=name, grid=(c // tc,), in_specs=[pl.BlockSpec((NDEV, r, tc), lambda j: (0, 0, j))] * len(xs),
        out_specs=pl.BlockSpec((len(xs), ro, tc), lambda j: (0, 0, j)),
        out_shape=jax.ShapeDtypeStruct((len(xs), ro, c), F32), compiler_params=_params(("parallel",)))(*xs)


def _adamw(w, g, m, v, name):
    if w.ndim == 3:
        nl, r, c = w.shape
        tc = _pick(c, (256, 128))
        grid = (nl, c // tc)
        blk = pl.BlockSpec((None, r, tc), lambda i, j: (i, 0, j))
        sem = ("parallel", "parallel")
    else:
        r, c = w.shape
        tr = _pick(r, (256, 128, 64, 32, 16, 8))
        grid = (r // tr,)
        blk = pl.BlockSpec((tr, c), lambda i: (i, 0))
        sem = ("parallel",)
    c1 = 1.0 - B1 ** STEP
    c2 = 1.0 - B2 ** STEP

    def body(w_ref, g_ref, m_ref, v_ref, d_ref, nm_ref, nv_ref):
        gv = g_ref[...]
        nm = B1 * m_ref[...] + (1.0 - B1) * gv
        nv = B2 * v_ref[...] + (1.0 - B2) * (gv * gv)
        d_ref[...] = -LR * ((nm / c1) / (jnp.sqrt(nv / c2) + ADAM_EPS) + WD * w_ref[...])
        nm_ref[...] = nm
        nv_ref[...] = nv

    sds = jax.ShapeDtypeStruct(w.shape, F32)
    return pl.pallas_call(
        body, name=name, grid=grid, in_specs=[blk] * 4, out_specs=[blk] * 3, out_shape=[sds] * 3,
        compiler_params=_params(sem))(w, g, m, v)


def _pad_rows(a, mult):
    r = a.shape[0]
    pad = (-r) % mult
    return a if pad == 0 else jnp.pad(a, ((0, pad), (0, 0)))


def _pad_lanes(v, width=HEAD):
    return jnp.pad(v.reshape(1, -1), ((0, 0), (0, width - v.shape[-1])))


def kernel(x, p, ln_mix, ln_ffn, ln_ple, gdn_w_in, gdn_conv, gdn_a_log, gdn_dt_bias, gdn_norm, gdn_w_out, kv_norm, w_kv, k_norm, sb_w_q, sb_q_norm, sb_w_out, ffn_w_in, ffn_w_out, ple_w_proj, ple_w_gate, loss_target, m_ln_mix, m_ln_ffn, m_ln_ple, m_gdn_w_in, m_gdn_conv, m_gdn_a_log, m_gdn_dt_bias, m_gdn_norm, m_gdn_w_out, m_kv_norm, m_w_kv, m_k_norm, m_sb_w_q, m_sb_q_norm, m_sb_w_out, m_ffn_w_in, m_ffn_w_out, m_ple_w_proj, m_ple_w_gate, v_ln_mix, v_ln_ffn, v_ln_ple, v_gdn_w_in, v_gdn_conv, v_gdn_a_log, v_gdn_dt_bias, v_gdn_norm, v_gdn_w_out, v_kv_norm, v_w_kv, v_k_norm, v_sb_w_q, v_sb_q_norm, v_sb_w_out, v_ffn_w_in, v_ffn_w_out, v_ple_w_proj, v_ple_w_gate):
    s, d = x.shape[1], x.shape[2]
    nh = d // HEAD
    depth = ln_mix.shape[0]
    n_a = gdn_w_in.shape[0]
    n_b = sb_w_q.shape[0]
    me = _my_index()
    win_cols = gdn_w_in.shape[2]
    win_rows = 4 * d + 2 * nh

    def col_t(w):
        return jnp.transpose(w).astype(BF16)

    local = {}
    for l in range(n_a):
        local[("gdn_w_in", l)] = col_t(gdn_w_in[l])
        local[("gdn_w_out", l)] = gdn_w_out[l].astype(BF16)
    local[("w_kv", 0)] = col_t(w_kv)
    for j in range(n_b):
        local[("sb_w_q", j)] = sb_w_q[j].astype(BF16)
        local[("sb_w_out", j)] = sb_w_out[j].astype(BF16)
    for l in range(depth):
        local[("ffn_w_in", l)] = col_t(ffn_w_in[l])
        local[("ffn_w_out", l)] = ffn_w_out[l].astype(BF16)
        local[("ple_w_proj", l)] = col_t(ple_w_proj[l]).reshape(-1, d)
        local[("ple_w_gate", l)] = ple_w_gate[l].astype(BF16)
    local = {key: _pad_rows(a, 16) for key, a in local.items()}

    chunks = []
    for l in range(depth):
        mix = [("gdn_w_in", l), ("gdn_w_out", l)] if l < n_a else [("sb_w_q", l - n_a), ("sb_w_out", l - n_a)]
        rest = [("ffn_w_in", l), ("ffn_w_out", l), ("ple_w_proj", l), ("ple_w_gate", l)]
        if l == n_a - 1:
            rest.append(("w_kv", 0))
        chunks += [(f"a{l}", mix), (f"f{l}", rest)]
    chunk_keys = dict(chunks)

    conv_rows = n_a * gdn_conv.shape[1]
    conv_sh = _pad_rows(gdn_conv.reshape(conv_rows, -1), 8)
    conv_g = _all_gather(conv_sh, "comm_gather_conv")
    token = conv_g
    conv_g = conv_g.reshape(NDEV, conv_sh.shape[0], -1)
    conv_full = jnp.transpose(conv_g[:, :conv_rows, :], (1, 0, 2)).reshape(n_a, gdn_conv.shape[1], 3 * d)

    w_started = {}
    for name, keys in chunks:
        w_started[name] = _send_start([local[k] for k in keys], False, token, f"comm_wstart_{name}")
        token = w_started[name]["token"]

    full = {}

    def fetch(name, after):
        lands = _send_wait(w_started[name], False, after, f"comm_wwait_{name}")
        for key, land in zip(chunk_keys[name], lands):
            full[key] = land

    def whole(key, valid=None):
        a = full[key]
        if valid is not None:
            a = a.reshape(NDEV, -1, d)[:, :valid, :].reshape(-1, d)
        return a

    pd = p.shape[-1]
    w_in_t, w_ab_t, w_gout, w_q, w_sout, wf_t, w_fout, wp_t, w_pg = {}, {}, {}, {}, {}, {}, {}, {}, {}
    wkv_t = None

    h = x[0]
    sv = []
    kv_sv = None
    k_sh = v_sh = None
    for l in range(depth):
        t = {}
        t["h0"] = h
        if l == 0:
            hn = _rms_fwd(h, ln_mix[l], f"rms_mix_{l}")
        t["hn"] = hn
        fetch(f"a{l}", token if l == 0 else hn)
        if l < n_a:
            wt = whole(("gdn_w_in", l), win_cols)
            w_in_t[l] = wt[:4 * d]
            w_ab_t[l] = jnp.pad(wt[4 * d:], ((0, HEAD - 2 * nh), (0, 0)))
            w_gout[l] = whole(("gdn_w_out", l))
        else:
            w_q[l - n_a] = whole(("sb_w_q", l - n_a))
            w_sout[l - n_a] = whole(("sb_w_out", l - n_a))
        if l < n_a:
            proj = _mm(hn, w_in_t[l], "nt", f"gdn_proj_{l}")
            pab = _mm(hn, w_ab_t[l], "nt", f"gdn_proj_ab_{l}")
            qkv = _conv_fwd(proj, conv_full[l], d, f"gdn_conv_{l}")
            al, dtb = _pad_lanes(gdn_a_log[l]), _pad_lanes(gdn_dt_bias[l])
            gb = _gates_fwd(pab, al, dtb, nh, f"gdn_gates_{l}")
            o_raw, states = _gdn_fwd(qkv, gb, nh, f"gdn_rule_{l}")
            o2 = _headnorm_fwd(o_raw, gdn_norm[l], f"gdn_outnorm_{l}", gate=proj, gate_col0=3 * d, head_major=True)
            h, hn2 = _mm(o2, w_gout[l], "nn", f"gdn_out_{l}", res=h, norm_g=ln_ffn[l])
            t.update(proj=proj, pab=pab, qkv=qkv, gb=gb, o_raw=o_raw, states=states, o2=o2, al=al, dtb=dtb)
        else:
            j = l - n_a
            qpre = _mm(hn, w_q[j], "nn", f"sb_qproj_{j}")
            qn = _headnorm_fwd(qpre, sb_q_norm[j], f"sb_qnorm_{j}", scale=HEAD ** -0.5)
            o, ctab = _sb_fwd(qn, k_sh, v_sh, f"sb_attn_{j}")
            h, hn2 = _mm(o, w_sout[j], "nn", f"sb_out_{j}", res=h, norm_g=ln_ffn[l])
            t.update(qpre=qpre, qn=qn, o=o, ctab=ctab)
        t["h1"] = h
        fetch(f"f{l}", hn2)
        wf_t[l] = whole(("ffn_w_in", l))
        w_fout[l] = whole(("ffn_w_out", l))
        wp_t[l] = full[("ple_w_proj", l)].reshape(d, pd)
        w_pg[l] = whole(("ple_w_gate", l))
        if l == n_a - 1:
            wkv_t = whole(("w_kv", 0))
        act, gs, us = _swiglu_fwd(hn2, wf_t[l], f"ffn_in_{l}")
        h, hn3 = _mm(act, w_fout[l], "nn", f"ffn_out_{l}", res=h, norm_g=ln_ple[l])
        t.update(hn2=hn2, act=act, gs=gs, us=us, h2=h)
        gains = ([ln_mix[l + 1]] if l + 1 < depth else []) + ([kv_norm] if l == n_a - 1 else [])
        h, gpre, pp, *normed = _ple_fwd(h, hn3, p[l, 0], w_pg[l], wp_t[l], f"ple_{l}", norm_gs=gains)
        if l + 1 < depth:
            hn = normed[0]
        t.update(hn3=hn3, gpre=gpre, pp=pp)
        sv.append(t)
        if l == n_a - 1:
            kvn = normed[-1]
            kv = _mm(kvn, wkv_t, "nt", "kv_proj")
            k_sh = _headnorm_fwd(kv, k_norm, "k_norm", width=d)
            v_sh = kv[:, d:].astype(BF16)
            kv_sv = dict(h=h, kvn=kvn, kv=kv)

    dh, loss_vec = _loss_fwd_bwd(h, loss_target[0], "loss")
    loss = lax.psum(jnp.sum(loss_vec), ("x", "y", "c"))

    gw = {}
    small = {}
    g_started = {}

    def scatter_start(name):
        gparts = []
        for key in chunk_keys[name]:
            g = gw[key]
            g = g.reshape(NDEV, -1, d) if key[0] == "ple_w_proj" else g.reshape(NDEV, -1, g.shape[-1])
            padr = local[key].shape[0] - g.shape[1]
            if padr:
                g = jnp.pad(g, ((0, 0), (0, padr), (0, 0)))
            gparts.append(g.reshape(-1, d))
        g_started[name] = _send_start(gparts, True, jnp.zeros((8, HEAD), F32), f"comm_gstart_{name}")
        return g_started[name]["token"]

    dks, dvs = [], []
    for l in reversed(range(depth)):
        t = sv[l]
        if l == n_a - 1:
            dkv, dkn = _kv_grad(dks, dvs, kv_sv["kv"], k_norm, "k_norm_bwd")
            gw[("w_kv", 0)] = _mm(dkv, kv_sv["kvn"], "tn", "kv_dw", out_dtype=BF16)
            dh, _, dg = _mm(dkv, wkv_t, "nn", "kv_dx", norm_bwd=(kv_sv["h"], kv_norm, dh))
            small["kv_norm"] = dg
            small["k_norm"] = dkn
        dgp, dpp = _ple_bwd(dh, t["gpre"], t["pp"], f"ple_bwd_{l}")
        gw[("ple_w_gate", l)] = _mm(t["hn3"], dgp, "tn", f"ple_dwg_{l}", out_dtype=BF16)
        gw[("ple_w_proj", l)] = _mm(dpp, p[l, 0], "tn", f"ple_dwp_{l}", out_dtype=BF16)
        dh, dhb, dg = _mm(dgp, w_pg[l], "nt", f"ple_dx_{l}", norm_bwd=(t["h2"], ln_ple[l], dh))
        small[("ln_ple", l)] = dg
        dgu = _swiglu_bwd(dhb, w_fout[l], t["gs"], t["us"], f"ffn_bwd_act_{l}")
        gw[("ffn_w_out", l)] = _mm(t["act"], dhb, "tn", f"ffn_dwo_{l}", out_dtype=BF16)
        gw[("ffn_w_in", l)] = _mm(dgu, t["hn2"], "tn", f"ffn_dwi_{l}", out_dtype=BF16)
        dh, dhb, dg = _mm(dgu, wf_t[l], "nn", f"ffn_dx_{l}", norm_bwd=(t["h1"], ln_ffn[l], dh),
                          after=scatter_start(f"f{l}"))
        small[("ln_ffn", l)] = dg
        if l < n_a:
            do2 = _mm(dhb, w_gout[l], "nt", f"gdn_out_dx_{l}")
            gw[("gdn_w_out", l)] = _mm(t["o2"], dhb, "tn", f"gdn_out_dw_{l}", out_dtype=BF16)
            do_raw, dgn, dgate = _headnorm_bwd(do2, t["o_raw"], gdn_norm[l], f"gdn_outnorm_bwd_{l}",
                                               gate=t["proj"], gate_col0=3 * d, head_major=True)
            small[("gdn_norm", l)] = dgn
            dqkv, dgb = _gdn_bwd(t["qkv"], t["gb"], do_raw, t["states"], nh, f"gdn_rule_bwd_{l}")
            dpab, dal, ddt = _gates_bwd(dgb, t["pab"], t["al"], t["dtb"], nh, f"gdn_gates_bwd_{l}")
            small[("gdn_a_log", l)] = dal
            small[("gdn_dt_bias", l)] = ddt
            dproj, dconv = _conv_bwd(dqkv, t["proj"], conv_full[l], d, f"gdn_conv_bwd_{l}", dgate)
            small[("gdn_conv", l)] = dconv
            dw_main = _mm(dproj, t["hn"], "tn", f"gdn_proj_dw_{l}", out_dtype=BF16)
            dw_ab = _mm(dpab, t["hn"], "tn", f"gdn_proj_ab_dw_{l}", out_dtype=BF16)
            gw[("gdn_w_in", l)] = jnp.concatenate([dw_main, dw_ab[:16]], axis=0)[:win_rows]
            dhn_ab = _mm(dpab, w_ab_t[l], "nn", f"gdn_proj_ab_dx_{l}")
            last = dict(a=dproj, b=w_in_t[l], mode="nn", name=f"gdn_proj_dx_{l}", res=dhn_ab)
        else:
            j = l - n_a
            do = _mm(dhb, w_sout[j], "nt", f"sb_out_dx_{j}", out_dtype=BF16)
            gw[("sb_w_out", j)] = _mm(t["o"], dhb, "tn", f"sb_out_dw_{j}", out_dtype=BF16)
            dq, dk, dv = _sb_bwd(t["qn"], k_sh, v_sh, do, t["ctab"], f"sb_attn_bwd_{j}")
            dks.append(dk)
            dvs.append(dv)
            dqpre, dqn = _headnorm_bwd(dq, t["qpre"], sb_q_norm[j], f"sb_qnorm_bwd_{j}", scale=HEAD ** -0.5, dx_dtype=BF16)
            small[("sb_q_norm", j)] = dqn
            gw[("sb_w_q", j)] = _mm(t["hn"], dqpre, "tn", f"sb_q_dw_{j}", out_dtype=BF16)
            last = dict(a=dqpre, b=w_q[j], mode="nt", name=f"sb_q_dx_{j}")
        dh, _, dg = _mm(**last, norm_bwd=(t["h0"], ln_mix[l], dh), after=scatter_start(f"a{l}"))
        small[("ln_mix", l)] = dg
    grad_x = dh[None]

    def vec_rows(v):
        return v.reshape(-1, HEAD)

    small_items = []
    for name_, cnt in (("ln_mix", depth), ("ln_ffn", depth), ("ln_ple", depth)):
        for l in range(cnt):
            small_items.append(((name_, l), vec_rows(small[(name_, l)])))
    for l in range(n_a):
        small_items.append((("gdn_conv", l), small[("gdn_conv", l)].reshape(-1, HEAD)))
        small_items.append((("gdn_a_log", l), small[("gdn_a_log", l)]))
        small_items.append((("gdn_dt_bias", l), small[("gdn_dt_bias", l)]))
        small_items.append((("gdn_norm", l), small[("gdn_norm", l)]))
    small_items.append(("kv_norm", vec_rows(small["kv_norm"])))
    small_items.append(("k_norm", small["k_norm"]))
    for j in range(n_b):
        small_items.append((("sb_q_norm", j), small[("sb_q_norm", j)]))
    spack = jnp.concatenate([_pad_rows(a, 8) for _, a in small_items], axis=0)
    s_started = _send_start([spack], False, jnp.zeros((8, HEAD), F32), "comm_sstart")

    landed = {}
    for name, keys in reversed(chunks):
        lands = _send_wait(g_started[name], True, dh, f"comm_gwait_{name}")
        for key, land in zip(keys, lands):
            landed[key] = land.reshape(NDEV, -1, d)

    def summed(wname, count, rows_out=None):
        return _sum_slots([landed[(wname, i)] for i in range(count)], f"grad_sum_{wname}", rows_out)

    gt_gdn_w_in = summed("gdn_w_in", n_a, win_cols)
    gt_ffn_w_in = summed("ffn_w_in", depth)
    g_gdn_w_in = jnp.transpose(gt_gdn_w_in, (0, 2, 1))
    g_gdn_w_out = summed("gdn_w_out", n_a)
    g_w_kv = jnp.transpose(summed("w_kv", 1)[0])
    g_sb_w_q = summed("sb_w_q", n_b)
    g_sb_w_out = summed("sb_w_out", n_b)
    g_ffn_w_in = jnp.transpose(gt_ffn_w_in, (0, 2, 1))
    g_ffn_w_out = summed("ffn_w_out", depth)
    g_ple_w_proj = jnp.transpose(summed("ple_w_proj", depth).reshape(depth, -1, pd), (0, 2, 1))
    g_ple_w_gate = summed("ple_w_gate", depth)

    sg = _send_wait(s_started, False, g_ple_w_gate, "comm_swait")[0].reshape(NDEV, spack.shape[0], HEAD)
    ssum = _sum_slots([sg], "small_sum")[0]
    sm = {}
    off = 0
    for key, a in small_items:
        sm[key] = ssum[off:off + a.shape[0]]
        off += a.shape[0] + (-a.shape[0]) % 8

    g_ln_mix = jnp.stack([sm[("ln_mix", l)].reshape(d) for l in range(depth)])
    g_ln_ffn = jnp.stack([sm[("ln_ffn", l)].reshape(d) for l in range(depth)])
    g_ln_ple = jnp.stack([sm[("ln_ple", l)].reshape(d) for l in range(depth)])
    conv_loc = gdn_conv.shape[2]
    g_conv_full = jnp.stack([sm[("gdn_conv", l)].reshape(gdn_conv.shape[1], 3 * d) for l in range(n_a)])
    g_gdn_conv = lax.dynamic_slice_in_dim(g_conv_full, me * conv_loc, conv_loc, axis=2)
    g_a_log = jnp.stack([sm[("gdn_a_log", l)][0, :nh] for l in range(n_a)])
    g_dt_bias = jnp.stack([sm[("gdn_dt_bias", l)][0, :nh] for l in range(n_a)])
    g_gdn_norm = jnp.stack([sm[("gdn_norm", l)][0] for l in range(n_a)])
    g_kv_norm = sm["kv_norm"].reshape(d)
    g_k_norm = sm["k_norm"][0]
    g_sb_q_norm = jnp.stack([sm[("sb_q_norm", j)][0] for j in range(n_b)])

    grads = [g_ln_mix, g_ln_ffn, g_ln_ple, g_gdn_w_in, g_gdn_conv, g_a_log, g_dt_bias, g_gdn_norm, g_gdn_w_out,
             g_kv_norm, g_w_kv, g_k_norm, g_sb_w_q, g_sb_q_norm, g_sb_w_out, g_ffn_w_in, g_ffn_w_out, g_ple_w_proj,
             g_ple_w_gate]
    weights = [ln_mix, ln_ffn, ln_ple, gdn_w_in, gdn_conv, gdn_a_log, gdn_dt_bias, gdn_norm, gdn_w_out, kv_norm, w_kv,
               k_norm, sb_w_q, sb_q_norm, sb_w_out, ffn_w_in, ffn_w_out, ple_w_proj, ple_w_gate]
    moms = [m_ln_mix, m_ln_ffn, m_ln_ple, m_gdn_w_in, m_gdn_conv, m_gdn_a_log, m_gdn_dt_bias, m_gdn_norm, m_gdn_w_out,
            m_kv_norm, m_w_kv, m_k_norm, m_sb_w_q, m_sb_q_norm, m_sb_w_out, m_ffn_w_in, m_ffn_w_out, m_ple_w_proj,
            m_ple_w_gate]
    vels = [v_ln_mix, v_ln_ffn, v_ln_ple, v_gdn_w_in, v_gdn_conv, v_gdn_a_log, v_gdn_dt_bias, v_gdn_norm, v_gdn_w_out,
            v_kv_norm, v_w_kv, v_k_norm, v_sb_w_q, v_sb_q_norm, v_sb_w_out, v_ffn_w_in, v_ffn_w_out, v_ple_w_proj,
            v_ple_w_gate]

    deltas, new_m, new_v = [], [], []
    small_idx = [i for i, w in enumerate(weights) if w.size < 8 * HEAD * 16]
    transposed = {3: gt_gdn_w_in, 15: gt_ffn_w_in}
    for i, (w, g, m, v) in enumerate(zip(weights, grads, moms, vels)):
        if i in small_idx:
            deltas.append(None), new_m.append(None), new_v.append(None)
            continue
        if i in transposed:
            tr = lambda a: jnp.transpose(a, (0, 2, 1))
            dl, nm, nv = _adamw(tr(w), transposed[i], tr(m), tr(v), f"adamw_{i}")
            deltas.append(tr(dl)), new_m.append(tr(nm)), new_v.append(tr(nv))
            continue
        shp = w.shape
        two = lambda a: a.reshape(-1, shp[-1])
        dl, nm, nv = _adamw(two(w), two(g), two(m), two(v), f"adamw_{i}")
        deltas.append(dl.reshape(shp)), new_m.append(nm.reshape(shp)), new_v.append(nv.reshape(shp))

    def flat_pack(arrs):
        flat = jnp.concatenate([a.reshape(-1) for a in arrs])
        pad = (-flat.shape[0]) % (8 * HEAD)
        return jnp.pad(flat, (0, pad)).reshape(-1, HEAD)

    sw = flat_pack([weights[i] for i in small_idx])
    sgr = flat_pack([grads[i] for i in small_idx])
    smo = flat_pack([moms[i] for i in small_idx])
    sve = flat_pack([vels[i] for i in small_idx])
    sdl, snm, snv = _adamw(sw, sgr, smo, sve, "adamw_small")
    off = 0
    for i in small_idx:
        n = weights[i].size
        shp = weights[i].shape
        deltas[i] = sdl.reshape(-1)[off:off + n].reshape(shp)
        new_m[i] = snm.reshape(-1)[off:off + n].reshape(shp)
        new_v[i] = snv.reshape(-1)[off:off + n].reshape(shp)
        off += n

    return (loss, grad_x, *grads, *deltas, *new_m, *new_v)
```
